```python
import math
import jax, jax.numpy as jnp
from jax import lax
import numpy as np

D_MODEL = 1024
BATCH = 8
SEQ = 8192
DEPTH = 2

D_MIX = D_MODEL
HEAD_DIM = 64
N_HEADS = 8
N_KV_HEADS = 2
Q_PER_KV = N_HEADS // N_KV_HEADS
D_ATTN = N_HEADS * HEAD_DIM
D_KV = N_KV_HEADS * HEAD_DIM
WINDOW = 128
BLOCK = 128
N_GM_GROUPS = 8
GM_GROUP_DIM = 64
D_GM = N_GM_GROUPS * GM_GROUP_DIM
CHUNK = 128
D_IN = D_ATTN + 2 * D_KV + D_ATTN + 3 * D_GM
EPS = 1e-6
NEG_INF = -1e30

kernel_name = "hymba_style_bidir_attn_gmlp_hybrid"


def _rms(x, eps=EPS):
    xf = x.astype(jnp.float32)
    return (xf * lax.rsqrt(jnp.mean(xf * xf, axis=-1, keepdims=True) + eps)).astype(x.dtype)


def _alibi_slopes(n_heads):
    return 2.0 ** (-8.0 * jnp.arange(1, n_heads + 1, dtype=jnp.float32) / n_heads)


def _windowed_gqa(q, k, v, q_gain, k_gain, sink):
    b, s = q.shape[0], q.shape[1]
    nb = s // BLOCK
    q = _rms(q) * q_gain
    k = _rms(k) * k_gain
    qb = q.reshape(b, nb, BLOCK, N_KV_HEADS, Q_PER_KV, HEAD_DIM)
    pad = ((0, 0), (BLOCK, BLOCK), (0, 0), (0, 0))
    kp = jnp.pad(k, pad).reshape(b, nb + 2, BLOCK, N_KV_HEADS, HEAD_DIM)
    vp = jnp.pad(v, pad).reshape(b, nb + 2, BLOCK, N_KV_HEADS, HEAD_DIM)
    kb = jnp.concatenate([kp[:, :-2], kp[:, 1:-1], kp[:, 2:]], axis=2)
    vb = jnp.concatenate([vp[:, :-2], vp[:, 1:-1], vp[:, 2:]], axis=2)
    scores = jnp.einsum('bnqkgd,bnskd->bnkgqs', qb, kb).astype(jnp.float32) / math.sqrt(HEAD_DIM)
    blk = jnp.arange(nb)[:, None, None]
    qpos = blk * BLOCK + jnp.arange(BLOCK)[None, :, None]
    kpos = blk * BLOCK - BLOCK + jnp.arange(3 * BLOCK)[None, None, :]
    dist = jnp.abs(kpos - qpos).astype(jnp.float32)
    valid = (dist <= WINDOW) & (kpos >= 0) & (kpos < s)
    slopes = _alibi_slopes(N_HEADS).reshape(N_KV_HEADS, Q_PER_KV)
    scores = scores - slopes[None, None, :, :, None, None] * dist[None, :, None, None]
    scores = jnp.where(valid[None, :, None, None], scores, NEG_INF)
    sink_col = jnp.broadcast_to(
        sink.astype(jnp.float32).reshape(N_KV_HEADS, Q_PER_KV)[None, None, :, :, None, None],
        scores.shape[:-1] + (1,))
    probs = jax.nn.softmax(jnp.concatenate([scores, sink_col], axis=-1), axis=-1)[..., :-1]
    out = jnp.einsum('bnkgqs,bnskd->bnqkgd', probs.astype(v.dtype), vb)
    return out.reshape(b, s, D_ATTN)


def _chunked_gmlp(u, vg, w_s, b_s):
    b, s = u.shape[0], u.shape[1]
    nc = s // CHUNK
    vn = _rms(vg.reshape(b, nc, CHUNK, N_GM_GROUPS, GM_GROUP_DIM))
    sv = jnp.einsum('gts,bcsge->bctge', w_s, vn) + b_s.T[None, None, :, :, None]
    return u * sv.reshape(b, s, D_GM)


def _fwd_setup_inputs(seed: int = 0) -> dict:
    key = jax.random.key(seed)
    ks = jax.random.split(key, 12)
    f32 = jnp.float32
    x = jax.random.normal(ks[0], (BATCH, SEQ, D_MODEL), f32)
    c = jax.random.normal(ks[1], (BATCH, D_MODEL), f32)
    w_ada = jax.random.normal(ks[2], (DEPTH, D_MODEL, 3 * D_MODEL), f32) * D_MODEL ** -0.5
    b_ada = jax.random.normal(ks[3], (DEPTH, 3 * D_MODEL), f32) * 0.02
    norm_gain = 1.0 + 0.01 * jax.random.normal(ks[4], (DEPTH, D_MODEL), f32)
    w_in = jax.random.normal(ks[5], (DEPTH, D_MODEL, D_IN), f32) * D_MODEL ** -0.5
    q_gain = 1.0 + 0.01 * jax.random.normal(ks[6], (DEPTH, HEAD_DIM), f32)
    k_gain = 1.0 + 0.01 * jax.random.normal(ks[7], (DEPTH, HEAD_DIM), f32)
    sink = jax.random.normal(ks[8], (DEPTH, N_HEADS), f32) * 0.5
    w_s = jax.random.normal(ks[9], (DEPTH, N_GM_GROUPS, CHUNK, CHUNK), f32) * (0.5 * CHUNK ** -0.5)
    b_s = 1.0 + 0.01 * jax.random.normal(ks[10], (DEPTH, N_GM_GROUPS, CHUNK), f32)
    w_out = jax.random.normal(ks[11], (DEPTH, D_MIX, D_MODEL), f32) * D_MIX ** -0.5
    return {"x": x, "c": c, "w_ada": w_ada, "b_ada": b_ada, "norm_gain": norm_gain,
            "w_in": w_in, "q_gain": q_gain, "k_gain": k_gain, "sink": sink,
            "w_s": w_s, "b_s": b_s, "w_out": w_out}


def _fwd_reference(x, c, w_ada, b_ada, norm_gain, w_in, q_gain, k_gain, sink, w_s, b_s, w_out):
    b, s, _ = x.shape
    cond = jax.nn.silu(c)
    splits = np.cumsum([D_ATTN, D_KV, D_KV, D_ATTN, D_GM, D_GM])
    for l in range(DEPTH):
        ada = cond @ w_ada[l] + b_ada[l]
        shift, scale, gate = jnp.split(ada, 3, axis=-1)
        h = _rms(x) * norm_gain[l]
        h = h * (1.0 + scale[:, None, :]) + shift[:, None, :]
        proj = h @ w_in[l]
        q, k, v, g_attn, u, v_gm, g_gm = jnp.split(proj, splits, axis=-1)
        attn = _windowed_gqa(q.reshape(b, s, N_HEADS, HEAD_DIM),
                             k.reshape(b, s, N_KV_HEADS, HEAD_DIM),
                             v.reshape(b, s, N_KV_HEADS, HEAD_DIM),
                             q_gain[l], k_gain[l], sink[l])
        gm = _chunked_gmlp(u, v_gm, w_s[l], b_s[l])
        y = jnp.concatenate([attn * jax.nn.silu(g_attn), gm * jax.nn.silu(g_gm)], axis=-1)
        x = x + gate[:, None, :] * (y @ w_out[l])
    return x


import jax as _jax
import jax.numpy as _jnp

TWIN_FORMAT = 'train_step'
FWD_PARAMS = ['x', 'c', 'w_ada', 'b_ada', 'norm_gain', 'w_in', 'q_gain', 'k_gain', 'sink', 'w_s', 'b_s', 'w_out']
TWIN_WEIGHTS = ['w_ada', 'b_ada', 'norm_gain', 'w_in', 'q_gain', 'k_gain', 'sink', 'w_s', 'b_s', 'w_out']
TWIN_DIFF_INPUT = 'x'
TWIN_INPUTS = ['x', 'c', 'w_ada', 'b_ada', 'norm_gain', 'w_in', 'q_gain', 'k_gain', 'sink', 'w_s', 'b_s', 'w_out', 'loss_target', 'm_w_ada', 'm_b_ada', 'm_norm_gain', 'm_w_in', 'm_q_gain', 'm_k_gain', 'm_sink', 'm_w_s', 'm_b_s', 'm_w_out', 'v_w_ada', 'v_b_ada', 'v_norm_gain', 'v_w_in', 'v_q_gain', 'v_k_gain', 'v_sink', 'v_w_s', 'v_b_s', 'v_w_out']
TWIN_OUTPUTS = ['loss', 'grad_x', 'grad_w_ada', 'grad_b_ada', 'grad_norm_gain', 'grad_w_in', 'grad_q_gain', 'grad_k_gain', 'grad_sink', 'grad_w_s', 'grad_b_s', 'grad_w_out', 'delta_w_ada', 'delta_b_ada', 'delta_norm_gain', 'delta_w_in', 'delta_q_gain', 'delta_k_gain', 'delta_sink', 'delta_w_s', 'delta_b_s', 'delta_w_out', 'new_m_w_ada', 'new_m_b_ada', 'new_m_norm_gain', 'new_m_w_in', 'new_m_q_gain', 'new_m_k_gain', 'new_m_sink', 'new_m_w_s', 'new_m_b_s', 'new_m_w_out', 'new_v_w_ada', 'new_v_b_ada', 'new_v_norm_gain', 'new_v_w_in', 'new_v_q_gain', 'new_v_k_gain', 'new_v_sink', 'new_v_w_s', 'new_v_b_s', 'new_v_w_out']
TWIN_LEAF_KINDS = {'loss': 'loss', 'grad_x': 'grad_x', 'grad_w_ada': 'grad_w', 'grad_b_ada': 'grad_w', 'grad_norm_gain': 'grad_w', 'grad_w_in': 'grad_w', 'grad_q_gain': 'grad_w', 'grad_k_gain': 'grad_w', 'grad_sink': 'grad_w', 'grad_w_s': 'grad_w', 'grad_b_s': 'grad_w', 'grad_w_out': 'grad_w', 'delta_w_ada': 'delta_w', 'delta_b_ada': 'delta_w', 'delta_norm_gain': 'delta_w', 'delta_w_in': 'delta_w', 'delta_q_gain': 'delta_w', 'delta_k_gain': 'delta_w', 'delta_sink': 'delta_w', 'delta_w_s': 'delta_w', 'delta_b_s': 'delta_w', 'delta_w_out': 'delta_w', 'new_m_w_ada': 'new_m', 'new_m_b_ada': 'new_m', 'new_m_norm_gain': 'new_m', 'new_m_w_in': 'new_m', 'new_m_q_gain': 'new_m', 'new_m_k_gain': 'new_m', 'new_m_sink': 'new_m', 'new_m_w_s': 'new_m', 'new_m_b_s': 'new_m', 'new_m_w_out': 'new_m', 'new_v_w_ada': 'new_v', 'new_v_b_ada': 'new_v', 'new_v_norm_gain': 'new_v', 'new_v_w_in': 'new_v', 'new_v_q_gain': 'new_v', 'new_v_k_gain': 'new_v', 'new_v_sink': 'new_v', 'new_v_w_s': 'new_v', 'new_v_b_s': 'new_v', 'new_v_w_out': 'new_v'}


def _forward(args):
    return _fwd_reference(*[args[k] for k in FWD_PARAMS])


def _output_shape():
    def fwd():
        inp = _fwd_setup_inputs(0)
        return _fwd_reference(*[inp[k] for k in FWD_PARAMS])
    out = _jax.eval_shape(fwd)
    return out.shape, out.dtype

N_MICROBATCH = 1
ADAM_LR = 0.001
ADAM_B1 = 0.9
ADAM_B2 = 0.999
ADAM_EPS = 1e-08
ADAM_WD = 0.01
ADAM_STEP = 10
PER_EXAMPLE_BATCH_AXIS = {'x': 0, 'c': 0, 'loss_target': 0}
SHARED_INPUTS = []
_WEIGHT_DTYPES = {'w_ada': _jnp.float32, 'b_ada': _jnp.float32, 'norm_gain': _jnp.float32, 'w_in': _jnp.float32, 'q_gain': _jnp.float32, 'k_gain': _jnp.float32, 'sink': _jnp.float32, 'w_s': _jnp.float32, 'b_s': _jnp.float32, 'w_out': _jnp.float32}
MOMENT_SCALE = {'w_ada': 7.978792e+00, 'b_ada': 2.092370e+01, 'norm_gain': 3.594035e+01, 'w_in': 4.054891e+00, 'q_gain': 4.159609e+00, 'k_gain': 4.177346e+00, 'sink': 2.470576e+01, 'w_s': 1.161103e+00, 'b_s': 1.362137e+01, 'w_out': 1.879089e+00}


def _to_microbatches(a, axis):
    t = _jnp.moveaxis(a, axis, 0)
    t = t.reshape((N_MICROBATCH, t.shape[0] // N_MICROBATCH) + t.shape[1:])
    return _jnp.moveaxis(t, 1, axis + 1)


def setup_inputs(seed: int = 0) -> dict:
    inp = _fwd_setup_inputs(seed)
    key = _jax.random.fold_in(_jax.random.key(seed), 7919)
    shape, _ = _output_shape()
    out = dict(inp)
    out["loss_target"] = _jax.random.normal(_jax.random.fold_in(key, 0), shape, _jnp.float32)
    for i, name in enumerate(TWIN_WEIGHTS):
        w = inp[name].astype(_jnp.float32)
        if MOMENT_SCALE is None:
            s = _jnp.sqrt(_jnp.mean(_jnp.square(w)) + 1e-30)
        else:
            s = MOMENT_SCALE[name]
        km, kv = _jax.random.split(_jax.random.fold_in(key, i + 1))
        out[name] = w
        out["m_" + name] = s * _jax.random.normal(km, w.shape, _jnp.float32)
        out["v_" + name] = (s * s) * _jax.random.uniform(kv, w.shape, _jnp.float32, 0.5, 1.5)
    if N_MICROBATCH > 1:
        for name, axis in PER_EXAMPLE_BATCH_AXIS.items():
            out[name] = _to_microbatches(out[name], axis)
    return {'x': out['x'], 'c': out['c'], 'w_ada': out['w_ada'], 'b_ada': out['b_ada'], 'norm_gain': out['norm_gain'], 'w_in': out['w_in'], 'q_gain': out['q_gain'], 'k_gain': out['k_gain'], 'sink': out['sink'], 'w_s': out['w_s'], 'b_s': out['b_s'], 'w_out': out['w_out'], 'loss_target': out['loss_target'], 'm_w_ada': out['m_w_ada'], 'm_b_ada': out['m_b_ada'], 'm_norm_gain': out['m_norm_gain'], 'm_w_in': out['m_w_in'], 'm_q_gain': out['m_q_gain'], 'm_k_gain': out['m_k_gain'], 'm_sink': out['m_sink'], 'm_w_s': out['m_w_s'], 'm_b_s': out['m_b_s'], 'm_w_out': out['m_w_out'], 'v_w_ada': out['v_w_ada'], 'v_b_ada': out['v_b_ada'], 'v_norm_gain': out['v_norm_gain'], 'v_w_in': out['v_w_in'], 'v_q_gain': out['v_q_gain'], 'v_k_gain': out['v_k_gain'], 'v_sink': out['v_sink'], 'v_w_s': out['v_w_s'], 'v_b_s': out['v_b_s'], 'v_w_out': out['v_w_out']}


def _loss(weights, diff, rest, loss_target):
    with _jax.named_scope("forward"):
        args = {**rest, TWIN_DIFF_INPUT: diff, **{k: w.astype(_WEIGHT_DTYPES[k]) for k, w in weights.items()}}
        y = _forward(args)
    with _jax.named_scope("loss_head"):
        err = _jnp.square(y.astype(_jnp.float32) - loss_target)
        return 0.5 * _jnp.sum(_jnp.mean(err, axis=-1)) if err.ndim else 0.5 * err


def _adamw(w, g, m, v):
    m = ADAM_B1 * m + (1.0 - ADAM_B1) * g
    v = ADAM_B2 * v + (1.0 - ADAM_B2) * _jnp.square(g)
    m_hat = m / (1.0 - ADAM_B1 ** ADAM_STEP)
    v_hat = v / (1.0 - ADAM_B2 ** ADAM_STEP)
    delta = -ADAM_LR * (m_hat / (_jnp.sqrt(v_hat) + ADAM_EPS) + ADAM_WD * w)
    return delta, m, v


def reference(x, c, w_ada, b_ada, norm_gain, w_in, q_gain, k_gain, sink, w_s, b_s, w_out, loss_target, m_w_ada, m_b_ada, m_norm_gain, m_w_in, m_q_gain, m_k_gain, m_sink, m_w_s, m_b_s, m_w_out, v_w_ada, v_b_ada, v_norm_gain, v_w_in, v_q_gain, v_k_gain, v_sink, v_w_s, v_b_s, v_w_out):
    given = dict(x=x, c=c, w_ada=w_ada, b_ada=b_ada, norm_gain=norm_gain, w_in=w_in, q_gain=q_gain, k_gain=k_gain, sink=sink, w_s=w_s, b_s=b_s, w_out=w_out, loss_target=loss_target, m_w_ada=m_w_ada, m_b_ada=m_b_ada, m_norm_gain=m_norm_gain, m_w_in=m_w_in, m_q_gain=m_q_gain, m_k_gain=m_k_gain, m_sink=m_sink, m_w_s=m_w_s, m_b_s=m_b_s, m_w_out=m_w_out, v_w_ada=v_w_ada, v_b_ada=v_b_ada, v_norm_gain=v_norm_gain, v_w_in=v_w_in, v_q_gain=v_q_gain, v_k_gain=v_k_gain, v_sink=v_sink, v_w_s=v_w_s, v_b_s=v_b_s, v_w_out=v_w_out)
    weights = {n: given[n] for n in TWIN_WEIGHTS}
    shared = {n: given[n] for n in SHARED_INPUTS}
    per_example = {n: given[n] for n in ['x', 'c']}
    grad_fn = _jax.value_and_grad(_loss, argnums=(0, 1))

    def one_microbatch(ex, loss_target):
        ex = dict(ex)
        diff = ex.pop(TWIN_DIFF_INPUT)
        return grad_fn(weights, diff, {**shared, **ex}, loss_target)

    if N_MICROBATCH == 1:
        loss, (grad_w, grad_x) = one_microbatch(per_example, given["loss_target"])
    else:
        def body(carry, xs):
            loss_sum, grad_sum = carry
            l_k, (gw_k, gx_k) = one_microbatch(xs[0], xs[1])
            with _jax.named_scope("update"):
                return (loss_sum + l_k, _jax.tree.map(_jnp.add, grad_sum, gw_k)), gx_k

        init = (_jnp.zeros((), _jnp.float32), _jax.tree.map(_jnp.zeros_like, weights))
        (loss, grad_w), grad_x = _jax.lax.scan(body, init, (per_example, given["loss_target"]))
    with _jax.named_scope("update"):
        delta_w, new_m, new_v = {}, {}, {}
        for n in TWIN_WEIGHTS:
            delta_w[n], new_m[n], new_v[n] = _adamw(weights[n], grad_w[n], given["m_" + n], given["v_" + n])
    return (loss, grad_x, *[grad_w[n] for n in TWIN_WEIGHTS], *[delta_w[n] for n in TWIN_WEIGHTS],
            *[new_m[n] for n in TWIN_WEIGHTS], *[new_v[n] for n in TWIN_WEIGHTS])
```

```python
import functools

import jax
import jax.numpy as jnp
from jax import lax
from jax.experimental import pallas as pl
from jax.experimental.pallas import tpu as pltpu

F32 = jnp.float32
BF16 = jnp.bfloat16

D_MODEL = 1024
DEPTH = 2
HEAD_DIM = 64
N_HEADS = 8
BLOCK = 128
D_ATTN = 512
D_KV = 128
D_GM = 512
D_IN = 2816
N_CHIPS = 4
N_DEV = 8
W_IN_BLK = D_IN // N_CHIPS
W_OUT_BLK = D_MODEL // N_CHIPS
W_ADA_BLK = 3 * D_MODEL // N_CHIPS
EPS = 1e-6
NEG_INF = -1e30

C_Q, C_K, C_V, C_GA, C_U, C_VG, C_GG = 0, 512, 640, 768, 1280, 1792, 2304

ADAM_LR = 0.001
ADAM_B1 = 0.9
ADAM_B2 = 0.999
ADAM_EPS = 1e-08
ADAM_WD = 0.01
ADAM_STEP = 10

MESH = pl.DeviceIdType.MESH
MIB = 1024 * 1024

NT_DIMS = (((1,), (1,)), ((), ()))
TN_DIMS = (((0,), (0,)), ((), ()))


def _params(vmem_mib, n_grid=1):
    return pltpu.CompilerParams(dimension_semantics=("arbitrary",) * n_grid,
                                vmem_limit_bytes=vmem_mib * MIB)


def _full(shape):
    n = len(shape)
    return pl.BlockSpec(shape, lambda *_: (0,) * n)


def _sigmoid(x):
    return 1.0 / (1.0 + jnp.exp(-x))


def _lo_mask(shape):
    return lax.broadcasted_iota(jnp.int32, shape, len(shape) - 1) < HEAD_DIM


def _half_sum(x, lo):
    a = jnp.sum(jnp.where(lo, x, 0.0), axis=-1, keepdims=True)
    b = jnp.sum(jnp.where(lo, 0.0, x), axis=-1, keepdims=True)
    return jnp.where(lo, a, b)


def _half_rms_scale(x, lo):
    return lax.rsqrt(_half_sum(x * x, lo) * (1.0 / HEAD_DIM) + EPS)


def _stack_heads(pairs, lo):
    return jnp.concatenate([jnp.where(lo, t, 0.0) for t in pairs] + [jnp.where(lo, 0.0, t) for t in pairs], axis=0)


def _unstack_pair(stack, p, lo):
    return jnp.where(lo, stack[BLOCK * p:BLOCK * (p + 1)], stack[BLOCK * (4 + p):BLOCK * (5 + p)])


def _attention_probs(q_stack, kn, sink_ref, i, nb):
    rows = N_HEADS * BLOCK
    s = lax.dot_general(q_stack, kn, NT_DIMS, preferred_element_type=F32)
    row = lax.broadcasted_iota(jnp.int32, (rows, 3 * BLOCK), 0)
    col = lax.broadcasted_iota(jnp.int32, (rows, 3 * BLOCK), 1)
    dist = jnp.abs(col - BLOCK - (row & (BLOCK - 1)))
    first_key = jnp.where(i > 0, 0, BLOCK)
    end_key = jnp.where(i < nb - 1, 3 * BLOCK, 2 * BLOCK)
    valid = (dist <= BLOCK) & (col >= first_key) & (col < end_key)
    slope = jnp.concatenate([jnp.full((BLOCK, 1), 2.0 ** -(h + 1), F32) for h in range(N_HEADS)], axis=0)
    sink = jnp.concatenate([jnp.full((BLOCK, 1), sink_ref[h], F32) for h in range(N_HEADS)], axis=0)
    s = s * 0.125 - slope * dist.astype(F32)
    s = jnp.where(valid, s, NEG_INF)
    m = jnp.maximum(jnp.max(s, axis=-1, keepdims=True), sink)
    e = jnp.exp(s - m)
    es = jnp.exp(sink - m)
    denom = jnp.sum(e, axis=-1, keepdims=True) + es
    return e / denom, es / denom


def _window(cur_ref, pkv_ref, nkv_ref):
    kwin = jnp.concatenate([pkv_ref[:, 0:D_KV], cur_ref[:, C_K:C_K + D_KV], nkv_ref[:, 0:D_KV]], axis=0)
    vwin = jnp.concatenate([pkv_ref[:, D_KV:2 * D_KV], cur_ref[:, C_V:C_V + D_KV], nkv_ref[:, D_KV:2 * D_KV]], axis=0)
    return kwin, vwin


def _mix_specs(nb):
    cur = pl.BlockSpec((BLOCK, D_IN), lambda i: (i, 0))
    pkv = pl.BlockSpec((BLOCK, 2 * D_KV), lambda i: (jnp.maximum(i - 1, 0), C_K // (2 * D_KV)))
    nkv = pl.BlockSpec((BLOCK, 2 * D_KV), lambda i: (jnp.minimum(i + 1, nb - 1), C_K // (2 * D_KV)))
    return cur, pkv, nkv


def _proj_fwd(x, ng, scale, shift, w, name):
    s = x.shape[0]
    ts = min(512, s)

    def body(x_ref, ng_ref, sc_ref, sh_ref, w_ref, o_ref):
        xv = x_ref[...]
        r = lax.rsqrt(jnp.mean(xv * xv, axis=-1, keepdims=True) + EPS)
        h = ((xv * r) * ng_ref[...]) * (1.0 + sc_ref[...]) + sh_ref[...]
        o_ref[...] = jnp.dot(h.astype(BF16), w_ref[...], preferred_element_type=F32)

    vec = _full((1, D_MODEL))
    return pl.pallas_call(
        body, name=name, grid=(s // ts,),
        in_specs=[pl.BlockSpec((ts, D_MODEL), lambda i: (i, 0)), vec, vec, vec, _full((D_MODEL, D_IN))],
        out_specs=pl.BlockSpec((ts, D_IN), lambda i: (i, 0)),
        out_shape=jax.ShapeDtypeStruct((s, D_IN), F32),
        compiler_params=_params(48),
    )(x, ng, scale, shift, w)


def _mix_fwd(proj, sink, qg2, kg2, ws, bsp, name):
    s = proj.shape[0]
    nb = s // BLOCK

    def body(sink_ref, cur_ref, pkv_ref, nkv_ref, qg_ref, kg_ref, ws_ref, bsp_ref, y_ref):
        i = pl.program_id(0)
        lo = _lo_mask((BLOCK, BLOCK))
        lo_w = _lo_mask((3 * BLOCK, BLOCK))
        kwin, vwin = _window(cur_ref, pkv_ref, nkv_ref)
        kn = ((kwin * _half_rms_scale(kwin, lo_w)) * kg_ref[...]).astype(BF16)
        qn = []
        for p in range(4):
            q = cur_ref[:, C_Q + BLOCK * p:C_Q + BLOCK * (p + 1)]
            qn.append((q * _half_rms_scale(q, lo)) * qg_ref[...])
        q_stack = _stack_heads(qn, lo).astype(BF16)
        prob, _ = _attention_probs(q_stack, kn, sink_ref, i, nb)
        o_stack = jnp.dot(prob.astype(BF16), vwin.astype(BF16), preferred_element_type=F32)
        for p in range(4):
            g = cur_ref[:, C_GA + BLOCK * p:C_GA + BLOCK * (p + 1)]
            y_ref[:, BLOCK * p:BLOCK * (p + 1)] = (_unstack_pair(o_stack, p, lo) * (g * _sigmoid(g))).astype(BF16)
        for p in range(4):
            vg = cur_ref[:, C_VG + BLOCK * p:C_VG + BLOCK * (p + 1)]
            vn = (vg * _half_rms_scale(vg, lo)).astype(BF16)
            sv = jnp.where(lo, jnp.dot(ws_ref[2 * p], vn, preferred_element_type=F32),
                           jnp.dot(ws_ref[2 * p + 1], vn, preferred_element_type=F32)) + bsp_ref[p]
            u = cur_ref[:, C_U + BLOCK * p:C_U + BLOCK * (p + 1)]
            g = cur_ref[:, C_GG + BLOCK * p:C_GG + BLOCK * (p + 1)]
            y_ref[:, D_ATTN + BLOCK * p:D_ATTN + BLOCK * (p + 1)] = ((u * sv) * (g * _sigmoid(g))).astype(BF16)

    cur, pkv, nkv = _mix_specs(nb)
    return pl.pallas_call(
        body, name=name, grid=(nb,),
        in_specs=[pl.BlockSpec(memory_space=pltpu.SMEM), cur, pkv, nkv, _full((1, BLOCK)), _full((1, BLOCK)),
                  _full((8, BLOCK, BLOCK)), _full((4, BLOCK, BLOCK))],
        out_specs=pl.BlockSpec((BLOCK, D_MODEL), lambda i: (i, 0)),
        out_shape=jax.ShapeDtypeStruct((s, D_MODEL), BF16),
        compiler_params=_params(40),
    )(sink, proj, proj, proj, qg2, kg2, ws, bsp)


def _out_fwd(y, x, gate, w_out, name):
    s = x.shape[0]
    ts = min(512, s)

    def body(y_ref, x_ref, g_ref, w_ref, o_ref):
        o_ref[...] = x_ref[...] + g_ref[...] * jnp.dot(y_ref[...], w_ref[...], preferred_element_type=F32)

    row = pl.BlockSpec((ts, D_MODEL), lambda i: (i, 0))
    return pl.pallas_call(
        body, name=name, grid=(s // ts,),
        in_specs=[row, row, _full((1, D_MODEL)), _full((D_MODEL, D_MODEL))],
        out_specs=row,
        out_shape=jax.ShapeDtypeStruct((s, D_MODEL), F32),
        compiler_params=_params(32),
    )(y, x, gate, w_out)


def _out_fwd_loss(y, x, gate, w_out, target, name):
    s = x.shape[0]
    ts = min(512, s)

    def body(y_ref, x_ref, g_ref, w_ref, t_ref, dx_ref, sq_ref):
        @pl.when(pl.program_id(0) == 0)
        def _():
            sq_ref[...] = jnp.zeros_like(sq_ref)

        out = x_ref[...] + g_ref[...] * jnp.dot(y_ref[...], w_ref[...], preferred_element_type=F32)
        diff = out - t_ref[...]
        dx_ref[...] = diff * (1.0 / D_MODEL)
        per_token = jnp.sum(diff * diff, axis=-1, keepdims=True) * (1.0 / D_MODEL)
        sq_ref[...] += jnp.sum(per_token, axis=0, keepdims=True)

    row = pl.BlockSpec((ts, D_MODEL), lambda i: (i, 0))
    return pl.pallas_call(
        body, name=name, grid=(s // ts,),
        in_specs=[row, row, _full((1, D_MODEL)), _full((D_MODEL, D_MODEL)), row],
        out_specs=[row, _full((1, 1))],
        out_shape=[jax.ShapeDtypeStruct((s, D_MODEL), F32), jax.ShapeDtypeStruct((1, 1), F32)],
        compiler_params=_params(40),
    )(y, x, gate, w_out, target)


def _out_bwd(dxo, y, gate, w_out, name):
    s = dxo.shape[0]
    ts = min(512, s)

    def body(dx_ref, y_ref, g_ref, w_ref, dy_ref, gw_ref, dg_ref):
        @pl.when(pl.program_id(0) == 0)
        def _():
            gw_ref[...] = jnp.zeros_like(gw_ref)
            dg_ref[...] = jnp.zeros_like(dg_ref)

        dx = dx_ref[...]
        do = (dx * g_ref[...]).astype(BF16)
        yv = y_ref[...]
        dy_ref[...] = lax.dot_general(do, w_ref[...], NT_DIMS, preferred_element_type=F32)
        gw_ref[...] += lax.dot_general(yv, do, TN_DIMS, preferred_element_type=F32)
        o = jnp.dot(yv, w_ref[...], preferred_element_type=F32)
        dg_ref[...] += jnp.sum(dx * o, axis=0, keepdims=True)

    row = pl.BlockSpec((ts, D_MODEL), lambda i: (i, 0))
    return pl.pallas_call(
        body, name=name, grid=(s // ts,),
        in_specs=[row, row, _full((1, D_MODEL)), _full((D_MODEL, D_MODEL))],
        out_specs=[row, _full((D_MODEL, D_MODEL)), _full((1, D_MODEL))],
        out_shape=[jax.ShapeDtypeStruct((s, D_MODEL), F32), jax.ShapeDtypeStruct((D_MODEL, D_MODEL), F32),
                   jax.ShapeDtypeStruct((1, D_MODEL), F32)],
        compiler_params=_params(48),
    )(dxo, y, gate, w_out)


def _mix_bwd(dy, proj, sink, qg2, kg2, ws, wst, bsp, name):
    s = proj.shape[0]
    nb = s // BLOCK

    def body(sink_ref, dy_ref, cur_ref, pkv_ref, nkv_ref, qg_ref, kg_ref, ws_ref, wst_ref, bsp_ref,
             dpm_ref, p0_ref, p2_ref, dqg_ref, dkg_ref, dsink_ref, dws_ref, dbsp_ref):
        i = pl.program_id(0)

        @pl.when(i == 0)
        def _():
            dqg_ref[...] = jnp.zeros_like(dqg_ref)
            dkg_ref[...] = jnp.zeros_like(dkg_ref)
            dsink_ref[...] = jnp.zeros_like(dsink_ref)
            dws_ref[...] = jnp.zeros_like(dws_ref)
            dbsp_ref[...] = jnp.zeros_like(dbsp_ref)

        lo = _lo_mask((BLOCK, BLOCK))
        lo_w = _lo_mask((3 * BLOCK, BLOCK))
        qg = qg_ref[...]
        kg = kg_ref[...]

        kwin, vwin = _window(cur_ref, pkv_ref, nkv_ref)
        rk = _half_rms_scale(kwin, lo_w)
        khat = kwin * rk
        kn = (khat * kg).astype(BF16)
        vb = vwin.astype(BF16)
        qhat, rq = [], []
        for p in range(4):
            q = cur_ref[:, C_Q + BLOCK * p:C_Q + BLOCK * (p + 1)]
            r = _half_rms_scale(q, lo)
            rq.append(r)
            qhat.append(q * r)
        q_stack = _stack_heads([qh * qg for qh in qhat], lo).astype(BF16)
        prob, psink = _attention_probs(q_stack, kn, sink_ref, i, nb)
        pb = prob.astype(BF16)
        o_stack = jnp.dot(pb, vb, preferred_element_type=F32)

        dout = []
        for p in range(4):
            g = cur_ref[:, C_GA + BLOCK * p:C_GA + BLOCK * (p + 1)]
            sg = _sigmoid(g)
            dya = dy_ref[:, BLOCK * p:BLOCK * (p + 1)]
            attn = _unstack_pair(o_stack, p, lo)
            dpm_ref[:, C_GA + BLOCK * p:C_GA + BLOCK * (p + 1)] = dya * attn * (sg * (1.0 + g * (1.0 - sg)))
            dout.append(dya * (g * sg))
        do_stack = _stack_heads(dout, lo).astype(BF16)
        dp = lax.dot_general(do_stack, vb, NT_DIMS, preferred_element_type=F32)
        delta = jnp.sum(prob * dp, axis=-1, keepdims=True)
        dsb = (prob * (dp - delta)).astype(BF16)

        csink = -(psink * delta)
        lane_row = lax.broadcasted_iota(jnp.int32, (1, BLOCK), 1)
        dsink = jnp.zeros((1, BLOCK), F32)
        for h in range(N_HEADS):
            tot = jnp.sum(csink[BLOCK * h:BLOCK * (h + 1)], axis=0, keepdims=True)
            dsink = dsink + jnp.where(lane_row == h, tot, 0.0)
        dsink_ref[...] += dsink

        dq_stack = jnp.dot(dsb, kn, preferred_element_type=F32) * 0.125
        dkn = lax.dot_general(dsb, q_stack, TN_DIMS, preferred_element_type=F32) * 0.125
        dv = lax.dot_general(pb, do_stack, TN_DIMS, preferred_element_type=F32)

        dqg = jnp.zeros((1, BLOCK), F32)
        for p in range(4):
            dqn = _unstack_pair(dq_stack, p, lo)
            qh = qhat[p]
            dqg = dqg + jnp.sum(dqn * qh, axis=0, keepdims=True)
            dqh = dqn * qg
            mean = _half_sum(dqh * qh, lo) * (1.0 / HEAD_DIM)
            dpm_ref[:, C_Q + BLOCK * p:C_Q + BLOCK * (p + 1)] = rq[p] * (dqh - qh * mean)
        dqg = jnp.broadcast_to(dqg, (8, BLOCK))
        dqg_ref[...] += dqg + pltpu.roll(dqg, HEAD_DIM, 1)
        dkg = jnp.broadcast_to(jnp.sum(dkn * khat, axis=0, keepdims=True), (8, BLOCK))
        dkg_ref[...] += dkg + pltpu.roll(dkg, HEAD_DIM, 1)
        dkh = dkn * kg
        dk = rk * (dkh - khat * (_half_sum(dkh * khat, lo_w) * (1.0 / HEAD_DIM)))
        dpm_ref[:, C_K:C_K + D_KV] = dk[BLOCK:2 * BLOCK]
        dpm_ref[:, C_V:C_V + D_KV] = dv[BLOCK:2 * BLOCK]
        p0_ref[:, 0:D_KV] = dk[0:BLOCK]
        p0_ref[:, D_KV:2 * D_KV] = dv[0:BLOCK]
        p2_ref[:, 0:D_KV] = dk[2 * BLOCK:3 * BLOCK]
        p2_ref[:, D_KV:2 * D_KV] = dv[2 * BLOCK:3 * BLOCK]

        for p in range(4):
            vg = cur_ref[:, C_VG + BLOCK * p:C_VG + BLOCK * (p + 1)]
            r = _half_rms_scale(vg, lo)
            vnf = vg * r
            vn = vnf.astype(BF16)
            sv = jnp.where(lo, jnp.dot(ws_ref[2 * p], vn, preferred_element_type=F32),
                           jnp.dot(ws_ref[2 * p + 1], vn, preferred_element_type=F32)) + bsp_ref[p]
            u = cur_ref[:, C_U + BLOCK * p:C_U + BLOCK * (p + 1)]
            g = cur_ref[:, C_GG + BLOCK * p:C_GG + BLOCK * (p + 1)]
            sg = _sigmoid(g)
            dym = dy_ref[:, D_ATTN + BLOCK * p:D_ATTN + BLOCK * (p + 1)]
            dpm_ref[:, C_GG + BLOCK * p:C_GG + BLOCK * (p + 1)] = dym * (u * sv) * (sg * (1.0 + g * (1.0 - sg)))
            dgm = dym * (g * sg)
            dpm_ref[:, C_U + BLOCK * p:C_U + BLOCK * (p + 1)] = dgm * sv
            dsv = dgm * u
            dsv_a = jnp.where(lo, dsv, 0.0)
            dsv_b = jnp.where(lo, 0.0, dsv)
            dws_ref[2 * p] += lax.dot_general(dsv_a.astype(BF16), vn, NT_DIMS, preferred_element_type=F32)
            dws_ref[2 * p + 1] += lax.dot_general(dsv_b.astype(BF16), vn, NT_DIMS, preferred_element_type=F32)
            dbsp_ref[p] += jnp.where(lo, jnp.sum(dsv_a, axis=-1, keepdims=True), jnp.sum(dsv_b, axis=-1, keepdims=True))
            dsvb = dsv.astype(BF16)
            dvn = jnp.where(lo, jnp.dot(wst_ref[2 * p], dsvb, preferred_element_type=F32),
                            jnp.dot(wst_ref[2 * p + 1], dsvb, preferred_element_type=F32))
            mean = _half_sum(dvn * vnf, lo) * (1.0 / HEAD_DIM)
            dpm_ref[:, C_VG + BLOCK * p:C_VG + BLOCK * (p + 1)] = r * (dvn - vnf * mean)

    cur, pkv, nkv = _mix_specs(nb)
    kv_blk = (BLOCK, 2 * D_KV)
    return pl.pallas_call(
        body, name=name, grid=(nb,),
        in_specs=[pl.BlockSpec(memory_space=pltpu.SMEM), pl.BlockSpec((BLOCK, D_MODEL), lambda i: (i, 0)),
                  cur, pkv, nkv, _full((1, BLOCK)), _full((1, BLOCK)),
                  _full((8, BLOCK, BLOCK)), _full((8, BLOCK, BLOCK)), _full((4, BLOCK, BLOCK))],
        out_specs=[cur,
                   pl.BlockSpec(kv_blk, lambda i: ((i + nb - 1) % nb, 0)),
                   pl.BlockSpec(kv_blk, lambda i: ((i + 1) % nb, 0)),
                   _full((8, BLOCK)), _full((8, BLOCK)), _full((1, BLOCK)),
                   _full((8, BLOCK, BLOCK)), _full((4, BLOCK, BLOCK))],
        out_shape=[jax.ShapeDtypeStruct((s, D_IN), F32),
                   jax.ShapeDtypeStruct((s, 2 * D_KV), F32), jax.ShapeDtypeStruct((s, 2 * D_KV), F32),
                   jax.ShapeDtypeStruct((8, BLOCK), F32), jax.ShapeDtypeStruct((8, BLOCK), F32),
                   jax.ShapeDtypeStruct((1, BLOCK), F32),
                   jax.ShapeDtypeStruct((8, BLOCK, BLOCK), F32), jax.ShapeDtypeStruct((4, BLOCK, BLOCK), F32)],
        compiler_params=_params(48),
    )(sink, dy, proj, proj, proj, qg2, kg2, ws, wst, bsp)


def _proj_bwd(dpm, p0, p2, x, dxo, ng, scale, shift, w, name):
    s = x.shape[0]
    ts = min(256, s)

    def body(dpm_ref, p0_ref, p2_ref, x_ref, dxo_ref, ng_ref, sc_ref, sh_ref, w_ref,
             dxi_ref, dpb_ref, h_ref, dsh_ref, dsc_ref, dng_ref):
        @pl.when(pl.program_id(0) == 0)
        def _():
            dsh_ref[...] = jnp.zeros_like(dsh_ref)
            dsc_ref[...] = jnp.zeros_like(dsc_ref)
            dng_ref[...] = jnp.zeros_like(dng_ref)

        dpb_ref[:, 0:C_K] = dpm_ref[:, 0:C_K].astype(BF16)
        dpb_ref[:, C_K:C_GA] = (dpm_ref[:, C_K:C_GA] + p0_ref[...] + p2_ref[...]).astype(BF16)
        dpb_ref[:, C_GA:D_IN] = dpm_ref[:, C_GA:D_IN].astype(BF16)
        dh = lax.dot_general(dpb_ref[...], w_ref[...], NT_DIMS, preferred_element_type=F32)

        xv = x_ref[...]
        r = lax.rsqrt(jnp.mean(xv * xv, axis=-1, keepdims=True) + EPS)
        xn = xv * r
        ngv = ng_ref[...]
        sc1 = 1.0 + sc_ref[...]
        h1 = xn * ngv
        h_ref[...] = (h1 * sc1 + sh_ref[...]).astype(BF16)
        dsh_ref[...] += jnp.sum(dh, axis=0, keepdims=True)
        dsc_ref[...] += jnp.sum(dh * h1, axis=0, keepdims=True)
        dh1 = dh * sc1
        dng_ref[...] += jnp.sum(dh1 * xn, axis=0, keepdims=True)
        dxn = dh1 * ngv
        dxi_ref[...] = r * (dxn - xn * jnp.mean(dxn * xn, axis=-1, keepdims=True)) + dxo_ref[...]

    row = pl.BlockSpec((ts, D_MODEL), lambda i: (i, 0))
    wide = pl.BlockSpec((ts, D_IN), lambda i: (i, 0))
    kv = pl.BlockSpec((ts, 2 * D_KV), lambda i: (i, 0))
    vec = _full((1, D_MODEL))
    return pl.pallas_call(
        body, name=name, grid=(s // ts,),
        in_specs=[wide, kv, kv, row, row, vec, vec, vec, _full((D_MODEL, D_IN))],
        out_specs=[row, wide, row, vec, vec, vec],
        out_shape=[jax.ShapeDtypeStruct((s, D_MODEL), F32), jax.ShapeDtypeStruct((s, D_IN), BF16),
                   jax.ShapeDtypeStruct((s, D_MODEL), BF16),
                   jax.ShapeDtypeStruct((1, D_MODEL), F32), jax.ShapeDtypeStruct((1, D_MODEL), F32),
                   jax.ShapeDtypeStruct((1, D_MODEL), F32)],
        compiler_params=_params(48),
    )(dpm, p0, p2, x, dxo, ng, scale, shift, w)


def _w_in_grad(h, dpb, name):
    s = h.shape[0]
    tk = min(1024, s)
    tn = D_IN // 2

    def body(h_ref, d_ref, o_ref):
        @pl.when(pl.program_id(1) == 0)
        def _():
            o_ref[...] = jnp.zeros_like(o_ref)

        o_ref[...] += lax.dot_general(h_ref[...], d_ref[...], TN_DIMS, preferred_element_type=F32)

    return pl.pallas_call(
        body, name=name, grid=(2, s // tk),
        in_specs=[pl.BlockSpec((tk, D_MODEL), lambda n, k: (k, 0)), pl.BlockSpec((tk, tn), lambda n, k: (k, n))],
        out_specs=pl.BlockSpec((D_MODEL, tn), lambda n, k: (0, n)),
        out_shape=jax.ShapeDtypeStruct((D_MODEL, D_IN), F32),
        compiler_params=_params(40, 2),
    )(h, dpb)


def _permute_heads(a, axis):
    shp = a.shape
    a = a.reshape(shp[:axis] + (2, 4, HEAD_DIM) + shp[axis + 1:])
    a = jnp.swapaxes(a, axis, axis + 1)
    return a.reshape(shp)


def _unpermute_heads(a, axis):
    shp = a.shape
    a = a.reshape(shp[:axis] + (4, 2, HEAD_DIM) + shp[axis + 1:])
    a = jnp.swapaxes(a, axis, axis + 1)
    return a.reshape(shp)


def _permute_w_in(w):
    return jnp.concatenate([_permute_heads(w[..., C_Q:C_K], w.ndim - 1), w[..., C_K:C_GA],
                            _permute_heads(w[..., C_GA:C_U], w.ndim - 1), w[..., C_U:]], axis=-1)


def _unpermute_w_in(w):
    return jnp.concatenate([_unpermute_heads(w[..., C_Q:C_K], w.ndim - 1), w[..., C_K:C_GA],
                            _unpermute_heads(w[..., C_GA:C_U], w.ndim - 1), w[..., C_U:]], axis=-1)


def _permute_w_out(w):
    return jnp.concatenate([_permute_heads(w[..., :D_ATTN, :], w.ndim - 2), w[..., D_ATTN:, :]], axis=-2)


def _unpermute_w_out(w):
    return jnp.concatenate([_unpermute_heads(w[..., :D_ATTN, :], w.ndim - 2), w[..., D_ATTN:, :]], axis=-2)


def _local_step(x, target, ada, w_in_p, w_out_p, norm_gain, q_gain, k_gain, sink, w_s, b_s):
    qg2 = jnp.concatenate([q_gain, q_gain], axis=-1)
    kg2 = jnp.concatenate([k_gain, k_gain], axis=-1)
    ws_b = w_s.astype(BF16)
    wst_b = jnp.swapaxes(w_s, -1, -2).astype(BF16)
    bsp = jnp.repeat(jnp.swapaxes(b_s.reshape(DEPTH, 4, 2, BLOCK), -1, -2), HEAD_DIM, axis=-1)

    shift = [ada[l:l + 1, 0:D_MODEL] for l in range(DEPTH)]
    scale = [ada[l:l + 1, D_MODEL:2 * D_MODEL] for l in range(DEPTH)]
    gate = [ada[l:l + 1, 2 * D_MODEL:] for l in range(DEPTH)]
    ng = [norm_gain[l:l + 1] for l in range(DEPTH)]

    xs, projs, ys = [x], [], []
    for l in range(DEPTH):
        proj = _proj_fwd(xs[l], ng[l], scale[l], shift[l], w_in_p[l], f"proj_fwd_{l}")
        y = _mix_fwd(proj, sink[l], qg2[l:l + 1], kg2[l:l + 1], ws_b[l], bsp[l], f"mix_fwd_{l}")
        projs.append(proj)
        ys.append(y)
        if l < DEPTH - 1:
            xs.append(_out_fwd(y, xs[l], gate[l], w_out_p[l], f"out_fwd_{l}"))
        else:
            dx, sq = _out_fwd_loss(y, xs[l], gate[l], w_out_p[l], target, f"out_fwd_loss_{l}")

    g = {k: [None] * DEPTH for k in ("w_in", "w_out", "ada", "ng", "qg", "kg", "sink", "ws", "bs")}
    for l in reversed(range(DEPTH)):
        dy, g["w_out"][l], dgate = _out_bwd(dx, ys[l], gate[l], w_out_p[l], f"out_bwd_{l}")
        dpm, p0, p2, dqg, dkg, dsink, dws, dbsp = _mix_bwd(
            dy, projs[l], sink[l], qg2[l:l + 1], kg2[l:l + 1], ws_b[l], wst_b[l], bsp[l], f"mix_bwd_{l}")
        dx, dpb, h, dsh, dsc, dng = _proj_bwd(dpm, p0, p2, xs[l], dx, ng[l], scale[l], shift[l], w_in_p[l],
                                              f"proj_bwd_{l}")
        g["w_in"][l] = _w_in_grad(h, dpb, f"w_in_grad_{l}")
        g["ada"][l] = jnp.concatenate([dsh, dsc, dgate], axis=-1)[0]
        g["ng"][l] = dng[0]
        g["qg"][l] = dqg[0, :HEAD_DIM]
        g["kg"][l] = dkg[0, :HEAD_DIM]
        g["sink"][l] = dsink[0, :N_HEADS]
        g["ws"][l] = dws
        g["bs"][l] = jnp.swapaxes(dbsp[:, :, ::HEAD_DIM], -1, -2).reshape(8, BLOCK)
    return sq, dx, {k: jnp.stack(v) for k, v in g.items()}


def _coords():
    return lax.axis_index("x"), lax.axis_index("y"), lax.axis_index("c")


def _other_chips(x, y):
    return [(1 - x, y), (x, 1 - y), (1 - x, 1 - y)]


def _gather_inputs(c, w_in, w_out):
    def body(c_ref, win_ref, wout_ref, call_ref, gin_ref, gout_ref, send_sems, recv_sems):
        x, y, cc = _coords()
        j = 2 * x + y
        b = 2 * j + cc
        gin_ref[j] = win_ref[...].astype(BF16)
        gout_ref[j] = wout_ref[...].astype(BF16)
        call_ref[b] = c_ref[...]

        def copy(ref, idx, k, dev):
            return pltpu.make_async_remote_copy(src_ref=ref.at[idx], dst_ref=ref.at[idx], send_sem=send_sems.at[k],
                                                recv_sem=recv_sems.at[k], device_id=dev, device_id_type=MESH)

        copies = []
        for k, chip in enumerate(_other_chips(x, y)):
            copies.append(copy(gin_ref, j, k, (*chip, cc)))
            copies.append(copy(gout_ref, j, 3 + k, (*chip, cc)))
        k = 6
        for fx in (0, 1):
            for fy in (0, 1):
                for fc in (0, 1):
                    if fx or fy or fc:
                        dev = (1 - x if fx else x, 1 - y if fy else y, 1 - cc if fc else cc)
                        copies.append(copy(call_ref, b, k, dev))
                        k += 1
        for cp in copies:
            cp.start()
        for cp in copies:
            cp.wait()

    vm = pl.BlockSpec(memory_space=pltpu.VMEM)
    return pl.pallas_call(
        body, name="gather_inputs",
        in_specs=[vm, vm, vm], out_specs=[vm, vm, vm],
        out_shape=[jax.ShapeDtypeStruct((N_DEV, 1, D_MODEL), F32),
                   jax.ShapeDtypeStruct((N_CHIPS, DEPTH, D_MODEL, W_IN_BLK), BF16),
                   jax.ShapeDtypeStruct((N_CHIPS, DEPTH, W_OUT_BLK, D_MODEL), BF16)],
        scratch_shapes=[pltpu.SemaphoreType.DMA((13,)), pltpu.SemaphoreType.DMA((13,))],
        compiler_params=pltpu.CompilerParams(vmem_limit_bytes=48 * MIB),
    )(c, w_in, w_out)


def _ada_rows(c_all, w_ada, b_blk):
    def body(c_ref, w_ref, b_ref, o_ref, cond_ref):
        cv = c_ref[...]
        cond = (cv * _sigmoid(cv)).astype(BF16)
        cond_ref[...] = cond.astype(F32)
        for l in range(DEPTH):
            o_ref[:, l, :] = jnp.dot(cond, w_ref[l].astype(BF16), preferred_element_type=F32) + b_ref[l:l + 1, :]

    vm = pl.BlockSpec(memory_space=pltpu.VMEM)
    return pl.pallas_call(
        body, name="ada_rows", in_specs=[vm, vm, vm], out_specs=[vm, vm],
        out_shape=[jax.ShapeDtypeStruct((N_DEV, DEPTH, W_ADA_BLK), F32), jax.ShapeDtypeStruct((N_DEV, D_MODEL), F32)],
        compiler_params=pltpu.CompilerParams(vmem_limit_bytes=32 * MIB),
    )(c_all, w_ada, b_blk)


def _exchange_ada(part):
    def body(part_ref, out_ref, send_sems, recv_sems):
        x, y, cc = _coords()
        j = 2 * x + y
        out_ref[j] = part_ref[2 * j + cc]
        copies = []
        for k, chip in enumerate(_other_chips(x, y)):
            b_dst = 4 * chip[0] + 2 * chip[1] + cc
            copies.append(pltpu.make_async_remote_copy(
                src_ref=part_ref.at[b_dst], dst_ref=out_ref.at[j], send_sem=send_sems.at[k], recv_sem=recv_sems.at[k],
                device_id=(*chip, cc), device_id_type=MESH))
        for cp in copies:
            cp.start()
        for cp in copies:
            cp.wait()

    vm = pl.BlockSpec(memory_space=pltpu.VMEM)
    return pl.pallas_call(
        body, name="exchange_ada", in_specs=[vm], out_specs=vm,
        out_shape=jax.ShapeDtypeStruct((N_CHIPS, DEPTH, W_ADA_BLK), F32),
        scratch_shapes=[pltpu.SemaphoreType.DMA((3,)), pltpu.SemaphoreType.DMA((3,))],
    )(part)


def _all_gather_rows(blk):
    m_per, n = blk.shape

    def body(x_ref, out_ref, send_sems, recv_sems, local_sem):
        x, y, c = _coords()
        me, sibling = (x, y, c), (x, y, 1 - c)
        chips = _other_chips(x, y)

        def rows(px, py, pc):
            return out_ref.at[pl.ds((4 * px + 2 * py + pc) * m_per, m_per), :]

        def copy(k, block, to, src=None):
            return pltpu.make_async_remote_copy(
                src_ref=rows(*block) if src is None else src, dst_ref=rows(*block),
                send_sem=send_sems.at[k], recv_sem=recv_sems.at[k], device_id=to, device_id_type=MESH)

        mine = pltpu.make_async_copy(x_ref, rows(*me), local_sem)
        mine.start()
        first = [copy(0, me, sibling, src=x_ref)]
        first += [copy(1 + j, me, (*chip, c), src=x_ref) for j, chip in enumerate(chips)]
        for cp in first:
            cp.start()
        passed = [copy(4 + j, (*chip, c), sibling) for j, chip in enumerate(chips)]
        for j, chip in enumerate(chips):
            copy(1 + j, (*chip, c), me).wait_recv()
            passed[j].start()
        copy(0, sibling, me).wait_recv()
        for j, chip in enumerate(chips):
            copy(4 + j, (*chip, 1 - c), me).wait_recv()
        for cp in first + passed:
            cp.wait_send()
        mine.wait()

    vm = pl.BlockSpec(memory_space=pltpu.VMEM)
    return pl.pallas_call(
        body, name="all_gather_small", in_specs=[vm], out_specs=vm,
        out_shape=jax.ShapeDtypeStruct((N_DEV * m_per, n), blk.dtype),
        scratch_shapes=[pltpu.SemaphoreType.DMA((7,)), pltpu.SemaphoreType.DMA((7,)), pltpu.SemaphoreType.DMA],
        compiler_params=pltpu.CompilerParams(vmem_limit_bytes=32 * MIB),
    )(blk)


ANY = pl.BlockSpec(memory_space=pl.ANY)


def _pair_swap_halves(gin, gout):
    hin, hout = D_MODEL // 2, W_OUT_BLK // 2

    def body(gin_ref, gout_ref, rin_ref, rout_ref, send_sems, recv_sems):
        x, y, c = _coords()
        sib = (x, y, 1 - c)
        a = pltpu.make_async_remote_copy(
            src_ref=gin_ref.at[:, :, pl.ds((1 - c) * hin, hin), :], dst_ref=rin_ref,
            send_sem=send_sems.at[0], recv_sem=recv_sems.at[0], device_id=sib, device_id_type=MESH)
        b = pltpu.make_async_remote_copy(
            src_ref=gout_ref.at[:, :, pl.ds((1 - c) * hout, hout), :], dst_ref=rout_ref,
            send_sem=send_sems.at[1], recv_sem=recv_sems.at[1], device_id=sib, device_id_type=MESH)
        a.start()
        b.start()
        a.wait()
        b.wait()

    return pl.pallas_call(
        body, name="pair_swap_halves", in_specs=[ANY, ANY], out_specs=[ANY, ANY],
        out_shape=[jax.ShapeDtypeStruct((N_CHIPS, DEPTH, hin, W_IN_BLK), F32),
                   jax.ShapeDtypeStruct((N_CHIPS, DEPTH, hout, D_MODEL), F32)],
        scratch_shapes=[pltpu.SemaphoreType.DMA((2,)), pltpu.SemaphoreType.DMA((2,))],
    )(gin, gout)


def _pair_sum(g, r, c_idx, name):
    _, _, rows, cols = g.shape
    half = rows // 2
    tr = min(256, half)
    nt = half // tr

    def body(c_ref, g_ref, r_ref, o_ref):
        o_ref[...] = g_ref[...] + r_ref[...]

    blk = (None, None, tr, cols)
    return pl.pallas_call(
        body, name=name,
        grid_spec=pltpu.PrefetchScalarGridSpec(
            num_scalar_prefetch=1, grid=(N_CHIPS, DEPTH, nt),
            in_specs=[pl.BlockSpec(blk, lambda j, l, t, c: (j, l, c[0] * nt + t, 0)),
                      pl.BlockSpec(blk, lambda j, l, t, c: (j, l, t, 0))],
            out_specs=pl.BlockSpec(blk, lambda j, l, t, c: (j, l, t, 0))),
        out_shape=jax.ShapeDtypeStruct((N_CHIPS, DEPTH, half, cols), F32),
        compiler_params=_params(32, 3),
    )(c_idx, g, r)


def _chip_scatter(pin, pout):
    def body(pin_ref, pout_ref, rin_ref, rout_ref, send_sems, recv_sems):
        x, y, c = _coords()
        copies = []
        for k, chip in enumerate(_other_chips(x, y)):
            jd = 2 * chip[0] + chip[1]
            copies.append(pltpu.make_async_remote_copy(
                src_ref=pin_ref.at[jd], dst_ref=rin_ref.at[k], send_sem=send_sems.at[k], recv_sem=recv_sems.at[k],
                device_id=(*chip, c), device_id_type=MESH))
            copies.append(pltpu.make_async_remote_copy(
                src_ref=pout_ref.at[jd], dst_ref=rout_ref.at[k], send_sem=send_sems.at[3 + k],
                recv_sem=recv_sems.at[3 + k], device_id=(*chip, c), device_id_type=MESH))
        for cp in copies:
            cp.start()
        for cp in copies:
            cp.wait()

    return pl.pallas_call(
        body, name="chip_scatter", in_specs=[ANY, ANY], out_specs=[ANY, ANY],
        out_shape=[jax.ShapeDtypeStruct((3,) + pin.shape[1:], F32), jax.ShapeDtypeStruct((3,) + pout.shape[1:], F32)],
        scratch_shapes=[pltpu.SemaphoreType.DMA((6,)), pltpu.SemaphoreType.DMA((6,))],
    )(pin, pout)


def _chip_sum(p, r, j_idx, name):
    _, _, rows, cols = p.shape
    tr = min(256, rows)

    def body(j_ref, p_ref, r_ref, o_ref):
        o_ref[...] = ((p_ref[...] + r_ref[0]) + r_ref[1]) + r_ref[2]

    return pl.pallas_call(
        body, name=name,
        grid_spec=pltpu.PrefetchScalarGridSpec(
            num_scalar_prefetch=1, grid=(DEPTH, rows // tr),
            in_specs=[pl.BlockSpec((None, None, tr, cols), lambda l, t, j: (j[0], l, t, 0)),
                      pl.BlockSpec((3, None, tr, cols), lambda l, t, j: (0, l, t, 0))],
            out_specs=pl.BlockSpec((None, tr, cols), lambda l, t, j: (l, t, 0))),
        out_shape=jax.ShapeDtypeStruct((DEPTH, rows, cols), F32),
        compiler_params=_params(32, 2),
    )(j_idx, p, r)


def _pair_join_halves(fin, fout):
    hin, hout = fin.shape[1], fout.shape[1]

    def body(fin_ref, fout_ref, gin_ref, gout_ref, send_sems, recv_sems, local_sems):
        x, y, c = _coords()
        sib = (x, y, 1 - c)
        din = gin_ref.at[:, pl.ds(c * hin, hin), :]
        dout = gout_ref.at[:, pl.ds(c * hout, hout), :]
        la = pltpu.make_async_copy(fin_ref, din, local_sems.at[0])
        lb = pltpu.make_async_copy(fout_ref, dout, local_sems.at[1])
        ra = pltpu.make_async_remote_copy(src_ref=fin_ref, dst_ref=din, send_sem=send_sems.at[0],
                                          recv_sem=recv_sems.at[0], device_id=sib, device_id_type=MESH)
        rb = pltpu.make_async_remote_copy(src_ref=fout_ref, dst_ref=dout, send_sem=send_sems.at[1],
                                          recv_sem=recv_sems.at[1], device_id=sib, device_id_type=MESH)
        for cp in (la, lb, ra, rb):
            cp.start()
        for cp in (la, lb, ra, rb):
            cp.wait()

    return pl.pallas_call(
        body, name="pair_join_halves", in_specs=[ANY, ANY], out_specs=[ANY, ANY],
        out_shape=[jax.ShapeDtypeStruct((DEPTH, 2 * hin, fin.shape[2]), F32),
                   jax.ShapeDtypeStruct((DEPTH, 2 * hout, fout.shape[2]), F32)],
        scratch_shapes=[pltpu.SemaphoreType.DMA((2,)), pltpu.SemaphoreType.DMA((2,)), pltpu.SemaphoreType.DMA((2,))],
    )(fin, fout)


def _adamw_math(w, g, m, v):
    m = ADAM_B1 * m + (1.0 - ADAM_B1) * g
    v = ADAM_B2 * v + (1.0 - ADAM_B2) * (g * g)
    m_hat = m / (1.0 - ADAM_B1 ** ADAM_STEP)
    v_hat = v / (1.0 - ADAM_B2 ** ADAM_STEP)
    delta = -ADAM_LR * (m_hat / (jnp.sqrt(v_hat) + ADAM_EPS) + ADAM_WD * w)
    return delta, m, v


def _adamw_block(w, g, m, v, name):
    _, rows, cols = w.shape
    tr = min(256, rows)

    def body(w_ref, g_ref, m_ref, v_ref, d_ref, nm_ref, nv_ref):
        d_ref[...], nm_ref[...], nv_ref[...] = _adamw_math(w_ref[...], g_ref[...], m_ref[...], v_ref[...])

    blk = pl.BlockSpec((None, tr, cols), lambda l, t: (l, t, 0))
    return pl.pallas_call(
        body, name=name, grid=(DEPTH, rows // tr), in_specs=[blk] * 4, out_specs=[blk] * 3,
        out_shape=[jax.ShapeDtypeStruct(w.shape, F32)] * 3,
        compiler_params=_params(32, 2),
    )(w, g, m, v)


def _w_ada_grad_adamw(cond_t, dada, w, m, v):
    _, rows, cols = w.shape
    tr = 256

    def body(ct_ref, da_ref, w_ref, m_ref, v_ref, g_ref, d_ref, nm_ref, nv_ref):
        g = jnp.dot(ct_ref[...], da_ref[...].astype(BF16), preferred_element_type=F32)
        g_ref[...] = g
        d_ref[...], nm_ref[...], nv_ref[...] = _adamw_math(w_ref[...], g, m_ref[...], v_ref[...])

    blk = pl.BlockSpec((None, tr, cols), lambda l, t: (l, t, 0))
    return pl.pallas_call(
        body, name="w_ada_grad_adamw", grid=(DEPTH, rows // tr),
        in_specs=[pl.BlockSpec((tr, BLOCK), lambda l, t: (t, 0)), pl.BlockSpec((None, BLOCK, cols), lambda l, t: (l, 0, 0)),
                  blk, blk, blk],
        out_specs=[blk] * 4, out_shape=[jax.ShapeDtypeStruct(w.shape, F32)] * 4,
        compiler_params=_params(32, 2),
    )(cond_t, dada, w, m, v)


def _small_sum_adamw(gathered, w, m, v):
    def body(a_ref, w_ref, m_ref, v_ref, g_ref, d_ref, nm_ref, nv_ref):
        g = a_ref[0]
        for b in range(1, N_DEV):
            g = g + a_ref[b]
        g_ref[...] = g
        d_ref[...], nm_ref[...], nv_ref[...] = _adamw_math(w_ref[...], g, m_ref[...], v_ref[...])

    vm = pl.BlockSpec(memory_space=pltpu.VMEM)
    return pl.pallas_call(
        body, name="small_sum_adamw", in_specs=[vm] * 4, out_specs=[vm] * 4,
        out_shape=[jax.ShapeDtypeStruct(w.shape, F32)] * 4,
        compiler_params=pltpu.CompilerParams(vmem_limit_bytes=48 * MIB),
    )(gathered, w, m, v)


_SMALL = (("w_s", DEPTH * 8 * BLOCK), ("b_s", DEPTH * 8), ("b_ada", DEPTH * 24), ("norm_gain", DEPTH * 8),
          ("q_gain", 1), ("k_gain", 1), ("sink", 1))
_SMALL_ROWS = sum(r for _, r in _SMALL)
_SMALL_PAD = -_SMALL_ROWS % 8


def _pack_small(parts):
    rows = []
    for name, n in _SMALL:
        flat = parts[name].reshape(-1)
        rows.append(jnp.pad(flat, (0, n * 128 - flat.shape[0])).reshape(n, 128))
    rows.append(jnp.zeros((_SMALL_PAD, 128), F32))
    return jnp.concatenate(rows, axis=0)


def _unpack_small(packed, shapes):
    out, r0 = {}, 0
    for name, n in _SMALL:
        size = 1
        for d in shapes[name]:
            size *= d
        out[name] = packed[r0:r0 + n].reshape(-1)[:size].reshape(shapes[name])
        r0 += n
    return out


def kernel(x, c, w_ada, b_ada, norm_gain, w_in, q_gain, k_gain, sink, w_s, b_s, w_out, loss_target, m_w_ada, m_b_ada, m_norm_gain, m_w_in, m_q_gain, m_k_gain, m_sink, m_w_s, m_b_s, m_w_out, v_w_ada, v_b_ada, v_norm_gain, v_w_in, v_q_gain, v_k_gain, v_sink, v_w_s, v_b_s, v_w_out):
    ix, iy, ic = _coords()
    chip = 2 * ix + iy
    chip_idx = jnp.reshape(chip, (1,)).astype(jnp.int32)
    core_idx = jnp.reshape(ic, (1,)).astype(jnp.int32)

    c_all, w_in_blocks, w_out_blocks = _gather_inputs(c, w_in, w_out)
    w_in_p = _permute_w_in(jnp.moveaxis(w_in_blocks, 0, 2).reshape(DEPTH, D_MODEL, D_IN))
    w_out_p = _permute_w_out(jnp.moveaxis(w_out_blocks, 0, 1).reshape(DEPTH, D_MODEL, D_MODEL))

    b_blk = lax.dynamic_slice_in_dim(b_ada, chip * W_ADA_BLK, W_ADA_BLK, axis=1)
    ada_part, cond = _ada_rows(c_all.reshape(N_DEV, D_MODEL), w_ada, b_blk)
    ada = jnp.moveaxis(_exchange_ada(ada_part), 0, 1).reshape(DEPTH, 3 * D_MODEL)

    sq, dx, g = _local_step(x[0], loss_target[0], ada, w_in_p, w_out_p, norm_gain, q_gain, k_gain, sink, w_s, b_s)
    loss = lax.psum(0.5 * sq[0, 0], ("x", "y", "c"))

    gin = jnp.moveaxis(_unpermute_w_in(g["w_in"]).reshape(DEPTH, D_MODEL, N_CHIPS, W_IN_BLK), 2, 0)
    gout = jnp.moveaxis(_unpermute_w_out(g["w_out"]).reshape(DEPTH, N_CHIPS, W_OUT_BLK, D_MODEL), 1, 0)
    rin, rout = _pair_swap_halves(gin, gout)
    pin = _pair_sum(gin, rin, core_idx, "pair_sum_w_in")
    pout = _pair_sum(gout, rout, core_idx, "pair_sum_w_out")
    cin, cout = _chip_scatter(pin, pout)
    fin = _chip_sum(pin, cin, chip_idx, "chip_sum_w_in")
    fout = _chip_sum(pout, cout, chip_idx, "chip_sum_w_out")
    grad_w_in, grad_w_out = _pair_join_halves(fin, fout)

    small_w = dict(w_s=w_s, b_s=b_s, b_ada=b_ada, norm_gain=norm_gain, q_gain=q_gain, k_gain=k_gain, sink=sink)
    small_m = dict(w_s=m_w_s, b_s=m_b_s, b_ada=m_b_ada, norm_gain=m_norm_gain, q_gain=m_q_gain, k_gain=m_k_gain, sink=m_sink)
    small_v = dict(w_s=v_w_s, b_s=v_b_s, b_ada=v_b_ada, norm_gain=v_norm_gain, q_gain=v_q_gain, k_gain=v_k_gain, sink=v_sink)
    small_g = dict(w_s=g["ws"], b_s=g["bs"], b_ada=g["ada"], norm_gain=g["ng"], q_gain=g["qg"], k_gain=g["kg"], sink=g["sink"])
    rows = _SMALL_ROWS + _SMALL_PAD
    gathered = _all_gather_rows(_pack_small(small_g)).reshape(N_DEV, rows, 128)
    packed = _small_sum_adamw(gathered, _pack_small(small_w), _pack_small(small_m), _pack_small(small_v))
    shapes = {k: a.shape for k, a in small_w.items()}
    sg, sd, sm, sv = (_unpack_small(p, shapes) for p in packed)

    r0 = DEPTH * 8 * BLOCK + DEPTH * 8
    dada_all = gathered[:, r0:r0 + DEPTH * 24].reshape(N_DEV, DEPTH, 3 * D_MODEL)
    dada_blk = jnp.moveaxis(lax.dynamic_slice_in_dim(dada_all, chip * W_ADA_BLK, W_ADA_BLK, axis=2), 0, 1)
    pad = BLOCK - N_DEV
    g_ada, d_ada, nm_ada, nv_ada = _w_ada_grad_adamw(
        jnp.pad(cond.T, ((0, 0), (0, pad))).astype(BF16), jnp.pad(dada_blk, ((0, 0), (0, pad), (0, 0))),
        w_ada, m_w_ada, v_w_ada)

    d_in, nm_in, nv_in = _adamw_block(w_in, grad_w_in, m_w_in, v_w_in, "adamw_w_in")
    d_out, nm_out, nv_out = _adamw_block(w_out, grad_w_out, m_w_out, v_w_out, "adamw_w_out")

    def ordered(ada_, small, in_, out_):
        return (ada_, small["b_ada"], small["norm_gain"], in_, small["q_gain"], small["k_gain"], small["sink"],
                small["w_s"], small["b_s"], out_)

    return (loss, dx[None],
            *ordered(g_ada, sg, grad_w_in, grad_w_out),
            *ordered(d_ada, sd, d_in, d_out),
            *ordered(nm_ada, sm, nm_in, nm_out),
            *ordered(nv_ada, sv, nv_in, nv_out))
```

```python
import numpy as np

import jax
import jax.numpy as jnp
from jax import lax
from jax.experimental import pallas as pl
from jax.experimental.pallas import tpu as pltpu

F32 = jnp.float32
BF16 = jnp.bfloat16

D_MODEL = 1024
DEPTH = 2
HEAD_DIM = 64
N_HEADS = 8
BLOCK = 128
D_ATTN = 512
D_KV = 128
D_IN = 2816
N_CHIPS = 4
N_DEV = 8
W_IN_BLK = D_IN // N_CHIPS
W_OUT_BLK = D_MODEL // N_CHIPS
W_ADA_BLK = 3 * D_MODEL // N_CHIPS
CHUNK_ROWS = HEAD_DIM
N_CHUNKS = W_IN_BLK // CHUNK_ROWS
EPS = 1e-6
NEG_INF = -1e30

C_Q, C_K, C_V, C_GA, C_U, C_VG, C_GG = 0, 512, 640, 768, 1280, 1792, 2304

ADAM_LR = 0.001
ADAM_B1 = 0.9
ADAM_B2 = 0.999
ADAM_EPS = 1e-08
ADAM_WD = 0.01
ADAM_STEP = 10

MESH = pl.DeviceIdType.MESH
MIB = 1024 * 1024
ANY = pl.BlockSpec(memory_space=pl.ANY)
VMEM = pl.BlockSpec(memory_space=pltpu.VMEM)

NT_DIMS = (((1,), (1,)), ((), ()))
TN_DIMS = (((0,), (0,)), ((), ()))

_PAIR_ORDER = (0, 4, 1, 5, 2, 6, 3, 7)
_CHUNK_SRC = np.array([
    list(_PAIR_ORDER) + [8, 9, 10],
    [0] + [1 + h for h in _PAIR_ORDER] + [9, 10],
    list(range(N_CHUNKS)),
    list(range(N_CHUNKS)),
], np.int32)
_CHUNK_POS = np.argsort(_CHUNK_SRC, axis=1).astype(np.int32)


def _bias_table():
    i = np.arange(N_HEADS * BLOCK)[:, None]
    j = np.arange(3 * BLOCK)[None, :]
    dist = np.abs(j - BLOCK - (i % BLOCK))
    slope = 2.0 ** -(i // BLOCK + 1.0)
    inner = np.where(dist <= BLOCK, -(slope * dist), NEG_INF)
    first = np.where(j >= BLOCK, inner, NEG_INF)
    last = np.where(j < 2 * BLOCK, inner, NEG_INF)
    return np.stack([first, inner, last]).astype(np.float32)


def _full(shape):
    n = len(shape)
    return pl.BlockSpec(shape, lambda *_: (0,) * n)


def _sds(shape, dtype=F32):
    return jax.ShapeDtypeStruct(shape, dtype)


def _coords():
    return lax.axis_index("x"), lax.axis_index("y"), lax.axis_index("c")


def _other_chips(x, y):
    return [(1 - x, y), (x, 1 - y), (1 - x, 1 - y)]


def _remote(src, dst, sems, dev):
    return pltpu.make_async_remote_copy(src_ref=src, dst_ref=dst, send_sem=sems[0], recv_sem=sems[1],
                                        device_id=dev, device_id_type=MESH)


class _Job:
    def __init__(self, inputs, out_shapes, n_remote, n_local, make):
        self.inputs, self.out_shapes, self.n_remote, self.n_local, self.make = inputs, out_shapes, n_remote, n_local, make


def _job_gather(sources, shapes):
    n = len(sources)

    def make(ins, outs, rsem, lsem):
        x, y, c = _coords()
        j = 2 * x + y
        res = []
        for t, ((_, layer), src, dst) in enumerate(zip(sources, ins, outs)):
            src = src if layer is None else src.at[layer]
            res.append(pltpu.make_async_copy(src, dst.at[j], lsem(t)))
            for k, chip in enumerate(_other_chips(x, y)):
                res.append(_remote(src, dst.at[j], rsem(3 * t + k), (*chip, c)))
        return res

    return _Job([a for a, _ in sources], [_sds((N_CHIPS,) + s, BF16) for s in shapes], 3 * n, n, make)


def _job_swap(g):
    _, rows, cols = g.shape
    half = rows // 2

    def make(ins, outs, rsem, lsem):
        x, y, c = _coords()
        return [_remote(ins[0].at[:, pl.ds((1 - c) * half, half), :], outs[0], rsem(0), (x, y, 1 - c))]

    return _Job([g], [_sds((N_CHIPS, half, cols))], 1, 0, make)


def _job_scatter(p):
    def make(ins, outs, rsem, lsem):
        x, y, c = _coords()
        return [_remote(ins[0].at[2 * chip[0] + chip[1]], outs[0].at[k], rsem(k), (*chip, c))
                for k, chip in enumerate(_other_chips(x, y))]

    return _Job([p], [_sds((3,) + p.shape[1:])], 3, 0, make)


def _job_join(f):
    half, cols = f.shape

    def make(ins, outs, rsem, lsem):
        x, y, c = _coords()
        dst = outs[0].at[pl.ds(c * half, half), :]
        return [pltpu.make_async_copy(ins[0], dst, lsem(0)), _remote(ins[0], dst, rsem(0), (x, y, 1 - c))]

    return _Job([f], [_sds((2 * half, cols))], 1, 1, make)


def _pallas(body, *, name, grid, in_specs, out_specs, out_shape, operands, vmem_mib, jobs=()):
    in_specs, out_specs, out_shape = list(in_specs), list(out_specs), list(out_shape)
    n_in, n_out = len(in_specs), len(out_specs)
    j_in = [a for j in jobs for a in j.inputs]
    j_out = [s for j in jobs for s in j.out_shapes]
    n_rem = max(1, sum(j.n_remote for j in jobs))
    n_loc = max(1, sum(j.n_local for j in jobs))
    scratch = [pltpu.SemaphoreType.DMA((n_rem,)), pltpu.SemaphoreType.DMA((n_rem,)),
               pltpu.SemaphoreType.DMA((n_loc,))] if jobs else []

    def wrapped(*refs):
        ins = refs[:n_in]
        jin = refs[n_in:n_in + len(j_in)]
        outs = refs[n_in + len(j_in):n_in + len(j_in) + n_out]
        jout = refs[n_in + len(j_in) + n_out:n_in + len(j_in) + n_out + len(j_out)]

        def copies():
            send, recv, loc = refs[-3:]
            res, a, b, r, l = [], 0, 0, 0, 0
            for j in jobs:
                res += j.make(jin[a:a + len(j.inputs)], jout[b:b + len(j.out_shapes)],
                              lambda k, r=r: (send.at[r + k], recv.at[r + k]), lambda k, l=l: loc.at[l + k])
                a, b, r, l = a + len(j.inputs), b + len(j.out_shapes), r + j.n_remote, l + j.n_local
            return res

        if jobs:
            first = last = None
            for d, n in enumerate(grid):
                f, e = pl.program_id(d) == 0, pl.program_id(d) == n - 1
                first, last = (f, e) if first is None else (first & f, last & e)

            @pl.when(first)
            def _():
                for cp in copies():
                    cp.start()

        body(*ins, *outs)

        if jobs:
            @pl.when(last)
            def _():
                for cp in copies():
                    cp.wait()

    return pl.pallas_call(
        wrapped, name=name, grid=grid,
        in_specs=in_specs + [ANY] * len(j_in), out_specs=out_specs + [ANY] * len(j_out),
        out_shape=out_shape + j_out, scratch_shapes=scratch,
        compiler_params=pltpu.CompilerParams(dimension_semantics=("arbitrary",) * len(grid),
                                             vmem_limit_bytes=vmem_mib * MIB),
    )(*operands, *j_in)


def _comm(jobs, name):
    j_in = [a for j in jobs for a in j.inputs]
    j_out = [s for j in jobs for s in j.out_shapes]
    n_rem = max(1, sum(j.n_remote for j in jobs))
    n_loc = max(1, sum(j.n_local for j in jobs))

    def body(*refs):
        jin, jout = refs[:len(j_in)], refs[len(j_in):len(j_in) + len(j_out)]
        send, recv, loc = refs[-3:]
        res, a, b, r, l = [], 0, 0, 0, 0
        for j in jobs:
            res += j.make(jin[a:a + len(j.inputs)], jout[b:b + len(j.out_shapes)],
                          lambda k, r=r: (send.at[r + k], recv.at[r + k]), lambda k, l=l: loc.at[l + k])
            a, b, r, l = a + len(j.inputs), b + len(j.out_shapes), r + j.n_remote, l + j.n_local
        for cp in res:
            cp.start()
        for cp in res:
            cp.wait()

    return pl.pallas_call(
        body, name=name, in_specs=[ANY] * len(j_in), out_specs=[ANY] * len(j_out), out_shape=j_out,
        scratch_shapes=[pltpu.SemaphoreType.DMA((n_rem,)), pltpu.SemaphoreType.DMA((n_rem,)),
                        pltpu.SemaphoreType.DMA((n_loc,))],
    )(*j_in)


def _sigmoid(x):
    return 1.0 / (1.0 + jnp.exp(-x))


def _lo_mask(shape):
    return lax.broadcasted_iota(jnp.int32, shape, len(shape) - 1) < HEAD_DIM


def _half_sum(x, lo):
    a = jnp.sum(jnp.where(lo, x, 0.0), axis=-1, keepdims=True)
    b = jnp.sum(jnp.where(lo, 0.0, x), axis=-1, keepdims=True)
    return jnp.where(lo, a, b)


def _half_rms_scale(x, lo):
    return lax.rsqrt(_half_sum(x * x, lo) * (1.0 / HEAD_DIM) + EPS)


def _stack_heads(pairs, lo):
    return jnp.concatenate([jnp.where(lo, t, 0.0) for t in pairs] + [jnp.where(lo, 0.0, t) for t in pairs], axis=0)


def _unstack_pair(stack, p, lo):
    return jnp.where(lo, stack[BLOCK * p:BLOCK * (p + 1)], stack[BLOCK * (4 + p):BLOCK * (5 + p)])


def _attention_probs(q_stack, kn, bias_ref, sink_ref):
    s = lax.dot_general(q_stack, kn, NT_DIMS, preferred_element_type=F32) + bias_ref[...]
    sink = jnp.concatenate([jnp.full((BLOCK, 1), sink_ref[h], F32) for h in range(N_HEADS)], axis=0)
    m = jnp.maximum(jnp.max(s, axis=-1, keepdims=True), sink)
    e = jnp.exp(s - m)
    es = jnp.exp(sink - m)
    inv = 1.0 / (jnp.sum(e, axis=-1, keepdims=True) + es)
    return e * inv, es * inv


def _window(cur_ref, pkv_ref, nkv_ref):
    kwin = jnp.concatenate([pkv_ref[:, 0:D_KV], cur_ref[:, C_K:C_K + D_KV], nkv_ref[:, 0:D_KV]], axis=0)
    vwin = jnp.concatenate([pkv_ref[:, D_KV:2 * D_KV], cur_ref[:, C_V:C_V + D_KV], nkv_ref[:, D_KV:2 * D_KV]], axis=0)
    return kwin, vwin


def _mix_specs(nb):
    assert nb >= 2
    cur = pl.BlockSpec((BLOCK, D_IN), lambda i: (i, 0))
    pkv = pl.BlockSpec((BLOCK, 2 * D_KV), lambda i: (jnp.maximum(i - 1, 0), C_K // (2 * D_KV)))
    nkv = pl.BlockSpec((BLOCK, 2 * D_KV), lambda i: (jnp.minimum(i + 1, nb - 1), C_K // (2 * D_KV)))
    bias = pl.BlockSpec((None, N_HEADS * BLOCK, 3 * BLOCK),
                        lambda i: (jnp.where(i == 0, 0, jnp.where(i == nb - 1, 2, 1)), 0, 0))
    return cur, pkv, nkv, bias


def _proj_fwd(x, ng, scale, shift, wt, name, jobs=()):
    s = x.shape[0]
    ts = min(512, s)

    def body(x_ref, ng_ref, sc_ref, sh_ref, w_ref, o_ref):
        xv = x_ref[...]
        r = lax.rsqrt(jnp.mean(xv * xv, axis=-1, keepdims=True) + EPS)
        h = ((xv * r) * ng_ref[...]) * (1.0 + sc_ref[...]) + sh_ref[...]
        o_ref[...] = lax.dot_general(h.astype(BF16), w_ref[...], NT_DIMS, preferred_element_type=F32)

    vec = _full((1, D_MODEL))
    return _pallas(
        body, name=name, grid=(s // ts,),
        in_specs=[pl.BlockSpec((ts, D_MODEL), lambda i: (i, 0)), vec, vec, vec, _full((D_IN, D_MODEL))],
        out_specs=[pl.BlockSpec((ts, D_IN), lambda i: (i, 0))], out_shape=[_sds((s, D_IN))],
        operands=(x, ng, scale, shift, wt), vmem_mib=48, jobs=jobs)


def _mix_fwd(proj, bias, sink, qg2, kg2, ws, bsp, name, jobs=()):
    s = proj.shape[0]
    nb = s // BLOCK

    def body(sink_ref, cur_ref, pkv_ref, nkv_ref, bias_ref, qg_ref, kg_ref, ws_ref, bsp_ref, y_ref):
        lo = _lo_mask((BLOCK, BLOCK))
        lo_w = _lo_mask((3 * BLOCK, BLOCK))
        kwin, vwin = _window(cur_ref, pkv_ref, nkv_ref)
        kn = ((kwin * _half_rms_scale(kwin, lo_w)) * kg_ref[...]).astype(BF16)
        qn = []
        for p in range(4):
            q = cur_ref[:, C_Q + BLOCK * p:C_Q + BLOCK * (p + 1)]
            qn.append(((q * _half_rms_scale(q, lo)) * qg_ref[...]) * 0.125)
        q_stack = _stack_heads(qn, lo).astype(BF16)
        prob, _ = _attention_probs(q_stack, kn, bias_ref, sink_ref)
        o_stack = jnp.dot(prob.astype(BF16), vwin.astype(BF16), preferred_element_type=F32)
        for p in range(4):
            g = cur_ref[:, C_GA + BLOCK * p:C_GA + BLOCK * (p + 1)]
            y_ref[:, BLOCK * p:BLOCK * (p + 1)] = (_unstack_pair(o_stack, p, lo) * (g * _sigmoid(g))).astype(BF16)
        for p in range(4):
            vg = cur_ref[:, C_VG + BLOCK * p:C_VG + BLOCK * (p + 1)]
            vn = (vg * _half_rms_scale(vg, lo)).astype(BF16)
            sv = jnp.where(lo, jnp.dot(ws_ref[2 * p], vn, preferred_element_type=F32),
                           jnp.dot(ws_ref[2 * p + 1], vn, preferred_element_type=F32)) + bsp_ref[p]
            u = cur_ref[:, C_U + BLOCK * p:C_U + BLOCK * (p + 1)]
            g = cur_ref[:, C_GG + BLOCK * p:C_GG + BLOCK * (p + 1)]
            y_ref[:, D_ATTN + BLOCK * p:D_ATTN + BLOCK * (p + 1)] = ((u * sv) * (g * _sigmoid(g))).astype(BF16)

    cur, pkv, nkv, bias_spec = _mix_specs(nb)
    return _pallas(
        body, name=name, grid=(nb,),
        in_specs=[pl.BlockSpec(memory_space=pltpu.SMEM), cur, pkv, nkv, bias_spec, _full((1, BLOCK)), _full((1, BLOCK)),
                  _full((8, BLOCK, BLOCK)), _full((4, BLOCK, BLOCK))],
        out_specs=[pl.BlockSpec((BLOCK, D_MODEL), lambda i: (i, 0))], out_shape=[_sds((s, D_MODEL), BF16)],
        operands=(sink, proj, proj, proj, bias, qg2, kg2, ws, bsp), vmem_mib=40, jobs=jobs)


def _out_fwd(y, x, gate, w_out, name):
    s = x.shape[0]
    ts = min(512, s)

    def body(y_ref, x_ref, g_ref, w_ref, o_ref):
        o_ref[...] = x_ref[...] + g_ref[...] * jnp.dot(y_ref[...], w_ref[...], preferred_element_type=F32)

    row = pl.BlockSpec((ts, D_MODEL), lambda i: (i, 0))
    return _pallas(
        body, name=name, grid=(s // ts,), in_specs=[row, row, _full((1, D_MODEL)), _full((D_MODEL, D_MODEL))],
        out_specs=[row], out_shape=[_sds((s, D_MODEL))], operands=(y, x, gate, w_out), vmem_mib=32)


def _out_fwd_loss(y, x, gate, w_out, target, name):
    s = x.shape[0]
    ts = min(512, s)

    def body(y_ref, x_ref, g_ref, w_ref, t_ref, dx_ref, sq_ref):
        @pl.when(pl.program_id(0) == 0)
        def _():
            sq_ref[...] = jnp.zeros_like(sq_ref)

        out = x_ref[...] + g_ref[...] * jnp.dot(y_ref[...], w_ref[...], preferred_element_type=F32)
        diff = out - t_ref[...]
        dx_ref[...] = diff * (1.0 / D_MODEL)
        per_token = jnp.sum(diff * diff, axis=-1, keepdims=True) * (1.0 / D_MODEL)
        sq_ref[...] += jnp.sum(per_token, axis=0, keepdims=True)

    row = pl.BlockSpec((ts, D_MODEL), lambda i: (i, 0))
    return _pallas(
        body, name=name, grid=(s // ts,), in_specs=[row, row, _full((1, D_MODEL)), _full((D_MODEL, D_MODEL)), row],
        out_specs=[row, _full((1, 1))], out_shape=[_sds((s, D_MODEL)), _sds((1, 1))],
        operands=(y, x, gate, w_out, target), vmem_mib=40)


def _out_bwd(dxo, y, gate, w_out, name, jobs=()):
    s = dxo.shape[0]
    ts = min(512, s)

    def body(dx_ref, y_ref, g_ref, w_ref, dy_ref, gw_ref, dg_ref):
        @pl.when(pl.program_id(0) == 0)
        def _():
            gw_ref[...] = jnp.zeros_like(gw_ref)
            dg_ref[...] = jnp.zeros_like(dg_ref)

        dx = dx_ref[...]
        do = (dx * g_ref[...]).astype(BF16)
        yv = y_ref[...]
        dy_ref[...] = lax.dot_general(do, w_ref[...], NT_DIMS, preferred_element_type=F32)
        gw_ref[...] += lax.dot_general(yv, do, TN_DIMS, preferred_element_type=F32)
        o = jnp.dot(yv, w_ref[...], preferred_element_type=F32)
        dg_ref[...] += jnp.sum(dx * o, axis=0, keepdims=True)

    row = pl.BlockSpec((ts, D_MODEL), lambda i: (i, 0))
    return _pallas(
        body, name=name, grid=(s // ts,), in_specs=[row, row, _full((1, D_MODEL)), _full((D_MODEL, D_MODEL))],
        out_specs=[row, _full((D_MODEL, D_MODEL)), _full((1, D_MODEL))],
        out_shape=[_sds((s, D_MODEL)), _sds((D_MODEL, D_MODEL)), _sds((1, D_MODEL))],
        operands=(dxo, y, gate, w_out), vmem_mib=48, jobs=jobs)


def _mix_bwd(dy, proj, bias, sink, qg2, kg2, ws, wst, bsp, name, jobs=()):
    s = proj.shape[0]
    nb = s // BLOCK

    def body(sink_ref, dy_ref, cur_ref, pkv_ref, nkv_ref, bias_ref, qg_ref, kg_ref, ws_ref, wst_ref, bsp_ref,
             dpm_ref, p0_ref, p2_ref, dqg_ref, dkg_ref, dsink_ref, dws_ref, dbsp_ref):
        @pl.when(pl.program_id(0) == 0)
        def _():
            dqg_ref[...] = jnp.zeros_like(dqg_ref)
            dkg_ref[...] = jnp.zeros_like(dkg_ref)
            dsink_ref[...] = jnp.zeros_like(dsink_ref)
            dws_ref[...] = jnp.zeros_like(dws_ref)
            dbsp_ref[...] = jnp.zeros_like(dbsp_ref)

        lo = _lo_mask((BLOCK, BLOCK))
        lo_w = _lo_mask((3 * BLOCK, BLOCK))
        qg = qg_ref[...]
        kg = kg_ref[...]

        kwin, vwin = _window(cur_ref, pkv_ref, nkv_ref)
        rk = _half_rms_scale(kwin, lo_w)
        khat = kwin * rk
        kn = (khat * kg).astype(BF16)
        vb = vwin.astype(BF16)
        qhat, rq = [], []
        for p in range(4):
            q = cur_ref[:, C_Q + BLOCK * p:C_Q + BLOCK * (p + 1)]
            r = _half_rms_scale(q, lo)
            rq.append(r)
            qhat.append(q * r)
        q_stack = _stack_heads([(qh * qg) * 0.125 for qh in qhat], lo).astype(BF16)
        prob, psink = _attention_probs(q_stack, kn, bias_ref, sink_ref)
        pb = prob.astype(BF16)
        o_stack = jnp.dot(pb, vb, preferred_element_type=F32)

        dout = []
        for p in range(4):
            g = cur_ref[:, C_GA + BLOCK * p:C_GA + BLOCK * (p + 1)]
            sg = _sigmoid(g)
            dya = dy_ref[:, BLOCK * p:BLOCK * (p + 1)]
            attn = _unstack_pair(o_stack, p, lo)
            dpm_ref[:, C_GA + BLOCK * p:C_GA + BLOCK * (p + 1)] = dya * attn * (sg * (1.0 + g * (1.0 - sg)))
            dout.append(dya * (g * sg))
        do_stack = _stack_heads(dout, lo).astype(BF16)
        dp = lax.dot_general(do_stack, vb, NT_DIMS, preferred_element_type=F32)
        delta = jnp.sum(prob * dp, axis=-1, keepdims=True)
        dsb = (prob * (dp - delta)).astype(BF16)

        csink = -(psink * delta)
        lane_row = lax.broadcasted_iota(jnp.int32, (1, BLOCK), 1)
        dsink = jnp.zeros((1, BLOCK), F32)
        for h in range(N_HEADS):
            tot = jnp.sum(csink[BLOCK * h:BLOCK * (h + 1)], axis=0, keepdims=True)
            dsink = dsink + jnp.where(lane_row == h, tot, 0.0)
        dsink_ref[...] += dsink

        dq_stack = jnp.dot(dsb, kn, preferred_element_type=F32) * 0.125
        dkn = lax.dot_general(dsb, q_stack, TN_DIMS, preferred_element_type=F32)
        dv = lax.dot_general(pb, do_stack, TN_DIMS, preferred_element_type=F32)

        dqg = jnp.zeros((1, BLOCK), F32)
        for p in range(4):
            dqn = _unstack_pair(dq_stack, p, lo)
            qh = qhat[p]
            dqg = dqg + jnp.sum(dqn * qh, axis=0, keepdims=True)
            dqh = dqn * qg
            mean = _half_sum(dqh * qh, lo) * (1.0 / HEAD_DIM)
            dpm_ref[:, C_Q + BLOCK * p:C_Q + BLOCK * (p + 1)] = rq[p] * (dqh - qh * mean)
        dqg = jnp.broadcast_to(dqg, (8, BLOCK))
        dqg_ref[...] += dqg + pltpu.roll(dqg, HEAD_DIM, 1)
        dkg = jnp.broadcast_to(jnp.sum(dkn * khat, axis=0, keepdims=True), (8, BLOCK))
        dkg_ref[...] += dkg + pltpu.roll(dkg, HEAD_DIM, 1)
        dkh = dkn * kg
        dk = rk * (dkh - khat * (_half_sum(dkh * khat, lo_w) * (1.0 / HEAD_DIM)))
        dpm_ref[:, C_K:C_K + D_KV] = dk[BLOCK:2 * BLOCK]
        dpm_ref[:, C_V:C_V + D_KV] = dv[BLOCK:2 * BLOCK]
        p0_ref[:, 0:D_KV] = dk[0:BLOCK]
        p0_ref[:, D_KV:2 * D_KV] = dv[0:BLOCK]
        p2_ref[:, 0:D_KV] = dk[2 * BLOCK:3 * BLOCK]
        p2_ref[:, D_KV:2 * D_KV] = dv[2 * BLOCK:3 * BLOCK]

        for p in range(4):
            vg = cur_ref[:, C_VG + BLOCK * p:C_VG + BLOCK * (p + 1)]
            r = _half_rms_scale(vg, lo)
            vnf = vg * r
            vn = vnf.astype(BF16)
            sv = jnp.where(lo, jnp.dot(ws_ref[2 * p], vn, preferred_element_type=F32),
                           jnp.dot(ws_ref[2 * p + 1], vn, preferred_element_type=F32)) + bsp_ref[p]
            u = cur_ref[:, C_U + BLOCK * p:C_U + BLOCK * (p + 1)]
            g = cur_ref[:, C_GG + BLOCK * p:C_GG + BLOCK * (p + 1)]
            sg = _sigmoid(g)
            dym = dy_ref[:, D_ATTN + BLOCK * p:D_ATTN + BLOCK * (p + 1)]
            dpm_ref[:, C_GG + BLOCK * p:C_GG + BLOCK * (p + 1)] = dym * (u * sv) * (sg * (1.0 + g * (1.0 - sg)))
            dgm = dym * (g * sg)
            dpm_ref[:, C_U + BLOCK * p:C_U + BLOCK * (p + 1)] = dgm * sv
            dsv = dgm * u
            dsv_a = jnp.where(lo, dsv, 0.0)
            dsv_b = jnp.where(lo, 0.0, dsv)
            dws_ref[2 * p] += lax.dot_general(dsv_a.astype(BF16), vn, NT_DIMS, preferred_element_type=F32)
            dws_ref[2 * p + 1] += lax.dot_general(dsv_b.astype(BF16), vn, NT_DIMS, preferred_element_type=F32)
            dbsp_ref[p] += jnp.where(lo, jnp.sum(dsv_a, axis=-1, keepdims=True), jnp.sum(dsv_b, axis=-1, keepdims=True))
            dsvb = dsv.astype(BF16)
            dvn = jnp.where(lo, jnp.dot(wst_ref[2 * p], dsvb, preferred_element_type=F32),
                            jnp.dot(wst_ref[2 * p + 1], dsvb, preferred_element_type=F32))
            mean = _half_sum(dvn * vnf, lo) * (1.0 / HEAD_DIM)
            dpm_ref[:, C_VG + BLOCK * p:C_VG + BLOCK * (p + 1)] = r * (dvn - vnf * mean)

    cur, pkv, nkv, bias_spec = _mix_specs(nb)
    kv_blk = (BLOCK, 2 * D_KV)
    return _pallas(
        body, name=name, grid=(nb,),
        in_specs=[pl.BlockSpec(memory_space=pltpu.SMEM), pl.BlockSpec((BLOCK, D_MODEL), lambda i: (i, 0)),
                  cur, pkv, nkv, bias_spec, _full((1, BLOCK)), _full((1, BLOCK)),
                  _full((8, BLOCK, BLOCK)), _full((8, BLOCK, BLOCK)), _full((4, BLOCK, BLOCK))],
        out_specs=[cur,
                   pl.BlockSpec(kv_blk, lambda i: ((i + nb - 1) % nb, 0)),
                   pl.BlockSpec(kv_blk, lambda i: ((i + 1) % nb, 0)),
                   _full((8, BLOCK)), _full((8, BLOCK)), _full((1, BLOCK)),
                   _full((8, BLOCK, BLOCK)), _full((4, BLOCK, BLOCK))],
        out_shape=[_sds((s, D_IN)), _sds((s, 2 * D_KV)), _sds((s, 2 * D_KV)),
                   _sds((8, BLOCK)), _sds((8, BLOCK)), _sds((1, BLOCK)),
                   _sds((8, BLOCK, BLOCK)), _sds((4, BLOCK, BLOCK))],
        operands=(sink, dy, proj, proj, proj, bias, qg2, kg2, ws, wst, bsp), vmem_mib=48, jobs=jobs)


def _proj_bwd(dpm, p0, p2, x, dxo, ng, scale, shift, wt, name, jobs=()):
    s = x.shape[0]
    ts = min(256, s)

    def body(dpm_ref, p0_ref, p2_ref, x_ref, dxo_ref, ng_ref, sc_ref, sh_ref, w_ref,
             dxi_ref, dpb_ref, h_ref, dsh_ref, dsc_ref, dng_ref):
        @pl.when(pl.program_id(0) == 0)
        def _():
            dsh_ref[...] = jnp.zeros_like(dsh_ref)
            dsc_ref[...] = jnp.zeros_like(dsc_ref)
            dng_ref[...] = jnp.zeros_like(dng_ref)

        dpb_ref[:, 0:C_K] = dpm_ref[:, 0:C_K].astype(BF16)
        dpb_ref[:, C_K:C_GA] = (dpm_ref[:, C_K:C_GA] + p0_ref[...] + p2_ref[...]).astype(BF16)
        dpb_ref[:, C_GA:D_IN] = dpm_ref[:, C_GA:D_IN].astype(BF16)
        dh = jnp.dot(dpb_ref[...], w_ref[...], preferred_element_type=F32)

        xv = x_ref[...]
        r = lax.rsqrt(jnp.mean(xv * xv, axis=-1, keepdims=True) + EPS)
        xn = xv * r
        ngv = ng_ref[...]
        sc1 = 1.0 + sc_ref[...]
        h1 = xn * ngv
        h_ref[...] = (h1 * sc1 + sh_ref[...]).astype(BF16)
        dsh_ref[...] += jnp.sum(dh, axis=0, keepdims=True)
        dsc_ref[...] += jnp.sum(dh * h1, axis=0, keepdims=True)
        dh1 = dh * sc1
        dng_ref[...] += jnp.sum(dh1 * xn, axis=0, keepdims=True)
        dxn = dh1 * ngv
        dxi_ref[...] = r * (dxn - xn * jnp.mean(dxn * xn, axis=-1, keepdims=True)) + dxo_ref[...]

    row = pl.BlockSpec((ts, D_MODEL), lambda i: (i, 0))
    wide = pl.BlockSpec((ts, D_IN), lambda i: (i, 0))
    kv = pl.BlockSpec((ts, 2 * D_KV), lambda i: (i, 0))
    vec = _full((1, D_MODEL))
    return _pallas(
        body, name=name, grid=(s // ts,),
        in_specs=[wide, kv, kv, row, row, vec, vec, vec, _full((D_IN, D_MODEL))],
        out_specs=[row, wide, row, vec, vec, vec],
        out_shape=[_sds((s, D_MODEL)), _sds((s, D_IN), BF16), _sds((s, D_MODEL), BF16),
                   _sds((1, D_MODEL)), _sds((1, D_MODEL)), _sds((1, D_MODEL))],
        operands=(dpm, p0, p2, x, dxo, ng, scale, shift, wt), vmem_mib=48, jobs=jobs)


def _w_in_grad(dpb, h, name, jobs=()):
    s = h.shape[0]
    tk = min(1024, s)
    tn = D_IN // 2

    def body(d_ref, h_ref, o_ref):
        @pl.when(pl.program_id(1) == 0)
        def _():
            o_ref[...] = jnp.zeros_like(o_ref)

        o_ref[...] += lax.dot_general(d_ref[...], h_ref[...], TN_DIMS, preferred_element_type=F32)

    return _pallas(
        body, name=name, grid=(2, s // tk),
        in_specs=[pl.BlockSpec((tk, tn), lambda n, k: (k, n)), pl.BlockSpec((tk, D_MODEL), lambda n, k: (k, 0))],
        out_specs=[pl.BlockSpec((tn, D_MODEL), lambda n, k: (n, 0))], out_shape=[_sds((D_IN, D_MODEL))],
        operands=(dpb, h), vmem_mib=48, jobs=jobs)


def _pair_sum(g, r, c_idx, name):
    _, rows, cols = g.shape
    half = rows // 2

    def body(c_ref, g_ref, r_ref, o_ref):
        o_ref[...] = g_ref[...] + r_ref[...]

    blk = (None, half, cols)
    return pl.pallas_call(
        body, name=name,
        grid_spec=pltpu.PrefetchScalarGridSpec(
            num_scalar_prefetch=1, grid=(N_CHIPS,),
            in_specs=[pl.BlockSpec(blk, lambda j, c: (j, c[0], 0)), pl.BlockSpec(blk, lambda j, c: (j, 0, 0))],
            out_specs=pl.BlockSpec(blk, lambda j, c: (j, 0, 0))),
        out_shape=_sds((N_CHIPS, half, cols)),
        compiler_params=pltpu.CompilerParams(dimension_semantics=("arbitrary",), vmem_limit_bytes=32 * MIB),
    )(c_idx, g, r)


def _chip_sum(p, r, j_idx, name):
    _, rows, cols = p.shape
    tr = rows // 2

    def body(j_ref, p_ref, r_ref, o_ref):
        o_ref[...] = ((p_ref[...] + r_ref[0]) + r_ref[1]) + r_ref[2]

    return pl.pallas_call(
        body, name=name,
        grid_spec=pltpu.PrefetchScalarGridSpec(
            num_scalar_prefetch=1, grid=(2,),
            in_specs=[pl.BlockSpec((None, tr, cols), lambda t, j: (j[0], t, 0)),
                      pl.BlockSpec((3, tr, cols), lambda t, j: (0, t, 0))],
            out_specs=pl.BlockSpec((tr, cols), lambda t, j: (t, 0))),
        out_shape=_sds((rows, cols)),
        compiler_params=pltpu.CompilerParams(dimension_semantics=("arbitrary",), vmem_limit_bytes=32 * MIB),
    )(j_idx, p, r)


def _cast_permute_w_in(wt, src_chunks):
    def body(t_ref, w_ref, o_ref):
        o_ref[...] = w_ref[...].astype(BF16)

    return pl.pallas_call(
        body, name="cast_permute_w_in",
        grid_spec=pltpu.PrefetchScalarGridSpec(
            num_scalar_prefetch=1, grid=(DEPTH, N_CHUNKS),
            in_specs=[pl.BlockSpec((None, CHUNK_ROWS, D_MODEL), lambda l, t, tbl: (l, tbl[t], 0))],
            out_specs=pl.BlockSpec((None, CHUNK_ROWS, D_MODEL), lambda l, t, tbl: (l, t, 0))),
        out_shape=_sds((DEPTH, W_IN_BLK, D_MODEL), BF16),
        compiler_params=pltpu.CompilerParams(dimension_semantics=("arbitrary", "arbitrary")),
    )(src_chunks, wt)


def _gather_inputs(c, w_out, wloc_in):
    half = W_IN_BLK // 2

    def body(c_ref, wout_ref, wloc_ref, call_ref, woutb_ref, w0_ref, send_sems, recv_sems, local_sem):
        x, y, cc = _coords()
        j = 2 * x + y
        b = 2 * j + cc
        sib = (x, y, 1 - cc)
        woutb_ref[...] = wout_ref[...].astype(BF16)
        call_ref[b] = c_ref[...]
        chips = _other_chips(x, y)

        def sems(k):
            return send_sems.at[k], recv_sems.at[k]

        def half_rows(chip_index):
            return w0_ref.at[chip_index, pl.ds(cc * half, half), :]

        mine = pltpu.make_async_copy(wloc_ref.at[0], w0_ref.at[j], local_sem)
        mine.start()
        first = [_remote(wloc_ref.at[0, pl.ds(cc * half, half), :], half_rows(j), sems(k), (*chip, cc))
                 for k, chip in enumerate(chips)]
        k = 3
        rest = []
        for fx in (0, 1):
            for fy in (0, 1):
                for fc in (0, 1):
                    if fx or fy or fc:
                        dev = (1 - x if fx else x, 1 - y if fy else y, 1 - cc if fc else cc)
                        rest.append(_remote(call_ref.at[b], call_ref.at[b], sems(k), dev))
                        k += 1
        for cp in first + rest:
            cp.start()
        passed = []
        for k, chip in enumerate(chips):
            jk = 2 * chip[0] + chip[1]
            first[k].wait_recv()
            passed.append(_remote(half_rows(jk), half_rows(jk), sems(10 + k), sib))
            passed[k].start()
        for cp in first:
            cp.wait_send()
        for cp in rest + passed:
            cp.wait()
        mine.wait()

    return pl.pallas_call(
        body, name="gather_inputs", in_specs=[VMEM, VMEM, ANY], out_specs=[VMEM, VMEM, ANY],
        out_shape=[_sds((N_DEV, 1, D_MODEL)), _sds((DEPTH, W_OUT_BLK, D_MODEL), BF16),
                   _sds((N_CHIPS, W_IN_BLK, D_MODEL), BF16)],
        scratch_shapes=[pltpu.SemaphoreType.DMA((13,)), pltpu.SemaphoreType.DMA((13,)), pltpu.SemaphoreType.DMA],
        compiler_params=pltpu.CompilerParams(vmem_limit_bytes=32 * MIB),
    )(c, w_out, wloc_in)


def _ada_rows(c_all, w_ada, b_blk):
    def body(c_ref, w_ref, b_ref, o_ref, cond_ref):
        cv = c_ref[...]
        cond = (cv * _sigmoid(cv)).astype(BF16)
        cond_ref[...] = cond.astype(F32)
        for l in range(DEPTH):
            o_ref[:, l, :] = jnp.dot(cond, w_ref[l].astype(BF16), preferred_element_type=F32) + b_ref[l:l + 1, :]

    return pl.pallas_call(
        body, name="ada_rows", in_specs=[VMEM, VMEM, VMEM], out_specs=[VMEM, VMEM],
        out_shape=[_sds((N_DEV, DEPTH, W_ADA_BLK)), _sds((N_DEV, D_MODEL))],
        compiler_params=pltpu.CompilerParams(vmem_limit_bytes=32 * MIB),
    )(c_all, w_ada, b_blk)


def _exchange_ada(part):
    def body(part_ref, out_ref, send_sems, recv_sems):
        x, y, cc = _coords()
        j = 2 * x + y
        out_ref[j] = part_ref[2 * j + cc]
        copies = []
        for k, chip in enumerate(_other_chips(x, y)):
            b_dst = 4 * chip[0] + 2 * chip[1] + cc
            copies.append(_remote(part_ref.at[b_dst], out_ref.at[j], (send_sems.at[k], recv_sems.at[k]), (*chip, cc)))
        for cp in copies:
            cp.start()
        for cp in copies:
            cp.wait()

    return pl.pallas_call(
        body, name="exchange_ada", in_specs=[VMEM], out_specs=VMEM,
        out_shape=_sds((N_CHIPS, DEPTH, W_ADA_BLK)),
        scratch_shapes=[pltpu.SemaphoreType.DMA((3,)), pltpu.SemaphoreType.DMA((3,))],
    )(part)


def _all_gather_rows(blk):
    m_per, n = blk.shape

    def body(x_ref, out_ref, send_sems, recv_sems, local_sem):
        x, y, c = _coords()
        me, sibling = (x, y, c), (x, y, 1 - c)
        chips = _other_chips(x, y)

        def rows(px, py, pc):
            return out_ref.at[pl.ds((4 * px + 2 * py + pc) * m_per, m_per), :]

        def copy(k, block, to, src=None):
            return _remote(rows(*block) if src is None else src, rows(*block), (send_sems.at[k], recv_sems.at[k]), to)

        mine = pltpu.make_async_copy(x_ref, rows(*me), local_sem)
        mine.start()
        first = [copy(0, me, sibling, src=x_ref)]
        first += [copy(1 + j, me, (*chip, c), src=x_ref) for j, chip in enumerate(chips)]
        for cp in first:
            cp.start()
        passed = [copy(4 + j, (*chip, c), sibling) for j, chip in enumerate(chips)]
        for j, chip in enumerate(chips):
            copy(1 + j, (*chip, c), me).wait_recv()
            passed[j].start()
        copy(0, sibling, me).wait_recv()
        for j, chip in enumerate(chips):
            copy(4 + j, (*chip, 1 - c), me).wait_recv()
        for cp in first + passed:
            cp.wait_send()
        mine.wait()

    return pl.pallas_call(
        body, name="all_gather_small", in_specs=[VMEM], out_specs=VMEM,
        out_shape=_sds((N_DEV * m_per, n), blk.dtype),
        scratch_shapes=[pltpu.SemaphoreType.DMA((7,)), pltpu.SemaphoreType.DMA((7,)), pltpu.SemaphoreType.DMA],
        compiler_params=pltpu.CompilerParams(vmem_limit_bytes=32 * MIB),
    )(blk)


def _adamw_math(w, g, m, v):
    m = ADAM_B1 * m + (1.0 - ADAM_B1) * g
    v = ADAM_B2 * v + (1.0 - ADAM_B2) * (g * g)
    m_hat = m / (1.0 - ADAM_B1 ** ADAM_STEP)
    v_hat = v / (1.0 - ADAM_B2 ** ADAM_STEP)
    delta = -ADAM_LR * (m_hat / (jnp.sqrt(v_hat) + ADAM_EPS) + ADAM_WD * w)
    return delta, m, v


def _adamw_w_in(w, g0, g1, m, v, pos_chunks):
    def body(t_ref, w_ref, g0_ref, g1_ref, m_ref, v_ref, g_ref, d_ref, nm_ref, nv_ref):
        g = jnp.where(pl.program_id(0) == 0, g0_ref[...], g1_ref[...])
        g_ref[...] = g
        d_ref[...], nm_ref[...], nv_ref[...] = _adamw_math(w_ref[...], g, m_ref[...], v_ref[...])

    nat = pl.BlockSpec((None, CHUNK_ROWS, D_MODEL), lambda l, t, tbl: (l, t, 0))
    per = pl.BlockSpec((CHUNK_ROWS, D_MODEL), lambda l, t, tbl: (tbl[t], 0))
    return pl.pallas_call(
        body, name="adamw_w_in",
        grid_spec=pltpu.PrefetchScalarGridSpec(num_scalar_prefetch=1, grid=(DEPTH, N_CHUNKS),
                                               in_specs=[nat, per, per, nat, nat], out_specs=[nat] * 4),
        out_shape=[_sds(w.shape)] * 4,
        compiler_params=pltpu.CompilerParams(dimension_semantics=("arbitrary", "arbitrary")),
    )(pos_chunks, w, g0, g1, m, v)


def _adamw_w_out(w, g0, g1, m, v):
    def body(w_ref, g0_ref, g1_ref, m_ref, v_ref, g_ref, d_ref, nm_ref, nv_ref):
        g = jnp.where(pl.program_id(0) == 0, g0_ref[...], g1_ref[...])
        g_ref[...] = g
        d_ref[...], nm_ref[...], nv_ref[...] = _adamw_math(w_ref[...], g, m_ref[...], v_ref[...])

    blk = pl.BlockSpec((None, W_OUT_BLK, D_MODEL), lambda l: (l, 0, 0))
    gblk = _full((W_OUT_BLK, D_MODEL))
    return pl.pallas_call(
        body, name="adamw_w_out", grid=(DEPTH,), in_specs=[blk, gblk, gblk, blk, blk], out_specs=[blk] * 4,
        out_shape=[_sds(w.shape)] * 4,
        compiler_params=pltpu.CompilerParams(dimension_semantics=("arbitrary",), vmem_limit_bytes=32 * MIB),
    )(w, g0, g1, m, v)


def _w_ada_grad_adamw(cond_t, dada, w, m, v):
    _, rows, cols = w.shape
    tr = 256

    def body(ct_ref, da_ref, w_ref, m_ref, v_ref, g_ref, d_ref, nm_ref, nv_ref):
        g = jnp.dot(ct_ref[...], da_ref[...].astype(BF16), preferred_element_type=F32)
        g_ref[...] = g
        d_ref[...], nm_ref[...], nv_ref[...] = _adamw_math(w_ref[...], g, m_ref[...], v_ref[...])

    blk = pl.BlockSpec((None, tr, cols), lambda l, t: (l, t, 0))
    return pl.pallas_call(
        body, name="w_ada_grad_adamw", grid=(DEPTH, rows // tr),
        in_specs=[pl.BlockSpec((tr, BLOCK), lambda l, t: (t, 0)), pl.BlockSpec((None, BLOCK, cols), lambda l, t: (l, 0, 0)),
                  blk, blk, blk],
        out_specs=[blk] * 4, out_shape=[_sds(w.shape)] * 4,
        compiler_params=pltpu.CompilerParams(dimension_semantics=("arbitrary", "arbitrary"), vmem_limit_bytes=32 * MIB),
    )(cond_t, dada, w, m, v)


def _small_sum_adamw(gathered, w, m, v):
    def body(a_ref, w_ref, m_ref, v_ref, g_ref, d_ref, nm_ref, nv_ref):
        g = a_ref[0]
        for b in range(1, N_DEV):
            g = g + a_ref[b]
        g_ref[...] = g
        d_ref[...], nm_ref[...], nv_ref[...] = _adamw_math(w_ref[...], g, m_ref[...], v_ref[...])

    return pl.pallas_call(
        body, name="small_sum_adamw", in_specs=[VMEM] * 4, out_specs=[VMEM] * 4, out_shape=[_sds(w.shape)] * 4,
        compiler_params=pltpu.CompilerParams(vmem_limit_bytes=48 * MIB),
    )(gathered, w, m, v)


_SMALL = (("w_s", DEPTH * 8 * BLOCK), ("b_s", DEPTH * 8), ("b_ada", DEPTH * 24), ("norm_gain", DEPTH * 8),
          ("q_gain", 1), ("k_gain", 1), ("sink", 1))
_SMALL_ROWS = sum(r for _, r in _SMALL)
_SMALL_PAD = -_SMALL_ROWS % 8


def _pack_small(parts):
    rows = []
    for name, n in _SMALL:
        flat = parts[name].reshape(-1)
        rows.append(jnp.pad(flat, (0, n * 128 - flat.shape[0])).reshape(n, 128))
    rows.append(jnp.zeros((_SMALL_PAD, 128), F32))
    return jnp.concatenate(rows, axis=0)


def _unpack_small(packed, shapes):
    out, r0 = {}, 0
    for name, n in _SMALL:
        size = 1
        for d in shapes[name]:
            size *= d
        out[name] = packed[r0:r0 + n].reshape(-1)[:size].reshape(shapes[name])
        r0 += n
    return out


def _permute_heads(a, axis):
    shp = a.shape
    a = a.reshape(shp[:axis] + (2, 4, HEAD_DIM) + shp[axis + 1:])
    a = jnp.swapaxes(a, axis, axis + 1)
    return a.reshape(shp)


def _unpermute_heads(a, axis):
    shp = a.shape
    a = a.reshape(shp[:axis] + (4, 2, HEAD_DIM) + shp[axis + 1:])
    a = jnp.swapaxes(a, axis, axis + 1)
    return a.reshape(shp)


def _permute_w_out(w):
    return jnp.concatenate([_permute_heads(w[:D_ATTN], 0), w[D_ATTN:]], axis=0)


def _unpermute_w_out(w):
    return jnp.concatenate([_unpermute_heads(w[:D_ATTN], 0), w[D_ATTN:]], axis=0)


def kernel(x, c, w_ada, b_ada, norm_gain, w_in, q_gain, k_gain, sink, w_s, b_s, w_out, loss_target, m_w_ada, m_b_ada, m_norm_gain, m_w_in, m_q_gain, m_k_gain, m_sink, m_w_s, m_b_s, m_w_out, v_w_ada, v_b_ada, v_norm_gain, v_w_in, v_q_gain, v_k_gain, v_sink, v_w_s, v_b_s, v_w_out):
    ix, iy, ic = _coords()
    chip = 2 * ix + iy
    chip_idx = jnp.reshape(chip, (1,)).astype(jnp.int32)
    core_idx = jnp.reshape(ic, (1,)).astype(jnp.int32)
    src_chunks = lax.dynamic_index_in_dim(jnp.asarray(_CHUNK_SRC), chip, 0, keepdims=False)
    pos_chunks = lax.dynamic_index_in_dim(jnp.asarray(_CHUNK_POS), chip, 0, keepdims=False)
    x0, target = x[0], loss_target[0]

    wt, mt, vt = (jnp.swapaxes(a, 1, 2) for a in (w_in, m_w_in, v_w_in))
    wloc_in = _cast_permute_w_in(wt, src_chunks)
    c_all, wloc_out, w0 = _gather_inputs(c, w_out, wloc_in)
    wts = [w0.reshape(D_IN, D_MODEL), None]

    b_blk = lax.dynamic_slice_in_dim(b_ada, chip * W_ADA_BLK, W_ADA_BLK, axis=1)
    ada_part, cond = _ada_rows(c_all.reshape(N_DEV, D_MODEL), w_ada, b_blk)
    ada = jnp.moveaxis(_exchange_ada(ada_part), 0, 1).reshape(DEPTH, 3 * D_MODEL)
    shift = [ada[l:l + 1, 0:D_MODEL] for l in range(DEPTH)]
    scale = [ada[l:l + 1, D_MODEL:2 * D_MODEL] for l in range(DEPTH)]
    gate = [ada[l:l + 1, 2 * D_MODEL:] for l in range(DEPTH)]
    ng = [norm_gain[l:l + 1] for l in range(DEPTH)]

    qg2 = jnp.concatenate([q_gain, q_gain], axis=-1)
    kg2 = jnp.concatenate([k_gain, k_gain], axis=-1)
    ws_b = w_s.astype(BF16)
    wst_b = jnp.swapaxes(w_s, -1, -2).astype(BF16)
    bsp = jnp.repeat(jnp.swapaxes(b_s.reshape(DEPTH, 4, 2, BLOCK), -1, -2), HEAD_DIM, axis=-1)
    bias = jnp.asarray(_bias_table())

    def mix_args(l):
        return bias, sink[l], qg2[l:l + 1], kg2[l:l + 1], ws_b[l]

    w_out_shape = (W_OUT_BLK, D_MODEL)
    proj0, wo0 = _proj_fwd(x0, ng[0], scale[0], shift[0], wts[0], "proj_fwd_0",
                           jobs=[_job_gather([(wloc_out, 0)], [w_out_shape])])
    y0, w1, wo1 = _mix_fwd(proj0, *mix_args(0), bsp[0], "mix_fwd_0",
                           jobs=[_job_gather([(wloc_in, 1), (wloc_out, 1)], [(W_IN_BLK, D_MODEL), w_out_shape])])
    wts[1] = w1.reshape(D_IN, D_MODEL)
    wos = [_permute_w_out(w.reshape(D_MODEL, D_MODEL)) for w in (wo0, wo1)]
    x1, = _out_fwd(y0, x0, gate[0], wos[0], "out_fwd_0")
    proj1, = _proj_fwd(x1, ng[1], scale[1], shift[1], wts[1], "proj_fwd_1")
    y1, = _mix_fwd(proj1, *mix_args(1), bsp[1], "mix_fwd_1")
    dx2, sq = _out_fwd_loss(y1, x1, gate[1], wos[1], target, "out_fwd_loss_1")
    loss = lax.psum(0.5 * sq[0, 0], ("x", "y", "c"))

    def blocks_out(gw):
        return _unpermute_w_out(gw).reshape(N_CHIPS, W_OUT_BLK, D_MODEL)

    dy1, gwo1, dgate1 = _out_bwd(dx2, y1, gate[1], wos[1], "out_bwd_1")
    go1 = blocks_out(gwo1)
    dpm, p0, p2, dqg1, dkg1, dsink1, dws1, dbsp1, ro1 = _mix_bwd(
        dy1, proj1, *mix_args(1), wst_b[1], bsp[1], "mix_bwd_1", jobs=[_job_swap(go1)])
    po1 = _pair_sum(go1, ro1, core_idx, "pair_sum_w_out_1")
    dx1, dpb, h, dsh1, dsc1, dng1, co1 = _proj_bwd(dpm, p0, p2, x1, dx2, ng[1], scale[1], shift[1], wts[1], "proj_bwd_1",
                                                   jobs=[_job_scatter(po1)])
    fo1 = _chip_sum(po1, co1, chip_idx, "chip_sum_w_out_1")
    gwi1, grad_wo1 = _w_in_grad(dpb, h, "w_in_grad_1", jobs=[_job_join(fo1)])
    gi1 = gwi1.reshape(N_CHIPS, W_IN_BLK, D_MODEL)

    dy0, gwo0, dgate0, ri1 = _out_bwd(dx1, y0, gate[0], wos[0], "out_bwd_0", jobs=[_job_swap(gi1)])
    pi1 = _pair_sum(gi1, ri1, core_idx, "pair_sum_w_in_1")
    go0 = blocks_out(gwo0)
    dpm, p0, p2, dqg0, dkg0, dsink0, dws0, dbsp0, ci1, ro0 = _mix_bwd(
        dy0, proj0, *mix_args(0), wst_b[0], bsp[0], "mix_bwd_0", jobs=[_job_scatter(pi1), _job_swap(go0)])
    fi1 = _chip_sum(pi1, ci1, chip_idx, "chip_sum_w_in_1")
    po0 = _pair_sum(go0, ro0, core_idx, "pair_sum_w_out_0")
    dx0, dpb, h, dsh0, dsc0, dng0, grad_wi1, co0 = _proj_bwd(
        dpm, p0, p2, x0, dx1, ng[0], scale[0], shift[0], wts[0], "proj_bwd_0", jobs=[_job_join(fi1), _job_scatter(po0)])
    fo0 = _chip_sum(po0, co0, chip_idx, "chip_sum_w_out_0")
    gwi0, grad_wo0 = _w_in_grad(dpb, h, "w_in_grad_0", jobs=[_job_join(fo0)])
    gi0 = gwi0.reshape(N_CHIPS, W_IN_BLK, D_MODEL)

    ri0, = _comm([_job_swap(gi0)], "swap_w_in_0")
    pi0 = _pair_sum(gi0, ri0, core_idx, "pair_sum_w_in_0")
    ci0, = _comm([_job_scatter(pi0)], "scatter_w_in_0")
    fi0 = _chip_sum(pi0, ci0, chip_idx, "chip_sum_w_in_0")
    grad_wi0, = _comm([_job_join(fi0)], "join_w_in_0")

    def bs_grad(dbsp):
        return jnp.swapaxes(dbsp[:, :, ::HEAD_DIM], -1, -2).reshape(8, BLOCK)

    small_g = dict(
        w_s=jnp.stack([dws0, dws1]), b_s=jnp.stack([bs_grad(dbsp0), bs_grad(dbsp1)]),
        b_ada=jnp.stack([jnp.concatenate([dsh0, dsc0, dgate0], axis=-1)[0], jnp.concatenate([dsh1, dsc1, dgate1], axis=-1)[0]]),
        norm_gain=jnp.stack([dng0[0], dng1[0]]),
        q_gain=jnp.stack([dqg0[0, :HEAD_DIM], dqg1[0, :HEAD_DIM]]), k_gain=jnp.stack([dkg0[0, :HEAD_DIM], dkg1[0, :HEAD_DIM]]),
        sink=jnp.stack([dsink0[0, :N_HEADS], dsink1[0, :N_HEADS]]))
    small_w = dict(w_s=w_s, b_s=b_s, b_ada=b_ada, norm_gain=norm_gain, q_gain=q_gain, k_gain=k_gain, sink=sink)
    small_m = dict(w_s=m_w_s, b_s=m_b_s, b_ada=m_b_ada, norm_gain=m_norm_gain, q_gain=m_q_gain, k_gain=m_k_gain, sink=m_sink)
    small_v = dict(w_s=v_w_s, b_s=v_b_s, b_ada=v_b_ada, norm_gain=v_norm_gain, q_gain=v_q_gain, k_gain=v_k_gain, sink=v_sink)
    rows = _SMALL_ROWS + _SMALL_PAD
    gathered = _all_gather_rows(_pack_small(small_g)).reshape(N_DEV, rows, 128)
    packed = _small_sum_adamw(gathered, _pack_small(small_w), _pack_small(small_m), _pack_small(small_v))
    shapes = {k: a.shape for k, a in small_w.items()}
    sg, sd, sm, sv = (_unpack_small(p, shapes) for p in packed)

    r0 = DEPTH * 8 * BLOCK + DEPTH * 8
    dada_all = gathered[:, r0:r0 + DEPTH * 24].reshape(N_DEV, DEPTH, 3 * D_MODEL)
    dada_blk = jnp.moveaxis(lax.dynamic_slice_in_dim(dada_all, chip * W_ADA_BLK, W_ADA_BLK, axis=2), 0, 1)
    pad = BLOCK - N_DEV
    ada_out = _w_ada_grad_adamw(
        jnp.pad(cond.T, ((0, 0), (0, pad))).astype(BF16), jnp.pad(dada_blk, ((0, 0), (0, pad), (0, 0))),
        w_ada, m_w_ada, v_w_ada)

    in_out = [jnp.swapaxes(a, 1, 2) for a in _adamw_w_in(wt, grad_wi0, grad_wi1, mt, vt, pos_chunks)]
    out_out = _adamw_w_out(w_out, grad_wo0, grad_wo1, m_w_out, v_w_out)

    def ordered(k):
        small = (sg, sd, sm, sv)[k]
        return (ada_out[k], small["b_ada"], small["norm_gain"], in_out[k], small["q_gain"], small["k_gain"], small["sink"],
                small["w_s"], small["b_s"], out_out[k])

    return (loss, dx0[None], *ordered(0), *ordered(1), *ordered(2), *ordered(3))
```

```python
import numpy as np

import jax
import jax.numpy as jnp
from jax import lax
from jax.experimental import pallas as pl
from jax.experimental.pallas import tpu as pltpu

F32 = jnp.float32
BF16 = jnp.bfloat16

D_MODEL = 1024
DEPTH = 2
HEAD_DIM = 64
N_HEADS = 8
BLOCK = 128
SUB = 2
TILE = SUB * BLOCK
D_ATTN = 512
D_KV = 128
D_IN = 2816
N_CHIPS = 4
N_DEV = 8
W_IN_BLK = D_IN // N_CHIPS
W_OUT_BLK = D_MODEL // N_CHIPS
W_ADA_BLK = 3 * D_MODEL // N_CHIPS
CHUNK_ROWS = HEAD_DIM
N_CHUNKS = W_IN_BLK // CHUNK_ROWS
EPS = 1e-6
NEG_INF = -1e30

C_Q, C_K, C_V, C_GA, C_U, C_VG, C_GG = 0, 512, 640, 768, 1280, 1792, 2304

ADAM_LR = 0.001
ADAM_B1 = 0.9
ADAM_B2 = 0.999
ADAM_EPS = 1e-08
ADAM_WD = 0.01
ADAM_STEP = 10

MESH = pl.DeviceIdType.MESH
MIB = 1024 * 1024
ANY = pl.BlockSpec(memory_space=pl.ANY)
VMEM = pl.BlockSpec(memory_space=pltpu.VMEM)

NT_DIMS = (((1,), (1,)), ((), ()))
TN_DIMS = (((0,), (0,)), ((), ()))

_PAIR_ORDER = (0, 4, 1, 5, 2, 6, 3, 7)
_CHUNK_SRC = np.array([
    list(_PAIR_ORDER) + [8, 9, 10],
    [0] + [1 + h for h in _PAIR_ORDER] + [9, 10],
    list(range(N_CHUNKS)),
    list(range(N_CHUNKS)),
], np.int32)
_CHUNK_POS = np.argsort(_CHUNK_SRC, axis=1).astype(np.int32)


def _bias_table():
    i = np.arange(N_HEADS * BLOCK)[:, None]
    j = np.arange(3 * BLOCK)[None, :]
    dist = np.abs(j - BLOCK - (i % BLOCK))
    slope = 2.0 ** -(i // BLOCK + 1.0)
    inner = np.where(dist <= BLOCK, -(slope * dist), NEG_INF)
    first = np.where(j >= BLOCK, inner, NEG_INF)
    last = np.where(j < 2 * BLOCK, inner, NEG_INF)
    return np.stack([first, inner, last]).astype(np.float32)


def _full(shape):
    n = len(shape)
    return pl.BlockSpec(shape, lambda *_: (0,) * n)


def _sds(shape, dtype=F32):
    return jax.ShapeDtypeStruct(shape, dtype)


def _coords():
    return lax.axis_index("x"), lax.axis_index("y"), lax.axis_index("c")


def _other_chips(x, y):
    return [(1 - x, y), (x, 1 - y), (1 - x, 1 - y)]


def _remote(src, dst, sems, dev):
    return pltpu.make_async_remote_copy(src_ref=src, dst_ref=dst, send_sem=sems[0], recv_sem=sems[1],
                                        device_id=dev, device_id_type=MESH)


class _Job:
    def __init__(self, inputs, out_shapes, n_remote, n_local, make):
        self.inputs, self.out_shapes, self.n_remote, self.n_local, self.make = inputs, out_shapes, n_remote, n_local, make


def _job_gather(sources, shapes):
    n = len(sources)

    def make(ins, outs, rsem, lsem):
        x, y, c = _coords()
        j = 2 * x + y
        res = []
        for t, ((_, layer), src, dst) in enumerate(zip(sources, ins, outs)):
            src = src if layer is None else src.at[layer]
            res.append(pltpu.make_async_copy(src, dst.at[j], lsem(t)))
            for k, chip in enumerate(_other_chips(x, y)):
                res.append(_remote(src, dst.at[j], rsem(3 * t + k), (*chip, c)))
        return res

    return _Job([a for a, _ in sources], [_sds((N_CHIPS,) + s, BF16) for s in shapes], 3 * n, n, make)


def _job_swap(g):
    _, rows, cols = g.shape
    half = rows // 2

    def make(ins, outs, rsem, lsem):
        x, y, c = _coords()
        return [_remote(ins[0].at[:, pl.ds((1 - c) * half, half), :], outs[0], rsem(0), (x, y, 1 - c))]

    return _Job([g], [_sds((N_CHIPS, half, cols))], 1, 0, make)


def _job_scatter(p):
    def make(ins, outs, rsem, lsem):
        x, y, c = _coords()
        return [_remote(ins[0].at[2 * chip[0] + chip[1]], outs[0].at[k], rsem(k), (*chip, c))
                for k, chip in enumerate(_other_chips(x, y))]

    return _Job([p], [_sds((3,) + p.shape[1:])], 3, 0, make)


def _job_join(f):
    half, cols = f.shape

    def make(ins, outs, rsem, lsem):
        x, y, c = _coords()
        dst = outs[0].at[pl.ds(c * half, half), :]
        return [pltpu.make_async_copy(ins[0], dst, lsem(0)), _remote(ins[0], dst, rsem(0), (x, y, 1 - c))]

    return _Job([f], [_sds((2 * half, cols))], 1, 1, make)


def _pallas(body, *, name, grid, in_specs, out_specs, out_shape, operands, vmem_mib, jobs=()):
    in_specs, out_specs, out_shape = list(in_specs), list(out_specs), list(out_shape)
    n_in, n_out = len(in_specs), len(out_specs)
    j_in = [a for j in jobs for a in j.inputs]
    j_out = [s for j in jobs for s in j.out_shapes]
    n_rem = max(1, sum(j.n_remote for j in jobs))
    n_loc = max(1, sum(j.n_local for j in jobs))
    scratch = [pltpu.SemaphoreType.DMA((n_rem,)), pltpu.SemaphoreType.DMA((n_rem,)),
               pltpu.SemaphoreType.DMA((n_loc,))] if jobs else []

    def wrapped(*refs):
        ins = refs[:n_in]
        jin = refs[n_in:n_in + len(j_in)]
        outs = refs[n_in + len(j_in):n_in + len(j_in) + n_out]
        jout = refs[n_in + len(j_in) + n_out:n_in + len(j_in) + n_out + len(j_out)]

        def copies():
            send, recv, loc = refs[-3:]
            res, a, b, r, l = [], 0, 0, 0, 0
            for j in jobs:
                res += j.make(jin[a:a + len(j.inputs)], jout[b:b + len(j.out_shapes)],
                              lambda k, r=r: (send.at[r + k], recv.at[r + k]), lambda k, l=l: loc.at[l + k])
                a, b, r, l = a + len(j.inputs), b + len(j.out_shapes), r + j.n_remote, l + j.n_local
            return res

        if jobs:
            first = last = None
            for d, n in enumerate(grid):
                f, e = pl.program_id(d) == 0, pl.program_id(d) == n - 1
                first, last = (f, e) if first is None else (first & f, last & e)

            @pl.when(first)
            def _():
                for cp in copies():
                    cp.start()

        body(*ins, *outs)

        if jobs:
            @pl.when(last)
            def _():
                for cp in copies():
                    cp.wait()

    return pl.pallas_call(
        wrapped, name=name, grid=grid,
        in_specs=in_specs + [ANY] * len(j_in), out_specs=out_specs + [ANY] * len(j_out),
        out_shape=out_shape + j_out, scratch_shapes=scratch,
        compiler_params=pltpu.CompilerParams(dimension_semantics=("arbitrary",) * len(grid),
                                             vmem_limit_bytes=vmem_mib * MIB),
    )(*operands, *j_in)


def _comm(jobs, name):
    j_in = [a for j in jobs for a in j.inputs]
    j_out = [s for j in jobs for s in j.out_shapes]
    n_rem = max(1, sum(j.n_remote for j in jobs))
    n_loc = max(1, sum(j.n_local for j in jobs))

    def body(*refs):
        jin, jout = refs[:len(j_in)], refs[len(j_in):len(j_in) + len(j_out)]
        send, recv, loc = refs[-3:]
        res, a, b, r, l = [], 0, 0, 0, 0
        for j in jobs:
            res += j.make(jin[a:a + len(j.inputs)], jout[b:b + len(j.out_shapes)],
                          lambda k, r=r: (send.at[r + k], recv.at[r + k]), lambda k, l=l: loc.at[l + k])
            a, b, r, l = a + len(j.inputs), b + len(j.out_shapes), r + j.n_remote, l + j.n_local
        for cp in res:
            cp.start()
        for cp in res:
            cp.wait()

    return pl.pallas_call(
        body, name=name, in_specs=[ANY] * len(j_in), out_specs=[ANY] * len(j_out), out_shape=j_out,
        scratch_shapes=[pltpu.SemaphoreType.DMA((n_rem,)), pltpu.SemaphoreType.DMA((n_rem,)),
                        pltpu.SemaphoreType.DMA((n_loc,))],
    )(*j_in)


def _sigmoid(x):
    return 1.0 / (1.0 + jnp.exp(-x))


def _lo_mask(shape):
    return lax.broadcasted_iota(jnp.int32, shape, len(shape) - 1) < HEAD_DIM


def _half_sum(x, lo):
    a = jnp.sum(jnp.where(lo, x, 0.0), axis=-1, keepdims=True)
    b = jnp.sum(jnp.where(lo, 0.0, x), axis=-1, keepdims=True)
    return jnp.where(lo, a, b)


def _half_rms_scale(x, lo):
    return lax.rsqrt(_half_sum(x * x, lo) * (1.0 / HEAD_DIM) + EPS)


def _stack_heads(pairs, lo):
    return jnp.concatenate([jnp.where(lo, t, 0.0) for t in pairs] + [jnp.where(lo, 0.0, t) for t in pairs], axis=0)


def _unstack_pair(stack, p, lo):
    return jnp.where(lo, stack[BLOCK * p:BLOCK * (p + 1)], stack[BLOCK * (4 + p):BLOCK * (5 + p)])


def _attention_probs(q_stack, kn, bias_ref, sink_ref):
    s = lax.dot_general(q_stack, kn, NT_DIMS, preferred_element_type=F32) + bias_ref[...]
    sink = jnp.concatenate([jnp.full((BLOCK, 1), sink_ref[h], F32) for h in range(N_HEADS)], axis=0)
    m = jnp.maximum(jnp.max(s, axis=-1, keepdims=True), sink)
    e = jnp.exp(s - m)
    es = jnp.exp(sink - m)
    inv = 1.0 / (jnp.sum(e, axis=-1, keepdims=True) + es)
    return e * inv, es * inv


def _kv_rows(cur_ref, pkv_ref, nkv_ref):
    k = jnp.concatenate([pkv_ref[:, 0:D_KV], cur_ref[:, C_K:C_K + D_KV], nkv_ref[:, 0:D_KV]], axis=0)
    v = jnp.concatenate([pkv_ref[:, D_KV:2 * D_KV], cur_ref[:, C_V:C_V + D_KV], nkv_ref[:, D_KV:2 * D_KV]], axis=0)
    return k, v


def _overlap_add(parts):
    blocks = []
    for j in range(SUB + 2):
        terms = [parts[b][BLOCK * (j - b):BLOCK * (j - b + 1)] for b in range(SUB) if 0 <= j - b <= 2]
        total = terms[0]
        for t in terms[1:]:
            total = total + t
        blocks.append(total)
    return jnp.concatenate(blocks, axis=0)


def _mix_specs(nt):
    cur = pl.BlockSpec((TILE, D_IN), lambda i: (i, 0))
    kv_col = C_K // (2 * D_KV)
    pkv = pl.BlockSpec((BLOCK, 2 * D_KV), lambda i: (jnp.maximum(i * SUB - 1, 0), kv_col))
    nkv = pl.BlockSpec((BLOCK, 2 * D_KV), lambda i: (jnp.minimum((i + 1) * SUB, nt * SUB - 1), kv_col))
    table = (None, N_HEADS * BLOCK, 3 * BLOCK)
    first = pl.BlockSpec(table, lambda i: (jnp.where(i == 0, 0, 1), 0, 0))
    inner = pl.BlockSpec(table, lambda i: (1, 0, 0))
    last = pl.BlockSpec(table, lambda i: (jnp.where(i == nt - 1, 2, 1), 0, 0))
    return cur, pkv, nkv, [first] + [inner] * (SUB - 2) + [last]


def _proj_fwd(x, ng, scale, shift, wt, name, jobs=()):
    s = x.shape[0]
    ts = min(512, s)

    def body(x_ref, ng_ref, sc_ref, sh_ref, w_ref, o_ref):
        xv = x_ref[...]
        r = lax.rsqrt(jnp.mean(xv * xv, axis=-1, keepdims=True) + EPS)
        h = ((xv * r) * ng_ref[...]) * (1.0 + sc_ref[...]) + sh_ref[...]
        o_ref[...] = lax.dot_general(h.astype(BF16), w_ref[...], NT_DIMS, preferred_element_type=F32)

    vec = _full((1, D_MODEL))
    return _pallas(
        body, name=name, grid=(s // ts,),
        in_specs=[pl.BlockSpec((ts, D_MODEL), lambda i: (i, 0)), vec, vec, vec, _full((D_IN, D_MODEL))],
        out_specs=[pl.BlockSpec((ts, D_IN), lambda i: (i, 0))], out_shape=[_sds((s, D_IN))],
        operands=(x, ng, scale, shift, wt), vmem_mib=48, jobs=jobs)


def _mix_fwd(proj, bias, sink, qg2, kg2, ws, bsp, name, jobs=()):
    s = proj.shape[0]
    nt = s // TILE

    def body(sink_ref, cur_ref, pkv_ref, nkv_ref, *rest):
        bias_refs = rest[:SUB]
        qg_ref, kg_ref, ws_ref, bsp_ref, y_ref = rest[SUB:]
        lo = _lo_mask((BLOCK, BLOCK))
        lo_kv = _lo_mask((TILE + 2 * BLOCK, BLOCK))
        k_all, v_all = _kv_rows(cur_ref, pkv_ref, nkv_ref)
        kn_all = ((k_all * _half_rms_scale(k_all, lo_kv)) * kg_ref[...]).astype(BF16)
        vb_all = v_all.astype(BF16)
        for b in range(SUB):
            rows = slice(BLOCK * b, BLOCK * (b + 1))
            window = slice(BLOCK * b, BLOCK * (b + 3))
            qn = []
            for p in range(4):
                q = cur_ref[rows, C_Q + BLOCK * p:C_Q + BLOCK * (p + 1)]
                qn.append(((q * _half_rms_scale(q, lo)) * qg_ref[...]) * 0.125)
            q_stack = _stack_heads(qn, lo).astype(BF16)
            prob, _ = _attention_probs(q_stack, kn_all[window], bias_refs[b], sink_ref)
            o_stack = jnp.dot(prob.astype(BF16), vb_all[window], preferred_element_type=F32)
            for p in range(4):
                g = cur_ref[rows, C_GA + BLOCK * p:C_GA + BLOCK * (p + 1)]
                y_ref[rows, BLOCK * p:BLOCK * (p + 1)] = (_unstack_pair(o_stack, p, lo) * (g * _sigmoid(g))).astype(BF16)
            for p in range(4):
                vg = cur_ref[rows, C_VG + BLOCK * p:C_VG + BLOCK * (p + 1)]
                vn = (vg * _half_rms_scale(vg, lo)).astype(BF16)
                sv = jnp.where(lo, jnp.dot(ws_ref[2 * p], vn, preferred_element_type=F32),
                               jnp.dot(ws_ref[2 * p + 1], vn, preferred_element_type=F32)) + bsp_ref[p]
                u = cur_ref[rows, C_U + BLOCK * p:C_U + BLOCK * (p + 1)]
                g = cur_ref[rows, C_GG + BLOCK * p:C_GG + BLOCK * (p + 1)]
                y_ref[rows, D_ATTN + BLOCK * p:D_ATTN + BLOCK * (p + 1)] = ((u * sv) * (g * _sigmoid(g))).astype(BF16)

    cur, pkv, nkv, bias_specs = _mix_specs(nt)
    return _pallas(
        body, name=name, grid=(nt,),
        in_specs=[pl.BlockSpec(memory_space=pltpu.SMEM), cur, pkv, nkv, *bias_specs, _full((1, BLOCK)), _full((1, BLOCK)),
                  _full((8, BLOCK, BLOCK)), _full((4, BLOCK, BLOCK))],
        out_specs=[pl.BlockSpec((TILE, D_MODEL), lambda i: (i, 0))], out_shape=[_sds((s, D_MODEL), BF16)],
        operands=(sink, proj, proj, proj, *([bias] * SUB), qg2, kg2, ws, bsp), vmem_mib=48, jobs=jobs)


def _out_fwd(y, x, gate, w_out, name):
    s = x.shape[0]
    ts = min(512, s)

    def body(y_ref, x_ref, g_ref, w_ref, o_ref):
        o_ref[...] = x_ref[...] + g_ref[...] * jnp.dot(y_ref[...], w_ref[...], preferred_element_type=F32)

    row = pl.BlockSpec((ts, D_MODEL), lambda i: (i, 0))
    return _pallas(
        body, name=name, grid=(s // ts,), in_specs=[row, row, _full((1, D_MODEL)), _full((D_MODEL, D_MODEL))],
        out_specs=[row], out_shape=[_sds((s, D_MODEL))], operands=(y, x, gate, w_out), vmem_mib=32)


def _out_fwd_loss(y, x, gate, w_out, target, name):
    s = x.shape[0]
    ts = min(512, s)

    def body(y_ref, x_ref, g_ref, w_ref, t_ref, dx_ref, sq_ref):
        @pl.when(pl.program_id(0) == 0)
        def _():
            sq_ref[...] = jnp.zeros_like(sq_ref)

        out = x_ref[...] + g_ref[...] * jnp.dot(y_ref[...], w_ref[...], preferred_element_type=F32)
        diff = out - t_ref[...]
        dx_ref[...] = diff * (1.0 / D_MODEL)
        per_token = jnp.sum(diff * diff, axis=-1, keepdims=True) * (1.0 / D_MODEL)
        sq_ref[...] += jnp.sum(per_token, axis=0, keepdims=True)

    row = pl.BlockSpec((ts, D_MODEL), lambda i: (i, 0))
    return _pallas(
        body, name=name, grid=(s // ts,), in_specs=[row, row, _full((1, D_MODEL)), _full((D_MODEL, D_MODEL)), row],
        out_specs=[row, _full((1, 1))], out_shape=[_sds((s, D_MODEL)), _sds((1, 1))],
        operands=(y, x, gate, w_out, target), vmem_mib=40)


def _out_bwd(dxo, y, gate, w_out, name, jobs=()):
    s = dxo.shape[0]
    ts = min(512, s)

    def body(dx_ref, y_ref, g_ref, w_ref, dy_ref, gw_ref, dg_ref):
        @pl.when(pl.program_id(0) == 0)
        def _():
            gw_ref[...] = jnp.zeros_like(gw_ref)
            dg_ref[...] = jnp.zeros_like(dg_ref)

        dx = dx_ref[...]
        do = (dx * g_ref[...]).astype(BF16)
        yv = y_ref[...]
        dy_ref[...] = lax.dot_general(do, w_ref[...], NT_DIMS, preferred_element_type=F32)
        gw_ref[...] += lax.dot_general(yv, do, TN_DIMS, preferred_element_type=F32)
        o = jnp.dot(yv, w_ref[...], preferred_element_type=F32)
        dg_ref[...] += jnp.sum(dx * o, axis=0, keepdims=True)

    row = pl.BlockSpec((ts, D_MODEL), lambda i: (i, 0))
    return _pallas(
        body, name=name, grid=(s // ts,), in_specs=[row, row, _full((1, D_MODEL)), _full((D_MODEL, D_MODEL))],
        out_specs=[row, _full((D_MODEL, D_MODEL)), _full((1, D_MODEL))],
        out_shape=[_sds((s, D_MODEL)), _sds((D_MODEL, D_MODEL)), _sds((1, D_MODEL))],
        operands=(dxo, y, gate, w_out), vmem_mib=48, jobs=jobs)


def _mix_bwd(dy, proj, bias, sink, qg2, kg2, ws, wst, bsp, name, jobs=()):
    s = proj.shape[0]
    nt = s // TILE

    def body(sink_ref, dy_ref, cur_ref, pkv_ref, nkv_ref, *rest):
        bias_refs = rest[:SUB]
        (qg_ref, kg_ref, ws_ref, wst_ref, bsp_ref,
         dpm_ref, p0_ref, p2_ref, dqg_ref, dkg_ref, dsink_ref, dws_ref, dbsp_ref) = rest[SUB:]

        @pl.when(pl.program_id(0) == 0)
        def _():
            dqg_ref[...] = jnp.zeros_like(dqg_ref)
            dkg_ref[...] = jnp.zeros_like(dkg_ref)
            dsink_ref[...] = jnp.zeros_like(dsink_ref)
            dws_ref[...] = jnp.zeros_like(dws_ref)
            dbsp_ref[...] = jnp.zeros_like(dbsp_ref)

        lo = _lo_mask((BLOCK, BLOCK))
        lo_kv = _lo_mask((TILE + 2 * BLOCK, BLOCK))
        lane_row = lax.broadcasted_iota(jnp.int32, (1, BLOCK), 1)
        qg = qg_ref[...]
        kg = kg_ref[...]

        k_all, v_all = _kv_rows(cur_ref, pkv_ref, nkv_ref)
        rk = _half_rms_scale(k_all, lo_kv)
        khat = k_all * rk
        kn_all = (khat * kg).astype(BF16)
        vb_all = v_all.astype(BF16)

        dkn_parts, dv_parts = [], []
        dsink = jnp.zeros((1, BLOCK), F32)
        dqg = jnp.zeros((1, BLOCK), F32)
        for b in range(SUB):
            rows = slice(BLOCK * b, BLOCK * (b + 1))
            window = slice(BLOCK * b, BLOCK * (b + 3))
            kn, vb = kn_all[window], vb_all[window]

            qhat, rq = [], []
            for p in range(4):
                q = cur_ref[rows, C_Q + BLOCK * p:C_Q + BLOCK * (p + 1)]
                r = _half_rms_scale(q, lo)
                rq.append(r)
                qhat.append(q * r)
            q_stack = _stack_heads([(qh * qg) * 0.125 for qh in qhat], lo).astype(BF16)
            prob, psink = _attention_probs(q_stack, kn, bias_refs[b], sink_ref)
            pb = prob.astype(BF16)
            o_stack = jnp.dot(pb, vb, preferred_element_type=F32)

            dout = []
            for p in range(4):
                g = cur_ref[rows, C_GA + BLOCK * p:C_GA + BLOCK * (p + 1)]
                sg = _sigmoid(g)
                dya = dy_ref[rows, BLOCK * p:BLOCK * (p + 1)]
                attn = _unstack_pair(o_stack, p, lo)
                dpm_ref[rows, C_GA + BLOCK * p:C_GA + BLOCK * (p + 1)] = dya * attn * (sg * (1.0 + g * (1.0 - sg)))
                dout.append(dya * (g * sg))
            do_stack = _stack_heads(dout, lo).astype(BF16)
            dp = lax.dot_general(do_stack, vb, NT_DIMS, preferred_element_type=F32)
            delta = jnp.sum(prob * dp, axis=-1, keepdims=True)
            dsb = (prob * (dp - delta)).astype(BF16)

            csink = -(psink * delta)
            for h in range(N_HEADS):
                tot = jnp.sum(csink[BLOCK * h:BLOCK * (h + 1)], axis=0, keepdims=True)
                dsink = dsink + jnp.where(lane_row == h, tot, 0.0)

            dq_stack = jnp.dot(dsb, kn, preferred_element_type=F32) * 0.125
            dkn_parts.append(lax.dot_general(dsb, q_stack, TN_DIMS, preferred_element_type=F32))
            dv_parts.append(lax.dot_general(pb, do_stack, TN_DIMS, preferred_element_type=F32))

            for p in range(4):
                dqn = _unstack_pair(dq_stack, p, lo)
                qh = qhat[p]
                dqg = dqg + jnp.sum(dqn * qh, axis=0, keepdims=True)
                dqh = dqn * qg
                mean = _half_sum(dqh * qh, lo) * (1.0 / HEAD_DIM)
                dpm_ref[rows, C_Q + BLOCK * p:C_Q + BLOCK * (p + 1)] = rq[p] * (dqh - qh * mean)

            for p in range(4):
                vg = cur_ref[rows, C_VG + BLOCK * p:C_VG + BLOCK * (p + 1)]
                r = _half_rms_scale(vg, lo)
                vnf = vg * r
                vn = vnf.astype(BF16)
                sv = jnp.where(lo, jnp.dot(ws_ref[2 * p], vn, preferred_element_type=F32),
                               jnp.dot(ws_ref[2 * p + 1], vn, preferred_element_type=F32)) + bsp_ref[p]
                u = cur_ref[rows, C_U + BLOCK * p:C_U + BLOCK * (p + 1)]
                g = cur_ref[rows, C_GG + BLOCK * p:C_GG + BLOCK * (p + 1)]
                sg = _sigmoid(g)
                dym = dy_ref[rows, D_ATTN + BLOCK * p:D_ATTN + BLOCK * (p + 1)]
                dpm_ref[rows, C_GG + BLOCK * p:C_GG + BLOCK * (p + 1)] = dym * (u * sv) * (sg * (1.0 + g * (1.0 - sg)))
                dgm = dym * (g * sg)
                dpm_ref[rows, C_U + BLOCK * p:C_U + BLOCK * (p + 1)] = dgm * sv
                dsv = dgm * u
                dsv_a = jnp.where(lo, dsv, 0.0)
                dsv_b = jnp.where(lo, 0.0, dsv)
                dws_ref[2 * p] += lax.dot_general(dsv_a.astype(BF16), vn, NT_DIMS, preferred_element_type=F32)
                dws_ref[2 * p + 1] += lax.dot_general(dsv_b.astype(BF16), vn, NT_DIMS, preferred_element_type=F32)
                dbsp_ref[p] += jnp.where(lo, jnp.sum(dsv_a, axis=-1, keepdims=True), jnp.sum(dsv_b, axis=-1, keepdims=True))
                dsvb = dsv.astype(BF16)
                dvn = jnp.where(lo, jnp.dot(wst_ref[2 * p], dsvb, preferred_element_type=F32),
                                jnp.dot(wst_ref[2 * p + 1], dsvb, preferred_element_type=F32))
                mean = _half_sum(dvn * vnf, lo) * (1.0 / HEAD_DIM)
                dpm_ref[rows, C_VG + BLOCK * p:C_VG + BLOCK * (p + 1)] = r * (dvn - vnf * mean)

        dsink_ref[...] += dsink
        dqg = jnp.broadcast_to(dqg, (8, BLOCK))
        dqg_ref[...] += dqg + pltpu.roll(dqg, HEAD_DIM, 1)

        dkn = _overlap_add(dkn_parts)
        dv = _overlap_add(dv_parts)
        dkg = jnp.broadcast_to(jnp.sum(dkn * khat, axis=0, keepdims=True), (8, BLOCK))
        dkg_ref[...] += dkg + pltpu.roll(dkg, HEAD_DIM, 1)
        dkh = dkn * kg
        dk = rk * (dkh - khat * (_half_sum(dkh * khat, lo_kv) * (1.0 / HEAD_DIM)))
        dpm_ref[:, C_K:C_K + D_KV] = dk[BLOCK:BLOCK + TILE]
        dpm_ref[:, C_V:C_V + D_KV] = dv[BLOCK:BLOCK + TILE]
        p0_ref[:, 0:D_KV] = dk[0:BLOCK]
        p0_ref[:, D_KV:2 * D_KV] = dv[0:BLOCK]
        p2_ref[:, 0:D_KV] = dk[BLOCK + TILE:]
        p2_ref[:, D_KV:2 * D_KV] = dv[BLOCK + TILE:]

    cur, pkv, nkv, bias_specs = _mix_specs(nt)
    kv_blk = (BLOCK, 2 * D_KV)
    return _pallas(
        body, name=name, grid=(nt,),
        in_specs=[pl.BlockSpec(memory_space=pltpu.SMEM), pl.BlockSpec((TILE, D_MODEL), lambda i: (i, 0)),
                  cur, pkv, nkv, *bias_specs, _full((1, BLOCK)), _full((1, BLOCK)),
                  _full((8, BLOCK, BLOCK)), _full((8, BLOCK, BLOCK)), _full((4, BLOCK, BLOCK))],
        out_specs=[cur,
                   pl.BlockSpec(kv_blk, lambda i: ((i + nt - 1) % nt, 0)),
                   pl.BlockSpec(kv_blk, lambda i: ((i + 1) % nt, 0)),
                   _full((8, BLOCK)), _full((8, BLOCK)), _full((1, BLOCK)),
                   _full((8, BLOCK, BLOCK)), _full((4, BLOCK, BLOCK))],
        out_shape=[_sds((s, D_IN)), _sds((nt * BLOCK, 2 * D_KV)), _sds((nt * BLOCK, 2 * D_KV)),
                   _sds((8, BLOCK)), _sds((8, BLOCK)), _sds((1, BLOCK)),
                   _sds((8, BLOCK, BLOCK)), _sds((4, BLOCK, BLOCK))],
        operands=(sink, dy, proj, proj, proj, *([bias] * SUB), qg2, kg2, ws, wst, bsp), vmem_mib=56, jobs=jobs)


def _proj_bwd(dpm, p0, p2, x, dxo, ng, scale, shift, wt, name, jobs=()):
    s = x.shape[0]
    ts = TILE

    def body(dpm_ref, p0_ref, p2_ref, x_ref, dxo_ref, ng_ref, sc_ref, sh_ref, w_ref,
             dxi_ref, dpb_ref, h_ref, dsh_ref, dsc_ref, dng_ref):
        @pl.when(pl.program_id(0) == 0)
        def _():
            dsh_ref[...] = jnp.zeros_like(dsh_ref)
            dsc_ref[...] = jnp.zeros_like(dsc_ref)
            dng_ref[...] = jnp.zeros_like(dng_ref)

        dpb_ref[:, 0:C_K] = dpm_ref[:, 0:C_K].astype(BF16)
        dpb_ref[0:BLOCK, C_K:C_GA] = (dpm_ref[0:BLOCK, C_K:C_GA] + p2_ref[...]).astype(BF16)
        if SUB > 2:
            dpb_ref[BLOCK:ts - BLOCK, C_K:C_GA] = dpm_ref[BLOCK:ts - BLOCK, C_K:C_GA].astype(BF16)
        dpb_ref[ts - BLOCK:ts, C_K:C_GA] = (dpm_ref[ts - BLOCK:ts, C_K:C_GA] + p0_ref[...]).astype(BF16)
        dpb_ref[:, C_GA:D_IN] = dpm_ref[:, C_GA:D_IN].astype(BF16)
        dh = jnp.dot(dpb_ref[...], w_ref[...], preferred_element_type=F32)

        xv = x_ref[...]
        r = lax.rsqrt(jnp.mean(xv * xv, axis=-1, keepdims=True) + EPS)
        xn = xv * r
        ngv = ng_ref[...]
        sc1 = 1.0 + sc_ref[...]
        h1 = xn * ngv
        h_ref[...] = (h1 * sc1 + sh_ref[...]).astype(BF16)
        dsh_ref[...] += jnp.sum(dh, axis=0, keepdims=True)
        dsc_ref[...] += jnp.sum(dh * h1, axis=0, keepdims=True)
        dh1 = dh * sc1
        dng_ref[...] += jnp.sum(dh1 * xn, axis=0, keepdims=True)
        dxn = dh1 * ngv
        dxi_ref[...] = r * (dxn - xn * jnp.mean(dxn * xn, axis=-1, keepdims=True)) + dxo_ref[...]

    row = pl.BlockSpec((ts, D_MODEL), lambda i: (i, 0))
    wide = pl.BlockSpec((ts, D_IN), lambda i: (i, 0))
    kv = pl.BlockSpec((BLOCK, 2 * D_KV), lambda i: (i, 0))
    vec = _full((1, D_MODEL))
    return _pallas(
        body, name=name, grid=(s // ts,),
        in_specs=[wide, kv, kv, row, row, vec, vec, vec, _full((D_IN, D_MODEL))],
        out_specs=[row, wide, row, vec, vec, vec],
        out_shape=[_sds((s, D_MODEL)), _sds((s, D_IN), BF16), _sds((s, D_MODEL), BF16),
                   _sds((1, D_MODEL)), _sds((1, D_MODEL)), _sds((1, D_MODEL))],
        operands=(dpm, p0, p2, x, dxo, ng, scale, shift, wt), vmem_mib=48, jobs=jobs)


def _w_in_grad(dpb, h, name, jobs=()):
    s = h.shape[0]
    tk = min(1024, s)
    tn = D_IN // 2

    def body(d_ref, h_ref, o_ref):
        @pl.when(pl.program_id(1) == 0)
        def _():
            o_ref[...] = jnp.zeros_like(o_ref)

        o_ref[...] += lax.dot_general(d_ref[...], h_ref[...], TN_DIMS, preferred_element_type=F32)

    return _pallas(
        body, name=name, grid=(2, s // tk),
        in_specs=[pl.BlockSpec((tk, tn), lambda n, k: (k, n)), pl.BlockSpec((tk, D_MODEL), lambda n, k: (k, 0))],
        out_specs=[pl.BlockSpec((tn, D_MODEL), lambda n, k: (n, 0))], out_shape=[_sds((D_IN, D_MODEL))],
        operands=(dpb, h), vmem_mib=48, jobs=jobs)


def _pair_sum(g, r, c_idx, name):
    _, rows, cols = g.shape
    half = rows // 2

    def body(c_ref, g_ref, r_ref, o_ref):
        o_ref[...] = g_ref[...] + r_ref[...]

    blk = (None, half, cols)
    return pl.pallas_call(
        body, name=name,
        grid_spec=pltpu.PrefetchScalarGridSpec(
            num_scalar_prefetch=1, grid=(N_CHIPS,),
            in_specs=[pl.BlockSpec(blk, lambda j, c: (j, c[0], 0)), pl.BlockSpec(blk, lambda j, c: (j, 0, 0))],
            out_specs=pl.BlockSpec(blk, lambda j, c: (j, 0, 0))),
        out_shape=_sds((N_CHIPS, half, cols)),
        compiler_params=pltpu.CompilerParams(dimension_semantics=("arbitrary",), vmem_limit_bytes=32 * MIB),
    )(c_idx, g, r)


def _chip_sum(p, r, j_idx, name):
    _, rows, cols = p.shape
    tr = rows // 2

    def body(j_ref, p_ref, r_ref, o_ref):
        o_ref[...] = ((p_ref[...] + r_ref[0]) + r_ref[1]) + r_ref[2]

    return pl.pallas_call(
        body, name=name,
        grid_spec=pltpu.PrefetchScalarGridSpec(
            num_scalar_prefetch=1, grid=(2,),
            in_specs=[pl.BlockSpec((None, tr, cols), lambda t, j: (j[0], t, 0)),
                      pl.BlockSpec((3, tr, cols), lambda t, j: (0, t, 0))],
            out_specs=pl.BlockSpec((tr, cols), lambda t, j: (t, 0))),
        out_shape=_sds((rows, cols)),
        compiler_params=pltpu.CompilerParams(dimension_semantics=("arbitrary",), vmem_limit_bytes=32 * MIB),
    )(j_idx, p, r)


def _cast_permute_w_in(wt, src_chunks):
    def body(t_ref, w_ref, o_ref):
        o_ref[...] = w_ref[...].astype(BF16)

    return pl.pallas_call(
        body, name="cast_permute_w_in",
        grid_spec=pltpu.PrefetchScalarGridSpec(
            num_scalar_prefetch=1, grid=(DEPTH, N_CHUNKS),
            in_specs=[pl.BlockSpec((None, CHUNK_ROWS, D_MODEL), lambda l, t, tbl: (l, tbl[t], 0))],
            out_specs=pl.BlockSpec((None, CHUNK_ROWS, D_MODEL), lambda l, t, tbl: (l, t, 0))),
        out_shape=_sds((DEPTH, W_IN_BLK, D_MODEL), BF16),
        compiler_params=pltpu.CompilerParams(dimension_semantics=("arbitrary", "arbitrary")),
    )(src_chunks, wt)


def _gather_inputs(c, w_out, wloc_in):
    half = W_IN_BLK // 2

    def body(c_ref, wout_ref, wloc_ref, call_ref, woutb_ref, w0_ref, send_sems, recv_sems, local_sem):
        x, y, cc = _coords()
        j = 2 * x + y
        b = 2 * j + cc
        sib = (x, y, 1 - cc)
        woutb_ref[...] = wout_ref[...].astype(BF16)
        call_ref[b] = c_ref[...]
        chips = _other_chips(x, y)

        def sems(k):
            return send_sems.at[k], recv_sems.at[k]

        def half_rows(chip_index):
            return w0_ref.at[chip_index, pl.ds(cc * half, half), :]

        mine = pltpu.make_async_copy(wloc_ref.at[0], w0_ref.at[j], local_sem)
        mine.start()
        first = [_remote(wloc_ref.at[0, pl.ds(cc * half, half), :], half_rows(j), sems(k), (*chip, cc))
                 for k, chip in enumerate(chips)]
        k = 3
        rest = []
        for fx in (0, 1):
            for fy in (0, 1):
                for fc in (0, 1):
                    if fx or fy or fc:
                        dev = (1 - x if fx else x, 1 - y if fy else y, 1 - cc if fc else cc)
                        rest.append(_remote(call_ref.at[b], call_ref.at[b], sems(k), dev))
                        k += 1
        for cp in first + rest:
            cp.start()
        passed = []
        for k, chip in enumerate(chips):
            jk = 2 * chip[0] + chip[1]
            first[k].wait_recv()
            passed.append(_remote(half_rows(jk), half_rows(jk), sems(10 + k), sib))
            passed[k].start()
        for cp in first:
            cp.wait_send()
        for cp in rest + passed:
            cp.wait()
        mine.wait()

    return pl.pallas_call(
        body, name="gather_inputs", in_specs=[VMEM, VMEM, ANY], out_specs=[VMEM, VMEM, ANY],
        out_shape=[_sds((N_DEV, 1, D_MODEL)), _sds((DEPTH, W_OUT_BLK, D_MODEL), BF16),
                   _sds((N_CHIPS, W_IN_BLK, D_MODEL), BF16)],
        scratch_shapes=[pltpu.SemaphoreType.DMA((13,)), pltpu.SemaphoreType.DMA((13,)), pltpu.SemaphoreType.DMA],
        compiler_params=pltpu.CompilerParams(vmem_limit_bytes=32 * MIB),
    )(c, w_out, wloc_in)


def _ada_rows(c_all, w_ada, b_blk):
    def body(c_ref, w_ref, b_ref, o_ref, cond_ref):
        cv = c_ref[...]
        cond = (cv * _sigmoid(cv)).astype(BF16)
        cond_ref[...] = cond.astype(F32)
        for l in range(DEPTH):
            o_ref[:, l, :] = jnp.dot(cond, w_ref[l].astype(BF16), preferred_element_type=F32) + b_ref[l:l + 1, :]

    return pl.pallas_call(
        body, name="ada_rows", in_specs=[VMEM, VMEM, VMEM], out_specs=[VMEM, VMEM],
        out_shape=[_sds((N_DEV, DEPTH, W_ADA_BLK)), _sds((N_DEV, D_MODEL))],
        compiler_params=pltpu.CompilerParams(vmem_limit_bytes=32 * MIB),
    )(c_all, w_ada, b_blk)


def _exchange_ada(part):
    def body(part_ref, out_ref, send_sems, recv_sems):
        x, y, cc = _coords()
        j = 2 * x + y
        out_ref[j] = part_ref[2 * j + cc]
        copies = []
        for k, chip in enumerate(_other_chips(x, y)):
            b_dst = 4 * chip[0] + 2 * chip[1] + cc
            copies.append(_remote(part_ref.at[b_dst], out_ref.at[j], (send_sems.at[k], recv_sems.at[k]), (*chip, cc)))
        for cp in copies:
            cp.start()
        for cp in copies:
            cp.wait()

    return pl.pallas_call(
        body, name="exchange_ada", in_specs=[VMEM], out_specs=VMEM,
        out_shape=_sds((N_CHIPS, DEPTH, W_ADA_BLK)),
        scratch_shapes=[pltpu.SemaphoreType.DMA((3,)), pltpu.SemaphoreType.DMA((3,))],
    )(part)


def _all_gather_rows(blk):
    m_per, n = blk.shape

    def body(x_ref, out_ref, send_sems, recv_sems, local_sem):
        x, y, c = _coords()
        me, sibling = (x, y, c), (x, y, 1 - c)
        chips = _other_chips(x, y)

        def rows(px, py, pc):
            return out_ref.at[pl.ds((4 * px + 2 * py + pc) * m_per, m_per), :]

        def copy(k, block, to, src=None):
            return _remote(rows(*block) if src is None else src, rows(*block), (send_sems.at[k], recv_sems.at[k]), to)

        mine = pltpu.make_async_copy(x_ref, rows(*me), local_sem)
        mine.start()
        first = [copy(0, me, sibling, src=x_ref)]
        first += [copy(1 + j, me, (*chip, c), src=x_ref) for j, chip in enumerate(chips)]
        for cp in first:
            cp.start()
        passed = [copy(4 + j, (*chip, c), sibling) for j, chip in enumerate(chips)]
        for j, chip in enumerate(chips):
            copy(1 + j, (*chip, c), me).wait_recv()
            passed[j].start()
        copy(0, sibling, me).wait_recv()
        for j, chip in enumerate(chips):
            copy(4 + j, (*chip, 1 - c), me).wait_recv()
        for cp in first + passed:
            cp.wait_send()
        mine.wait()

    return pl.pallas_call(
        body, name="all_gather_small", in_specs=[VMEM], out_specs=VMEM,
        out_shape=_sds((N_DEV * m_per, n), blk.dtype),
        scratch_shapes=[pltpu.SemaphoreType.DMA((7,)), pltpu.SemaphoreType.DMA((7,)), pltpu.SemaphoreType.DMA],
        compiler_params=pltpu.CompilerParams(vmem_limit_bytes=32 * MIB),
    )(blk)


def _adamw_math(w, g, m, v):
    m = ADAM_B1 * m + (1.0 - ADAM_B1) * g
    v = ADAM_B2 * v + (1.0 - ADAM_B2) * (g * g)
    m_hat = m / (1.0 - ADAM_B1 ** ADAM_STEP)
    v_hat = v / (1.0 - ADAM_B2 ** ADAM_STEP)
    delta = -ADAM_LR * (m_hat / (jnp.sqrt(v_hat) + ADAM_EPS) + ADAM_WD * w)
    return delta, m, v


def _adamw_w_in(w, g0, g1, m, v, pos_chunks):
    def body(t_ref, w_ref, g0_ref, g1_ref, m_ref, v_ref, g_ref, d_ref, nm_ref, nv_ref):
        g = jnp.where(pl.program_id(0) == 0, g0_ref[...], g1_ref[...])
        g_ref[...] = g
        d_ref[...], nm_ref[...], nv_ref[...] = _adamw_math(w_ref[...], g, m_ref[...], v_ref[...])

    nat = pl.BlockSpec((None, CHUNK_ROWS, D_MODEL), lambda l, t, tbl: (l, t, 0))
    per = pl.BlockSpec((CHUNK_ROWS, D_MODEL), lambda l, t, tbl: (tbl[t], 0))
    return pl.pallas_call(
        body, name="adamw_w_in",
        grid_spec=pltpu.PrefetchScalarGridSpec(num_scalar_prefetch=1, grid=(DEPTH, N_CHUNKS),
                                               in_specs=[nat, per, per, nat, nat], out_specs=[nat] * 4),
        out_shape=[_sds(w.shape)] * 4,
        compiler_params=pltpu.CompilerParams(dimension_semantics=("arbitrary", "arbitrary")),
    )(pos_chunks, w, g0, g1, m, v)


def _adamw_w_out(w, g0, g1, m, v):
    def body(w_ref, g0_ref, g1_ref, m_ref, v_ref, g_ref, d_ref, nm_ref, nv_ref):
        g = jnp.where(pl.program_id(0) == 0, g0_ref[...], g1_ref[...])
        g_ref[...] = g
        d_ref[...], nm_ref[...], nv_ref[...] = _adamw_math(w_ref[...], g, m_ref[...], v_ref[...])

    blk = pl.BlockSpec((None, W_OUT_BLK, D_MODEL), lambda l: (l, 0, 0))
    gblk = _full((W_OUT_BLK, D_MODEL))
    return pl.pallas_call(
        body, name="adamw_w_out", grid=(DEPTH,), in_specs=[blk, gblk, gblk, blk, blk], out_specs=[blk] * 4,
        out_shape=[_sds(w.shape)] * 4,
        compiler_params=pltpu.CompilerParams(dimension_semantics=("arbitrary",), vmem_limit_bytes=32 * MIB),
    )(w, g0, g1, m, v)


def _w_ada_grad_adamw(cond_t, dada, w, m, v):
    _, rows, cols = w.shape
    tr = 256

    def body(ct_ref, da_ref, w_ref, m_ref, v_ref, g_ref, d_ref, nm_ref, nv_ref):
        g = jnp.dot(ct_ref[...], da_ref[...].astype(BF16), preferred_element_type=F32)
        g_ref[...] = g
        d_ref[...], nm_ref[...], nv_ref[...] = _adamw_math(w_ref[...], g, m_ref[...], v_ref[...])

    blk = pl.BlockSpec((None, tr, cols), lambda l, t: (l, t, 0))
    return pl.pallas_call(
        body, name="w_ada_grad_adamw", grid=(DEPTH, rows // tr),
        in_specs=[pl.BlockSpec((tr, BLOCK), lambda l, t: (t, 0)), pl.BlockSpec((None, BLOCK, cols), lambda l, t: (l, 0, 0)),
                  blk, blk, blk],
        out_specs=[blk] * 4, out_shape=[_sds(w.shape)] * 4,
        compiler_params=pltpu.CompilerParams(dimension_semantics=("arbitrary", "arbitrary"), vmem_limit_bytes=32 * MIB),
    )(cond_t, dada, w, m, v)


def _small_sum_adamw(gathered, w, m, v):
    def body(a_ref, w_ref, m_ref, v_ref, g_ref, d_ref, nm_ref, nv_ref):
        g = a_ref[0]
        for b in range(1, N_DEV):
            g = g + a_ref[b]
        g_ref[...] = g
        d_ref[...], nm_ref[...], nv_ref[...] = _adamw_math(w_ref[...], g, m_ref[...], v_ref[...])

    return pl.pallas_call(
        body, name="small_sum_adamw", in_specs=[VMEM] * 4, out_specs=[VMEM] * 4, out_shape=[_sds(w.shape)] * 4,
        compiler_params=pltpu.CompilerParams(vmem_limit_bytes=48 * MIB),
    )(gathered, w, m, v)


_SMALL = (("w_s", DEPTH * 8 * BLOCK), ("b_s", DEPTH * 8), ("b_ada", DEPTH * 24), ("norm_gain", DEPTH * 8),
          ("q_gain", 1), ("k_gain", 1), ("sink", 1))
_SMALL_ROWS = sum(r for _, r in _SMALL)
_SMALL_PAD = -_SMALL_ROWS % 8


def _pack_small(parts):
    rows = []
    for name, n in _SMALL:
        flat = parts[name].reshape(-1)
        rows.append(jnp.pad(flat, (0, n * 128 - flat.shape[0])).reshape(n, 128))
    rows.append(jnp.zeros((_SMALL_PAD, 128), F32))
    return jnp.concatenate(rows, axis=0)


def _unpack_small(packed, shapes):
    out, r0 = {}, 0
    for name, n in _SMALL:
        size = 1
        for d in shapes[name]:
            size *= d
        out[name] = packed[r0:r0 + n].reshape(-1)[:size].reshape(shapes[name])
        r0 += n
    return out


def _permute_heads(a, axis):
    shp = a.shape
    a = a.reshape(shp[:axis] + (2, 4, HEAD_DIM) + shp[axis + 1:])
    a = jnp.swapaxes(a, axis, axis + 1)
    return a.reshape(shp)


def _unpermute_heads(a, axis):
    shp = a.shape
    a = a.reshape(shp[:axis] + (4, 2, HEAD_DIM) + shp[axis + 1:])
    a = jnp.swapaxes(a, axis, axis + 1)
    return a.reshape(shp)


def _permute_w_out(w):
    return jnp.concatenate([_permute_heads(w[:D_ATTN], 0), w[D_ATTN:]], axis=0)


def _unpermute_w_out(w):
    return jnp.concatenate([_unpermute_heads(w[:D_ATTN], 0), w[D_ATTN:]], axis=0)


def kernel(x, c, w_ada, b_ada, norm_gain, w_in, q_gain, k_gain, sink, w_s, b_s, w_out, loss_target, m_w_ada, m_b_ada, m_norm_gain, m_w_in, m_q_gain, m_k_gain, m_sink, m_w_s, m_b_s, m_w_out, v_w_ada, v_b_ada, v_norm_gain, v_w_in, v_q_gain, v_k_gain, v_sink, v_w_s, v_b_s, v_w_out):
    ix, iy, ic = _coords()
    chip = 2 * ix + iy
    chip_idx = jnp.reshape(chip, (1,)).astype(jnp.int32)
    core_idx = jnp.reshape(ic, (1,)).astype(jnp.int32)
    src_chunks = lax.dynamic_index_in_dim(jnp.asarray(_CHUNK_SRC), chip, 0, keepdims=False)
    pos_chunks = lax.dynamic_index_in_dim(jnp.asarray(_CHUNK_POS), chip, 0, keepdims=False)
    x0, target = x[0], loss_target[0]

    wt, mt, vt = (jnp.swapaxes(a, 1, 2) for a in (w_in, m_w_in, v_w_in))
    wloc_in = _cast_permute_w_in(wt, src_chunks)
    c_all, wloc_out, w0 = _gather_inputs(c, w_out, wloc_in)
    wts = [w0.reshape(D_IN, D_MODEL), None]

    b_blk = lax.dynamic_slice_in_dim(b_ada, chip * W_ADA_BLK, W_ADA_BLK, axis=1)
    ada_part, cond = _ada_rows(c_all.reshape(N_DEV, D_MODEL), w_ada, b_blk)
    ada = jnp.moveaxis(_exchange_ada(ada_part), 0, 1).reshape(DEPTH, 3 * D_MODEL)
    shift = [ada[l:l + 1, 0:D_MODEL] for l in range(DEPTH)]
    scale = [ada[l:l + 1, D_MODEL:2 * D_MODEL] for l in range(DEPTH)]
    gate = [ada[l:l + 1, 2 * D_MODEL:] for l in range(DEPTH)]
    ng = [norm_gain[l:l + 1] for l in range(DEPTH)]

    qg2 = jnp.concatenate([q_gain, q_gain], axis=-1)
    kg2 = jnp.concatenate([k_gain, k_gain], axis=-1)
    ws_b = w_s.astype(BF16)
    wst_b = jnp.swapaxes(w_s, -1, -2).astype(BF16)
    bsp = jnp.repeat(jnp.swapaxes(b_s.reshape(DEPTH, 4, 2, BLOCK), -1, -2), HEAD_DIM, axis=-1)
    bias = jnp.asarray(_bias_table())

    def mix_args(l):
        return bias, sink[l], qg2[l:l + 1], kg2[l:l + 1], ws_b[l]

    w_out_shape = (W_OUT_BLK, D_MODEL)
    proj0, wo0 = _proj_fwd(x0, ng[0], scale[0], shift[0], wts[0], "proj_fwd_0",
                           jobs=[_job_gather([(wloc_out, 0)], [w_out_shape])])
    y0, w1, wo1 = _mix_fwd(proj0, *mix_args(0), bsp[0], "mix_fwd_0",
                           jobs=[_job_gather([(wloc_in, 1), (wloc_out, 1)], [(W_IN_BLK, D_MODEL), w_out_shape])])
    wts[1] = w1.reshape(D_IN, D_MODEL)
    wos = [_permute_w_out(w.reshape(D_MODEL, D_MODEL)) for w in (wo0, wo1)]
    x1, = _out_fwd(y0, x0, gate[0], wos[0], "out_fwd_0")
    proj1, = _proj_fwd(x1, ng[1], scale[1], shift[1], wts[1], "proj_fwd_1")
    y1, = _mix_fwd(proj1, *mix_args(1), bsp[1], "mix_fwd_1")
    dx2, sq = _out_fwd_loss(y1, x1, gate[1], wos[1], target, "out_fwd_loss_1")
    loss = lax.psum(0.5 * sq[0, 0], ("x", "y", "c"))

    def blocks_out(gw):
        return _unpermute_w_out(gw).reshape(N_CHIPS, W_OUT_BLK, D_MODEL)

    dy1, gwo1, dgate1 = _out_bwd(dx2, y1, gate[1], wos[1], "out_bwd_1")
    go1 = blocks_out(gwo1)
    dpm, p0, p2, dqg1, dkg1, dsink1, dws1, dbsp1, ro1 = _mix_bwd(
        dy1, proj1, *mix_args(1), wst_b[1], bsp[1], "mix_bwd_1", jobs=[_job_swap(go1)])
    po1 = _pair_sum(go1, ro1, core_idx, "pair_sum_w_out_1")
    dx1, dpb, h, dsh1, dsc1, dng1, co1 = _proj_bwd(dpm, p0, p2, x1, dx2, ng[1], scale[1], shift[1], wts[1], "proj_bwd_1",
                                                   jobs=[_job_scatter(po1)])
    fo1 = _chip_sum(po1, co1, chip_idx, "chip_sum_w_out_1")
    gwi1, grad_wo1 = _w_in_grad(dpb, h, "w_in_grad_1", jobs=[_job_join(fo1)])
    gi1 = gwi1.reshape(N_CHIPS, W_IN_BLK, D_MODEL)

    dy0, gwo0, dgate0, ri1 = _out_bwd(dx1, y0, gate[0], wos[0], "out_bwd_0", jobs=[_job_swap(gi1)])
    pi1 = _pair_sum(gi1, ri1, core_idx, "pair_sum_w_in_1")
    go0 = blocks_out(gwo0)
    dpm, p0, p2, dqg0, dkg0, dsink0, dws0, dbsp0, ci1, ro0 = _mix_bwd(
        dy0, proj0, *mix_args(0), wst_b[0], bsp[0], "mix_bwd_0", jobs=[_job_scatter(pi1), _job_swap(go0)])
    fi1 = _chip_sum(pi1, ci1, chip_idx, "chip_sum_w_in_1")
    po0 = _pair_sum(go0, ro0, core_idx, "pair_sum_w_out_0")
    dx0, dpb, h, dsh0, dsc0, dng0, grad_wi1, co0 = _proj_bwd(
        dpm, p0, p2, x0, dx1, ng[0], scale[0], shift[0], wts[0], "proj_bwd_0", jobs=[_job_join(fi1), _job_scatter(po0)])
    fo0 = _chip_sum(po0, co0, chip_idx, "chip_sum_w_out_0")
    gwi0, grad_wo0 = _w_in_grad(dpb, h, "w_in_grad_0", jobs=[_job_join(fo0)])
    gi0 = gwi0.reshape(N_CHIPS, W_IN_BLK, D_MODEL)

    ri0, = _comm([_job_swap(gi0)], "swap_w_in_0")
    pi0 = _pair_sum(gi0, ri0, core_idx, "pair_sum_w_in_0")
    ci0, = _comm([_job_scatter(pi0)], "scatter_w_in_0")
    fi0 = _chip_sum(pi0, ci0, chip_idx, "chip_sum_w_in_0")
    grad_wi0, = _comm([_job_join(fi0)], "join_w_in_0")

    def bs_grad(dbsp):
        return jnp.swapaxes(dbsp[:, :, ::HEAD_DIM], -1, -2).reshape(8, BLOCK)

    small_g = dict(
        w_s=jnp.stack([dws0, dws1]), b_s=jnp.stack([bs_grad(dbsp0), bs_grad(dbsp1)]),
        b_ada=jnp.stack([jnp.concatenate([dsh0, dsc0, dgate0], axis=-1)[0], jnp.concatenate([dsh1, dsc1, dgate1], axis=-1)[0]]),
        norm_gain=jnp.stack([dng0[0], dng1[0]]),
        q_gain=jnp.stack([dqg0[0, :HEAD_DIM], dqg1[0, :HEAD_DIM]]), k_gain=jnp.stack([dkg0[0, :HEAD_DIM], dkg1[0, :HEAD_DIM]]),
        sink=jnp.stack([dsink0[0, :N_HEADS], dsink1[0, :N_HEADS]]))
    small_w = dict(w_s=w_s, b_s=b_s, b_ada=b_ada, norm_gain=norm_gain, q_gain=q_gain, k_gain=k_gain, sink=sink)
    small_m = dict(w_s=m_w_s, b_s=m_b_s, b_ada=m_b_ada, norm_gain=m_norm_gain, q_gain=m_q_gain, k_gain=m_k_gain, sink=m_sink)
    small_v = dict(w_s=v_w_s, b_s=v_b_s, b_ada=v_b_ada, norm_gain=v_norm_gain, q_gain=v_q_gain, k_gain=v_k_gain, sink=v_sink)
    rows = _SMALL_ROWS + _SMALL_PAD
    gathered = _all_gather_rows(_pack_small(small_g)).reshape(N_DEV, rows, 128)
    packed = _small_sum_adamw(gathered, _pack_small(small_w), _pack_small(small_m), _pack_small(small_v))
    shapes = {k: a.shape for k, a in small_w.items()}
    sg, sd, sm, sv = (_unpack_small(p, shapes) for p in packed)

    r0 = DEPTH * 8 * BLOCK + DEPTH * 8
    dada_all = gathered[:, r0:r0 + DEPTH * 24].reshape(N_DEV, DEPTH, 3 * D_MODEL)
    dada_blk = jnp.moveaxis(lax.dynamic_slice_in_dim(dada_all, chip * W_ADA_BLK, W_ADA_BLK, axis=2), 0, 1)
    pad = BLOCK - N_DEV
    ada_out = _w_ada_grad_adamw(
        jnp.pad(cond.T, ((0, 0), (0, pad))).astype(BF16), jnp.pad(dada_blk, ((0, 0), (0, pad), (0, 0))),
        w_ada, m_w_ada, v_w_ada)

    in_out = [jnp.swapaxes(a, 1, 2) for a in _adamw_w_in(wt, grad_wi0, grad_wi1, mt, vt, pos_chunks)]
    out_out = _adamw_w_out(w_out, grad_wo0, grad_wo1, m_w_out, v_w_out)

    def ordered(k):
        small = (sg, sd, sm, sv)[k]
        return (ada_out[k], small["b_ada"], small["norm_gain"], in_out[k], small["q_gain"], small["k_gain"], small["sink"],
                small["w_s"], small["b_s"], out_out[k])

    return (loss, dx0[None], *ordered(0), *ordered(1), *ordered(2), *ordered(3))
```

```python
import numpy as np

import jax
import jax.numpy as jnp
from jax import lax
from jax.experimental import pallas as pl
from jax.experimental.pallas import tpu as pltpu

F32 = jnp.float32
BF16 = jnp.bfloat16

D_MODEL = 1024
DEPTH = 2
HEAD_DIM = 64
N_HEADS = 8
BLOCK = 128
SUB = 2
TILE = SUB * BLOCK
D_ATTN = 512
D_KV = 128
D_IN = 2816
N_CHIPS = 4
N_DEV = 8
W_IN_BLK = D_IN // N_CHIPS
W_OUT_BLK = D_MODEL // N_CHIPS
W_ADA_BLK = 3 * D_MODEL // N_CHIPS
CHUNK_ROWS = HEAD_DIM
N_CHUNKS = W_IN_BLK // CHUNK_ROWS
EPS = 1e-6
NEG_INF = -1e30

C_Q, C_K, C_V, C_GA, C_U, C_VG, C_GG = 0, 512, 640, 768, 1280, 1792, 2304

ADAM_LR = 0.001
ADAM_B1 = 0.9
ADAM_B2 = 0.999
ADAM_EPS = 1e-08
ADAM_WD = 0.01
ADAM_STEP = 10

MESH = pl.DeviceIdType.MESH
MIB = 1024 * 1024
ANY = pl.BlockSpec(memory_space=pl.ANY)
VMEM = pl.BlockSpec(memory_space=pltpu.VMEM)

NT_DIMS = (((1,), (1,)), ((), ()))
TN_DIMS = (((0,), (0,)), ((), ()))

_PAIR_ORDER = (0, 4, 1, 5, 2, 6, 3, 7)
_CHUNK_SRC = np.array([
    list(_PAIR_ORDER) + [8, 9, 10],
    [0] + [1 + h for h in _PAIR_ORDER] + [9, 10],
    list(range(N_CHUNKS)),
    list(range(N_CHUNKS)),
], np.int32)
_CHUNK_POS = np.argsort(_CHUNK_SRC, axis=1).astype(np.int32)


def _bias_table():
    i = np.arange(N_HEADS * BLOCK)[:, None]
    j = np.arange(3 * BLOCK)[None, :]
    dist = np.abs(j - BLOCK - (i % BLOCK))
    slope = 2.0 ** -(i // BLOCK + 1.0)
    inner = np.where(dist <= BLOCK, -(slope * dist), NEG_INF)
    first = np.where(j >= BLOCK, inner, NEG_INF)
    last = np.where(j < 2 * BLOCK, inner, NEG_INF)
    return np.stack([first, inner, last]).astype(np.float32)


def _full(shape):
    n = len(shape)
    return pl.BlockSpec(shape, lambda *_: (0,) * n)


def _sds(shape, dtype=F32):
    return jax.ShapeDtypeStruct(shape, dtype)


def _coords():
    return lax.axis_index("x"), lax.axis_index("y"), lax.axis_index("c")


def _other_chips(x, y):
    return [(1 - x, y), (x, 1 - y), (1 - x, 1 - y)]


def _remote(src, dst, sems, dev):
    return pltpu.make_async_remote_copy(src_ref=src, dst_ref=dst, send_sem=sems[0], recv_sem=sems[1],
                                        device_id=dev, device_id_type=MESH)


class _Job:
    def __init__(self, inputs, out_shapes, n_remote, n_local, make):
        self.inputs, self.out_shapes, self.n_remote, self.n_local, self.make = inputs, out_shapes, n_remote, n_local, make


def _job_gather(sources, shapes):
    n = len(sources)

    def make(ins, outs, rsem, lsem):
        x, y, c = _coords()
        j = 2 * x + y
        res = []
        for t, ((_, layer), src, dst) in enumerate(zip(sources, ins, outs)):
            src = src if layer is None else src.at[layer]
            res.append(pltpu.make_async_copy(src, dst.at[j], lsem(t)))
            for k, chip in enumerate(_other_chips(x, y)):
                res.append(_remote(src, dst.at[j], rsem(3 * t + k), (*chip, c)))
        return res

    return _Job([a for a, _ in sources], [_sds((N_CHIPS,) + s, BF16) for s in shapes], 3 * n, n, make)


def _job_swap(g):
    _, rows, cols = g.shape
    half = rows // 2

    def make(ins, outs, rsem, lsem):
        x, y, c = _coords()
        return [_remote(ins[0].at[:, pl.ds((1 - c) * half, half), :], outs[0], rsem(0), (x, y, 1 - c))]

    return _Job([g], [_sds((N_CHIPS, half, cols))], 1, 0, make)


def _job_scatter(p):
    def make(ins, outs, rsem, lsem):
        x, y, c = _coords()
        return [_remote(ins[0].at[2 * chip[0] + chip[1]], outs[0].at[k], rsem(k), (*chip, c))
                for k, chip in enumerate(_other_chips(x, y))]

    return _Job([p], [_sds((3,) + p.shape[1:])], 3, 0, make)


def _job_join(f):
    half, cols = f.shape

    def make(ins, outs, rsem, lsem):
        x, y, c = _coords()
        dst = outs[0].at[pl.ds(c * half, half), :]
        return [pltpu.make_async_copy(ins[0], dst, lsem(0)), _remote(ins[0], dst, rsem(0), (x, y, 1 - c))]

    return _Job([f], [_sds((2 * half, cols))], 1, 1, make)


def _pallas(body, *, name, grid, in_specs, out_specs, out_shape, operands, vmem_mib, jobs=()):
    in_specs, out_specs, out_shape = list(in_specs), list(out_specs), list(out_shape)
    n_in, n_out = len(in_specs), len(out_specs)
    j_in = [a for j in jobs for a in j.inputs]
    j_out = [s for j in jobs for s in j.out_shapes]
    n_rem = max(1, sum(j.n_remote for j in jobs))
    n_loc = max(1, sum(j.n_local for j in jobs))
    scratch = [pltpu.SemaphoreType.DMA((n_rem,)), pltpu.SemaphoreType.DMA((n_rem,)),
               pltpu.SemaphoreType.DMA((n_loc,))] if jobs else []

    def wrapped(*refs):
        ins = refs[:n_in]
        jin = refs[n_in:n_in + len(j_in)]
        outs = refs[n_in + len(j_in):n_in + len(j_in) + n_out]
        jout = refs[n_in + len(j_in) + n_out:n_in + len(j_in) + n_out + len(j_out)]

        def copies():
            send, recv, loc = refs[-3:]
            res, a, b, r, l = [], 0, 0, 0, 0
            for j in jobs:
                res += j.make(jin[a:a + len(j.inputs)], jout[b:b + len(j.out_shapes)],
                              lambda k, r=r: (send.at[r + k], recv.at[r + k]), lambda k, l=l: loc.at[l + k])
                a, b, r, l = a + len(j.inputs), b + len(j.out_shapes), r + j.n_remote, l + j.n_local
            return res

        if jobs:
            first = last = None
            for d, n in enumerate(grid):
                f, e = pl.program_id(d) == 0, pl.program_id(d) == n - 1
                first, last = (f, e) if first is None else (first & f, last & e)

            @pl.when(first)
            def _():
                for cp in copies():
                    cp.start()

        body(*ins, *outs)

        if jobs:
            @pl.when(last)
            def _():
                for cp in copies():
                    cp.wait()

    return pl.pallas_call(
        wrapped, name=name, grid=grid,
        in_specs=in_specs + [ANY] * len(j_in), out_specs=out_specs + [ANY] * len(j_out),
        out_shape=out_shape + j_out, scratch_shapes=scratch,
        compiler_params=pltpu.CompilerParams(dimension_semantics=("arbitrary",) * len(grid),
                                             vmem_limit_bytes=vmem_mib * MIB),
    )(*operands, *j_in)


def _comm(jobs, name):
    j_in = [a for j in jobs for a in j.inputs]
    j_out = [s for j in jobs for s in j.out_shapes]
    n_rem = max(1, sum(j.n_remote for j in jobs))
    n_loc = max(1, sum(j.n_local for j in jobs))

    def body(*refs):
        jin, jout = refs[:len(j_in)], refs[len(j_in):len(j_in) + len(j_out)]
        send, recv, loc = refs[-3:]
        res, a, b, r, l = [], 0, 0, 0, 0
        for j in jobs:
            res += j.make(jin[a:a + len(j.inputs)], jout[b:b + len(j.out_shapes)],
                          lambda k, r=r: (send.at[r + k], recv.at[r + k]), lambda k, l=l: loc.at[l + k])
            a, b, r, l = a + len(j.inputs), b + len(j.out_shapes), r + j.n_remote, l + j.n_local
        for cp in res:
            cp.start()
        for cp in res:
            cp.wait()

    return pl.pallas_call(
        body, name=name, in_specs=[ANY] * len(j_in), out_specs=[ANY] * len(j_out), out_shape=j_out,
        scratch_shapes=[pltpu.SemaphoreType.DMA((n_rem,)), pltpu.SemaphoreType.DMA((n_rem,)),
                        pltpu.SemaphoreType.DMA((n_loc,))],
    )(*j_in)


def _sigmoid(x):
    return 1.0 / (1.0 + jnp.exp(-x))


def _lo_mask(shape):
    return lax.broadcasted_iota(jnp.int32, shape, len(shape) - 1) < HEAD_DIM


def _half_sum(x, lo):
    a = jnp.sum(jnp.where(lo, x, 0.0), axis=-1, keepdims=True)
    b = jnp.sum(jnp.where(lo, 0.0, x), axis=-1, keepdims=True)
    return jnp.where(lo, a, b)


def _half_rms_scale(x, lo):
    return lax.rsqrt(_half_sum(x * x, lo) * (1.0 / HEAD_DIM) + EPS)


def _stack_heads(pairs, lo):
    return jnp.concatenate([jnp.where(lo, t, 0.0) for t in pairs] + [jnp.where(lo, 0.0, t) for t in pairs], axis=0)


def _unstack_pair(stack, p, lo):
    return jnp.where(lo, stack[BLOCK * p:BLOCK * (p + 1)], stack[BLOCK * (4 + p):BLOCK * (5 + p)])


def _attention_probs(q_stack, kn, bias_ref, sink_ref):
    rows = N_HEADS * BLOCK
    s = lax.dot_general(q_stack, kn, NT_DIMS, preferred_element_type=F32) + bias_ref[...]
    sink = jnp.concatenate([jnp.full((BLOCK, BLOCK), sink_ref[h], F32) for h in range(N_HEADS)], axis=0)
    cols = [s[:, BLOCK * j:BLOCK * (j + 1)] for j in range(3)]
    top = jnp.max(jnp.maximum(jnp.maximum(cols[0], cols[1]), cols[2]), axis=-1, keepdims=True)
    m = jnp.maximum(jnp.broadcast_to(top, (rows, BLOCK)), sink)
    e = [jnp.exp(c - m) for c in cols]
    es = jnp.exp(sink - m)
    inv = 1.0 / (jnp.broadcast_to(jnp.sum((e[0] + e[1]) + e[2], axis=-1, keepdims=True), (rows, BLOCK)) + es)
    return jnp.concatenate([c * inv for c in e], axis=1), es * inv


def _kv_rows(cur_ref, pkv_ref, nkv_ref):
    k = jnp.concatenate([pkv_ref[:, 0:D_KV], cur_ref[:, C_K:C_K + D_KV], nkv_ref[:, 0:D_KV]], axis=0)
    v = jnp.concatenate([pkv_ref[:, D_KV:2 * D_KV], cur_ref[:, C_V:C_V + D_KV], nkv_ref[:, D_KV:2 * D_KV]], axis=0)
    return k, v


def _overlap_add(parts):
    blocks = []
    for j in range(SUB + 2):
        terms = [parts[b][BLOCK * (j - b):BLOCK * (j - b + 1)] for b in range(SUB) if 0 <= j - b <= 2]
        total = terms[0]
        for t in terms[1:]:
            total = total + t
        blocks.append(total)
    return jnp.concatenate(blocks, axis=0)


def _mix_specs(nt):
    cur = pl.BlockSpec((TILE, D_IN), lambda i: (i, 0))
    kv_col = C_K // (2 * D_KV)
    pkv = pl.BlockSpec((BLOCK, 2 * D_KV), lambda i: (jnp.maximum(i * SUB - 1, 0), kv_col))
    nkv = pl.BlockSpec((BLOCK, 2 * D_KV), lambda i: (jnp.minimum((i + 1) * SUB, nt * SUB - 1), kv_col))
    table = (None, N_HEADS * BLOCK, 3 * BLOCK)
    first = pl.BlockSpec(table, lambda i: (jnp.where(i == 0, 0, 1), 0, 0))
    inner = pl.BlockSpec(table, lambda i: (1, 0, 0))
    last = pl.BlockSpec(table, lambda i: (jnp.where(i == nt - 1, 2, 1), 0, 0))
    return cur, pkv, nkv, [first] + [inner] * (SUB - 2) + [last]


def _proj_fwd(x, ng, scale, shift, wt, name, jobs=()):
    s = x.shape[0]
    ts = min(512, s)

    def body(x_ref, ng_ref, sc_ref, sh_ref, w_ref, o_ref):
        xv = x_ref[...]
        r = lax.rsqrt(jnp.mean(xv * xv, axis=-1, keepdims=True) + EPS)
        h = ((xv * r) * ng_ref[...]) * (1.0 + sc_ref[...]) + sh_ref[...]
        o_ref[...] = lax.dot_general(h.astype(BF16), w_ref[...], NT_DIMS, preferred_element_type=F32)

    vec = _full((1, D_MODEL))
    return _pallas(
        body, name=name, grid=(s // ts,),
        in_specs=[pl.BlockSpec((ts, D_MODEL), lambda i: (i, 0)), vec, vec, vec, _full((D_IN, D_MODEL))],
        out_specs=[pl.BlockSpec((ts, D_IN), lambda i: (i, 0))], out_shape=[_sds((s, D_IN))],
        operands=(x, ng, scale, shift, wt), vmem_mib=48, jobs=jobs)


def _mix_fwd(proj, bias, sink, qg2, kg2, ws, bsp, name, jobs=()):
    s = proj.shape[0]
    nt = s // TILE

    def body(sink_ref, cur_ref, pkv_ref, nkv_ref, *rest):
        bias_refs = rest[:SUB]
        qg_ref, kg_ref, ws_ref, bsp_ref, y_ref = rest[SUB:]
        lo = _lo_mask((BLOCK, BLOCK))
        lo_kv = _lo_mask((TILE + 2 * BLOCK, BLOCK))
        k_all, v_all = _kv_rows(cur_ref, pkv_ref, nkv_ref)
        kn_all = ((k_all * _half_rms_scale(k_all, lo_kv)) * kg_ref[...]).astype(BF16)
        vb_all = v_all.astype(BF16)
        for b in range(SUB):
            rows = slice(BLOCK * b, BLOCK * (b + 1))
            window = slice(BLOCK * b, BLOCK * (b + 3))
            qn = []
            for p in range(4):
                q = cur_ref[rows, C_Q + BLOCK * p:C_Q + BLOCK * (p + 1)]
                qn.append(((q * _half_rms_scale(q, lo)) * qg_ref[...]) * 0.125)
            q_stack = _stack_heads(qn, lo).astype(BF16)
            prob, _ = _attention_probs(q_stack, kn_all[window], bias_refs[b], sink_ref)
            o_stack = jnp.dot(prob.astype(BF16), vb_all[window], preferred_element_type=F32)
            for p in range(4):
                g = cur_ref[rows, C_GA + BLOCK * p:C_GA + BLOCK * (p + 1)]
                y_ref[rows, BLOCK * p:BLOCK * (p + 1)] = (_unstack_pair(o_stack, p, lo) * (g * _sigmoid(g))).astype(BF16)
            for p in range(4):
                vg = cur_ref[rows, C_VG + BLOCK * p:C_VG + BLOCK * (p + 1)]
                vn = (vg * _half_rms_scale(vg, lo)).astype(BF16)
                sv = jnp.where(lo, jnp.dot(ws_ref[2 * p], vn, preferred_element_type=F32),
                               jnp.dot(ws_ref[2 * p + 1], vn, preferred_element_type=F32)) + bsp_ref[p]
                u = cur_ref[rows, C_U + BLOCK * p:C_U + BLOCK * (p + 1)]
                g = cur_ref[rows, C_GG + BLOCK * p:C_GG + BLOCK * (p + 1)]
                y_ref[rows, D_ATTN + BLOCK * p:D_ATTN + BLOCK * (p + 1)] = ((u * sv) * (g * _sigmoid(g))).astype(BF16)

    cur, pkv, nkv, bias_specs = _mix_specs(nt)
    return _pallas(
        body, name=name, grid=(nt,),
        in_specs=[pl.BlockSpec(memory_space=pltpu.SMEM), cur, pkv, nkv, *bias_specs, _full((1, BLOCK)), _full((1, BLOCK)),
                  _full((8, BLOCK, BLOCK)), _full((4, BLOCK, BLOCK))],
        out_specs=[pl.BlockSpec((TILE, D_MODEL), lambda i: (i, 0))], out_shape=[_sds((s, D_MODEL), BF16)],
        operands=(sink, proj, proj, proj, *([bias] * SUB), qg2, kg2, ws, bsp), vmem_mib=48, jobs=jobs)


def _out_fwd(y, x, gate, w_out, name):
    s = x.shape[0]
    ts = min(512, s)

    def body(y_ref, x_ref, g_ref, w_ref, o_ref):
        o_ref[...] = x_ref[...] + g_ref[...] * jnp.dot(y_ref[...], w_ref[...], preferred_element_type=F32)

    row = pl.BlockSpec((ts, D_MODEL), lambda i: (i, 0))
    return _pallas(
        body, name=name, grid=(s // ts,), in_specs=[row, row, _full((1, D_MODEL)), _full((D_MODEL, D_MODEL))],
        out_specs=[row], out_shape=[_sds((s, D_MODEL))], operands=(y, x, gate, w_out), vmem_mib=32)


def _out_fwd_loss(y, x, gate, w_out, target, name):
    s = x.shape[0]
    ts = min(512, s)

    def body(y_ref, x_ref, g_ref, w_ref, t_ref, dx_ref, sq_ref):
        @pl.when(pl.program_id(0) == 0)
        def _():
            sq_ref[...] = jnp.zeros_like(sq_ref)

        out = x_ref[...] + g_ref[...] * jnp.dot(y_ref[...], w_ref[...], preferred_element_type=F32)
        diff = out - t_ref[...]
        dx_ref[...] = diff * (1.0 / D_MODEL)
        per_token = jnp.sum(diff * diff, axis=-1, keepdims=True) * (1.0 / D_MODEL)
        sq_ref[...] += jnp.sum(per_token, axis=0, keepdims=True)

    row = pl.BlockSpec((ts, D_MODEL), lambda i: (i, 0))
    return _pallas(
        body, name=name, grid=(s // ts,), in_specs=[row, row, _full((1, D_MODEL)), _full((D_MODEL, D_MODEL)), row],
        out_specs=[row, _full((1, 1))], out_shape=[_sds((s, D_MODEL)), _sds((1, 1))],
        operands=(y, x, gate, w_out, target), vmem_mib=40)


def _out_bwd(dxo, y, gate, w_out, name, jobs=()):
    s = dxo.shape[0]
    ts = min(512, s)

    def body(dx_ref, y_ref, g_ref, w_ref, dy_ref, gw_ref, dg_ref):
        @pl.when(pl.program_id(0) == 0)
        def _():
            gw_ref[...] = jnp.zeros_like(gw_ref)
            dg_ref[...] = jnp.zeros_like(dg_ref)

        dx = dx_ref[...]
        do = (dx * g_ref[...]).astype(BF16)
        yv = y_ref[...]
        dy_ref[...] = lax.dot_general(do, w_ref[...], NT_DIMS, preferred_element_type=F32)
        gw_ref[...] += lax.dot_general(yv, do, TN_DIMS, preferred_element_type=F32)
        o = jnp.dot(yv, w_ref[...], preferred_element_type=F32)
        dg_ref[...] += jnp.sum(dx * o, axis=0, keepdims=True)

    row = pl.BlockSpec((ts, D_MODEL), lambda i: (i, 0))
    return _pallas(
        body, name=name, grid=(s // ts,), in_specs=[row, row, _full((1, D_MODEL)), _full((D_MODEL, D_MODEL))],
        out_specs=[row, _full((D_MODEL, D_MODEL)), _full((1, D_MODEL))],
        out_shape=[_sds((s, D_MODEL)), _sds((D_MODEL, D_MODEL)), _sds((1, D_MODEL))],
        operands=(dxo, y, gate, w_out), vmem_mib=48, jobs=jobs)


def _mix_bwd(dy, proj, bias, sink, qg2, kg2, ws, wst, bsp, name, jobs=()):
    s = proj.shape[0]
    nt = s // TILE

    def body(sink_ref, dy_ref, cur_ref, pkv_ref, nkv_ref, *rest):
        bias_refs = rest[:SUB]
        (qg_ref, kg_ref, ws_ref, wst_ref, bsp_ref,
         dpm_ref, p0_ref, p2_ref, dqg_ref, dkg_ref, dsink_ref, dws_ref, dbsp_ref) = rest[SUB:]

        @pl.when(pl.program_id(0) == 0)
        def _():
            dqg_ref[...] = jnp.zeros_like(dqg_ref)
            dkg_ref[...] = jnp.zeros_like(dkg_ref)
            dsink_ref[...] = jnp.zeros_like(dsink_ref)
            dws_ref[...] = jnp.zeros_like(dws_ref)
            dbsp_ref[...] = jnp.zeros_like(dbsp_ref)

        lo = _lo_mask((BLOCK, BLOCK))
        lo_kv = _lo_mask((TILE + 2 * BLOCK, BLOCK))
        lane_row = lax.broadcasted_iota(jnp.int32, (1, BLOCK), 1)
        qg = qg_ref[...]
        kg = kg_ref[...]

        k_all, v_all = _kv_rows(cur_ref, pkv_ref, nkv_ref)
        rk = _half_rms_scale(k_all, lo_kv)
        khat = k_all * rk
        kn_all = (khat * kg).astype(BF16)
        vb_all = v_all.astype(BF16)

        dkn_parts, dv_parts = [], []
        dsink = jnp.zeros((1, BLOCK), F32)
        dqg = jnp.zeros((1, BLOCK), F32)
        for b in range(SUB):
            rows = slice(BLOCK * b, BLOCK * (b + 1))
            window = slice(BLOCK * b, BLOCK * (b + 3))
            kn, vb = kn_all[window], vb_all[window]

            qhat, rq = [], []
            for p in range(4):
                q = cur_ref[rows, C_Q + BLOCK * p:C_Q + BLOCK * (p + 1)]
                r = _half_rms_scale(q, lo)
                rq.append(r)
                qhat.append(q * r)
            q_stack = _stack_heads([(qh * qg) * 0.125 for qh in qhat], lo).astype(BF16)
            prob, psink = _attention_probs(q_stack, kn, bias_refs[b], sink_ref)
            pb = prob.astype(BF16)
            o_stack = jnp.dot(pb, vb, preferred_element_type=F32)

            dout = []
            for p in range(4):
                g = cur_ref[rows, C_GA + BLOCK * p:C_GA + BLOCK * (p + 1)]
                sg = _sigmoid(g)
                dya = dy_ref[rows, BLOCK * p:BLOCK * (p + 1)]
                attn = _unstack_pair(o_stack, p, lo)
                dpm_ref[rows, C_GA + BLOCK * p:C_GA + BLOCK * (p + 1)] = dya * attn * (sg * (1.0 + g * (1.0 - sg)))
                dout.append(dya * (g * sg))
            do_stack = _stack_heads(dout, lo).astype(BF16)
            dp = lax.dot_general(do_stack, vb, NT_DIMS, preferred_element_type=F32)
            delta = jnp.sum(prob * dp, axis=-1, keepdims=True)
            dsb = (prob * (dp - delta)).astype(BF16)

            csink = -(psink * delta)
            for h in range(N_HEADS):
                tot = jnp.sum(csink[BLOCK * h:BLOCK * (h + 1)], axis=0, keepdims=True)
                dsink = dsink + jnp.where(lane_row == h, tot, 0.0)

            dq_stack = jnp.dot(dsb, kn, preferred_element_type=F32) * 0.125
            dkn_parts.append(lax.dot_general(dsb, q_stack, TN_DIMS, preferred_element_type=F32))
            dv_parts.append(lax.dot_general(pb, do_stack, TN_DIMS, preferred_element_type=F32))

            for p in range(4):
                dqn = _unstack_pair(dq_stack, p, lo)
                qh = qhat[p]
                dqg = dqg + jnp.sum(dqn * qh, axis=0, keepdims=True)
                dqh = dqn * qg
                mean = _half_sum(dqh * qh, lo) * (1.0 / HEAD_DIM)
                dpm_ref[rows, C_Q + BLOCK * p:C_Q + BLOCK * (p + 1)] = rq[p] * (dqh - qh * mean)

            for p in range(4):
                vg = cur_ref[rows, C_VG + BLOCK * p:C_VG + BLOCK * (p + 1)]
                r = _half_rms_scale(vg, lo)
                vnf = vg * r
                vn = vnf.astype(BF16)
                sv = jnp.where(lo, jnp.dot(ws_ref[2 * p], vn, preferred_element_type=F32),
                               jnp.dot(ws_ref[2 * p + 1], vn, preferred_element_type=F32)) + bsp_ref[p]
                u = cur_ref[rows, C_U + BLOCK * p:C_U + BLOCK * (p + 1)]
                g = cur_ref[rows, C_GG + BLOCK * p:C_GG + BLOCK * (p + 1)]
                sg = _sigmoid(g)
                dym = dy_ref[rows, D_ATTN + BLOCK * p:D_ATTN + BLOCK * (p + 1)]
                dpm_ref[rows, C_GG + BLOCK * p:C_GG + BLOCK * (p + 1)] = dym * (u * sv) * (sg * (1.0 + g * (1.0 - sg)))
                dgm = dym * (g * sg)
                dpm_ref[rows, C_U + BLOCK * p:C_U + BLOCK * (p + 1)] = dgm * sv
                dsv = dgm * u
                dsv_a = jnp.where(lo, dsv, 0.0)
                dsv_b = jnp.where(lo, 0.0, dsv)
                dws_ref[2 * p] += lax.dot_general(dsv_a.astype(BF16), vn, NT_DIMS, preferred_element_type=F32)
                dws_ref[2 * p + 1] += lax.dot_general(dsv_b.astype(BF16), vn, NT_DIMS, preferred_element_type=F32)
                dbsp_ref[p] += jnp.where(lo, jnp.sum(dsv_a, axis=-1, keepdims=True), jnp.sum(dsv_b, axis=-1, keepdims=True))
                dsvb = dsv.astype(BF16)
                dvn = jnp.where(lo, jnp.dot(wst_ref[2 * p], dsvb, preferred_element_type=F32),
                                jnp.dot(wst_ref[2 * p + 1], dsvb, preferred_element_type=F32))
                mean = _half_sum(dvn * vnf, lo) * (1.0 / HEAD_DIM)
                dpm_ref[rows, C_VG + BLOCK * p:C_VG + BLOCK * (p + 1)] = r * (dvn - vnf * mean)

        dsink_ref[...] += dsink
        dqg = jnp.broadcast_to(dqg, (8, BLOCK))
        dqg_ref[...] += dqg + pltpu.roll(dqg, HEAD_DIM, 1)

        dkn = _overlap_add(dkn_parts)
        dv = _overlap_add(dv_parts)
        dkg = jnp.broadcast_to(jnp.sum(dkn * khat, axis=0, keepdims=True), (8, BLOCK))
        dkg_ref[...] += dkg + pltpu.roll(dkg, HEAD_DIM, 1)
        dkh = dkn * kg
        dk = rk * (dkh - khat * (_half_sum(dkh * khat, lo_kv) * (1.0 / HEAD_DIM)))
        dpm_ref[:, C_K:C_K + D_KV] = dk[BLOCK:BLOCK + TILE]
        dpm_ref[:, C_V:C_V + D_KV] = dv[BLOCK:BLOCK + TILE]
        p0_ref[:, 0:D_KV] = dk[0:BLOCK]
        p0_ref[:, D_KV:2 * D_KV] = dv[0:BLOCK]
        p2_ref[:, 0:D_KV] = dk[BLOCK + TILE:]
        p2_ref[:, D_KV:2 * D_KV] = dv[BLOCK + TILE:]

    cur, pkv, nkv, bias_specs = _mix_specs(nt)
    kv_blk = (BLOCK, 2 * D_KV)
    return _pallas(
        body, name=name, grid=(nt,),
        in_specs=[pl.BlockSpec(memory_space=pltpu.SMEM), pl.BlockSpec((TILE, D_MODEL), lambda i: (i, 0)),
                  cur, pkv, nkv, *bias_specs, _full((1, BLOCK)), _full((1, BLOCK)),
                  _full((8, BLOCK, BLOCK)), _full((8, BLOCK, BLOCK)), _full((4, BLOCK, BLOCK))],
        out_specs=[cur,
                   pl.BlockSpec(kv_blk, lambda i: ((i + nt - 1) % nt, 0)),
                   pl.BlockSpec(kv_blk, lambda i: ((i + 1) % nt, 0)),
                   _full((8, BLOCK)), _full((8, BLOCK)), _full((1, BLOCK)),
                   _full((8, BLOCK, BLOCK)), _full((4, BLOCK, BLOCK))],
        out_shape=[_sds((s, D_IN)), _sds((nt * BLOCK, 2 * D_KV)), _sds((nt * BLOCK, 2 * D_KV)),
                   _sds((8, BLOCK)), _sds((8, BLOCK)), _sds((1, BLOCK)),
                   _sds((8, BLOCK, BLOCK)), _sds((4, BLOCK, BLOCK))],
        operands=(sink, dy, proj, proj, proj, *([bias] * SUB), qg2, kg2, ws, wst, bsp), vmem_mib=56, jobs=jobs)


def _proj_bwd(dpm, p0, p2, x, dxo, ng, scale, shift, wt, name, jobs=()):
    s = x.shape[0]
    ts = TILE

    def body(dpm_ref, p0_ref, p2_ref, x_ref, dxo_ref, ng_ref, sc_ref, sh_ref, w_ref,
             dxi_ref, dpb_ref, h_ref, dsh_ref, dsc_ref, dng_ref):
        @pl.when(pl.program_id(0) == 0)
        def _():
            dsh_ref[...] = jnp.zeros_like(dsh_ref)
            dsc_ref[...] = jnp.zeros_like(dsc_ref)
            dng_ref[...] = jnp.zeros_like(dng_ref)

        dpb_ref[:, 0:C_K] = dpm_ref[:, 0:C_K].astype(BF16)
        dpb_ref[0:BLOCK, C_K:C_GA] = (dpm_ref[0:BLOCK, C_K:C_GA] + p2_ref[...]).astype(BF16)
        if SUB > 2:
            dpb_ref[BLOCK:ts - BLOCK, C_K:C_GA] = dpm_ref[BLOCK:ts - BLOCK, C_K:C_GA].astype(BF16)
        dpb_ref[ts - BLOCK:ts, C_K:C_GA] = (dpm_ref[ts - BLOCK:ts, C_K:C_GA] + p0_ref[...]).astype(BF16)
        dpb_ref[:, C_GA:D_IN] = dpm_ref[:, C_GA:D_IN].astype(BF16)
        dh = jnp.dot(dpb_ref[...], w_ref[...], preferred_element_type=F32)

        xv = x_ref[...]
        r = lax.rsqrt(jnp.mean(xv * xv, axis=-1, keepdims=True) + EPS)
        xn = xv * r
        ngv = ng_ref[...]
        sc1 = 1.0 + sc_ref[...]
        h1 = xn * ngv
        h_ref[...] = (h1 * sc1 + sh_ref[...]).astype(BF16)
        dsh_ref[...] += jnp.sum(dh, axis=0, keepdims=True)
        dsc_ref[...] += jnp.sum(dh * h1, axis=0, keepdims=True)
        dh1 = dh * sc1
        dng_ref[...] += jnp.sum(dh1 * xn, axis=0, keepdims=True)
        dxn = dh1 * ngv
        dxi_ref[...] = r * (dxn - xn * jnp.mean(dxn * xn, axis=-1, keepdims=True)) + dxo_ref[...]

    row = pl.BlockSpec((ts, D_MODEL), lambda i: (i, 0))
    wide = pl.BlockSpec((ts, D_IN), lambda i: (i, 0))
    kv = pl.BlockSpec((BLOCK, 2 * D_KV), lambda i: (i, 0))
    vec = _full((1, D_MODEL))
    return _pallas(
        body, name=name, grid=(s // ts,),
        in_specs=[wide, kv, kv, row, row, vec, vec, vec, _full((D_IN, D_MODEL))],
        out_specs=[row, wide, row, vec, vec, vec],
        out_shape=[_sds((s, D_MODEL)), _sds((s, D_IN), BF16), _sds((s, D_MODEL), BF16),
                   _sds((1, D_MODEL)), _sds((1, D_MODEL)), _sds((1, D_MODEL))],
        operands=(dpm, p0, p2, x, dxo, ng, scale, shift, wt), vmem_mib=48, jobs=jobs)


def _w_in_grad(dpb, h, name, jobs=()):
    s = h.shape[0]
    tk = min(1024, s)
    tn = D_IN // 2

    def body(d_ref, h_ref, o_ref):
        @pl.when(pl.program_id(1) == 0)
        def _():
            o_ref[...] = jnp.zeros_like(o_ref)

        o_ref[...] += lax.dot_general(d_ref[...], h_ref[...], TN_DIMS, preferred_element_type=F32)

    return _pallas(
        body, name=name, grid=(2, s // tk),
        in_specs=[pl.BlockSpec((tk, tn), lambda n, k: (k, n)), pl.BlockSpec((tk, D_MODEL), lambda n, k: (k, 0))],
        out_specs=[pl.BlockSpec((tn, D_MODEL), lambda n, k: (n, 0))], out_shape=[_sds((D_IN, D_MODEL))],
        operands=(dpb, h), vmem_mib=48, jobs=jobs)


def _pair_sum(g, r, c_idx, name):
    _, rows, cols = g.shape
    half = rows // 2

    def body(c_ref, g_ref, r_ref, o_ref):
        o_ref[...] = g_ref[...] + r_ref[...]

    blk = (None, half, cols)
    return pl.pallas_call(
        body, name=name,
        grid_spec=pltpu.PrefetchScalarGridSpec(
            num_scalar_prefetch=1, grid=(N_CHIPS,),
            in_specs=[pl.BlockSpec(blk, lambda j, c: (j, c[0], 0)), pl.BlockSpec(blk, lambda j, c: (j, 0, 0))],
            out_specs=pl.BlockSpec(blk, lambda j, c: (j, 0, 0))),
        out_shape=_sds((N_CHIPS, half, cols)),
        compiler_params=pltpu.CompilerParams(dimension_semantics=("arbitrary",), vmem_limit_bytes=32 * MIB),
    )(c_idx, g, r)


def _chip_sum(p, r, j_idx, name):
    _, rows, cols = p.shape
    tr = rows // 2

    def body(j_ref, p_ref, r_ref, o_ref):
        o_ref[...] = ((p_ref[...] + r_ref[0]) + r_ref[1]) + r_ref[2]

    return pl.pallas_call(
        body, name=name,
        grid_spec=pltpu.PrefetchScalarGridSpec(
            num_scalar_prefetch=1, grid=(2,),
            in_specs=[pl.BlockSpec((None, tr, cols), lambda t, j: (j[0], t, 0)),
                      pl.BlockSpec((3, tr, cols), lambda t, j: (0, t, 0))],
            out_specs=pl.BlockSpec((tr, cols), lambda t, j: (t, 0))),
        out_shape=_sds((rows, cols)),
        compiler_params=pltpu.CompilerParams(dimension_semantics=("arbitrary",), vmem_limit_bytes=32 * MIB),
    )(j_idx, p, r)


def _cast_permute_w_in(wt, src_chunks):
    def body(t_ref, w_ref, o_ref):
        o_ref[...] = w_ref[...].astype(BF16)

    return pl.pallas_call(
        body, name="cast_permute_w_in",
        grid_spec=pltpu.PrefetchScalarGridSpec(
            num_scalar_prefetch=1, grid=(DEPTH, N_CHUNKS),
            in_specs=[pl.BlockSpec((None, CHUNK_ROWS, D_MODEL), lambda l, t, tbl: (l, tbl[t], 0))],
            out_specs=pl.BlockSpec((None, CHUNK_ROWS, D_MODEL), lambda l, t, tbl: (l, t, 0))),
        out_shape=_sds((DEPTH, W_IN_BLK, D_MODEL), BF16),
        compiler_params=pltpu.CompilerParams(dimension_semantics=("arbitrary", "arbitrary")),
    )(src_chunks, wt)


def _gather_inputs(c, w_out, wloc_in):
    half = W_IN_BLK // 2

    def body(c_ref, wout_ref, wloc_ref, call_ref, woutb_ref, w0_ref, send_sems, recv_sems, local_sem):
        x, y, cc = _coords()
        j = 2 * x + y
        b = 2 * j + cc
        sib = (x, y, 1 - cc)
        woutb_ref[...] = wout_ref[...].astype(BF16)
        call_ref[b] = c_ref[...]
        chips = _other_chips(x, y)

        def sems(k):
            return send_sems.at[k], recv_sems.at[k]

        def half_rows(chip_index):
            return w0_ref.at[chip_index, pl.ds(cc * half, half), :]

        mine = pltpu.make_async_copy(wloc_ref.at[0], w0_ref.at[j], local_sem)
        mine.start()
        first = [_remote(wloc_ref.at[0, pl.ds(cc * half, half), :], half_rows(j), sems(k), (*chip, cc))
                 for k, chip in enumerate(chips)]
        k = 3
        rest = []
        for fx in (0, 1):
            for fy in (0, 1):
                for fc in (0, 1):
                    if fx or fy or fc:
                        dev = (1 - x if fx else x, 1 - y if fy else y, 1 - cc if fc else cc)
                        rest.append(_remote(call_ref.at[b], call_ref.at[b], sems(k), dev))
                        k += 1
        for cp in first + rest:
            cp.start()
        passed = []
        for k, chip in enumerate(chips):
            jk = 2 * chip[0] + chip[1]
            first[k].wait_recv()
            passed.append(_remote(half_rows(jk), half_rows(jk), sems(10 + k), sib))
            passed[k].start()
        for cp in first:
            cp.wait_send()
        for cp in rest + passed:
            cp.wait()
        mine.wait()

    return pl.pallas_call(
        body, name="gather_inputs", in_specs=[VMEM, VMEM, ANY], out_specs=[VMEM, VMEM, ANY],
        out_shape=[_sds((N_DEV, 1, D_MODEL)), _sds((DEPTH, W_OUT_BLK, D_MODEL), BF16),
                   _sds((N_CHIPS, W_IN_BLK, D_MODEL), BF16)],
        scratch_shapes=[pltpu.SemaphoreType.DMA((13,)), pltpu.SemaphoreType.DMA((13,)), pltpu.SemaphoreType.DMA],
        compiler_params=pltpu.CompilerParams(vmem_limit_bytes=32 * MIB),
    )(c, w_out, wloc_in)


def _ada_rows(c_all, w_ada, b_blk):
    def body(c_ref, w_ref, b_ref, o_ref, cond_ref):
        cv = c_ref[...]
        cond = (cv * _sigmoid(cv)).astype(BF16)
        cond_ref[...] = cond.astype(F32)
        for l in range(DEPTH):
            o_ref[:, l, :] = jnp.dot(cond, w_ref[l].astype(BF16), preferred_element_type=F32) + b_ref[l:l + 1, :]

    return pl.pallas_call(
        body, name="ada_rows", in_specs=[VMEM, VMEM, VMEM], out_specs=[VMEM, VMEM],
        out_shape=[_sds((N_DEV, DEPTH, W_ADA_BLK)), _sds((N_DEV, D_MODEL))],
        compiler_params=pltpu.CompilerParams(vmem_limit_bytes=32 * MIB),
    )(c_all, w_ada, b_blk)


def _exchange_ada(part):
    def body(part_ref, out_ref, send_sems, recv_sems):
        x, y, cc = _coords()
        j = 2 * x + y
        out_ref[j] = part_ref[2 * j + cc]
        copies = []
        for k, chip in enumerate(_other_chips(x, y)):
            b_dst = 4 * chip[0] + 2 * chip[1] + cc
            copies.append(_remote(part_ref.at[b_dst], out_ref.at[j], (send_sems.at[k], recv_sems.at[k]), (*chip, cc)))
        for cp in copies:
            cp.start()
        for cp in copies:
            cp.wait()

    return pl.pallas_call(
        body, name="exchange_ada", in_specs=[VMEM], out_specs=VMEM,
        out_shape=_sds((N_CHIPS, DEPTH, W_ADA_BLK)),
        scratch_shapes=[pltpu.SemaphoreType.DMA((3,)), pltpu.SemaphoreType.DMA((3,))],
    )(part)


def _all_gather_rows(blk):
    m_per, n = blk.shape

    def body(x_ref, out_ref, send_sems, recv_sems, local_sem):
        x, y, c = _coords()
        me, sibling = (x, y, c), (x, y, 1 - c)
        chips = _other_chips(x, y)

        def rows(px, py, pc):
            return out_ref.at[pl.ds((4 * px + 2 * py + pc) * m_per, m_per), :]

        def copy(k, block, to, src=None):
            return _remote(rows(*block) if src is None else src, rows(*block), (send_sems.at[k], recv_sems.at[k]), to)

        mine = pltpu.make_async_copy(x_ref, rows(*me), local_sem)
        mine.start()
        first = [copy(0, me, sibling, src=x_ref)]
        first += [copy(1 + j, me, (*chip, c), src=x_ref) for j, chip in enumerate(chips)]
        for cp in first:
            cp.start()
        passed = [copy(4 + j, (*chip, c), sibling) for j, chip in enumerate(chips)]
        for j, chip in enumerate(chips):
            copy(1 + j, (*chip, c), me).wait_recv()
            passed[j].start()
        copy(0, sibling, me).wait_recv()
        for j, chip in enumerate(chips):
            copy(4 + j, (*chip, 1 - c), me).wait_recv()
        for cp in first + passed:
            cp.wait_send()
        mine.wait()

    return pl.pallas_call(
        body, name="all_gather_small", in_specs=[VMEM], out_specs=VMEM,
        out_shape=_sds((N_DEV * m_per, n), blk.dtype),
        scratch_shapes=[pltpu.SemaphoreType.DMA((7,)), pltpu.SemaphoreType.DMA((7,)), pltpu.SemaphoreType.DMA],
        compiler_params=pltpu.CompilerParams(vmem_limit_bytes=32 * MIB),
    )(blk)


def _adamw_math(w, g, m, v):
    m = ADAM_B1 * m + (1.0 - ADAM_B1) * g
    v = ADAM_B2 * v + (1.0 - ADAM_B2) * (g * g)
    m_hat = m / (1.0 - ADAM_B1 ** ADAM_STEP)
    v_hat = v / (1.0 - ADAM_B2 ** ADAM_STEP)
    delta = -ADAM_LR * (m_hat / (jnp.sqrt(v_hat) + ADAM_EPS) + ADAM_WD * w)
    return delta, m, v


def _adamw_w_in(w, g0, g1, m, v, pos_chunks):
    def body(t_ref, w_ref, g0_ref, g1_ref, m_ref, v_ref, g_ref, d_ref, nm_ref, nv_ref):
        g = jnp.where(pl.program_id(0) == 0, g0_ref[...], g1_ref[...])
        g_ref[...] = g
        d_ref[...], nm_ref[...], nv_ref[...] = _adamw_math(w_ref[...], g, m_ref[...], v_ref[...])

    nat = pl.BlockSpec((None, CHUNK_ROWS, D_MODEL), lambda l, t, tbl: (l, t, 0))
    per = pl.BlockSpec((CHUNK_ROWS, D_MODEL), lambda l, t, tbl: (tbl[t], 0))
    return pl.pallas_call(
        body, name="adamw_w_in",
        grid_spec=pltpu.PrefetchScalarGridSpec(num_scalar_prefetch=1, grid=(DEPTH, N_CHUNKS),
                                               in_specs=[nat, per, per, nat, nat], out_specs=[nat] * 4),
        out_shape=[_sds(w.shape)] * 4,
        compiler_params=pltpu.CompilerParams(dimension_semantics=("arbitrary", "arbitrary")),
    )(pos_chunks, w, g0, g1, m, v)


def _adamw_w_out(w, g0, g1, m, v):
    def body(w_ref, g0_ref, g1_ref, m_ref, v_ref, g_ref, d_ref, nm_ref, nv_ref):
        g = jnp.where(pl.program_id(0) == 0, g0_ref[...], g1_ref[...])
        g_ref[...] = g
        d_ref[...], nm_ref[...], nv_ref[...] = _adamw_math(w_ref[...], g, m_ref[...], v_ref[...])

    blk = pl.BlockSpec((None, W_OUT_BLK, D_MODEL), lambda l: (l, 0, 0))
    gblk = _full((W_OUT_BLK, D_MODEL))
    return pl.pallas_call(
        body, name="adamw_w_out", grid=(DEPTH,), in_specs=[blk, gblk, gblk, blk, blk], out_specs=[blk] * 4,
        out_shape=[_sds(w.shape)] * 4,
        compiler_params=pltpu.CompilerParams(dimension_semantics=("arbitrary",), vmem_limit_bytes=32 * MIB),
    )(w, g0, g1, m, v)


def _w_ada_grad_adamw(cond_t, dada, w, m, v):
    _, rows, cols = w.shape
    tr = 256

    def body(ct_ref, da_ref, w_ref, m_ref, v_ref, g_ref, d_ref, nm_ref, nv_ref):
        g = jnp.dot(ct_ref[...], da_ref[...].astype(BF16), preferred_element_type=F32)
        g_ref[...] = g
        d_ref[...], nm_ref[...], nv_ref[...] = _adamw_math(w_ref[...], g, m_ref[...], v_ref[...])

    blk = pl.BlockSpec((None, tr, cols), lambda l, t: (l, t, 0))
    return pl.pallas_call(
        body, name="w_ada_grad_adamw", grid=(DEPTH, rows // tr),
        in_specs=[pl.BlockSpec((tr, BLOCK), lambda l, t: (t, 0)), pl.BlockSpec((None, BLOCK, cols), lambda l, t: (l, 0, 0)),
                  blk, blk, blk],
        out_specs=[blk] * 4, out_shape=[_sds(w.shape)] * 4,
        compiler_params=pltpu.CompilerParams(dimension_semantics=("arbitrary", "arbitrary"), vmem_limit_bytes=32 * MIB),
    )(cond_t, dada, w, m, v)


def _small_sum_adamw(gathered, w, m, v):
    def body(a_ref, w_ref, m_ref, v_ref, g_ref, d_ref, nm_ref, nv_ref):
        g = a_ref[0]
        for b in range(1, N_DEV):
            g = g + a_ref[b]
        g_ref[...] = g
        d_ref[...], nm_ref[...], nv_ref[...] = _adamw_math(w_ref[...], g, m_ref[...], v_ref[...])

    return pl.pallas_call(
        body, name="small_sum_adamw", in_specs=[VMEM] * 4, out_specs=[VMEM] * 4, out_shape=[_sds(w.shape)] * 4,
        compiler_params=pltpu.CompilerParams(vmem_limit_bytes=48 * MIB),
    )(gathered, w, m, v)


_SMALL = (("w_s", DEPTH * 8 * BLOCK), ("b_s", DEPTH * 8), ("b_ada", DEPTH * 24), ("norm_gain", DEPTH * 8),
          ("q_gain", 1), ("k_gain", 1), ("sink", 1))
_SMALL_ROWS = sum(r for _, r in _SMALL)
_SMALL_PAD = -_SMALL_ROWS % 8


def _pack_small(parts):
    rows = []
    for name, n in _SMALL:
        flat = parts[name].reshape(-1)
        rows.append(jnp.pad(flat, (0, n * 128 - flat.shape[0])).reshape(n, 128))
    rows.append(jnp.zeros((_SMALL_PAD, 128), F32))
    return jnp.concatenate(rows, axis=0)


def _unpack_small(packed, shapes):
    out, r0 = {}, 0
    for name, n in _SMALL:
        size = 1
        for d in shapes[name]:
            size *= d
        out[name] = packed[r0:r0 + n].reshape(-1)[:size].reshape(shapes[name])
        r0 += n
    return out


def _permute_heads(a, axis):
    shp = a.shape
    a = a.reshape(shp[:axis] + (2, 4, HEAD_DIM) + shp[axis + 1:])
    a = jnp.swapaxes(a, axis, axis + 1)
    return a.reshape(shp)


def _unpermute_heads(a, axis):
    shp = a.shape
    a = a.reshape(shp[:axis] + (4, 2, HEAD_DIM) + shp[axis + 1:])
    a = jnp.swapaxes(a, axis, axis + 1)
    return a.reshape(shp)


def _permute_w_out(w):
    return jnp.concatenate([_permute_heads(w[:D_ATTN], 0), w[D_ATTN:]], axis=0)


def _unpermute_w_out(w):
    return jnp.concatenate([_unpermute_heads(w[:D_ATTN], 0), w[D_ATTN:]], axis=0)


def kernel(x, c, w_ada, b_ada, norm_gain, w_in, q_gain, k_gain, sink, w_s, b_s, w_out, loss_target, m_w_ada, m_b_ada, m_norm_gain, m_w_in, m_q_gain, m_k_gain, m_sink, m_w_s, m_b_s, m_w_out, v_w_ada, v_b_ada, v_norm_gain, v_w_in, v_q_gain, v_k_gain, v_sink, v_w_s, v_b_s, v_w_out):
    ix, iy, ic = _coords()
    chip = 2 * ix + iy
    chip_idx = jnp.reshape(chip, (1,)).astype(jnp.int32)
    core_idx = jnp.reshape(ic, (1,)).astype(jnp.int32)
    src_chunks = lax.dynamic_index_in_dim(jnp.asarray(_CHUNK_SRC), chip, 0, keepdims=False)
    pos_chunks = lax.dynamic_index_in_dim(jnp.asarray(_CHUNK_POS), chip, 0, keepdims=False)
    x0, target = x[0], loss_target[0]

    wt, mt, vt = (jnp.swapaxes(a, 1, 2) for a in (w_in, m_w_in, v_w_in))
    wloc_in = _cast_permute_w_in(wt, src_chunks)
    c_all, wloc_out, w0 = _gather_inputs(c, w_out, wloc_in)
    wts = [w0.reshape(D_IN, D_MODEL), None]

    b_blk = lax.dynamic_slice_in_dim(b_ada, chip * W_ADA_BLK, W_ADA_BLK, axis=1)
    ada_part, cond = _ada_rows(c_all.reshape(N_DEV, D_MODEL), w_ada, b_blk)
    ada = jnp.moveaxis(_exchange_ada(ada_part), 0, 1).reshape(DEPTH, 3 * D_MODEL)
    shift = [ada[l:l + 1, 0:D_MODEL] for l in range(DEPTH)]
    scale = [ada[l:l + 1, D_MODEL:2 * D_MODEL] for l in range(DEPTH)]
    gate = [ada[l:l + 1, 2 * D_MODEL:] for l in range(DEPTH)]
    ng = [norm_gain[l:l + 1] for l in range(DEPTH)]

    qg2 = jnp.concatenate([q_gain, q_gain], axis=-1)
    kg2 = jnp.concatenate([k_gain, k_gain], axis=-1)
    ws_b = w_s.astype(BF16)
    wst_b = jnp.swapaxes(w_s, -1, -2).astype(BF16)
    bsp = jnp.repeat(jnp.swapaxes(b_s.reshape(DEPTH, 4, 2, BLOCK), -1, -2), HEAD_DIM, axis=-1)
    bias = jnp.asarray(_bias_table())

    def mix_args(l):
        return bias, sink[l], qg2[l:l + 1], kg2[l:l + 1], ws_b[l]

    w_out_shape = (W_OUT_BLK, D_MODEL)
    proj0, wo0 = _proj_fwd(x0, ng[0], scale[0], shift[0], wts[0], "proj_fwd_0",
                           jobs=[_job_gather([(wloc_out, 0)], [w_out_shape])])
    y0, w1, wo1 = _mix_fwd(proj0, *mix_args(0), bsp[0], "mix_fwd_0",
                           jobs=[_job_gather([(wloc_in, 1), (wloc_out, 1)], [(W_IN_BLK, D_MODEL), w_out_shape])])
    wts[1] = w1.reshape(D_IN, D_MODEL)
    wos = [_permute_w_out(w.reshape(D_MODEL, D_MODEL)) for w in (wo0, wo1)]
    x1, = _out_fwd(y0, x0, gate[0], wos[0], "out_fwd_0")
    proj1, = _proj_fwd(x1, ng[1], scale[1], shift[1], wts[1], "proj_fwd_1")
    y1, = _mix_fwd(proj1, *mix_args(1), bsp[1], "mix_fwd_1")
    dx2, sq = _out_fwd_loss(y1, x1, gate[1], wos[1], target, "out_fwd_loss_1")
    loss = lax.psum(0.5 * sq[0, 0], ("x", "y", "c"))

    def blocks_out(gw):
        return _unpermute_w_out(gw).reshape(N_CHIPS, W_OUT_BLK, D_MODEL)

    dy1, gwo1, dgate1 = _out_bwd(dx2, y1, gate[1], wos[1], "out_bwd_1")
    go1 = blocks_out(gwo1)
    dpm, p0, p2, dqg1, dkg1, dsink1, dws1, dbsp1, ro1 = _mix_bwd(
        dy1, proj1, *mix_args(1), wst_b[1], bsp[1], "mix_bwd_1", jobs=[_job_swap(go1)])
    po1 = _pair_sum(go1, ro1, core_idx, "pair_sum_w_out_1")
    dx1, dpb, h, dsh1, dsc1, dng1, co1 = _proj_bwd(dpm, p0, p2, x1, dx2, ng[1], scale[1], shift[1], wts[1], "proj_bwd_1",
                                                   jobs=[_job_scatter(po1)])
    fo1 = _chip_sum(po1, co1, chip_idx, "chip_sum_w_out_1")
    gwi1, grad_wo1 = _w_in_grad(dpb, h, "w_in_grad_1", jobs=[_job_join(fo1)])
    gi1 = gwi1.reshape(N_CHIPS, W_IN_BLK, D_MODEL)

    dy0, gwo0, dgate0, ri1 = _out_bwd(dx1, y0, gate[0], wos[0], "out_bwd_0", jobs=[_job_swap(gi1)])
    pi1 = _pair_sum(gi1, ri1, core_idx, "pair_sum_w_in_1")
    go0 = blocks_out(gwo0)
    dpm, p0, p2, dqg0, dkg0, dsink0, dws0, dbsp0, ci1, ro0 = _mix_bwd(
        dy0, proj0, *mix_args(0), wst_b[0], bsp[0], "mix_bwd_0", jobs=[_job_scatter(pi1), _job_swap(go0)])
    fi1 = _chip_sum(pi1, ci1, chip_idx, "chip_sum_w_in_1")
    po0 = _pair_sum(go0, ro0, core_idx, "pair_sum_w_out_0")
    dx0, dpb, h, dsh0, dsc0, dng0, grad_wi1, co0 = _proj_bwd(
        dpm, p0, p2, x0, dx1, ng[0], scale[0], shift[0], wts[0], "proj_bwd_0", jobs=[_job_join(fi1), _job_scatter(po0)])
    fo0 = _chip_sum(po0, co0, chip_idx, "chip_sum_w_out_0")
    gwi0, grad_wo0 = _w_in_grad(dpb, h, "w_in_grad_0", jobs=[_job_join(fo0)])
    gi0 = gwi0.reshape(N_CHIPS, W_IN_BLK, D_MODEL)

    ri0, = _comm([_job_swap(gi0)], "swap_w_in_0")
    pi0 = _pair_sum(gi0, ri0, core_idx, "pair_sum_w_in_0")
    ci0, = _comm([_job_scatter(pi0)], "scatter_w_in_0")
    fi0 = _chip_sum(pi0, ci0, chip_idx, "chip_sum_w_in_0")
    grad_wi0, = _comm([_job_join(fi0)], "join_w_in_0")

    def bs_grad(dbsp):
        return jnp.swapaxes(dbsp[:, :, ::HEAD_DIM], -1, -2).reshape(8, BLOCK)

    small_g = dict(
        w_s=jnp.stack([dws0, dws1]), b_s=jnp.stack([bs_grad(dbsp0), bs_grad(dbsp1)]),
        b_ada=jnp.stack([jnp.concatenate([dsh0, dsc0, dgate0], axis=-1)[0], jnp.concatenate([dsh1, dsc1, dgate1], axis=-1)[0]]),
        norm_gain=jnp.stack([dng0[0], dng1[0]]),
        q_gain=jnp.stack([dqg0[0, :HEAD_DIM], dqg1[0, :HEAD_DIM]]), k_gain=jnp.stack([dkg0[0, :HEAD_DIM], dkg1[0, :HEAD_DIM]]),
        sink=jnp.stack([dsink0[0, :N_HEADS], dsink1[0, :N_HEADS]]))
    small_w = dict(w_s=w_s, b_s=b_s, b_ada=b_ada, norm_gain=norm_gain, q_gain=q_gain, k_gain=k_gain, sink=sink)
    small_m = dict(w_s=m_w_s, b_s=m_b_s, b_ada=m_b_ada, norm_gain=m_norm_gain, q_gain=m_q_gain, k_gain=m_k_gain, sink=m_sink)
    small_v = dict(w_s=v_w_s, b_s=v_b_s, b_ada=v_b_ada, norm_gain=v_norm_gain, q_gain=v_q_gain, k_gain=v_k_gain, sink=v_sink)
    rows = _SMALL_ROWS + _SMALL_PAD
    gathered = _all_gather_rows(_pack_small(small_g)).reshape(N_DEV, rows, 128)
    packed = _small_sum_adamw(gathered, _pack_small(small_w), _pack_small(small_m), _pack_small(small_v))
    shapes = {k: a.shape for k, a in small_w.items()}
    sg, sd, sm, sv = (_unpack_small(p, shapes) for p in packed)

    r0 = DEPTH * 8 * BLOCK + DEPTH * 8
    dada_all = gathered[:, r0:r0 + DEPTH * 24].reshape(N_DEV, DEPTH, 3 * D_MODEL)
    dada_blk = jnp.moveaxis(lax.dynamic_slice_in_dim(dada_all, chip * W_ADA_BLK, W_ADA_BLK, axis=2), 0, 1)
    pad = BLOCK - N_DEV
    ada_out = _w_ada_grad_adamw(
        jnp.pad(cond.T, ((0, 0), (0, pad))).astype(BF16), jnp.pad(dada_blk, ((0, 0), (0, pad), (0, 0))),
        w_ada, m_w_ada, v_w_ada)

    in_out = [jnp.swapaxes(a, 1, 2) for a in _adamw_w_in(wt, grad_wi0, grad_wi1, mt, vt, pos_chunks)]
    out_out = _adamw_w_out(w_out, grad_wo0, grad_wo1, m_w_out, v_w_out)

    def ordered(k):
        small = (sg, sd, sm, sv)[k]
        return (ada_out[k], small["b_ada"], small["norm_gain"], in_out[k], small["q_gain"], small["k_gain"], small["sink"],
                small["w_s"], small["b_s"], out_out[k])

    return (loss, dx0[None], *ordered(0), *ordered(1), *ordered(2), *ordered(3))
```

```python
import numpy as np

import jax
import jax.numpy as jnp
from jax import lax
from jax.experimental import pallas as pl
from jax.experimental.pallas import tpu as pltpu

F32 = jnp.float32
BF16 = jnp.bfloat16

D_MODEL = 1024
DEPTH = 2
HEAD_DIM = 64
N_HEADS = 8
BLOCK = 128
SUB = 2
TILE = SUB * BLOCK
D_ATTN = 512
D_KV = 128
D_IN = 2816
N_CHIPS = 4
N_DEV = 8
W_IN_BLK = D_IN // N_CHIPS
W_OUT_BLK = D_MODEL // N_CHIPS
W_ADA_BLK = 3 * D_MODEL // N_CHIPS
CHUNK_ROWS = HEAD_DIM
N_CHUNKS = W_IN_BLK // CHUNK_ROWS
EPS = 1e-6
NEG_INF = -1e30

C_Q, C_K, C_V, C_GA, C_U, C_VG, C_GG = 0, 512, 640, 768, 1280, 1792, 2304

ADAM_LR = 0.001
ADAM_B1 = 0.9
ADAM_B2 = 0.999
ADAM_EPS = 1e-08
ADAM_WD = 0.01
ADAM_STEP = 10

MESH = pl.DeviceIdType.MESH
MIB = 1024 * 1024
ANY = pl.BlockSpec(memory_space=pl.ANY)
VMEM = pl.BlockSpec(memory_space=pltpu.VMEM)

NT_DIMS = (((1,), (1,)), ((), ()))
TN_DIMS = (((0,), (0,)), ((), ()))

_PAIR_ORDER = (0, 4, 1, 5, 2, 6, 3, 7)
_CHUNK_SRC = np.array([
    list(_PAIR_ORDER) + [8, 9, 10],
    [0] + [1 + h for h in _PAIR_ORDER] + [9, 10],
    list(range(N_CHUNKS)),
    list(range(N_CHUNKS)),
], np.int32)
_CHUNK_POS = np.argsort(_CHUNK_SRC, axis=1).astype(np.int32)


def _bias_table():
    i = np.arange(N_HEADS * BLOCK)[:, None]
    j = np.arange(3 * BLOCK)[None, :]
    dist = np.abs(j - BLOCK - (i % BLOCK))
    slope = 2.0 ** -(i // BLOCK + 1.0)
    inner = np.where(dist <= BLOCK, -(slope * dist), NEG_INF)
    first = np.where(j >= BLOCK, inner, NEG_INF)
    last = np.where(j < 2 * BLOCK, inner, NEG_INF)
    return np.stack([first, inner, last]).astype(np.float32)


def _full(shape):
    n = len(shape)
    return pl.BlockSpec(shape, lambda *_: (0,) * n)


def _sds(shape, dtype=F32):
    return jax.ShapeDtypeStruct(shape, dtype)


def _coords():
    return lax.axis_index("x"), lax.axis_index("y"), lax.axis_index("c")


def _other_chips(x, y):
    return [(1 - x, y), (x, 1 - y), (1 - x, 1 - y)]


def _remote(src, dst, sems, dev):
    return pltpu.make_async_remote_copy(src_ref=src, dst_ref=dst, send_sem=sems[0], recv_sem=sems[1],
                                        device_id=dev, device_id_type=MESH)


class _Job:
    def __init__(self, inputs, out_shapes, n_remote, n_local, make):
        self.inputs, self.out_shapes, self.n_remote, self.n_local, self.make = inputs, out_shapes, n_remote, n_local, make


def _job_gather(sources, shapes):
    n = len(sources)

    def make(ins, outs, rsem, lsem):
        x, y, c = _coords()
        j = 2 * x + y
        res = []
        for t, ((_, layer), src, dst) in enumerate(zip(sources, ins, outs)):
            src = src if layer is None else src.at[layer]
            res.append(pltpu.make_async_copy(src, dst.at[j], lsem(t)))
            for k, chip in enumerate(_other_chips(x, y)):
                res.append(_remote(src, dst.at[j], rsem(3 * t + k), (*chip, c)))
        return res

    return _Job([a for a, _ in sources], [_sds((N_CHIPS,) + s, BF16) for s in shapes], 3 * n, n, make)


def _job_swap(g):
    _, rows, cols = g.shape
    half = rows // 2

    def make(ins, outs, rsem, lsem):
        x, y, c = _coords()
        return [_remote(ins[0].at[:, pl.ds((1 - c) * half, half), :], outs[0], rsem(0), (x, y, 1 - c))]

    return _Job([g], [_sds((N_CHIPS, half, cols))], 1, 0, make)


def _job_scatter(p):
    def make(ins, outs, rsem, lsem):
        x, y, c = _coords()
        return [_remote(ins[0].at[2 * chip[0] + chip[1]], outs[0].at[k], rsem(k), (*chip, c))
                for k, chip in enumerate(_other_chips(x, y))]

    return _Job([p], [_sds((3,) + p.shape[1:])], 3, 0, make)


def _job_join(f):
    half, cols = f.shape

    def make(ins, outs, rsem, lsem):
        x, y, c = _coords()
        dst = outs[0].at[pl.ds(c * half, half), :]
        return [pltpu.make_async_copy(ins[0], dst, lsem(0)), _remote(ins[0], dst, rsem(0), (x, y, 1 - c))]

    return _Job([f], [_sds((2 * half, cols))], 1, 1, make)


def _pallas(body, *, name, grid, in_specs, out_specs, out_shape, operands, vmem_mib, jobs=()):
    in_specs, out_specs, out_shape = list(in_specs), list(out_specs), list(out_shape)
    n_in, n_out = len(in_specs), len(out_specs)
    j_in = [a for j in jobs for a in j.inputs]
    j_out = [s for j in jobs for s in j.out_shapes]
    n_rem = max(1, sum(j.n_remote for j in jobs))
    n_loc = max(1, sum(j.n_local for j in jobs))
    scratch = [pltpu.SemaphoreType.DMA((n_rem,)), pltpu.SemaphoreType.DMA((n_rem,)),
               pltpu.SemaphoreType.DMA((n_loc,))] if jobs else []

    def wrapped(*refs):
        ins = refs[:n_in]
        jin = refs[n_in:n_in + len(j_in)]
        outs = refs[n_in + len(j_in):n_in + len(j_in) + n_out]
        jout = refs[n_in + len(j_in) + n_out:n_in + len(j_in) + n_out + len(j_out)]

        def copies():
            send, recv, loc = refs[-3:]
            res, a, b, r, l = [], 0, 0, 0, 0
            for j in jobs:
                res += j.make(jin[a:a + len(j.inputs)], jout[b:b + len(j.out_shapes)],
                              lambda k, r=r: (send.at[r + k], recv.at[r + k]), lambda k, l=l: loc.at[l + k])
                a, b, r, l = a + len(j.inputs), b + len(j.out_shapes), r + j.n_remote, l + j.n_local
            return res

        if jobs:
            first = last = None
            for d, n in enumerate(grid):
                f, e = pl.program_id(d) == 0, pl.program_id(d) == n - 1
                first, last = (f, e) if first is None else (first & f, last & e)

            @pl.when(first)
            def _():
                for cp in copies():
                    cp.start()

        body(*ins, *outs)

        if jobs:
            @pl.when(last)
            def _():
                for cp in copies():
                    cp.wait()

    return pl.pallas_call(
        wrapped, name=name, grid=grid,
        in_specs=in_specs + [ANY] * len(j_in), out_specs=out_specs + [ANY] * len(j_out),
        out_shape=out_shape + j_out, scratch_shapes=scratch,
        compiler_params=pltpu.CompilerParams(dimension_semantics=("arbitrary",) * len(grid),
                                             vmem_limit_bytes=vmem_mib * MIB),
    )(*operands, *j_in)


def _comm(jobs, name):
    j_in = [a for j in jobs for a in j.inputs]
    j_out = [s for j in jobs for s in j.out_shapes]
    n_rem = max(1, sum(j.n_remote for j in jobs))
    n_loc = max(1, sum(j.n_local for j in jobs))

    def body(*refs):
        jin, jout = refs[:len(j_in)], refs[len(j_in):len(j_in) + len(j_out)]
        send, recv, loc = refs[-3:]
        res, a, b, r, l = [], 0, 0, 0, 0
        for j in jobs:
            res += j.make(jin[a:a + len(j.inputs)], jout[b:b + len(j.out_shapes)],
                          lambda k, r=r: (send.at[r + k], recv.at[r + k]), lambda k, l=l: loc.at[l + k])
            a, b, r, l = a + len(j.inputs), b + len(j.out_shapes), r + j.n_remote, l + j.n_local
        for cp in res:
            cp.start()
        for cp in res:
            cp.wait()

    return pl.pallas_call(
        body, name=name, in_specs=[ANY] * len(j_in), out_specs=[ANY] * len(j_out), out_shape=j_out,
        scratch_shapes=[pltpu.SemaphoreType.DMA((n_rem,)), pltpu.SemaphoreType.DMA((n_rem,)),
                        pltpu.SemaphoreType.DMA((n_loc,))],
    )(*j_in)


def _sigmoid(x):
    return 1.0 / (1.0 + jnp.exp(-x))


def _lo_mask(shape):
    return lax.broadcasted_iota(jnp.int32, shape, len(shape) - 1) < HEAD_DIM


def _half_sum(x, lo):
    a = jnp.sum(jnp.where(lo, x, 0.0), axis=-1, keepdims=True)
    b = jnp.sum(jnp.where(lo, 0.0, x), axis=-1, keepdims=True)
    return jnp.where(lo, a, b)


def _half_rms_scale(x, lo):
    return lax.rsqrt(_half_sum(x * x, lo) * (1.0 / HEAD_DIM) + EPS)


def _stack_heads(pairs, lo):
    return jnp.concatenate([jnp.where(lo, t, 0.0) for t in pairs] + [jnp.where(lo, 0.0, t) for t in pairs], axis=0)


def _unstack_pair(stack, p, lo):
    return jnp.where(lo, stack[BLOCK * p:BLOCK * (p + 1)], stack[BLOCK * (4 + p):BLOCK * (5 + p)])


def _attention_probs(q_stack, kn, bias_ref, sink_ref):
    rows = N_HEADS * BLOCK
    s = lax.dot_general(q_stack, kn, NT_DIMS, preferred_element_type=F32) + bias_ref[...]
    sink = jnp.concatenate([jnp.full((BLOCK, BLOCK), sink_ref[h], F32) for h in range(N_HEADS)], axis=0)
    cols = [s[:, BLOCK * j:BLOCK * (j + 1)] for j in range(3)]
    top = jnp.max(jnp.maximum(jnp.maximum(cols[0], cols[1]), cols[2]), axis=-1, keepdims=True)
    m = jnp.maximum(jnp.broadcast_to(top, (rows, BLOCK)), sink)
    e = [jnp.exp(c - m) for c in cols]
    es = jnp.exp(sink - m)
    inv = 1.0 / (jnp.broadcast_to(jnp.sum((e[0] + e[1]) + e[2], axis=-1, keepdims=True), (rows, BLOCK)) + es)
    return jnp.concatenate([c * inv for c in e], axis=1), es * inv


def _kv_rows(cur_ref, pkv_ref, nkv_ref):
    k = jnp.concatenate([pkv_ref[:, 0:D_KV], cur_ref[:, C_K:C_K + D_KV], nkv_ref[:, 0:D_KV]], axis=0)
    v = jnp.concatenate([pkv_ref[:, D_KV:2 * D_KV], cur_ref[:, C_V:C_V + D_KV], nkv_ref[:, D_KV:2 * D_KV]], axis=0)
    return k, v


def _overlap_add(parts):
    blocks = []
    for j in range(SUB + 2):
        terms = [parts[b][BLOCK * (j - b):BLOCK * (j - b + 1)] for b in range(SUB) if 0 <= j - b <= 2]
        total = terms[0]
        for t in terms[1:]:
            total = total + t
        blocks.append(total)
    return jnp.concatenate(blocks, axis=0)


def _mix_specs(nt):
    cur = pl.BlockSpec((TILE, D_IN), lambda i: (i, 0))
    kv_col = C_K // (2 * D_KV)
    pkv = pl.BlockSpec((BLOCK, 2 * D_KV), lambda i: (jnp.maximum(i * SUB - 1, 0), kv_col))
    nkv = pl.BlockSpec((BLOCK, 2 * D_KV), lambda i: (jnp.minimum((i + 1) * SUB, nt * SUB - 1), kv_col))
    table = (None, N_HEADS * BLOCK, 3 * BLOCK)
    first = pl.BlockSpec(table, lambda i: (jnp.where(i == 0, 0, 1), 0, 0))
    inner = pl.BlockSpec(table, lambda i: (1, 0, 0))
    last = pl.BlockSpec(table, lambda i: (jnp.where(i == nt - 1, 2, 1), 0, 0))
    return cur, pkv, nkv, [first] + [inner] * (SUB - 2) + [last]


def _proj_fwd(x, ng, scale, shift, wt, name, jobs=()):
    s = x.shape[0]
    ts = min(512, s)

    def body(x_ref, ng_ref, sc_ref, sh_ref, w_ref, o_ref):
        xv = x_ref[...]
        r = lax.rsqrt(jnp.mean(xv * xv, axis=-1, keepdims=True) + EPS)
        h = ((xv * r) * ng_ref[...]) * (1.0 + sc_ref[...]) + sh_ref[...]
        o_ref[...] = lax.dot_general(h.astype(BF16), w_ref[...], NT_DIMS, preferred_element_type=F32)

    vec = _full((1, D_MODEL))
    return _pallas(
        body, name=name, grid=(s // ts,),
        in_specs=[pl.BlockSpec((ts, D_MODEL), lambda i: (i, 0)), vec, vec, vec, _full((D_IN, D_MODEL))],
        out_specs=[pl.BlockSpec((ts, D_IN), lambda i: (i, 0))], out_shape=[_sds((s, D_IN))],
        operands=(x, ng, scale, shift, wt), vmem_mib=48, jobs=jobs)


def _mix_fwd(proj, bias, sink, qg2, kg2, ws, bsp, name, jobs=()):
    s = proj.shape[0]
    nt = s // TILE

    def body(sink_ref, cur_ref, pkv_ref, nkv_ref, *rest):
        bias_refs = rest[:SUB]
        qg_ref, kg_ref, ws_ref, bsp_ref, y_ref = rest[SUB:]
        lo = _lo_mask((BLOCK, BLOCK))
        lo_kv = _lo_mask((TILE + 2 * BLOCK, BLOCK))
        k_all, v_all = _kv_rows(cur_ref, pkv_ref, nkv_ref)
        kn_all = ((k_all * _half_rms_scale(k_all, lo_kv)) * kg_ref[...]).astype(BF16)
        vb_all = v_all.astype(BF16)
        for b in range(SUB):
            rows = slice(BLOCK * b, BLOCK * (b + 1))
            window = slice(BLOCK * b, BLOCK * (b + 3))
            qn = []
            for p in range(4):
                q = cur_ref[rows, C_Q + BLOCK * p:C_Q + BLOCK * (p + 1)]
                qn.append(((q * _half_rms_scale(q, lo)) * qg_ref[...]) * 0.125)
            q_stack = _stack_heads(qn, lo).astype(BF16)
            prob, _ = _attention_probs(q_stack, kn_all[window], bias_refs[b], sink_ref)
            o_stack = jnp.dot(prob.astype(BF16), vb_all[window], preferred_element_type=F32)
            for p in range(4):
                g = cur_ref[rows, C_GA + BLOCK * p:C_GA + BLOCK * (p + 1)]
                y_ref[rows, BLOCK * p:BLOCK * (p + 1)] = (_unstack_pair(o_stack, p, lo) * (g * _sigmoid(g))).astype(BF16)
            for p in range(4):
                vg = cur_ref[rows, C_VG + BLOCK * p:C_VG + BLOCK * (p + 1)]
                vn = (vg * _half_rms_scale(vg, lo)).astype(BF16)
                sv = jnp.where(lo, jnp.dot(ws_ref[2 * p], vn, preferred_element_type=F32),
                               jnp.dot(ws_ref[2 * p + 1], vn, preferred_element_type=F32)) + bsp_ref[p]
                u = cur_ref[rows, C_U + BLOCK * p:C_U + BLOCK * (p + 1)]
                g = cur_ref[rows, C_GG + BLOCK * p:C_GG + BLOCK * (p + 1)]
                y_ref[rows, D_ATTN + BLOCK * p:D_ATTN + BLOCK * (p + 1)] = ((u * sv) * (g * _sigmoid(g))).astype(BF16)

    cur, pkv, nkv, bias_specs = _mix_specs(nt)
    return _pallas(
        body, name=name, grid=(nt,),
        in_specs=[pl.BlockSpec(memory_space=pltpu.SMEM), cur, pkv, nkv, *bias_specs, _full((1, BLOCK)), _full((1, BLOCK)),
                  _full((8, BLOCK, BLOCK)), _full((4, BLOCK, BLOCK))],
        out_specs=[pl.BlockSpec((TILE, D_MODEL), lambda i: (i, 0))], out_shape=[_sds((s, D_MODEL), BF16)],
        operands=(sink, proj, proj, proj, *([bias] * SUB), qg2, kg2, ws, bsp), vmem_mib=48, jobs=jobs)


def _out_fwd(y, x, gate, w_out, name):
    s = x.shape[0]
    ts = min(512, s)

    def body(y_ref, x_ref, g_ref, w_ref, o_ref):
        o_ref[...] = x_ref[...] + g_ref[...] * jnp.dot(y_ref[...], w_ref[...], preferred_element_type=F32)

    row = pl.BlockSpec((ts, D_MODEL), lambda i: (i, 0))
    return _pallas(
        body, name=name, grid=(s // ts,), in_specs=[row, row, _full((1, D_MODEL)), _full((D_MODEL, D_MODEL))],
        out_specs=[row], out_shape=[_sds((s, D_MODEL))], operands=(y, x, gate, w_out), vmem_mib=32)


def _out_fwd_loss(y, x, gate, w_out, target, name):
    s = x.shape[0]
    ts = min(512, s)

    def body(y_ref, x_ref, g_ref, w_ref, t_ref, dx_ref, sq_ref):
        @pl.when(pl.program_id(0) == 0)
        def _():
            sq_ref[...] = jnp.zeros_like(sq_ref)

        out = x_ref[...] + g_ref[...] * jnp.dot(y_ref[...], w_ref[...], preferred_element_type=F32)
        diff = out - t_ref[...]
        dx_ref[...] = diff * (1.0 / D_MODEL)
        per_token = jnp.sum(diff * diff, axis=-1, keepdims=True) * (1.0 / D_MODEL)
        sq_ref[...] += jnp.sum(per_token, axis=0, keepdims=True)

    row = pl.BlockSpec((ts, D_MODEL), lambda i: (i, 0))
    return _pallas(
        body, name=name, grid=(s // ts,), in_specs=[row, row, _full((1, D_MODEL)), _full((D_MODEL, D_MODEL)), row],
        out_specs=[row, _full((1, 1))], out_shape=[_sds((s, D_MODEL)), _sds((1, 1))],
        operands=(y, x, gate, w_out, target), vmem_mib=40)


def _out_bwd(dxo, y, gate, w_out, name, jobs=()):
    s = dxo.shape[0]
    ts = min(512, s)

    def body(dx_ref, y_ref, g_ref, w_ref, dy_ref, gw_ref, dg_ref):
        @pl.when(pl.program_id(0) == 0)
        def _():
            gw_ref[...] = jnp.zeros_like(gw_ref)
            dg_ref[...] = jnp.zeros_like(dg_ref)

        dx = dx_ref[...]
        do = (dx * g_ref[...]).astype(BF16)
        yv = y_ref[...]
        dy_ref[...] = lax.dot_general(do, w_ref[...], NT_DIMS, preferred_element_type=F32)
        gw_ref[...] += lax.dot_general(yv, do, TN_DIMS, preferred_element_type=F32)
        o = jnp.dot(yv, w_ref[...], preferred_element_type=F32)
        dg_ref[...] += jnp.sum(dx * o, axis=0, keepdims=True)

    row = pl.BlockSpec((ts, D_MODEL), lambda i: (i, 0))
    return _pallas(
        body, name=name, grid=(s // ts,), in_specs=[row, row, _full((1, D_MODEL)), _full((D_MODEL, D_MODEL))],
        out_specs=[row, _full((D_MODEL, D_MODEL)), _full((1, D_MODEL))],
        out_shape=[_sds((s, D_MODEL)), _sds((D_MODEL, D_MODEL)), _sds((1, D_MODEL))],
        operands=(dxo, y, gate, w_out), vmem_mib=48, jobs=jobs)


def _mix_bwd(dy, proj, bias, sink, qg2, kg2, ws, wst, bsp, name, jobs=()):
    s = proj.shape[0]
    nt = s // TILE

    def body(sink_ref, dy_ref, cur_ref, pkv_ref, nkv_ref, *rest):
        bias_refs = rest[:SUB]
        (qg_ref, kg_ref, ws_ref, wst_ref, bsp_ref,
         dpb_ref, dkv_ref, p0_ref, p2_ref, dqg_ref, dkg_ref, dsink_ref, dws_ref, dbsp_ref) = rest[SUB:]

        def put(rows, col, value):
            dpb_ref[rows, col:col + BLOCK] = value.astype(BF16)

        @pl.when(pl.program_id(0) == 0)
        def _():
            dqg_ref[...] = jnp.zeros_like(dqg_ref)
            dkg_ref[...] = jnp.zeros_like(dkg_ref)
            dsink_ref[...] = jnp.zeros_like(dsink_ref)
            dws_ref[...] = jnp.zeros_like(dws_ref)
            dbsp_ref[...] = jnp.zeros_like(dbsp_ref)

        lo = _lo_mask((BLOCK, BLOCK))
        lo_kv = _lo_mask((TILE + 2 * BLOCK, BLOCK))
        lane_row = lax.broadcasted_iota(jnp.int32, (1, BLOCK), 1)
        qg = qg_ref[...]
        kg = kg_ref[...]

        k_all, v_all = _kv_rows(cur_ref, pkv_ref, nkv_ref)
        rk = _half_rms_scale(k_all, lo_kv)
        khat = k_all * rk
        kn_all = (khat * kg).astype(BF16)
        vb_all = v_all.astype(BF16)

        dkn_parts, dv_parts = [], []
        dsink = jnp.zeros((1, BLOCK), F32)
        dqg = jnp.zeros((1, BLOCK), F32)
        for b in range(SUB):
            rows = slice(BLOCK * b, BLOCK * (b + 1))
            window = slice(BLOCK * b, BLOCK * (b + 3))
            kn, vb = kn_all[window], vb_all[window]

            qhat, rq = [], []
            for p in range(4):
                q = cur_ref[rows, C_Q + BLOCK * p:C_Q + BLOCK * (p + 1)]
                r = _half_rms_scale(q, lo)
                rq.append(r)
                qhat.append(q * r)
            q_stack = _stack_heads([(qh * qg) * 0.125 for qh in qhat], lo).astype(BF16)
            prob, psink = _attention_probs(q_stack, kn, bias_refs[b], sink_ref)
            pb = prob.astype(BF16)
            o_stack = jnp.dot(pb, vb, preferred_element_type=F32)

            dout = []
            for p in range(4):
                g = cur_ref[rows, C_GA + BLOCK * p:C_GA + BLOCK * (p + 1)]
                sg = _sigmoid(g)
                dya = dy_ref[rows, BLOCK * p:BLOCK * (p + 1)]
                attn = _unstack_pair(o_stack, p, lo)
                put(rows, C_GA + BLOCK * p, dya * attn * (sg * (1.0 + g * (1.0 - sg))))
                dout.append(dya * (g * sg))
            do_stack = _stack_heads(dout, lo).astype(BF16)
            dp = lax.dot_general(do_stack, vb, NT_DIMS, preferred_element_type=F32)
            delta = jnp.sum(prob * dp, axis=-1, keepdims=True)
            dsb = (prob * (dp - delta)).astype(BF16)

            csink = -(psink * delta)
            for h in range(N_HEADS):
                tot = jnp.sum(csink[BLOCK * h:BLOCK * (h + 1)], axis=0, keepdims=True)
                dsink = dsink + jnp.where(lane_row == h, tot, 0.0)

            dq_stack = jnp.dot(dsb, kn, preferred_element_type=F32) * 0.125
            dkn_parts.append(lax.dot_general(dsb, q_stack, TN_DIMS, preferred_element_type=F32))
            dv_parts.append(lax.dot_general(pb, do_stack, TN_DIMS, preferred_element_type=F32))

            for p in range(4):
                dqn = _unstack_pair(dq_stack, p, lo)
                qh = qhat[p]
                dqg = dqg + jnp.sum(dqn * qh, axis=0, keepdims=True)
                dqh = dqn * qg
                mean = _half_sum(dqh * qh, lo) * (1.0 / HEAD_DIM)
                put(rows, C_Q + BLOCK * p, rq[p] * (dqh - qh * mean))

            for p in range(4):
                vg = cur_ref[rows, C_VG + BLOCK * p:C_VG + BLOCK * (p + 1)]
                r = _half_rms_scale(vg, lo)
                vnf = vg * r
                vn = vnf.astype(BF16)
                sv = jnp.where(lo, jnp.dot(ws_ref[2 * p], vn, preferred_element_type=F32),
                               jnp.dot(ws_ref[2 * p + 1], vn, preferred_element_type=F32)) + bsp_ref[p]
                u = cur_ref[rows, C_U + BLOCK * p:C_U + BLOCK * (p + 1)]
                g = cur_ref[rows, C_GG + BLOCK * p:C_GG + BLOCK * (p + 1)]
                sg = _sigmoid(g)
                dym = dy_ref[rows, D_ATTN + BLOCK * p:D_ATTN + BLOCK * (p + 1)]
                put(rows, C_GG + BLOCK * p, dym * (u * sv) * (sg * (1.0 + g * (1.0 - sg))))
                dgm = dym * (g * sg)
                put(rows, C_U + BLOCK * p, dgm * sv)
                dsv = dgm * u
                dsv_a = jnp.where(lo, dsv, 0.0)
                dsv_b = jnp.where(lo, 0.0, dsv)
                dws_ref[2 * p] += lax.dot_general(dsv_a.astype(BF16), vn, NT_DIMS, preferred_element_type=F32)
                dws_ref[2 * p + 1] += lax.dot_general(dsv_b.astype(BF16), vn, NT_DIMS, preferred_element_type=F32)
                dbsp_ref[p] += jnp.where(lo, jnp.sum(dsv_a, axis=-1, keepdims=True), jnp.sum(dsv_b, axis=-1, keepdims=True))
                dsvb = dsv.astype(BF16)
                dvn = jnp.where(lo, jnp.dot(wst_ref[2 * p], dsvb, preferred_element_type=F32),
                                jnp.dot(wst_ref[2 * p + 1], dsvb, preferred_element_type=F32))
                mean = _half_sum(dvn * vnf, lo) * (1.0 / HEAD_DIM)
                put(rows, C_VG + BLOCK * p, r * (dvn - vnf * mean))

        dsink_ref[...] += dsink
        dqg = jnp.broadcast_to(dqg, (8, BLOCK))
        dqg_ref[...] += dqg + pltpu.roll(dqg, HEAD_DIM, 1)

        dkn = _overlap_add(dkn_parts)
        dv = _overlap_add(dv_parts)
        dkg = jnp.broadcast_to(jnp.sum(dkn * khat, axis=0, keepdims=True), (8, BLOCK))
        dkg_ref[...] += dkg + pltpu.roll(dkg, HEAD_DIM, 1)
        dkh = dkn * kg
        dk = rk * (dkh - khat * (_half_sum(dkh * khat, lo_kv) * (1.0 / HEAD_DIM)))
        dpb_ref[:, C_K:C_GA] = jnp.zeros((TILE, 2 * D_KV), BF16)
        dkv_ref[:, 0:D_KV] = dk[BLOCK:BLOCK + TILE]
        dkv_ref[:, D_KV:2 * D_KV] = dv[BLOCK:BLOCK + TILE]
        p0_ref[:, 0:D_KV] = dk[0:BLOCK]
        p0_ref[:, D_KV:2 * D_KV] = dv[0:BLOCK]
        p2_ref[:, 0:D_KV] = dk[BLOCK + TILE:]
        p2_ref[:, D_KV:2 * D_KV] = dv[BLOCK + TILE:]

    cur, pkv, nkv, bias_specs = _mix_specs(nt)
    kv_blk = (BLOCK, 2 * D_KV)
    return _pallas(
        body, name=name, grid=(nt,),
        in_specs=[pl.BlockSpec(memory_space=pltpu.SMEM), pl.BlockSpec((TILE, D_MODEL), lambda i: (i, 0)),
                  cur, pkv, nkv, *bias_specs, _full((1, BLOCK)), _full((1, BLOCK)),
                  _full((8, BLOCK, BLOCK)), _full((8, BLOCK, BLOCK)), _full((4, BLOCK, BLOCK))],
        out_specs=[cur, pl.BlockSpec((TILE, 2 * D_KV), lambda i: (i, 0)),
                   pl.BlockSpec(kv_blk, lambda i: ((i + nt - 1) % nt, 0)),
                   pl.BlockSpec(kv_blk, lambda i: ((i + 1) % nt, 0)),
                   _full((8, BLOCK)), _full((8, BLOCK)), _full((1, BLOCK)),
                   _full((8, BLOCK, BLOCK)), _full((4, BLOCK, BLOCK))],
        out_shape=[_sds((s, D_IN), BF16), _sds((s, 2 * D_KV)), _sds((nt * BLOCK, 2 * D_KV)), _sds((nt * BLOCK, 2 * D_KV)),
                   _sds((8, BLOCK)), _sds((8, BLOCK)), _sds((1, BLOCK)),
                   _sds((8, BLOCK, BLOCK)), _sds((4, BLOCK, BLOCK))],
        operands=(sink, dy, proj, proj, proj, *([bias] * SUB), qg2, kg2, ws, wst, bsp), vmem_mib=56, jobs=jobs)


def _w_in_grad(dpb, dkv, p0, p2, x, ng, scale, shift, name, jobs=()):
    s = x.shape[0]
    ts = min(2 * TILE, s)
    tiles = ts // TILE

    def body(dpb_ref, dkv_ref, p0_ref, p2_ref, x_ref, ng_ref, sc_ref, sh_ref, gw_ref, dkvb_ref):
        @pl.when(pl.program_id(0) == 0)
        def _():
            gw_ref[...] = jnp.zeros_like(gw_ref)

        xv = x_ref[...]
        r = lax.rsqrt(jnp.mean(xv * xv, axis=-1, keepdims=True) + EPS)
        h = (((xv * r) * ng_ref[...]) * (1.0 + sc_ref[...]) + sh_ref[...]).astype(BF16)
        for t in range(tiles):
            halo = slice(BLOCK * t, BLOCK * (t + 1))
            first = slice(TILE * t, TILE * t + BLOCK)
            last = slice(TILE * (t + 1) - BLOCK, TILE * (t + 1))
            dkvb_ref[first, :] = (dkv_ref[first, :] + p2_ref[halo, :]).astype(BF16)
            if SUB > 2:
                inner = slice(TILE * t + BLOCK, TILE * (t + 1) - BLOCK)
                dkvb_ref[inner, :] = dkv_ref[inner, :].astype(BF16)
            dkvb_ref[last, :] = (dkv_ref[last, :] + p0_ref[halo, :]).astype(BF16)
        gw_ref[...] += lax.dot_general(dpb_ref[...], h, TN_DIMS, preferred_element_type=F32)
        gw_ref[C_K:C_GA, :] += lax.dot_general(dkvb_ref[...], h, TN_DIMS, preferred_element_type=F32)

    kv = pl.BlockSpec((ts, 2 * D_KV), lambda i: (i, 0))
    halo = pl.BlockSpec((tiles * BLOCK, 2 * D_KV), lambda i: (i, 0))
    vec = _full((1, D_MODEL))
    return _pallas(
        body, name=name, grid=(s // ts,),
        in_specs=[pl.BlockSpec((ts, D_IN), lambda i: (i, 0)), kv, halo, halo,
                  pl.BlockSpec((ts, D_MODEL), lambda i: (i, 0)), vec, vec, vec],
        out_specs=[_full((D_IN, D_MODEL)), kv], out_shape=[_sds((D_IN, D_MODEL)), _sds((s, 2 * D_KV), BF16)],
        operands=(dpb, dkv, p0, p2, x, ng, scale, shift), vmem_mib=56, jobs=jobs)


def _proj_bwd(dpb, dkvb, x, dxo, ng, scale, wt, name, jobs=()):
    s = x.shape[0]
    ts = min(512, s)

    def body(dpb_ref, dkvb_ref, x_ref, dxo_ref, ng_ref, sc_ref, w_ref, dxi_ref, dsh_ref, dsc_ref, dng_ref):
        @pl.when(pl.program_id(0) == 0)
        def _():
            dsh_ref[...] = jnp.zeros_like(dsh_ref)
            dsc_ref[...] = jnp.zeros_like(dsc_ref)
            dng_ref[...] = jnp.zeros_like(dng_ref)

        dh = (jnp.dot(dpb_ref[...], w_ref[...], preferred_element_type=F32)
              + jnp.dot(dkvb_ref[...], w_ref[C_K:C_GA, :], preferred_element_type=F32))

        xv = x_ref[...]
        r = lax.rsqrt(jnp.mean(xv * xv, axis=-1, keepdims=True) + EPS)
        xn = xv * r
        ngv = ng_ref[...]
        sc1 = 1.0 + sc_ref[...]
        dsh_ref[...] += jnp.sum(dh, axis=0, keepdims=True)
        dsc_ref[...] += jnp.sum(dh * (xn * ngv), axis=0, keepdims=True)
        dh1 = dh * sc1
        dng_ref[...] += jnp.sum(dh1 * xn, axis=0, keepdims=True)
        dxn = dh1 * ngv
        dxi_ref[...] = r * (dxn - xn * jnp.mean(dxn * xn, axis=-1, keepdims=True)) + dxo_ref[...]

    row = pl.BlockSpec((ts, D_MODEL), lambda i: (i, 0))
    vec = _full((1, D_MODEL))
    return _pallas(
        body, name=name, grid=(s // ts,),
        in_specs=[pl.BlockSpec((ts, D_IN), lambda i: (i, 0)), pl.BlockSpec((ts, 2 * D_KV), lambda i: (i, 0)),
                  row, row, vec, vec, _full((D_IN, D_MODEL))],
        out_specs=[row, vec, vec, vec],
        out_shape=[_sds((s, D_MODEL)), _sds((1, D_MODEL)), _sds((1, D_MODEL)), _sds((1, D_MODEL))],
        operands=(dpb, dkvb, x, dxo, ng, scale, wt), vmem_mib=48, jobs=jobs)


def _pair_sum(g, r, c_idx, name):
    _, rows, cols = g.shape
    half = rows // 2

    def body(c_ref, g_ref, r_ref, o_ref):
        o_ref[...] = g_ref[...] + r_ref[...]

    blk = (None, half, cols)
    return pl.pallas_call(
        body, name=name,
        grid_spec=pltpu.PrefetchScalarGridSpec(
            num_scalar_prefetch=1, grid=(N_CHIPS,),
            in_specs=[pl.BlockSpec(blk, lambda j, c: (j, c[0], 0)), pl.BlockSpec(blk, lambda j, c: (j, 0, 0))],
            out_specs=pl.BlockSpec(blk, lambda j, c: (j, 0, 0))),
        out_shape=_sds((N_CHIPS, half, cols)),
        compiler_params=pltpu.CompilerParams(dimension_semantics=("arbitrary",), vmem_limit_bytes=32 * MIB),
    )(c_idx, g, r)


def _chip_sum(p, r, j_idx, name):
    _, rows, cols = p.shape
    tr = rows // 2

    def body(j_ref, p_ref, r_ref, o_ref):
        o_ref[...] = ((p_ref[...] + r_ref[0]) + r_ref[1]) + r_ref[2]

    return pl.pallas_call(
        body, name=name,
        grid_spec=pltpu.PrefetchScalarGridSpec(
            num_scalar_prefetch=1, grid=(2,),
            in_specs=[pl.BlockSpec((None, tr, cols), lambda t, j: (j[0], t, 0)),
                      pl.BlockSpec((3, tr, cols), lambda t, j: (0, t, 0))],
            out_specs=pl.BlockSpec((tr, cols), lambda t, j: (t, 0))),
        out_shape=_sds((rows, cols)),
        compiler_params=pltpu.CompilerParams(dimension_semantics=("arbitrary",), vmem_limit_bytes=32 * MIB),
    )(j_idx, p, r)


def _cast_permute_w_in(wt, src_chunks):
    def body(t_ref, w_ref, o_ref):
        o_ref[...] = w_ref[...].astype(BF16)

    return pl.pallas_call(
        body, name="cast_permute_w_in",
        grid_spec=pltpu.PrefetchScalarGridSpec(
            num_scalar_prefetch=1, grid=(DEPTH, N_CHUNKS),
            in_specs=[pl.BlockSpec((None, CHUNK_ROWS, D_MODEL), lambda l, t, tbl: (l, tbl[t], 0))],
            out_specs=pl.BlockSpec((None, CHUNK_ROWS, D_MODEL), lambda l, t, tbl: (l, t, 0))),
        out_shape=_sds((DEPTH, W_IN_BLK, D_MODEL), BF16),
        compiler_params=pltpu.CompilerParams(dimension_semantics=("arbitrary", "arbitrary")),
    )(src_chunks, wt)


def _gather_inputs(c, w_out, wloc_in):
    half = W_IN_BLK // 2

    def body(c_ref, wout_ref, wloc_ref, call_ref, woutb_ref, w0_ref, send_sems, recv_sems, local_sem):
        x, y, cc = _coords()
        j = 2 * x + y
        b = 2 * j + cc
        sib = (x, y, 1 - cc)
        woutb_ref[...] = wout_ref[...].astype(BF16)
        call_ref[b] = c_ref[...]
        chips = _other_chips(x, y)

        def sems(k):
            return send_sems.at[k], recv_sems.at[k]

        def half_rows(chip_index):
            return w0_ref.at[chip_index, pl.ds(cc * half, half), :]

        mine = pltpu.make_async_copy(wloc_ref.at[0], w0_ref.at[j], local_sem)
        mine.start()
        first = [_remote(wloc_ref.at[0, pl.ds(cc * half, half), :], half_rows(j), sems(k), (*chip, cc))
                 for k, chip in enumerate(chips)]
        k = 3
        rest = []
        for fx in (0, 1):
            for fy in (0, 1):
                for fc in (0, 1):
                    if fx or fy or fc:
                        dev = (1 - x if fx else x, 1 - y if fy else y, 1 - cc if fc else cc)
                        rest.append(_remote(call_ref.at[b], call_ref.at[b], sems(k), dev))
                        k += 1
        for cp in first + rest:
            cp.start()
        passed = []
        for k, chip in enumerate(chips):
            jk = 2 * chip[0] + chip[1]
            first[k].wait_recv()
            passed.append(_remote(half_rows(jk), half_rows(jk), sems(10 + k), sib))
            passed[k].start()
        for cp in first:
            cp.wait_send()
        for cp in rest + passed:
            cp.wait()
        mine.wait()

    return pl.pallas_call(
        body, name="gather_inputs", in_specs=[VMEM, VMEM, ANY], out_specs=[VMEM, VMEM, ANY],
        out_shape=[_sds((N_DEV, 1, D_MODEL)), _sds((DEPTH, W_OUT_BLK, D_MODEL), BF16),
                   _sds((N_CHIPS, W_IN_BLK, D_MODEL), BF16)],
        scratch_shapes=[pltpu.SemaphoreType.DMA((13,)), pltpu.SemaphoreType.DMA((13,)), pltpu.SemaphoreType.DMA],
        compiler_params=pltpu.CompilerParams(vmem_limit_bytes=32 * MIB),
    )(c, w_out, wloc_in)


def _ada_rows(c_all, w_ada, b_blk):
    def body(c_ref, w_ref, b_ref, o_ref, cond_ref):
        cv = c_ref[...]
        cond = (cv * _sigmoid(cv)).astype(BF16)
        cond_ref[...] = cond.astype(F32)
        for l in range(DEPTH):
            o_ref[:, l, :] = jnp.dot(cond, w_ref[l].astype(BF16), preferred_element_type=F32) + b_ref[l:l + 1, :]

    return pl.pallas_call(
        body, name="ada_rows", in_specs=[VMEM, VMEM, VMEM], out_specs=[VMEM, VMEM],
        out_shape=[_sds((N_DEV, DEPTH, W_ADA_BLK)), _sds((N_DEV, D_MODEL))],
        compiler_params=pltpu.CompilerParams(vmem_limit_bytes=32 * MIB),
    )(c_all, w_ada, b_blk)


def _exchange_ada(part):
    def body(part_ref, out_ref, send_sems, recv_sems):
        x, y, cc = _coords()
        j = 2 * x + y
        out_ref[j] = part_ref[2 * j + cc]
        copies = []
        for k, chip in enumerate(_other_chips(x, y)):
            b_dst = 4 * chip[0] + 2 * chip[1] + cc
            copies.append(_remote(part_ref.at[b_dst], out_ref.at[j], (send_sems.at[k], recv_sems.at[k]), (*chip, cc)))
        for cp in copies:
            cp.start()
        for cp in copies:
            cp.wait()

    return pl.pallas_call(
        body, name="exchange_ada", in_specs=[VMEM], out_specs=VMEM,
        out_shape=_sds((N_CHIPS, DEPTH, W_ADA_BLK)),
        scratch_shapes=[pltpu.SemaphoreType.DMA((3,)), pltpu.SemaphoreType.DMA((3,))],
    )(part)


def _all_gather_rows(blk):
    m_per, n = blk.shape

    def body(x_ref, out_ref, send_sems, recv_sems, local_sem):
        x, y, c = _coords()
        me, sibling = (x, y, c), (x, y, 1 - c)
        chips = _other_chips(x, y)

        def rows(px, py, pc):
            return out_ref.at[pl.ds((4 * px + 2 * py + pc) * m_per, m_per), :]

        def copy(k, block, to, src=None):
            return _remote(rows(*block) if src is None else src, rows(*block), (send_sems.at[k], recv_sems.at[k]), to)

        mine = pltpu.make_async_copy(x_ref, rows(*me), local_sem)
        mine.start()
        first = [copy(0, me, sibling, src=x_ref)]
        first += [copy(1 + j, me, (*chip, c), src=x_ref) for j, chip in enumerate(chips)]
        for cp in first:
            cp.start()
        passed = [copy(4 + j, (*chip, c), sibling) for j, chip in enumerate(chips)]
        for j, chip in enumerate(chips):
            copy(1 + j, (*chip, c), me).wait_recv()
            passed[j].start()
        copy(0, sibling, me).wait_recv()
        for j, chip in enumerate(chips):
            copy(4 + j, (*chip, 1 - c), me).wait_recv()
        for cp in first + passed:
            cp.wait_send()
        mine.wait()

    return pl.pallas_call(
        body, name="all_gather_small", in_specs=[VMEM], out_specs=VMEM,
        out_shape=_sds((N_DEV * m_per, n), blk.dtype),
        scratch_shapes=[pltpu.SemaphoreType.DMA((7,)), pltpu.SemaphoreType.DMA((7,)), pltpu.SemaphoreType.DMA],
        compiler_params=pltpu.CompilerParams(vmem_limit_bytes=32 * MIB),
    )(blk)


def _adamw_math(w, g, m, v):
    m = ADAM_B1 * m + (1.0 - ADAM_B1) * g
    v = ADAM_B2 * v + (1.0 - ADAM_B2) * (g * g)
    m_hat = m / (1.0 - ADAM_B1 ** ADAM_STEP)
    v_hat = v / (1.0 - ADAM_B2 ** ADAM_STEP)
    delta = -ADAM_LR * (m_hat / (jnp.sqrt(v_hat) + ADAM_EPS) + ADAM_WD * w)
    return delta, m, v


def _adamw_w_in(w, g0, g1, m, v, pos_chunks):
    def body(t_ref, w_ref, g0_ref, g1_ref, m_ref, v_ref, g_ref, d_ref, nm_ref, nv_ref):
        g = jnp.where(pl.program_id(0) == 0, g0_ref[...], g1_ref[...])
        g_ref[...] = g
        d_ref[...], nm_ref[...], nv_ref[...] = _adamw_math(w_ref[...], g, m_ref[...], v_ref[...])

    nat = pl.BlockSpec((None, CHUNK_ROWS, D_MODEL), lambda l, t, tbl: (l, t, 0))
    per = pl.BlockSpec((CHUNK_ROWS, D_MODEL), lambda l, t, tbl: (tbl[t], 0))
    return pl.pallas_call(
        body, name="adamw_w_in",
        grid_spec=pltpu.PrefetchScalarGridSpec(num_scalar_prefetch=1, grid=(DEPTH, N_CHUNKS),
                                               in_specs=[nat, per, per, nat, nat], out_specs=[nat] * 4),
        out_shape=[_sds(w.shape)] * 4,
        compiler_params=pltpu.CompilerParams(dimension_semantics=("arbitrary", "arbitrary")),
    )(pos_chunks, w, g0, g1, m, v)


def _adamw_w_out(w, g0, g1, m, v):
    def body(w_ref, g0_ref, g1_ref, m_ref, v_ref, g_ref, d_ref, nm_ref, nv_ref):
        g = jnp.where(pl.program_id(0) == 0, g0_ref[...], g1_ref[...])
        g_ref[...] = g
        d_ref[...], nm_ref[...], nv_ref[...] = _adamw_math(w_ref[...], g, m_ref[...], v_ref[...])

    blk = pl.BlockSpec((None, W_OUT_BLK, D_MODEL), lambda l: (l, 0, 0))
    gblk = _full((W_OUT_BLK, D_MODEL))
    return pl.pallas_call(
        body, name="adamw_w_out", grid=(DEPTH,), in_specs=[blk, gblk, gblk, blk, blk], out_specs=[blk] * 4,
        out_shape=[_sds(w.shape)] * 4,
        compiler_params=pltpu.CompilerParams(dimension_semantics=("arbitrary",), vmem_limit_bytes=32 * MIB),
    )(w, g0, g1, m, v)


def _w_ada_grad_adamw(cond_t, dada, w, m, v):
    _, rows, cols = w.shape
    tr = 256

    def body(ct_ref, da_ref, w_ref, m_ref, v_ref, g_ref, d_ref, nm_ref, nv_ref):
        g = jnp.dot(ct_ref[...], da_ref[...].astype(BF16), preferred_element_type=F32)
        g_ref[...] = g
        d_ref[...], nm_ref[...], nv_ref[...] = _adamw_math(w_ref[...], g, m_ref[...], v_ref[...])

    blk = pl.BlockSpec((None, tr, cols), lambda l, t: (l, t, 0))
    return pl.pallas_call(
        body, name="w_ada_grad_adamw", grid=(DEPTH, rows // tr),
        in_specs=[pl.BlockSpec((tr, BLOCK), lambda l, t: (t, 0)), pl.BlockSpec((None, BLOCK, cols), lambda l, t: (l, 0, 0)),
                  blk, blk, blk],
        out_specs=[blk] * 4, out_shape=[_sds(w.shape)] * 4,
        compiler_params=pltpu.CompilerParams(dimension_semantics=("arbitrary", "arbitrary"), vmem_limit_bytes=32 * MIB),
    )(cond_t, dada, w, m, v)


def _small_sum_adamw(gathered, w, m, v):
    def body(a_ref, w_ref, m_ref, v_ref, g_ref, d_ref, nm_ref, nv_ref):
        g = a_ref[0]
        for b in range(1, N_DEV):
            g = g + a_ref[b]
        g_ref[...] = g
        d_ref[...], nm_ref[...], nv_ref[...] = _adamw_math(w_ref[...], g, m_ref[...], v_ref[...])

    return pl.pallas_call(
        body, name="small_sum_adamw", in_specs=[VMEM] * 4, out_specs=[VMEM] * 4, out_shape=[_sds(w.shape)] * 4,
        compiler_params=pltpu.CompilerParams(vmem_limit_bytes=48 * MIB),
    )(gathered, w, m, v)


_SMALL = (("w_s", DEPTH * 8 * BLOCK), ("b_s", DEPTH * 8), ("b_ada", DEPTH * 24), ("norm_gain", DEPTH * 8),
          ("q_gain", 1), ("k_gain", 1), ("sink", 1))
_SMALL_ROWS = sum(r for _, r in _SMALL)
_SMALL_PAD = -_SMALL_ROWS % 8


def _pack_small(parts):
    rows = []
    for name, n in _SMALL:
        flat = parts[name].reshape(-1)
        rows.append(jnp.pad(flat, (0, n * 128 - flat.shape[0])).reshape(n, 128))
    rows.append(jnp.zeros((_SMALL_PAD, 128), F32))
    return jnp.concatenate(rows, axis=0)


def _unpack_small(packed, shapes):
    out, r0 = {}, 0
    for name, n in _SMALL:
        size = 1
        for d in shapes[name]:
            size *= d
        out[name] = packed[r0:r0 + n].reshape(-1)[:size].reshape(shapes[name])
        r0 += n
    return out


def _permute_heads(a, axis):
    shp = a.shape
    a = a.reshape(shp[:axis] + (2, 4, HEAD_DIM) + shp[axis + 1:])
    a = jnp.swapaxes(a, axis, axis + 1)
    return a.reshape(shp)


def _unpermute_heads(a, axis):
    shp = a.shape
    a = a.reshape(shp[:axis] + (4, 2, HEAD_DIM) + shp[axis + 1:])
    a = jnp.swapaxes(a, axis, axis + 1)
    return a.reshape(shp)


def _permute_w_out(w):
    return jnp.concatenate([_permute_heads(w[:D_ATTN], 0), w[D_ATTN:]], axis=0)


def _unpermute_w_out(w):
    return jnp.concatenate([_unpermute_heads(w[:D_ATTN], 0), w[D_ATTN:]], axis=0)


def kernel(x, c, w_ada, b_ada, norm_gain, w_in, q_gain, k_gain, sink, w_s, b_s, w_out, loss_target, m_w_ada, m_b_ada, m_norm_gain, m_w_in, m_q_gain, m_k_gain, m_sink, m_w_s, m_b_s, m_w_out, v_w_ada, v_b_ada, v_norm_gain, v_w_in, v_q_gain, v_k_gain, v_sink, v_w_s, v_b_s, v_w_out):
    ix, iy, ic = _coords()
    chip = 2 * ix + iy
    chip_idx = jnp.reshape(chip, (1,)).astype(jnp.int32)
    core_idx = jnp.reshape(ic, (1,)).astype(jnp.int32)
    src_chunks = lax.dynamic_index_in_dim(jnp.asarray(_CHUNK_SRC), chip, 0, keepdims=False)
    pos_chunks = lax.dynamic_index_in_dim(jnp.asarray(_CHUNK_POS), chip, 0, keepdims=False)
    x0, target = x[0], loss_target[0]

    wt, mt, vt = (jnp.swapaxes(a, 1, 2) for a in (w_in, m_w_in, v_w_in))
    wloc_in = _cast_permute_w_in(wt, src_chunks)
    c_all, wloc_out, w0 = _gather_inputs(c, w_out, wloc_in)
    wts = [w0.reshape(D_IN, D_MODEL), None]

    b_blk = lax.dynamic_slice_in_dim(b_ada, chip * W_ADA_BLK, W_ADA_BLK, axis=1)
    ada_part, cond = _ada_rows(c_all.reshape(N_DEV, D_MODEL), w_ada, b_blk)
    ada = jnp.moveaxis(_exchange_ada(ada_part), 0, 1).reshape(DEPTH, 3 * D_MODEL)
    shift = [ada[l:l + 1, 0:D_MODEL] for l in range(DEPTH)]
    scale = [ada[l:l + 1, D_MODEL:2 * D_MODEL] for l in range(DEPTH)]
    gate = [ada[l:l + 1, 2 * D_MODEL:] for l in range(DEPTH)]
    ng = [norm_gain[l:l + 1] for l in range(DEPTH)]

    qg2 = jnp.concatenate([q_gain, q_gain], axis=-1)
    kg2 = jnp.concatenate([k_gain, k_gain], axis=-1)
    ws_b = w_s.astype(BF16)
    wst_b = jnp.swapaxes(w_s, -1, -2).astype(BF16)
    bsp = jnp.repeat(jnp.swapaxes(b_s.reshape(DEPTH, 4, 2, BLOCK), -1, -2), HEAD_DIM, axis=-1)
    bias = jnp.asarray(_bias_table())

    def mix_args(l):
        return bias, sink[l], qg2[l:l + 1], kg2[l:l + 1], ws_b[l]

    w_out_shape = (W_OUT_BLK, D_MODEL)
    proj0, w1 = _proj_fwd(x0, ng[0], scale[0], shift[0], wts[0], "proj_fwd_0",
                          jobs=[_job_gather([(wloc_in, 1)], [(W_IN_BLK, D_MODEL)])])
    y0, wo0, wo1 = _mix_fwd(proj0, *mix_args(0), bsp[0], "mix_fwd_0",
                            jobs=[_job_gather([(wloc_out, 0), (wloc_out, 1)], [w_out_shape, w_out_shape])])
    wts[1] = w1.reshape(D_IN, D_MODEL)
    wos = [_permute_w_out(w.reshape(D_MODEL, D_MODEL)) for w in (wo0, wo1)]
    x1, = _out_fwd(y0, x0, gate[0], wos[0], "out_fwd_0")
    proj1, = _proj_fwd(x1, ng[1], scale[1], shift[1], wts[1], "proj_fwd_1")
    y1, = _mix_fwd(proj1, *mix_args(1), bsp[1], "mix_fwd_1")
    dx2, sq = _out_fwd_loss(y1, x1, gate[1], wos[1], target, "out_fwd_loss_1")
    loss = lax.psum(0.5 * sq[0, 0], ("x", "y", "c"))

    def blocks_out(gw):
        return _unpermute_w_out(gw).reshape(N_CHIPS, W_OUT_BLK, D_MODEL)

    dy1, gwo1, dgate1 = _out_bwd(dx2, y1, gate[1], wos[1], "out_bwd_1")
    go1 = blocks_out(gwo1)
    dpb, dkv, p0, p2, dqg1, dkg1, dsink1, dws1, dbsp1, ro1 = _mix_bwd(
        dy1, proj1, *mix_args(1), wst_b[1], bsp[1], "mix_bwd_1", jobs=[_job_swap(go1)])
    po1 = _pair_sum(go1, ro1, core_idx, "pair_sum_w_out_1")
    gwi1, dkvb, co1 = _w_in_grad(dpb, dkv, p0, p2, x1, ng[1], scale[1], shift[1], "w_in_grad_1", jobs=[_job_scatter(po1)])
    gi1 = gwi1.reshape(N_CHIPS, W_IN_BLK, D_MODEL)
    fo1 = _chip_sum(po1, co1, chip_idx, "chip_sum_w_out_1")
    dx1, dsh1, dsc1, dng1, grad_wo1, ri1 = _proj_bwd(dpb, dkvb, x1, dx2, ng[1], scale[1], wts[1], "proj_bwd_1",
                                                     jobs=[_job_join(fo1), _job_swap(gi1)])
    pi1 = _pair_sum(gi1, ri1, core_idx, "pair_sum_w_in_1")

    dy0, gwo0, dgate0 = _out_bwd(dx1, y0, gate[0], wos[0], "out_bwd_0")
    go0 = blocks_out(gwo0)
    dpb, dkv, p0, p2, dqg0, dkg0, dsink0, dws0, dbsp0, ci1, ro0 = _mix_bwd(
        dy0, proj0, *mix_args(0), wst_b[0], bsp[0], "mix_bwd_0", jobs=[_job_scatter(pi1), _job_swap(go0)])
    fi1 = _chip_sum(pi1, ci1, chip_idx, "chip_sum_w_in_1")
    po0 = _pair_sum(go0, ro0, core_idx, "pair_sum_w_out_0")
    gwi0, dkvb, grad_wi1, co0 = _w_in_grad(dpb, dkv, p0, p2, x0, ng[0], scale[0], shift[0], "w_in_grad_0",
                                           jobs=[_job_join(fi1), _job_scatter(po0)])
    gi0 = gwi0.reshape(N_CHIPS, W_IN_BLK, D_MODEL)
    fo0 = _chip_sum(po0, co0, chip_idx, "chip_sum_w_out_0")

    ri0, grad_wo0 = _comm([_job_swap(gi0), _job_join(fo0)], "swap_w_in_0")
    pi0 = _pair_sum(gi0, ri0, core_idx, "pair_sum_w_in_0")
    dx0, dsh0, dsc0, dng0, ci0 = _proj_bwd(dpb, dkvb, x0, dx1, ng[0], scale[0], wts[0], "proj_bwd_0",
                                           jobs=[_job_scatter(pi0)])
    fi0 = _chip_sum(pi0, ci0, chip_idx, "chip_sum_w_in_0")
    grad_wi0, = _comm([_job_join(fi0)], "join_w_in_0")

    def bs_grad(dbsp):
        return jnp.swapaxes(dbsp[:, :, ::HEAD_DIM], -1, -2).reshape(8, BLOCK)

    small_g = dict(
        w_s=jnp.stack([dws0, dws1]), b_s=jnp.stack([bs_grad(dbsp0), bs_grad(dbsp1)]),
        b_ada=jnp.stack([jnp.concatenate([dsh0, dsc0, dgate0], axis=-1)[0], jnp.concatenate([dsh1, dsc1, dgate1], axis=-1)[0]]),
        norm_gain=jnp.stack([dng0[0], dng1[0]]),
        q_gain=jnp.stack([dqg0[0, :HEAD_DIM], dqg1[0, :HEAD_DIM]]), k_gain=jnp.stack([dkg0[0, :HEAD_DIM], dkg1[0, :HEAD_DIM]]),
        sink=jnp.stack([dsink0[0, :N_HEADS], dsink1[0, :N_HEADS]]))
    small_w = dict(w_s=w_s, b_s=b_s, b_ada=b_ada, norm_gain=norm_gain, q_gain=q_gain, k_gain=k_gain, sink=sink)
    small_m = dict(w_s=m_w_s, b_s=m_b_s, b_ada=m_b_ada, norm_gain=m_norm_gain, q_gain=m_q_gain, k_gain=m_k_gain, sink=m_sink)
    small_v = dict(w_s=v_w_s, b_s=v_b_s, b_ada=v_b_ada, norm_gain=v_norm_gain, q_gain=v_q_gain, k_gain=v_k_gain, sink=v_sink)
    rows = _SMALL_ROWS + _SMALL_PAD
    gathered = _all_gather_rows(_pack_small(small_g)).reshape(N_DEV, rows, 128)
    packed = _small_sum_adamw(gathered, _pack_small(small_w), _pack_small(small_m), _pack_small(small_v))
    shapes = {k: a.shape for k, a in small_w.items()}
    sg, sd, sm, sv = (_unpack_small(p, shapes) for p in packed)

    r0 = DEPTH * 8 * BLOCK + DEPTH * 8
    dada_all = gathered[:, r0:r0 + DEPTH * 24].reshape(N_DEV, DEPTH, 3 * D_MODEL)
    dada_blk = jnp.moveaxis(lax.dynamic_slice_in_dim(dada_all, chip * W_ADA_BLK, W_ADA_BLK, axis=2), 0, 1)
    pad = BLOCK - N_DEV
    ada_out = _w_ada_grad_adamw(
        jnp.pad(cond.T, ((0, 0), (0, pad))).astype(BF16), jnp.pad(dada_blk, ((0, 0), (0, pad), (0, 0))),
        w_ada, m_w_ada, v_w_ada)

    in_out = [jnp.swapaxes(a, 1, 2) for a in _adamw_w_in(wt, grad_wi0, grad_wi1, mt, vt, pos_chunks)]
    out_out = _adamw_w_out(w_out, grad_wo0, grad_wo1, m_w_out, v_w_out)

    def ordered(k):
        small = (sg, sd, sm, sv)[k]
        return (ada_out[k], small["b_ada"], small["norm_gain"], in_out[k], small["q_gain"], small["k_gain"], small["sink"],
                small["w_s"], small["b_s"], out_out[k])

    return (loss, dx0[None], *ordered(0), *ordered(1), *ordered(2), *ordered(3))
```

```python
import numpy as np

import jax
import jax.numpy as jnp
from jax import lax
from jax.experimental import pallas as pl
from jax.experimental.pallas import tpu as pltpu

F32 = jnp.float32
BF16 = jnp.bfloat16

D_MODEL = 1024
DEPTH = 2
HEAD_DIM = 64
N_HEADS = 8
BLOCK = 128
SUB = 4
TILE = SUB * BLOCK
D_ATTN = 512
D_KV = 128
D_IN = 2816
N_CHIPS = 4
N_DEV = 8
W_IN_BLK = D_IN // N_CHIPS
W_OUT_BLK = D_MODEL // N_CHIPS
W_ADA_BLK = 3 * D_MODEL // N_CHIPS
CHUNK_ROWS = HEAD_DIM
N_CHUNKS = W_IN_BLK // CHUNK_ROWS
EPS = 1e-6
NEG_INF = -1e30

C_Q, C_K, C_V, C_GA, C_U, C_VG, C_GG = 0, 512, 640, 768, 1280, 1792, 2304

ADAM_LR = 0.001
ADAM_B1 = 0.9
ADAM_B2 = 0.999
ADAM_EPS = 1e-08
ADAM_WD = 0.01
ADAM_STEP = 10

MESH = pl.DeviceIdType.MESH
MIB = 1024 * 1024
ANY = pl.BlockSpec(memory_space=pl.ANY)
VMEM = pl.BlockSpec(memory_space=pltpu.VMEM)

NT_DIMS = (((1,), (1,)), ((), ()))
TN_DIMS = (((0,), (0,)), ((), ()))

_PAIR_ORDER = (0, 4, 1, 5, 2, 6, 3, 7)
_CHUNK_SRC = np.array([
    list(_PAIR_ORDER) + [8, 9, 10],
    [0] + [1 + h for h in _PAIR_ORDER] + [9, 10],
    list(range(N_CHUNKS)),
    list(range(N_CHUNKS)),
], np.int32)
_CHUNK_POS = np.argsort(_CHUNK_SRC, axis=1).astype(np.int32)


def _bias_table():
    i = np.arange(N_HEADS * BLOCK)[:, None]
    j = np.arange(3 * BLOCK)[None, :]
    dist = np.abs(j - BLOCK - (i % BLOCK))
    slope = 2.0 ** -(i // BLOCK + 1.0)
    inner = np.where(dist <= BLOCK, -(slope * dist), NEG_INF)
    first = np.where(j >= BLOCK, inner, NEG_INF)
    last = np.where(j < 2 * BLOCK, inner, NEG_INF)
    return np.stack([first, inner, last]).astype(np.float32)


def _full(shape):
    n = len(shape)
    return pl.BlockSpec(shape, lambda *_: (0,) * n)


def _sds(shape, dtype=F32):
    return jax.ShapeDtypeStruct(shape, dtype)


def _coords():
    return lax.axis_index("x"), lax.axis_index("y"), lax.axis_index("c")


def _other_chips(x, y):
    return [(1 - x, y), (x, 1 - y), (1 - x, 1 - y)]


def _remote(src, dst, sems, dev):
    return pltpu.make_async_remote_copy(src_ref=src, dst_ref=dst, send_sem=sems[0], recv_sem=sems[1],
                                        device_id=dev, device_id_type=MESH)


class _Job:
    def __init__(self, inputs, out_shapes, n_remote, n_local, make, then=None):
        self.inputs, self.out_shapes, self.n_remote, self.n_local, self.make = inputs, out_shapes, n_remote, n_local, make
        self.then = then


def _job_copies(jobs, jin, jout, sems, second=False):
    send, recv, loc = sems
    res, a, b, r, l = [], 0, 0, 0, 0
    for j in jobs:
        build = j.then if second else j.make
        if build is not None:
            res += build(jin[a:a + len(j.inputs)], jout[b:b + len(j.out_shapes)],
                         lambda k, r=r: (send.at[r + k], recv.at[r + k]), lambda k, l=l: loc.at[l + k])
        a, b, r, l = a + len(j.inputs), b + len(j.out_shapes), r + j.n_remote, l + j.n_local
    return res


def _run(copies):
    for cp in copies:
        cp.start()
    for cp in copies:
        cp.wait()


def _job_gather(sources, shapes):
    n = len(sources)

    def make(ins, outs, rsem, lsem):
        x, y, c = _coords()
        j = 2 * x + y
        res = []
        for t, ((_, layer), src, dst) in enumerate(zip(sources, ins, outs)):
            src = src if layer is None else src.at[layer]
            res.append(pltpu.make_async_copy(src, dst.at[j], lsem(t)))
            for k, chip in enumerate(_other_chips(x, y)):
                res.append(_remote(src, dst.at[j], rsem(3 * t + k), (*chip, c)))
        return res

    return _Job([a for a, _ in sources], [_sds((N_CHIPS,) + s, BF16) for s in shapes], 3 * n, n, make)


def _job_swap(g):
    _, rows, cols = g.shape
    half = rows // 2

    def make(ins, outs, rsem, lsem):
        x, y, c = _coords()
        return [_remote(ins[0].at[:, pl.ds((1 - c) * half, half), :], outs[0], rsem(0), (x, y, 1 - c))]

    return _Job([g], [_sds((N_CHIPS, half, cols))], 1, 0, make)


def _job_scatter(p):
    def make(ins, outs, rsem, lsem):
        x, y, c = _coords()
        return [_remote(ins[0].at[2 * chip[0] + chip[1]], outs[0].at[k], rsem(k), (*chip, c))
                for k, chip in enumerate(_other_chips(x, y))]

    return _Job([p], [_sds((3,) + p.shape[1:], p.dtype)], 3, 0, make)


def _job_all_gather(blk):
    m_per = blk.shape[0]

    def rows(ref, px, py, pc):
        return ref.at[pl.ds((4 * px + 2 * py + pc) * m_per, m_per), :]

    def make(ins, outs, rsem, lsem):
        x, y, c = _coords()
        res = [pltpu.make_async_copy(ins[0], rows(outs[0], x, y, c), lsem(0)),
               _remote(ins[0], rows(outs[0], x, y, c), rsem(0), (x, y, 1 - c))]
        res += [_remote(ins[0], rows(outs[0], x, y, c), rsem(1 + k), (*chip, c)) for k, chip in enumerate(_other_chips(x, y))]
        return res

    def then(ins, outs, rsem, lsem):
        x, y, c = _coords()
        return [_remote(rows(outs[0], *chip, c), rows(outs[0], *chip, c), rsem(4 + k), (x, y, 1 - c))
                for k, chip in enumerate(_other_chips(x, y))]

    return _Job([blk], [_sds((N_DEV * m_per, blk.shape[1]), blk.dtype)], 7, 1, make, then)


def _job_join(f):
    half, cols = f.shape

    def make(ins, outs, rsem, lsem):
        x, y, c = _coords()
        dst = outs[0].at[pl.ds(c * half, half), :]
        return [pltpu.make_async_copy(ins[0], dst, lsem(0)), _remote(ins[0], dst, rsem(0), (x, y, 1 - c))]

    return _Job([f], [_sds((2 * half, cols))], 1, 1, make)


def _pallas(body, *, name, grid, in_specs, out_specs, out_shape, operands, vmem_mib, jobs=()):
    in_specs, out_specs, out_shape = list(in_specs), list(out_specs), list(out_shape)
    n_in, n_out = len(in_specs), len(out_specs)
    j_in = [a for j in jobs for a in j.inputs]
    j_out = [s for j in jobs for s in j.out_shapes]
    n_rem = max(1, sum(j.n_remote for j in jobs))
    n_loc = max(1, sum(j.n_local for j in jobs))
    scratch = [pltpu.SemaphoreType.DMA((n_rem,)), pltpu.SemaphoreType.DMA((n_rem,)),
               pltpu.SemaphoreType.DMA((n_loc,))] if jobs else []

    def wrapped(*refs):
        ins = refs[:n_in]
        jin = refs[n_in:n_in + len(j_in)]
        outs = refs[n_in + len(j_in):n_in + len(j_in) + n_out]
        jout = refs[n_in + len(j_in) + n_out:n_in + len(j_in) + n_out + len(j_out)]

        if jobs:
            first = last = None
            for d, n in enumerate(grid):
                f, e = pl.program_id(d) == 0, pl.program_id(d) == n - 1
                first, last = (f, e) if first is None else (first & f, last & e)

            @pl.when(first)
            def _():
                for cp in _job_copies(jobs, jin, jout, refs[-3:]):
                    cp.start()

        body(*ins, *outs)

        if jobs:
            @pl.when(last)
            def _():
                for cp in _job_copies(jobs, jin, jout, refs[-3:]):
                    cp.wait()
                _run(_job_copies(jobs, jin, jout, refs[-3:], second=True))

    return pl.pallas_call(
        wrapped, name=name, grid=grid,
        in_specs=in_specs + [ANY] * len(j_in), out_specs=out_specs + [ANY] * len(j_out),
        out_shape=out_shape + j_out, scratch_shapes=scratch,
        compiler_params=pltpu.CompilerParams(dimension_semantics=("arbitrary",) * len(grid),
                                             vmem_limit_bytes=vmem_mib * MIB),
    )(*operands, *j_in)


def _comm(jobs, name):
    j_in = [a for j in jobs for a in j.inputs]
    j_out = [s for j in jobs for s in j.out_shapes]
    n_rem = max(1, sum(j.n_remote for j in jobs))
    n_loc = max(1, sum(j.n_local for j in jobs))

    def body(*refs):
        jin, jout = refs[:len(j_in)], refs[len(j_in):len(j_in) + len(j_out)]
        _run(_job_copies(jobs, jin, jout, refs[-3:]))
        _run(_job_copies(jobs, jin, jout, refs[-3:], second=True))

    return pl.pallas_call(
        body, name=name, in_specs=[ANY] * len(j_in), out_specs=[ANY] * len(j_out), out_shape=j_out,
        scratch_shapes=[pltpu.SemaphoreType.DMA((n_rem,)), pltpu.SemaphoreType.DMA((n_rem,)),
                        pltpu.SemaphoreType.DMA((n_loc,))],
    )(*j_in)


def _sigmoid(x):
    return 1.0 / (1.0 + jnp.exp(-x))


def _lo_mask(shape):
    return lax.broadcasted_iota(jnp.int32, shape, len(shape) - 1) < HEAD_DIM


def _half_sum(x, lo):
    a = jnp.sum(jnp.where(lo, x, 0.0), axis=-1, keepdims=True)
    b = jnp.sum(jnp.where(lo, 0.0, x), axis=-1, keepdims=True)
    return jnp.where(lo, a, b)


def _half_rms_scale(x, lo):
    return lax.rsqrt(_half_sum(x * x, lo) * (1.0 / HEAD_DIM) + EPS)


def _stack_heads(pairs, lo):
    return jnp.concatenate([jnp.where(lo, t, 0.0) for t in pairs] + [jnp.where(lo, 0.0, t) for t in pairs], axis=0)


def _unstack_pair(stack, p, lo):
    return jnp.where(lo, stack[BLOCK * p:BLOCK * (p + 1)], stack[BLOCK * (4 + p):BLOCK * (5 + p)])


def _attention_probs(q_stack, kn, bias_ref, sink_ref):
    rows = N_HEADS * BLOCK
    s = lax.dot_general(q_stack, kn, NT_DIMS, preferred_element_type=F32) + bias_ref[...]
    sink = jnp.concatenate([jnp.full((BLOCK, BLOCK), sink_ref[h], F32) for h in range(N_HEADS)], axis=0)
    cols = [s[:, BLOCK * j:BLOCK * (j + 1)] for j in range(3)]
    top = jnp.max(jnp.maximum(jnp.maximum(cols[0], cols[1]), cols[2]), axis=-1, keepdims=True)
    m = jnp.maximum(jnp.broadcast_to(top, (rows, BLOCK)), sink)
    e = [jnp.exp(c - m) for c in cols]
    es = jnp.exp(sink - m)
    inv = 1.0 / (jnp.broadcast_to(jnp.sum((e[0] + e[1]) + e[2], axis=-1, keepdims=True), (rows, BLOCK)) + es)
    return jnp.concatenate([c * inv for c in e], axis=1), es * inv


def _kv_rows(cur_ref, pkv_ref, nkv_ref):
    k = jnp.concatenate([pkv_ref[:, 0:D_KV], cur_ref[:, C_K:C_K + D_KV], nkv_ref[:, 0:D_KV]], axis=0)
    v = jnp.concatenate([pkv_ref[:, D_KV:2 * D_KV], cur_ref[:, C_V:C_V + D_KV], nkv_ref[:, D_KV:2 * D_KV]], axis=0)
    return k, v


def _overlap_add(parts):
    blocks = []
    for j in range(SUB + 2):
        terms = [parts[b][BLOCK * (j - b):BLOCK * (j - b + 1)] for b in range(SUB) if 0 <= j - b <= 2]
        total = terms[0]
        for t in terms[1:]:
            total = total + t
        blocks.append(total)
    return jnp.concatenate(blocks, axis=0)


def _mix_specs(nt):
    cur = pl.BlockSpec((TILE, D_IN), lambda i: (i, 0))
    kv_col = C_K // (2 * D_KV)
    pkv = pl.BlockSpec((BLOCK, 2 * D_KV), lambda i: (jnp.maximum(i * SUB - 1, 0), kv_col))
    nkv = pl.BlockSpec((BLOCK, 2 * D_KV), lambda i: (jnp.minimum((i + 1) * SUB, nt * SUB - 1), kv_col))
    table = (None, N_HEADS * BLOCK, 3 * BLOCK)
    first = pl.BlockSpec(table, lambda i: (jnp.where(i == 0, 0, 1), 0, 0))
    inner = pl.BlockSpec(table, lambda i: (1, 0, 0))
    last = pl.BlockSpec(table, lambda i: (jnp.where(i == nt - 1, 2, 1), 0, 0))
    return cur, pkv, nkv, [first] + [inner] * (SUB - 2) + [last]


def _proj_fwd(x, ng, scale, shift, wt, name, jobs=()):
    s = x.shape[0]
    ts = min(512, s)

    def body(x_ref, ng_ref, sc_ref, sh_ref, w_ref, o_ref):
        xv = x_ref[...]
        r = lax.rsqrt(jnp.mean(xv * xv, axis=-1, keepdims=True) + EPS)
        h = ((xv * r) * ng_ref[...]) * (1.0 + sc_ref[...]) + sh_ref[...]
        o_ref[...] = lax.dot_general(h.astype(BF16), w_ref[...], NT_DIMS, preferred_element_type=F32)

    vec = _full((1, D_MODEL))
    return _pallas(
        body, name=name, grid=(s // ts,),
        in_specs=[pl.BlockSpec((ts, D_MODEL), lambda i: (i, 0)), vec, vec, vec, _full((D_IN, D_MODEL))],
        out_specs=[pl.BlockSpec((ts, D_IN), lambda i: (i, 0))], out_shape=[_sds((s, D_IN))],
        operands=(x, ng, scale, shift, wt), vmem_mib=48, jobs=jobs)


def _mix_fwd(proj, bias, sink, qg2, kg2, ws, bsp, name, jobs=()):
    s = proj.shape[0]
    nt = s // TILE

    def body(sink_ref, cur_ref, pkv_ref, nkv_ref, *rest):
        bias_refs = rest[:SUB]
        qg_ref, kg_ref, ws_ref, bsp_ref, y_ref = rest[SUB:]
        lo = _lo_mask((BLOCK, BLOCK))
        lo_kv = _lo_mask((TILE + 2 * BLOCK, BLOCK))
        k_all, v_all = _kv_rows(cur_ref, pkv_ref, nkv_ref)
        kn_all = ((k_all * _half_rms_scale(k_all, lo_kv)) * kg_ref[...]).astype(BF16)
        vb_all = v_all.astype(BF16)
        for b in range(SUB):
            rows = slice(BLOCK * b, BLOCK * (b + 1))
            window = slice(BLOCK * b, BLOCK * (b + 3))
            qn = []
            for p in range(4):
                q = cur_ref[rows, C_Q + BLOCK * p:C_Q + BLOCK * (p + 1)]
                qn.append(((q * _half_rms_scale(q, lo)) * qg_ref[...]) * 0.125)
            q_stack = _stack_heads(qn, lo).astype(BF16)
            prob, _ = _attention_probs(q_stack, kn_all[window], bias_refs[b], sink_ref)
            o_stack = jnp.dot(prob.astype(BF16), vb_all[window], preferred_element_type=F32)
            for p in range(4):
                g = cur_ref[rows, C_GA + BLOCK * p:C_GA + BLOCK * (p + 1)]
                y_ref[rows, BLOCK * p:BLOCK * (p + 1)] = (_unstack_pair(o_stack, p, lo) * (g * _sigmoid(g))).astype(BF16)
            for p in range(4):
                vg = cur_ref[rows, C_VG + BLOCK * p:C_VG + BLOCK * (p + 1)]
                vn = (vg * _half_rms_scale(vg, lo)).astype(BF16)
                sv = jnp.where(lo, jnp.dot(ws_ref[2 * p], vn, preferred_element_type=F32),
                               jnp.dot(ws_ref[2 * p + 1], vn, preferred_element_type=F32)) + bsp_ref[p]
                u = cur_ref[rows, C_U + BLOCK * p:C_U + BLOCK * (p + 1)]
                g = cur_ref[rows, C_GG + BLOCK * p:C_GG + BLOCK * (p + 1)]
                y_ref[rows, D_ATTN + BLOCK * p:D_ATTN + BLOCK * (p + 1)] = ((u * sv) * (g * _sigmoid(g))).astype(BF16)

    cur, pkv, nkv, bias_specs = _mix_specs(nt)
    return _pallas(
        body, name=name, grid=(nt,),
        in_specs=[pl.BlockSpec(memory_space=pltpu.SMEM), cur, pkv, nkv, *bias_specs, _full((1, BLOCK)), _full((1, BLOCK)),
                  _full((8, BLOCK, BLOCK)), _full((4, BLOCK, BLOCK))],
        out_specs=[pl.BlockSpec((TILE, D_MODEL), lambda i: (i, 0))], out_shape=[_sds((s, D_MODEL), BF16)],
        operands=(sink, proj, proj, proj, *([bias] * SUB), qg2, kg2, ws, bsp), vmem_mib=48, jobs=jobs)


def _out_fwd(y, x, gate, w_out, name):
    s = x.shape[0]
    ts = min(512, s)

    def body(y_ref, x_ref, g_ref, w_ref, o_ref):
        o_ref[...] = x_ref[...] + g_ref[...] * jnp.dot(y_ref[...], w_ref[...], preferred_element_type=F32)

    row = pl.BlockSpec((ts, D_MODEL), lambda i: (i, 0))
    return _pallas(
        body, name=name, grid=(s // ts,), in_specs=[row, row, _full((1, D_MODEL)), _full((D_MODEL, D_MODEL))],
        out_specs=[row], out_shape=[_sds((s, D_MODEL))], operands=(y, x, gate, w_out), vmem_mib=32)


def _out_fwd_loss(y, x, gate, w_out, target, name):
    s = x.shape[0]
    ts = min(512, s)

    def body(y_ref, x_ref, g_ref, w_ref, t_ref, dx_ref, sq_ref):
        @pl.when(pl.program_id(0) == 0)
        def _():
            sq_ref[...] = jnp.zeros_like(sq_ref)

        out = x_ref[...] + g_ref[...] * jnp.dot(y_ref[...], w_ref[...], preferred_element_type=F32)
        diff = out - t_ref[...]
        dx_ref[...] = diff * (1.0 / D_MODEL)
        per_token = jnp.sum(diff * diff, axis=-1, keepdims=True) * (1.0 / D_MODEL)
        sq_ref[...] += jnp.sum(per_token, axis=0, keepdims=True)

    row = pl.BlockSpec((ts, D_MODEL), lambda i: (i, 0))
    return _pallas(
        body, name=name, grid=(s // ts,), in_specs=[row, row, _full((1, D_MODEL)), _full((D_MODEL, D_MODEL)), row],
        out_specs=[row, _full((1, 1))], out_shape=[_sds((s, D_MODEL)), _sds((1, 1))],
        operands=(y, x, gate, w_out, target), vmem_mib=40)


def _out_bwd(dxo, y, gate, w_out, name, jobs=()):
    s = dxo.shape[0]
    ts = min(512, s)

    def body(dx_ref, y_ref, g_ref, w_ref, dy_ref, gw_ref, dg_ref):
        @pl.when(pl.program_id(0) == 0)
        def _():
            gw_ref[...] = jnp.zeros_like(gw_ref)
            dg_ref[...] = jnp.zeros_like(dg_ref)

        dx = dx_ref[...]
        do = (dx * g_ref[...]).astype(BF16)
        yv = y_ref[...]
        dy_ref[...] = lax.dot_general(do, w_ref[...], NT_DIMS, preferred_element_type=F32)
        gw_ref[...] += lax.dot_general(yv, do, TN_DIMS, preferred_element_type=F32)
        o = jnp.dot(yv, w_ref[...], preferred_element_type=F32)
        dg_ref[...] += jnp.sum(dx * o, axis=0, keepdims=True)

    row = pl.BlockSpec((ts, D_MODEL), lambda i: (i, 0))
    return _pallas(
        body, name=name, grid=(s // ts,), in_specs=[row, row, _full((1, D_MODEL)), _full((D_MODEL, D_MODEL))],
        out_specs=[row, _full((D_MODEL, D_MODEL)), _full((1, D_MODEL))],
        out_shape=[_sds((s, D_MODEL)), _sds((D_MODEL, D_MODEL)), _sds((1, D_MODEL))],
        operands=(dxo, y, gate, w_out), vmem_mib=48, jobs=jobs)


def _mix_bwd(dy, proj, bias, sink, qg2, kg2, ws, wst, bsp, name, jobs=()):
    s = proj.shape[0]
    nt = s // TILE

    def body(sink_ref, dy_ref, cur_ref, pkv_ref, nkv_ref, *rest):
        bias_refs = rest[:SUB]
        (qg_ref, kg_ref, ws_ref, wst_ref, bsp_ref,
         dpb_ref, dkv_ref, p0_ref, p2_ref, dqg_ref, dkg_ref, dsink_ref, dws_ref, dbsp_ref) = rest[SUB:]

        def put(rows, col, value):
            dpb_ref[rows, col:col + BLOCK] = value.astype(BF16)

        @pl.when(pl.program_id(0) == 0)
        def _():
            dqg_ref[...] = jnp.zeros_like(dqg_ref)
            dkg_ref[...] = jnp.zeros_like(dkg_ref)
            dsink_ref[...] = jnp.zeros_like(dsink_ref)
            dws_ref[...] = jnp.zeros_like(dws_ref)
            dbsp_ref[...] = jnp.zeros_like(dbsp_ref)

        lo = _lo_mask((BLOCK, BLOCK))
        lo_kv = _lo_mask((TILE + 2 * BLOCK, BLOCK))
        lane_row = lax.broadcasted_iota(jnp.int32, (1, BLOCK), 1)
        qg = qg_ref[...]
        kg = kg_ref[...]

        k_all, v_all = _kv_rows(cur_ref, pkv_ref, nkv_ref)
        rk = _half_rms_scale(k_all, lo_kv)
        khat = k_all * rk
        kn_all = (khat * kg).astype(BF16)
        vb_all = v_all.astype(BF16)

        dkn_parts, dv_parts = [], []
        dsink = jnp.zeros((1, BLOCK), F32)
        dqg = jnp.zeros((1, BLOCK), F32)
        for b in range(SUB):
            rows = slice(BLOCK * b, BLOCK * (b + 1))
            window = slice(BLOCK * b, BLOCK * (b + 3))
            kn, vb = kn_all[window], vb_all[window]

            qhat, rq = [], []
            for p in range(4):
                q = cur_ref[rows, C_Q + BLOCK * p:C_Q + BLOCK * (p + 1)]
                r = _half_rms_scale(q, lo)
                rq.append(r)
                qhat.append(q * r)
            q_stack = _stack_heads([(qh * qg) * 0.125 for qh in qhat], lo).astype(BF16)
            prob, psink = _attention_probs(q_stack, kn, bias_refs[b], sink_ref)
            pb = prob.astype(BF16)
            o_stack = jnp.dot(pb, vb, preferred_element_type=F32)

            dout = []
            for p in range(4):
                g = cur_ref[rows, C_GA + BLOCK * p:C_GA + BLOCK * (p + 1)]
                sg = _sigmoid(g)
                dya = dy_ref[rows, BLOCK * p:BLOCK * (p + 1)]
                attn = _unstack_pair(o_stack, p, lo)
                put(rows, C_GA + BLOCK * p, dya * attn * (sg * (1.0 + g * (1.0 - sg))))
                dout.append(dya * (g * sg))
            do_stack = _stack_heads(dout, lo).astype(BF16)
            dp = lax.dot_general(do_stack, vb, NT_DIMS, preferred_element_type=F32)
            delta = jnp.sum(prob * dp, axis=-1, keepdims=True)
            dsb = (prob * (dp - delta)).astype(BF16)

            csink = -(psink * delta)
            for h in range(N_HEADS):
                tot = jnp.sum(csink[BLOCK * h:BLOCK * (h + 1)], axis=0, keepdims=True)
                dsink = dsink + jnp.where(lane_row == h, tot, 0.0)

            dq_stack = jnp.dot(dsb, kn, preferred_element_type=F32) * 0.125
            dkn_parts.append(lax.dot_general(dsb, q_stack, TN_DIMS, preferred_element_type=F32))
            dv_parts.append(lax.dot_general(pb, do_stack, TN_DIMS, preferred_element_type=F32))

            for p in range(4):
                dqn = _unstack_pair(dq_stack, p, lo)
                qh = qhat[p]
                dqg = dqg + jnp.sum(dqn * qh, axis=0, keepdims=True)
                dqh = dqn * qg
                mean = _half_sum(dqh * qh, lo) * (1.0 / HEAD_DIM)
                put(rows, C_Q + BLOCK * p, rq[p] * (dqh - qh * mean))

            for p in range(4):
                vg = cur_ref[rows, C_VG + BLOCK * p:C_VG + BLOCK * (p + 1)]
                r = _half_rms_scale(vg, lo)
                vnf = vg * r
                vn = vnf.astype(BF16)
                sv = jnp.where(lo, jnp.dot(ws_ref[2 * p], vn, preferred_element_type=F32),
                               jnp.dot(ws_ref[2 * p + 1], vn, preferred_element_type=F32)) + bsp_ref[p]
                u = cur_ref[rows, C_U + BLOCK * p:C_U + BLOCK * (p + 1)]
                g = cur_ref[rows, C_GG + BLOCK * p:C_GG + BLOCK * (p + 1)]
                sg = _sigmoid(g)
                dym = dy_ref[rows, D_ATTN + BLOCK * p:D_ATTN + BLOCK * (p + 1)]
                put(rows, C_GG + BLOCK * p, dym * (u * sv) * (sg * (1.0 + g * (1.0 - sg))))
                dgm = dym * (g * sg)
                put(rows, C_U + BLOCK * p, dgm * sv)
                dsv = dgm * u
                dsv_a = jnp.where(lo, dsv, 0.0)
                dsv_b = jnp.where(lo, 0.0, dsv)
                dws_ref[2 * p] += lax.dot_general(dsv_a.astype(BF16), vn, NT_DIMS, preferred_element_type=F32)
                dws_ref[2 * p + 1] += lax.dot_general(dsv_b.astype(BF16), vn, NT_DIMS, preferred_element_type=F32)
                dbsp_ref[p] += jnp.where(lo, jnp.sum(dsv_a, axis=-1, keepdims=True), jnp.sum(dsv_b, axis=-1, keepdims=True))
                dsvb = dsv.astype(BF16)
                dvn = jnp.where(lo, jnp.dot(wst_ref[2 * p], dsvb, preferred_element_type=F32),
                                jnp.dot(wst_ref[2 * p + 1], dsvb, preferred_element_type=F32))
                mean = _half_sum(dvn * vnf, lo) * (1.0 / HEAD_DIM)
                put(rows, C_VG + BLOCK * p, r * (dvn - vnf * mean))

        dsink_ref[...] += dsink
        dqg = jnp.broadcast_to(dqg, (8, BLOCK))
        dqg_ref[...] += dqg + pltpu.roll(dqg, HEAD_DIM, 1)

        dkn = _overlap_add(dkn_parts)
        dv = _overlap_add(dv_parts)
        dkg = jnp.broadcast_to(jnp.sum(dkn * khat, axis=0, keepdims=True), (8, BLOCK))
        dkg_ref[...] += dkg + pltpu.roll(dkg, HEAD_DIM, 1)
        dkh = dkn * kg
        dk = rk * (dkh - khat * (_half_sum(dkh * khat, lo_kv) * (1.0 / HEAD_DIM)))
        dpb_ref[:, C_K:C_GA] = jnp.zeros((TILE, 2 * D_KV), BF16)
        dkv_ref[:, 0:D_KV] = dk[BLOCK:BLOCK + TILE]
        dkv_ref[:, D_KV:2 * D_KV] = dv[BLOCK:BLOCK + TILE]
        p0_ref[:, 0:D_KV] = dk[0:BLOCK]
        p0_ref[:, D_KV:2 * D_KV] = dv[0:BLOCK]
        p2_ref[:, 0:D_KV] = dk[BLOCK + TILE:]
        p2_ref[:, D_KV:2 * D_KV] = dv[BLOCK + TILE:]

    cur, pkv, nkv, bias_specs = _mix_specs(nt)
    kv_blk = (BLOCK, 2 * D_KV)
    return _pallas(
        body, name=name, grid=(nt,),
        in_specs=[pl.BlockSpec(memory_space=pltpu.SMEM), pl.BlockSpec((TILE, D_MODEL), lambda i: (i, 0)),
                  cur, pkv, nkv, *bias_specs, _full((1, BLOCK)), _full((1, BLOCK)),
                  _full((8, BLOCK, BLOCK)), _full((8, BLOCK, BLOCK)), _full((4, BLOCK, BLOCK))],
        out_specs=[cur, pl.BlockSpec((TILE, 2 * D_KV), lambda i: (i, 0)),
                   pl.BlockSpec(kv_blk, lambda i: ((i + nt - 1) % nt, 0)),
                   pl.BlockSpec(kv_blk, lambda i: ((i + 1) % nt, 0)),
                   _full((8, BLOCK)), _full((8, BLOCK)), _full((1, BLOCK)),
                   _full((8, BLOCK, BLOCK)), _full((4, BLOCK, BLOCK))],
        out_shape=[_sds((s, D_IN), BF16), _sds((s, 2 * D_KV)), _sds((nt * BLOCK, 2 * D_KV)), _sds((nt * BLOCK, 2 * D_KV)),
                   _sds((8, BLOCK)), _sds((8, BLOCK)), _sds((1, BLOCK)),
                   _sds((8, BLOCK, BLOCK)), _sds((4, BLOCK, BLOCK))],
        operands=(sink, dy, proj, proj, proj, *([bias] * SUB), qg2, kg2, ws, wst, bsp), vmem_mib=56, jobs=jobs)


def _w_in_grad(dpb, dkv, p0, p2, x, ng, scale, shift, name, jobs=()):
    s = x.shape[0]
    ts = min(2 * TILE, s)
    tiles = ts // TILE

    def body(dpb_ref, dkv_ref, p0_ref, p2_ref, x_ref, ng_ref, sc_ref, sh_ref, gw_ref, dkvb_ref):
        @pl.when(pl.program_id(0) == 0)
        def _():
            gw_ref[...] = jnp.zeros_like(gw_ref)

        xv = x_ref[...]
        r = lax.rsqrt(jnp.mean(xv * xv, axis=-1, keepdims=True) + EPS)
        h = (((xv * r) * ng_ref[...]) * (1.0 + sc_ref[...]) + sh_ref[...]).astype(BF16)
        for t in range(tiles):
            halo = slice(BLOCK * t, BLOCK * (t + 1))
            first = slice(TILE * t, TILE * t + BLOCK)
            last = slice(TILE * (t + 1) - BLOCK, TILE * (t + 1))
            dkvb_ref[first, :] = (dkv_ref[first, :] + p2_ref[halo, :]).astype(BF16)
            if SUB > 2:
                inner = slice(TILE * t + BLOCK, TILE * (t + 1) - BLOCK)
                dkvb_ref[inner, :] = dkv_ref[inner, :].astype(BF16)
            dkvb_ref[last, :] = (dkv_ref[last, :] + p0_ref[halo, :]).astype(BF16)
        gw_ref[...] += lax.dot_general(dpb_ref[...], h, TN_DIMS, preferred_element_type=F32)
        gw_ref[C_K:C_GA, :] += lax.dot_general(dkvb_ref[...], h, TN_DIMS, preferred_element_type=F32)

    kv = pl.BlockSpec((ts, 2 * D_KV), lambda i: (i, 0))
    halo = pl.BlockSpec((tiles * BLOCK, 2 * D_KV), lambda i: (i, 0))
    vec = _full((1, D_MODEL))
    return _pallas(
        body, name=name, grid=(s // ts,),
        in_specs=[pl.BlockSpec((ts, D_IN), lambda i: (i, 0)), kv, halo, halo,
                  pl.BlockSpec((ts, D_MODEL), lambda i: (i, 0)), vec, vec, vec],
        out_specs=[_full((D_IN, D_MODEL)), kv], out_shape=[_sds((D_IN, D_MODEL)), _sds((s, 2 * D_KV), BF16)],
        operands=(dpb, dkv, p0, p2, x, ng, scale, shift), vmem_mib=56, jobs=jobs)


def _proj_bwd(dpb, dkvb, x, dxo, ng, scale, wt, name, jobs=()):
    s = x.shape[0]
    ts = min(512, s)

    def body(dpb_ref, dkvb_ref, x_ref, dxo_ref, ng_ref, sc_ref, w_ref, dxi_ref, dsh_ref, dsc_ref, dng_ref):
        @pl.when(pl.program_id(0) == 0)
        def _():
            dsh_ref[...] = jnp.zeros_like(dsh_ref)
            dsc_ref[...] = jnp.zeros_like(dsc_ref)
            dng_ref[...] = jnp.zeros_like(dng_ref)

        dh = (jnp.dot(dpb_ref[...], w_ref[...], preferred_element_type=F32)
              + jnp.dot(dkvb_ref[...], w_ref[C_K:C_GA, :], preferred_element_type=F32))

        xv = x_ref[...]
        r = lax.rsqrt(jnp.mean(xv * xv, axis=-1, keepdims=True) + EPS)
        xn = xv * r
        ngv = ng_ref[...]
        sc1 = 1.0 + sc_ref[...]
        dsh_ref[...] += jnp.sum(dh, axis=0, keepdims=True)
        dsc_ref[...] += jnp.sum(dh * (xn * ngv), axis=0, keepdims=True)
        dh1 = dh * sc1
        dng_ref[...] += jnp.sum(dh1 * xn, axis=0, keepdims=True)
        dxn = dh1 * ngv
        dxi_ref[...] = r * (dxn - xn * jnp.mean(dxn * xn, axis=-1, keepdims=True)) + dxo_ref[...]

    row = pl.BlockSpec((ts, D_MODEL), lambda i: (i, 0))
    vec = _full((1, D_MODEL))
    return _pallas(
        body, name=name, grid=(s // ts,),
        in_specs=[pl.BlockSpec((ts, D_IN), lambda i: (i, 0)), pl.BlockSpec((ts, 2 * D_KV), lambda i: (i, 0)),
                  row, row, vec, vec, _full((D_IN, D_MODEL))],
        out_specs=[row, vec, vec, vec],
        out_shape=[_sds((s, D_MODEL)), _sds((1, D_MODEL)), _sds((1, D_MODEL)), _sds((1, D_MODEL))],
        operands=(dpb, dkvb, x, dxo, ng, scale, wt), vmem_mib=48, jobs=jobs)


def _pair_sum(g, r, c_idx, name, send_dtype=None):
    _, rows, cols = g.shape
    half = rows // 2

    def body(c_ref, g_ref, r_ref, o_ref, *narrow):
        total = g_ref[...] + r_ref[...]
        o_ref[...] = total
        for n_ref in narrow:
            n_ref[...] = total.astype(n_ref.dtype)

    blk = (None, half, cols)
    out_blk = pl.BlockSpec(blk, lambda j, c: (j, 0, 0))
    shapes = [_sds((N_CHIPS, half, cols))] + ([_sds((N_CHIPS, half, cols), send_dtype)] if send_dtype else [])
    return pl.pallas_call(
        body, name=name,
        grid_spec=pltpu.PrefetchScalarGridSpec(
            num_scalar_prefetch=1, grid=(N_CHIPS,),
            in_specs=[pl.BlockSpec(blk, lambda j, c: (j, c[0], 0)), out_blk], out_specs=[out_blk] * len(shapes)),
        out_shape=shapes,
        compiler_params=pltpu.CompilerParams(dimension_semantics=("arbitrary",), vmem_limit_bytes=32 * MIB),
    )(c_idx, g, r)


def _chip_sum(p, r, j_idx, name):
    _, rows, cols = p.shape
    tr = rows // 2

    def body(j_ref, p_ref, r_ref, o_ref):
        o_ref[...] = ((p_ref[...] + r_ref[0].astype(F32)) + r_ref[1].astype(F32)) + r_ref[2].astype(F32)

    return pl.pallas_call(
        body, name=name,
        grid_spec=pltpu.PrefetchScalarGridSpec(
            num_scalar_prefetch=1, grid=(2,),
            in_specs=[pl.BlockSpec((None, tr, cols), lambda t, j: (j[0], t, 0)),
                      pl.BlockSpec((3, tr, cols), lambda t, j: (0, t, 0))],
            out_specs=pl.BlockSpec((tr, cols), lambda t, j: (t, 0))),
        out_shape=_sds((rows, cols)),
        compiler_params=pltpu.CompilerParams(dimension_semantics=("arbitrary",), vmem_limit_bytes=32 * MIB),
    )(j_idx, p, r)


def _cast_permute_w_in(wt, src_chunks):
    def body(t_ref, w_ref, o_ref):
        o_ref[...] = w_ref[...].astype(BF16)

    return pl.pallas_call(
        body, name="cast_permute_w_in",
        grid_spec=pltpu.PrefetchScalarGridSpec(
            num_scalar_prefetch=1, grid=(DEPTH, N_CHUNKS),
            in_specs=[pl.BlockSpec((None, CHUNK_ROWS, D_MODEL), lambda l, t, tbl: (l, tbl[t], 0))],
            out_specs=pl.BlockSpec((None, CHUNK_ROWS, D_MODEL), lambda l, t, tbl: (l, t, 0))),
        out_shape=_sds((DEPTH, W_IN_BLK, D_MODEL), BF16),
        compiler_params=pltpu.CompilerParams(dimension_semantics=("arbitrary", "arbitrary")),
    )(src_chunks, wt)


def _gather_inputs(c, w_out, wloc_in):
    half = W_IN_BLK // 2

    def body(c_ref, wout_ref, wloc_ref, call_ref, woutb_ref, w0_ref, send_sems, recv_sems, local_sem):
        x, y, cc = _coords()
        j = 2 * x + y
        b = 2 * j + cc
        sib = (x, y, 1 - cc)
        woutb_ref[...] = wout_ref[...].astype(BF16)
        call_ref[b] = c_ref[...]
        chips = _other_chips(x, y)

        def sems(k):
            return send_sems.at[k], recv_sems.at[k]

        def half_rows(chip_index):
            return w0_ref.at[chip_index, pl.ds(cc * half, half), :]

        mine = pltpu.make_async_copy(wloc_ref.at[0], w0_ref.at[j], local_sem)
        mine.start()
        first = [_remote(wloc_ref.at[0, pl.ds(cc * half, half), :], half_rows(j), sems(k), (*chip, cc))
                 for k, chip in enumerate(chips)]
        k = 3
        rest = []
        for fx in (0, 1):
            for fy in (0, 1):
                for fc in (0, 1):
                    if fx or fy or fc:
                        dev = (1 - x if fx else x, 1 - y if fy else y, 1 - cc if fc else cc)
                        rest.append(_remote(call_ref.at[b], call_ref.at[b], sems(k), dev))
                        k += 1
        for cp in first + rest:
            cp.start()
        passed = []
        for k, chip in enumerate(chips):
            jk = 2 * chip[0] + chip[1]
            first[k].wait_recv()
            passed.append(_remote(half_rows(jk), half_rows(jk), sems(10 + k), sib))
            passed[k].start()
        for cp in first:
            cp.wait_send()
        for cp in rest + passed:
            cp.wait()
        mine.wait()

    return pl.pallas_call(
        body, name="gather_inputs", in_specs=[VMEM, VMEM, ANY], out_specs=[VMEM, VMEM, ANY],
        out_shape=[_sds((N_DEV, 1, D_MODEL)), _sds((DEPTH, W_OUT_BLK, D_MODEL), BF16),
                   _sds((N_CHIPS, W_IN_BLK, D_MODEL), BF16)],
        scratch_shapes=[pltpu.SemaphoreType.DMA((13,)), pltpu.SemaphoreType.DMA((13,)), pltpu.SemaphoreType.DMA],
        compiler_params=pltpu.CompilerParams(vmem_limit_bytes=32 * MIB),
    )(c, w_out, wloc_in)


def _ada_rows(c_all, w_ada, b_blk):
    def body(c_ref, w_ref, b_ref, o_ref, cond_ref):
        cv = c_ref[...]
        cond = (cv * _sigmoid(cv)).astype(BF16)
        cond_ref[...] = cond.astype(F32)
        for l in range(DEPTH):
            o_ref[:, l, :] = jnp.dot(cond, w_ref[l].astype(BF16), preferred_element_type=F32) + b_ref[l:l + 1, :]

    return pl.pallas_call(
        body, name="ada_rows", in_specs=[VMEM, VMEM, VMEM], out_specs=[VMEM, VMEM],
        out_shape=[_sds((N_DEV, DEPTH, W_ADA_BLK)), _sds((N_DEV, D_MODEL))],
        compiler_params=pltpu.CompilerParams(vmem_limit_bytes=32 * MIB),
    )(c_all, w_ada, b_blk)


def _exchange_ada(part):
    def body(part_ref, out_ref, send_sems, recv_sems):
        x, y, cc = _coords()
        j = 2 * x + y
        out_ref[j] = part_ref[2 * j + cc]
        copies = []
        for k, chip in enumerate(_other_chips(x, y)):
            b_dst = 4 * chip[0] + 2 * chip[1] + cc
            copies.append(_remote(part_ref.at[b_dst], out_ref.at[j], (send_sems.at[k], recv_sems.at[k]), (*chip, cc)))
        for cp in copies:
            cp.start()
        for cp in copies:
            cp.wait()

    return pl.pallas_call(
        body, name="exchange_ada", in_specs=[VMEM], out_specs=VMEM,
        out_shape=_sds((N_CHIPS, DEPTH, W_ADA_BLK)),
        scratch_shapes=[pltpu.SemaphoreType.DMA((3,)), pltpu.SemaphoreType.DMA((3,))],
    )(part)


def _all_gather_rows(blk):
    m_per, n = blk.shape

    def body(x_ref, out_ref, send_sems, recv_sems, local_sem):
        x, y, c = _coords()
        me, sibling = (x, y, c), (x, y, 1 - c)
        chips = _other_chips(x, y)

        def rows(px, py, pc):
            return out_ref.at[pl.ds((4 * px + 2 * py + pc) * m_per, m_per), :]

        def copy(k, block, to, src=None):
            return _remote(rows(*block) if src is None else src, rows(*block), (send_sems.at[k], recv_sems.at[k]), to)

        mine = pltpu.make_async_copy(x_ref, rows(*me), local_sem)
        mine.start()
        first = [copy(0, me, sibling, src=x_ref)]
        first += [copy(1 + j, me, (*chip, c), src=x_ref) for j, chip in enumerate(chips)]
        for cp in first:
            cp.start()
        passed = [copy(4 + j, (*chip, c), sibling) for j, chip in enumerate(chips)]
        for j, chip in enumerate(chips):
            copy(1 + j, (*chip, c), me).wait_recv()
            passed[j].start()
        copy(0, sibling, me).wait_recv()
        for j, chip in enumerate(chips):
            copy(4 + j, (*chip, 1 - c), me).wait_recv()
        for cp in first + passed:
            cp.wait_send()
        mine.wait()

    return pl.pallas_call(
        body, name="all_gather_small", in_specs=[VMEM], out_specs=VMEM,
        out_shape=_sds((N_DEV * m_per, n), blk.dtype),
        scratch_shapes=[pltpu.SemaphoreType.DMA((7,)), pltpu.SemaphoreType.DMA((7,)), pltpu.SemaphoreType.DMA],
        compiler_params=pltpu.CompilerParams(vmem_limit_bytes=32 * MIB),
    )(blk)


def _adamw_math(w, g, m, v):
    m = ADAM_B1 * m + (1.0 - ADAM_B1) * g
    v = ADAM_B2 * v + (1.0 - ADAM_B2) * (g * g)
    m_hat = m / (1.0 - ADAM_B1 ** ADAM_STEP)
    v_hat = v / (1.0 - ADAM_B2 ** ADAM_STEP)
    delta = -ADAM_LR * (m_hat / (jnp.sqrt(v_hat) + ADAM_EPS) + ADAM_WD * w)
    return delta, m, v


def _adamw_w_in(w, g0, g1, m, v, pos_chunks):
    def body(t_ref, w_ref, g0_ref, g1_ref, m_ref, v_ref, g_ref, d_ref, nm_ref, nv_ref):
        g = jnp.where(pl.program_id(0) == 0, g0_ref[...], g1_ref[...])
        g_ref[...] = g
        d_ref[...], nm_ref[...], nv_ref[...] = _adamw_math(w_ref[...], g, m_ref[...], v_ref[...])

    nat = pl.BlockSpec((None, CHUNK_ROWS, D_MODEL), lambda l, t, tbl: (l, t, 0))
    per = pl.BlockSpec((CHUNK_ROWS, D_MODEL), lambda l, t, tbl: (tbl[t], 0))
    return pl.pallas_call(
        body, name="adamw_w_in",
        grid_spec=pltpu.PrefetchScalarGridSpec(num_scalar_prefetch=1, grid=(DEPTH, N_CHUNKS),
                                               in_specs=[nat, per, per, nat, nat], out_specs=[nat] * 4),
        out_shape=[_sds(w.shape)] * 4,
        compiler_params=pltpu.CompilerParams(dimension_semantics=("arbitrary", "arbitrary")),
    )(pos_chunks, w, g0, g1, m, v)


def _adamw_w_out(w, g0, g1, m, v):
    def body(w_ref, g0_ref, g1_ref, m_ref, v_ref, g_ref, d_ref, nm_ref, nv_ref):
        g = jnp.where(pl.program_id(0) == 0, g0_ref[...], g1_ref[...])
        g_ref[...] = g
        d_ref[...], nm_ref[...], nv_ref[...] = _adamw_math(w_ref[...], g, m_ref[...], v_ref[...])

    blk = pl.BlockSpec((None, W_OUT_BLK, D_MODEL), lambda l: (l, 0, 0))
    gblk = _full((W_OUT_BLK, D_MODEL))
    return pl.pallas_call(
        body, name="adamw_w_out", grid=(DEPTH,), in_specs=[blk, gblk, gblk, blk, blk], out_specs=[blk] * 4,
        out_shape=[_sds(w.shape)] * 4,
        compiler_params=pltpu.CompilerParams(dimension_semantics=("arbitrary",), vmem_limit_bytes=32 * MIB),
    )(w, g0, g1, m, v)


def _w_ada_grad_adamw(cond_t, dada, w, m, v):
    _, rows, cols = w.shape
    tr = 256

    def body(ct_ref, da_ref, w_ref, m_ref, v_ref, g_ref, d_ref, nm_ref, nv_ref):
        g = jnp.dot(ct_ref[...], da_ref[...].astype(BF16), preferred_element_type=F32)
        g_ref[...] = g
        d_ref[...], nm_ref[...], nv_ref[...] = _adamw_math(w_ref[...], g, m_ref[...], v_ref[...])

    blk = pl.BlockSpec((None, tr, cols), lambda l, t: (l, t, 0))
    return pl.pallas_call(
        body, name="w_ada_grad_adamw", grid=(DEPTH, rows // tr),
        in_specs=[pl.BlockSpec((tr, BLOCK), lambda l, t: (t, 0)), pl.BlockSpec((None, BLOCK, cols), lambda l, t: (l, 0, 0)),
                  blk, blk, blk],
        out_specs=[blk] * 4, out_shape=[_sds(w.shape)] * 4,
        compiler_params=pltpu.CompilerParams(dimension_semantics=("arbitrary", "arbitrary"), vmem_limit_bytes=32 * MIB),
    )(cond_t, dada, w, m, v)


def _small_sum_adamw(gathered_a, gathered_b, w, m, v):
    def body(a_ref, b_ref, w_ref, m_ref, v_ref, g_ref, d_ref, nm_ref, nv_ref):
        def total(ref):
            g = ref[0]
            for b in range(1, N_DEV):
                g = g + ref[b]
            return g

        g = jnp.concatenate([total(a_ref), total(b_ref)], axis=0)
        g_ref[...] = g
        d_ref[...], nm_ref[...], nv_ref[...] = _adamw_math(w_ref[...], g, m_ref[...], v_ref[...])

    return pl.pallas_call(
        body, name="small_sum_adamw", in_specs=[VMEM] * 5, out_specs=[VMEM] * 4, out_shape=[_sds(w.shape)] * 4,
        compiler_params=pltpu.CompilerParams(vmem_limit_bytes=48 * MIB),
    )(gathered_a, gathered_b, w, m, v)


_SMALL_A = (("w_s", DEPTH * 8 * BLOCK), ("b_s", DEPTH * 8), ("q_gain", 1), ("k_gain", 1), ("sink", 1))
_SMALL_B = (("b_ada", DEPTH * 24), ("norm_gain", DEPTH * 8))


def _pack_rows(parts, layout):
    rows = []
    for name, n in layout:
        flat = parts[name].reshape(-1)
        rows.append(jnp.pad(flat, (0, n * 128 - flat.shape[0])).reshape(n, 128))
    n_rows = sum(n for _, n in layout)
    if n_rows % 8:
        rows.append(jnp.zeros((-n_rows % 8, 128), F32))
    return jnp.concatenate(rows, axis=0)


def _pack_small(parts):
    return jnp.concatenate([_pack_rows(parts, _SMALL_A), _pack_rows(parts, _SMALL_B)], axis=0)


def _unpack_small(packed, shapes):
    out, r0 = {}, 0
    for layout in (_SMALL_A, _SMALL_B):
        for name, n in layout:
            size = 1
            for d in shapes[name]:
                size *= d
            out[name] = packed[r0:r0 + n].reshape(-1)[:size].reshape(shapes[name])
            r0 += n
        r0 += -r0 % 8
    return out


def _permute_heads(a, axis):
    shp = a.shape
    a = a.reshape(shp[:axis] + (2, 4, HEAD_DIM) + shp[axis + 1:])
    a = jnp.swapaxes(a, axis, axis + 1)
    return a.reshape(shp)


def _unpermute_heads(a, axis):
    shp = a.shape
    a = a.reshape(shp[:axis] + (4, 2, HEAD_DIM) + shp[axis + 1:])
    a = jnp.swapaxes(a, axis, axis + 1)
    return a.reshape(shp)


def _permute_w_out(w):
    return jnp.concatenate([_permute_heads(w[:D_ATTN], 0), w[D_ATTN:]], axis=0)


def _unpermute_w_out(w):
    return jnp.concatenate([_unpermute_heads(w[:D_ATTN], 0), w[D_ATTN:]], axis=0)


def kernel(x, c, w_ada, b_ada, norm_gain, w_in, q_gain, k_gain, sink, w_s, b_s, w_out, loss_target, m_w_ada, m_b_ada, m_norm_gain, m_w_in, m_q_gain, m_k_gain, m_sink, m_w_s, m_b_s, m_w_out, v_w_ada, v_b_ada, v_norm_gain, v_w_in, v_q_gain, v_k_gain, v_sink, v_w_s, v_b_s, v_w_out):
    ix, iy, ic = _coords()
    chip = 2 * ix + iy
    chip_idx = jnp.reshape(chip, (1,)).astype(jnp.int32)
    core_idx = jnp.reshape(ic, (1,)).astype(jnp.int32)
    src_chunks = lax.dynamic_index_in_dim(jnp.asarray(_CHUNK_SRC), chip, 0, keepdims=False)
    pos_chunks = lax.dynamic_index_in_dim(jnp.asarray(_CHUNK_POS), chip, 0, keepdims=False)
    x0, target = x[0], loss_target[0]

    wt, mt, vt = (jnp.swapaxes(a, 1, 2) for a in (w_in, m_w_in, v_w_in))
    wloc_in = _cast_permute_w_in(wt, src_chunks)
    c_all, wloc_out, w0 = _gather_inputs(c, w_out, wloc_in)
    wts = [w0.reshape(D_IN, D_MODEL), None]

    b_blk = lax.dynamic_slice_in_dim(b_ada, chip * W_ADA_BLK, W_ADA_BLK, axis=1)
    ada_part, cond = _ada_rows(c_all.reshape(N_DEV, D_MODEL), w_ada, b_blk)
    ada = jnp.moveaxis(_exchange_ada(ada_part), 0, 1).reshape(DEPTH, 3 * D_MODEL)
    shift = [ada[l:l + 1, 0:D_MODEL] for l in range(DEPTH)]
    scale = [ada[l:l + 1, D_MODEL:2 * D_MODEL] for l in range(DEPTH)]
    gate = [ada[l:l + 1, 2 * D_MODEL:] for l in range(DEPTH)]
    ng = [norm_gain[l:l + 1] for l in range(DEPTH)]

    qg2 = jnp.concatenate([q_gain, q_gain], axis=-1)
    kg2 = jnp.concatenate([k_gain, k_gain], axis=-1)
    ws_b = w_s.astype(BF16)
    wst_b = jnp.swapaxes(w_s, -1, -2).astype(BF16)
    bsp = jnp.repeat(jnp.swapaxes(b_s.reshape(DEPTH, 4, 2, BLOCK), -1, -2), HEAD_DIM, axis=-1)
    bias = jnp.asarray(_bias_table())

    def mix_args(l):
        return bias, sink[l], qg2[l:l + 1], kg2[l:l + 1], ws_b[l]

    w_out_shape = (W_OUT_BLK, D_MODEL)
    proj0, w1 = _proj_fwd(x0, ng[0], scale[0], shift[0], wts[0], "proj_fwd_0",
                          jobs=[_job_gather([(wloc_in, 1)], [(W_IN_BLK, D_MODEL)])])
    y0, wo0, wo1 = _mix_fwd(proj0, *mix_args(0), bsp[0], "mix_fwd_0",
                            jobs=[_job_gather([(wloc_out, 0), (wloc_out, 1)], [w_out_shape, w_out_shape])])
    wts[1] = w1.reshape(D_IN, D_MODEL)
    wos = [_permute_w_out(w.reshape(D_MODEL, D_MODEL)) for w in (wo0, wo1)]
    x1, = _out_fwd(y0, x0, gate[0], wos[0], "out_fwd_0")
    proj1, = _proj_fwd(x1, ng[1], scale[1], shift[1], wts[1], "proj_fwd_1")
    y1, = _mix_fwd(proj1, *mix_args(1), bsp[1], "mix_fwd_1")
    dx2, sq = _out_fwd_loss(y1, x1, gate[1], wos[1], target, "out_fwd_loss_1")
    loss = lax.psum(0.5 * sq[0, 0], ("x", "y", "c"))

    def blocks_out(gw):
        return _unpermute_w_out(gw).reshape(N_CHIPS, W_OUT_BLK, D_MODEL)

    dy1, gwo1, dgate1 = _out_bwd(dx2, y1, gate[1], wos[1], "out_bwd_1")
    go1 = blocks_out(gwo1)
    dpb, dkv, p0, p2, dqg1, dkg1, dsink1, dws1, dbsp1, ro1 = _mix_bwd(
        dy1, proj1, *mix_args(1), wst_b[1], bsp[1], "mix_bwd_1", jobs=[_job_swap(go1)])
    po1, = _pair_sum(go1, ro1, core_idx, "pair_sum_w_out_1")
    gwi1, dkvb, co1 = _w_in_grad(dpb, dkv, p0, p2, x1, ng[1], scale[1], shift[1], "w_in_grad_1", jobs=[_job_scatter(po1)])
    gi1 = gwi1.reshape(N_CHIPS, W_IN_BLK, D_MODEL)
    fo1 = _chip_sum(po1, co1, chip_idx, "chip_sum_w_out_1")
    dx1, dsh1, dsc1, dng1, grad_wo1, ri1 = _proj_bwd(dpb, dkvb, x1, dx2, ng[1], scale[1], wts[1], "proj_bwd_1",
                                                     jobs=[_job_join(fo1), _job_swap(gi1)])
    pi1, = _pair_sum(gi1, ri1, core_idx, "pair_sum_w_in_1")

    dy0, gwo0, dgate0 = _out_bwd(dx1, y0, gate[0], wos[0], "out_bwd_0")
    go0 = blocks_out(gwo0)
    dpb, dkv, p0, p2, dqg0, dkg0, dsink0, dws0, dbsp0, ci1, ro0 = _mix_bwd(
        dy0, proj0, *mix_args(0), wst_b[0], bsp[0], "mix_bwd_0", jobs=[_job_scatter(pi1), _job_swap(go0)])
    fi1 = _chip_sum(pi1, ci1, chip_idx, "chip_sum_w_in_1")
    po0, = _pair_sum(go0, ro0, core_idx, "pair_sum_w_out_0")

    def bs_grad(dbsp):
        return jnp.swapaxes(dbsp[:, :, ::HEAD_DIM], -1, -2).reshape(8, BLOCK)

    small_g = dict(
        w_s=jnp.stack([dws0, dws1]), b_s=jnp.stack([bs_grad(dbsp0), bs_grad(dbsp1)]),
        q_gain=jnp.stack([dqg0[0, :HEAD_DIM], dqg1[0, :HEAD_DIM]]), k_gain=jnp.stack([dkg0[0, :HEAD_DIM], dkg1[0, :HEAD_DIM]]),
        sink=jnp.stack([dsink0[0, :N_HEADS], dsink1[0, :N_HEADS]]))
    gwi0, dkvb, grad_wi1, co0, gathered_a = _w_in_grad(
        dpb, dkv, p0, p2, x0, ng[0], scale[0], shift[0], "w_in_grad_0",
        jobs=[_job_join(fi1), _job_scatter(po0), _job_all_gather(_pack_rows(small_g, _SMALL_A))])
    gi0 = gwi0.reshape(N_CHIPS, W_IN_BLK, D_MODEL)
    fo0 = _chip_sum(po0, co0, chip_idx, "chip_sum_w_out_0")

    ri0, grad_wo0 = _comm([_job_swap(gi0), _job_join(fo0)], "swap_w_in_0")
    pi0, pi0_send = _pair_sum(gi0, ri0, core_idx, "pair_sum_w_in_0", send_dtype=BF16)
    dx0, dsh0, dsc0, dng0, ci0 = _proj_bwd(dpb, dkvb, x0, dx1, ng[0], scale[0], wts[0], "proj_bwd_0",
                                           jobs=[_job_scatter(pi0_send)])
    fi0 = _chip_sum(pi0, ci0, chip_idx, "chip_sum_w_in_0")
    grad_wi0, = _comm([_job_join(fi0)], "join_w_in_0")

    small_g.update(
        b_ada=jnp.stack([jnp.concatenate([dsh0, dsc0, dgate0], axis=-1)[0], jnp.concatenate([dsh1, dsc1, dgate1], axis=-1)[0]]),
        norm_gain=jnp.stack([dng0[0], dng1[0]]))
    small_w = dict(w_s=w_s, b_s=b_s, b_ada=b_ada, norm_gain=norm_gain, q_gain=q_gain, k_gain=k_gain, sink=sink)
    small_m = dict(w_s=m_w_s, b_s=m_b_s, b_ada=m_b_ada, norm_gain=m_norm_gain, q_gain=m_q_gain, k_gain=m_k_gain, sink=m_sink)
    small_v = dict(w_s=v_w_s, b_s=v_b_s, b_ada=v_b_ada, norm_gain=v_norm_gain, q_gain=v_q_gain, k_gain=v_k_gain, sink=v_sink)
    gathered_b = _all_gather_rows(_pack_rows(small_g, _SMALL_B))
    packed = _small_sum_adamw(gathered_a.reshape(N_DEV, -1, 128), gathered_b.reshape(N_DEV, -1, 128),
                              _pack_small(small_w), _pack_small(small_m), _pack_small(small_v))
    shapes = {k: a.shape for k, a in small_w.items()}
    sg, sd, sm, sv = (_unpack_small(p, shapes) for p in packed)

    dada_all = gathered_b.reshape(N_DEV, -1, 128)[:, 0:DEPTH * 24].reshape(N_DEV, DEPTH, 3 * D_MODEL)
    dada_blk = jnp.moveaxis(lax.dynamic_slice_in_dim(dada_all, chip * W_ADA_BLK, W_ADA_BLK, axis=2), 0, 1)
    pad = BLOCK - N_DEV
    ada_out = _w_ada_grad_adamw(
        jnp.pad(cond.T, ((0, 0), (0, pad))).astype(BF16), jnp.pad(dada_blk, ((0, 0), (0, pad), (0, 0))),
        w_ada, m_w_ada, v_w_ada)

    in_out = [jnp.swapaxes(a, 1, 2) for a in _adamw_w_in(wt, grad_wi0, grad_wi1, mt, vt, pos_chunks)]
    out_out = _adamw_w_out(w_out, grad_wo0, grad_wo1, m_w_out, v_w_out)

    def ordered(k):
        small = (sg, sd, sm, sv)[k]
        return (ada_out[k], small["b_ada"], small["norm_gain"], in_out[k], small["q_gain"], small["k_gain"], small["sink"],
                small["w_s"], small["b_s"], out_out[k])

    return (loss, dx0[None], *ordered(0), *ordered(1), *ordered(2), *ordered(3))
```

```python
import numpy as np

import jax
import jax.numpy as jnp
from jax import lax
from jax.experimental import pallas as pl
from jax.experimental.pallas import tpu as pltpu

F32 = jnp.float32
BF16 = jnp.bfloat16

D_MODEL = 1024
DEPTH = 2
HEAD_DIM = 64
N_HEADS = 8
BLOCK = 128
SUB = 4
TILE = SUB * BLOCK
D_ATTN = 512
D_KV = 128
D_IN = 2816
N_CHIPS = 4
N_DEV = 8
W_IN_BLK = D_IN // N_CHIPS
W_OUT_BLK = D_MODEL // N_CHIPS
W_ADA_BLK = 3 * D_MODEL // N_CHIPS
CHUNK_ROWS = HEAD_DIM
N_CHUNKS = W_IN_BLK // CHUNK_ROWS
EPS = 1e-6
NEG_INF = -1e30

C_Q, C_K, C_V, C_GA, C_U, C_VG, C_GG = 0, 512, 640, 768, 1280, 1792, 2304

ADAM_LR = 0.001
ADAM_B1 = 0.9
ADAM_B2 = 0.999
ADAM_EPS = 1e-08
ADAM_WD = 0.01
ADAM_STEP = 10

MESH = pl.DeviceIdType.MESH
MIB = 1024 * 1024
ANY = pl.BlockSpec(memory_space=pl.ANY)
VMEM = pl.BlockSpec(memory_space=pltpu.VMEM)

NT_DIMS = (((1,), (1,)), ((), ()))
TN_DIMS = (((0,), (0,)), ((), ()))

_PAIR_ORDER = (0, 4, 1, 5, 2, 6, 3, 7)
_CHUNK_SRC = np.array([
    list(_PAIR_ORDER) + [8, 9, 10],
    [0] + [1 + h for h in _PAIR_ORDER] + [9, 10],
    list(range(N_CHUNKS)),
    list(range(N_CHUNKS)),
], np.int32)
_CHUNK_POS = np.argsort(_CHUNK_SRC, axis=1).astype(np.int32)


def _bias_table():
    i = np.arange(N_HEADS * BLOCK)[:, None]
    j = np.arange(3 * BLOCK)[None, :]
    dist = np.abs(j - BLOCK - (i % BLOCK))
    slope = 2.0 ** -(i // BLOCK + 1.0)
    inner = np.where(dist <= BLOCK, -(slope * dist), NEG_INF)
    first = np.where(j >= BLOCK, inner, NEG_INF)
    last = np.where(j < 2 * BLOCK, inner, NEG_INF)
    return np.stack([first, inner, last]).astype(np.float32)


def _full(shape):
    n = len(shape)
    return pl.BlockSpec(shape, lambda *_: (0,) * n)


def _sds(shape, dtype=F32):
    return jax.ShapeDtypeStruct(shape, dtype)


def _coords():
    return lax.axis_index("x"), lax.axis_index("y"), lax.axis_index("c")


def _other_chips(x, y):
    return [(1 - x, y), (x, 1 - y), (1 - x, 1 - y)]


def _remote(src, dst, sems, dev):
    return pltpu.make_async_remote_copy(src_ref=src, dst_ref=dst, send_sem=sems[0], recv_sem=sems[1],
                                        device_id=dev, device_id_type=MESH)


class _Job:
    def __init__(self, inputs, out_shapes, n_remote, n_local, make, then=None, in_place=False):
        self.inputs, self.out_shapes, self.n_remote, self.n_local, self.make = inputs, out_shapes, n_remote, n_local, make
        self.then = then
        self.in_place = in_place


def _job_aliases(jobs, in_base, out_base):
    aliases, a, b = {}, 0, 0
    for j in jobs:
        if j.in_place:
            aliases.update({in_base + a + k: out_base + b + k for k in range(len(j.inputs))})
        a, b = a + len(j.inputs), b + len(j.out_shapes)
    return aliases


def _job_copies(jobs, jin, jout, sems, second=False):
    send, recv, loc = sems
    res, a, b, r, l = [], 0, 0, 0, 0
    for j in jobs:
        build = j.then if second else j.make
        if build is not None:
            res += build(jin[a:a + len(j.inputs)], jout[b:b + len(j.out_shapes)],
                         lambda k, r=r: (send.at[r + k], recv.at[r + k]), lambda k, l=l: loc.at[l + k])
        a, b, r, l = a + len(j.inputs), b + len(j.out_shapes), r + j.n_remote, l + j.n_local
    return res


def _run(copies):
    for cp in copies:
        cp.start()
    for cp in copies:
        cp.wait()


def _job_gather(sources, shapes):
    n = len(sources)

    def make(ins, outs, rsem, lsem):
        x, y, c = _coords()
        j = 2 * x + y
        res = []
        for t, ((_, layer), src, dst) in enumerate(zip(sources, ins, outs)):
            src = src if layer is None else src.at[layer]
            res.append(pltpu.make_async_copy(src, dst.at[j], lsem(t)))
            for k, chip in enumerate(_other_chips(x, y)):
                res.append(_remote(src, dst.at[j], rsem(3 * t + k), (*chip, c)))
        return res

    return _Job([a for a, _ in sources], [_sds((N_CHIPS,) + s, BF16) for s in shapes], 3 * n, n, make)


def _job_swap(g):
    _, rows, cols = g.shape
    half = rows // 2

    def make(ins, outs, rsem, lsem):
        x, y, c = _coords()
        return [_remote(ins[0].at[:, pl.ds((1 - c) * half, half), :], outs[0], rsem(0), (x, y, 1 - c))]

    return _Job([g], [_sds((N_CHIPS, half, cols))], 1, 0, make)


def _job_scatter(p):
    def make(ins, outs, rsem, lsem):
        x, y, c = _coords()
        return [_remote(ins[0].at[2 * chip[0] + chip[1]], outs[0].at[k], rsem(k), (*chip, c))
                for k, chip in enumerate(_other_chips(x, y))]

    return _Job([p], [_sds((3,) + p.shape[1:], p.dtype)], 3, 0, make)


def _job_all_gather(blk):
    m_per = blk.shape[0]

    def rows(ref, px, py, pc):
        return ref.at[pl.ds((4 * px + 2 * py + pc) * m_per, m_per), :]

    def make(ins, outs, rsem, lsem):
        x, y, c = _coords()
        res = [pltpu.make_async_copy(ins[0], rows(outs[0], x, y, c), lsem(0)),
               _remote(ins[0], rows(outs[0], x, y, c), rsem(0), (x, y, 1 - c))]
        res += [_remote(ins[0], rows(outs[0], x, y, c), rsem(1 + k), (*chip, c)) for k, chip in enumerate(_other_chips(x, y))]
        return res

    def then(ins, outs, rsem, lsem):
        x, y, c = _coords()
        return [_remote(rows(outs[0], *chip, c), rows(outs[0], *chip, c), rsem(4 + k), (x, y, 1 - c))
                for k, chip in enumerate(_other_chips(x, y))]

    return _Job([blk], [_sds((N_DEV * m_per, blk.shape[1]), blk.dtype)], 7, 1, make, then)


def _job_join(f):
    half = f.shape[0] // 2

    def make(ins, outs, rsem, lsem):
        x, y, c = _coords()
        mine = pl.ds(c * half, half)
        return [_remote(ins[0].at[mine, :], outs[0].at[mine, :], rsem(0), (x, y, 1 - c))]

    return _Job([f], [_sds(f.shape, f.dtype)], 1, 0, make, in_place=True)


def _pallas(body, *, name, grid, in_specs, out_specs, out_shape, operands, vmem_mib, jobs=()):
    in_specs, out_specs, out_shape = list(in_specs), list(out_specs), list(out_shape)
    n_in, n_out = len(in_specs), len(out_specs)
    j_in = [a for j in jobs for a in j.inputs]
    j_out = [s for j in jobs for s in j.out_shapes]
    n_rem = max(1, sum(j.n_remote for j in jobs))
    n_loc = max(1, sum(j.n_local for j in jobs))
    scratch = [pltpu.SemaphoreType.DMA((n_rem,)), pltpu.SemaphoreType.DMA((n_rem,)),
               pltpu.SemaphoreType.DMA((n_loc,))] if jobs else []

    def wrapped(*refs):
        ins = refs[:n_in]
        jin = refs[n_in:n_in + len(j_in)]
        outs = refs[n_in + len(j_in):n_in + len(j_in) + n_out]
        jout = refs[n_in + len(j_in) + n_out:n_in + len(j_in) + n_out + len(j_out)]

        if jobs:
            first = last = None
            for d, n in enumerate(grid):
                f, e = pl.program_id(d) == 0, pl.program_id(d) == n - 1
                first, last = (f, e) if first is None else (first & f, last & e)

            @pl.when(first)
            def _():
                for cp in _job_copies(jobs, jin, jout, refs[-3:]):
                    cp.start()

        body(*ins, *outs)

        if jobs:
            @pl.when(last)
            def _():
                for cp in _job_copies(jobs, jin, jout, refs[-3:]):
                    cp.wait()
                _run(_job_copies(jobs, jin, jout, refs[-3:], second=True))

    return pl.pallas_call(
        wrapped, name=name, grid=grid,
        in_specs=in_specs + [ANY] * len(j_in), out_specs=out_specs + [ANY] * len(j_out),
        out_shape=out_shape + j_out, scratch_shapes=scratch, input_output_aliases=_job_aliases(jobs, n_in, n_out),
        compiler_params=pltpu.CompilerParams(dimension_semantics=("arbitrary",) * len(grid),
                                             vmem_limit_bytes=vmem_mib * MIB),
    )(*operands, *j_in)


def _comm(jobs, name):
    j_in = [a for j in jobs for a in j.inputs]
    j_out = [s for j in jobs for s in j.out_shapes]
    n_rem = max(1, sum(j.n_remote for j in jobs))
    n_loc = max(1, sum(j.n_local for j in jobs))

    def body(*refs):
        jin, jout = refs[:len(j_in)], refs[len(j_in):len(j_in) + len(j_out)]
        _run(_job_copies(jobs, jin, jout, refs[-3:]))
        _run(_job_copies(jobs, jin, jout, refs[-3:], second=True))

    return pl.pallas_call(
        body, name=name, in_specs=[ANY] * len(j_in), out_specs=[ANY] * len(j_out), out_shape=j_out,
        scratch_shapes=[pltpu.SemaphoreType.DMA((n_rem,)), pltpu.SemaphoreType.DMA((n_rem,)),
                        pltpu.SemaphoreType.DMA((n_loc,))],
        input_output_aliases=_job_aliases(jobs, 0, 0),
    )(*j_in)


def _sigmoid(x):
    return 1.0 / (1.0 + jnp.exp(-x))


def _lo_mask(shape):
    return lax.broadcasted_iota(jnp.int32, shape, len(shape) - 1) < HEAD_DIM


def _half_sum(x, lo):
    a = jnp.sum(jnp.where(lo, x, 0.0), axis=-1, keepdims=True)
    b = jnp.sum(jnp.where(lo, 0.0, x), axis=-1, keepdims=True)
    return jnp.where(lo, a, b)


def _half_rms_scale(x, lo):
    return lax.rsqrt(_half_sum(x * x, lo) * (1.0 / HEAD_DIM) + EPS)


def _stack_heads(pairs, lo):
    return jnp.concatenate([jnp.where(lo, t, 0.0) for t in pairs] + [jnp.where(lo, 0.0, t) for t in pairs], axis=0)


def _unstack_pair(stack, p, lo):
    return jnp.where(lo, stack[BLOCK * p:BLOCK * (p + 1)], stack[BLOCK * (4 + p):BLOCK * (5 + p)])


def _attention_probs(q_stack, kn, bias_ref, sink_ref):
    rows = N_HEADS * BLOCK
    s = lax.dot_general(q_stack, kn, NT_DIMS, preferred_element_type=F32) + bias_ref[...]
    sink = jnp.concatenate([jnp.full((BLOCK, BLOCK), sink_ref[h], F32) for h in range(N_HEADS)], axis=0)
    cols = [s[:, BLOCK * j:BLOCK * (j + 1)] for j in range(3)]
    top = jnp.max(jnp.maximum(jnp.maximum(cols[0], cols[1]), cols[2]), axis=-1, keepdims=True)
    m = jnp.maximum(jnp.broadcast_to(top, (rows, BLOCK)), sink)
    e = [jnp.exp(c - m) for c in cols]
    es = jnp.exp(sink - m)
    inv = 1.0 / (jnp.broadcast_to(jnp.sum((e[0] + e[1]) + e[2], axis=-1, keepdims=True), (rows, BLOCK)) + es)
    return jnp.concatenate([c * inv for c in e], axis=1), es * inv


def _kv_rows(cur_ref, pkv_ref, nkv_ref):
    k = jnp.concatenate([pkv_ref[:, 0:D_KV], cur_ref[:, C_K:C_K + D_KV], nkv_ref[:, 0:D_KV]], axis=0)
    v = jnp.concatenate([pkv_ref[:, D_KV:2 * D_KV], cur_ref[:, C_V:C_V + D_KV], nkv_ref[:, D_KV:2 * D_KV]], axis=0)
    return k, v


def _overlap_add(parts):
    blocks = []
    for j in range(SUB + 2):
        terms = [parts[b][BLOCK * (j - b):BLOCK * (j - b + 1)] for b in range(SUB) if 0 <= j - b <= 2]
        total = terms[0]
        for t in terms[1:]:
            total = total + t
        blocks.append(total)
    return jnp.concatenate(blocks, axis=0)


def _mix_specs(nt):
    cur = pl.BlockSpec((TILE, D_IN), lambda i: (i, 0))
    kv_col = C_K // (2 * D_KV)
    pkv = pl.BlockSpec((BLOCK, 2 * D_KV), lambda i: (jnp.maximum(i * SUB - 1, 0), kv_col))
    nkv = pl.BlockSpec((BLOCK, 2 * D_KV), lambda i: (jnp.minimum((i + 1) * SUB, nt * SUB - 1), kv_col))
    table = (None, N_HEADS * BLOCK, 3 * BLOCK)
    first = pl.BlockSpec(table, lambda i: (jnp.where(i == 0, 0, 1), 0, 0))
    inner = pl.BlockSpec(table, lambda i: (1, 0, 0))
    last = pl.BlockSpec(table, lambda i: (jnp.where(i == nt - 1, 2, 1), 0, 0))
    return cur, pkv, nkv, [first] + [inner] * (SUB - 2) + [last]


def _proj_fwd(x, ng, scale, shift, wt, name, jobs=()):
    s = x.shape[0]
    ts = min(512, s)

    def body(x_ref, ng_ref, sc_ref, sh_ref, w_ref, o_ref):
        xv = x_ref[...]
        r = lax.rsqrt(jnp.mean(xv * xv, axis=-1, keepdims=True) + EPS)
        h = ((xv * r) * ng_ref[...]) * (1.0 + sc_ref[...]) + sh_ref[...]
        o_ref[...] = lax.dot_general(h.astype(BF16), w_ref[...], NT_DIMS, preferred_element_type=F32)

    vec = _full((1, D_MODEL))
    return _pallas(
        body, name=name, grid=(s // ts,),
        in_specs=[pl.BlockSpec((ts, D_MODEL), lambda i: (i, 0)), vec, vec, vec, _full((D_IN, D_MODEL))],
        out_specs=[pl.BlockSpec((ts, D_IN), lambda i: (i, 0))], out_shape=[_sds((s, D_IN))],
        operands=(x, ng, scale, shift, wt), vmem_mib=48, jobs=jobs)


def _mix_fwd(proj, bias, sink, qg2, kg2, ws, bsp, name, jobs=()):
    s = proj.shape[0]
    nt = s // TILE

    def body(sink_ref, cur_ref, pkv_ref, nkv_ref, *rest):
        bias_refs = rest[:SUB]
        qg_ref, kg_ref, ws_ref, bsp_ref, y_ref = rest[SUB:]
        lo = _lo_mask((BLOCK, BLOCK))
        lo_kv = _lo_mask((TILE + 2 * BLOCK, BLOCK))
        k_all, v_all = _kv_rows(cur_ref, pkv_ref, nkv_ref)
        kn_all = ((k_all * _half_rms_scale(k_all, lo_kv)) * kg_ref[...]).astype(BF16)
        vb_all = v_all.astype(BF16)
        for b in range(SUB):
            rows = slice(BLOCK * b, BLOCK * (b + 1))
            window = slice(BLOCK * b, BLOCK * (b + 3))
            qn = []
            for p in range(4):
                q = cur_ref[rows, C_Q + BLOCK * p:C_Q + BLOCK * (p + 1)]
                qn.append(((q * _half_rms_scale(q, lo)) * qg_ref[...]) * 0.125)
            q_stack = _stack_heads(qn, lo).astype(BF16)
            prob, _ = _attention_probs(q_stack, kn_all[window], bias_refs[b], sink_ref)
            o_stack = jnp.dot(prob.astype(BF16), vb_all[window], preferred_element_type=F32)
            for p in range(4):
                g = cur_ref[rows, C_GA + BLOCK * p:C_GA + BLOCK * (p + 1)]
                y_ref[rows, BLOCK * p:BLOCK * (p + 1)] = (_unstack_pair(o_stack, p, lo) * (g * _sigmoid(g))).astype(BF16)
            for p in range(4):
                vg = cur_ref[rows, C_VG + BLOCK * p:C_VG + BLOCK * (p + 1)]
                vn = (vg * _half_rms_scale(vg, lo)).astype(BF16)
                sv = jnp.where(lo, jnp.dot(ws_ref[2 * p], vn, preferred_element_type=F32),
                               jnp.dot(ws_ref[2 * p + 1], vn, preferred_element_type=F32)) + bsp_ref[p]
                u = cur_ref[rows, C_U + BLOCK * p:C_U + BLOCK * (p + 1)]
                g = cur_ref[rows, C_GG + BLOCK * p:C_GG + BLOCK * (p + 1)]
                y_ref[rows, D_ATTN + BLOCK * p:D_ATTN + BLOCK * (p + 1)] = ((u * sv) * (g * _sigmoid(g))).astype(BF16)

    cur, pkv, nkv, bias_specs = _mix_specs(nt)
    return _pallas(
        body, name=name, grid=(nt,),
        in_specs=[pl.BlockSpec(memory_space=pltpu.SMEM), cur, pkv, nkv, *bias_specs, _full((1, BLOCK)), _full((1, BLOCK)),
                  _full((8, BLOCK, BLOCK)), _full((4, BLOCK, BLOCK))],
        out_specs=[pl.BlockSpec((TILE, D_MODEL), lambda i: (i, 0))], out_shape=[_sds((s, D_MODEL), BF16)],
        operands=(sink, proj, proj, proj, *([bias] * SUB), qg2, kg2, ws, bsp), vmem_mib=48, jobs=jobs)


def _out_fwd(y, x, gate, w_out, name):
    s = x.shape[0]
    ts = min(512, s)

    def body(y_ref, x_ref, g_ref, w_ref, o_ref):
        o_ref[...] = x_ref[...] + g_ref[...] * jnp.dot(y_ref[...], w_ref[...], preferred_element_type=F32)

    row = pl.BlockSpec((ts, D_MODEL), lambda i: (i, 0))
    return _pallas(
        body, name=name, grid=(s // ts,), in_specs=[row, row, _full((1, D_MODEL)), _full((D_MODEL, D_MODEL))],
        out_specs=[row], out_shape=[_sds((s, D_MODEL))], operands=(y, x, gate, w_out), vmem_mib=32)


def _out_fwd_loss(y, x, gate, w_out, target, name):
    s = x.shape[0]
    ts = min(512, s)

    def body(y_ref, x_ref, g_ref, w_ref, t_ref, dx_ref, sq_ref):
        @pl.when(pl.program_id(0) == 0)
        def _():
            sq_ref[...] = jnp.zeros_like(sq_ref)

        out = x_ref[...] + g_ref[...] * jnp.dot(y_ref[...], w_ref[...], preferred_element_type=F32)
        diff = out - t_ref[...]
        dx_ref[...] = diff * (1.0 / D_MODEL)
        per_token = jnp.sum(diff * diff, axis=-1, keepdims=True) * (1.0 / D_MODEL)
        sq_ref[...] += jnp.sum(per_token, axis=0, keepdims=True)

    row = pl.BlockSpec((ts, D_MODEL), lambda i: (i, 0))
    return _pallas(
        body, name=name, grid=(s // ts,), in_specs=[row, row, _full((1, D_MODEL)), _full((D_MODEL, D_MODEL)), row],
        out_specs=[row, _full((1, 1))], out_shape=[_sds((s, D_MODEL)), _sds((1, 1))],
        operands=(y, x, gate, w_out, target), vmem_mib=40)


def _out_bwd(dxo, y, gate, w_out, name, jobs=()):
    s = dxo.shape[0]
    ts = min(512, s)

    def body(dx_ref, y_ref, g_ref, w_ref, dy_ref, gw_ref, dg_ref):
        @pl.when(pl.program_id(0) == 0)
        def _():
            gw_ref[...] = jnp.zeros_like(gw_ref)
            dg_ref[...] = jnp.zeros_like(dg_ref)

        dx = dx_ref[...]
        do = (dx * g_ref[...]).astype(BF16)
        yv = y_ref[...]
        dy_ref[...] = lax.dot_general(do, w_ref[...], NT_DIMS, preferred_element_type=F32)
        gw_ref[...] += lax.dot_general(yv, do, TN_DIMS, preferred_element_type=F32)
        o = jnp.dot(yv, w_ref[...], preferred_element_type=F32)
        dg_ref[...] += jnp.sum(dx * o, axis=0, keepdims=True)

    row = pl.BlockSpec((ts, D_MODEL), lambda i: (i, 0))
    return _pallas(
        body, name=name, grid=(s // ts,), in_specs=[row, row, _full((1, D_MODEL)), _full((D_MODEL, D_MODEL))],
        out_specs=[row, _full((D_MODEL, D_MODEL)), _full((1, D_MODEL))],
        out_shape=[_sds((s, D_MODEL)), _sds((D_MODEL, D_MODEL)), _sds((1, D_MODEL))],
        operands=(dxo, y, gate, w_out), vmem_mib=48, jobs=jobs)


def _mix_bwd(dy, proj, bias, sink, qg2, kg2, ws, wst, bsp, name, jobs=()):
    s = proj.shape[0]
    nt = s // TILE

    def body(sink_ref, dy_ref, cur_ref, pkv_ref, nkv_ref, *rest):
        bias_refs = rest[:SUB]
        (qg_ref, kg_ref, ws_ref, wst_ref, bsp_ref,
         dpb_ref, dkv_ref, p0_ref, p2_ref, dqg_ref, dkg_ref, dsink_ref, dws_ref, dbsp_ref) = rest[SUB:]

        def put(rows, col, value):
            dpb_ref[rows, col:col + BLOCK] = value.astype(BF16)

        @pl.when(pl.program_id(0) == 0)
        def _():
            dqg_ref[...] = jnp.zeros_like(dqg_ref)
            dkg_ref[...] = jnp.zeros_like(dkg_ref)
            dsink_ref[...] = jnp.zeros_like(dsink_ref)
            dws_ref[...] = jnp.zeros_like(dws_ref)
            dbsp_ref[...] = jnp.zeros_like(dbsp_ref)

        lo = _lo_mask((BLOCK, BLOCK))
        lo_kv = _lo_mask((TILE + 2 * BLOCK, BLOCK))
        lane_row = lax.broadcasted_iota(jnp.int32, (1, BLOCK), 1)
        qg = qg_ref[...]
        kg = kg_ref[...]

        k_all, v_all = _kv_rows(cur_ref, pkv_ref, nkv_ref)
        rk = _half_rms_scale(k_all, lo_kv)
        khat = k_all * rk
        kn_all = (khat * kg).astype(BF16)
        vb_all = v_all.astype(BF16)

        dkn_parts, dv_parts = [], []
        dsink = jnp.zeros((1, BLOCK), F32)
        dqg = jnp.zeros((1, BLOCK), F32)
        for b in range(SUB):
            rows = slice(BLOCK * b, BLOCK * (b + 1))
            window = slice(BLOCK * b, BLOCK * (b + 3))
            kn, vb = kn_all[window], vb_all[window]

            qhat, rq = [], []
            for p in range(4):
                q = cur_ref[rows, C_Q + BLOCK * p:C_Q + BLOCK * (p + 1)]
                r = _half_rms_scale(q, lo)
                rq.append(r)
                qhat.append(q * r)
            q_stack = _stack_heads([(qh * qg) * 0.125 for qh in qhat], lo).astype(BF16)
            prob, psink = _attention_probs(q_stack, kn, bias_refs[b], sink_ref)
            pb = prob.astype(BF16)
            o_stack = jnp.dot(pb, vb, preferred_element_type=F32)

            dout = []
            for p in range(4):
                g = cur_ref[rows, C_GA + BLOCK * p:C_GA + BLOCK * (p + 1)]
                sg = _sigmoid(g)
                dya = dy_ref[rows, BLOCK * p:BLOCK * (p + 1)]
                attn = _unstack_pair(o_stack, p, lo)
                put(rows, C_GA + BLOCK * p, dya * attn * (sg * (1.0 + g * (1.0 - sg))))
                dout.append(dya * (g * sg))
            do_stack = _stack_heads(dout, lo).astype(BF16)
            dp = lax.dot_general(do_stack, vb, NT_DIMS, preferred_element_type=F32)
            delta = jnp.sum(prob * dp, axis=-1, keepdims=True)
            dsb = (prob * (dp - delta)).astype(BF16)

            csink = -(psink * delta)
            for h in range(N_HEADS):
                tot = jnp.sum(csink[BLOCK * h:BLOCK * (h + 1)], axis=0, keepdims=True)
                dsink = dsink + jnp.where(lane_row == h, tot, 0.0)

            dq_stack = jnp.dot(dsb, kn, preferred_element_type=F32) * 0.125
            dkn_parts.append(lax.dot_general(dsb, q_stack, TN_DIMS, preferred_element_type=F32))
            dv_parts.append(lax.dot_general(pb, do_stack, TN_DIMS, preferred_element_type=F32))

            for p in range(4):
                dqn = _unstack_pair(dq_stack, p, lo)
                qh = qhat[p]
                dqg = dqg + jnp.sum(dqn * qh, axis=0, keepdims=True)
                dqh = dqn * qg
                mean = _half_sum(dqh * qh, lo) * (1.0 / HEAD_DIM)
                put(rows, C_Q + BLOCK * p, rq[p] * (dqh - qh * mean))

            for p in range(4):
                vg = cur_ref[rows, C_VG + BLOCK * p:C_VG + BLOCK * (p + 1)]
                r = _half_rms_scale(vg, lo)
                vnf = vg * r
                vn = vnf.astype(BF16)
                sv = jnp.where(lo, jnp.dot(ws_ref[2 * p], vn, preferred_element_type=F32),
                               jnp.dot(ws_ref[2 * p + 1], vn, preferred_element_type=F32)) + bsp_ref[p]
                u = cur_ref[rows, C_U + BLOCK * p:C_U + BLOCK * (p + 1)]
                g = cur_ref[rows, C_GG + BLOCK * p:C_GG + BLOCK * (p + 1)]
                sg = _sigmoid(g)
                dym = dy_ref[rows, D_ATTN + BLOCK * p:D_ATTN + BLOCK * (p + 1)]
                put(rows, C_GG + BLOCK * p, dym * (u * sv) * (sg * (1.0 + g * (1.0 - sg))))
                dgm = dym * (g * sg)
                put(rows, C_U + BLOCK * p, dgm * sv)
                dsv = dgm * u
                dsv_a = jnp.where(lo, dsv, 0.0)
                dsv_b = jnp.where(lo, 0.0, dsv)
                dws_ref[2 * p] += lax.dot_general(dsv_a.astype(BF16), vn, NT_DIMS, preferred_element_type=F32)
                dws_ref[2 * p + 1] += lax.dot_general(dsv_b.astype(BF16), vn, NT_DIMS, preferred_element_type=F32)
                dbsp_ref[p] += jnp.where(lo, jnp.sum(dsv_a, axis=-1, keepdims=True), jnp.sum(dsv_b, axis=-1, keepdims=True))
                dsvb = dsv.astype(BF16)
                dvn = jnp.where(lo, jnp.dot(wst_ref[2 * p], dsvb, preferred_element_type=F32),
                                jnp.dot(wst_ref[2 * p + 1], dsvb, preferred_element_type=F32))
                mean = _half_sum(dvn * vnf, lo) * (1.0 / HEAD_DIM)
                put(rows, C_VG + BLOCK * p, r * (dvn - vnf * mean))

        dsink_ref[...] += dsink
        dqg = jnp.broadcast_to(dqg, (8, BLOCK))
        dqg_ref[...] += dqg + pltpu.roll(dqg, HEAD_DIM, 1)

        dkn = _overlap_add(dkn_parts)
        dv = _overlap_add(dv_parts)
        dkg = jnp.broadcast_to(jnp.sum(dkn * khat, axis=0, keepdims=True), (8, BLOCK))
        dkg_ref[...] += dkg + pltpu.roll(dkg, HEAD_DIM, 1)
        dkh = dkn * kg
        dk = rk * (dkh - khat * (_half_sum(dkh * khat, lo_kv) * (1.0 / HEAD_DIM)))
        dpb_ref[:, C_K:C_GA] = jnp.zeros((TILE, 2 * D_KV), BF16)
        dkv_ref[:, 0:D_KV] = dk[BLOCK:BLOCK + TILE]
        dkv_ref[:, D_KV:2 * D_KV] = dv[BLOCK:BLOCK + TILE]
        p0_ref[:, 0:D_KV] = dk[0:BLOCK]
        p0_ref[:, D_KV:2 * D_KV] = dv[0:BLOCK]
        p2_ref[:, 0:D_KV] = dk[BLOCK + TILE:]
        p2_ref[:, D_KV:2 * D_KV] = dv[BLOCK + TILE:]

    cur, pkv, nkv, bias_specs = _mix_specs(nt)
    kv_blk = (BLOCK, 2 * D_KV)
    return _pallas(
        body, name=name, grid=(nt,),
        in_specs=[pl.BlockSpec(memory_space=pltpu.SMEM), pl.BlockSpec((TILE, D_MODEL), lambda i: (i, 0)),
                  cur, pkv, nkv, *bias_specs, _full((1, BLOCK)), _full((1, BLOCK)),
                  _full((8, BLOCK, BLOCK)), _full((8, BLOCK, BLOCK)), _full((4, BLOCK, BLOCK))],
        out_specs=[cur, pl.BlockSpec((TILE, 2 * D_KV), lambda i: (i, 0)),
                   pl.BlockSpec(kv_blk, lambda i: ((i + nt - 1) % nt, 0)),
                   pl.BlockSpec(kv_blk, lambda i: ((i + 1) % nt, 0)),
                   _full((8, BLOCK)), _full((8, BLOCK)), _full((1, BLOCK)),
                   _full((8, BLOCK, BLOCK)), _full((4, BLOCK, BLOCK))],
        out_shape=[_sds((s, D_IN), BF16), _sds((s, 2 * D_KV)), _sds((nt * BLOCK, 2 * D_KV)), _sds((nt * BLOCK, 2 * D_KV)),
                   _sds((8, BLOCK)), _sds((8, BLOCK)), _sds((1, BLOCK)),
                   _sds((8, BLOCK, BLOCK)), _sds((4, BLOCK, BLOCK))],
        operands=(sink, dy, proj, proj, proj, *([bias] * SUB), qg2, kg2, ws, wst, bsp), vmem_mib=56, jobs=jobs)


def _w_in_grad(dpb, dkv, p0, p2, x, ng, scale, shift, name, jobs=()):
    s = x.shape[0]
    ts = min(2 * TILE, s)
    tiles = ts // TILE

    def body(dpb_ref, dkv_ref, p0_ref, p2_ref, x_ref, ng_ref, sc_ref, sh_ref, gw_ref, dkvb_ref):
        @pl.when(pl.program_id(0) == 0)
        def _():
            gw_ref[...] = jnp.zeros_like(gw_ref)

        xv = x_ref[...]
        r = lax.rsqrt(jnp.mean(xv * xv, axis=-1, keepdims=True) + EPS)
        h = (((xv * r) * ng_ref[...]) * (1.0 + sc_ref[...]) + sh_ref[...]).astype(BF16)
        for t in range(tiles):
            halo = slice(BLOCK * t, BLOCK * (t + 1))
            first = slice(TILE * t, TILE * t + BLOCK)
            last = slice(TILE * (t + 1) - BLOCK, TILE * (t + 1))
            dkvb_ref[first, :] = (dkv_ref[first, :] + p2_ref[halo, :]).astype(BF16)
            if SUB > 2:
                inner = slice(TILE * t + BLOCK, TILE * (t + 1) - BLOCK)
                dkvb_ref[inner, :] = dkv_ref[inner, :].astype(BF16)
            dkvb_ref[last, :] = (dkv_ref[last, :] + p0_ref[halo, :]).astype(BF16)
        gw_ref[...] += lax.dot_general(dpb_ref[...], h, TN_DIMS, preferred_element_type=F32)
        gw_ref[C_K:C_GA, :] += lax.dot_general(dkvb_ref[...], h, TN_DIMS, preferred_element_type=F32)

    kv = pl.BlockSpec((ts, 2 * D_KV), lambda i: (i, 0))
    halo = pl.BlockSpec((tiles * BLOCK, 2 * D_KV), lambda i: (i, 0))
    vec = _full((1, D_MODEL))
    return _pallas(
        body, name=name, grid=(s // ts,),
        in_specs=[pl.BlockSpec((ts, D_IN), lambda i: (i, 0)), kv, halo, halo,
                  pl.BlockSpec((ts, D_MODEL), lambda i: (i, 0)), vec, vec, vec],
        out_specs=[_full((D_IN, D_MODEL)), kv], out_shape=[_sds((D_IN, D_MODEL)), _sds((s, 2 * D_KV), BF16)],
        operands=(dpb, dkv, p0, p2, x, ng, scale, shift), vmem_mib=56, jobs=jobs)


def _proj_bwd(dpb, dkvb, x, dxo, ng, scale, wt, name, jobs=()):
    s = x.shape[0]
    ts = min(512, s)

    def body(dpb_ref, dkvb_ref, x_ref, dxo_ref, ng_ref, sc_ref, w_ref, dxi_ref, dsh_ref, dsc_ref, dng_ref):
        @pl.when(pl.program_id(0) == 0)
        def _():
            dsh_ref[...] = jnp.zeros_like(dsh_ref)
            dsc_ref[...] = jnp.zeros_like(dsc_ref)
            dng_ref[...] = jnp.zeros_like(dng_ref)

        dh = (jnp.dot(dpb_ref[...], w_ref[...], preferred_element_type=F32)
              + jnp.dot(dkvb_ref[...], w_ref[C_K:C_GA, :], preferred_element_type=F32))

        xv = x_ref[...]
        r = lax.rsqrt(jnp.mean(xv * xv, axis=-1, keepdims=True) + EPS)
        xn = xv * r
        ngv = ng_ref[...]
        sc1 = 1.0 + sc_ref[...]
        dsh_ref[...] += jnp.sum(dh, axis=0, keepdims=True)
        dsc_ref[...] += jnp.sum(dh * (xn * ngv), axis=0, keepdims=True)
        dh1 = dh * sc1
        dng_ref[...] += jnp.sum(dh1 * xn, axis=0, keepdims=True)
        dxn = dh1 * ngv
        dxi_ref[...] = r * (dxn - xn * jnp.mean(dxn * xn, axis=-1, keepdims=True)) + dxo_ref[...]

    row = pl.BlockSpec((ts, D_MODEL), lambda i: (i, 0))
    vec = _full((1, D_MODEL))
    return _pallas(
        body, name=name, grid=(s // ts,),
        in_specs=[pl.BlockSpec((ts, D_IN), lambda i: (i, 0)), pl.BlockSpec((ts, 2 * D_KV), lambda i: (i, 0)),
                  row, row, vec, vec, _full((D_IN, D_MODEL))],
        out_specs=[row, vec, vec, vec],
        out_shape=[_sds((s, D_MODEL)), _sds((1, D_MODEL)), _sds((1, D_MODEL)), _sds((1, D_MODEL))],
        operands=(dpb, dkvb, x, dxo, ng, scale, wt), vmem_mib=48, jobs=jobs)


def _pair_sum(g, r, c_idx, name, send_dtype=None):
    _, rows, cols = g.shape
    half = rows // 2

    def body(c_ref, g_ref, r_ref, o_ref, *narrow):
        total = g_ref[...] + r_ref[...]
        o_ref[...] = total
        for n_ref in narrow:
            n_ref[...] = total.astype(n_ref.dtype)

    blk = (None, half, cols)
    out_blk = pl.BlockSpec(blk, lambda j, c: (j, 0, 0))
    shapes = [_sds((N_CHIPS, half, cols))] + ([_sds((N_CHIPS, half, cols), send_dtype)] if send_dtype else [])
    return pl.pallas_call(
        body, name=name,
        grid_spec=pltpu.PrefetchScalarGridSpec(
            num_scalar_prefetch=1, grid=(N_CHIPS,),
            in_specs=[pl.BlockSpec(blk, lambda j, c: (j, c[0], 0)), out_blk], out_specs=[out_blk] * len(shapes)),
        out_shape=shapes,
        compiler_params=pltpu.CompilerParams(dimension_semantics=("arbitrary",), vmem_limit_bytes=32 * MIB),
    )(c_idx, g, r)


def _chip_sum(p, r, place, name):
    _, rows, cols = p.shape
    tr = rows // 2

    def body(j_ref, p_ref, r_ref, o_ref):
        o_ref[...] = ((p_ref[...] + r_ref[0].astype(F32)) + r_ref[1].astype(F32)) + r_ref[2].astype(F32)

    return pl.pallas_call(
        body, name=name,
        grid_spec=pltpu.PrefetchScalarGridSpec(
            num_scalar_prefetch=1, grid=(2,),
            in_specs=[pl.BlockSpec((None, tr, cols), lambda t, j: (j[0], t, 0)),
                      pl.BlockSpec((3, tr, cols), lambda t, j: (0, t, 0))],
            out_specs=pl.BlockSpec((tr, cols), lambda t, j: (2 * j[1] + t, 0))),
        out_shape=_sds((2 * rows, cols)),
        compiler_params=pltpu.CompilerParams(dimension_semantics=("arbitrary",), vmem_limit_bytes=32 * MIB),
    )(place, p, r)


def _cast_permute_w_in(wt, place_chunks):
    def body(t_ref, w_ref, w0_ref, w1_ref):
        def cast_into(o_ref):
            for t in range(N_CHUNKS):
                src = pl.multiple_of(t_ref[1 + t] * CHUNK_ROWS, CHUNK_ROWS)
                o_ref[CHUNK_ROWS * t:CHUNK_ROWS * (t + 1), :] = w_ref[pl.ds(src, CHUNK_ROWS), :].astype(BF16)

        @pl.when(pl.program_id(0) == 0)
        def _():
            cast_into(w0_ref)

        @pl.when(pl.program_id(0) == 1)
        def _():
            cast_into(w1_ref)

    return pl.pallas_call(
        body, name="cast_permute_w_in",
        grid_spec=pltpu.PrefetchScalarGridSpec(
            num_scalar_prefetch=1, grid=(DEPTH,),
            in_specs=[pl.BlockSpec((None, W_IN_BLK, D_MODEL), lambda l, tbl: (l, 0, 0))],
            out_specs=[pl.BlockSpec((None, W_IN_BLK, D_MODEL), lambda l, tbl: (tbl[0], 0, 0)),
                       pl.BlockSpec((W_IN_BLK, D_MODEL), lambda l, tbl: (0, 0))]),
        out_shape=[_sds((N_CHIPS, W_IN_BLK, D_MODEL), BF16), _sds((W_IN_BLK, D_MODEL), BF16)],
        compiler_params=pltpu.CompilerParams(dimension_semantics=("arbitrary",), vmem_limit_bytes=32 * MIB),
    )(place_chunks, wt)


def _gather_inputs(c, w_out, w0):
    half = W_IN_BLK // 2

    def body(c_ref, wout_ref, mine_ref, call_ref, woutb_ref, w0_ref, send_sems, recv_sems):
        x, y, cc = _coords()
        j = 2 * x + y
        b = 2 * j + cc
        sib = (x, y, 1 - cc)
        woutb_ref[...] = wout_ref[...].astype(BF16)
        call_ref[b] = c_ref[...]
        chips = _other_chips(x, y)

        def sems(k):
            return send_sems.at[k], recv_sems.at[k]

        def half_rows(chip_index):
            return w0_ref.at[chip_index, pl.ds(cc * half, half), :]

        first = [_remote(mine_ref.at[j, pl.ds(cc * half, half), :], half_rows(j), sems(k), (*chip, cc))
                 for k, chip in enumerate(chips)]
        k = 3
        rest = []
        for fx in (0, 1):
            for fy in (0, 1):
                for fc in (0, 1):
                    if fx or fy or fc:
                        dev = (1 - x if fx else x, 1 - y if fy else y, 1 - cc if fc else cc)
                        rest.append(_remote(call_ref.at[b], call_ref.at[b], sems(k), dev))
                        k += 1
        for cp in first + rest:
            cp.start()
        passed = []
        for k, chip in enumerate(chips):
            jk = 2 * chip[0] + chip[1]
            first[k].wait_recv()
            passed.append(_remote(half_rows(jk), half_rows(jk), sems(10 + k), sib))
            passed[k].start()
        for cp in first:
            cp.wait_send()
        for cp in rest + passed:
            cp.wait()

    return pl.pallas_call(
        body, name="gather_inputs", in_specs=[VMEM, VMEM, ANY], out_specs=[VMEM, VMEM, ANY],
        out_shape=[_sds((N_DEV, 1, D_MODEL)), _sds((DEPTH, W_OUT_BLK, D_MODEL), BF16),
                   _sds((N_CHIPS, W_IN_BLK, D_MODEL), BF16)],
        scratch_shapes=[pltpu.SemaphoreType.DMA((13,)), pltpu.SemaphoreType.DMA((13,))],
        input_output_aliases={2: 2},
        compiler_params=pltpu.CompilerParams(vmem_limit_bytes=32 * MIB),
    )(c, w_out, w0)


def _ada_rows(c_all, w_ada, b_blk):
    def body(c_ref, w_ref, b_ref, o_ref, cond_ref):
        cv = c_ref[...]
        cond = (cv * _sigmoid(cv)).astype(BF16)
        cond_ref[...] = cond.astype(F32)
        for l in range(DEPTH):
            o_ref[:, l, :] = jnp.dot(cond, w_ref[l].astype(BF16), preferred_element_type=F32) + b_ref[l:l + 1, :]

    return pl.pallas_call(
        body, name="ada_rows", in_specs=[VMEM, VMEM, VMEM], out_specs=[VMEM, VMEM],
        out_shape=[_sds((N_DEV, DEPTH, W_ADA_BLK)), _sds((N_DEV, D_MODEL))],
        compiler_params=pltpu.CompilerParams(vmem_limit_bytes=32 * MIB),
    )(c_all, w_ada, b_blk)


def _exchange_ada(part):
    def body(part_ref, out_ref, send_sems, recv_sems):
        x, y, cc = _coords()
        j = 2 * x + y
        out_ref[j] = part_ref[2 * j + cc]
        copies = []
        for k, chip in enumerate(_other_chips(x, y)):
            b_dst = 4 * chip[0] + 2 * chip[1] + cc
            copies.append(_remote(part_ref.at[b_dst], out_ref.at[j], (send_sems.at[k], recv_sems.at[k]), (*chip, cc)))
        for cp in copies:
            cp.start()
        for cp in copies:
            cp.wait()

    return pl.pallas_call(
        body, name="exchange_ada", in_specs=[VMEM], out_specs=VMEM,
        out_shape=_sds((N_CHIPS, DEPTH, W_ADA_BLK)),
        scratch_shapes=[pltpu.SemaphoreType.DMA((3,)), pltpu.SemaphoreType.DMA((3,))],
    )(part)


def _all_gather_rows(blk):
    m_per, n = blk.shape

    def body(x_ref, out_ref, send_sems, recv_sems, local_sem):
        x, y, c = _coords()
        me, sibling = (x, y, c), (x, y, 1 - c)
        chips = _other_chips(x, y)

        def rows(px, py, pc):
            return out_ref.at[pl.ds((4 * px + 2 * py + pc) * m_per, m_per), :]

        def copy(k, block, to, src=None):
            return _remote(rows(*block) if src is None else src, rows(*block), (send_sems.at[k], recv_sems.at[k]), to)

        mine = pltpu.make_async_copy(x_ref, rows(*me), local_sem)
        mine.start()
        first = [copy(0, me, sibling, src=x_ref)]
        first += [copy(1 + j, me, (*chip, c), src=x_ref) for j, chip in enumerate(chips)]
        for cp in first:
            cp.start()
        passed = [copy(4 + j, (*chip, c), sibling) for j, chip in enumerate(chips)]
        for j, chip in enumerate(chips):
            copy(1 + j, (*chip, c), me).wait_recv()
            passed[j].start()
        copy(0, sibling, me).wait_recv()
        for j, chip in enumerate(chips):
            copy(4 + j, (*chip, 1 - c), me).wait_recv()
        for cp in first + passed:
            cp.wait_send()
        mine.wait()

    return pl.pallas_call(
        body, name="all_gather_small", in_specs=[VMEM], out_specs=VMEM,
        out_shape=_sds((N_DEV * m_per, n), blk.dtype),
        scratch_shapes=[pltpu.SemaphoreType.DMA((7,)), pltpu.SemaphoreType.DMA((7,)), pltpu.SemaphoreType.DMA],
        compiler_params=pltpu.CompilerParams(vmem_limit_bytes=32 * MIB),
    )(blk)


def _adamw_math(w, g, m, v):
    m = ADAM_B1 * m + (1.0 - ADAM_B1) * g
    v = ADAM_B2 * v + (1.0 - ADAM_B2) * (g * g)
    m_hat = m / (1.0 - ADAM_B1 ** ADAM_STEP)
    v_hat = v / (1.0 - ADAM_B2 ** ADAM_STEP)
    delta = -ADAM_LR * (m_hat / (jnp.sqrt(v_hat) + ADAM_EPS) + ADAM_WD * w)
    return delta, m, v


def _adamw_w_in(w, g0, g1, m, v, pos_chunks):
    def body(t_ref, w_ref, g0_ref, g1_ref, m_ref, v_ref, g_ref, d_ref, nm_ref, nv_ref):
        for l, src in enumerate((g0_ref, g1_ref)):
            g = src[...]
            g_ref[l] = g
            d_ref[l], nm_ref[l], nv_ref[l] = _adamw_math(w_ref[l], g, m_ref[l], v_ref[l])

    nat = pl.BlockSpec((DEPTH, CHUNK_ROWS, D_MODEL), lambda t, tbl: (0, t, 0))
    per = pl.BlockSpec((CHUNK_ROWS, D_MODEL), lambda t, tbl: (tbl[t], 0))
    return pl.pallas_call(
        body, name="adamw_w_in",
        grid_spec=pltpu.PrefetchScalarGridSpec(num_scalar_prefetch=1, grid=(N_CHUNKS,),
                                               in_specs=[nat, per, per, nat, nat], out_specs=[nat] * 4),
        out_shape=[_sds(w.shape)] * 4,
        compiler_params=pltpu.CompilerParams(dimension_semantics=("arbitrary",)),
    )(pos_chunks, w, g0, g1, m, v)


def _adamw_w_out(w, g0, g1, m, v):
    def body(w_ref, g0_ref, g1_ref, m_ref, v_ref, g_ref, d_ref, nm_ref, nv_ref):
        g = jnp.where(pl.program_id(0) == 0, g0_ref[...], g1_ref[...])
        g_ref[...] = g
        d_ref[...], nm_ref[...], nv_ref[...] = _adamw_math(w_ref[...], g, m_ref[...], v_ref[...])

    blk = pl.BlockSpec((None, W_OUT_BLK, D_MODEL), lambda l: (l, 0, 0))
    gblk = _full((W_OUT_BLK, D_MODEL))
    return pl.pallas_call(
        body, name="adamw_w_out", grid=(DEPTH,), in_specs=[blk, gblk, gblk, blk, blk], out_specs=[blk] * 4,
        out_shape=[_sds(w.shape)] * 4,
        compiler_params=pltpu.CompilerParams(dimension_semantics=("arbitrary",), vmem_limit_bytes=32 * MIB),
    )(w, g0, g1, m, v)


def _w_ada_grad_adamw(cond_t, dada, w, m, v):
    _, rows, cols = w.shape
    tr = 256

    def body(ct_ref, da_ref, w_ref, m_ref, v_ref, g_ref, d_ref, nm_ref, nv_ref):
        g = jnp.dot(ct_ref[...], da_ref[...].astype(BF16), preferred_element_type=F32)
        g_ref[...] = g
        d_ref[...], nm_ref[...], nv_ref[...] = _adamw_math(w_ref[...], g, m_ref[...], v_ref[...])

    blk = pl.BlockSpec((None, tr, cols), lambda l, t: (l, t, 0))
    return pl.pallas_call(
        body, name="w_ada_grad_adamw", grid=(DEPTH, rows // tr),
        in_specs=[pl.BlockSpec((tr, BLOCK), lambda l, t: (t, 0)), pl.BlockSpec((None, BLOCK, cols), lambda l, t: (l, 0, 0)),
                  blk, blk, blk],
        out_specs=[blk] * 4, out_shape=[_sds(w.shape)] * 4,
        compiler_params=pltpu.CompilerParams(dimension_semantics=("arbitrary", "arbitrary"), vmem_limit_bytes=32 * MIB),
    )(cond_t, dada, w, m, v)


def _small_sum_adamw(gathered_a, gathered_b, w, m, v):
    def body(a_ref, b_ref, w_ref, m_ref, v_ref, g_ref, d_ref, nm_ref, nv_ref):
        def total(ref):
            g = ref[0]
            for b in range(1, N_DEV):
                g = g + ref[b]
            return g

        g = jnp.concatenate([total(a_ref), total(b_ref)], axis=0)
        g_ref[...] = g
        d_ref[...], nm_ref[...], nv_ref[...] = _adamw_math(w_ref[...], g, m_ref[...], v_ref[...])

    return pl.pallas_call(
        body, name="small_sum_adamw", in_specs=[VMEM] * 5, out_specs=[VMEM] * 4, out_shape=[_sds(w.shape)] * 4,
        compiler_params=pltpu.CompilerParams(vmem_limit_bytes=48 * MIB),
    )(gathered_a, gathered_b, w, m, v)


_SMALL_A = (("w_s", DEPTH * 8 * BLOCK), ("b_s", DEPTH * 8), ("q_gain", 1), ("k_gain", 1), ("sink", 1))
_SMALL_B = (("b_ada", DEPTH * 24), ("norm_gain", DEPTH * 8))


def _pack_rows(parts, layout):
    rows = []
    for name, n in layout:
        flat = parts[name].reshape(-1)
        rows.append(jnp.pad(flat, (0, n * 128 - flat.shape[0])).reshape(n, 128))
    n_rows = sum(n for _, n in layout)
    if n_rows % 8:
        rows.append(jnp.zeros((-n_rows % 8, 128), F32))
    return jnp.concatenate(rows, axis=0)


def _pack_small(parts):
    return jnp.concatenate([_pack_rows(parts, _SMALL_A), _pack_rows(parts, _SMALL_B)], axis=0)


def _unpack_small(packed, shapes):
    out, r0 = {}, 0
    for layout in (_SMALL_A, _SMALL_B):
        for name, n in layout:
            size = 1
            for d in shapes[name]:
                size *= d
            out[name] = packed[r0:r0 + n].reshape(-1)[:size].reshape(shapes[name])
            r0 += n
        r0 += -r0 % 8
    return out


def _permute_heads(a, axis):
    shp = a.shape
    a = a.reshape(shp[:axis] + (2, 4, HEAD_DIM) + shp[axis + 1:])
    a = jnp.swapaxes(a, axis, axis + 1)
    return a.reshape(shp)


def _unpermute_heads(a, axis):
    shp = a.shape
    a = a.reshape(shp[:axis] + (4, 2, HEAD_DIM) + shp[axis + 1:])
    a = jnp.swapaxes(a, axis, axis + 1)
    return a.reshape(shp)


def _permute_w_out(w):
    return jnp.concatenate([_permute_heads(w[:D_ATTN], 0), w[D_ATTN:]], axis=0)


def _unpermute_w_out(w):
    return jnp.concatenate([_unpermute_heads(w[:D_ATTN], 0), w[D_ATTN:]], axis=0)


def kernel(x, c, w_ada, b_ada, norm_gain, w_in, q_gain, k_gain, sink, w_s, b_s, w_out, loss_target, m_w_ada, m_b_ada, m_norm_gain, m_w_in, m_q_gain, m_k_gain, m_sink, m_w_s, m_b_s, m_w_out, v_w_ada, v_b_ada, v_norm_gain, v_w_in, v_q_gain, v_k_gain, v_sink, v_w_s, v_b_s, v_w_out):
    ix, iy, ic = _coords()
    chip = 2 * ix + iy
    chip_idx = jnp.stack([chip, ic]).astype(jnp.int32)
    core_idx = jnp.reshape(ic, (1,)).astype(jnp.int32)
    src_chunks = lax.dynamic_index_in_dim(jnp.asarray(_CHUNK_SRC), chip, 0, keepdims=False)
    pos_chunks = lax.dynamic_index_in_dim(jnp.asarray(_CHUNK_POS), chip, 0, keepdims=False)
    x0, target = x[0], loss_target[0]

    wt, mt, vt = (jnp.swapaxes(a, 1, 2) for a in (w_in, m_w_in, v_w_in))
    w0_mine, wloc_in1 = _cast_permute_w_in(wt, jnp.concatenate([chip_idx[:1], src_chunks]))
    c_all, wloc_out, w0 = _gather_inputs(c, w_out, w0_mine)
    wts = [w0.reshape(D_IN, D_MODEL), None]

    b_blk = lax.dynamic_slice_in_dim(b_ada, chip * W_ADA_BLK, W_ADA_BLK, axis=1)
    ada_part, cond = _ada_rows(c_all.reshape(N_DEV, D_MODEL), w_ada, b_blk)
    ada = jnp.moveaxis(_exchange_ada(ada_part), 0, 1).reshape(DEPTH, 3 * D_MODEL)
    shift = [ada[l:l + 1, 0:D_MODEL] for l in range(DEPTH)]
    scale = [ada[l:l + 1, D_MODEL:2 * D_MODEL] for l in range(DEPTH)]
    gate = [ada[l:l + 1, 2 * D_MODEL:] for l in range(DEPTH)]
    ng = [norm_gain[l:l + 1] for l in range(DEPTH)]

    qg2 = jnp.concatenate([q_gain, q_gain], axis=-1)
    kg2 = jnp.concatenate([k_gain, k_gain], axis=-1)
    ws_b = w_s.astype(BF16)
    wst_b = jnp.swapaxes(w_s, -1, -2).astype(BF16)
    bsp = jnp.repeat(jnp.swapaxes(b_s.reshape(DEPTH, 4, 2, BLOCK), -1, -2), HEAD_DIM, axis=-1)
    bias = jnp.asarray(_bias_table())

    def mix_args(l):
        return bias, sink[l], qg2[l:l + 1], kg2[l:l + 1], ws_b[l]

    w_out_shape = (W_OUT_BLK, D_MODEL)
    proj0, w1 = _proj_fwd(x0, ng[0], scale[0], shift[0], wts[0], "proj_fwd_0",
                          jobs=[_job_gather([(wloc_in1, None)], [(W_IN_BLK, D_MODEL)])])
    y0, wo0, wo1 = _mix_fwd(proj0, *mix_args(0), bsp[0], "mix_fwd_0",
                            jobs=[_job_gather([(wloc_out, 0), (wloc_out, 1)], [w_out_shape, w_out_shape])])
    wts[1] = w1.reshape(D_IN, D_MODEL)
    wos = [_permute_w_out(w.reshape(D_MODEL, D_MODEL)) for w in (wo0, wo1)]
    x1, = _out_fwd(y0, x0, gate[0], wos[0], "out_fwd_0")
    proj1, = _proj_fwd(x1, ng[1], scale[1], shift[1], wts[1], "proj_fwd_1")
    y1, = _mix_fwd(proj1, *mix_args(1), bsp[1], "mix_fwd_1")
    dx2, sq = _out_fwd_loss(y1, x1, gate[1], wos[1], target, "out_fwd_loss_1")
    loss = lax.psum(0.5 * sq[0, 0], ("x", "y", "c"))

    def blocks_out(gw):
        return _unpermute_w_out(gw).reshape(N_CHIPS, W_OUT_BLK, D_MODEL)

    dy1, gwo1, dgate1 = _out_bwd(dx2, y1, gate[1], wos[1], "out_bwd_1")
    go1 = blocks_out(gwo1)
    dpb, dkv, p0, p2, dqg1, dkg1, dsink1, dws1, dbsp1, ro1 = _mix_bwd(
        dy1, proj1, *mix_args(1), wst_b[1], bsp[1], "mix_bwd_1", jobs=[_job_swap(go1)])
    po1, = _pair_sum(go1, ro1, core_idx, "pair_sum_w_out_1")
    gwi1, dkvb, co1 = _w_in_grad(dpb, dkv, p0, p2, x1, ng[1], scale[1], shift[1], "w_in_grad_1", jobs=[_job_scatter(po1)])
    gi1 = gwi1.reshape(N_CHIPS, W_IN_BLK, D_MODEL)
    fo1 = _chip_sum(po1, co1, chip_idx, "chip_sum_w_out_1")
    dx1, dsh1, dsc1, dng1, grad_wo1, ri1 = _proj_bwd(dpb, dkvb, x1, dx2, ng[1], scale[1], wts[1], "proj_bwd_1",
                                                     jobs=[_job_join(fo1), _job_swap(gi1)])
    pi1, = _pair_sum(gi1, ri1, core_idx, "pair_sum_w_in_1")

    dy0, gwo0, dgate0 = _out_bwd(dx1, y0, gate[0], wos[0], "out_bwd_0")
    go0 = blocks_out(gwo0)
    dpb, dkv, p0, p2, dqg0, dkg0, dsink0, dws0, dbsp0, ci1, ro0 = _mix_bwd(
        dy0, proj0, *mix_args(0), wst_b[0], bsp[0], "mix_bwd_0", jobs=[_job_scatter(pi1), _job_swap(go0)])
    fi1 = _chip_sum(pi1, ci1, chip_idx, "chip_sum_w_in_1")
    po0, = _pair_sum(go0, ro0, core_idx, "pair_sum_w_out_0")

    def bs_grad(dbsp):
        return jnp.swapaxes(dbsp[:, :, ::HEAD_DIM], -1, -2).reshape(8, BLOCK)

    small_g = dict(
        w_s=jnp.stack([dws0, dws1]), b_s=jnp.stack([bs_grad(dbsp0), bs_grad(dbsp1)]),
        q_gain=jnp.stack([dqg0[0, :HEAD_DIM], dqg1[0, :HEAD_DIM]]), k_gain=jnp.stack([dkg0[0, :HEAD_DIM], dkg1[0, :HEAD_DIM]]),
        sink=jnp.stack([dsink0[0, :N_HEADS], dsink1[0, :N_HEADS]]))
    gwi0, dkvb, grad_wi1, co0, gathered_a = _w_in_grad(
        dpb, dkv, p0, p2, x0, ng[0], scale[0], shift[0], "w_in_grad_0",
        jobs=[_job_join(fi1), _job_scatter(po0), _job_all_gather(_pack_rows(small_g, _SMALL_A))])
    gi0 = gwi0.reshape(N_CHIPS, W_IN_BLK, D_MODEL)
    fo0 = _chip_sum(po0, co0, chip_idx, "chip_sum_w_out_0")

    ri0, grad_wo0 = _comm([_job_swap(gi0), _job_join(fo0)], "swap_w_in_0")
    pi0, pi0_send = _pair_sum(gi0, ri0, core_idx, "pair_sum_w_in_0", send_dtype=BF16)
    dx0, dsh0, dsc0, dng0, ci0 = _proj_bwd(dpb, dkvb, x0, dx1, ng[0], scale[0], wts[0], "proj_bwd_0",
                                           jobs=[_job_scatter(pi0_send)])
    fi0 = _chip_sum(pi0, ci0, chip_idx, "chip_sum_w_in_0")
    grad_wi0, = _comm([_job_join(fi0)], "join_w_in_0")

    small_g.update(
        b_ada=jnp.stack([jnp.concatenate([dsh0, dsc0, dgate0], axis=-1)[0], jnp.concatenate([dsh1, dsc1, dgate1], axis=-1)[0]]),
        norm_gain=jnp.stack([dng0[0], dng1[0]]))
    small_w = dict(w_s=w_s, b_s=b_s, b_ada=b_ada, norm_gain=norm_gain, q_gain=q_gain, k_gain=k_gain, sink=sink)
    small_m = dict(w_s=m_w_s, b_s=m_b_s, b_ada=m_b_ada, norm_gain=m_norm_gain, q_gain=m_q_gain, k_gain=m_k_gain, sink=m_sink)
    small_v = dict(w_s=v_w_s, b_s=v_b_s, b_ada=v_b_ada, norm_gain=v_norm_gain, q_gain=v_q_gain, k_gain=v_k_gain, sink=v_sink)
    gathered_b = _all_gather_rows(_pack_rows(small_g, _SMALL_B))
    packed = _small_sum_adamw(gathered_a.reshape(N_DEV, -1, 128), gathered_b.reshape(N_DEV, -1, 128),
                              _pack_small(small_w), _pack_small(small_m), _pack_small(small_v))
    shapes = {k: a.shape for k, a in small_w.items()}
    sg, sd, sm, sv = (_unpack_small(p, shapes) for p in packed)

    dada_all = gathered_b.reshape(N_DEV, -1, 128)[:, 0:DEPTH * 24].reshape(N_DEV, DEPTH, 3 * D_MODEL)
    dada_blk = jnp.moveaxis(lax.dynamic_slice_in_dim(dada_all, chip * W_ADA_BLK, W_ADA_BLK, axis=2), 0, 1)
    pad = BLOCK - N_DEV
    ada_out = _w_ada_grad_adamw(
        jnp.pad(cond.T, ((0, 0), (0, pad))).astype(BF16), jnp.pad(dada_blk, ((0, 0), (0, pad), (0, 0))),
        w_ada, m_w_ada, v_w_ada)

    in_out = [jnp.swapaxes(a, 1, 2) for a in _adamw_w_in(wt, grad_wi0, grad_wi1, mt, vt, pos_chunks)]
    out_out = _adamw_w_out(w_out, grad_wo0, grad_wo1, m_w_out, v_w_out)

    def ordered(k):
        small = (sg, sd, sm, sv)[k]
        return (ada_out[k], small["b_ada"], small["norm_gain"], in_out[k], small["q_gain"], small["k_gain"], small["sink"],
                small["w_s"], small["b_s"], out_out[k])

    return (loss, dx0[None], *ordered(0), *ordered(1), *ordered(2), *ordered(3))
```

```python
import numpy as np

import jax
import jax.numpy as jnp
from jax import lax
from jax.experimental import pallas as pl
from jax.experimental.pallas import tpu as pltpu

F32 = jnp.float32
BF16 = jnp.bfloat16

D_MODEL = 1024
DEPTH = 2
HEAD_DIM = 64
N_HEADS = 8
BLOCK = 128
SUB = 4
TILE = SUB * BLOCK
D_ATTN = 512
D_KV = 128
D_IN = 2816
N_CHIPS = 4
N_DEV = 8
W_IN_BLK = D_IN // N_CHIPS
W_OUT_BLK = D_MODEL // N_CHIPS
W_ADA_BLK = 3 * D_MODEL // N_CHIPS
CHUNK_ROWS = HEAD_DIM
N_CHUNKS = W_IN_BLK // CHUNK_ROWS
EPS = 1e-6
NEG_INF = -1e30

C_Q, C_K, C_V, C_GA, C_U, C_VG, C_GG = 0, 512, 640, 768, 1280, 1792, 2304

ADAM_LR = 0.001
ADAM_B1 = 0.9
ADAM_B2 = 0.999
ADAM_EPS = 1e-08
ADAM_WD = 0.01
ADAM_STEP = 10

MESH = pl.DeviceIdType.MESH
MIB = 1024 * 1024
ANY = pl.BlockSpec(memory_space=pl.ANY)
VMEM = pl.BlockSpec(memory_space=pltpu.VMEM)

NT_DIMS = (((1,), (1,)), ((), ()))
TN_DIMS = (((0,), (0,)), ((), ()))

_PAIR_ORDER = (0, 4, 1, 5, 2, 6, 3, 7)
_CHUNK_SRC = np.array([
    list(_PAIR_ORDER) + [8, 9, 10],
    [0] + [1 + h for h in _PAIR_ORDER] + [9, 10],
    list(range(N_CHUNKS)),
    list(range(N_CHUNKS)),
], np.int32)
_CHUNK_POS = np.argsort(_CHUNK_SRC, axis=1).astype(np.int32)


def _bias_table():
    i = np.arange(N_HEADS * BLOCK)[:, None]
    j = np.arange(3 * BLOCK)[None, :]
    dist = np.abs(j - BLOCK - (i % BLOCK))
    slope = 2.0 ** -(i // BLOCK + 1.0)
    inner = np.where(dist <= BLOCK, -(slope * dist), NEG_INF)
    first = np.where(j >= BLOCK, inner, NEG_INF)
    last = np.where(j < 2 * BLOCK, inner, NEG_INF)
    return np.stack([first, inner, last]).astype(np.float32)


def _full(shape):
    n = len(shape)
    return pl.BlockSpec(shape, lambda *_: (0,) * n)


def _sds(shape, dtype=F32):
    return jax.ShapeDtypeStruct(shape, dtype)


def _coords():
    return lax.axis_index("x"), lax.axis_index("y"), lax.axis_index("c")


def _other_chips(x, y):
    return [(1 - x, y), (x, 1 - y), (1 - x, 1 - y)]


def _remote(src, dst, sems, dev):
    return pltpu.make_async_remote_copy(src_ref=src, dst_ref=dst, send_sem=sems[0], recv_sem=sems[1],
                                        device_id=dev, device_id_type=MESH)


class _Job:
    def __init__(self, inputs, out_shapes, n_remote, n_local, make, then=None, in_place=False):
        self.inputs, self.out_shapes, self.n_remote, self.n_local, self.make = inputs, out_shapes, n_remote, n_local, make
        self.then = then
        self.in_place = in_place


def _job_aliases(jobs, in_base, out_base):
    aliases, a, b = {}, 0, 0
    for j in jobs:
        if j.in_place:
            aliases.update({in_base + a + k: out_base + b + k for k in range(len(j.inputs))})
        a, b = a + len(j.inputs), b + len(j.out_shapes)
    return aliases


def _job_copies(jobs, jin, jout, sems, second=False):
    send, recv, loc = sems
    res, a, b, r, l = [], 0, 0, 0, 0
    for j in jobs:
        build = j.then if second else j.make
        if build is not None:
            res += build(jin[a:a + len(j.inputs)], jout[b:b + len(j.out_shapes)],
                         lambda k, r=r: (send.at[r + k], recv.at[r + k]), lambda k, l=l: loc.at[l + k])
        a, b, r, l = a + len(j.inputs), b + len(j.out_shapes), r + j.n_remote, l + j.n_local
    return res


def _run(copies):
    for cp in copies:
        cp.start()
    for cp in copies:
        cp.wait()


def _job_gather(sources, shapes):
    n = len(sources)

    def make(ins, outs, rsem, lsem):
        x, y, c = _coords()
        j = 2 * x + y
        res = []
        for t, ((_, layer), src, dst) in enumerate(zip(sources, ins, outs)):
            src = src if layer is None else src.at[layer]
            res.append(pltpu.make_async_copy(src, dst.at[j], lsem(t)))
            for k, chip in enumerate(_other_chips(x, y)):
                res.append(_remote(src, dst.at[j], rsem(3 * t + k), (*chip, c)))
        return res

    return _Job([a for a, _ in sources], [_sds((N_CHIPS,) + s, BF16) for s in shapes], 3 * n, n, make)


def _job_swap(g):
    _, rows, cols = g.shape
    half = rows // 2

    def make(ins, outs, rsem, lsem):
        x, y, c = _coords()
        return [_remote(ins[0].at[:, pl.ds((1 - c) * half, half), :], outs[0], rsem(0), (x, y, 1 - c))]

    return _Job([g], [_sds((N_CHIPS, half, cols))], 1, 0, make)


def _job_scatter(p):
    def make(ins, outs, rsem, lsem):
        x, y, c = _coords()
        return [_remote(ins[0].at[2 * chip[0] + chip[1]], outs[0].at[k], rsem(k), (*chip, c))
                for k, chip in enumerate(_other_chips(x, y))]

    return _Job([p], [_sds((3,) + p.shape[1:], p.dtype)], 3, 0, make)


def _job_all_gather(blk):
    m_per = blk.shape[0]

    def rows(ref, px, py, pc):
        return ref.at[pl.ds((4 * px + 2 * py + pc) * m_per, m_per), :]

    def make(ins, outs, rsem, lsem):
        x, y, c = _coords()
        res = [pltpu.make_async_copy(ins[0], rows(outs[0], x, y, c), lsem(0)),
               _remote(ins[0], rows(outs[0], x, y, c), rsem(0), (x, y, 1 - c))]
        res += [_remote(ins[0], rows(outs[0], x, y, c), rsem(1 + k), (*chip, c)) for k, chip in enumerate(_other_chips(x, y))]
        return res

    def then(ins, outs, rsem, lsem):
        x, y, c = _coords()
        return [_remote(rows(outs[0], *chip, c), rows(outs[0], *chip, c), rsem(4 + k), (x, y, 1 - c))
                for k, chip in enumerate(_other_chips(x, y))]

    return _Job([blk], [_sds((N_DEV * m_per, blk.shape[1]), blk.dtype)], 7, 1, make, then)


def _job_join(f):
    half = f.shape[0] // 2

    def make(ins, outs, rsem, lsem):
        x, y, c = _coords()
        mine = pl.ds(c * half, half)
        return [_remote(ins[0].at[mine, :], outs[0].at[mine, :], rsem(0), (x, y, 1 - c))]

    return _Job([f], [_sds(f.shape, f.dtype)], 1, 0, make, in_place=True)


def _pallas(body, *, name, grid, in_specs, out_specs, out_shape, operands, vmem_mib, jobs=()):
    in_specs, out_specs, out_shape = list(in_specs), list(out_specs), list(out_shape)
    n_in, n_out = len(in_specs), len(out_specs)
    j_in = [a for j in jobs for a in j.inputs]
    j_out = [s for j in jobs for s in j.out_shapes]
    n_rem = max(1, sum(j.n_remote for j in jobs))
    n_loc = max(1, sum(j.n_local for j in jobs))
    scratch = [pltpu.SemaphoreType.DMA((n_rem,)), pltpu.SemaphoreType.DMA((n_rem,)),
               pltpu.SemaphoreType.DMA((n_loc,))] if jobs else []

    def wrapped(*refs):
        ins = refs[:n_in]
        jin = refs[n_in:n_in + len(j_in)]
        outs = refs[n_in + len(j_in):n_in + len(j_in) + n_out]
        jout = refs[n_in + len(j_in) + n_out:n_in + len(j_in) + n_out + len(j_out)]

        if jobs:
            first = last = None
            for d, n in enumerate(grid):
                f, e = pl.program_id(d) == 0, pl.program_id(d) == n - 1
                first, last = (f, e) if first is None else (first & f, last & e)

            @pl.when(first)
            def _():
                for cp in _job_copies(jobs, jin, jout, refs[-3:]):
                    cp.start()

        body(*ins, *outs)

        if jobs:
            @pl.when(last)
            def _():
                for cp in _job_copies(jobs, jin, jout, refs[-3:]):
                    cp.wait()
                _run(_job_copies(jobs, jin, jout, refs[-3:], second=True))

    operands = [pltpu.with_memory_space_constraint(a, pltpu.HBM) if a.size * a.dtype.itemsize >= MIB else a
                for a in operands]
    return pl.pallas_call(
        wrapped, name=name, grid=grid,
        in_specs=in_specs + [ANY] * len(j_in), out_specs=out_specs + [ANY] * len(j_out),
        out_shape=out_shape + j_out, scratch_shapes=scratch, input_output_aliases=_job_aliases(jobs, n_in, n_out),
        compiler_params=pltpu.CompilerParams(dimension_semantics=("arbitrary",) * len(grid),
                                             vmem_limit_bytes=vmem_mib * MIB),
    )(*operands, *j_in)


def _comm(jobs, name):
    j_in = [a for j in jobs for a in j.inputs]
    j_out = [s for j in jobs for s in j.out_shapes]
    n_rem = max(1, sum(j.n_remote for j in jobs))
    n_loc = max(1, sum(j.n_local for j in jobs))

    def body(*refs):
        jin, jout = refs[:len(j_in)], refs[len(j_in):len(j_in) + len(j_out)]
        _run(_job_copies(jobs, jin, jout, refs[-3:]))
        _run(_job_copies(jobs, jin, jout, refs[-3:], second=True))

    return pl.pallas_call(
        body, name=name, in_specs=[ANY] * len(j_in), out_specs=[ANY] * len(j_out), out_shape=j_out,
        scratch_shapes=[pltpu.SemaphoreType.DMA((n_rem,)), pltpu.SemaphoreType.DMA((n_rem,)),
                        pltpu.SemaphoreType.DMA((n_loc,))],
        input_output_aliases=_job_aliases(jobs, 0, 0),
    )(*j_in)


def _sigmoid(x):
    return 1.0 / (1.0 + jnp.exp(-x))


def _lo_mask(shape):
    return lax.broadcasted_iota(jnp.int32, shape, len(shape) - 1) < HEAD_DIM


def _half_sum(x, lo):
    a = jnp.sum(jnp.where(lo, x, 0.0), axis=-1, keepdims=True)
    b = jnp.sum(jnp.where(lo, 0.0, x), axis=-1, keepdims=True)
    return jnp.where(lo, a, b)


def _half_rms_scale(x, lo):
    return lax.rsqrt(_half_sum(x * x, lo) * (1.0 / HEAD_DIM) + EPS)


def _stack_heads(pairs, lo):
    return jnp.concatenate([jnp.where(lo, t, 0.0) for t in pairs] + [jnp.where(lo, 0.0, t) for t in pairs], axis=0)


def _unstack_pair(stack, p, lo):
    return jnp.where(lo, stack[BLOCK * p:BLOCK * (p + 1)], stack[BLOCK * (4 + p):BLOCK * (5 + p)])


def _attention_probs(q_stack, kn, bias_ref, sink_ref):
    rows = N_HEADS * BLOCK
    s = lax.dot_general(q_stack, kn, NT_DIMS, preferred_element_type=F32) + bias_ref[...]
    sink = jnp.concatenate([jnp.full((BLOCK, BLOCK), sink_ref[h], F32) for h in range(N_HEADS)], axis=0)
    cols = [s[:, BLOCK * j:BLOCK * (j + 1)] for j in range(3)]
    top = jnp.max(jnp.maximum(jnp.maximum(cols[0], cols[1]), cols[2]), axis=-1, keepdims=True)
    m = jnp.maximum(jnp.broadcast_to(top, (rows, BLOCK)), sink)
    e = [jnp.exp(c - m) for c in cols]
    es = jnp.exp(sink - m)
    inv = 1.0 / (jnp.broadcast_to(jnp.sum((e[0] + e[1]) + e[2], axis=-1, keepdims=True), (rows, BLOCK)) + es)
    return jnp.concatenate([c * inv for c in e], axis=1), es * inv


def _kv_rows(cur_ref, pkv_ref, nkv_ref):
    k = jnp.concatenate([pkv_ref[:, 0:D_KV], cur_ref[:, C_K:C_K + D_KV], nkv_ref[:, 0:D_KV]], axis=0)
    v = jnp.concatenate([pkv_ref[:, D_KV:2 * D_KV], cur_ref[:, C_V:C_V + D_KV], nkv_ref[:, D_KV:2 * D_KV]], axis=0)
    return k, v


def _overlap_add(parts):
    blocks = []
    for j in range(SUB + 2):
        terms = [parts[b][BLOCK * (j - b):BLOCK * (j - b + 1)] for b in range(SUB) if 0 <= j - b <= 2]
        total = terms[0]
        for t in terms[1:]:
            total = total + t
        blocks.append(total)
    return jnp.concatenate(blocks, axis=0)


def _mix_specs(nt):
    cur = pl.BlockSpec((TILE, D_IN), lambda i: (i, 0))
    kv_col = C_K // (2 * D_KV)
    pkv = pl.BlockSpec((BLOCK, 2 * D_KV), lambda i: (jnp.maximum(i * SUB - 1, 0), kv_col))
    nkv = pl.BlockSpec((BLOCK, 2 * D_KV), lambda i: (jnp.minimum((i + 1) * SUB, nt * SUB - 1), kv_col))
    table = (None, N_HEADS * BLOCK, 3 * BLOCK)
    first = pl.BlockSpec(table, lambda i: (jnp.where(i == 0, 0, 1), 0, 0))
    inner = pl.BlockSpec(table, lambda i: (1, 0, 0))
    last = pl.BlockSpec(table, lambda i: (jnp.where(i == nt - 1, 2, 1), 0, 0))
    return cur, pkv, nkv, [first] + [inner] * (SUB - 2) + [last]


def _proj_fwd(x, ng, scale, shift, wt, name, jobs=()):
    s = x.shape[0]
    ts = min(512, s)

    def body(x_ref, ng_ref, sc_ref, sh_ref, w_ref, o_ref):
        xv = x_ref[...]
        r = lax.rsqrt(jnp.mean(xv * xv, axis=-1, keepdims=True) + EPS)
        h = ((xv * r) * ng_ref[...]) * (1.0 + sc_ref[...]) + sh_ref[...]
        o_ref[...] = lax.dot_general(h.astype(BF16), w_ref[...], NT_DIMS, preferred_element_type=F32)

    vec = _full((1, D_MODEL))
    return _pallas(
        body, name=name, grid=(s // ts,),
        in_specs=[pl.BlockSpec((ts, D_MODEL), lambda i: (i, 0)), vec, vec, vec, _full((D_IN, D_MODEL))],
        out_specs=[pl.BlockSpec((ts, D_IN), lambda i: (i, 0))], out_shape=[_sds((s, D_IN))],
        operands=(x, ng, scale, shift, wt), vmem_mib=48, jobs=jobs)


def _mix_fwd(proj, bias, sink, qg2, kg2, ws, bsp, name, jobs=()):
    s = proj.shape[0]
    nt = s // TILE

    def body(sink_ref, cur_ref, pkv_ref, nkv_ref, *rest):
        bias_refs = rest[:SUB]
        qg_ref, kg_ref, ws_ref, bsp_ref, y_ref = rest[SUB:]
        lo = _lo_mask((BLOCK, BLOCK))
        lo_kv = _lo_mask((TILE + 2 * BLOCK, BLOCK))
        k_all, v_all = _kv_rows(cur_ref, pkv_ref, nkv_ref)
        kn_all = ((k_all * _half_rms_scale(k_all, lo_kv)) * kg_ref[...]).astype(BF16)
        vb_all = v_all.astype(BF16)
        for b in range(SUB):
            rows = slice(BLOCK * b, BLOCK * (b + 1))
            window = slice(BLOCK * b, BLOCK * (b + 3))
            qn = []
            for p in range(4):
                q = cur_ref[rows, C_Q + BLOCK * p:C_Q + BLOCK * (p + 1)]
                qn.append(((q * _half_rms_scale(q, lo)) * qg_ref[...]) * 0.125)
            q_stack = _stack_heads(qn, lo).astype(BF16)
            prob, _ = _attention_probs(q_stack, kn_all[window], bias_refs[b], sink_ref)
            o_stack = jnp.dot(prob.astype(BF16), vb_all[window], preferred_element_type=F32)
            for p in range(4):
                g = cur_ref[rows, C_GA + BLOCK * p:C_GA + BLOCK * (p + 1)]
                y_ref[rows, BLOCK * p:BLOCK * (p + 1)] = (_unstack_pair(o_stack, p, lo) * (g * _sigmoid(g))).astype(BF16)
            for p in range(4):
                vg = cur_ref[rows, C_VG + BLOCK * p:C_VG + BLOCK * (p + 1)]
                vn = (vg * _half_rms_scale(vg, lo)).astype(BF16)
                sv = jnp.where(lo, jnp.dot(ws_ref[2 * p], vn, preferred_element_type=F32),
                               jnp.dot(ws_ref[2 * p + 1], vn, preferred_element_type=F32)) + bsp_ref[p]
                u = cur_ref[rows, C_U + BLOCK * p:C_U + BLOCK * (p + 1)]
                g = cur_ref[rows, C_GG + BLOCK * p:C_GG + BLOCK * (p + 1)]
                y_ref[rows, D_ATTN + BLOCK * p:D_ATTN + BLOCK * (p + 1)] = ((u * sv) * (g * _sigmoid(g))).astype(BF16)

    cur, pkv, nkv, bias_specs = _mix_specs(nt)
    return _pallas(
        body, name=name, grid=(nt,),
        in_specs=[pl.BlockSpec(memory_space=pltpu.SMEM), cur, pkv, nkv, *bias_specs, _full((1, BLOCK)), _full((1, BLOCK)),
                  _full((8, BLOCK, BLOCK)), _full((4, BLOCK, BLOCK))],
        out_specs=[pl.BlockSpec((TILE, D_MODEL), lambda i: (i, 0))], out_shape=[_sds((s, D_MODEL), BF16)],
        operands=(sink, proj, proj, proj, *([bias] * SUB), qg2, kg2, ws, bsp), vmem_mib=48, jobs=jobs)


def _out_fwd(y, x, gate, w_out, name):
    s = x.shape[0]
    ts = min(512, s)

    def body(y_ref, x_ref, g_ref, w_ref, o_ref):
        o_ref[...] = x_ref[...] + g_ref[...] * jnp.dot(y_ref[...], w_ref[...], preferred_element_type=F32)

    row = pl.BlockSpec((ts, D_MODEL), lambda i: (i, 0))
    return _pallas(
        body, name=name, grid=(s // ts,), in_specs=[row, row, _full((1, D_MODEL)), _full((D_MODEL, D_MODEL))],
        out_specs=[row], out_shape=[_sds((s, D_MODEL))], operands=(y, x, gate, w_out), vmem_mib=32)


def _out_fwd_loss(y, x, gate, w_out, target, name):
    s = x.shape[0]
    ts = min(512, s)

    def body(y_ref, x_ref, g_ref, w_ref, t_ref, dx_ref, sq_ref):
        @pl.when(pl.program_id(0) == 0)
        def _():
            sq_ref[...] = jnp.zeros_like(sq_ref)

        out = x_ref[...] + g_ref[...] * jnp.dot(y_ref[...], w_ref[...], preferred_element_type=F32)
        diff = out - t_ref[...]
        dx_ref[...] = diff * (1.0 / D_MODEL)
        per_token = jnp.sum(diff * diff, axis=-1, keepdims=True) * (1.0 / D_MODEL)
        sq_ref[...] += jnp.sum(per_token, axis=0, keepdims=True)

    row = pl.BlockSpec((ts, D_MODEL), lambda i: (i, 0))
    return _pallas(
        body, name=name, grid=(s // ts,), in_specs=[row, row, _full((1, D_MODEL)), _full((D_MODEL, D_MODEL)), row],
        out_specs=[row, _full((1, 1))], out_shape=[_sds((s, D_MODEL)), _sds((1, 1))],
        operands=(y, x, gate, w_out, target), vmem_mib=40)


def _out_bwd(dxo, y, gate, w_out, name, jobs=()):
    s = dxo.shape[0]
    ts = min(512, s)
    steps = s // ts

    def body(dx_ref, y_ref, g_ref, w_ref, dy_ref, gw_ref, dg_ref):
        @pl.when(pl.program_id(0) == 0)
        def _():
            gw_ref[...] = jnp.zeros_like(gw_ref)

        dx = dx_ref[...]
        dy_ref[...] = lax.dot_general((dx * g_ref[...]).astype(BF16), w_ref[...], NT_DIMS, preferred_element_type=F32)
        gw_ref[...] += lax.dot_general(y_ref[...], dx.astype(BF16), TN_DIMS, preferred_element_type=F32)

        @pl.when(pl.program_id(0) == steps - 1)
        def _():
            m = gw_ref[...]
            dg_ref[...] = jnp.sum(w_ref[...].astype(F32) * m, axis=0, keepdims=True)
            gw_ref[...] = m * g_ref[...]

    row = pl.BlockSpec((ts, D_MODEL), lambda i: (i, 0))
    return _pallas(
        body, name=name, grid=(s // ts,), in_specs=[row, row, _full((1, D_MODEL)), _full((D_MODEL, D_MODEL))],
        out_specs=[row, _full((D_MODEL, D_MODEL)), _full((1, D_MODEL))],
        out_shape=[_sds((s, D_MODEL)), _sds((D_MODEL, D_MODEL)), _sds((1, D_MODEL))],
        operands=(dxo, y, gate, w_out), vmem_mib=48, jobs=jobs)


def _mix_bwd(dy, proj, bias, sink, qg2, kg2, ws, wst, bsp, name, jobs=()):
    s = proj.shape[0]
    nt = s // TILE

    def body(sink_ref, dy_ref, cur_ref, pkv_ref, nkv_ref, *rest):
        bias_refs = rest[:SUB]
        (qg_ref, kg_ref, ws_ref, wst_ref, bsp_ref,
         dpb_ref, dkv_ref, p0_ref, p2_ref, dqg_ref, dkg_ref, dsink_ref, dws_ref, dbsp_ref) = rest[SUB:]

        def put(rows, col, value):
            dpb_ref[rows, col:col + BLOCK] = value.astype(BF16)

        @pl.when(pl.program_id(0) == 0)
        def _():
            dqg_ref[...] = jnp.zeros_like(dqg_ref)
            dkg_ref[...] = jnp.zeros_like(dkg_ref)
            dsink_ref[...] = jnp.zeros_like(dsink_ref)
            dws_ref[...] = jnp.zeros_like(dws_ref)
            dbsp_ref[...] = jnp.zeros_like(dbsp_ref)

        lo = _lo_mask((BLOCK, BLOCK))
        lo_kv = _lo_mask((TILE + 2 * BLOCK, BLOCK))
        lane_row = lax.broadcasted_iota(jnp.int32, (1, BLOCK), 1)
        qg = qg_ref[...]
        kg = kg_ref[...]

        k_all, v_all = _kv_rows(cur_ref, pkv_ref, nkv_ref)
        rk = _half_rms_scale(k_all, lo_kv)
        khat = k_all * rk
        kn_all = (khat * kg).astype(BF16)
        vb_all = v_all.astype(BF16)

        dkn_parts, dv_parts = [], []
        dsink = jnp.zeros((1, BLOCK), F32)
        dqg = jnp.zeros((1, BLOCK), F32)
        for b in range(SUB):
            rows = slice(BLOCK * b, BLOCK * (b + 1))
            window = slice(BLOCK * b, BLOCK * (b + 3))
            kn, vb = kn_all[window], vb_all[window]

            qhat, rq = [], []
            for p in range(4):
                q = cur_ref[rows, C_Q + BLOCK * p:C_Q + BLOCK * (p + 1)]
                r = _half_rms_scale(q, lo)
                rq.append(r)
                qhat.append(q * r)
            q_stack = _stack_heads([(qh * qg) * 0.125 for qh in qhat], lo).astype(BF16)
            prob, psink = _attention_probs(q_stack, kn, bias_refs[b], sink_ref)
            pb = prob.astype(BF16)
            o_stack = jnp.dot(pb, vb, preferred_element_type=F32)

            dout = []
            for p in range(4):
                g = cur_ref[rows, C_GA + BLOCK * p:C_GA + BLOCK * (p + 1)]
                sg = _sigmoid(g)
                dya = dy_ref[rows, BLOCK * p:BLOCK * (p + 1)]
                attn = _unstack_pair(o_stack, p, lo)
                put(rows, C_GA + BLOCK * p, dya * attn * (sg * (1.0 + g * (1.0 - sg))))
                dout.append(dya * (g * sg))
            do_stack = _stack_heads(dout, lo).astype(BF16)
            dp = lax.dot_general(do_stack, vb, NT_DIMS, preferred_element_type=F32)
            delta = jnp.sum(prob * dp, axis=-1, keepdims=True)
            dsb = (prob * (dp - delta)).astype(BF16)

            csink = -(psink * delta)
            for h in range(N_HEADS):
                tot = jnp.sum(csink[BLOCK * h:BLOCK * (h + 1)], axis=0, keepdims=True)
                dsink = dsink + jnp.where(lane_row == h, tot, 0.0)

            dq_stack = jnp.dot(dsb, kn, preferred_element_type=F32) * 0.125
            dkn_parts.append(lax.dot_general(dsb, q_stack, TN_DIMS, preferred_element_type=F32))
            dv_parts.append(lax.dot_general(pb, do_stack, TN_DIMS, preferred_element_type=F32))

            for p in range(4):
                dqn = _unstack_pair(dq_stack, p, lo)
                qh = qhat[p]
                dqg = dqg + jnp.sum(dqn * qh, axis=0, keepdims=True)
                dqh = dqn * qg
                mean = _half_sum(dqh * qh, lo) * (1.0 / HEAD_DIM)
                put(rows, C_Q + BLOCK * p, rq[p] * (dqh - qh * mean))

            for p in range(4):
                vg = cur_ref[rows, C_VG + BLOCK * p:C_VG + BLOCK * (p + 1)]
                r = _half_rms_scale(vg, lo)
                vnf = vg * r
                vn = vnf.astype(BF16)
                sv = jnp.where(lo, jnp.dot(ws_ref[2 * p], vn, preferred_element_type=F32),
                               jnp.dot(ws_ref[2 * p + 1], vn, preferred_element_type=F32)) + bsp_ref[p]
                u = cur_ref[rows, C_U + BLOCK * p:C_U + BLOCK * (p + 1)]
                g = cur_ref[rows, C_GG + BLOCK * p:C_GG + BLOCK * (p + 1)]
                sg = _sigmoid(g)
                dym = dy_ref[rows, D_ATTN + BLOCK * p:D_ATTN + BLOCK * (p + 1)]
                put(rows, C_GG + BLOCK * p, dym * (u * sv) * (sg * (1.0 + g * (1.0 - sg))))
                dgm = dym * (g * sg)
                put(rows, C_U + BLOCK * p, dgm * sv)
                dsv = dgm * u
                dsv_a = jnp.where(lo, dsv, 0.0)
                dsv_b = jnp.where(lo, 0.0, dsv)
                dws_ref[2 * p] += lax.dot_general(dsv_a.astype(BF16), vn, NT_DIMS, preferred_element_type=F32)
                dws_ref[2 * p + 1] += lax.dot_general(dsv_b.astype(BF16), vn, NT_DIMS, preferred_element_type=F32)
                dbsp_ref[p] += jnp.where(lo, jnp.sum(dsv_a, axis=-1, keepdims=True), jnp.sum(dsv_b, axis=-1, keepdims=True))
                dsvb = dsv.astype(BF16)
                dvn = jnp.where(lo, jnp.dot(wst_ref[2 * p], dsvb, preferred_element_type=F32),
                                jnp.dot(wst_ref[2 * p + 1], dsvb, preferred_element_type=F32))
                mean = _half_sum(dvn * vnf, lo) * (1.0 / HEAD_DIM)
                put(rows, C_VG + BLOCK * p, r * (dvn - vnf * mean))

        dsink_ref[...] += dsink
        dqg = jnp.broadcast_to(dqg, (8, BLOCK))
        dqg_ref[...] += dqg + pltpu.roll(dqg, HEAD_DIM, 1)

        dkn = _overlap_add(dkn_parts)
        dv = _overlap_add(dv_parts)
        dkg = jnp.broadcast_to(jnp.sum(dkn * khat, axis=0, keepdims=True), (8, BLOCK))
        dkg_ref[...] += dkg + pltpu.roll(dkg, HEAD_DIM, 1)
        dkh = dkn * kg
        dk = rk * (dkh - khat * (_half_sum(dkh * khat, lo_kv) * (1.0 / HEAD_DIM)))
        dpb_ref[:, C_K:C_GA] = jnp.zeros((TILE, 2 * D_KV), BF16)
        dkv_ref[:, 0:D_KV] = dk[BLOCK:BLOCK + TILE]
        dkv_ref[:, D_KV:2 * D_KV] = dv[BLOCK:BLOCK + TILE]
        p0_ref[:, 0:D_KV] = dk[0:BLOCK]
        p0_ref[:, D_KV:2 * D_KV] = dv[0:BLOCK]
        p2_ref[:, 0:D_KV] = dk[BLOCK + TILE:]
        p2_ref[:, D_KV:2 * D_KV] = dv[BLOCK + TILE:]

    cur, pkv, nkv, bias_specs = _mix_specs(nt)
    kv_blk = (BLOCK, 2 * D_KV)
    return _pallas(
        body, name=name, grid=(nt,),
        in_specs=[pl.BlockSpec(memory_space=pltpu.SMEM), pl.BlockSpec((TILE, D_MODEL), lambda i: (i, 0)),
                  cur, pkv, nkv, *bias_specs, _full((1, BLOCK)), _full((1, BLOCK)),
                  _full((8, BLOCK, BLOCK)), _full((8, BLOCK, BLOCK)), _full((4, BLOCK, BLOCK))],
        out_specs=[cur, pl.BlockSpec((TILE, 2 * D_KV), lambda i: (i, 0)),
                   pl.BlockSpec(kv_blk, lambda i: ((i + nt - 1) % nt, 0)),
                   pl.BlockSpec(kv_blk, lambda i: ((i + 1) % nt, 0)),
                   _full((8, BLOCK)), _full((8, BLOCK)), _full((1, BLOCK)),
                   _full((8, BLOCK, BLOCK)), _full((4, BLOCK, BLOCK))],
        out_shape=[_sds((s, D_IN), BF16), _sds((s, 2 * D_KV)), _sds((nt * BLOCK, 2 * D_KV)), _sds((nt * BLOCK, 2 * D_KV)),
                   _sds((8, BLOCK)), _sds((8, BLOCK)), _sds((1, BLOCK)),
                   _sds((8, BLOCK, BLOCK)), _sds((4, BLOCK, BLOCK))],
        operands=(sink, dy, proj, proj, proj, *([bias] * SUB), qg2, kg2, ws, wst, bsp), vmem_mib=56, jobs=jobs)


def _w_in_grad(dpb, dkv, p0, p2, x, ng, scale, shift, name, jobs=()):
    s = x.shape[0]
    ts = min(2 * TILE, s)
    tiles = ts // TILE

    def body(dpb_ref, dkv_ref, p0_ref, p2_ref, x_ref, ng_ref, sc_ref, sh_ref, gw_ref, dkvb_ref):
        @pl.when(pl.program_id(0) == 0)
        def _():
            gw_ref[...] = jnp.zeros_like(gw_ref)

        xv = x_ref[...]
        r = lax.rsqrt(jnp.mean(xv * xv, axis=-1, keepdims=True) + EPS)
        h = (((xv * r) * ng_ref[...]) * (1.0 + sc_ref[...]) + sh_ref[...]).astype(BF16)
        for t in range(tiles):
            halo = slice(BLOCK * t, BLOCK * (t + 1))
            first = slice(TILE * t, TILE * t + BLOCK)
            last = slice(TILE * (t + 1) - BLOCK, TILE * (t + 1))
            dkvb_ref[first, :] = (dkv_ref[first, :] + p2_ref[halo, :]).astype(BF16)
            if SUB > 2:
                inner = slice(TILE * t + BLOCK, TILE * (t + 1) - BLOCK)
                dkvb_ref[inner, :] = dkv_ref[inner, :].astype(BF16)
            dkvb_ref[last, :] = (dkv_ref[last, :] + p0_ref[halo, :]).astype(BF16)
        gw_ref[...] += lax.dot_general(dpb_ref[...], h, TN_DIMS, preferred_element_type=F32)
        gw_ref[C_K:C_GA, :] += lax.dot_general(dkvb_ref[...], h, TN_DIMS, preferred_element_type=F32)

    kv = pl.BlockSpec((ts, 2 * D_KV), lambda i: (i, 0))
    halo = pl.BlockSpec((tiles * BLOCK, 2 * D_KV), lambda i: (i, 0))
    vec = _full((1, D_MODEL))
    return _pallas(
        body, name=name, grid=(s // ts,),
        in_specs=[pl.BlockSpec((ts, D_IN), lambda i: (i, 0)), kv, halo, halo,
                  pl.BlockSpec((ts, D_MODEL), lambda i: (i, 0)), vec, vec, vec],
        out_specs=[_full((D_IN, D_MODEL)), kv], out_shape=[_sds((D_IN, D_MODEL)), _sds((s, 2 * D_KV), BF16)],
        operands=(dpb, dkv, p0, p2, x, ng, scale, shift), vmem_mib=56, jobs=jobs)


def _proj_bwd(dpb, dkvb, x, dxo, ng, scale, wt, name, jobs=()):
    s = x.shape[0]
    ts = min(512, s)

    def body(dpb_ref, dkvb_ref, x_ref, dxo_ref, ng_ref, sc_ref, w_ref, dxi_ref, dsh_ref, dsc_ref, dng_ref):
        @pl.when(pl.program_id(0) == 0)
        def _():
            dsh_ref[...] = jnp.zeros_like(dsh_ref)
            dsc_ref[...] = jnp.zeros_like(dsc_ref)
            dng_ref[...] = jnp.zeros_like(dng_ref)

        dh = (jnp.dot(dpb_ref[...], w_ref[...], preferred_element_type=F32)
              + jnp.dot(dkvb_ref[...], w_ref[C_K:C_GA, :], preferred_element_type=F32))

        xv = x_ref[...]
        r = lax.rsqrt(jnp.mean(xv * xv, axis=-1, keepdims=True) + EPS)
        xn = xv * r
        ngv = ng_ref[...]
        sc1 = 1.0 + sc_ref[...]
        dsh_ref[...] += jnp.sum(dh, axis=0, keepdims=True)
        dsc_ref[...] += jnp.sum(dh * (xn * ngv), axis=0, keepdims=True)
        dh1 = dh * sc1
        dng_ref[...] += jnp.sum(dh1 * xn, axis=0, keepdims=True)
        dxn = dh1 * ngv
        dxi_ref[...] = r * (dxn - xn * jnp.mean(dxn * xn, axis=-1, keepdims=True)) + dxo_ref[...]

    row = pl.BlockSpec((ts, D_MODEL), lambda i: (i, 0))
    vec = _full((1, D_MODEL))
    return _pallas(
        body, name=name, grid=(s // ts,),
        in_specs=[pl.BlockSpec((ts, D_IN), lambda i: (i, 0)), pl.BlockSpec((ts, 2 * D_KV), lambda i: (i, 0)),
                  row, row, vec, vec, _full((D_IN, D_MODEL))],
        out_specs=[row, vec, vec, vec],
        out_shape=[_sds((s, D_MODEL)), _sds((1, D_MODEL)), _sds((1, D_MODEL)), _sds((1, D_MODEL))],
        operands=(dpb, dkvb, x, dxo, ng, scale, wt), vmem_mib=48, jobs=jobs)


def _pair_sum(g, r, c_idx, name, send_dtype=None):
    _, rows, cols = g.shape
    half = rows // 2

    def body(c_ref, g_ref, r_ref, o_ref, *narrow):
        total = g_ref[...] + r_ref[...]
        o_ref[...] = total
        for n_ref in narrow:
            n_ref[...] = total.astype(n_ref.dtype)

    blk = (None, half, cols)
    out_blk = pl.BlockSpec(blk, lambda j, c: (j, 0, 0))
    shapes = [_sds((N_CHIPS, half, cols))] + ([_sds((N_CHIPS, half, cols), send_dtype)] if send_dtype else [])
    return pl.pallas_call(
        body, name=name,
        grid_spec=pltpu.PrefetchScalarGridSpec(
            num_scalar_prefetch=1, grid=(N_CHIPS,),
            in_specs=[pl.BlockSpec(blk, lambda j, c: (j, c[0], 0)), out_blk], out_specs=[out_blk] * len(shapes)),
        out_shape=shapes,
        compiler_params=pltpu.CompilerParams(dimension_semantics=("arbitrary",), vmem_limit_bytes=32 * MIB),
    )(c_idx, g, r)


def _chip_sum(p, r, place, name):
    _, rows, cols = p.shape
    tr = rows // 2

    def body(j_ref, p_ref, r_ref, o_ref):
        o_ref[...] = ((p_ref[...] + r_ref[0].astype(F32)) + r_ref[1].astype(F32)) + r_ref[2].astype(F32)

    return pl.pallas_call(
        body, name=name,
        grid_spec=pltpu.PrefetchScalarGridSpec(
            num_scalar_prefetch=1, grid=(2,),
            in_specs=[pl.BlockSpec((None, tr, cols), lambda t, j: (j[0], t, 0)),
                      pl.BlockSpec((3, tr, cols), lambda t, j: (0, t, 0))],
            out_specs=pl.BlockSpec((tr, cols), lambda t, j: (2 * j[1] + t, 0))),
        out_shape=_sds((2 * rows, cols)),
        compiler_params=pltpu.CompilerParams(dimension_semantics=("arbitrary",), vmem_limit_bytes=32 * MIB),
    )(place, p, r)


def _cast_permute_w_in(wt, place_chunks):
    def body(t_ref, w_ref, w0_ref, w1_ref):
        def cast_into(o_ref):
            for t in range(N_CHUNKS):
                src = pl.multiple_of(t_ref[1 + t] * CHUNK_ROWS, CHUNK_ROWS)
                o_ref[CHUNK_ROWS * t:CHUNK_ROWS * (t + 1), :] = w_ref[pl.ds(src, CHUNK_ROWS), :].astype(BF16)

        @pl.when(pl.program_id(0) == 0)
        def _():
            cast_into(w0_ref)

        @pl.when(pl.program_id(0) == 1)
        def _():
            cast_into(w1_ref)

    return pl.pallas_call(
        body, name="cast_permute_w_in",
        grid_spec=pltpu.PrefetchScalarGridSpec(
            num_scalar_prefetch=1, grid=(DEPTH,),
            in_specs=[pl.BlockSpec((None, W_IN_BLK, D_MODEL), lambda l, tbl: (l, 0, 0))],
            out_specs=[pl.BlockSpec((None, W_IN_BLK, D_MODEL), lambda l, tbl: (tbl[0], 0, 0)),
                       pl.BlockSpec((W_IN_BLK, D_MODEL), lambda l, tbl: (0, 0))]),
        out_shape=[_sds((N_CHIPS, W_IN_BLK, D_MODEL), BF16), _sds((W_IN_BLK, D_MODEL), BF16)],
        compiler_params=pltpu.CompilerParams(dimension_semantics=("arbitrary",), vmem_limit_bytes=32 * MIB),
    )(place_chunks, wt)


def _gather_inputs(c, w_out, w0):
    half = W_IN_BLK // 2

    def body(c_ref, wout_ref, mine_ref, call_ref, woutb_ref, w0_ref, send_sems, recv_sems):
        x, y, cc = _coords()
        j = 2 * x + y
        b = 2 * j + cc
        sib = (x, y, 1 - cc)
        woutb_ref[...] = wout_ref[...].astype(BF16)
        call_ref[b] = c_ref[...]
        chips = _other_chips(x, y)

        def sems(k):
            return send_sems.at[k], recv_sems.at[k]

        def half_rows(chip_index):
            return w0_ref.at[chip_index, pl.ds(cc * half, half), :]

        first = [_remote(mine_ref.at[j, pl.ds(cc * half, half), :], half_rows(j), sems(k), (*chip, cc))
                 for k, chip in enumerate(chips)]
        k = 3
        rest = []
        for fx in (0, 1):
            for fy in (0, 1):
                for fc in (0, 1):
                    if fx or fy or fc:
                        dev = (1 - x if fx else x, 1 - y if fy else y, 1 - cc if fc else cc)
                        rest.append(_remote(call_ref.at[b], call_ref.at[b], sems(k), dev))
                        k += 1
        for cp in first + rest:
            cp.start()
        passed = []
        for k, chip in enumerate(chips):
            jk = 2 * chip[0] + chip[1]
            first[k].wait_recv()
            passed.append(_remote(half_rows(jk), half_rows(jk), sems(10 + k), sib))
            passed[k].start()
        for cp in first:
            cp.wait_send()
        for cp in rest + passed:
            cp.wait()

    return pl.pallas_call(
        body, name="gather_inputs", in_specs=[VMEM, VMEM, ANY], out_specs=[VMEM, VMEM, ANY],
        out_shape=[_sds((N_DEV, 1, D_MODEL)), _sds((DEPTH, W_OUT_BLK, D_MODEL), BF16),
                   _sds((N_CHIPS, W_IN_BLK, D_MODEL), BF16)],
        scratch_shapes=[pltpu.SemaphoreType.DMA((13,)), pltpu.SemaphoreType.DMA((13,))],
        input_output_aliases={2: 2},
        compiler_params=pltpu.CompilerParams(vmem_limit_bytes=32 * MIB),
    )(c, w_out, w0)


def _ada_rows(c_all, w_ada, b_blk):
    def body(c_ref, w_ref, b_ref, o_ref, cond_ref):
        cv = c_ref[...]
        cond = (cv * _sigmoid(cv)).astype(BF16)
        cond_ref[...] = cond.astype(F32)
        for l in range(DEPTH):
            o_ref[:, l, :] = jnp.dot(cond, w_ref[l].astype(BF16), preferred_element_type=F32) + b_ref[l:l + 1, :]

    return pl.pallas_call(
        body, name="ada_rows", in_specs=[VMEM, VMEM, VMEM], out_specs=[VMEM, VMEM],
        out_shape=[_sds((N_DEV, DEPTH, W_ADA_BLK)), _sds((N_DEV, D_MODEL))],
        compiler_params=pltpu.CompilerParams(vmem_limit_bytes=32 * MIB),
    )(c_all, w_ada, b_blk)


def _exchange_ada(part):
    def body(part_ref, out_ref, send_sems, recv_sems):
        x, y, cc = _coords()
        j = 2 * x + y
        out_ref[j] = part_ref[2 * j + cc]
        copies = []
        for k, chip in enumerate(_other_chips(x, y)):
            b_dst = 4 * chip[0] + 2 * chip[1] + cc
            copies.append(_remote(part_ref.at[b_dst], out_ref.at[j], (send_sems.at[k], recv_sems.at[k]), (*chip, cc)))
        for cp in copies:
            cp.start()
        for cp in copies:
            cp.wait()

    return pl.pallas_call(
        body, name="exchange_ada", in_specs=[VMEM], out_specs=VMEM,
        out_shape=_sds((N_CHIPS, DEPTH, W_ADA_BLK)),
        scratch_shapes=[pltpu.SemaphoreType.DMA((3,)), pltpu.SemaphoreType.DMA((3,))],
    )(part)


def _all_gather_rows(blk):
    m_per, n = blk.shape

    def body(x_ref, out_ref, send_sems, recv_sems, local_sem):
        x, y, c = _coords()
        me, sibling = (x, y, c), (x, y, 1 - c)
        chips = _other_chips(x, y)

        def rows(px, py, pc):
            return out_ref.at[pl.ds((4 * px + 2 * py + pc) * m_per, m_per), :]

        def copy(k, block, to, src=None):
            return _remote(rows(*block) if src is None else src, rows(*block), (send_sems.at[k], recv_sems.at[k]), to)

        mine = pltpu.make_async_copy(x_ref, rows(*me), local_sem)
        mine.start()
        first = [copy(0, me, sibling, src=x_ref)]
        first += [copy(1 + j, me, (*chip, c), src=x_ref) for j, chip in enumerate(chips)]
        for cp in first:
            cp.start()
        passed = [copy(4 + j, (*chip, c), sibling) for j, chip in enumerate(chips)]
        for j, chip in enumerate(chips):
            copy(1 + j, (*chip, c), me).wait_recv()
            passed[j].start()
        copy(0, sibling, me).wait_recv()
        for j, chip in enumerate(chips):
            copy(4 + j, (*chip, 1 - c), me).wait_recv()
        for cp in first + passed:
            cp.wait_send()
        mine.wait()

    return pl.pallas_call(
        body, name="all_gather_small", in_specs=[VMEM], out_specs=VMEM,
        out_shape=_sds((N_DEV * m_per, n), blk.dtype),
        scratch_shapes=[pltpu.SemaphoreType.DMA((7,)), pltpu.SemaphoreType.DMA((7,)), pltpu.SemaphoreType.DMA],
        compiler_params=pltpu.CompilerParams(vmem_limit_bytes=32 * MIB),
    )(blk)


def _adamw_math(w, g, m, v):
    m = ADAM_B1 * m + (1.0 - ADAM_B1) * g
    v = ADAM_B2 * v + (1.0 - ADAM_B2) * (g * g)
    m_hat = m / (1.0 - ADAM_B1 ** ADAM_STEP)
    v_hat = v / (1.0 - ADAM_B2 ** ADAM_STEP)
    delta = -ADAM_LR * (m_hat / (jnp.sqrt(v_hat) + ADAM_EPS) + ADAM_WD * w)
    return delta, m, v


def _adamw_w_in(w, g0, g1, m, v, pos_chunks):
    def body(t_ref, w_ref, g0_ref, g1_ref, m_ref, v_ref, g_ref, d_ref, nm_ref, nv_ref):
        for l, src in enumerate((g0_ref, g1_ref)):
            g = src[...]
            g_ref[l] = g
            d_ref[l], nm_ref[l], nv_ref[l] = _adamw_math(w_ref[l], g, m_ref[l], v_ref[l])

    nat = pl.BlockSpec((DEPTH, CHUNK_ROWS, D_MODEL), lambda t, tbl: (0, t, 0))
    per = pl.BlockSpec((CHUNK_ROWS, D_MODEL), lambda t, tbl: (tbl[t], 0))
    return pl.pallas_call(
        body, name="adamw_w_in",
        grid_spec=pltpu.PrefetchScalarGridSpec(num_scalar_prefetch=1, grid=(N_CHUNKS,),
                                               in_specs=[nat, per, per, nat, nat], out_specs=[nat] * 4),
        out_shape=[_sds(w.shape)] * 4,
        compiler_params=pltpu.CompilerParams(dimension_semantics=("arbitrary",)),
    )(pos_chunks, w, g0, g1, m, v)


def _adamw_w_out(w, g0, g1, m, v):
    def body(w_ref, g0_ref, g1_ref, m_ref, v_ref, g_ref, d_ref, nm_ref, nv_ref):
        g = jnp.where(pl.program_id(0) == 0, g0_ref[...], g1_ref[...])
        g_ref[...] = g
        d_ref[...], nm_ref[...], nv_ref[...] = _adamw_math(w_ref[...], g, m_ref[...], v_ref[...])

    blk = pl.BlockSpec((None, W_OUT_BLK, D_MODEL), lambda l: (l, 0, 0))
    gblk = _full((W_OUT_BLK, D_MODEL))
    return pl.pallas_call(
        body, name="adamw_w_out", grid=(DEPTH,), in_specs=[blk, gblk, gblk, blk, blk], out_specs=[blk] * 4,
        out_shape=[_sds(w.shape)] * 4,
        compiler_params=pltpu.CompilerParams(dimension_semantics=("arbitrary",), vmem_limit_bytes=32 * MIB),
    )(w, g0, g1, m, v)


def _w_ada_grad_adamw(cond_t, dada, w, m, v):
    _, rows, cols = w.shape
    tr = 256

    def body(ct_ref, da_ref, w_ref, m_ref, v_ref, g_ref, d_ref, nm_ref, nv_ref):
        g = jnp.dot(ct_ref[...], da_ref[...].astype(BF16), preferred_element_type=F32)
        g_ref[...] = g
        d_ref[...], nm_ref[...], nv_ref[...] = _adamw_math(w_ref[...], g, m_ref[...], v_ref[...])

    blk = pl.BlockSpec((None, tr, cols), lambda l, t: (l, t, 0))
    return pl.pallas_call(
        body, name="w_ada_grad_adamw", grid=(DEPTH, rows // tr),
        in_specs=[pl.BlockSpec((tr, BLOCK), lambda l, t: (t, 0)), pl.BlockSpec((None, BLOCK, cols), lambda l, t: (l, 0, 0)),
                  blk, blk, blk],
        out_specs=[blk] * 4, out_shape=[_sds(w.shape)] * 4,
        compiler_params=pltpu.CompilerParams(dimension_semantics=("arbitrary", "arbitrary"), vmem_limit_bytes=32 * MIB),
    )(cond_t, dada, w, m, v)


def _small_sum_adamw(gathered_a, gathered_b, w, m, v):
    def body(a_ref, b_ref, w_ref, m_ref, v_ref, g_ref, d_ref, nm_ref, nv_ref):
        def total(ref):
            g = ref[0]
            for b in range(1, N_DEV):
                g = g + ref[b]
            return g

        g = jnp.concatenate([total(a_ref), total(b_ref)], axis=0)
        g_ref[...] = g
        d_ref[...], nm_ref[...], nv_ref[...] = _adamw_math(w_ref[...], g, m_ref[...], v_ref[...])

    return pl.pallas_call(
        body, name="small_sum_adamw", in_specs=[VMEM] * 5, out_specs=[VMEM] * 4, out_shape=[_sds(w.shape)] * 4,
        compiler_params=pltpu.CompilerParams(vmem_limit_bytes=48 * MIB),
    )(gathered_a, gathered_b, w, m, v)


_SMALL_A = (("w_s", DEPTH * 8 * BLOCK), ("b_s", DEPTH * 8), ("q_gain", 1), ("k_gain", 1), ("sink", 1))
_SMALL_B = (("b_ada", DEPTH * 24), ("norm_gain", DEPTH * 8), ("sq_err", 1))


def _pack_rows(parts, layout):
    rows = []
    for name, n in layout:
        flat = parts[name].reshape(-1)
        rows.append(jnp.pad(flat, (0, n * 128 - flat.shape[0])).reshape(n, 128))
    n_rows = sum(n for _, n in layout)
    if n_rows % 8:
        rows.append(jnp.zeros((-n_rows % 8, 128), F32))
    return jnp.concatenate(rows, axis=0)


def _pack_small(parts):
    return jnp.concatenate([_pack_rows(parts, _SMALL_A), _pack_rows(parts, _SMALL_B)], axis=0)


def _unpack_small(packed, shapes):
    out, r0 = {}, 0
    for layout in (_SMALL_A, _SMALL_B):
        for name, n in layout:
            size = 1
            for d in shapes[name]:
                size *= d
            out[name] = packed[r0:r0 + n].reshape(-1)[:size].reshape(shapes[name])
            r0 += n
        r0 += -r0 % 8
    return out


def _permute_heads(a, axis):
    shp = a.shape
    a = a.reshape(shp[:axis] + (2, 4, HEAD_DIM) + shp[axis + 1:])
    a = jnp.swapaxes(a, axis, axis + 1)
    return a.reshape(shp)


def _unpermute_heads(a, axis):
    shp = a.shape
    a = a.reshape(shp[:axis] + (4, 2, HEAD_DIM) + shp[axis + 1:])
    a = jnp.swapaxes(a, axis, axis + 1)
    return a.reshape(shp)


def _permute_w_out(w):
    return jnp.concatenate([_permute_heads(w[:D_ATTN], 0), w[D_ATTN:]], axis=0)


def _unpermute_w_out(w):
    return jnp.concatenate([_unpermute_heads(w[:D_ATTN], 0), w[D_ATTN:]], axis=0)


def kernel(x, c, w_ada, b_ada, norm_gain, w_in, q_gain, k_gain, sink, w_s, b_s, w_out, loss_target, m_w_ada, m_b_ada, m_norm_gain, m_w_in, m_q_gain, m_k_gain, m_sink, m_w_s, m_b_s, m_w_out, v_w_ada, v_b_ada, v_norm_gain, v_w_in, v_q_gain, v_k_gain, v_sink, v_w_s, v_b_s, v_w_out):
    ix, iy, ic = _coords()
    chip = 2 * ix + iy
    chip_idx = jnp.stack([chip, ic]).astype(jnp.int32)
    core_idx = jnp.reshape(ic, (1,)).astype(jnp.int32)
    src_chunks = lax.dynamic_index_in_dim(jnp.asarray(_CHUNK_SRC), chip, 0, keepdims=False)
    pos_chunks = lax.dynamic_index_in_dim(jnp.asarray(_CHUNK_POS), chip, 0, keepdims=False)
    x0, target = x[0], loss_target[0]

    wt, mt, vt = (jnp.swapaxes(a, 1, 2) for a in (w_in, m_w_in, v_w_in))
    w0_mine, wloc_in1 = _cast_permute_w_in(wt, jnp.concatenate([chip_idx[:1], src_chunks]))
    c_all, wloc_out, w0 = _gather_inputs(c, w_out, w0_mine)
    wts = [w0.reshape(D_IN, D_MODEL), None]

    b_blk = lax.dynamic_slice_in_dim(b_ada, chip * W_ADA_BLK, W_ADA_BLK, axis=1)
    ada_part, cond = _ada_rows(c_all.reshape(N_DEV, D_MODEL), w_ada, b_blk)
    ada = jnp.moveaxis(_exchange_ada(ada_part), 0, 1).reshape(DEPTH, 3 * D_MODEL)
    shift = [ada[l:l + 1, 0:D_MODEL] for l in range(DEPTH)]
    scale = [ada[l:l + 1, D_MODEL:2 * D_MODEL] for l in range(DEPTH)]
    gate = [ada[l:l + 1, 2 * D_MODEL:] for l in range(DEPTH)]
    ng = [norm_gain[l:l + 1] for l in range(DEPTH)]

    qg2 = jnp.concatenate([q_gain, q_gain], axis=-1)
    kg2 = jnp.concatenate([k_gain, k_gain], axis=-1)
    ws_b = w_s.astype(BF16)
    wst_b = jnp.swapaxes(w_s, -1, -2).astype(BF16)
    bsp = jnp.repeat(jnp.swapaxes(b_s.reshape(DEPTH, 4, 2, BLOCK), -1, -2), HEAD_DIM, axis=-1)
    bias = jnp.asarray(_bias_table())

    def mix_args(l):
        return bias, sink[l], qg2[l:l + 1], kg2[l:l + 1], ws_b[l]

    w_out_shape = (W_OUT_BLK, D_MODEL)
    proj0, w1 = _proj_fwd(x0, ng[0], scale[0], shift[0], wts[0], "proj_fwd_0",
                          jobs=[_job_gather([(wloc_in1, None)], [(W_IN_BLK, D_MODEL)])])
    y0, wo0, wo1 = _mix_fwd(proj0, *mix_args(0), bsp[0], "mix_fwd_0",
                            jobs=[_job_gather([(wloc_out, 0), (wloc_out, 1)], [w_out_shape, w_out_shape])])
    wts[1] = w1.reshape(D_IN, D_MODEL)
    wos = [_permute_w_out(w.reshape(D_MODEL, D_MODEL)) for w in (wo0, wo1)]
    x1, = _out_fwd(y0, x0, gate[0], wos[0], "out_fwd_0")
    proj1, = _proj_fwd(x1, ng[1], scale[1], shift[1], wts[1], "proj_fwd_1")
    y1, = _mix_fwd(proj1, *mix_args(1), bsp[1], "mix_fwd_1")
    dx2, sq = _out_fwd_loss(y1, x1, gate[1], wos[1], target, "out_fwd_loss_1")

    def blocks_out(gw):
        return _unpermute_w_out(gw).reshape(N_CHIPS, W_OUT_BLK, D_MODEL)

    dy1, gwo1, dgate1 = _out_bwd(dx2, y1, gate[1], wos[1], "out_bwd_1")
    go1 = blocks_out(gwo1)
    dpb, dkv, p0, p2, dqg1, dkg1, dsink1, dws1, dbsp1, ro1 = _mix_bwd(
        dy1, proj1, *mix_args(1), wst_b[1], bsp[1], "mix_bwd_1", jobs=[_job_swap(go1)])
    po1, = _pair_sum(go1, ro1, core_idx, "pair_sum_w_out_1")
    gwi1, dkvb, co1 = _w_in_grad(dpb, dkv, p0, p2, x1, ng[1], scale[1], shift[1], "w_in_grad_1", jobs=[_job_scatter(po1)])
    gi1 = gwi1.reshape(N_CHIPS, W_IN_BLK, D_MODEL)
    fo1 = _chip_sum(po1, co1, chip_idx, "chip_sum_w_out_1")
    dx1, dsh1, dsc1, dng1, grad_wo1, ri1 = _proj_bwd(dpb, dkvb, x1, dx2, ng[1], scale[1], wts[1], "proj_bwd_1",
                                                     jobs=[_job_join(fo1), _job_swap(gi1)])
    pi1, = _pair_sum(gi1, ri1, core_idx, "pair_sum_w_in_1")

    dy0, gwo0, dgate0 = _out_bwd(dx1, y0, gate[0], wos[0], "out_bwd_0")
    go0 = blocks_out(gwo0)
    dpb, dkv, p0, p2, dqg0, dkg0, dsink0, dws0, dbsp0, ci1, ro0 = _mix_bwd(
        dy0, proj0, *mix_args(0), wst_b[0], bsp[0], "mix_bwd_0", jobs=[_job_scatter(pi1), _job_swap(go0)])
    fi1 = _chip_sum(pi1, ci1, chip_idx, "chip_sum_w_in_1")
    po0, = _pair_sum(go0, ro0, core_idx, "pair_sum_w_out_0")

    def bs_grad(dbsp):
        return jnp.swapaxes(dbsp[:, :, ::HEAD_DIM], -1, -2).reshape(8, BLOCK)

    small_g = dict(
        w_s=jnp.stack([dws0, dws1]), b_s=jnp.stack([bs_grad(dbsp0), bs_grad(dbsp1)]),
        q_gain=jnp.stack([dqg0[0, :HEAD_DIM], dqg1[0, :HEAD_DIM]]), k_gain=jnp.stack([dkg0[0, :HEAD_DIM], dkg1[0, :HEAD_DIM]]),
        sink=jnp.stack([dsink0[0, :N_HEADS], dsink1[0, :N_HEADS]]))
    gwi0, dkvb, grad_wi1, co0, gathered_a = _w_in_grad(
        dpb, dkv, p0, p2, x0, ng[0], scale[0], shift[0], "w_in_grad_0",
        jobs=[_job_join(fi1), _job_scatter(po0), _job_all_gather(_pack_rows(small_g, _SMALL_A))])
    gi0 = gwi0.reshape(N_CHIPS, W_IN_BLK, D_MODEL)
    fo0 = _chip_sum(po0, co0, chip_idx, "chip_sum_w_out_0")

    ri0, grad_wo0 = _comm([_job_swap(gi0), _job_join(fo0)], "swap_w_in_0")
    pi0, pi0_send = _pair_sum(gi0, ri0, core_idx, "pair_sum_w_in_0", send_dtype=BF16)
    dx0, dsh0, dsc0, dng0, ci0 = _proj_bwd(dpb, dkvb, x0, dx1, ng[0], scale[0], wts[0], "proj_bwd_0",
                                           jobs=[_job_scatter(pi0_send)])
    fi0 = _chip_sum(pi0, ci0, chip_idx, "chip_sum_w_in_0")
    grad_wi0, = _comm([_job_join(fi0)], "join_w_in_0")

    small_g.update(
        b_ada=jnp.stack([jnp.concatenate([dsh0, dsc0, dgate0], axis=-1)[0], jnp.concatenate([dsh1, dsc1, dgate1], axis=-1)[0]]),
        norm_gain=jnp.stack([dng0[0], dng1[0]]), sq_err=sq[0])
    none = jnp.zeros((1,), F32)
    small_w = dict(w_s=w_s, b_s=b_s, b_ada=b_ada, norm_gain=norm_gain, q_gain=q_gain, k_gain=k_gain, sink=sink, sq_err=none)
    small_m = dict(w_s=m_w_s, b_s=m_b_s, b_ada=m_b_ada, norm_gain=m_norm_gain, q_gain=m_q_gain, k_gain=m_k_gain, sink=m_sink,
                   sq_err=none)
    small_v = dict(w_s=v_w_s, b_s=v_b_s, b_ada=v_b_ada, norm_gain=v_norm_gain, q_gain=v_q_gain, k_gain=v_k_gain, sink=v_sink,
                   sq_err=none)
    gathered_b = _all_gather_rows(_pack_rows(small_g, _SMALL_B))
    packed = _small_sum_adamw(gathered_a.reshape(N_DEV, -1, 128), gathered_b.reshape(N_DEV, -1, 128),
                              _pack_small(small_w), _pack_small(small_m), _pack_small(small_v))
    shapes = {k: a.shape for k, a in small_w.items()}
    sg, sd, sm, sv = (_unpack_small(p, shapes) for p in packed)
    loss = 0.5 * sg["sq_err"][0]

    dada_all = gathered_b.reshape(N_DEV, -1, 128)[:, 0:DEPTH * 24].reshape(N_DEV, DEPTH, 3 * D_MODEL)
    dada_blk = jnp.moveaxis(lax.dynamic_slice_in_dim(dada_all, chip * W_ADA_BLK, W_ADA_BLK, axis=2), 0, 1)
    pad = BLOCK - N_DEV
    ada_out = _w_ada_grad_adamw(
        jnp.pad(cond.T, ((0, 0), (0, pad))).astype(BF16), jnp.pad(dada_blk, ((0, 0), (0, pad), (0, 0))),
        w_ada, m_w_ada, v_w_ada)

    in_out = [jnp.swapaxes(a, 1, 2) for a in _adamw_w_in(wt, grad_wi0, grad_wi1, mt, vt, pos_chunks)]
    out_out = _adamw_w_out(w_out, grad_wo0, grad_wo1, m_w_out, v_w_out)

    def ordered(k):
        small = (sg, sd, sm, sv)[k]
        return (ada_out[k], small["b_ada"], small["norm_gain"], in_out[k], small["q_gain"], small["k_gain"], small["sink"],
                small["w_s"], small["b_s"], out_out[k])

    return (loss, dx0[None], *ordered(0), *ordered(1), *ordered(2), *ordered(3))
```

```python
import numpy as np

import jax
import jax.numpy as jnp
from jax import lax
from jax.experimental import pallas as pl
from jax.experimental.pallas import tpu as pltpu

F32 = jnp.float32
BF16 = jnp.bfloat16

D_MODEL = 1024
DEPTH = 2
HEAD_DIM = 64
N_HEADS = 8
BLOCK = 128
SUB = 4
TILE = SUB * BLOCK
STACK = 8 * BLOCK
D_ATTN = 512
D_KV = 128
D_IN = 2816
N_CHIPS = 4
N_DEV = 8
W_IN_BLK = D_IN // N_CHIPS
W_OUT_BLK = D_MODEL // N_CHIPS
W_ADA_BLK = 3 * D_MODEL // N_CHIPS
CHUNK_ROWS = HEAD_DIM
N_CHUNKS = W_IN_BLK // CHUNK_ROWS
EPS = 1e-6
NEG_INF = -1e30

C_Q, C_K, C_V, C_GA, C_U, C_VG, C_GG = 0, 512, 640, 768, 1280, 1792, 2304

ADAM_LR = 0.001
ADAM_B1 = 0.9
ADAM_B2 = 0.999
ADAM_EPS = 1e-08
ADAM_WD = 0.01
ADAM_STEP = 10

MESH = pl.DeviceIdType.MESH
MIB = 1024 * 1024
ANY = pl.BlockSpec(memory_space=pl.ANY)
VMEM = pl.BlockSpec(memory_space=pltpu.VMEM)

NT_DIMS = (((1,), (1,)), ((), ()))
TN_DIMS = (((0,), (0,)), ((), ()))

_PAIR_ORDER = (0, 4, 1, 5, 2, 6, 3, 7)
_CHUNK_SRC = np.array([
    list(_PAIR_ORDER) + [8, 9, 10],
    [0] + [1 + h for h in _PAIR_ORDER] + [9, 10],
    list(range(N_CHUNKS)),
    list(range(N_CHUNKS)),
], np.int32)
_CHUNK_POS = np.argsort(_CHUNK_SRC, axis=1).astype(np.int32)


def _bias_table():
    i = np.arange(N_HEADS * BLOCK)[:, None]
    j = np.arange(3 * BLOCK)[None, :]
    dist = np.abs(j - BLOCK - (i % BLOCK))
    slope = 2.0 ** -(i // BLOCK + 1.0)
    inner = np.where(dist <= BLOCK, -(slope * dist), NEG_INF)
    first = np.where(j >= BLOCK, inner, NEG_INF)
    last = np.where(j < 2 * BLOCK, inner, NEG_INF)
    return np.stack([first, inner, last]).astype(np.float32)


def _full(shape):
    n = len(shape)
    return pl.BlockSpec(shape, lambda *_: (0,) * n)


def _sds(shape, dtype=F32):
    return jax.ShapeDtypeStruct(shape, dtype)


def _coords():
    return lax.axis_index("x"), lax.axis_index("y"), lax.axis_index("c")


def _other_chips(x, y):
    return [(1 - x, y), (x, 1 - y), (1 - x, 1 - y)]


def _remote(src, dst, sems, dev):
    return pltpu.make_async_remote_copy(src_ref=src, dst_ref=dst, send_sem=sems[0], recv_sem=sems[1],
                                        device_id=dev, device_id_type=MESH)


class _Job:
    def __init__(self, inputs, out_shapes, n_remote, n_local, make, then=None, in_place=False):
        self.inputs, self.out_shapes, self.n_remote, self.n_local, self.make = inputs, out_shapes, n_remote, n_local, make
        self.then = then
        self.in_place = in_place


def _job_aliases(jobs, in_base, out_base):
    aliases, a, b = {}, 0, 0
    for j in jobs:
        if j.in_place:
            aliases.update({in_base + a + k: out_base + b + k for k in range(len(j.inputs))})
        a, b = a + len(j.inputs), b + len(j.out_shapes)
    return aliases


def _job_copies(jobs, jin, jout, sems, second=False):
    send, recv, loc = sems
    res, a, b, r, l = [], 0, 0, 0, 0
    for j in jobs:
        build = j.then if second else j.make
        if build is not None:
            res += build(jin[a:a + len(j.inputs)], jout[b:b + len(j.out_shapes)],
                         lambda k, r=r: (send.at[r + k], recv.at[r + k]), lambda k, l=l: loc.at[l + k])
        a, b, r, l = a + len(j.inputs), b + len(j.out_shapes), r + j.n_remote, l + j.n_local
    return res


def _run(copies):
    for cp in copies:
        cp.start()
    for cp in copies:
        cp.wait()


def _job_gather(sources, shapes):
    n = len(sources)

    def make(ins, outs, rsem, lsem):
        x, y, c = _coords()
        j = 2 * x + y
        res = []
        for t, ((_, layer), src, dst) in enumerate(zip(sources, ins, outs)):
            src = src if layer is None else src.at[layer]
            res.append(pltpu.make_async_copy(src, dst.at[j], lsem(t)))
            for k, chip in enumerate(_other_chips(x, y)):
                res.append(_remote(src, dst.at[j], rsem(3 * t + k), (*chip, c)))
        return res

    return _Job([a for a, _ in sources], [_sds((N_CHIPS,) + s, BF16) for s in shapes], 3 * n, n, make)


def _job_swap(g):
    _, rows, cols = g.shape
    half = rows // 2

    def make(ins, outs, rsem, lsem):
        x, y, c = _coords()
        return [_remote(ins[0].at[:, pl.ds((1 - c) * half, half), :], outs[0], rsem(0), (x, y, 1 - c))]

    return _Job([g], [_sds((N_CHIPS, half, cols))], 1, 0, make)


def _job_scatter(p):
    def make(ins, outs, rsem, lsem):
        x, y, c = _coords()
        return [_remote(ins[0].at[2 * chip[0] + chip[1]], outs[0].at[k], rsem(k), (*chip, c))
                for k, chip in enumerate(_other_chips(x, y))]

    return _Job([p], [_sds((3,) + p.shape[1:], p.dtype)], 3, 0, make)


def _job_all_gather(blk):
    m_per = blk.shape[0]

    def rows(ref, px, py, pc):
        return ref.at[pl.ds((4 * px + 2 * py + pc) * m_per, m_per), :]

    def make(ins, outs, rsem, lsem):
        x, y, c = _coords()
        res = [pltpu.make_async_copy(ins[0], rows(outs[0], x, y, c), lsem(0)),
               _remote(ins[0], rows(outs[0], x, y, c), rsem(0), (x, y, 1 - c))]
        res += [_remote(ins[0], rows(outs[0], x, y, c), rsem(1 + k), (*chip, c)) for k, chip in enumerate(_other_chips(x, y))]
        return res

    def then(ins, outs, rsem, lsem):
        x, y, c = _coords()
        return [_remote(rows(outs[0], *chip, c), rows(outs[0], *chip, c), rsem(4 + k), (x, y, 1 - c))
                for k, chip in enumerate(_other_chips(x, y))]

    return _Job([blk], [_sds((N_DEV * m_per, blk.shape[1]), blk.dtype)], 7, 1, make, then)


def _job_join(f):
    half = f.shape[0] // 2

    def make(ins, outs, rsem, lsem):
        x, y, c = _coords()
        mine = pl.ds(c * half, half)
        return [_remote(ins[0].at[mine, :], outs[0].at[mine, :], rsem(0), (x, y, 1 - c))]

    return _Job([f], [_sds(f.shape, f.dtype)], 1, 0, make, in_place=True)


def _pallas(body, *, name, grid, in_specs, out_specs, out_shape, operands, vmem_mib, jobs=()):
    in_specs, out_specs, out_shape = list(in_specs), list(out_specs), list(out_shape)
    n_in, n_out = len(in_specs), len(out_specs)
    j_in = [a for j in jobs for a in j.inputs]
    j_out = [s for j in jobs for s in j.out_shapes]
    n_rem = max(1, sum(j.n_remote for j in jobs))
    n_loc = max(1, sum(j.n_local for j in jobs))
    scratch = [pltpu.SemaphoreType.DMA((n_rem,)), pltpu.SemaphoreType.DMA((n_rem,)),
               pltpu.SemaphoreType.DMA((n_loc,))] if jobs else []

    def wrapped(*refs):
        ins = refs[:n_in]
        jin = refs[n_in:n_in + len(j_in)]
        outs = refs[n_in + len(j_in):n_in + len(j_in) + n_out]
        jout = refs[n_in + len(j_in) + n_out:n_in + len(j_in) + n_out + len(j_out)]

        if jobs:
            first = last = None
            for d, n in enumerate(grid):
                f, e = pl.program_id(d) == 0, pl.program_id(d) == n - 1
                first, last = (f, e) if first is None else (first & f, last & e)

            @pl.when(first)
            def _():
                for cp in _job_copies(jobs, jin, jout, refs[-3:]):
                    cp.start()

        body(*ins, *outs)

        if jobs:
            @pl.when(last)
            def _():
                for cp in _job_copies(jobs, jin, jout, refs[-3:]):
                    cp.wait()
                _run(_job_copies(jobs, jin, jout, refs[-3:], second=True))

    operands = [pltpu.with_memory_space_constraint(a, pltpu.HBM) if a.size * a.dtype.itemsize >= MIB else a
                for a in operands]
    return pl.pallas_call(
        wrapped, name=name, grid=grid,
        in_specs=in_specs + [ANY] * len(j_in), out_specs=out_specs + [ANY] * len(j_out),
        out_shape=out_shape + j_out, scratch_shapes=scratch, input_output_aliases=_job_aliases(jobs, n_in, n_out),
        compiler_params=pltpu.CompilerParams(dimension_semantics=("arbitrary",) * len(grid),
                                             vmem_limit_bytes=vmem_mib * MIB),
    )(*operands, *j_in)


def _comm(jobs, name):
    j_in = [a for j in jobs for a in j.inputs]
    j_out = [s for j in jobs for s in j.out_shapes]
    n_rem = max(1, sum(j.n_remote for j in jobs))
    n_loc = max(1, sum(j.n_local for j in jobs))

    def body(*refs):
        jin, jout = refs[:len(j_in)], refs[len(j_in):len(j_in) + len(j_out)]
        _run(_job_copies(jobs, jin, jout, refs[-3:]))
        _run(_job_copies(jobs, jin, jout, refs[-3:], second=True))

    return pl.pallas_call(
        body, name=name, in_specs=[ANY] * len(j_in), out_specs=[ANY] * len(j_out), out_shape=j_out,
        scratch_shapes=[pltpu.SemaphoreType.DMA((n_rem,)), pltpu.SemaphoreType.DMA((n_rem,)),
                        pltpu.SemaphoreType.DMA((n_loc,))],
        input_output_aliases=_job_aliases(jobs, 0, 0),
    )(*j_in)


def _sigmoid(x):
    return 1.0 / (1.0 + jnp.exp(-x))


def _lo_mask(shape):
    return lax.broadcasted_iota(jnp.int32, shape, len(shape) - 1) < HEAD_DIM


def _half_sum(x, lo):
    a = jnp.sum(jnp.where(lo, x, 0.0), axis=-1, keepdims=True)
    b = jnp.sum(jnp.where(lo, 0.0, x), axis=-1, keepdims=True)
    return jnp.where(lo, a, b)


def _half_rms_scale(x, lo):
    return lax.rsqrt(_half_sum(x * x, lo) * (1.0 / HEAD_DIM) + EPS)


def _stack_heads(pairs, lo):
    return jnp.concatenate([jnp.where(lo, t, 0.0) for t in pairs] + [jnp.where(lo, 0.0, t) for t in pairs], axis=0)


def _unstack_pair(stack, p, lo):
    return jnp.where(lo, stack[BLOCK * p:BLOCK * (p + 1)], stack[BLOCK * (4 + p):BLOCK * (5 + p)])


def _attention_probs(q_stack, kn, bias_ref, sink_ref):
    rows = N_HEADS * BLOCK
    s = lax.dot_general(q_stack, kn, NT_DIMS, preferred_element_type=F32) + bias_ref[...]
    sink = jnp.concatenate([jnp.full((BLOCK, BLOCK), sink_ref[h], F32) for h in range(N_HEADS)], axis=0)
    cols = [s[:, BLOCK * j:BLOCK * (j + 1)] for j in range(3)]
    top = jnp.max(jnp.maximum(jnp.maximum(cols[0], cols[1]), cols[2]), axis=-1, keepdims=True)
    m = jnp.maximum(jnp.broadcast_to(top, (rows, BLOCK)), sink)
    e = [jnp.exp(c - m) for c in cols]
    es = jnp.exp(sink - m)
    inv = 1.0 / (jnp.broadcast_to(jnp.sum((e[0] + e[1]) + e[2], axis=-1, keepdims=True), (rows, BLOCK)) + es)
    return jnp.concatenate([c * inv for c in e], axis=1), es * inv


def _kv_rows(cur_ref, pkv_ref, nkv_ref):
    k = jnp.concatenate([pkv_ref[:, 0:D_KV], cur_ref[:, C_K:C_K + D_KV], nkv_ref[:, 0:D_KV]], axis=0)
    v = jnp.concatenate([pkv_ref[:, D_KV:2 * D_KV], cur_ref[:, C_V:C_V + D_KV], nkv_ref[:, D_KV:2 * D_KV]], axis=0)
    return k, v


def _overlap_add(parts):
    blocks = []
    for j in range(SUB + 2):
        terms = [parts[b][BLOCK * (j - b):BLOCK * (j - b + 1)] for b in range(SUB) if 0 <= j - b <= 2]
        total = terms[0]
        for t in terms[1:]:
            total = total + t
        blocks.append(total)
    return jnp.concatenate(blocks, axis=0)


def _mix_specs(nt):
    cur = pl.BlockSpec((TILE, D_IN), lambda i: (i, 0))
    kv_col = C_K // (2 * D_KV)
    pkv = pl.BlockSpec((BLOCK, 2 * D_KV), lambda i: (jnp.maximum(i * SUB - 1, 0), kv_col))
    nkv = pl.BlockSpec((BLOCK, 2 * D_KV), lambda i: (jnp.minimum((i + 1) * SUB, nt * SUB - 1), kv_col))
    table = (None, N_HEADS * BLOCK, 3 * BLOCK)
    first = pl.BlockSpec(table, lambda i: (jnp.where(i == 0, 0, 1), 0, 0))
    inner = pl.BlockSpec(table, lambda i: (1, 0, 0))
    last = pl.BlockSpec(table, lambda i: (jnp.where(i == nt - 1, 2, 1), 0, 0))
    return cur, pkv, nkv, [first] + [inner] * (SUB - 2) + [last]


def _proj_fwd(x, ng, scale, shift, wt, name, jobs=()):
    s = x.shape[0]
    ts = min(512, s)

    def body(x_ref, ng_ref, sc_ref, sh_ref, w_ref, o_ref):
        xv = x_ref[...]
        r = lax.rsqrt(jnp.mean(xv * xv, axis=-1, keepdims=True) + EPS)
        h = ((xv * r) * ng_ref[...]) * (1.0 + sc_ref[...]) + sh_ref[...]
        o_ref[...] = lax.dot_general(h.astype(BF16), w_ref[...], NT_DIMS, preferred_element_type=F32)

    vec = _full((1, D_MODEL))
    return _pallas(
        body, name=name, grid=(s // ts,),
        in_specs=[pl.BlockSpec((ts, D_MODEL), lambda i: (i, 0)), vec, vec, vec, _full((D_IN, D_MODEL))],
        out_specs=[pl.BlockSpec((ts, D_IN), lambda i: (i, 0))], out_shape=[_sds((s, D_IN))],
        operands=(x, ng, scale, shift, wt), vmem_mib=48, jobs=jobs)


def _mix_fwd(proj, bias, sink, qg2, kg2, ws, bsp, name, jobs=()):
    s = proj.shape[0]
    nt = s // TILE

    def body(sink_ref, cur_ref, pkv_ref, nkv_ref, *rest):
        bias_refs = rest[:SUB]
        qg_ref, kg_ref, ws_ref, bsp_ref, y_ref, p_ref, ps_ref, attn_ref, sv_ref = rest[SUB:]
        lo = _lo_mask((BLOCK, BLOCK))
        lo_kv = _lo_mask((TILE + 2 * BLOCK, BLOCK))
        k_all, v_all = _kv_rows(cur_ref, pkv_ref, nkv_ref)
        kn_all = ((k_all * _half_rms_scale(k_all, lo_kv)) * kg_ref[...]).astype(BF16)
        vb_all = v_all.astype(BF16)
        for b in range(SUB):
            rows = slice(BLOCK * b, BLOCK * (b + 1))
            window = slice(BLOCK * b, BLOCK * (b + 3))
            qn = []
            for p in range(4):
                q = cur_ref[rows, C_Q + BLOCK * p:C_Q + BLOCK * (p + 1)]
                qn.append(((q * _half_rms_scale(q, lo)) * qg_ref[...]) * 0.125)
            q_stack = _stack_heads(qn, lo).astype(BF16)
            prob, psink = _attention_probs(q_stack, kn_all[window], bias_refs[b], sink_ref)
            pb = prob.astype(BF16)
            p_ref[STACK * b:STACK * (b + 1), :] = pb
            ps_ref[STACK * b:STACK * (b + 1), :] = psink
            o_stack = jnp.dot(pb, vb_all[window], preferred_element_type=F32)
            for p in range(4):
                g = cur_ref[rows, C_GA + BLOCK * p:C_GA + BLOCK * (p + 1)]
                attn = _unstack_pair(o_stack, p, lo)
                attn_ref[rows, BLOCK * p:BLOCK * (p + 1)] = attn
                y_ref[rows, BLOCK * p:BLOCK * (p + 1)] = (attn * (g * _sigmoid(g))).astype(BF16)

        for p in range(4):
            cols = slice(C_VG + BLOCK * p, C_VG + BLOCK * (p + 1))
            vn = []
            for b in range(SUB):
                vg = cur_ref[BLOCK * b:BLOCK * (b + 1), cols]
                vn.append((vg * _half_rms_scale(vg, lo)).astype(BF16))
            vn = jnp.concatenate(vn, axis=1)
            sv_a = jnp.dot(ws_ref[2 * p], vn, preferred_element_type=F32)
            sv_b = jnp.dot(ws_ref[2 * p + 1], vn, preferred_element_type=F32)
            for b in range(SUB):
                rows = slice(BLOCK * b, BLOCK * (b + 1))
                lanes = slice(BLOCK * b, BLOCK * (b + 1))
                sv = jnp.where(lo, sv_a[:, lanes], sv_b[:, lanes]) + bsp_ref[p]
                sv_ref[rows, BLOCK * p:BLOCK * (p + 1)] = sv
                u = cur_ref[rows, C_U + BLOCK * p:C_U + BLOCK * (p + 1)]
                g = cur_ref[rows, C_GG + BLOCK * p:C_GG + BLOCK * (p + 1)]
                y_ref[rows, D_ATTN + BLOCK * p:D_ATTN + BLOCK * (p + 1)] = ((u * sv) * (g * _sigmoid(g))).astype(BF16)

    cur, pkv, nkv, bias_specs = _mix_specs(nt)
    nb = nt * SUB
    half = pl.BlockSpec((TILE, D_ATTN), lambda i: (i, 0))
    return _pallas(
        body, name=name, grid=(nt,),
        in_specs=[pl.BlockSpec(memory_space=pltpu.SMEM), cur, pkv, nkv, *bias_specs, _full((1, BLOCK)), _full((1, BLOCK)),
                  _full((8, BLOCK, BLOCK)), _full((4, BLOCK, BLOCK))],
        out_specs=[pl.BlockSpec((TILE, D_MODEL), lambda i: (i, 0)), pl.BlockSpec((SUB * STACK, 3 * BLOCK), lambda i: (i, 0)),
                   pl.BlockSpec((SUB * STACK, BLOCK), lambda i: (i, 0)), half, half],
        out_shape=[_sds((s, D_MODEL), BF16), _sds((nb * STACK, 3 * BLOCK), BF16), _sds((nb * STACK, BLOCK)),
                   _sds((s, D_ATTN)), _sds((s, D_ATTN))],
        operands=(sink, proj, proj, proj, *([bias] * SUB), qg2, kg2, ws, bsp), vmem_mib=56, jobs=jobs)


def _out_fwd(y, x, gate, w_out, name):
    s = x.shape[0]
    ts = min(512, s)

    def body(y_ref, x_ref, g_ref, w_ref, o_ref):
        o_ref[...] = x_ref[...] + g_ref[...] * jnp.dot(y_ref[...], w_ref[...], preferred_element_type=F32)

    row = pl.BlockSpec((ts, D_MODEL), lambda i: (i, 0))
    return _pallas(
        body, name=name, grid=(s // ts,), in_specs=[row, row, _full((1, D_MODEL)), _full((D_MODEL, D_MODEL))],
        out_specs=[row], out_shape=[_sds((s, D_MODEL))], operands=(y, x, gate, w_out), vmem_mib=32)


def _out_fwd_loss(y, x, gate, w_out, target, name):
    s = x.shape[0]
    ts = min(512, s)

    def body(y_ref, x_ref, g_ref, w_ref, t_ref, dx_ref, sq_ref):
        @pl.when(pl.program_id(0) == 0)
        def _():
            sq_ref[...] = jnp.zeros_like(sq_ref)

        out = x_ref[...] + g_ref[...] * jnp.dot(y_ref[...], w_ref[...], preferred_element_type=F32)
        diff = out - t_ref[...]
        dx_ref[...] = diff * (1.0 / D_MODEL)
        per_token = jnp.sum(diff * diff, axis=-1, keepdims=True) * (1.0 / D_MODEL)
        sq_ref[...] += jnp.sum(per_token, axis=0, keepdims=True)

    row = pl.BlockSpec((ts, D_MODEL), lambda i: (i, 0))
    return _pallas(
        body, name=name, grid=(s // ts,), in_specs=[row, row, _full((1, D_MODEL)), _full((D_MODEL, D_MODEL)), row],
        out_specs=[row, _full((1, 1))], out_shape=[_sds((s, D_MODEL)), _sds((1, 1))],
        operands=(y, x, gate, w_out, target), vmem_mib=40)


def _out_bwd(dxo, y, gate, w_out, name, jobs=()):
    s = dxo.shape[0]
    ts = min(512, s)
    steps = s // ts

    def body(dx_ref, y_ref, g_ref, w_ref, dy_ref, gw_ref, dg_ref):
        @pl.when(pl.program_id(0) == 0)
        def _():
            gw_ref[...] = jnp.zeros_like(gw_ref)

        dx = dx_ref[...]
        dy_ref[...] = lax.dot_general((dx * g_ref[...]).astype(BF16), w_ref[...], NT_DIMS, preferred_element_type=F32)
        gw_ref[...] += lax.dot_general(y_ref[...], dx.astype(BF16), TN_DIMS, preferred_element_type=F32)

        @pl.when(pl.program_id(0) == steps - 1)
        def _():
            m = gw_ref[...]
            dg_ref[...] = jnp.sum(w_ref[...].astype(F32) * m, axis=0, keepdims=True)
            gw_ref[...] = m * g_ref[...]

    row = pl.BlockSpec((ts, D_MODEL), lambda i: (i, 0))
    return _pallas(
        body, name=name, grid=(s // ts,), in_specs=[row, row, _full((1, D_MODEL)), _full((D_MODEL, D_MODEL))],
        out_specs=[row, _full((D_MODEL, D_MODEL)), _full((1, D_MODEL))],
        out_shape=[_sds((s, D_MODEL)), _sds((D_MODEL, D_MODEL)), _sds((1, D_MODEL))],
        operands=(dxo, y, gate, w_out), vmem_mib=48, jobs=jobs)


def _mix_bwd(dy, proj, probs, psink, attn, sv, qg2, kg2, wst, name, jobs=()):
    s = proj.shape[0]
    nt = s // TILE

    def body(dy_ref, cur_ref, pkv_ref, nkv_ref, p_ref, ps_ref, attn_ref, sv_ref, qg_ref, kg_ref, wst_ref,
             dpb_ref, dkv_ref, p0_ref, p2_ref, dqg_ref, dkg_ref, dsink_ref, dws_ref, dbsp_ref):
        def put(rows, col, value):
            dpb_ref[rows, col:col + BLOCK] = value.astype(BF16)

        @pl.when(pl.program_id(0) == 0)
        def _():
            dqg_ref[...] = jnp.zeros_like(dqg_ref)
            dkg_ref[...] = jnp.zeros_like(dkg_ref)
            dsink_ref[...] = jnp.zeros_like(dsink_ref)
            dws_ref[...] = jnp.zeros_like(dws_ref)
            dbsp_ref[...] = jnp.zeros_like(dbsp_ref)

        lo = _lo_mask((BLOCK, BLOCK))
        lo_kv = _lo_mask((TILE + 2 * BLOCK, BLOCK))
        lane_row = lax.broadcasted_iota(jnp.int32, (1, BLOCK), 1)
        qg = qg_ref[...]
        kg = kg_ref[...]

        k_all, v_all = _kv_rows(cur_ref, pkv_ref, nkv_ref)
        rk = _half_rms_scale(k_all, lo_kv)
        khat = k_all * rk
        kn_all = (khat * kg).astype(BF16)
        vb_all = v_all.astype(BF16)

        dkn_parts, dv_parts = [], []
        dsink = jnp.zeros((1, BLOCK), F32)
        dqg = jnp.zeros((1, BLOCK), F32)
        for b in range(SUB):
            rows = slice(BLOCK * b, BLOCK * (b + 1))
            window = slice(BLOCK * b, BLOCK * (b + 3))
            kn, vb = kn_all[window], vb_all[window]

            qhat, rq = [], []
            for p in range(4):
                q = cur_ref[rows, C_Q + BLOCK * p:C_Q + BLOCK * (p + 1)]
                r = _half_rms_scale(q, lo)
                rq.append(r)
                qhat.append(q * r)
            q_stack = _stack_heads([(qh * qg) * 0.125 for qh in qhat], lo).astype(BF16)
            pb = p_ref[STACK * b:STACK * (b + 1), :]
            prob = pb.astype(F32)

            dout = []
            for p in range(4):
                g = cur_ref[rows, C_GA + BLOCK * p:C_GA + BLOCK * (p + 1)]
                sg = _sigmoid(g)
                dya = dy_ref[rows, BLOCK * p:BLOCK * (p + 1)]
                attn = attn_ref[rows, BLOCK * p:BLOCK * (p + 1)]
                put(rows, C_GA + BLOCK * p, dya * attn * (sg * (1.0 + g * (1.0 - sg))))
                dout.append(dya * (g * sg))
            do_stack = _stack_heads(dout, lo).astype(BF16)
            dp = lax.dot_general(do_stack, vb, NT_DIMS, preferred_element_type=F32)
            delta = jnp.sum(prob * dp, axis=-1, keepdims=True)
            dsb = (prob * (dp - delta)).astype(BF16)

            csink = -(ps_ref[STACK * b:STACK * (b + 1), :] * delta)
            for h in range(N_HEADS):
                tot = jnp.sum(csink[BLOCK * h:BLOCK * (h + 1)], axis=0, keepdims=True)
                dsink = dsink + jnp.where(lane_row == h, tot, 0.0)

            dq_stack = jnp.dot(dsb, kn, preferred_element_type=F32) * 0.125
            dkn_parts.append(lax.dot_general(dsb, q_stack, TN_DIMS, preferred_element_type=F32))
            dv_parts.append(lax.dot_general(pb, do_stack, TN_DIMS, preferred_element_type=F32))

            for p in range(4):
                dqn = _unstack_pair(dq_stack, p, lo)
                qh = qhat[p]
                dqg = dqg + jnp.sum(dqn * qh, axis=0, keepdims=True)
                dqh = dqn * qg
                mean = _half_sum(dqh * qh, lo) * (1.0 / HEAD_DIM)
                put(rows, C_Q + BLOCK * p, rq[p] * (dqh - qh * mean))

        for p in range(4):
            rs, vnfs, vns, dsvs, dbs = [], [], [], [], None
            for b in range(SUB):
                rows = slice(BLOCK * b, BLOCK * (b + 1))
                vg = cur_ref[rows, C_VG + BLOCK * p:C_VG + BLOCK * (p + 1)]
                r = _half_rms_scale(vg, lo)
                vnf = vg * r
                sv = sv_ref[rows, BLOCK * p:BLOCK * (p + 1)]
                u = cur_ref[rows, C_U + BLOCK * p:C_U + BLOCK * (p + 1)]
                g = cur_ref[rows, C_GG + BLOCK * p:C_GG + BLOCK * (p + 1)]
                sg = _sigmoid(g)
                dym = dy_ref[rows, D_ATTN + BLOCK * p:D_ATTN + BLOCK * (p + 1)]
                put(rows, C_GG + BLOCK * p, dym * (u * sv) * (sg * (1.0 + g * (1.0 - sg))))
                dgm = dym * (g * sg)
                put(rows, C_U + BLOCK * p, dgm * sv)
                dsv = dgm * u
                term = jnp.where(lo, jnp.sum(jnp.where(lo, dsv, 0.0), axis=-1, keepdims=True),
                                 jnp.sum(jnp.where(lo, 0.0, dsv), axis=-1, keepdims=True))
                dbs = term if dbs is None else dbs + term
                rs.append(r)
                vnfs.append(vnf)
                vns.append(vnf.astype(BF16))
                dsvs.append(dsv)
            dbsp_ref[p] += dbs
            vn = jnp.concatenate(vns, axis=1)
            dsv = jnp.concatenate(dsvs, axis=1)
            lo_t = (lax.broadcasted_iota(jnp.int32, dsv.shape, 1) & (BLOCK - 1)) < HEAD_DIM
            dws_ref[2 * p] += lax.dot_general(jnp.where(lo_t, dsv, 0.0).astype(BF16), vn, NT_DIMS, preferred_element_type=F32)
            dws_ref[2 * p + 1] += lax.dot_general(jnp.where(lo_t, 0.0, dsv).astype(BF16), vn, NT_DIMS,
                                                  preferred_element_type=F32)
            dsvb = dsv.astype(BF16)
            dvn_a = jnp.dot(wst_ref[2 * p], dsvb, preferred_element_type=F32)
            dvn_b = jnp.dot(wst_ref[2 * p + 1], dsvb, preferred_element_type=F32)
            for b in range(SUB):
                lanes = slice(BLOCK * b, BLOCK * (b + 1))
                dvn = jnp.where(lo, dvn_a[:, lanes], dvn_b[:, lanes])
                mean = _half_sum(dvn * vnfs[b], lo) * (1.0 / HEAD_DIM)
                put(slice(BLOCK * b, BLOCK * (b + 1)), C_VG + BLOCK * p, rs[b] * (dvn - vnfs[b] * mean))

        dsink_ref[...] += dsink
        dqg = jnp.broadcast_to(dqg, (8, BLOCK))
        dqg_ref[...] += dqg + pltpu.roll(dqg, HEAD_DIM, 1)

        dkn = _overlap_add(dkn_parts)
        dv = _overlap_add(dv_parts)
        dkg = jnp.broadcast_to(jnp.sum(dkn * khat, axis=0, keepdims=True), (8, BLOCK))
        dkg_ref[...] += dkg + pltpu.roll(dkg, HEAD_DIM, 1)
        dkh = dkn * kg
        dk = rk * (dkh - khat * (_half_sum(dkh * khat, lo_kv) * (1.0 / HEAD_DIM)))
        dpb_ref[:, C_K:C_GA] = jnp.zeros((TILE, 2 * D_KV), BF16)
        dkv_ref[:, 0:D_KV] = dk[BLOCK:BLOCK + TILE]
        dkv_ref[:, D_KV:2 * D_KV] = dv[BLOCK:BLOCK + TILE]
        p0_ref[:, 0:D_KV] = dk[0:BLOCK]
        p0_ref[:, D_KV:2 * D_KV] = dv[0:BLOCK]
        p2_ref[:, 0:D_KV] = dk[BLOCK + TILE:]
        p2_ref[:, D_KV:2 * D_KV] = dv[BLOCK + TILE:]

    cur, pkv, nkv, _ = _mix_specs(nt)
    kv_blk = (BLOCK, 2 * D_KV)
    half = pl.BlockSpec((TILE, D_ATTN), lambda i: (i, 0))
    return _pallas(
        body, name=name, grid=(nt,),
        in_specs=[pl.BlockSpec((TILE, D_MODEL), lambda i: (i, 0)), cur, pkv, nkv,
                  pl.BlockSpec((SUB * STACK, 3 * BLOCK), lambda i: (i, 0)), pl.BlockSpec((SUB * STACK, BLOCK), lambda i: (i, 0)),
                  half, half, _full((1, BLOCK)), _full((1, BLOCK)), _full((8, BLOCK, BLOCK))],
        out_specs=[cur, pl.BlockSpec((TILE, 2 * D_KV), lambda i: (i, 0)),
                   pl.BlockSpec(kv_blk, lambda i: ((i + nt - 1) % nt, 0)),
                   pl.BlockSpec(kv_blk, lambda i: ((i + 1) % nt, 0)),
                   _full((8, BLOCK)), _full((8, BLOCK)), _full((1, BLOCK)),
                   _full((8, BLOCK, BLOCK)), _full((4, BLOCK, BLOCK))],
        out_shape=[_sds((s, D_IN), BF16), _sds((s, 2 * D_KV)), _sds((nt * BLOCK, 2 * D_KV)), _sds((nt * BLOCK, 2 * D_KV)),
                   _sds((8, BLOCK)), _sds((8, BLOCK)), _sds((1, BLOCK)),
                   _sds((8, BLOCK, BLOCK)), _sds((4, BLOCK, BLOCK))],
        operands=(dy, proj, proj, proj, probs, psink, attn, sv, qg2, kg2, wst), vmem_mib=56, jobs=jobs)


def _w_in_grad(dpb, dkv, p0, p2, x, ng, scale, shift, name, jobs=()):
    s = x.shape[0]
    ts = min(2 * TILE, s)
    tiles = ts // TILE

    def body(dpb_ref, dkv_ref, p0_ref, p2_ref, x_ref, ng_ref, sc_ref, sh_ref, gw_ref, dkvb_ref):
        @pl.when(pl.program_id(0) == 0)
        def _():
            gw_ref[...] = jnp.zeros_like(gw_ref)

        xv = x_ref[...]
        r = lax.rsqrt(jnp.mean(xv * xv, axis=-1, keepdims=True) + EPS)
        h = (((xv * r) * ng_ref[...]) * (1.0 + sc_ref[...]) + sh_ref[...]).astype(BF16)
        for t in range(tiles):
            halo = slice(BLOCK * t, BLOCK * (t + 1))
            first = slice(TILE * t, TILE * t + BLOCK)
            last = slice(TILE * (t + 1) - BLOCK, TILE * (t + 1))
            dkvb_ref[first, :] = (dkv_ref[first, :] + p2_ref[halo, :]).astype(BF16)
            if SUB > 2:
                inner = slice(TILE * t + BLOCK, TILE * (t + 1) - BLOCK)
                dkvb_ref[inner, :] = dkv_ref[inner, :].astype(BF16)
            dkvb_ref[last, :] = (dkv_ref[last, :] + p0_ref[halo, :]).astype(BF16)
        gw_ref[...] += lax.dot_general(dpb_ref[...], h, TN_DIMS, preferred_element_type=F32)
        gw_ref[C_K:C_GA, :] += lax.dot_general(dkvb_ref[...], h, TN_DIMS, preferred_element_type=F32)

    kv = pl.BlockSpec((ts, 2 * D_KV), lambda i: (i, 0))
    halo = pl.BlockSpec((tiles * BLOCK, 2 * D_KV), lambda i: (i, 0))
    vec = _full((1, D_MODEL))
    return _pallas(
        body, name=name, grid=(s // ts,),
        in_specs=[pl.BlockSpec((ts, D_IN), lambda i: (i, 0)), kv, halo, halo,
                  pl.BlockSpec((ts, D_MODEL), lambda i: (i, 0)), vec, vec, vec],
        out_specs=[_full((D_IN, D_MODEL)), kv], out_shape=[_sds((D_IN, D_MODEL)), _sds((s, 2 * D_KV), BF16)],
        operands=(dpb, dkv, p0, p2, x, ng, scale, shift), vmem_mib=56, jobs=jobs)


def _proj_bwd(dpb, dkvb, x, dxo, ng, scale, wt, name, jobs=()):
    s = x.shape[0]
    ts = min(512, s)

    def body(dpb_ref, dkvb_ref, x_ref, dxo_ref, ng_ref, sc_ref, w_ref, dxi_ref, dsh_ref, dsc_ref, dng_ref):
        @pl.when(pl.program_id(0) == 0)
        def _():
            dsh_ref[...] = jnp.zeros_like(dsh_ref)
            dsc_ref[...] = jnp.zeros_like(dsc_ref)
            dng_ref[...] = jnp.zeros_like(dng_ref)

        dh = (jnp.dot(dpb_ref[...], w_ref[...], preferred_element_type=F32)
              + jnp.dot(dkvb_ref[...], w_ref[C_K:C_GA, :], preferred_element_type=F32))

        xv = x_ref[...]
        r = lax.rsqrt(jnp.mean(xv * xv, axis=-1, keepdims=True) + EPS)
        xn = xv * r
        ngv = ng_ref[...]
        sc1 = 1.0 + sc_ref[...]
        dsh_ref[...] += jnp.sum(dh, axis=0, keepdims=True)
        dsc_ref[...] += jnp.sum(dh * (xn * ngv), axis=0, keepdims=True)
        dh1 = dh * sc1
        dng_ref[...] += jnp.sum(dh1 * xn, axis=0, keepdims=True)
        dxn = dh1 * ngv
        dxi_ref[...] = r * (dxn - xn * jnp.mean(dxn * xn, axis=-1, keepdims=True)) + dxo_ref[...]

    row = pl.BlockSpec((ts, D_MODEL), lambda i: (i, 0))
    vec = _full((1, D_MODEL))
    return _pallas(
        body, name=name, grid=(s // ts,),
        in_specs=[pl.BlockSpec((ts, D_IN), lambda i: (i, 0)), pl.BlockSpec((ts, 2 * D_KV), lambda i: (i, 0)),
                  row, row, vec, vec, _full((D_IN, D_MODEL))],
        out_specs=[row, vec, vec, vec],
        out_shape=[_sds((s, D_MODEL)), _sds((1, D_MODEL)), _sds((1, D_MODEL)), _sds((1, D_MODEL))],
        operands=(dpb, dkvb, x, dxo, ng, scale, wt), vmem_mib=48, jobs=jobs)


def _pair_sum(g, r, c_idx, name, send_dtype=None):
    _, rows, cols = g.shape
    half = rows // 2

    def body(c_ref, g_ref, r_ref, o_ref, *narrow):
        total = g_ref[...] + r_ref[...]
        o_ref[...] = total
        for n_ref in narrow:
            n_ref[...] = total.astype(n_ref.dtype)

    blk = (None, half, cols)
    out_blk = pl.BlockSpec(blk, lambda j, c: (j, 0, 0))
    shapes = [_sds((N_CHIPS, half, cols))] + ([_sds((N_CHIPS, half, cols), send_dtype)] if send_dtype else [])
    return pl.pallas_call(
        body, name=name,
        grid_spec=pltpu.PrefetchScalarGridSpec(
            num_scalar_prefetch=1, grid=(N_CHIPS,),
            in_specs=[pl.BlockSpec(blk, lambda j, c: (j, c[0], 0)), out_blk], out_specs=[out_blk] * len(shapes)),
        out_shape=shapes,
        compiler_params=pltpu.CompilerParams(dimension_semantics=("arbitrary",), vmem_limit_bytes=32 * MIB),
    )(c_idx, g, r)


def _chip_sum(p, r, place, name):
    _, rows, cols = p.shape
    tr = rows // 2

    def body(j_ref, p_ref, r_ref, o_ref):
        o_ref[...] = ((p_ref[...] + r_ref[0].astype(F32)) + r_ref[1].astype(F32)) + r_ref[2].astype(F32)

    return pl.pallas_call(
        body, name=name,
        grid_spec=pltpu.PrefetchScalarGridSpec(
            num_scalar_prefetch=1, grid=(2,),
            in_specs=[pl.BlockSpec((None, tr, cols), lambda t, j: (j[0], t, 0)),
                      pl.BlockSpec((3, tr, cols), lambda t, j: (0, t, 0))],
            out_specs=pl.BlockSpec((tr, cols), lambda t, j: (2 * j[1] + t, 0))),
        out_shape=_sds((2 * rows, cols)),
        compiler_params=pltpu.CompilerParams(dimension_semantics=("arbitrary",), vmem_limit_bytes=32 * MIB),
    )(place, p, r)


def _cast_permute_w_in(wt, place_chunks):
    def body(t_ref, w_ref, w0_ref, w1_ref):
        def cast_into(o_ref):
            for t in range(N_CHUNKS):
                src = pl.multiple_of(t_ref[1 + t] * CHUNK_ROWS, CHUNK_ROWS)
                o_ref[CHUNK_ROWS * t:CHUNK_ROWS * (t + 1), :] = w_ref[pl.ds(src, CHUNK_ROWS), :].astype(BF16)

        @pl.when(pl.program_id(0) == 0)
        def _():
            cast_into(w0_ref)

        @pl.when(pl.program_id(0) == 1)
        def _():
            cast_into(w1_ref)

    return pl.pallas_call(
        body, name="cast_permute_w_in",
        grid_spec=pltpu.PrefetchScalarGridSpec(
            num_scalar_prefetch=1, grid=(DEPTH,),
            in_specs=[pl.BlockSpec((None, W_IN_BLK, D_MODEL), lambda l, tbl: (l, 0, 0))],
            out_specs=[pl.BlockSpec((None, W_IN_BLK, D_MODEL), lambda l, tbl: (tbl[0], 0, 0)),
                       pl.BlockSpec((W_IN_BLK, D_MODEL), lambda l, tbl: (0, 0))]),
        out_shape=[_sds((N_CHIPS, W_IN_BLK, D_MODEL), BF16), _sds((W_IN_BLK, D_MODEL), BF16)],
        compiler_params=pltpu.CompilerParams(dimension_semantics=("arbitrary",), vmem_limit_bytes=32 * MIB),
    )(place_chunks, wt)


def _gather_inputs(c, w_out, w0):
    half = W_IN_BLK // 2

    def body(c_ref, wout_ref, mine_ref, call_ref, woutb_ref, w0_ref, send_sems, recv_sems):
        x, y, cc = _coords()
        j = 2 * x + y
        b = 2 * j + cc
        sib = (x, y, 1 - cc)
        woutb_ref[...] = wout_ref[...].astype(BF16)
        call_ref[b] = c_ref[...]
        chips = _other_chips(x, y)

        def sems(k):
            return send_sems.at[k], recv_sems.at[k]

        def half_rows(chip_index):
            return w0_ref.at[chip_index, pl.ds(cc * half, half), :]

        first = [_remote(mine_ref.at[j, pl.ds(cc * half, half), :], half_rows(j), sems(k), (*chip, cc))
                 for k, chip in enumerate(chips)]
        k = 3
        rest = []
        for fx in (0, 1):
            for fy in (0, 1):
                for fc in (0, 1):
                    if fx or fy or fc:
                        dev = (1 - x if fx else x, 1 - y if fy else y, 1 - cc if fc else cc)
                        rest.append(_remote(call_ref.at[b], call_ref.at[b], sems(k), dev))
                        k += 1
        for cp in first + rest:
            cp.start()
        passed = []
        for k, chip in enumerate(chips):
            jk = 2 * chip[0] + chip[1]
            first[k].wait_recv()
            passed.append(_remote(half_rows(jk), half_rows(jk), sems(10 + k), sib))
            passed[k].start()
        for cp in first:
            cp.wait_send()
        for cp in rest + passed:
            cp.wait()

    return pl.pallas_call(
        body, name="gather_inputs", in_specs=[VMEM, VMEM, ANY], out_specs=[VMEM, VMEM, ANY],
        out_shape=[_sds((N_DEV, 1, D_MODEL)), _sds((DEPTH, W_OUT_BLK, D_MODEL), BF16),
                   _sds((N_CHIPS, W_IN_BLK, D_MODEL), BF16)],
        scratch_shapes=[pltpu.SemaphoreType.DMA((13,)), pltpu.SemaphoreType.DMA((13,))],
        input_output_aliases={2: 2},
        compiler_params=pltpu.CompilerParams(vmem_limit_bytes=32 * MIB),
    )(c, w_out, w0)


def _ada_rows(c_all, w_ada, b_blk):
    def body(c_ref, w_ref, b_ref, o_ref, cond_ref):
        cv = c_ref[...]
        cond = (cv * _sigmoid(cv)).astype(BF16)
        cond_ref[...] = cond.astype(F32)
        for l in range(DEPTH):
            o_ref[:, l, :] = jnp.dot(cond, w_ref[l].astype(BF16), preferred_element_type=F32) + b_ref[l:l + 1, :]

    return pl.pallas_call(
        body, name="ada_rows", in_specs=[VMEM, VMEM, VMEM], out_specs=[VMEM, VMEM],
        out_shape=[_sds((N_DEV, DEPTH, W_ADA_BLK)), _sds((N_DEV, D_MODEL))],
        compiler_params=pltpu.CompilerParams(vmem_limit_bytes=32 * MIB),
    )(c_all, w_ada, b_blk)


def _exchange_ada(part):
    def body(part_ref, out_ref, send_sems, recv_sems):
        x, y, cc = _coords()
        j = 2 * x + y
        out_ref[j] = part_ref[2 * j + cc]
        copies = []
        for k, chip in enumerate(_other_chips(x, y)):
            b_dst = 4 * chip[0] + 2 * chip[1] + cc
            copies.append(_remote(part_ref.at[b_dst], out_ref.at[j], (send_sems.at[k], recv_sems.at[k]), (*chip, cc)))
        for cp in copies:
            cp.start()
        for cp in copies:
            cp.wait()

    return pl.pallas_call(
        body, name="exchange_ada", in_specs=[VMEM], out_specs=VMEM,
        out_shape=_sds((N_CHIPS, DEPTH, W_ADA_BLK)),
        scratch_shapes=[pltpu.SemaphoreType.DMA((3,)), pltpu.SemaphoreType.DMA((3,))],
    )(part)


def _all_gather_rows(blk):
    m_per, n = blk.shape

    def body(x_ref, out_ref, send_sems, recv_sems, local_sem):
        x, y, c = _coords()
        me, sibling = (x, y, c), (x, y, 1 - c)
        chips = _other_chips(x, y)

        def rows(px, py, pc):
            return out_ref.at[pl.ds((4 * px + 2 * py + pc) * m_per, m_per), :]

        def copy(k, block, to, src=None):
            return _remote(rows(*block) if src is None else src, rows(*block), (send_sems.at[k], recv_sems.at[k]), to)

        mine = pltpu.make_async_copy(x_ref, rows(*me), local_sem)
        mine.start()
        first = [copy(0, me, sibling, src=x_ref)]
        first += [copy(1 + j, me, (*chip, c), src=x_ref) for j, chip in enumerate(chips)]
        for cp in first:
            cp.start()
        passed = [copy(4 + j, (*chip, c), sibling) for j, chip in enumerate(chips)]
        for j, chip in enumerate(chips):
            copy(1 + j, (*chip, c), me).wait_recv()
            passed[j].start()
        copy(0, sibling, me).wait_recv()
        for j, chip in enumerate(chips):
            copy(4 + j, (*chip, 1 - c), me).wait_recv()
        for cp in first + passed:
            cp.wait_send()
        mine.wait()

    return pl.pallas_call(
        body, name="all_gather_small", in_specs=[VMEM], out_specs=VMEM,
        out_shape=_sds((N_DEV * m_per, n), blk.dtype),
        scratch_shapes=[pltpu.SemaphoreType.DMA((7,)), pltpu.SemaphoreType.DMA((7,)), pltpu.SemaphoreType.DMA],
        compiler_params=pltpu.CompilerParams(vmem_limit_bytes=32 * MIB),
    )(blk)


def _adamw_math(w, g, m, v):
    m = ADAM_B1 * m + (1.0 - ADAM_B1) * g
    v = ADAM_B2 * v + (1.0 - ADAM_B2) * (g * g)
    m_hat = m / (1.0 - ADAM_B1 ** ADAM_STEP)
    v_hat = v / (1.0 - ADAM_B2 ** ADAM_STEP)
    delta = -ADAM_LR * (m_hat / (jnp.sqrt(v_hat) + ADAM_EPS) + ADAM_WD * w)
    return delta, m, v


def _adamw_w_in(w, g0, g1, m, v, pos_chunks):
    def body(t_ref, w_ref, g0_ref, g1_ref, m_ref, v_ref, g_ref, d_ref, nm_ref, nv_ref):
        for l, src in enumerate((g0_ref, g1_ref)):
            g = src[...]
            g_ref[l] = g
            d_ref[l], nm_ref[l], nv_ref[l] = _adamw_math(w_ref[l], g, m_ref[l], v_ref[l])

    nat = pl.BlockSpec((DEPTH, CHUNK_ROWS, D_MODEL), lambda t, tbl: (0, t, 0))
    per = pl.BlockSpec((CHUNK_ROWS, D_MODEL), lambda t, tbl: (tbl[t], 0))
    return pl.pallas_call(
        body, name="adamw_w_in",
        grid_spec=pltpu.PrefetchScalarGridSpec(num_scalar_prefetch=1, grid=(N_CHUNKS,),
                                               in_specs=[nat, per, per, nat, nat], out_specs=[nat] * 4),
        out_shape=[_sds(w.shape)] * 4,
        compiler_params=pltpu.CompilerParams(dimension_semantics=("arbitrary",)),
    )(pos_chunks, w, g0, g1, m, v)


def _adamw_w_out(w, g0, g1, m, v):
    def body(w_ref, g0_ref, g1_ref, m_ref, v_ref, g_ref, d_ref, nm_ref, nv_ref):
        g = jnp.where(pl.program_id(0) == 0, g0_ref[...], g1_ref[...])
        g_ref[...] = g
        d_ref[...], nm_ref[...], nv_ref[...] = _adamw_math(w_ref[...], g, m_ref[...], v_ref[...])

    blk = pl.BlockSpec((None, W_OUT_BLK, D_MODEL), lambda l: (l, 0, 0))
    gblk = _full((W_OUT_BLK, D_MODEL))
    return pl.pallas_call(
        body, name="adamw_w_out", grid=(DEPTH,), in_specs=[blk, gblk, gblk, blk, blk], out_specs=[blk] * 4,
        out_shape=[_sds(w.shape)] * 4,
        compiler_params=pltpu.CompilerParams(dimension_semantics=("arbitrary",), vmem_limit_bytes=32 * MIB),
    )(w, g0, g1, m, v)


def _w_ada_grad_adamw(cond_t, dada, w, m, v):
    _, rows, cols = w.shape
    tr = 256

    def body(ct_ref, da_ref, w_ref, m_ref, v_ref, g_ref, d_ref, nm_ref, nv_ref):
        g = jnp.dot(ct_ref[...], da_ref[...].astype(BF16), preferred_element_type=F32)
        g_ref[...] = g
        d_ref[...], nm_ref[...], nv_ref[...] = _adamw_math(w_ref[...], g, m_ref[...], v_ref[...])

    blk = pl.BlockSpec((None, tr, cols), lambda l, t: (l, t, 0))
    return pl.pallas_call(
        body, name="w_ada_grad_adamw", grid=(DEPTH, rows // tr),
        in_specs=[pl.BlockSpec((tr, BLOCK), lambda l, t: (t, 0)), pl.BlockSpec((None, BLOCK, cols), lambda l, t: (l, 0, 0)),
                  blk, blk, blk],
        out_specs=[blk] * 4, out_shape=[_sds(w.shape)] * 4,
        compiler_params=pltpu.CompilerParams(dimension_semantics=("arbitrary", "arbitrary"), vmem_limit_bytes=32 * MIB),
    )(cond_t, dada, w, m, v)


def _small_sum_adamw(gathered_a, gathered_b, w, m, v):
    def body(a_ref, b_ref, w_ref, m_ref, v_ref, g_ref, d_ref, nm_ref, nv_ref):
        def total(ref):
            g = ref[0]
            for b in range(1, N_DEV):
                g = g + ref[b]
            return g

        g = jnp.concatenate([total(a_ref), total(b_ref)], axis=0)
        g_ref[...] = g
        d_ref[...], nm_ref[...], nv_ref[...] = _adamw_math(w_ref[...], g, m_ref[...], v_ref[...])

    return pl.pallas_call(
        body, name="small_sum_adamw", in_specs=[VMEM] * 5, out_specs=[VMEM] * 4, out_shape=[_sds(w.shape)] * 4,
        compiler_params=pltpu.CompilerParams(vmem_limit_bytes=48 * MIB),
    )(gathered_a, gathered_b, w, m, v)


_SMALL_A = (("w_s", DEPTH * 8 * BLOCK), ("b_s", DEPTH * 8), ("q_gain", 1), ("k_gain", 1), ("sink", 1))
_SMALL_B = (("b_ada", DEPTH * 24), ("norm_gain", DEPTH * 8), ("sq_err", 1))


def _pack_rows(parts, layout):
    rows = []
    for name, n in layout:
        flat = parts[name].reshape(-1)
        rows.append(jnp.pad(flat, (0, n * 128 - flat.shape[0])).reshape(n, 128))
    n_rows = sum(n for _, n in layout)
    if n_rows % 8:
        rows.append(jnp.zeros((-n_rows % 8, 128), F32))
    return jnp.concatenate(rows, axis=0)


def _pack_small(parts):
    return jnp.concatenate([_pack_rows(parts, _SMALL_A), _pack_rows(parts, _SMALL_B)], axis=0)


def _unpack_small(packed, shapes):
    out, r0 = {}, 0
    for layout in (_SMALL_A, _SMALL_B):
        for name, n in layout:
            size = 1
            for d in shapes[name]:
                size *= d
            out[name] = packed[r0:r0 + n].reshape(-1)[:size].reshape(shapes[name])
            r0 += n
        r0 += -r0 % 8
    return out


def _permute_heads(a, axis):
    shp = a.shape
    a = a.reshape(shp[:axis] + (2, 4, HEAD_DIM) + shp[axis + 1:])
    a = jnp.swapaxes(a, axis, axis + 1)
    return a.reshape(shp)


def _unpermute_heads(a, axis):
    shp = a.shape
    a = a.reshape(shp[:axis] + (4, 2, HEAD_DIM) + shp[axis + 1:])
    a = jnp.swapaxes(a, axis, axis + 1)
    return a.reshape(shp)


def _permute_w_out(w):
    return jnp.concatenate([_permute_heads(w[:D_ATTN], 0), w[D_ATTN:]], axis=0)


def _unpermute_w_out(w):
    return jnp.concatenate([_unpermute_heads(w[:D_ATTN], 0), w[D_ATTN:]], axis=0)


def kernel(x, c, w_ada, b_ada, norm_gain, w_in, q_gain, k_gain, sink, w_s, b_s, w_out, loss_target, m_w_ada, m_b_ada, m_norm_gain, m_w_in, m_q_gain, m_k_gain, m_sink, m_w_s, m_b_s, m_w_out, v_w_ada, v_b_ada, v_norm_gain, v_w_in, v_q_gain, v_k_gain, v_sink, v_w_s, v_b_s, v_w_out):
    ix, iy, ic = _coords()
    chip = 2 * ix + iy
    chip_idx = jnp.stack([chip, ic]).astype(jnp.int32)
    core_idx = jnp.reshape(ic, (1,)).astype(jnp.int32)
    src_chunks = lax.dynamic_index_in_dim(jnp.asarray(_CHUNK_SRC), chip, 0, keepdims=False)
    pos_chunks = lax.dynamic_index_in_dim(jnp.asarray(_CHUNK_POS), chip, 0, keepdims=False)
    x0, target = x[0], loss_target[0]

    wt, mt, vt = (jnp.swapaxes(a, 1, 2) for a in (w_in, m_w_in, v_w_in))
    w0_mine, wloc_in1 = _cast_permute_w_in(wt, jnp.concatenate([chip_idx[:1], src_chunks]))
    c_all, wloc_out, w0 = _gather_inputs(c, w_out, w0_mine)
    wts = [w0.reshape(D_IN, D_MODEL), None]

    b_blk = lax.dynamic_slice_in_dim(b_ada, chip * W_ADA_BLK, W_ADA_BLK, axis=1)
    ada_part, cond = _ada_rows(c_all.reshape(N_DEV, D_MODEL), w_ada, b_blk)
    ada = jnp.moveaxis(_exchange_ada(ada_part), 0, 1).reshape(DEPTH, 3 * D_MODEL)
    shift = [ada[l:l + 1, 0:D_MODEL] for l in range(DEPTH)]
    scale = [ada[l:l + 1, D_MODEL:2 * D_MODEL] for l in range(DEPTH)]
    gate = [ada[l:l + 1, 2 * D_MODEL:] for l in range(DEPTH)]
    ng = [norm_gain[l:l + 1] for l in range(DEPTH)]

    qg2 = jnp.concatenate([q_gain, q_gain], axis=-1)
    kg2 = jnp.concatenate([k_gain, k_gain], axis=-1)
    ws_b = w_s.astype(BF16)
    wst_b = jnp.swapaxes(w_s, -1, -2).astype(BF16)
    bsp = jnp.repeat(jnp.swapaxes(b_s.reshape(DEPTH, 4, 2, BLOCK), -1, -2), HEAD_DIM, axis=-1)
    bias = jnp.asarray(_bias_table())

    def mix_args(l):
        return bias, sink[l], qg2[l:l + 1], kg2[l:l + 1], ws_b[l]

    w_out_shape = (W_OUT_BLK, D_MODEL)
    proj0, w1 = _proj_fwd(x0, ng[0], scale[0], shift[0], wts[0], "proj_fwd_0",
                          jobs=[_job_gather([(wloc_in1, None)], [(W_IN_BLK, D_MODEL)])])
    y0, *kept0, wo0, wo1 = _mix_fwd(proj0, *mix_args(0), bsp[0], "mix_fwd_0",
                            jobs=[_job_gather([(wloc_out, 0), (wloc_out, 1)], [w_out_shape, w_out_shape])])
    wts[1] = w1.reshape(D_IN, D_MODEL)
    wos = [_permute_w_out(w.reshape(D_MODEL, D_MODEL)) for w in (wo0, wo1)]
    x1, = _out_fwd(y0, x0, gate[0], wos[0], "out_fwd_0")
    proj1, = _proj_fwd(x1, ng[1], scale[1], shift[1], wts[1], "proj_fwd_1")
    y1, *kept1 = _mix_fwd(proj1, *mix_args(1), bsp[1], "mix_fwd_1")
    dx2, sq = _out_fwd_loss(y1, x1, gate[1], wos[1], target, "out_fwd_loss_1")

    def blocks_out(gw):
        return _unpermute_w_out(gw).reshape(N_CHIPS, W_OUT_BLK, D_MODEL)

    dy1, gwo1, dgate1 = _out_bwd(dx2, y1, gate[1], wos[1], "out_bwd_1")
    go1 = blocks_out(gwo1)
    dpb, dkv, p0, p2, dqg1, dkg1, dsink1, dws1, dbsp1, ro1 = _mix_bwd(
        dy1, proj1, *kept1, qg2[1:2], kg2[1:2], wst_b[1], "mix_bwd_1", jobs=[_job_swap(go1)])
    po1, = _pair_sum(go1, ro1, core_idx, "pair_sum_w_out_1")
    gwi1, dkvb, co1 = _w_in_grad(dpb, dkv, p0, p2, x1, ng[1], scale[1], shift[1], "w_in_grad_1", jobs=[_job_scatter(po1)])
    gi1 = gwi1.reshape(N_CHIPS, W_IN_BLK, D_MODEL)
    fo1 = _chip_sum(po1, co1, chip_idx, "chip_sum_w_out_1")
    dx1, dsh1, dsc1, dng1, grad_wo1, ri1 = _proj_bwd(dpb, dkvb, x1, dx2, ng[1], scale[1], wts[1], "proj_bwd_1",
                                                     jobs=[_job_join(fo1), _job_swap(gi1)])
    pi1, = _pair_sum(gi1, ri1, core_idx, "pair_sum_w_in_1")

    dy0, gwo0, dgate0 = _out_bwd(dx1, y0, gate[0], wos[0], "out_bwd_0")
    go0 = blocks_out(gwo0)
    dpb, dkv, p0, p2, dqg0, dkg0, dsink0, dws0, dbsp0, ci1, ro0 = _mix_bwd(
        dy0, proj0, *kept0, qg2[0:1], kg2[0:1], wst_b[0], "mix_bwd_0", jobs=[_job_scatter(pi1), _job_swap(go0)])
    fi1 = _chip_sum(pi1, ci1, chip_idx, "chip_sum_w_in_1")
    po0, = _pair_sum(go0, ro0, core_idx, "pair_sum_w_out_0")

    def bs_grad(dbsp):
        return jnp.swapaxes(dbsp[:, :, ::HEAD_DIM], -1, -2).reshape(8, BLOCK)

    small_g = dict(
        w_s=jnp.stack([dws0, dws1]), b_s=jnp.stack([bs_grad(dbsp0), bs_grad(dbsp1)]),
        q_gain=jnp.stack([dqg0[0, :HEAD_DIM], dqg1[0, :HEAD_DIM]]), k_gain=jnp.stack([dkg0[0, :HEAD_DIM], dkg1[0, :HEAD_DIM]]),
        sink=jnp.stack([dsink0[0, :N_HEADS], dsink1[0, :N_HEADS]]))
    gwi0, dkvb, grad_wi1, co0, gathered_a = _w_in_grad(
        dpb, dkv, p0, p2, x0, ng[0], scale[0], shift[0], "w_in_grad_0",
        jobs=[_job_join(fi1), _job_scatter(po0), _job_all_gather(_pack_rows(small_g, _SMALL_A))])
    gi0 = gwi0.reshape(N_CHIPS, W_IN_BLK, D_MODEL)
    fo0 = _chip_sum(po0, co0, chip_idx, "chip_sum_w_out_0")

    ri0, grad_wo0 = _comm([_job_swap(gi0), _job_join(fo0)], "swap_w_in_0")
    pi0, pi0_send = _pair_sum(gi0, ri0, core_idx, "pair_sum_w_in_0", send_dtype=BF16)
    dx0, dsh0, dsc0, dng0, ci0 = _proj_bwd(dpb, dkvb, x0, dx1, ng[0], scale[0], wts[0], "proj_bwd_0",
                                           jobs=[_job_scatter(pi0_send)])
    fi0 = _chip_sum(pi0, ci0, chip_idx, "chip_sum_w_in_0")
    grad_wi0, = _comm([_job_join(fi0)], "join_w_in_0")

    small_g.update(
        b_ada=jnp.stack([jnp.concatenate([dsh0, dsc0, dgate0], axis=-1)[0], jnp.concatenate([dsh1, dsc1, dgate1], axis=-1)[0]]),
        norm_gain=jnp.stack([dng0[0], dng1[0]]), sq_err=sq[0])
    none = jnp.zeros((1,), F32)
    small_w = dict(w_s=w_s, b_s=b_s, b_ada=b_ada, norm_gain=norm_gain, q_gain=q_gain, k_gain=k_gain, sink=sink, sq_err=none)
    small_m = dict(w_s=m_w_s, b_s=m_b_s, b_ada=m_b_ada, norm_gain=m_norm_gain, q_gain=m_q_gain, k_gain=m_k_gain, sink=m_sink,
                   sq_err=none)
    small_v = dict(w_s=v_w_s, b_s=v_b_s, b_ada=v_b_ada, norm_gain=v_norm_gain, q_gain=v_q_gain, k_gain=v_k_gain, sink=v_sink,
                   sq_err=none)
    gathered_b = _all_gather_rows(_pack_rows(small_g, _SMALL_B))
    packed = _small_sum_adamw(gathered_a.reshape(N_DEV, -1, 128), gathered_b.reshape(N_DEV, -1, 128),
                              _pack_small(small_w), _pack_small(small_m), _pack_small(small_v))
    shapes = {k: a.shape for k, a in small_w.items()}
    sg, sd, sm, sv = (_unpack_small(p, shapes) for p in packed)
    loss = 0.5 * sg["sq_err"][0]

    dada_all = gathered_b.reshape(N_DEV, -1, 128)[:, 0:DEPTH * 24].reshape(N_DEV, DEPTH, 3 * D_MODEL)
    dada_blk = jnp.moveaxis(lax.dynamic_slice_in_dim(dada_all, chip * W_ADA_BLK, W_ADA_BLK, axis=2), 0, 1)
    pad = BLOCK - N_DEV
    ada_out = _w_ada_grad_adamw(
        jnp.pad(cond.T, ((0, 0), (0, pad))).astype(BF16), jnp.pad(dada_blk, ((0, 0), (0, pad), (0, 0))),
        w_ada, m_w_ada, v_w_ada)

    in_out = [jnp.swapaxes(a, 1, 2) for a in _adamw_w_in(wt, grad_wi0, grad_wi1, mt, vt, pos_chunks)]
    out_out = _adamw_w_out(w_out, grad_wo0, grad_wo1, m_w_out, v_w_out)

    def ordered(k):
        small = (sg, sd, sm, sv)[k]
        return (ada_out[k], small["b_ada"], small["norm_gain"], in_out[k], small["q_gain"], small["k_gain"], small["sink"],
                small["w_s"], small["b_s"], out_out[k])

    return (loss, dx0[None], *ordered(0), *ordered(1), *ordered(2), *ordered(3))
```

```python
import numpy as np

import jax
import jax.numpy as jnp
from jax import lax
from jax.experimental import pallas as pl
from jax.experimental.pallas import tpu as pltpu

F32 = jnp.float32
BF16 = jnp.bfloat16

D_MODEL = 1024
DEPTH = 2
HEAD_DIM = 64
N_HEADS = 8
BLOCK = 128
SUB = 4
TILE = SUB * BLOCK
STACK = 8 * BLOCK
D_ATTN = 512
D_KV = 128
D_IN = 2816
N_CHIPS = 4
N_DEV = 8
W_IN_BLK = D_IN // N_CHIPS
W_OUT_BLK = D_MODEL // N_CHIPS
W_ADA_BLK = 3 * D_MODEL // N_CHIPS
CHUNK_ROWS = HEAD_DIM
N_CHUNKS = W_IN_BLK // CHUNK_ROWS
EPS = 1e-6
NEG_INF = -1e30

C_Q, C_K, C_V, C_GA, C_U, C_VG, C_GG = 0, 512, 640, 768, 1280, 1792, 2304

ADAM_LR = 0.001
ADAM_B1 = 0.9
ADAM_B2 = 0.999
ADAM_EPS = 1e-08
ADAM_WD = 0.01
ADAM_STEP = 10

MESH = pl.DeviceIdType.MESH
MIB = 1024 * 1024
ANY = pl.BlockSpec(memory_space=pl.ANY)
VMEM = pl.BlockSpec(memory_space=pltpu.VMEM)

NT_DIMS = (((1,), (1,)), ((), ()))
TN_DIMS = (((0,), (0,)), ((), ()))

_PAIR_ORDER = (0, 4, 1, 5, 2, 6, 3, 7)
_CHUNK_SRC = np.array([
    list(_PAIR_ORDER) + [8, 9, 10],
    [0] + [1 + h for h in _PAIR_ORDER] + [9, 10],
    list(range(N_CHUNKS)),
    list(range(N_CHUNKS)),
], np.int32)
_CHUNK_POS = np.argsort(_CHUNK_SRC, axis=1).astype(np.int32)


def _bias_table():
    i = np.arange(N_HEADS * BLOCK)[:, None]
    j = np.arange(3 * BLOCK)[None, :]
    dist = np.abs(j - BLOCK - (i % BLOCK))
    slope = 2.0 ** -(i // BLOCK + 1.0)
    inner = np.where(dist <= BLOCK, -(slope * dist), NEG_INF)
    first = np.where(j >= BLOCK, inner, NEG_INF)
    last = np.where(j < 2 * BLOCK, inner, NEG_INF)
    return np.stack([first, inner, last]).astype(np.float32)


def _full(shape):
    n = len(shape)
    return pl.BlockSpec(shape, lambda *_: (0,) * n)


def _sds(shape, dtype=F32):
    return jax.ShapeDtypeStruct(shape, dtype)


def _coords():
    return lax.axis_index("x"), lax.axis_index("y"), lax.axis_index("c")


def _other_chips(x, y):
    return [(1 - x, y), (x, 1 - y), (1 - x, 1 - y)]


def _remote(src, dst, sems, dev):
    return pltpu.make_async_remote_copy(src_ref=src, dst_ref=dst, send_sem=sems[0], recv_sem=sems[1],
                                        device_id=dev, device_id_type=MESH)


class _Job:
    def __init__(self, inputs, out_shapes, n_remote, n_local, make, then=None, in_place=False):
        self.inputs, self.out_shapes, self.n_remote, self.n_local, self.make = inputs, out_shapes, n_remote, n_local, make
        self.then = then
        self.in_place = in_place


def _job_aliases(jobs, in_base, out_base):
    aliases, a, b = {}, 0, 0
    for j in jobs:
        if j.in_place:
            aliases.update({in_base + a + k: out_base + b + k for k in range(len(j.inputs))})
        a, b = a + len(j.inputs), b + len(j.out_shapes)
    return aliases


def _job_copies(jobs, jin, jout, sems, second=False):
    send, recv, loc = sems
    res, a, b, r, l = [], 0, 0, 0, 0
    for j in jobs:
        build = j.then if second else j.make
        if build is not None:
            res += build(jin[a:a + len(j.inputs)], jout[b:b + len(j.out_shapes)],
                         lambda k, r=r: (send.at[r + k], recv.at[r + k]), lambda k, l=l: loc.at[l + k])
        a, b, r, l = a + len(j.inputs), b + len(j.out_shapes), r + j.n_remote, l + j.n_local
    return res


def _run(copies):
    for cp in copies:
        cp.start()
    for cp in copies:
        cp.wait()


def _job_gather(sources, shapes):
    n = len(sources)

    def make(ins, outs, rsem, lsem):
        x, y, c = _coords()
        j = 2 * x + y
        res = []
        for t, ((_, layer), src, dst) in enumerate(zip(sources, ins, outs)):
            src = src if layer is None else src.at[layer]
            res.append(pltpu.make_async_copy(src, dst.at[j], lsem(t)))
            for k, chip in enumerate(_other_chips(x, y)):
                res.append(_remote(src, dst.at[j], rsem(3 * t + k), (*chip, c)))
        return res

    return _Job([a for a, _ in sources], [_sds((N_CHIPS,) + s, BF16) for s in shapes], 3 * n, n, make)


def _job_swap(g):
    _, rows, cols = g.shape
    half = rows // 2

    def make(ins, outs, rsem, lsem):
        x, y, c = _coords()
        return [_remote(ins[0].at[:, pl.ds((1 - c) * half, half), :], outs[0], rsem(0), (x, y, 1 - c))]

    return _Job([g], [_sds((N_CHIPS, half, cols))], 1, 0, make)


def _job_scatter(p):
    def make(ins, outs, rsem, lsem):
        x, y, c = _coords()
        return [_remote(ins[0].at[2 * chip[0] + chip[1]], outs[0].at[k], rsem(k), (*chip, c))
                for k, chip in enumerate(_other_chips(x, y))]

    return _Job([p], [_sds((3,) + p.shape[1:], p.dtype)], 3, 0, make)


def _job_all_gather(blk):
    m_per = blk.shape[0]

    def rows(ref, px, py, pc):
        return ref.at[pl.ds((4 * px + 2 * py + pc) * m_per, m_per), :]

    def make(ins, outs, rsem, lsem):
        x, y, c = _coords()
        res = [pltpu.make_async_copy(ins[0], rows(outs[0], x, y, c), lsem(0)),
               _remote(ins[0], rows(outs[0], x, y, c), rsem(0), (x, y, 1 - c))]
        res += [_remote(ins[0], rows(outs[0], x, y, c), rsem(1 + k), (*chip, c)) for k, chip in enumerate(_other_chips(x, y))]
        return res

    def then(ins, outs, rsem, lsem):
        x, y, c = _coords()
        return [_remote(rows(outs[0], *chip, c), rows(outs[0], *chip, c), rsem(4 + k), (x, y, 1 - c))
                for k, chip in enumerate(_other_chips(x, y))]

    return _Job([blk], [_sds((N_DEV * m_per, blk.shape[1]), blk.dtype)], 7, 1, make, then)


def _job_join(f):
    half = f.shape[0] // 2

    def make(ins, outs, rsem, lsem):
        x, y, c = _coords()
        mine = pl.ds(c * half, half)
        return [_remote(ins[0].at[mine, :], outs[0].at[mine, :], rsem(0), (x, y, 1 - c))]

    return _Job([f], [_sds(f.shape, f.dtype)], 1, 0, make, in_place=True)


def _pallas(body, *, name, grid, in_specs, out_specs, out_shape, operands, vmem_mib, jobs=()):
    in_specs, out_specs, out_shape = list(in_specs), list(out_specs), list(out_shape)
    n_in, n_out = len(in_specs), len(out_specs)
    j_in = [a for j in jobs for a in j.inputs]
    j_out = [s for j in jobs for s in j.out_shapes]
    n_rem = max(1, sum(j.n_remote for j in jobs))
    n_loc = max(1, sum(j.n_local for j in jobs))
    scratch = [pltpu.SemaphoreType.DMA((n_rem,)), pltpu.SemaphoreType.DMA((n_rem,)),
               pltpu.SemaphoreType.DMA((n_loc,))] if jobs else []

    def wrapped(*refs):
        ins = refs[:n_in]
        jin = refs[n_in:n_in + len(j_in)]
        outs = refs[n_in + len(j_in):n_in + len(j_in) + n_out]
        jout = refs[n_in + len(j_in) + n_out:n_in + len(j_in) + n_out + len(j_out)]

        if jobs:
            first = last = None
            for d, n in enumerate(grid):
                f, e = pl.program_id(d) == 0, pl.program_id(d) == n - 1
                first, last = (f, e) if first is None else (first & f, last & e)

            @pl.when(first)
            def _():
                for cp in _job_copies(jobs, jin, jout, refs[-3:]):
                    cp.start()

        body(*ins, *outs)

        if jobs:
            @pl.when(last)
            def _():
                for cp in _job_copies(jobs, jin, jout, refs[-3:]):
                    cp.wait()
                _run(_job_copies(jobs, jin, jout, refs[-3:], second=True))

    operands = [pltpu.with_memory_space_constraint(a, pltpu.HBM) if a.size * a.dtype.itemsize >= MIB else a
                for a in operands]
    return pl.pallas_call(
        wrapped, name=name, grid=grid,
        in_specs=in_specs + [ANY] * len(j_in), out_specs=out_specs + [ANY] * len(j_out),
        out_shape=out_shape + j_out, scratch_shapes=scratch, input_output_aliases=_job_aliases(jobs, n_in, n_out),
        compiler_params=pltpu.CompilerParams(dimension_semantics=("arbitrary",) * len(grid),
                                             vmem_limit_bytes=vmem_mib * MIB),
    )(*operands, *j_in)


def _comm(jobs, name):
    j_in = [a for j in jobs for a in j.inputs]
    j_out = [s for j in jobs for s in j.out_shapes]
    n_rem = max(1, sum(j.n_remote for j in jobs))
    n_loc = max(1, sum(j.n_local for j in jobs))

    def body(*refs):
        jin, jout = refs[:len(j_in)], refs[len(j_in):len(j_in) + len(j_out)]
        _run(_job_copies(jobs, jin, jout, refs[-3:]))
        _run(_job_copies(jobs, jin, jout, refs[-3:], second=True))

    return pl.pallas_call(
        body, name=name, in_specs=[ANY] * len(j_in), out_specs=[ANY] * len(j_out), out_shape=j_out,
        scratch_shapes=[pltpu.SemaphoreType.DMA((n_rem,)), pltpu.SemaphoreType.DMA((n_rem,)),
                        pltpu.SemaphoreType.DMA((n_loc,))],
        input_output_aliases=_job_aliases(jobs, 0, 0),
    )(*j_in)


def _sigmoid(x):
    return 1.0 / (1.0 + jnp.exp(-x))


def _lo_mask(shape):
    return lax.broadcasted_iota(jnp.int32, shape, len(shape) - 1) < HEAD_DIM


def _half_sum(x, lo):
    a = jnp.sum(jnp.where(lo, x, 0.0), axis=-1, keepdims=True)
    b = jnp.sum(jnp.where(lo, 0.0, x), axis=-1, keepdims=True)
    return jnp.where(lo, a, b)


def _half_rms_scale(x, lo):
    return lax.rsqrt(_half_sum(x * x, lo) * (1.0 / HEAD_DIM) + EPS)


def _stack_heads(pairs, lo):
    return jnp.concatenate([jnp.where(lo, t, 0.0) for t in pairs] + [jnp.where(lo, 0.0, t) for t in pairs], axis=0)


def _unstack_pair(stack, p, lo):
    return jnp.where(lo, stack[BLOCK * p:BLOCK * (p + 1)], stack[BLOCK * (4 + p):BLOCK * (5 + p)])


def _attention_probs(q_stack, kn, bias_ref, sink_ref):
    rows = N_HEADS * BLOCK
    s = lax.dot_general(q_stack, kn, NT_DIMS, preferred_element_type=F32) + bias_ref[...]
    sink = jnp.concatenate([jnp.full((BLOCK, BLOCK), sink_ref[h], F32) for h in range(N_HEADS)], axis=0)
    cols = [s[:, BLOCK * j:BLOCK * (j + 1)] for j in range(3)]
    top = jnp.max(jnp.maximum(jnp.maximum(cols[0], cols[1]), cols[2]), axis=-1, keepdims=True)
    m = jnp.maximum(jnp.broadcast_to(top, (rows, BLOCK)), sink)
    e = [jnp.exp(c - m) for c in cols]
    es = jnp.exp(sink - m)
    inv = 1.0 / (jnp.broadcast_to(jnp.sum((e[0] + e[1]) + e[2], axis=-1, keepdims=True), (rows, BLOCK)) + es)
    return jnp.concatenate([c * inv for c in e], axis=1), es * inv


def _kv_rows(cur_ref, pkv_ref, nkv_ref):
    k = jnp.concatenate([pkv_ref[:, 0:D_KV], cur_ref[:, C_K:C_K + D_KV], nkv_ref[:, 0:D_KV]], axis=0)
    v = jnp.concatenate([pkv_ref[:, D_KV:2 * D_KV], cur_ref[:, C_V:C_V + D_KV], nkv_ref[:, D_KV:2 * D_KV]], axis=0)
    return k, v


def _overlap_add(parts):
    blocks = []
    for j in range(SUB + 2):
        terms = [parts[b][BLOCK * (j - b):BLOCK * (j - b + 1)] for b in range(SUB) if 0 <= j - b <= 2]
        total = terms[0]
        for t in terms[1:]:
            total = total + t
        blocks.append(total)
    return jnp.concatenate(blocks, axis=0)


def _mix_specs(nt):
    cur = pl.BlockSpec((TILE, D_IN), lambda i: (i, 0))
    kv_col = C_K // (2 * D_KV)
    pkv = pl.BlockSpec((BLOCK, 2 * D_KV), lambda i: (jnp.maximum(i * SUB - 1, 0), kv_col))
    nkv = pl.BlockSpec((BLOCK, 2 * D_KV), lambda i: (jnp.minimum((i + 1) * SUB, nt * SUB - 1), kv_col))
    table = (None, N_HEADS * BLOCK, 3 * BLOCK)
    first = pl.BlockSpec(table, lambda i: (jnp.where(i == 0, 0, 1), 0, 0))
    inner = pl.BlockSpec(table, lambda i: (1, 0, 0))
    last = pl.BlockSpec(table, lambda i: (jnp.where(i == nt - 1, 2, 1), 0, 0))
    return cur, pkv, nkv, [first] + [inner] * (SUB - 2) + [last]


def _proj_fwd(x, ng, scale, shift, wt, name, jobs=()):
    s = x.shape[0]
    ts = min(512, s)

    def body(x_ref, ng_ref, sc_ref, sh_ref, w_ref, o_ref):
        xv = x_ref[...]
        r = lax.rsqrt(jnp.mean(xv * xv, axis=-1, keepdims=True) + EPS)
        h = ((xv * r) * ng_ref[...]) * (1.0 + sc_ref[...]) + sh_ref[...]
        o_ref[...] = lax.dot_general(h.astype(BF16), w_ref[...], NT_DIMS, preferred_element_type=F32)

    vec = _full((1, D_MODEL))
    return _pallas(
        body, name=name, grid=(s // ts,),
        in_specs=[pl.BlockSpec((ts, D_MODEL), lambda i: (i, 0)), vec, vec, vec, _full((D_IN, D_MODEL))],
        out_specs=[pl.BlockSpec((ts, D_IN), lambda i: (i, 0))], out_shape=[_sds((s, D_IN))],
        operands=(x, ng, scale, shift, wt), vmem_mib=48, jobs=jobs)


def _diagonal():
    return lax.broadcasted_iota(jnp.int32, (BLOCK, BLOCK), 0) == lax.broadcasted_iota(jnp.int32, (BLOCK, BLOCK), 1)


def _column_as_row(wide, eye):
    return jnp.sum(jnp.where(eye, wide, 0.0), axis=0, keepdims=True)


def _mix_fwd(proj, bias, sink, qg2, kg2, ws, bsp, name, jobs=()):
    s = proj.shape[0]
    nt = s // TILE

    def body(sink_ref, cur_ref, pkv_ref, nkv_ref, *rest):
        bias_refs = rest[:SUB]
        qg_ref, kg_ref, ws_ref, bsp_ref, y_ref, p_ref, ps_ref, attn_ref, sv_ref = rest[SUB:]
        lo = _lo_mask((BLOCK, BLOCK))
        eye = _diagonal()
        lo_kv = _lo_mask((TILE + 2 * BLOCK, BLOCK))
        k_all, v_all = _kv_rows(cur_ref, pkv_ref, nkv_ref)
        kn_all = ((k_all * _half_rms_scale(k_all, lo_kv)) * kg_ref[...]).astype(BF16)
        vb_all = v_all.astype(BF16)
        for b in range(SUB):
            rows = slice(BLOCK * b, BLOCK * (b + 1))
            window = slice(BLOCK * b, BLOCK * (b + 3))
            qn = []
            for p in range(4):
                q = cur_ref[rows, C_Q + BLOCK * p:C_Q + BLOCK * (p + 1)]
                qn.append(((q * _half_rms_scale(q, lo)) * qg_ref[...]) * 0.125)
            q_stack = _stack_heads(qn, lo).astype(BF16)
            prob, psink = _attention_probs(q_stack, kn_all[window], bias_refs[b], sink_ref)
            pb = prob.astype(BF16)
            p_ref[STACK * b:STACK * (b + 1), :] = pb
            ps_ref[N_HEADS * b:N_HEADS * (b + 1), :] = jnp.concatenate(
                [_column_as_row(psink[BLOCK * h:BLOCK * (h + 1)], eye) for h in range(N_HEADS)], axis=0)
            o_stack = jnp.dot(pb, vb_all[window], preferred_element_type=F32)
            for p in range(4):
                g = cur_ref[rows, C_GA + BLOCK * p:C_GA + BLOCK * (p + 1)]
                attn = _unstack_pair(o_stack, p, lo)
                attn_ref[rows, BLOCK * p:BLOCK * (p + 1)] = attn.astype(BF16)
                y_ref[rows, BLOCK * p:BLOCK * (p + 1)] = (attn * (g * _sigmoid(g))).astype(BF16)

        for p in range(4):
            cols = slice(C_VG + BLOCK * p, C_VG + BLOCK * (p + 1))
            vn = []
            for b in range(SUB):
                vg = cur_ref[BLOCK * b:BLOCK * (b + 1), cols]
                vn.append((vg * _half_rms_scale(vg, lo)).astype(BF16))
            vn = jnp.concatenate(vn, axis=1)
            sv_a = jnp.dot(ws_ref[2 * p], vn, preferred_element_type=F32)
            sv_b = jnp.dot(ws_ref[2 * p + 1], vn, preferred_element_type=F32)
            for b in range(SUB):
                rows = slice(BLOCK * b, BLOCK * (b + 1))
                lanes = slice(BLOCK * b, BLOCK * (b + 1))
                sv = jnp.where(lo, sv_a[:, lanes], sv_b[:, lanes]) + bsp_ref[p]
                sv_ref[rows, BLOCK * p:BLOCK * (p + 1)] = sv.astype(BF16)
                u = cur_ref[rows, C_U + BLOCK * p:C_U + BLOCK * (p + 1)]
                g = cur_ref[rows, C_GG + BLOCK * p:C_GG + BLOCK * (p + 1)]
                y_ref[rows, D_ATTN + BLOCK * p:D_ATTN + BLOCK * (p + 1)] = ((u * sv) * (g * _sigmoid(g))).astype(BF16)

    cur, pkv, nkv, bias_specs = _mix_specs(nt)
    nb = nt * SUB
    half = pl.BlockSpec((TILE, D_ATTN), lambda i: (i, 0))
    return _pallas(
        body, name=name, grid=(nt,),
        in_specs=[pl.BlockSpec(memory_space=pltpu.SMEM), cur, pkv, nkv, *bias_specs, _full((1, BLOCK)), _full((1, BLOCK)),
                  _full((8, BLOCK, BLOCK)), _full((4, BLOCK, BLOCK))],
        out_specs=[pl.BlockSpec((TILE, D_MODEL), lambda i: (i, 0)), pl.BlockSpec((SUB * STACK, 3 * BLOCK), lambda i: (i, 0)),
                   pl.BlockSpec((SUB * N_HEADS, BLOCK), lambda i: (i, 0)), half, half],
        out_shape=[_sds((s, D_MODEL), BF16), _sds((nb * STACK, 3 * BLOCK), BF16), _sds((nb * N_HEADS, BLOCK)),
                   _sds((s, D_ATTN), BF16), _sds((s, D_ATTN), BF16)],
        operands=(sink, proj, proj, proj, *([bias] * SUB), qg2, kg2, ws, bsp), vmem_mib=56, jobs=jobs)


def _out_fwd(y, x, gate, w_out, name):
    s = x.shape[0]
    ts = min(512, s)

    def body(y_ref, x_ref, g_ref, w_ref, o_ref):
        o_ref[...] = x_ref[...] + g_ref[...] * jnp.dot(y_ref[...], w_ref[...], preferred_element_type=F32)

    row = pl.BlockSpec((ts, D_MODEL), lambda i: (i, 0))
    return _pallas(
        body, name=name, grid=(s // ts,), in_specs=[row, row, _full((1, D_MODEL)), _full((D_MODEL, D_MODEL))],
        out_specs=[row], out_shape=[_sds((s, D_MODEL))], operands=(y, x, gate, w_out), vmem_mib=32)


def _out_fwd_loss(y, x, gate, w_out, target, name):
    s = x.shape[0]
    ts = min(512, s)

    def body(y_ref, x_ref, g_ref, w_ref, t_ref, dx_ref, sq_ref):
        @pl.when(pl.program_id(0) == 0)
        def _():
            sq_ref[...] = jnp.zeros_like(sq_ref)

        out = x_ref[...] + g_ref[...] * jnp.dot(y_ref[...], w_ref[...], preferred_element_type=F32)
        diff = out - t_ref[...]
        dx_ref[...] = diff * (1.0 / D_MODEL)
        per_token = jnp.sum(diff * diff, axis=-1, keepdims=True) * (1.0 / D_MODEL)
        sq_ref[...] += jnp.sum(per_token, axis=0, keepdims=True)

    row = pl.BlockSpec((ts, D_MODEL), lambda i: (i, 0))
    return _pallas(
        body, name=name, grid=(s // ts,), in_specs=[row, row, _full((1, D_MODEL)), _full((D_MODEL, D_MODEL)), row],
        out_specs=[row, _full((1, 1))], out_shape=[_sds((s, D_MODEL)), _sds((1, 1))],
        operands=(y, x, gate, w_out, target), vmem_mib=40)


def _out_bwd(dxo, y, gate, w_out, name, jobs=()):
    s = dxo.shape[0]
    ts = min(512, s)
    steps = s // ts

    def body(dx_ref, y_ref, g_ref, w_ref, dy_ref, gw_ref, dg_ref):
        @pl.when(pl.program_id(0) == 0)
        def _():
            gw_ref[...] = jnp.zeros_like(gw_ref)

        dx = dx_ref[...]
        dy_ref[...] = lax.dot_general((dx * g_ref[...]).astype(BF16), w_ref[...], NT_DIMS, preferred_element_type=F32)
        gw_ref[...] += lax.dot_general(y_ref[...], dx.astype(BF16), TN_DIMS, preferred_element_type=F32)

        @pl.when(pl.program_id(0) == steps - 1)
        def _():
            m = gw_ref[...]
            dg_ref[...] = jnp.sum(w_ref[...].astype(F32) * m, axis=0, keepdims=True)
            gw_ref[...] = m * g_ref[...]

    row = pl.BlockSpec((ts, D_MODEL), lambda i: (i, 0))
    return _pallas(
        body, name=name, grid=(s // ts,), in_specs=[row, row, _full((1, D_MODEL)), _full((D_MODEL, D_MODEL))],
        out_specs=[row, _full((D_MODEL, D_MODEL)), _full((1, D_MODEL))],
        out_shape=[_sds((s, D_MODEL)), _sds((D_MODEL, D_MODEL)), _sds((1, D_MODEL))],
        operands=(dxo, y, gate, w_out), vmem_mib=48, jobs=jobs)


def _mix_bwd(dy, proj, probs, psink, attn, sv, qg2, kg2, wst, name, jobs=()):
    s = proj.shape[0]
    nt = s // TILE

    def body(dy_ref, cur_ref, pkv_ref, nkv_ref, p_ref, ps_ref, attn_ref, sv_ref, qg_ref, kg_ref, wst_ref,
             dpb_ref, dkv_ref, p0_ref, p2_ref, dqg_ref, dkg_ref, dsink_ref, dws_ref, dbsp_ref):
        def put(rows, col, value):
            dpb_ref[rows, col:col + BLOCK] = value.astype(BF16)

        @pl.when(pl.program_id(0) == 0)
        def _():
            dqg_ref[...] = jnp.zeros_like(dqg_ref)
            dkg_ref[...] = jnp.zeros_like(dkg_ref)
            dsink_ref[...] = jnp.zeros_like(dsink_ref)
            dws_ref[...] = jnp.zeros_like(dws_ref)
            dbsp_ref[...] = jnp.zeros_like(dbsp_ref)

        lo = _lo_mask((BLOCK, BLOCK))
        lo_kv = _lo_mask((TILE + 2 * BLOCK, BLOCK))
        eye = _diagonal()
        lane_row = lax.broadcasted_iota(jnp.int32, (1, BLOCK), 1)
        qg = qg_ref[...]
        kg = kg_ref[...]

        k_all, v_all = _kv_rows(cur_ref, pkv_ref, nkv_ref)
        rk = _half_rms_scale(k_all, lo_kv)
        khat = k_all * rk
        kn_all = (khat * kg).astype(BF16)
        vb_all = v_all.astype(BF16)

        dkn_parts, dv_parts = [], []
        dsink = jnp.zeros((1, BLOCK), F32)
        dqg = jnp.zeros((1, BLOCK), F32)
        for b in range(SUB):
            rows = slice(BLOCK * b, BLOCK * (b + 1))
            window = slice(BLOCK * b, BLOCK * (b + 3))
            kn, vb = kn_all[window], vb_all[window]

            qhat, rq = [], []
            for p in range(4):
                q = cur_ref[rows, C_Q + BLOCK * p:C_Q + BLOCK * (p + 1)]
                r = _half_rms_scale(q, lo)
                rq.append(r)
                qhat.append(q * r)
            q_stack = _stack_heads([(qh * qg) * 0.125 for qh in qhat], lo).astype(BF16)
            pb = p_ref[STACK * b:STACK * (b + 1), :]
            prob = pb.astype(F32)

            dout = []
            for p in range(4):
                g = cur_ref[rows, C_GA + BLOCK * p:C_GA + BLOCK * (p + 1)]
                sg = _sigmoid(g)
                dya = dy_ref[rows, BLOCK * p:BLOCK * (p + 1)]
                attn = attn_ref[rows, BLOCK * p:BLOCK * (p + 1)]
                put(rows, C_GA + BLOCK * p, dya * attn * (sg * (1.0 + g * (1.0 - sg))))
                dout.append(dya * (g * sg))
            do_stack = _stack_heads(dout, lo).astype(BF16)
            dp = lax.dot_general(do_stack, vb, NT_DIMS, preferred_element_type=F32)
            delta = jnp.sum(prob * dp, axis=-1, keepdims=True)
            dsb = (prob * (dp - delta)).astype(BF16)

            for h in range(N_HEADS):
                delta_row = _column_as_row(jnp.broadcast_to(delta[BLOCK * h:BLOCK * (h + 1)], (BLOCK, BLOCK)), eye)
                tot = jnp.sum(ps_ref[N_HEADS * b + h:N_HEADS * b + h + 1, :] * delta_row, axis=-1, keepdims=True)
                dsink = dsink - jnp.where(lane_row == h, tot, 0.0)

            dq_stack = jnp.dot(dsb, kn, preferred_element_type=F32) * 0.125
            dkn_parts.append(lax.dot_general(dsb, q_stack, TN_DIMS, preferred_element_type=F32))
            dv_parts.append(lax.dot_general(pb, do_stack, TN_DIMS, preferred_element_type=F32))

            for p in range(4):
                dqn = _unstack_pair(dq_stack, p, lo)
                qh = qhat[p]
                dqg = dqg + jnp.sum(dqn * qh, axis=0, keepdims=True)
                dqh = dqn * qg
                mean = _half_sum(dqh * qh, lo) * (1.0 / HEAD_DIM)
                put(rows, C_Q + BLOCK * p, rq[p] * (dqh - qh * mean))

        for p in range(4):
            rs, vnfs, vns, dsvs, dbs = [], [], [], [], None
            for b in range(SUB):
                rows = slice(BLOCK * b, BLOCK * (b + 1))
                vg = cur_ref[rows, C_VG + BLOCK * p:C_VG + BLOCK * (p + 1)]
                r = _half_rms_scale(vg, lo)
                vnf = vg * r
                sv = sv_ref[rows, BLOCK * p:BLOCK * (p + 1)]
                u = cur_ref[rows, C_U + BLOCK * p:C_U + BLOCK * (p + 1)]
                g = cur_ref[rows, C_GG + BLOCK * p:C_GG + BLOCK * (p + 1)]
                sg = _sigmoid(g)
                dym = dy_ref[rows, D_ATTN + BLOCK * p:D_ATTN + BLOCK * (p + 1)]
                put(rows, C_GG + BLOCK * p, dym * (u * sv) * (sg * (1.0 + g * (1.0 - sg))))
                dgm = dym * (g * sg)
                put(rows, C_U + BLOCK * p, dgm * sv)
                dsv = dgm * u
                term = jnp.where(lo, jnp.sum(jnp.where(lo, dsv, 0.0), axis=-1, keepdims=True),
                                 jnp.sum(jnp.where(lo, 0.0, dsv), axis=-1, keepdims=True))
                dbs = term if dbs is None else dbs + term
                rs.append(r)
                vnfs.append(vnf)
                vns.append(vnf.astype(BF16))
                dsvs.append(dsv)
            dbsp_ref[p] += dbs
            vn = jnp.concatenate(vns, axis=1)
            dsv = jnp.concatenate(dsvs, axis=1)
            lo_t = (lax.broadcasted_iota(jnp.int32, dsv.shape, 1) & (BLOCK - 1)) < HEAD_DIM
            dws_ref[2 * p] += lax.dot_general(jnp.where(lo_t, dsv, 0.0).astype(BF16), vn, NT_DIMS, preferred_element_type=F32)
            dws_ref[2 * p + 1] += lax.dot_general(jnp.where(lo_t, 0.0, dsv).astype(BF16), vn, NT_DIMS,
                                                  preferred_element_type=F32)
            dsvb = dsv.astype(BF16)
            dvn_a = jnp.dot(wst_ref[2 * p], dsvb, preferred_element_type=F32)
            dvn_b = jnp.dot(wst_ref[2 * p + 1], dsvb, preferred_element_type=F32)
            for b in range(SUB):
                lanes = slice(BLOCK * b, BLOCK * (b + 1))
                dvn = jnp.where(lo, dvn_a[:, lanes], dvn_b[:, lanes])
                mean = _half_sum(dvn * vnfs[b], lo) * (1.0 / HEAD_DIM)
                put(slice(BLOCK * b, BLOCK * (b + 1)), C_VG + BLOCK * p, rs[b] * (dvn - vnfs[b] * mean))

        dsink_ref[...] += dsink
        dqg = jnp.broadcast_to(dqg, (8, BLOCK))
        dqg_ref[...] += dqg + pltpu.roll(dqg, HEAD_DIM, 1)

        dkn = _overlap_add(dkn_parts)
        dv = _overlap_add(dv_parts)
        dkg = jnp.broadcast_to(jnp.sum(dkn * khat, axis=0, keepdims=True), (8, BLOCK))
        dkg_ref[...] += dkg + pltpu.roll(dkg, HEAD_DIM, 1)
        dkh = dkn * kg
        dk = rk * (dkh - khat * (_half_sum(dkh * khat, lo_kv) * (1.0 / HEAD_DIM)))
        dpb_ref[:, C_K:C_GA] = jnp.zeros((TILE, 2 * D_KV), BF16)
        dkv_ref[:, 0:D_KV] = dk[BLOCK:BLOCK + TILE]
        dkv_ref[:, D_KV:2 * D_KV] = dv[BLOCK:BLOCK + TILE]
        p0_ref[:, 0:D_KV] = dk[0:BLOCK]
        p0_ref[:, D_KV:2 * D_KV] = dv[0:BLOCK]
        p2_ref[:, 0:D_KV] = dk[BLOCK + TILE:]
        p2_ref[:, D_KV:2 * D_KV] = dv[BLOCK + TILE:]

    cur, pkv, nkv, _ = _mix_specs(nt)
    kv_blk = (BLOCK, 2 * D_KV)
    half = pl.BlockSpec((TILE, D_ATTN), lambda i: (i, 0))
    return _pallas(
        body, name=name, grid=(nt,),
        in_specs=[pl.BlockSpec((TILE, D_MODEL), lambda i: (i, 0)), cur, pkv, nkv,
                  pl.BlockSpec((SUB * STACK, 3 * BLOCK), lambda i: (i, 0)), pl.BlockSpec((SUB * N_HEADS, BLOCK), lambda i: (i, 0)),
                  half, half, _full((1, BLOCK)), _full((1, BLOCK)), _full((8, BLOCK, BLOCK))],
        out_specs=[cur, pl.BlockSpec((TILE, 2 * D_KV), lambda i: (i, 0)),
                   pl.BlockSpec(kv_blk, lambda i: ((i + nt - 1) % nt, 0)),
                   pl.BlockSpec(kv_blk, lambda i: ((i + 1) % nt, 0)),
                   _full((8, BLOCK)), _full((8, BLOCK)), _full((1, BLOCK)),
                   _full((8, BLOCK, BLOCK)), _full((4, BLOCK, BLOCK))],
        out_shape=[_sds((s, D_IN), BF16), _sds((s, 2 * D_KV)), _sds((nt * BLOCK, 2 * D_KV)), _sds((nt * BLOCK, 2 * D_KV)),
                   _sds((8, BLOCK)), _sds((8, BLOCK)), _sds((1, BLOCK)),
                   _sds((8, BLOCK, BLOCK)), _sds((4, BLOCK, BLOCK))],
        operands=(dy, proj, proj, proj, probs, psink, attn, sv, qg2, kg2, wst), vmem_mib=56, jobs=jobs)


def _w_in_grad(dpb, dkv, p0, p2, x, ng, scale, shift, name, jobs=()):
    s = x.shape[0]
    ts = min(2 * TILE, s)
    tiles = ts // TILE

    def body(dpb_ref, dkv_ref, p0_ref, p2_ref, x_ref, ng_ref, sc_ref, sh_ref, gw_ref, dkvb_ref):
        @pl.when(pl.program_id(0) == 0)
        def _():
            gw_ref[...] = jnp.zeros_like(gw_ref)

        xv = x_ref[...]
        r = lax.rsqrt(jnp.mean(xv * xv, axis=-1, keepdims=True) + EPS)
        h = (((xv * r) * ng_ref[...]) * (1.0 + sc_ref[...]) + sh_ref[...]).astype(BF16)
        for t in range(tiles):
            halo = slice(BLOCK * t, BLOCK * (t + 1))
            first = slice(TILE * t, TILE * t + BLOCK)
            last = slice(TILE * (t + 1) - BLOCK, TILE * (t + 1))
            dkvb_ref[first, :] = (dkv_ref[first, :] + p2_ref[halo, :]).astype(BF16)
            if SUB > 2:
                inner = slice(TILE * t + BLOCK, TILE * (t + 1) - BLOCK)
                dkvb_ref[inner, :] = dkv_ref[inner, :].astype(BF16)
            dkvb_ref[last, :] = (dkv_ref[last, :] + p0_ref[halo, :]).astype(BF16)
        gw_ref[...] += lax.dot_general(dpb_ref[...], h, TN_DIMS, preferred_element_type=F32)
        gw_ref[C_K:C_GA, :] += lax.dot_general(dkvb_ref[...], h, TN_DIMS, preferred_element_type=F32)

    kv = pl.BlockSpec((ts, 2 * D_KV), lambda i: (i, 0))
    halo = pl.BlockSpec((tiles * BLOCK, 2 * D_KV), lambda i: (i, 0))
    vec = _full((1, D_MODEL))
    return _pallas(
        body, name=name, grid=(s // ts,),
        in_specs=[pl.BlockSpec((ts, D_IN), lambda i: (i, 0)), kv, halo, halo,
                  pl.BlockSpec((ts, D_MODEL), lambda i: (i, 0)), vec, vec, vec],
        out_specs=[_full((D_IN, D_MODEL)), kv], out_shape=[_sds((D_IN, D_MODEL)), _sds((s, 2 * D_KV), BF16)],
        operands=(dpb, dkv, p0, p2, x, ng, scale, shift), vmem_mib=56, jobs=jobs)


def _proj_bwd(dpb, dkvb, x, dxo, ng, scale, wt, name, jobs=()):
    s = x.shape[0]
    ts = min(512, s)

    def body(dpb_ref, dkvb_ref, x_ref, dxo_ref, ng_ref, sc_ref, w_ref, dxi_ref, dsh_ref, dsc_ref, dng_ref):
        @pl.when(pl.program_id(0) == 0)
        def _():
            dsh_ref[...] = jnp.zeros_like(dsh_ref)
            dsc_ref[...] = jnp.zeros_like(dsc_ref)
            dng_ref[...] = jnp.zeros_like(dng_ref)

        dh = (jnp.dot(dpb_ref[...], w_ref[...], preferred_element_type=F32)
              + jnp.dot(dkvb_ref[...], w_ref[C_K:C_GA, :], preferred_element_type=F32))

        xv = x_ref[...]
        r = lax.rsqrt(jnp.mean(xv * xv, axis=-1, keepdims=True) + EPS)
        xn = xv * r
        ngv = ng_ref[...]
        sc1 = 1.0 + sc_ref[...]
        dsh_ref[...] += jnp.sum(dh, axis=0, keepdims=True)
        dsc_ref[...] += jnp.sum(dh * (xn * ngv), axis=0, keepdims=True)
        dh1 = dh * sc1
        dng_ref[...] += jnp.sum(dh1 * xn, axis=0, keepdims=True)
        dxn = dh1 * ngv
        dxi_ref[...] = r * (dxn - xn * jnp.mean(dxn * xn, axis=-1, keepdims=True)) + dxo_ref[...]

    row = pl.BlockSpec((ts, D_MODEL), lambda i: (i, 0))
    vec = _full((1, D_MODEL))
    return _pallas(
        body, name=name, grid=(s // ts,),
        in_specs=[pl.BlockSpec((ts, D_IN), lambda i: (i, 0)), pl.BlockSpec((ts, 2 * D_KV), lambda i: (i, 0)),
                  row, row, vec, vec, _full((D_IN, D_MODEL))],
        out_specs=[row, vec, vec, vec],
        out_shape=[_sds((s, D_MODEL)), _sds((1, D_MODEL)), _sds((1, D_MODEL)), _sds((1, D_MODEL))],
        operands=(dpb, dkvb, x, dxo, ng, scale, wt), vmem_mib=48, jobs=jobs)


def _pair_sum(g, r, c_idx, name, send_dtype=None):
    _, rows, cols = g.shape
    half = rows // 2

    def body(c_ref, g_ref, r_ref, o_ref, *narrow):
        total = g_ref[...] + r_ref[...]
        o_ref[...] = total
        for n_ref in narrow:
            n_ref[...] = total.astype(n_ref.dtype)

    blk = (None, half, cols)
    out_blk = pl.BlockSpec(blk, lambda j, c: (j, 0, 0))
    shapes = [_sds((N_CHIPS, half, cols))] + ([_sds((N_CHIPS, half, cols), send_dtype)] if send_dtype else [])
    return pl.pallas_call(
        body, name=name,
        grid_spec=pltpu.PrefetchScalarGridSpec(
            num_scalar_prefetch=1, grid=(N_CHIPS,),
            in_specs=[pl.BlockSpec(blk, lambda j, c: (j, c[0], 0)), out_blk], out_specs=[out_blk] * len(shapes)),
        out_shape=shapes,
        compiler_params=pltpu.CompilerParams(dimension_semantics=("arbitrary",), vmem_limit_bytes=32 * MIB),
    )(c_idx, g, r)


def _chip_sum(p, r, place, name):
    _, rows, cols = p.shape
    tr = rows // 2

    def body(j_ref, p_ref, r_ref, o_ref):
        o_ref[...] = ((p_ref[...] + r_ref[0].astype(F32)) + r_ref[1].astype(F32)) + r_ref[2].astype(F32)

    return pl.pallas_call(
        body, name=name,
        grid_spec=pltpu.PrefetchScalarGridSpec(
            num_scalar_prefetch=1, grid=(2,),
            in_specs=[pl.BlockSpec((None, tr, cols), lambda t, j: (j[0], t, 0)),
                      pl.BlockSpec((3, tr, cols), lambda t, j: (0, t, 0))],
            out_specs=pl.BlockSpec((tr, cols), lambda t, j: (2 * j[1] + t, 0))),
        out_shape=_sds((2 * rows, cols)),
        compiler_params=pltpu.CompilerParams(dimension_semantics=("arbitrary",), vmem_limit_bytes=32 * MIB),
    )(place, p, r)


def _cast_permute_w_in(wt, place_chunks):
    def body(t_ref, w_ref, w0_ref, w1_ref):
        def cast_into(o_ref):
            for t in range(N_CHUNKS):
                src = pl.multiple_of(t_ref[1 + t] * CHUNK_ROWS, CHUNK_ROWS)
                o_ref[CHUNK_ROWS * t:CHUNK_ROWS * (t + 1), :] = w_ref[pl.ds(src, CHUNK_ROWS), :].astype(BF16)

        @pl.when(pl.program_id(0) == 0)
        def _():
            cast_into(w0_ref)

        @pl.when(pl.program_id(0) == 1)
        def _():
            cast_into(w1_ref)

    return pl.pallas_call(
        body, name="cast_permute_w_in",
        grid_spec=pltpu.PrefetchScalarGridSpec(
            num_scalar_prefetch=1, grid=(DEPTH,),
            in_specs=[pl.BlockSpec((None, W_IN_BLK, D_MODEL), lambda l, tbl: (l, 0, 0))],
            out_specs=[pl.BlockSpec((None, W_IN_BLK, D_MODEL), lambda l, tbl: (tbl[0], 0, 0)),
                       pl.BlockSpec((W_IN_BLK, D_MODEL), lambda l, tbl: (0, 0))]),
        out_shape=[_sds((N_CHIPS, W_IN_BLK, D_MODEL), BF16), _sds((W_IN_BLK, D_MODEL), BF16)],
        compiler_params=pltpu.CompilerParams(dimension_semantics=("arbitrary",), vmem_limit_bytes=32 * MIB),
    )(place_chunks, wt)


def _gather_inputs(c, w_out, w0):
    half = W_IN_BLK // 2

    def body(c_ref, wout_ref, mine_ref, call_ref, woutb_ref, w0_ref, send_sems, recv_sems):
        x, y, cc = _coords()
        j = 2 * x + y
        b = 2 * j + cc
        sib = (x, y, 1 - cc)
        woutb_ref[...] = wout_ref[...].astype(BF16)
        call_ref[b] = c_ref[...]
        chips = _other_chips(x, y)

        def sems(k):
            return send_sems.at[k], recv_sems.at[k]

        def half_rows(chip_index):
            return w0_ref.at[chip_index, pl.ds(cc * half, half), :]

        first = [_remote(mine_ref.at[j, pl.ds(cc * half, half), :], half_rows(j), sems(k), (*chip, cc))
                 for k, chip in enumerate(chips)]
        k = 3
        rest = []
        for fx in (0, 1):
            for fy in (0, 1):
                for fc in (0, 1):
                    if fx or fy or fc:
                        dev = (1 - x if fx else x, 1 - y if fy else y, 1 - cc if fc else cc)
                        rest.append(_remote(call_ref.at[b], call_ref.at[b], sems(k), dev))
                        k += 1
        for cp in first + rest:
            cp.start()
        passed = []
        for k, chip in enumerate(chips):
            jk = 2 * chip[0] + chip[1]
            first[k].wait_recv()
            passed.append(_remote(half_rows(jk), half_rows(jk), sems(10 + k), sib))
            passed[k].start()
        for cp in first:
            cp.wait_send()
        for cp in rest + passed:
            cp.wait()

    return pl.pallas_call(
        body, name="gather_inputs", in_specs=[VMEM, VMEM, ANY], out_specs=[VMEM, VMEM, ANY],
        out_shape=[_sds((N_DEV, 1, D_MODEL)), _sds((DEPTH, W_OUT_BLK, D_MODEL), BF16),
                   _sds((N_CHIPS, W_IN_BLK, D_MODEL), BF16)],
        scratch_shapes=[pltpu.SemaphoreType.DMA((13,)), pltpu.SemaphoreType.DMA((13,))],
        input_output_aliases={2: 2},
        compiler_params=pltpu.CompilerParams(vmem_limit_bytes=32 * MIB),
    )(c, w_out, w0)


def _ada_rows(c_all, w_ada, b_blk):
    def body(c_ref, w_ref, b_ref, o_ref, cond_ref):
        cv = c_ref[...]
        cond = (cv * _sigmoid(cv)).astype(BF16)
        cond_ref[...] = cond.astype(F32)
        for l in range(DEPTH):
            o_ref[:, l, :] = jnp.dot(cond, w_ref[l].astype(BF16), preferred_element_type=F32) + b_ref[l:l + 1, :]

    return pl.pallas_call(
        body, name="ada_rows", in_specs=[VMEM, VMEM, VMEM], out_specs=[VMEM, VMEM],
        out_shape=[_sds((N_DEV, DEPTH, W_ADA_BLK)), _sds((N_DEV, D_MODEL))],
        compiler_params=pltpu.CompilerParams(vmem_limit_bytes=32 * MIB),
    )(c_all, w_ada, b_blk)


def _exchange_ada(part):
    def body(part_ref, out_ref, send_sems, recv_sems):
        x, y, cc = _coords()
        j = 2 * x + y
        out_ref[j] = part_ref[2 * j + cc]
        copies = []
        for k, chip in enumerate(_other_chips(x, y)):
            b_dst = 4 * chip[0] + 2 * chip[1] + cc
            copies.append(_remote(part_ref.at[b_dst], out_ref.at[j], (send_sems.at[k], recv_sems.at[k]), (*chip, cc)))
        for cp in copies:
            cp.start()
        for cp in copies:
            cp.wait()

    return pl.pallas_call(
        body, name="exchange_ada", in_specs=[VMEM], out_specs=VMEM,
        out_shape=_sds((N_CHIPS, DEPTH, W_ADA_BLK)),
        scratch_shapes=[pltpu.SemaphoreType.DMA((3,)), pltpu.SemaphoreType.DMA((3,))],
    )(part)


def _all_gather_rows(blk):
    m_per, n = blk.shape

    def body(x_ref, out_ref, send_sems, recv_sems, local_sem):
        x, y, c = _coords()
        me, sibling = (x, y, c), (x, y, 1 - c)
        chips = _other_chips(x, y)

        def rows(px, py, pc):
            return out_ref.at[pl.ds((4 * px + 2 * py + pc) * m_per, m_per), :]

        def copy(k, block, to, src=None):
            return _remote(rows(*block) if src is None else src, rows(*block), (send_sems.at[k], recv_sems.at[k]), to)

        mine = pltpu.make_async_copy(x_ref, rows(*me), local_sem)
        mine.start()
        first = [copy(0, me, sibling, src=x_ref)]
        first += [copy(1 + j, me, (*chip, c), src=x_ref) for j, chip in enumerate(chips)]
        for cp in first:
            cp.start()
        passed = [copy(4 + j, (*chip, c), sibling) for j, chip in enumerate(chips)]
        for j, chip in enumerate(chips):
            copy(1 + j, (*chip, c), me).wait_recv()
            passed[j].start()
        copy(0, sibling, me).wait_recv()
        for j, chip in enumerate(chips):
            copy(4 + j, (*chip, 1 - c), me).wait_recv()
        for cp in first + passed:
            cp.wait_send()
        mine.wait()

    return pl.pallas_call(
        body, name="all_gather_small", in_specs=[VMEM], out_specs=VMEM,
        out_shape=_sds((N_DEV * m_per, n), blk.dtype),
        scratch_shapes=[pltpu.SemaphoreType.DMA((7,)), pltpu.SemaphoreType.DMA((7,)), pltpu.SemaphoreType.DMA],
        compiler_params=pltpu.CompilerParams(vmem_limit_bytes=32 * MIB),
    )(blk)


def _adamw_math(w, g, m, v):
    m = ADAM_B1 * m + (1.0 - ADAM_B1) * g
    v = ADAM_B2 * v + (1.0 - ADAM_B2) * (g * g)
    m_hat = m / (1.0 - ADAM_B1 ** ADAM_STEP)
    v_hat = v / (1.0 - ADAM_B2 ** ADAM_STEP)
    delta = -ADAM_LR * (m_hat / (jnp.sqrt(v_hat) + ADAM_EPS) + ADAM_WD * w)
    return delta, m, v


def _adamw_w_in(w, g0, g1, m, v, pos_chunks):
    def body(t_ref, w_ref, g0_ref, g1_ref, m_ref, v_ref, g_ref, d_ref, nm_ref, nv_ref):
        for l, src in enumerate((g0_ref, g1_ref)):
            g = src[...]
            g_ref[l] = g
            d_ref[l], nm_ref[l], nv_ref[l] = _adamw_math(w_ref[l], g, m_ref[l], v_ref[l])

    nat = pl.BlockSpec((DEPTH, CHUNK_ROWS, D_MODEL), lambda t, tbl: (0, t, 0))
    per = pl.BlockSpec((CHUNK_ROWS, D_MODEL), lambda t, tbl: (tbl[t], 0))
    return pl.pallas_call(
        body, name="adamw_w_in",
        grid_spec=pltpu.PrefetchScalarGridSpec(num_scalar_prefetch=1, grid=(N_CHUNKS,),
                                               in_specs=[nat, per, per, nat, nat], out_specs=[nat] * 4),
        out_shape=[_sds(w.shape)] * 4,
        compiler_params=pltpu.CompilerParams(dimension_semantics=("arbitrary",)),
    )(pos_chunks, w, g0, g1, m, v)


def _adamw_w_out(w, g0, g1, m, v):
    def body(w_ref, g0_ref, g1_ref, m_ref, v_ref, g_ref, d_ref, nm_ref, nv_ref):
        g = jnp.where(pl.program_id(0) == 0, g0_ref[...], g1_ref[...])
        g_ref[...] = g
        d_ref[...], nm_ref[...], nv_ref[...] = _adamw_math(w_ref[...], g, m_ref[...], v_ref[...])

    blk = pl.BlockSpec((None, W_OUT_BLK, D_MODEL), lambda l: (l, 0, 0))
    gblk = _full((W_OUT_BLK, D_MODEL))
    return pl.pallas_call(
        body, name="adamw_w_out", grid=(DEPTH,), in_specs=[blk, gblk, gblk, blk, blk], out_specs=[blk] * 4,
        out_shape=[_sds(w.shape)] * 4,
        compiler_params=pltpu.CompilerParams(dimension_semantics=("arbitrary",), vmem_limit_bytes=32 * MIB),
    )(w, g0, g1, m, v)


def _w_ada_grad_adamw(cond_t, dada, w, m, v):
    _, rows, cols = w.shape
    tr = 256

    def body(ct_ref, da_ref, w_ref, m_ref, v_ref, g_ref, d_ref, nm_ref, nv_ref):
        g = jnp.dot(ct_ref[...], da_ref[...].astype(BF16), preferred_element_type=F32)
        g_ref[...] = g
        d_ref[...], nm_ref[...], nv_ref[...] = _adamw_math(w_ref[...], g, m_ref[...], v_ref[...])

    blk = pl.BlockSpec((None, tr, cols), lambda l, t: (l, t, 0))
    return pl.pallas_call(
        body, name="w_ada_grad_adamw", grid=(DEPTH, rows // tr),
        in_specs=[pl.BlockSpec((tr, BLOCK), lambda l, t: (t, 0)), pl.BlockSpec((None, BLOCK, cols), lambda l, t: (l, 0, 0)),
                  blk, blk, blk],
        out_specs=[blk] * 4, out_shape=[_sds(w.shape)] * 4,
        compiler_params=pltpu.CompilerParams(dimension_semantics=("arbitrary", "arbitrary"), vmem_limit_bytes=32 * MIB),
    )(cond_t, dada, w, m, v)


def _small_sum_adamw(gathered_a, gathered_b, w, m, v):
    def body(a_ref, b_ref, w_ref, m_ref, v_ref, g_ref, d_ref, nm_ref, nv_ref):
        def total(ref):
            g = ref[0]
            for b in range(1, N_DEV):
                g = g + ref[b]
            return g

        g = jnp.concatenate([total(a_ref), total(b_ref)], axis=0)
        g_ref[...] = g
        d_ref[...], nm_ref[...], nv_ref[...] = _adamw_math(w_ref[...], g, m_ref[...], v_ref[...])

    return pl.pallas_call(
        body, name="small_sum_adamw", in_specs=[VMEM] * 5, out_specs=[VMEM] * 4, out_shape=[_sds(w.shape)] * 4,
        compiler_params=pltpu.CompilerParams(vmem_limit_bytes=48 * MIB),
    )(gathered_a, gathered_b, w, m, v)


_SMALL_A = (("w_s", DEPTH * 8 * BLOCK), ("b_s", DEPTH * 8), ("q_gain", 1), ("k_gain", 1), ("sink", 1))
_SMALL_B = (("b_ada", DEPTH * 24), ("norm_gain", DEPTH * 8), ("sq_err", 1))


def _pack_rows(parts, layout):
    rows = []
    for name, n in layout:
        flat = parts[name].reshape(-1)
        rows.append(jnp.pad(flat, (0, n * 128 - flat.shape[0])).reshape(n, 128))
    n_rows = sum(n for _, n in layout)
    if n_rows % 8:
        rows.append(jnp.zeros((-n_rows % 8, 128), F32))
    return jnp.concatenate(rows, axis=0)


def _pack_small(parts):
    return jnp.concatenate([_pack_rows(parts, _SMALL_A), _pack_rows(parts, _SMALL_B)], axis=0)


def _unpack_small(packed, shapes):
    out, r0 = {}, 0
    for layout in (_SMALL_A, _SMALL_B):
        for name, n in layout:
            size = 1
            for d in shapes[name]:
                size *= d
            out[name] = packed[r0:r0 + n].reshape(-1)[:size].reshape(shapes[name])
            r0 += n
        r0 += -r0 % 8
    return out


def _permute_heads(a, axis):
    shp = a.shape
    a = a.reshape(shp[:axis] + (2, 4, HEAD_DIM) + shp[axis + 1:])
    a = jnp.swapaxes(a, axis, axis + 1)
    return a.reshape(shp)


def _unpermute_heads(a, axis):
    shp = a.shape
    a = a.reshape(shp[:axis] + (4, 2, HEAD_DIM) + shp[axis + 1:])
    a = jnp.swapaxes(a, axis, axis + 1)
    return a.reshape(shp)


def _permute_w_out(w):
    return jnp.concatenate([_permute_heads(w[:D_ATTN], 0), w[D_ATTN:]], axis=0)


def _unpermute_w_out(w):
    return jnp.concatenate([_unpermute_heads(w[:D_ATTN], 0), w[D_ATTN:]], axis=0)


def kernel(x, c, w_ada, b_ada, norm_gain, w_in, q_gain, k_gain, sink, w_s, b_s, w_out, loss_target, m_w_ada, m_b_ada, m_norm_gain, m_w_in, m_q_gain, m_k_gain, m_sink, m_w_s, m_b_s, m_w_out, v_w_ada, v_b_ada, v_norm_gain, v_w_in, v_q_gain, v_k_gain, v_sink, v_w_s, v_b_s, v_w_out):
    ix, iy, ic = _coords()
    chip = 2 * ix + iy
    chip_idx = jnp.stack([chip, ic]).astype(jnp.int32)
    core_idx = jnp.reshape(ic, (1,)).astype(jnp.int32)
    src_chunks = lax.dynamic_index_in_dim(jnp.asarray(_CHUNK_SRC), chip, 0, keepdims=False)
    pos_chunks = lax.dynamic_index_in_dim(jnp.asarray(_CHUNK_POS), chip, 0, keepdims=False)
    x0, target = x[0], loss_target[0]

    wt, mt, vt = (jnp.swapaxes(a, 1, 2) for a in (w_in, m_w_in, v_w_in))
    w0_mine, wloc_in1 = _cast_permute_w_in(wt, jnp.concatenate([chip_idx[:1], src_chunks]))
    c_all, wloc_out, w0 = _gather_inputs(c, w_out, w0_mine)
    wts = [w0.reshape(D_IN, D_MODEL), None]

    b_blk = lax.dynamic_slice_in_dim(b_ada, chip * W_ADA_BLK, W_ADA_BLK, axis=1)
    ada_part, cond = _ada_rows(c_all.reshape(N_DEV, D_MODEL), w_ada, b_blk)
    ada = jnp.moveaxis(_exchange_ada(ada_part), 0, 1).reshape(DEPTH, 3 * D_MODEL)
    shift = [ada[l:l + 1, 0:D_MODEL] for l in range(DEPTH)]
    scale = [ada[l:l + 1, D_MODEL:2 * D_MODEL] for l in range(DEPTH)]
    gate = [ada[l:l + 1, 2 * D_MODEL:] for l in range(DEPTH)]
    ng = [norm_gain[l:l + 1] for l in range(DEPTH)]

    qg2 = jnp.concatenate([q_gain, q_gain], axis=-1)
    kg2 = jnp.concatenate([k_gain, k_gain], axis=-1)
    ws_b = w_s.astype(BF16)
    wst_b = jnp.swapaxes(w_s, -1, -2).astype(BF16)
    bsp = jnp.repeat(jnp.swapaxes(b_s.reshape(DEPTH, 4, 2, BLOCK), -1, -2), HEAD_DIM, axis=-1)
    bias = jnp.asarray(_bias_table())

    def mix_args(l):
        return bias, sink[l], qg2[l:l + 1], kg2[l:l + 1], ws_b[l]

    w_out_shape = (W_OUT_BLK, D_MODEL)
    proj0, w1 = _proj_fwd(x0, ng[0], scale[0], shift[0], wts[0], "proj_fwd_0",
                          jobs=[_job_gather([(wloc_in1, None)], [(W_IN_BLK, D_MODEL)])])
    y0, *kept0, wo0, wo1 = _mix_fwd(proj0, *mix_args(0), bsp[0], "mix_fwd_0",
                            jobs=[_job_gather([(wloc_out, 0), (wloc_out, 1)], [w_out_shape, w_out_shape])])
    wts[1] = w1.reshape(D_IN, D_MODEL)
    wos = [_permute_w_out(w.reshape(D_MODEL, D_MODEL)) for w in (wo0, wo1)]
    x1, = _out_fwd(y0, x0, gate[0], wos[0], "out_fwd_0")
    proj1, = _proj_fwd(x1, ng[1], scale[1], shift[1], wts[1], "proj_fwd_1")
    y1, *kept1 = _mix_fwd(proj1, *mix_args(1), bsp[1], "mix_fwd_1")
    dx2, sq = _out_fwd_loss(y1, x1, gate[1], wos[1], target, "out_fwd_loss_1")

    def blocks_out(gw):
        return _unpermute_w_out(gw).reshape(N_CHIPS, W_OUT_BLK, D_MODEL)

    dy1, gwo1, dgate1 = _out_bwd(dx2, y1, gate[1], wos[1], "out_bwd_1")
    go1 = blocks_out(gwo1)
    dpb, dkv, p0, p2, dqg1, dkg1, dsink1, dws1, dbsp1, ro1 = _mix_bwd(
        dy1, proj1, *kept1, qg2[1:2], kg2[1:2], wst_b[1], "mix_bwd_1", jobs=[_job_swap(go1)])
    po1, = _pair_sum(go1, ro1, core_idx, "pair_sum_w_out_1")
    gwi1, dkvb, co1 = _w_in_grad(dpb, dkv, p0, p2, x1, ng[1], scale[1], shift[1], "w_in_grad_1", jobs=[_job_scatter(po1)])
    gi1 = gwi1.reshape(N_CHIPS, W_IN_BLK, D_MODEL)
    fo1 = _chip_sum(po1, co1, chip_idx, "chip_sum_w_out_1")
    dx1, dsh1, dsc1, dng1, grad_wo1, ri1 = _proj_bwd(dpb, dkvb, x1, dx2, ng[1], scale[1], wts[1], "proj_bwd_1",
                                                     jobs=[_job_join(fo1), _job_swap(gi1)])
    pi1, = _pair_sum(gi1, ri1, core_idx, "pair_sum_w_in_1")

    dy0, gwo0, dgate0 = _out_bwd(dx1, y0, gate[0], wos[0], "out_bwd_0")
    go0 = blocks_out(gwo0)
    dpb, dkv, p0, p2, dqg0, dkg0, dsink0, dws0, dbsp0, ci1, ro0 = _mix_bwd(
        dy0, proj0, *kept0, qg2[0:1], kg2[0:1], wst_b[0], "mix_bwd_0", jobs=[_job_scatter(pi1), _job_swap(go0)])
    fi1 = _chip_sum(pi1, ci1, chip_idx, "chip_sum_w_in_1")
    po0, = _pair_sum(go0, ro0, core_idx, "pair_sum_w_out_0")

    def bs_grad(dbsp):
        return jnp.swapaxes(dbsp[:, :, ::HEAD_DIM], -1, -2).reshape(8, BLOCK)

    small_g = dict(
        w_s=jnp.stack([dws0, dws1]), b_s=jnp.stack([bs_grad(dbsp0), bs_grad(dbsp1)]),
        q_gain=jnp.stack([dqg0[0, :HEAD_DIM], dqg1[0, :HEAD_DIM]]), k_gain=jnp.stack([dkg0[0, :HEAD_DIM], dkg1[0, :HEAD_DIM]]),
        sink=jnp.stack([dsink0[0, :N_HEADS], dsink1[0, :N_HEADS]]))
    gwi0, dkvb, grad_wi1, co0, gathered_a = _w_in_grad(
        dpb, dkv, p0, p2, x0, ng[0], scale[0], shift[0], "w_in_grad_0",
        jobs=[_job_join(fi1), _job_scatter(po0), _job_all_gather(_pack_rows(small_g, _SMALL_A))])
    gi0 = gwi0.reshape(N_CHIPS, W_IN_BLK, D_MODEL)
    fo0 = _chip_sum(po0, co0, chip_idx, "chip_sum_w_out_0")

    ri0, grad_wo0 = _comm([_job_swap(gi0), _job_join(fo0)], "swap_w_in_0")
    pi0, pi0_send = _pair_sum(gi0, ri0, core_idx, "pair_sum_w_in_0", send_dtype=BF16)
    dx0, dsh0, dsc0, dng0, ci0 = _proj_bwd(dpb, dkvb, x0, dx1, ng[0], scale[0], wts[0], "proj_bwd_0",
                                           jobs=[_job_scatter(pi0_send)])
    fi0 = _chip_sum(pi0, ci0, chip_idx, "chip_sum_w_in_0")
    grad_wi0, = _comm([_job_join(fi0)], "join_w_in_0")

    small_g.update(
        b_ada=jnp.stack([jnp.concatenate([dsh0, dsc0, dgate0], axis=-1)[0], jnp.concatenate([dsh1, dsc1, dgate1], axis=-1)[0]]),
        norm_gain=jnp.stack([dng0[0], dng1[0]]), sq_err=sq[0])
    none = jnp.zeros((1,), F32)
    small_w = dict(w_s=w_s, b_s=b_s, b_ada=b_ada, norm_gain=norm_gain, q_gain=q_gain, k_gain=k_gain, sink=sink, sq_err=none)
    small_m = dict(w_s=m_w_s, b_s=m_b_s, b_ada=m_b_ada, norm_gain=m_norm_gain, q_gain=m_q_gain, k_gain=m_k_gain, sink=m_sink,
                   sq_err=none)
    small_v = dict(w_s=v_w_s, b_s=v_b_s, b_ada=v_b_ada, norm_gain=v_norm_gain, q_gain=v_q_gain, k_gain=v_k_gain, sink=v_sink,
                   sq_err=none)
    gathered_b = _all_gather_rows(_pack_rows(small_g, _SMALL_B))
    packed = _small_sum_adamw(gathered_a.reshape(N_DEV, -1, 128), gathered_b.reshape(N_DEV, -1, 128),
                              _pack_small(small_w), _pack_small(small_m), _pack_small(small_v))
    shapes = {k: a.shape for k, a in small_w.items()}
    sg, sd, sm, sv = (_unpack_small(p, shapes) for p in packed)
    loss = 0.5 * sg["sq_err"][0]

    dada_all = gathered_b.reshape(N_DEV, -1, 128)[:, 0:DEPTH * 24].reshape(N_DEV, DEPTH, 3 * D_MODEL)
    dada_blk = jnp.moveaxis(lax.dynamic_slice_in_dim(dada_all, chip * W_ADA_BLK, W_ADA_BLK, axis=2), 0, 1)
    pad = BLOCK - N_DEV
    ada_out = _w_ada_grad_adamw(
        jnp.pad(cond.T, ((0, 0), (0, pad))).astype(BF16), jnp.pad(dada_blk, ((0, 0), (0, pad), (0, 0))),
        w_ada, m_w_ada, v_w_ada)

    in_out = [jnp.swapaxes(a, 1, 2) for a in _adamw_w_in(wt, grad_wi0, grad_wi1, mt, vt, pos_chunks)]
    out_out = _adamw_w_out(w_out, grad_wo0, grad_wo1, m_w_out, v_w_out)

    def ordered(k):
        small = (sg, sd, sm, sv)[k]
        return (ada_out[k], small["b_ada"], small["norm_gain"], in_out[k], small["q_gain"], small["k_gain"], small["sink"],
                small["w_s"], small["b_s"], out_out[k])

    return (loss, dx0[None], *ordered(0), *ordered(1), *ordered(2), *ordered(3))
```

```python
import numpy as np

import jax
import jax.numpy as jnp
from jax import lax
from jax.experimental import pallas as pl
from jax.experimental.pallas import tpu as pltpu

F32 = jnp.float32
BF16 = jnp.bfloat16

D_MODEL = 1024
DEPTH = 2
HEAD_DIM = 64
N_HEADS = 8
BLOCK = 128
SUB = 4
TILE = SUB * BLOCK
STACK = 8 * BLOCK
D_ATTN = 512
D_KV = 128
D_IN = 2816
N_CHIPS = 4
N_DEV = 8
W_IN_BLK = D_IN // N_CHIPS
W_OUT_BLK = D_MODEL // N_CHIPS
W_ADA_BLK = 3 * D_MODEL // N_CHIPS
CHUNK_ROWS = HEAD_DIM
N_CHUNKS = W_IN_BLK // CHUNK_ROWS
EPS = 1e-6
NEG_INF = -1e30

C_Q, C_K, C_V, C_GA, C_U, C_VG, C_GG = 0, 512, 640, 768, 1280, 1792, 2304

ADAM_LR = 0.001
ADAM_B1 = 0.9
ADAM_B2 = 0.999
ADAM_EPS = 1e-08
ADAM_WD = 0.01
ADAM_STEP = 10

MESH = pl.DeviceIdType.MESH
MIB = 1024 * 1024
ANY = pl.BlockSpec(memory_space=pl.ANY)
VMEM = pl.BlockSpec(memory_space=pltpu.VMEM)

NT_DIMS = (((1,), (1,)), ((), ()))
TN_DIMS = (((0,), (0,)), ((), ()))

_PAIR_ORDER = (0, 4, 1, 5, 2, 6, 3, 7)
_CHUNK_SRC = np.array([
    list(_PAIR_ORDER) + [8, 9, 10],
    [0] + [1 + h for h in _PAIR_ORDER] + [9, 10],
    list(range(N_CHUNKS)),
    list(range(N_CHUNKS)),
], np.int32)
_CHUNK_POS = np.argsort(_CHUNK_SRC, axis=1).astype(np.int32)


def _bias_table():
    i = np.arange(N_HEADS * BLOCK)[:, None]
    j = np.arange(3 * BLOCK)[None, :]
    dist = np.abs(j - BLOCK - (i % BLOCK))
    slope = 2.0 ** -(i // BLOCK + 1.0)
    inner = np.where(dist <= BLOCK, -(slope * dist), NEG_INF)
    first = np.where(j >= BLOCK, inner, NEG_INF)
    last = np.where(j < 2 * BLOCK, inner, NEG_INF)
    return np.stack([first, inner, last]).astype(np.float32)


def _full(shape):
    n = len(shape)
    return pl.BlockSpec(shape, lambda *_: (0,) * n)


def _sds(shape, dtype=F32):
    return jax.ShapeDtypeStruct(shape, dtype)


def _coords():
    return lax.axis_index("x"), lax.axis_index("y"), lax.axis_index("c")


def _other_chips(x, y):
    return [(1 - x, y), (x, 1 - y), (1 - x, 1 - y)]


def _in_hbm(*operands):
    return [pltpu.with_memory_space_constraint(a, pltpu.HBM) if a.size * a.dtype.itemsize >= MIB // 4 else a
            for a in operands]


def _remote(src, dst, sems, dev):
    return pltpu.make_async_remote_copy(src_ref=src, dst_ref=dst, send_sem=sems[0], recv_sem=sems[1],
                                        device_id=dev, device_id_type=MESH)


class _Job:
    def __init__(self, inputs, out_shapes, n_remote, n_local, make, then=None, in_place=False):
        self.inputs, self.out_shapes, self.n_remote, self.n_local, self.make = inputs, out_shapes, n_remote, n_local, make
        self.then = then
        self.in_place = in_place


def _job_aliases(jobs, in_base, out_base):
    aliases, a, b = {}, 0, 0
    for j in jobs:
        if j.in_place:
            aliases.update({in_base + a + k: out_base + b + k for k in range(len(j.inputs))})
        a, b = a + len(j.inputs), b + len(j.out_shapes)
    return aliases


def _job_copies(jobs, jin, jout, sems, second=False):
    send, recv, loc = sems
    res, a, b, r, l = [], 0, 0, 0, 0
    for j in jobs:
        build = j.then if second else j.make
        if build is not None:
            res += build(jin[a:a + len(j.inputs)], jout[b:b + len(j.out_shapes)],
                         lambda k, r=r: (send.at[r + k], recv.at[r + k]), lambda k, l=l: loc.at[l + k])
        a, b, r, l = a + len(j.inputs), b + len(j.out_shapes), r + j.n_remote, l + j.n_local
    return res


def _run(copies):
    for cp in copies:
        cp.start()
    for cp in copies:
        cp.wait()


def _job_gather(sources, shapes):
    n = len(sources)

    def make(ins, outs, rsem, lsem):
        x, y, c = _coords()
        j = 2 * x + y
        res = []
        for t, ((_, layer), src, dst) in enumerate(zip(sources, ins, outs)):
            src = src if layer is None else src.at[layer]
            res.append(pltpu.make_async_copy(src, dst.at[j], lsem(t)))
            for k, chip in enumerate(_other_chips(x, y)):
                res.append(_remote(src, dst.at[j], rsem(3 * t + k), (*chip, c)))
        return res

    return _Job([a for a, _ in sources], [_sds((N_CHIPS,) + s, BF16) for s in shapes], 3 * n, n, make)


def _job_swap(g):
    _, rows, cols = g.shape
    half = rows // 2

    def make(ins, outs, rsem, lsem):
        x, y, c = _coords()
        return [_remote(ins[0].at[:, pl.ds((1 - c) * half, half), :], outs[0], rsem(0), (x, y, 1 - c))]

    return _Job([g], [_sds((N_CHIPS, half, cols))], 1, 0, make)


def _job_scatter(p):
    def make(ins, outs, rsem, lsem):
        x, y, c = _coords()
        return [_remote(ins[0].at[2 * chip[0] + chip[1]], outs[0].at[k], rsem(k), (*chip, c))
                for k, chip in enumerate(_other_chips(x, y))]

    return _Job([p], [_sds((3,) + p.shape[1:], p.dtype)], 3, 0, make)


def _job_all_gather(blk):
    m_per = blk.shape[0]

    def rows(ref, px, py, pc):
        return ref.at[pl.ds((4 * px + 2 * py + pc) * m_per, m_per), :]

    def make(ins, outs, rsem, lsem):
        x, y, c = _coords()
        res = [pltpu.make_async_copy(ins[0], rows(outs[0], x, y, c), lsem(0)),
               _remote(ins[0], rows(outs[0], x, y, c), rsem(0), (x, y, 1 - c))]
        res += [_remote(ins[0], rows(outs[0], x, y, c), rsem(1 + k), (*chip, c)) for k, chip in enumerate(_other_chips(x, y))]
        return res

    def then(ins, outs, rsem, lsem):
        x, y, c = _coords()
        return [_remote(rows(outs[0], *chip, c), rows(outs[0], *chip, c), rsem(4 + k), (x, y, 1 - c))
                for k, chip in enumerate(_other_chips(x, y))]

    return _Job([blk], [_sds((N_DEV * m_per, blk.shape[1]), blk.dtype)], 7, 1, make, then)


def _job_join(f):
    half = f.shape[0] // 2

    def make(ins, outs, rsem, lsem):
        x, y, c = _coords()
        mine = pl.ds(c * half, half)
        return [_remote(ins[0].at[mine, :], outs[0].at[mine, :], rsem(0), (x, y, 1 - c))]

    return _Job([f], [_sds(f.shape, f.dtype)], 1, 0, make, in_place=True)


def _pallas(body, *, name, grid, in_specs, out_specs, out_shape, operands, vmem_mib, jobs=()):
    in_specs, out_specs, out_shape = list(in_specs), list(out_specs), list(out_shape)
    n_in, n_out = len(in_specs), len(out_specs)
    j_in = [a for j in jobs for a in j.inputs]
    j_out = [s for j in jobs for s in j.out_shapes]
    n_rem = max(1, sum(j.n_remote for j in jobs))
    n_loc = max(1, sum(j.n_local for j in jobs))
    scratch = [pltpu.SemaphoreType.DMA((n_rem,)), pltpu.SemaphoreType.DMA((n_rem,)),
               pltpu.SemaphoreType.DMA((n_loc,))] if jobs else []

    def wrapped(*refs):
        ins = refs[:n_in]
        jin = refs[n_in:n_in + len(j_in)]
        outs = refs[n_in + len(j_in):n_in + len(j_in) + n_out]
        jout = refs[n_in + len(j_in) + n_out:n_in + len(j_in) + n_out + len(j_out)]

        if jobs:
            first = last = None
            for d, n in enumerate(grid):
                f, e = pl.program_id(d) == 0, pl.program_id(d) == n - 1
                first, last = (f, e) if first is None else (first & f, last & e)

            @pl.when(first)
            def _():
                for cp in _job_copies(jobs, jin, jout, refs[-3:]):
                    cp.start()

        body(*ins, *outs)

        if jobs:
            @pl.when(last)
            def _():
                for cp in _job_copies(jobs, jin, jout, refs[-3:]):
                    cp.wait()
                _run(_job_copies(jobs, jin, jout, refs[-3:], second=True))

    return pl.pallas_call(
        wrapped, name=name, grid=grid,
        in_specs=in_specs + [ANY] * len(j_in), out_specs=out_specs + [ANY] * len(j_out),
        out_shape=out_shape + j_out, scratch_shapes=scratch, input_output_aliases=_job_aliases(jobs, n_in, n_out),
        compiler_params=pltpu.CompilerParams(dimension_semantics=("arbitrary",) * len(grid),
                                             vmem_limit_bytes=vmem_mib * MIB),
    )(*_in_hbm(*operands, *j_in))


def _comm(jobs, name):
    j_in = [a for j in jobs for a in j.inputs]
    j_out = [s for j in jobs for s in j.out_shapes]
    n_rem = max(1, sum(j.n_remote for j in jobs))
    n_loc = max(1, sum(j.n_local for j in jobs))

    def body(*refs):
        jin, jout = refs[:len(j_in)], refs[len(j_in):len(j_in) + len(j_out)]
        _run(_job_copies(jobs, jin, jout, refs[-3:]))
        _run(_job_copies(jobs, jin, jout, refs[-3:], second=True))

    return pl.pallas_call(
        body, name=name, in_specs=[ANY] * len(j_in), out_specs=[ANY] * len(j_out), out_shape=j_out,
        scratch_shapes=[pltpu.SemaphoreType.DMA((n_rem,)), pltpu.SemaphoreType.DMA((n_rem,)),
                        pltpu.SemaphoreType.DMA((n_loc,))],
        input_output_aliases=_job_aliases(jobs, 0, 0),
    )(*_in_hbm(*j_in))


def _sigmoid(x):
    return 1.0 / (1.0 + jnp.exp(-x))


def _lo_mask(shape):
    return lax.broadcasted_iota(jnp.int32, shape, len(shape) - 1) < HEAD_DIM


def _half_sum(x, lo):
    a = jnp.sum(jnp.where(lo, x, 0.0), axis=-1, keepdims=True)
    b = jnp.sum(jnp.where(lo, 0.0, x), axis=-1, keepdims=True)
    return jnp.where(lo, a, b)


def _half_rms_scale(x, lo):
    return lax.rsqrt(_half_sum(x * x, lo) * (1.0 / HEAD_DIM) + EPS)


def _stack_heads(pairs, lo):
    return jnp.concatenate([jnp.where(lo, t, 0.0) for t in pairs] + [jnp.where(lo, 0.0, t) for t in pairs], axis=0)


def _unstack_pair(stack, p, lo):
    return jnp.where(lo, stack[BLOCK * p:BLOCK * (p + 1)], stack[BLOCK * (4 + p):BLOCK * (5 + p)])


def _attention_probs(q_stack, kn, bias_ref, sink_ref):
    rows = N_HEADS * BLOCK
    s = lax.dot_general(q_stack, kn, NT_DIMS, preferred_element_type=F32) + bias_ref[...]
    sink = jnp.concatenate([jnp.full((BLOCK, BLOCK), sink_ref[h], F32) for h in range(N_HEADS)], axis=0)
    cols = [s[:, BLOCK * j:BLOCK * (j + 1)] for j in range(3)]
    top = jnp.max(jnp.maximum(jnp.maximum(cols[0], cols[1]), cols[2]), axis=-1, keepdims=True)
    m = jnp.maximum(jnp.broadcast_to(top, (rows, BLOCK)), sink)
    e = [jnp.exp(c - m) for c in cols]
    es = jnp.exp(sink - m)
    inv = 1.0 / (jnp.broadcast_to(jnp.sum((e[0] + e[1]) + e[2], axis=-1, keepdims=True), (rows, BLOCK)) + es)
    return jnp.concatenate([c * inv for c in e], axis=1), es * inv


def _kv_rows(cur_ref, pkv_ref, nkv_ref):
    k = jnp.concatenate([pkv_ref[:, 0:D_KV], cur_ref[:, C_K:C_K + D_KV], nkv_ref[:, 0:D_KV]], axis=0)
    v = jnp.concatenate([pkv_ref[:, D_KV:2 * D_KV], cur_ref[:, C_V:C_V + D_KV], nkv_ref[:, D_KV:2 * D_KV]], axis=0)
    return k, v


def _overlap_add(parts):
    blocks = []
    for j in range(SUB + 2):
        terms = [parts[b][BLOCK * (j - b):BLOCK * (j - b + 1)] for b in range(SUB) if 0 <= j - b <= 2]
        total = terms[0]
        for t in terms[1:]:
            total = total + t
        blocks.append(total)
    return jnp.concatenate(blocks, axis=0)


def _mix_specs(nt):
    cur = pl.BlockSpec((TILE, D_IN), lambda i: (i, 0))
    kv_col = C_K // (2 * D_KV)
    pkv = pl.BlockSpec((BLOCK, 2 * D_KV), lambda i: (jnp.maximum(i * SUB - 1, 0), kv_col))
    nkv = pl.BlockSpec((BLOCK, 2 * D_KV), lambda i: (jnp.minimum((i + 1) * SUB, nt * SUB - 1), kv_col))
    table = (None, N_HEADS * BLOCK, 3 * BLOCK)
    first = pl.BlockSpec(table, lambda i: (jnp.where(i == 0, 0, 1), 0, 0))
    inner = pl.BlockSpec(table, lambda i: (1, 0, 0))
    last = pl.BlockSpec(table, lambda i: (jnp.where(i == nt - 1, 2, 1), 0, 0))
    return cur, pkv, nkv, [first] + [inner] * (SUB - 2) + [last]


def _proj_fwd(x, ng, scale, shift, wt, name, jobs=()):
    s = x.shape[0]
    ts = min(512, s)

    def body(x_ref, ng_ref, sc_ref, sh_ref, w_ref, o_ref):
        xv = x_ref[...]
        r = lax.rsqrt(jnp.mean(xv * xv, axis=-1, keepdims=True) + EPS)
        h = ((xv * r) * ng_ref[...]) * (1.0 + sc_ref[...]) + sh_ref[...]
        o_ref[...] = lax.dot_general(h.astype(BF16), w_ref[...], NT_DIMS, preferred_element_type=F32)

    vec = _full((1, D_MODEL))
    return _pallas(
        body, name=name, grid=(s // ts,),
        in_specs=[pl.BlockSpec((ts, D_MODEL), lambda i: (i, 0)), vec, vec, vec, _full((D_IN, D_MODEL))],
        out_specs=[pl.BlockSpec((ts, D_IN), lambda i: (i, 0))], out_shape=[_sds((s, D_IN))],
        operands=(x, ng, scale, shift, wt), vmem_mib=48, jobs=jobs)


def _diagonal():
    return lax.broadcasted_iota(jnp.int32, (BLOCK, BLOCK), 0) == lax.broadcasted_iota(jnp.int32, (BLOCK, BLOCK), 1)


def _column_as_row(wide, eye):
    return jnp.sum(jnp.where(eye, wide, 0.0), axis=0, keepdims=True)


def _mix_fwd(proj, bias, sink, qg2, kg2, ws, bsp, name, jobs=()):
    s = proj.shape[0]
    nt = s // TILE

    def body(sink_ref, cur_ref, pkv_ref, nkv_ref, *rest):
        bias_refs = rest[:SUB]
        qg_ref, kg_ref, ws_ref, bsp_ref, y_ref, p_ref, ps_ref, attn_ref, sv_ref = rest[SUB:]
        lo = _lo_mask((BLOCK, BLOCK))
        eye = _diagonal()
        lo_kv = _lo_mask((TILE + 2 * BLOCK, BLOCK))
        k_all, v_all = _kv_rows(cur_ref, pkv_ref, nkv_ref)
        kn_all = ((k_all * _half_rms_scale(k_all, lo_kv)) * kg_ref[...]).astype(BF16)
        vb_all = v_all.astype(BF16)
        for b in range(SUB):
            rows = slice(BLOCK * b, BLOCK * (b + 1))
            window = slice(BLOCK * b, BLOCK * (b + 3))
            qn = []
            for p in range(4):
                q = cur_ref[rows, C_Q + BLOCK * p:C_Q + BLOCK * (p + 1)]
                qn.append(((q * _half_rms_scale(q, lo)) * qg_ref[...]) * 0.125)
            q_stack = _stack_heads(qn, lo).astype(BF16)
            prob, psink = _attention_probs(q_stack, kn_all[window], bias_refs[b], sink_ref)
            pb = prob.astype(BF16)
            p_ref[STACK * b:STACK * (b + 1), :] = pb
            ps_ref[N_HEADS * b:N_HEADS * (b + 1), :] = jnp.concatenate(
                [_column_as_row(psink[BLOCK * h:BLOCK * (h + 1)], eye) for h in range(N_HEADS)], axis=0)
            o_stack = jnp.dot(pb, vb_all[window], preferred_element_type=F32)
            for p in range(4):
                g = cur_ref[rows, C_GA + BLOCK * p:C_GA + BLOCK * (p + 1)]
                attn = _unstack_pair(o_stack, p, lo)
                attn_ref[rows, BLOCK * p:BLOCK * (p + 1)] = attn.astype(BF16)
                y_ref[rows, BLOCK * p:BLOCK * (p + 1)] = (attn * (g * _sigmoid(g))).astype(BF16)

        for p in range(4):
            cols = slice(C_VG + BLOCK * p, C_VG + BLOCK * (p + 1))
            vn = []
            for b in range(SUB):
                vg = cur_ref[BLOCK * b:BLOCK * (b + 1), cols]
                vn.append((vg * _half_rms_scale(vg, lo)).astype(BF16))
            vn = jnp.concatenate(vn, axis=1)
            sv_a = jnp.dot(ws_ref[2 * p], vn, preferred_element_type=F32)
            sv_b = jnp.dot(ws_ref[2 * p + 1], vn, preferred_element_type=F32)
            for b in range(SUB):
                rows = slice(BLOCK * b, BLOCK * (b + 1))
                lanes = slice(BLOCK * b, BLOCK * (b + 1))
                sv = jnp.where(lo, sv_a[:, lanes], sv_b[:, lanes]) + bsp_ref[p]
                sv_ref[rows, BLOCK * p:BLOCK * (p + 1)] = sv.astype(BF16)
                u = cur_ref[rows, C_U + BLOCK * p:C_U + BLOCK * (p + 1)]
                g = cur_ref[rows, C_GG + BLOCK * p:C_GG + BLOCK * (p + 1)]
                y_ref[rows, D_ATTN + BLOCK * p:D_ATTN + BLOCK * (p + 1)] = ((u * sv) * (g * _sigmoid(g))).astype(BF16)

    cur, pkv, nkv, bias_specs = _mix_specs(nt)
    nb = nt * SUB
    half = pl.BlockSpec((TILE, D_ATTN), lambda i: (i, 0))
    return _pallas(
        body, name=name, grid=(nt,),
        in_specs=[pl.BlockSpec(memory_space=pltpu.SMEM), cur, pkv, nkv, *bias_specs, _full((1, BLOCK)), _full((1, BLOCK)),
                  _full((8, BLOCK, BLOCK)), _full((4, BLOCK, BLOCK))],
        out_specs=[pl.BlockSpec((TILE, D_MODEL), lambda i: (i, 0)), pl.BlockSpec((SUB * STACK, 3 * BLOCK), lambda i: (i, 0)),
                   pl.BlockSpec((SUB * N_HEADS, BLOCK), lambda i: (i, 0)), half, half],
        out_shape=[_sds((s, D_MODEL), BF16), _sds((nb * STACK, 3 * BLOCK), BF16), _sds((nb * N_HEADS, BLOCK)),
                   _sds((s, D_ATTN), BF16), _sds((s, D_ATTN), BF16)],
        operands=(sink, proj, proj, proj, *([bias] * SUB), qg2, kg2, ws, bsp), vmem_mib=56, jobs=jobs)


def _out_fwd(y, x, gate, w_out, name):
    s = x.shape[0]
    ts = min(512, s)

    def body(y_ref, x_ref, g_ref, w_ref, o_ref):
        o_ref[...] = x_ref[...] + g_ref[...] * jnp.dot(y_ref[...], w_ref[...], preferred_element_type=F32)

    row = pl.BlockSpec((ts, D_MODEL), lambda i: (i, 0))
    return _pallas(
        body, name=name, grid=(s // ts,), in_specs=[row, row, _full((1, D_MODEL)), _full((D_MODEL, D_MODEL))],
        out_specs=[row], out_shape=[_sds((s, D_MODEL))], operands=(y, x, gate, w_out), vmem_mib=32)


def _out_fwd_loss(y, x, gate, w_out, target, name):
    s = x.shape[0]
    ts = min(512, s)

    def body(y_ref, x_ref, g_ref, w_ref, t_ref, dx_ref, sq_ref):
        @pl.when(pl.program_id(0) == 0)
        def _():
            sq_ref[...] = jnp.zeros_like(sq_ref)

        out = x_ref[...] + g_ref[...] * jnp.dot(y_ref[...], w_ref[...], preferred_element_type=F32)
        diff = out - t_ref[...]
        dx_ref[...] = diff * (1.0 / D_MODEL)
        per_token = jnp.sum(diff * diff, axis=-1, keepdims=True) * (1.0 / D_MODEL)
        sq_ref[...] += jnp.sum(per_token, axis=0, keepdims=True)

    row = pl.BlockSpec((ts, D_MODEL), lambda i: (i, 0))
    return _pallas(
        body, name=name, grid=(s // ts,), in_specs=[row, row, _full((1, D_MODEL)), _full((D_MODEL, D_MODEL)), row],
        out_specs=[row, _full((1, 1))], out_shape=[_sds((s, D_MODEL)), _sds((1, 1))],
        operands=(y, x, gate, w_out, target), vmem_mib=40)


def _out_bwd(dxo, y, gate, w_out, name, jobs=()):
    s = dxo.shape[0]
    ts = min(512, s)
    steps = s // ts

    def body(dx_ref, y_ref, g_ref, w_ref, dy_ref, gw_ref, dg_ref):
        @pl.when(pl.program_id(0) == 0)
        def _():
            gw_ref[...] = jnp.zeros_like(gw_ref)

        dx = dx_ref[...]
        dy_ref[...] = lax.dot_general((dx * g_ref[...]).astype(BF16), w_ref[...], NT_DIMS, preferred_element_type=F32)
        gw_ref[...] += lax.dot_general(y_ref[...], dx.astype(BF16), TN_DIMS, preferred_element_type=F32)

        @pl.when(pl.program_id(0) == steps - 1)
        def _():
            m = gw_ref[...]
            dg_ref[...] = jnp.sum(w_ref[...].astype(F32) * m, axis=0, keepdims=True)
            gw_ref[...] = m * g_ref[...]

    row = pl.BlockSpec((ts, D_MODEL), lambda i: (i, 0))
    return _pallas(
        body, name=name, grid=(s // ts,), in_specs=[row, row, _full((1, D_MODEL)), _full((D_MODEL, D_MODEL))],
        out_specs=[row, _full((D_MODEL, D_MODEL)), _full((1, D_MODEL))],
        out_shape=[_sds((s, D_MODEL)), _sds((D_MODEL, D_MODEL)), _sds((1, D_MODEL))],
        operands=(dxo, y, gate, w_out), vmem_mib=48, jobs=jobs)


def _mix_bwd(dy, proj, probs, psink, attn, sv, qg2, kg2, wst, name, jobs=()):
    s = proj.shape[0]
    nt = s // TILE

    def body(dy_ref, cur_ref, pkv_ref, nkv_ref, p_ref, ps_ref, attn_ref, sv_ref, qg_ref, kg_ref, wst_ref,
             dpb_ref, dkv_ref, p0_ref, p2_ref, dqg_ref, dkg_ref, dsink_ref, dws_ref, dbsp_ref):
        def put(rows, col, value):
            dpb_ref[rows, col:col + BLOCK] = value.astype(BF16)

        @pl.when(pl.program_id(0) == 0)
        def _():
            dqg_ref[...] = jnp.zeros_like(dqg_ref)
            dkg_ref[...] = jnp.zeros_like(dkg_ref)
            dsink_ref[...] = jnp.zeros_like(dsink_ref)
            dws_ref[...] = jnp.zeros_like(dws_ref)
            dbsp_ref[...] = jnp.zeros_like(dbsp_ref)

        lo = _lo_mask((BLOCK, BLOCK))
        lo_kv = _lo_mask((TILE + 2 * BLOCK, BLOCK))
        eye = _diagonal()
        lane_row = lax.broadcasted_iota(jnp.int32, (1, BLOCK), 1)
        qg = qg_ref[...]
        kg = kg_ref[...]

        k_all, v_all = _kv_rows(cur_ref, pkv_ref, nkv_ref)
        rk = _half_rms_scale(k_all, lo_kv)
        khat = k_all * rk
        kn_all = (khat * kg).astype(BF16)
        vb_all = v_all.astype(BF16)

        dkn_parts, dv_parts = [], []
        dsink = jnp.zeros((1, BLOCK), F32)
        dqg = jnp.zeros((1, BLOCK), F32)
        for b in range(SUB):
            rows = slice(BLOCK * b, BLOCK * (b + 1))
            window = slice(BLOCK * b, BLOCK * (b + 3))
            kn, vb = kn_all[window], vb_all[window]

            qhat, rq = [], []
            for p in range(4):
                q = cur_ref[rows, C_Q + BLOCK * p:C_Q + BLOCK * (p + 1)]
                r = _half_rms_scale(q, lo)
                rq.append(r)
                qhat.append(q * r)
            q_stack = _stack_heads([(qh * qg) * 0.125 for qh in qhat], lo).astype(BF16)
            pb = p_ref[STACK * b:STACK * (b + 1), :]
            prob = pb.astype(F32)

            dout = []
            for p in range(4):
                g = cur_ref[rows, C_GA + BLOCK * p:C_GA + BLOCK * (p + 1)]
                sg = _sigmoid(g)
                dya = dy_ref[rows, BLOCK * p:BLOCK * (p + 1)]
                attn = attn_ref[rows, BLOCK * p:BLOCK * (p + 1)]
                put(rows, C_GA + BLOCK * p, dya * attn * (sg * (1.0 + g * (1.0 - sg))))
                dout.append(dya * (g * sg))
            do_stack = _stack_heads(dout, lo).astype(BF16)
            dp = lax.dot_general(do_stack, vb, NT_DIMS, preferred_element_type=F32)
            delta = jnp.sum(prob * dp, axis=-1, keepdims=True)
            dsb = (prob * (dp - delta)).astype(BF16)

            for h in range(N_HEADS):
                delta_row = _column_as_row(jnp.broadcast_to(delta[BLOCK * h:BLOCK * (h + 1)], (BLOCK, BLOCK)), eye)
                tot = jnp.sum(ps_ref[N_HEADS * b + h:N_HEADS * b + h + 1, :] * delta_row, axis=-1, keepdims=True)
                dsink = dsink - jnp.where(lane_row == h, tot, 0.0)

            dq_stack = jnp.dot(dsb, kn, preferred_element_type=F32) * 0.125
            dkn_parts.append(lax.dot_general(dsb, q_stack, TN_DIMS, preferred_element_type=F32))
            dv_parts.append(lax.dot_general(pb, do_stack, TN_DIMS, preferred_element_type=F32))

            for p in range(4):
                dqn = _unstack_pair(dq_stack, p, lo)
                qh = qhat[p]
                dqg = dqg + jnp.sum(dqn * qh, axis=0, keepdims=True)
                dqh = dqn * qg
                mean = _half_sum(dqh * qh, lo) * (1.0 / HEAD_DIM)
                put(rows, C_Q + BLOCK * p, rq[p] * (dqh - qh * mean))

        dws_new, dbs_new = [], []
        for p in range(4):
            rs, vnfs, vns, dsvs, dbs = [], [], [], [], None
            for b in range(SUB):
                rows = slice(BLOCK * b, BLOCK * (b + 1))
                vg = cur_ref[rows, C_VG + BLOCK * p:C_VG + BLOCK * (p + 1)]
                r = _half_rms_scale(vg, lo)
                vnf = vg * r
                sv = sv_ref[rows, BLOCK * p:BLOCK * (p + 1)]
                u = cur_ref[rows, C_U + BLOCK * p:C_U + BLOCK * (p + 1)]
                g = cur_ref[rows, C_GG + BLOCK * p:C_GG + BLOCK * (p + 1)]
                sg = _sigmoid(g)
                dym = dy_ref[rows, D_ATTN + BLOCK * p:D_ATTN + BLOCK * (p + 1)]
                put(rows, C_GG + BLOCK * p, dym * (u * sv) * (sg * (1.0 + g * (1.0 - sg))))
                dgm = dym * (g * sg)
                put(rows, C_U + BLOCK * p, dgm * sv)
                dsv = dgm * u
                term = jnp.where(lo, jnp.sum(jnp.where(lo, dsv, 0.0), axis=-1, keepdims=True),
                                 jnp.sum(jnp.where(lo, 0.0, dsv), axis=-1, keepdims=True))
                dbs = term if dbs is None else dbs + term
                rs.append(r)
                vnfs.append(vnf)
                vns.append(vnf.astype(BF16))
                dsvs.append(dsv)
            dbs_new.append(dbs)
            vn = jnp.concatenate(vns, axis=1)
            dsv = jnp.concatenate(dsvs, axis=1)
            lo_t = (lax.broadcasted_iota(jnp.int32, dsv.shape, 1) & (BLOCK - 1)) < HEAD_DIM
            dws_new.append(lax.dot_general(jnp.where(lo_t, dsv, 0.0).astype(BF16), vn, NT_DIMS, preferred_element_type=F32))
            dws_new.append(lax.dot_general(jnp.where(lo_t, 0.0, dsv).astype(BF16), vn, NT_DIMS, preferred_element_type=F32))
            dsvb = dsv.astype(BF16)
            dvn_a = jnp.dot(wst_ref[2 * p], dsvb, preferred_element_type=F32)
            dvn_b = jnp.dot(wst_ref[2 * p + 1], dsvb, preferred_element_type=F32)
            for b in range(SUB):
                lanes = slice(BLOCK * b, BLOCK * (b + 1))
                dvn = jnp.where(lo, dvn_a[:, lanes], dvn_b[:, lanes])
                mean = _half_sum(dvn * vnfs[b], lo) * (1.0 / HEAD_DIM)
                put(slice(BLOCK * b, BLOCK * (b + 1)), C_VG + BLOCK * p, rs[b] * (dvn - vnfs[b] * mean))

        dsink_ref[...] += dsink
        dqg = jnp.broadcast_to(dqg, (8, BLOCK))
        dqg_ref[...] += dqg + pltpu.roll(dqg, HEAD_DIM, 1)

        dkn = _overlap_add(dkn_parts)
        dv = _overlap_add(dv_parts)
        dkg = jnp.broadcast_to(jnp.sum(dkn * khat, axis=0, keepdims=True), (8, BLOCK))
        dkg_ref[...] += dkg + pltpu.roll(dkg, HEAD_DIM, 1)
        dkh = dkn * kg
        dk = rk * (dkh - khat * (_half_sum(dkh * khat, lo_kv) * (1.0 / HEAD_DIM)))
        dpb_ref[:, C_K:C_GA] = jnp.zeros((TILE, 2 * D_KV), BF16)
        dkv_ref[:, 0:D_KV] = dk[BLOCK:BLOCK + TILE]
        dkv_ref[:, D_KV:2 * D_KV] = dv[BLOCK:BLOCK + TILE]
        p0_ref[:, 0:D_KV] = dk[0:BLOCK]
        p0_ref[:, D_KV:2 * D_KV] = dv[0:BLOCK]
        p2_ref[:, 0:D_KV] = dk[BLOCK + TILE:]
        p2_ref[:, D_KV:2 * D_KV] = dv[BLOCK + TILE:]
        for g, new in enumerate(dws_new):
            dws_ref[g] += new
        for p, new in enumerate(dbs_new):
            dbsp_ref[p] += new

    cur, pkv, nkv, _ = _mix_specs(nt)
    kv_blk = (BLOCK, 2 * D_KV)
    half = pl.BlockSpec((TILE, D_ATTN), lambda i: (i, 0))
    return _pallas(
        body, name=name, grid=(nt,),
        in_specs=[pl.BlockSpec((TILE, D_MODEL), lambda i: (i, 0)), cur, pkv, nkv,
                  pl.BlockSpec((SUB * STACK, 3 * BLOCK), lambda i: (i, 0)), pl.BlockSpec((SUB * N_HEADS, BLOCK), lambda i: (i, 0)),
                  half, half, _full((1, BLOCK)), _full((1, BLOCK)), _full((8, BLOCK, BLOCK))],
        out_specs=[cur, pl.BlockSpec((TILE, 2 * D_KV), lambda i: (i, 0)),
                   pl.BlockSpec(kv_blk, lambda i: ((i + nt - 1) % nt, 0)),
                   pl.BlockSpec(kv_blk, lambda i: ((i + 1) % nt, 0)),
                   _full((8, BLOCK)), _full((8, BLOCK)), _full((1, BLOCK)),
                   _full((8, BLOCK, BLOCK)), _full((4, BLOCK, BLOCK))],
        out_shape=[_sds((s, D_IN), BF16), _sds((s, 2 * D_KV)), _sds((nt * BLOCK, 2 * D_KV)), _sds((nt * BLOCK, 2 * D_KV)),
                   _sds((8, BLOCK)), _sds((8, BLOCK)), _sds((1, BLOCK)),
                   _sds((8, BLOCK, BLOCK)), _sds((4, BLOCK, BLOCK))],
        operands=(dy, proj, proj, proj, probs, psink, attn, sv, qg2, kg2, wst), vmem_mib=56, jobs=jobs)


def _w_in_grad(dpb, dkv, p0, p2, x, ng, scale, shift, name, jobs=()):
    s = x.shape[0]
    ts = min(2 * TILE, s)
    tiles = ts // TILE

    def body(dpb_ref, dkv_ref, p0_ref, p2_ref, x_ref, ng_ref, sc_ref, sh_ref, gw_ref, dkvb_ref):
        @pl.when(pl.program_id(0) == 0)
        def _():
            gw_ref[...] = jnp.zeros_like(gw_ref)

        xv = x_ref[...]
        r = lax.rsqrt(jnp.mean(xv * xv, axis=-1, keepdims=True) + EPS)
        h = (((xv * r) * ng_ref[...]) * (1.0 + sc_ref[...]) + sh_ref[...]).astype(BF16)
        for t in range(tiles):
            halo = slice(BLOCK * t, BLOCK * (t + 1))
            first = slice(TILE * t, TILE * t + BLOCK)
            last = slice(TILE * (t + 1) - BLOCK, TILE * (t + 1))
            dkvb_ref[first, :] = (dkv_ref[first, :] + p2_ref[halo, :]).astype(BF16)
            if SUB > 2:
                inner = slice(TILE * t + BLOCK, TILE * (t + 1) - BLOCK)
                dkvb_ref[inner, :] = dkv_ref[inner, :].astype(BF16)
            dkvb_ref[last, :] = (dkv_ref[last, :] + p0_ref[halo, :]).astype(BF16)
        gw_ref[...] += lax.dot_general(dpb_ref[...], h, TN_DIMS, preferred_element_type=F32)
        gw_ref[C_K:C_GA, :] += lax.dot_general(dkvb_ref[...], h, TN_DIMS, preferred_element_type=F32)

    kv = pl.BlockSpec((ts, 2 * D_KV), lambda i: (i, 0))
    halo = pl.BlockSpec((tiles * BLOCK, 2 * D_KV), lambda i: (i, 0))
    vec = _full((1, D_MODEL))
    return _pallas(
        body, name=name, grid=(s // ts,),
        in_specs=[pl.BlockSpec((ts, D_IN), lambda i: (i, 0)), kv, halo, halo,
                  pl.BlockSpec((ts, D_MODEL), lambda i: (i, 0)), vec, vec, vec],
        out_specs=[_full((D_IN, D_MODEL)), kv], out_shape=[_sds((D_IN, D_MODEL)), _sds((s, 2 * D_KV), BF16)],
        operands=(dpb, dkv, p0, p2, x, ng, scale, shift), vmem_mib=56, jobs=jobs)


def _proj_bwd(dpb, dkvb, x, dxo, ng, scale, wt, name, jobs=()):
    s = x.shape[0]
    ts = min(512, s)

    def body(dpb_ref, dkvb_ref, x_ref, dxo_ref, ng_ref, sc_ref, w_ref, dxi_ref, dsh_ref, dsc_ref, dng_ref):
        @pl.when(pl.program_id(0) == 0)
        def _():
            dsh_ref[...] = jnp.zeros_like(dsh_ref)
            dsc_ref[...] = jnp.zeros_like(dsc_ref)
            dng_ref[...] = jnp.zeros_like(dng_ref)

        dh = (jnp.dot(dpb_ref[...], w_ref[...], preferred_element_type=F32)
              + jnp.dot(dkvb_ref[...], w_ref[C_K:C_GA, :], preferred_element_type=F32))

        xv = x_ref[...]
        r = lax.rsqrt(jnp.mean(xv * xv, axis=-1, keepdims=True) + EPS)
        xn = xv * r
        ngv = ng_ref[...]
        sc1 = 1.0 + sc_ref[...]
        dsh_ref[...] += jnp.sum(dh, axis=0, keepdims=True)
        dsc_ref[...] += jnp.sum(dh * (xn * ngv), axis=0, keepdims=True)
        dh1 = dh * sc1
        dng_ref[...] += jnp.sum(dh1 * xn, axis=0, keepdims=True)
        dxn = dh1 * ngv
        dxi_ref[...] = r * (dxn - xn * jnp.mean(dxn * xn, axis=-1, keepdims=True)) + dxo_ref[...]

    row = pl.BlockSpec((ts, D_MODEL), lambda i: (i, 0))
    vec = _full((1, D_MODEL))
    return _pallas(
        body, name=name, grid=(s // ts,),
        in_specs=[pl.BlockSpec((ts, D_IN), lambda i: (i, 0)), pl.BlockSpec((ts, 2 * D_KV), lambda i: (i, 0)),
                  row, row, vec, vec, _full((D_IN, D_MODEL))],
        out_specs=[row, vec, vec, vec],
        out_shape=[_sds((s, D_MODEL)), _sds((1, D_MODEL)), _sds((1, D_MODEL)), _sds((1, D_MODEL))],
        operands=(dpb, dkvb, x, dxo, ng, scale, wt), vmem_mib=48, jobs=jobs)


def _pair_sum(g, r, c_idx, name, send_dtype=None):
    _, rows, cols = g.shape
    half = rows // 2

    def body(c_ref, g_ref, r_ref, o_ref, *narrow):
        total = g_ref[...] + r_ref[...]
        o_ref[...] = total
        for n_ref in narrow:
            n_ref[...] = total.astype(n_ref.dtype)

    blk = (None, half, cols)
    out_blk = pl.BlockSpec(blk, lambda j, c: (j, 0, 0))
    shapes = [_sds((N_CHIPS, half, cols))] + ([_sds((N_CHIPS, half, cols), send_dtype)] if send_dtype else [])
    return pl.pallas_call(
        body, name=name,
        grid_spec=pltpu.PrefetchScalarGridSpec(
            num_scalar_prefetch=1, grid=(N_CHIPS,),
            in_specs=[pl.BlockSpec(blk, lambda j, c: (j, c[0], 0)), out_blk], out_specs=[out_blk] * len(shapes)),
        out_shape=shapes,
        compiler_params=pltpu.CompilerParams(dimension_semantics=("arbitrary",), vmem_limit_bytes=32 * MIB),
    )(*_in_hbm(c_idx, g, r))


def _chip_sum(p, r, place, name):
    _, rows, cols = p.shape
    tr = rows // 2

    def body(j_ref, p_ref, r_ref, o_ref):
        o_ref[...] = ((p_ref[...] + r_ref[0].astype(F32)) + r_ref[1].astype(F32)) + r_ref[2].astype(F32)

    return pl.pallas_call(
        body, name=name,
        grid_spec=pltpu.PrefetchScalarGridSpec(
            num_scalar_prefetch=1, grid=(2,),
            in_specs=[pl.BlockSpec((None, tr, cols), lambda t, j: (j[0], t, 0)),
                      pl.BlockSpec((3, tr, cols), lambda t, j: (0, t, 0))],
            out_specs=pl.BlockSpec((tr, cols), lambda t, j: (2 * j[1] + t, 0))),
        out_shape=_sds((2 * rows, cols)),
        compiler_params=pltpu.CompilerParams(dimension_semantics=("arbitrary",), vmem_limit_bytes=32 * MIB),
    )(*_in_hbm(place, p, r))


def _cast_permute_w_in(wt, place_chunks):
    def body(t_ref, w_ref, w0_ref, w1_ref):
        def cast_into(o_ref):
            for t in range(N_CHUNKS):
                src = pl.multiple_of(t_ref[1 + t] * CHUNK_ROWS, CHUNK_ROWS)
                o_ref[CHUNK_ROWS * t:CHUNK_ROWS * (t + 1), :] = w_ref[pl.ds(src, CHUNK_ROWS), :].astype(BF16)

        @pl.when(pl.program_id(0) == 0)
        def _():
            cast_into(w0_ref)

        @pl.when(pl.program_id(0) == 1)
        def _():
            cast_into(w1_ref)

    return pl.pallas_call(
        body, name="cast_permute_w_in",
        grid_spec=pltpu.PrefetchScalarGridSpec(
            num_scalar_prefetch=1, grid=(DEPTH,),
            in_specs=[pl.BlockSpec((None, W_IN_BLK, D_MODEL), lambda l, tbl: (l, 0, 0))],
            out_specs=[pl.BlockSpec((None, W_IN_BLK, D_MODEL), lambda l, tbl: (tbl[0], 0, 0)),
                       pl.BlockSpec((W_IN_BLK, D_MODEL), lambda l, tbl: (0, 0))]),
        out_shape=[_sds((N_CHIPS, W_IN_BLK, D_MODEL), BF16), _sds((W_IN_BLK, D_MODEL), BF16)],
        compiler_params=pltpu.CompilerParams(dimension_semantics=("arbitrary",), vmem_limit_bytes=32 * MIB),
    )(*_in_hbm(place_chunks, wt))


def _gather_inputs(c, w_out, w0):
    half = W_IN_BLK // 2

    def body(c_ref, wout_ref, mine_ref, call_ref, woutb_ref, w0_ref, send_sems, recv_sems):
        x, y, cc = _coords()
        j = 2 * x + y
        b = 2 * j + cc
        sib = (x, y, 1 - cc)
        woutb_ref[...] = wout_ref[...].astype(BF16)
        call_ref[b] = c_ref[...]
        chips = _other_chips(x, y)

        def sems(k):
            return send_sems.at[k], recv_sems.at[k]

        def half_rows(chip_index):
            return w0_ref.at[chip_index, pl.ds(cc * half, half), :]

        first = [_remote(mine_ref.at[j, pl.ds(cc * half, half), :], half_rows(j), sems(k), (*chip, cc))
                 for k, chip in enumerate(chips)]
        k = 3
        rest = []
        for fx in (0, 1):
            for fy in (0, 1):
                for fc in (0, 1):
                    if fx or fy or fc:
                        dev = (1 - x if fx else x, 1 - y if fy else y, 1 - cc if fc else cc)
                        rest.append(_remote(call_ref.at[b], call_ref.at[b], sems(k), dev))
                        k += 1
        for cp in first + rest:
            cp.start()
        passed = []
        for k, chip in enumerate(chips):
            jk = 2 * chip[0] + chip[1]
            first[k].wait_recv()
            passed.append(_remote(half_rows(jk), half_rows(jk), sems(10 + k), sib))
            passed[k].start()
        for cp in first:
            cp.wait_send()
        for cp in rest + passed:
            cp.wait()

    return pl.pallas_call(
        body, name="gather_inputs", in_specs=[VMEM, VMEM, ANY], out_specs=[VMEM, VMEM, ANY],
        out_shape=[_sds((N_DEV, 1, D_MODEL)), _sds((DEPTH, W_OUT_BLK, D_MODEL), BF16),
                   _sds((N_CHIPS, W_IN_BLK, D_MODEL), BF16)],
        scratch_shapes=[pltpu.SemaphoreType.DMA((13,)), pltpu.SemaphoreType.DMA((13,))],
        input_output_aliases={2: 2},
        compiler_params=pltpu.CompilerParams(vmem_limit_bytes=32 * MIB),
    )(c, w_out, w0)


def _ada_rows(c_all, w_ada, b_blk):
    def body(c_ref, w_ref, b_ref, o_ref, cond_ref):
        cv = c_ref[...]
        cond = (cv * _sigmoid(cv)).astype(BF16)
        cond_ref[...] = cond.astype(F32)
        for l in range(DEPTH):
            o_ref[:, l, :] = jnp.dot(cond, w_ref[l].astype(BF16), preferred_element_type=F32) + b_ref[l:l + 1, :]

    return pl.pallas_call(
        body, name="ada_rows", in_specs=[VMEM, VMEM, VMEM], out_specs=[VMEM, VMEM],
        out_shape=[_sds((N_DEV, DEPTH, W_ADA_BLK)), _sds((N_DEV, D_MODEL))],
        compiler_params=pltpu.CompilerParams(vmem_limit_bytes=32 * MIB),
    )(c_all, w_ada, b_blk)


def _exchange_ada(part):
    def body(part_ref, out_ref, send_sems, recv_sems):
        x, y, cc = _coords()
        j = 2 * x + y
        out_ref[j] = part_ref[2 * j + cc]
        copies = []
        for k, chip in enumerate(_other_chips(x, y)):
            b_dst = 4 * chip[0] + 2 * chip[1] + cc
            copies.append(_remote(part_ref.at[b_dst], out_ref.at[j], (send_sems.at[k], recv_sems.at[k]), (*chip, cc)))
        for cp in copies:
            cp.start()
        for cp in copies:
            cp.wait()

    return pl.pallas_call(
        body, name="exchange_ada", in_specs=[VMEM], out_specs=VMEM,
        out_shape=_sds((N_CHIPS, DEPTH, W_ADA_BLK)),
        scratch_shapes=[pltpu.SemaphoreType.DMA((3,)), pltpu.SemaphoreType.DMA((3,))],
    )(part)


def _all_gather_rows(blk):
    m_per, n = blk.shape

    def body(x_ref, out_ref, send_sems, recv_sems, local_sem):
        x, y, c = _coords()
        me, sibling = (x, y, c), (x, y, 1 - c)
        chips = _other_chips(x, y)

        def rows(px, py, pc):
            return out_ref.at[pl.ds((4 * px + 2 * py + pc) * m_per, m_per), :]

        def copy(k, block, to, src=None):
            return _remote(rows(*block) if src is None else src, rows(*block), (send_sems.at[k], recv_sems.at[k]), to)

        mine = pltpu.make_async_copy(x_ref, rows(*me), local_sem)
        mine.start()
        first = [copy(0, me, sibling, src=x_ref)]
        first += [copy(1 + j, me, (*chip, c), src=x_ref) for j, chip in enumerate(chips)]
        for cp in first:
            cp.start()
        passed = [copy(4 + j, (*chip, c), sibling) for j, chip in enumerate(chips)]
        for j, chip in enumerate(chips):
            copy(1 + j, (*chip, c), me).wait_recv()
            passed[j].start()
        copy(0, sibling, me).wait_recv()
        for j, chip in enumerate(chips):
            copy(4 + j, (*chip, 1 - c), me).wait_recv()
        for cp in first + passed:
            cp.wait_send()
        mine.wait()

    return pl.pallas_call(
        body, name="all_gather_small", in_specs=[VMEM], out_specs=VMEM,
        out_shape=_sds((N_DEV * m_per, n), blk.dtype),
        scratch_shapes=[pltpu.SemaphoreType.DMA((7,)), pltpu.SemaphoreType.DMA((7,)), pltpu.SemaphoreType.DMA],
        compiler_params=pltpu.CompilerParams(vmem_limit_bytes=32 * MIB),
    )(blk)


def _adamw_math(w, g, m, v):
    m = ADAM_B1 * m + (1.0 - ADAM_B1) * g
    v = ADAM_B2 * v + (1.0 - ADAM_B2) * (g * g)
    m_hat = m / (1.0 - ADAM_B1 ** ADAM_STEP)
    v_hat = v / (1.0 - ADAM_B2 ** ADAM_STEP)
    delta = -ADAM_LR * (m_hat / (jnp.sqrt(v_hat) + ADAM_EPS) + ADAM_WD * w)
    return delta, m, v


def _adamw_w_in(w, g0, g1, m, v, pos_chunks):
    def body(t_ref, w_ref, g0_ref, g1_ref, m_ref, v_ref, g_ref, d_ref, nm_ref, nv_ref):
        for l, src in enumerate((g0_ref, g1_ref)):
            g = src[...]
            g_ref[l] = g
            d_ref[l], nm_ref[l], nv_ref[l] = _adamw_math(w_ref[l], g, m_ref[l], v_ref[l])

    nat = pl.BlockSpec((DEPTH, CHUNK_ROWS, D_MODEL), lambda t, tbl: (0, t, 0))
    per = pl.BlockSpec((CHUNK_ROWS, D_MODEL), lambda t, tbl: (tbl[t], 0))
    return pl.pallas_call(
        body, name="adamw_w_in",
        grid_spec=pltpu.PrefetchScalarGridSpec(num_scalar_prefetch=1, grid=(N_CHUNKS,),
                                               in_specs=[nat, per, per, nat, nat], out_specs=[nat] * 4),
        out_shape=[_sds(w.shape)] * 4,
        compiler_params=pltpu.CompilerParams(dimension_semantics=("arbitrary",)),
    )(*_in_hbm(pos_chunks, w, g0, g1, m, v))


def _adamw_w_out(w, g0, g1, m, v):
    def body(w_ref, g0_ref, g1_ref, m_ref, v_ref, g_ref, d_ref, nm_ref, nv_ref):
        g = jnp.where(pl.program_id(0) == 0, g0_ref[...], g1_ref[...])
        g_ref[...] = g
        d_ref[...], nm_ref[...], nv_ref[...] = _adamw_math(w_ref[...], g, m_ref[...], v_ref[...])

    blk = pl.BlockSpec((None, W_OUT_BLK, D_MODEL), lambda l: (l, 0, 0))
    gblk = _full((W_OUT_BLK, D_MODEL))
    return pl.pallas_call(
        body, name="adamw_w_out", grid=(DEPTH,), in_specs=[blk, gblk, gblk, blk, blk], out_specs=[blk] * 4,
        out_shape=[_sds(w.shape)] * 4,
        compiler_params=pltpu.CompilerParams(dimension_semantics=("arbitrary",), vmem_limit_bytes=32 * MIB),
    )(*_in_hbm(w, g0, g1, m, v))


def _w_ada_grad_adamw(cond_t, dada, w, m, v):
    _, rows, cols = w.shape
    tr = 256

    def body(ct_ref, da_ref, w_ref, m_ref, v_ref, g_ref, d_ref, nm_ref, nv_ref):
        g = jnp.dot(ct_ref[...], da_ref[...].astype(BF16), preferred_element_type=F32)
        g_ref[...] = g
        d_ref[...], nm_ref[...], nv_ref[...] = _adamw_math(w_ref[...], g, m_ref[...], v_ref[...])

    blk = pl.BlockSpec((None, tr, cols), lambda l, t: (l, t, 0))
    return pl.pallas_call(
        body, name="w_ada_grad_adamw", grid=(DEPTH, rows // tr),
        in_specs=[pl.BlockSpec((tr, BLOCK), lambda l, t: (t, 0)), pl.BlockSpec((None, BLOCK, cols), lambda l, t: (l, 0, 0)),
                  blk, blk, blk],
        out_specs=[blk] * 4, out_shape=[_sds(w.shape)] * 4,
        compiler_params=pltpu.CompilerParams(dimension_semantics=("arbitrary", "arbitrary"), vmem_limit_bytes=32 * MIB),
    )(*_in_hbm(cond_t, dada, w, m, v))


def _small_sum_adamw(gathered_a, gathered_b, w, m, v):
    def body(a_ref, b_ref, w_ref, m_ref, v_ref, g_ref, d_ref, nm_ref, nv_ref):
        def total(ref):
            g = ref[0]
            for b in range(1, N_DEV):
                g = g + ref[b]
            return g

        g = jnp.concatenate([total(a_ref), total(b_ref)], axis=0)
        g_ref[...] = g
        d_ref[...], nm_ref[...], nv_ref[...] = _adamw_math(w_ref[...], g, m_ref[...], v_ref[...])

    return pl.pallas_call(
        body, name="small_sum_adamw", in_specs=[VMEM] * 5, out_specs=[VMEM] * 4, out_shape=[_sds(w.shape)] * 4,
        compiler_params=pltpu.CompilerParams(vmem_limit_bytes=48 * MIB),
    )(gathered_a, gathered_b, w, m, v)


_SMALL_A = (("w_s", DEPTH * 8 * BLOCK), ("b_s", DEPTH * 8), ("q_gain", 1), ("k_gain", 1), ("sink", 1))
_SMALL_B = (("b_ada", DEPTH * 24), ("norm_gain", DEPTH * 8), ("sq_err", 1))


def _pack_rows(parts, layout):
    rows = []
    for name, n in layout:
        flat = parts[name].reshape(-1)
        rows.append(jnp.pad(flat, (0, n * 128 - flat.shape[0])).reshape(n, 128))
    n_rows = sum(n for _, n in layout)
    if n_rows % 8:
        rows.append(jnp.zeros((-n_rows % 8, 128), F32))
    return jnp.concatenate(rows, axis=0)


def _pack_small(parts):
    return jnp.concatenate([_pack_rows(parts, _SMALL_A), _pack_rows(parts, _SMALL_B)], axis=0)


def _unpack_small(packed, shapes):
    out, r0 = {}, 0
    for layout in (_SMALL_A, _SMALL_B):
        for name, n in layout:
            size = 1
            for d in shapes[name]:
                size *= d
            out[name] = packed[r0:r0 + n].reshape(-1)[:size].reshape(shapes[name])
            r0 += n
        r0 += -r0 % 8
    return out


def _permute_heads(a, axis):
    shp = a.shape
    a = a.reshape(shp[:axis] + (2, 4, HEAD_DIM) + shp[axis + 1:])
    a = jnp.swapaxes(a, axis, axis + 1)
    return a.reshape(shp)


def _unpermute_heads(a, axis):
    shp = a.shape
    a = a.reshape(shp[:axis] + (4, 2, HEAD_DIM) + shp[axis + 1:])
    a = jnp.swapaxes(a, axis, axis + 1)
    return a.reshape(shp)


def _permute_w_out(w):
    return jnp.concatenate([_permute_heads(w[:D_ATTN], 0), w[D_ATTN:]], axis=0)


def _unpermute_w_out(w):
    return jnp.concatenate([_unpermute_heads(w[:D_ATTN], 0), w[D_ATTN:]], axis=0)


def kernel(x, c, w_ada, b_ada, norm_gain, w_in, q_gain, k_gain, sink, w_s, b_s, w_out, loss_target, m_w_ada, m_b_ada, m_norm_gain, m_w_in, m_q_gain, m_k_gain, m_sink, m_w_s, m_b_s, m_w_out, v_w_ada, v_b_ada, v_norm_gain, v_w_in, v_q_gain, v_k_gain, v_sink, v_w_s, v_b_s, v_w_out):
    ix, iy, ic = _coords()
    chip = 2 * ix + iy
    chip_idx = jnp.stack([chip, ic]).astype(jnp.int32)
    core_idx = jnp.reshape(ic, (1,)).astype(jnp.int32)
    src_chunks = lax.dynamic_index_in_dim(jnp.asarray(_CHUNK_SRC), chip, 0, keepdims=False)
    pos_chunks = lax.dynamic_index_in_dim(jnp.asarray(_CHUNK_POS), chip, 0, keepdims=False)
    x0, target = x[0], loss_target[0]

    wt, mt, vt = (jnp.swapaxes(a, 1, 2) for a in (w_in, m_w_in, v_w_in))
    w0_mine, wloc_in1 = _cast_permute_w_in(wt, jnp.concatenate([chip_idx[:1], src_chunks]))
    c_all, wloc_out, w0 = _gather_inputs(c, w_out, w0_mine)
    wts = [w0.reshape(D_IN, D_MODEL), None]

    b_blk = lax.dynamic_slice_in_dim(b_ada, chip * W_ADA_BLK, W_ADA_BLK, axis=1)
    ada_part, cond = _ada_rows(c_all.reshape(N_DEV, D_MODEL), w_ada, b_blk)
    ada = jnp.moveaxis(_exchange_ada(ada_part), 0, 1).reshape(DEPTH, 3 * D_MODEL)
    shift = [ada[l:l + 1, 0:D_MODEL] for l in range(DEPTH)]
    scale = [ada[l:l + 1, D_MODEL:2 * D_MODEL] for l in range(DEPTH)]
    gate = [ada[l:l + 1, 2 * D_MODEL:] for l in range(DEPTH)]
    ng = [norm_gain[l:l + 1] for l in range(DEPTH)]

    qg2 = jnp.concatenate([q_gain, q_gain], axis=-1)
    kg2 = jnp.concatenate([k_gain, k_gain], axis=-1)
    ws_b = w_s.astype(BF16)
    wst_b = jnp.swapaxes(w_s, -1, -2).astype(BF16)
    bsp = jnp.repeat(jnp.swapaxes(b_s.reshape(DEPTH, 4, 2, BLOCK), -1, -2), HEAD_DIM, axis=-1)
    bias = jnp.asarray(_bias_table())

    def mix_args(l):
        return bias, sink[l], qg2[l:l + 1], kg2[l:l + 1], ws_b[l]

    w_out_shape = (W_OUT_BLK, D_MODEL)
    proj0, w1 = _proj_fwd(x0, ng[0], scale[0], shift[0], wts[0], "proj_fwd_0",
                          jobs=[_job_gather([(wloc_in1, None)], [(W_IN_BLK, D_MODEL)])])
    y0, *kept0, wo0, wo1 = _mix_fwd(proj0, *mix_args(0), bsp[0], "mix_fwd_0",
                            jobs=[_job_gather([(wloc_out, 0), (wloc_out, 1)], [w_out_shape, w_out_shape])])
    wts[1] = w1.reshape(D_IN, D_MODEL)
    wos = [_permute_w_out(w.reshape(D_MODEL, D_MODEL)) for w in (wo0, wo1)]
    x1, = _out_fwd(y0, x0, gate[0], wos[0], "out_fwd_0")
    proj1, = _proj_fwd(x1, ng[1], scale[1], shift[1], wts[1], "proj_fwd_1")
    y1, *kept1 = _mix_fwd(proj1, *mix_args(1), bsp[1], "mix_fwd_1")
    dx2, sq = _out_fwd_loss(y1, x1, gate[1], wos[1], target, "out_fwd_loss_1")

    def blocks_out(gw):
        return _unpermute_w_out(gw).reshape(N_CHIPS, W_OUT_BLK, D_MODEL)

    dy1, gwo1, dgate1 = _out_bwd(dx2, y1, gate[1], wos[1], "out_bwd_1")
    go1 = blocks_out(gwo1)
    dpb, dkv, p0, p2, dqg1, dkg1, dsink1, dws1, dbsp1, ro1 = _mix_bwd(
        dy1, proj1, *kept1, qg2[1:2], kg2[1:2], wst_b[1], "mix_bwd_1", jobs=[_job_swap(go1)])
    po1, = _pair_sum(go1, ro1, core_idx, "pair_sum_w_out_1")
    gwi1, dkvb, co1 = _w_in_grad(dpb, dkv, p0, p2, x1, ng[1], scale[1], shift[1], "w_in_grad_1", jobs=[_job_scatter(po1)])
    gi1 = gwi1.reshape(N_CHIPS, W_IN_BLK, D_MODEL)
    fo1 = _chip_sum(po1, co1, chip_idx, "chip_sum_w_out_1")
    dx1, dsh1, dsc1, dng1, grad_wo1, ri1 = _proj_bwd(dpb, dkvb, x1, dx2, ng[1], scale[1], wts[1], "proj_bwd_1",
                                                     jobs=[_job_join(fo1), _job_swap(gi1)])
    pi1, = _pair_sum(gi1, ri1, core_idx, "pair_sum_w_in_1")

    dy0, gwo0, dgate0 = _out_bwd(dx1, y0, gate[0], wos[0], "out_bwd_0")
    go0 = blocks_out(gwo0)
    dpb, dkv, p0, p2, dqg0, dkg0, dsink0, dws0, dbsp0, ci1, ro0 = _mix_bwd(
        dy0, proj0, *kept0, qg2[0:1], kg2[0:1], wst_b[0], "mix_bwd_0", jobs=[_job_scatter(pi1), _job_swap(go0)])
    fi1 = _chip_sum(pi1, ci1, chip_idx, "chip_sum_w_in_1")
    po0, = _pair_sum(go0, ro0, core_idx, "pair_sum_w_out_0")

    def bs_grad(dbsp):
        return jnp.swapaxes(dbsp[:, :, ::HEAD_DIM], -1, -2).reshape(8, BLOCK)

    small_g = dict(
        w_s=jnp.stack([dws0, dws1]), b_s=jnp.stack([bs_grad(dbsp0), bs_grad(dbsp1)]),
        q_gain=jnp.stack([dqg0[0, :HEAD_DIM], dqg1[0, :HEAD_DIM]]), k_gain=jnp.stack([dkg0[0, :HEAD_DIM], dkg1[0, :HEAD_DIM]]),
        sink=jnp.stack([dsink0[0, :N_HEADS], dsink1[0, :N_HEADS]]))
    gwi0, dkvb, grad_wi1, co0, gathered_a = _w_in_grad(
        dpb, dkv, p0, p2, x0, ng[0], scale[0], shift[0], "w_in_grad_0",
        jobs=[_job_join(fi1), _job_scatter(po0), _job_all_gather(_pack_rows(small_g, _SMALL_A))])
    gi0 = gwi0.reshape(N_CHIPS, W_IN_BLK, D_MODEL)
    fo0 = _chip_sum(po0, co0, chip_idx, "chip_sum_w_out_0")

    ri0, grad_wo0 = _comm([_job_swap(gi0), _job_join(fo0)], "swap_w_in_0")
    pi0, pi0_send = _pair_sum(gi0, ri0, core_idx, "pair_sum_w_in_0", send_dtype=BF16)
    dx0, dsh0, dsc0, dng0, ci0 = _proj_bwd(dpb, dkvb, x0, dx1, ng[0], scale[0], wts[0], "proj_bwd_0",
                                           jobs=[_job_scatter(pi0_send)])
    fi0 = _chip_sum(pi0, ci0, chip_idx, "chip_sum_w_in_0")
    grad_wi0, = _comm([_job_join(fi0)], "join_w_in_0")

    small_g.update(
        b_ada=jnp.stack([jnp.concatenate([dsh0, dsc0, dgate0], axis=-1)[0], jnp.concatenate([dsh1, dsc1, dgate1], axis=-1)[0]]),
        norm_gain=jnp.stack([dng0[0], dng1[0]]), sq_err=sq[0])
    none = jnp.zeros((1,), F32)
    small_w = dict(w_s=w_s, b_s=b_s, b_ada=b_ada, norm_gain=norm_gain, q_gain=q_gain, k_gain=k_gain, sink=sink, sq_err=none)
    small_m = dict(w_s=m_w_s, b_s=m_b_s, b_ada=m_b_ada, norm_gain=m_norm_gain, q_gain=m_q_gain, k_gain=m_k_gain, sink=m_sink,
                   sq_err=none)
    small_v = dict(w_s=v_w_s, b_s=v_b_s, b_ada=v_b_ada, norm_gain=v_norm_gain, q_gain=v_q_gain, k_gain=v_k_gain, sink=v_sink,
                   sq_err=none)
    gathered_b = _all_gather_rows(_pack_rows(small_g, _SMALL_B))
    packed = _small_sum_adamw(gathered_a.reshape(N_DEV, -1, 128), gathered_b.reshape(N_DEV, -1, 128),
                              _pack_small(small_w), _pack_small(small_m), _pack_small(small_v))
    shapes = {k: a.shape for k, a in small_w.items()}
    sg, sd, sm, sv = (_unpack_small(p, shapes) for p in packed)
    loss = 0.5 * sg["sq_err"][0]

    dada_all = gathered_b.reshape(N_DEV, -1, 128)[:, 0:DEPTH * 24].reshape(N_DEV, DEPTH, 3 * D_MODEL)
    dada_blk = jnp.moveaxis(lax.dynamic_slice_in_dim(dada_all, chip * W_ADA_BLK, W_ADA_BLK, axis=2), 0, 1)
    pad = BLOCK - N_DEV
    ada_out = _w_ada_grad_adamw(
        jnp.pad(cond.T, ((0, 0), (0, pad))).astype(BF16), jnp.pad(dada_blk, ((0, 0), (0, pad), (0, 0))),
        w_ada, m_w_ada, v_w_ada)

    in_out = [jnp.swapaxes(a, 1, 2) for a in _adamw_w_in(wt, grad_wi0, grad_wi1, mt, vt, pos_chunks)]
    out_out = _adamw_w_out(w_out, grad_wo0, grad_wo1, m_w_out, v_w_out)

    def ordered(k):
        small = (sg, sd, sm, sv)[k]
        return (ada_out[k], small["b_ada"], small["norm_gain"], in_out[k], small["q_gain"], small["k_gain"], small["sink"],
                small["w_s"], small["b_s"], out_out[k])

    return (loss, dx0[None], *ordered(0), *ordered(1), *ordered(2), *ordered(3))
```

```python
import numpy as np

import jax
import jax.numpy as jnp
from jax import lax
from jax.experimental import pallas as pl
from jax.experimental.pallas import tpu as pltpu

F32 = jnp.float32
BF16 = jnp.bfloat16

D_MODEL = 1024
DEPTH = 2
HEAD_DIM = 64
N_HEADS = 8
BLOCK = 128
SUB = 4
TILE = SUB * BLOCK
STACK = 8 * BLOCK
D_ATTN = 512
D_KV = 128
D_IN = 2816
N_CHIPS = 4
N_DEV = 8
W_IN_BLK = D_IN // N_CHIPS
W_OUT_BLK = D_MODEL // N_CHIPS
W_ADA_BLK = 3 * D_MODEL // N_CHIPS
CHUNK_ROWS = HEAD_DIM
N_CHUNKS = W_IN_BLK // CHUNK_ROWS
EPS = 1e-6
NEG_INF = -1e30

C_Q, C_K, C_V, C_GA, C_U, C_VG, C_GG = 0, 512, 640, 768, 1280, 1792, 2304

ADAM_LR = 0.001
ADAM_B1 = 0.9
ADAM_B2 = 0.999
ADAM_EPS = 1e-08
ADAM_WD = 0.01
ADAM_STEP = 10

MESH = pl.DeviceIdType.MESH
MIB = 1024 * 1024
ANY = pl.BlockSpec(memory_space=pl.ANY)
VMEM = pl.BlockSpec(memory_space=pltpu.VMEM)

NT_DIMS = (((1,), (1,)), ((), ()))
TN_DIMS = (((0,), (0,)), ((), ()))

_PAIR_ORDER = (0, 4, 1, 5, 2, 6, 3, 7)
_CHUNK_SRC = np.array([
    list(_PAIR_ORDER) + [8, 9, 10],
    [0] + [1 + h for h in _PAIR_ORDER] + [9, 10],
    list(range(N_CHUNKS)),
    list(range(N_CHUNKS)),
], np.int32)
_CHUNK_POS = np.argsort(_CHUNK_SRC, axis=1).astype(np.int32)


def _bias_table():
    i = np.arange(N_HEADS * BLOCK)[:, None]
    j = np.arange(3 * BLOCK)[None, :]
    dist = np.abs(j - BLOCK - (i % BLOCK))
    slope = 2.0 ** -(i // BLOCK + 1.0)
    inner = np.where(dist <= BLOCK, -(slope * dist), NEG_INF)
    first = np.where(j >= BLOCK, inner, NEG_INF)
    last = np.where(j < 2 * BLOCK, inner, NEG_INF)
    return np.stack([first, inner, last]).astype(np.float32)


def _full(shape):
    n = len(shape)
    return pl.BlockSpec(shape, lambda *_: (0,) * n)


def _sds(shape, dtype=F32):
    return jax.ShapeDtypeStruct(shape, dtype)


def _coords():
    return lax.axis_index("x"), lax.axis_index("y"), lax.axis_index("c")


def _other_chips(x, y):
    return [(1 - x, y), (x, 1 - y), (1 - x, 1 - y)]


def _in_hbm(*operands):
    return [pltpu.with_memory_space_constraint(a, pltpu.HBM) if a.size * a.dtype.itemsize >= MIB // 4 else a
            for a in operands]


def _remote(src, dst, sems, dev):
    return pltpu.make_async_remote_copy(src_ref=src, dst_ref=dst, send_sem=sems[0], recv_sem=sems[1],
                                        device_id=dev, device_id_type=MESH)


class _Job:
    def __init__(self, inputs, out_shapes, n_remote, n_local, make, then=None, in_place=False):
        self.inputs, self.out_shapes, self.n_remote, self.n_local, self.make = inputs, out_shapes, n_remote, n_local, make
        self.then = then
        self.in_place = in_place


def _job_aliases(jobs, in_base, out_base):
    aliases, a, b = {}, 0, 0
    for j in jobs:
        if j.in_place:
            aliases.update({in_base + a + k: out_base + b + k for k in range(len(j.inputs))})
        a, b = a + len(j.inputs), b + len(j.out_shapes)
    return aliases


def _job_copies(jobs, jin, jout, sems, second=False):
    send, recv, loc = sems
    res, a, b, r, l = [], 0, 0, 0, 0
    for j in jobs:
        build = j.then if second else j.make
        if build is not None:
            res += build(jin[a:a + len(j.inputs)], jout[b:b + len(j.out_shapes)],
                         lambda k, r=r: (send.at[r + k], recv.at[r + k]), lambda k, l=l: loc.at[l + k])
        a, b, r, l = a + len(j.inputs), b + len(j.out_shapes), r + j.n_remote, l + j.n_local
    return res


def _run(copies):
    for cp in copies:
        cp.start()
    for cp in copies:
        cp.wait()


def _job_gather(sources, shapes):
    n = len(sources)

    def make(ins, outs, rsem, lsem):
        x, y, c = _coords()
        j = 2 * x + y
        res = []
        for t, ((_, layer), src, dst) in enumerate(zip(sources, ins, outs)):
            src = src if layer is None else src.at[layer]
            res.append(pltpu.make_async_copy(src, dst.at[j], lsem(t)))
            for k, chip in enumerate(_other_chips(x, y)):
                res.append(_remote(src, dst.at[j], rsem(3 * t + k), (*chip, c)))
        return res

    return _Job([a for a, _ in sources], [_sds((N_CHIPS,) + s, BF16) for s in shapes], 3 * n, n, make)


def _job_swap(g):
    _, rows, cols = g.shape
    half = rows // 2

    def make(ins, outs, rsem, lsem):
        x, y, c = _coords()
        return [_remote(ins[0].at[:, pl.ds((1 - c) * half, half), :], outs[0], rsem(0), (x, y, 1 - c))]

    return _Job([g], [_sds((N_CHIPS, half, cols))], 1, 0, make)


def _job_scatter(p):
    def make(ins, outs, rsem, lsem):
        x, y, c = _coords()
        return [_remote(ins[0].at[2 * chip[0] + chip[1]], outs[0].at[k], rsem(k), (*chip, c))
                for k, chip in enumerate(_other_chips(x, y))]

    return _Job([p], [_sds((3,) + p.shape[1:], p.dtype)], 3, 0, make)


def _job_all_gather(blk):
    m_per = blk.shape[0]

    def rows(ref, px, py, pc):
        return ref.at[pl.ds((4 * px + 2 * py + pc) * m_per, m_per), :]

    def make(ins, outs, rsem, lsem):
        x, y, c = _coords()
        res = [pltpu.make_async_copy(ins[0], rows(outs[0], x, y, c), lsem(0)),
               _remote(ins[0], rows(outs[0], x, y, c), rsem(0), (x, y, 1 - c))]
        res += [_remote(ins[0], rows(outs[0], x, y, c), rsem(1 + k), (*chip, c)) for k, chip in enumerate(_other_chips(x, y))]
        return res

    def then(ins, outs, rsem, lsem):
        x, y, c = _coords()
        return [_remote(rows(outs[0], *chip, c), rows(outs[0], *chip, c), rsem(4 + k), (x, y, 1 - c))
                for k, chip in enumerate(_other_chips(x, y))]

    return _Job([blk], [_sds((N_DEV * m_per, blk.shape[1]), blk.dtype)], 7, 1, make, then)


def _job_join(f):
    half = f.shape[0] // 2

    def make(ins, outs, rsem, lsem):
        x, y, c = _coords()
        mine = pl.ds(c * half, half)
        return [_remote(ins[0].at[mine, :], outs[0].at[mine, :], rsem(0), (x, y, 1 - c))]

    return _Job([f], [_sds(f.shape, f.dtype)], 1, 0, make, in_place=True)


def _pallas(body, *, name, grid, in_specs, out_specs, out_shape, operands, vmem_mib, jobs=(), scratch=()):
    in_specs, out_specs, out_shape = list(in_specs), list(out_specs), list(out_shape)
    n_in, n_out = len(in_specs), len(out_specs)
    j_in = [a for j in jobs for a in j.inputs]
    j_out = [s for j in jobs for s in j.out_shapes]
    n_rem = max(1, sum(j.n_remote for j in jobs))
    n_loc = max(1, sum(j.n_local for j in jobs))
    sems = [pltpu.SemaphoreType.DMA((n_rem,)), pltpu.SemaphoreType.DMA((n_rem,)),
            pltpu.SemaphoreType.DMA((n_loc,))] if jobs else []
    scratch = list(scratch) + sems

    def wrapped(*refs):
        ins = refs[:n_in]
        jin = refs[n_in:n_in + len(j_in)]
        outs = refs[n_in + len(j_in):n_in + len(j_in) + n_out]
        jout = refs[n_in + len(j_in) + n_out:n_in + len(j_in) + n_out + len(j_out)]
        own = refs[n_in + len(j_in) + n_out + len(j_out):len(refs) - len(sems)]

        if jobs:
            first = last = None
            for d, n in enumerate(grid):
                f, e = pl.program_id(d) == 0, pl.program_id(d) == n - 1
                first, last = (f, e) if first is None else (first & f, last & e)

            @pl.when(first)
            def _():
                for cp in _job_copies(jobs, jin, jout, refs[-3:]):
                    cp.start()

        body(*ins, *outs, *own)

        if jobs:
            @pl.when(last)
            def _():
                for cp in _job_copies(jobs, jin, jout, refs[-3:]):
                    cp.wait()
                _run(_job_copies(jobs, jin, jout, refs[-3:], second=True))

    return pl.pallas_call(
        wrapped, name=name, grid=grid,
        in_specs=in_specs + [ANY] * len(j_in), out_specs=out_specs + [ANY] * len(j_out),
        out_shape=out_shape + j_out, scratch_shapes=scratch, input_output_aliases=_job_aliases(jobs, n_in, n_out),
        compiler_params=pltpu.CompilerParams(dimension_semantics=("arbitrary",) * len(grid),
                                             vmem_limit_bytes=vmem_mib * MIB),
    )(*_in_hbm(*operands, *j_in))


def _comm(jobs, name):
    j_in = [a for j in jobs for a in j.inputs]
    j_out = [s for j in jobs for s in j.out_shapes]
    n_rem = max(1, sum(j.n_remote for j in jobs))
    n_loc = max(1, sum(j.n_local for j in jobs))

    def body(*refs):
        jin, jout = refs[:len(j_in)], refs[len(j_in):len(j_in) + len(j_out)]
        _run(_job_copies(jobs, jin, jout, refs[-3:]))
        _run(_job_copies(jobs, jin, jout, refs[-3:], second=True))

    return pl.pallas_call(
        body, name=name, in_specs=[ANY] * len(j_in), out_specs=[ANY] * len(j_out), out_shape=j_out,
        scratch_shapes=[pltpu.SemaphoreType.DMA((n_rem,)), pltpu.SemaphoreType.DMA((n_rem,)),
                        pltpu.SemaphoreType.DMA((n_loc,))],
        input_output_aliases=_job_aliases(jobs, 0, 0),
    )(*_in_hbm(*j_in))


def _sigmoid(x):
    return 1.0 / (1.0 + jnp.exp(-x))


def _lo_mask(shape):
    return lax.broadcasted_iota(jnp.int32, shape, len(shape) - 1) < HEAD_DIM


def _half_sum(x, lo):
    a = jnp.sum(jnp.where(lo, x, 0.0), axis=-1, keepdims=True)
    b = jnp.sum(jnp.where(lo, 0.0, x), axis=-1, keepdims=True)
    return jnp.where(lo, a, b)


def _half_rms_scale(x, lo):
    return lax.rsqrt(_half_sum(x * x, lo) * (1.0 / HEAD_DIM) + EPS)


def _stack_heads(pairs, lo):
    return jnp.concatenate([jnp.where(lo, t, 0.0) for t in pairs] + [jnp.where(lo, 0.0, t) for t in pairs], axis=0)


def _unstack_pair(stack, p, lo):
    return jnp.where(lo, stack[BLOCK * p:BLOCK * (p + 1)], stack[BLOCK * (4 + p):BLOCK * (5 + p)])


def _attention_probs(q_stack, kn, bias_ref, sink_ref):
    rows = N_HEADS * BLOCK
    s = lax.dot_general(q_stack, kn, NT_DIMS, preferred_element_type=F32) + bias_ref[...]
    sink = jnp.concatenate([jnp.full((BLOCK, BLOCK), sink_ref[h], F32) for h in range(N_HEADS)], axis=0)
    cols = [s[:, BLOCK * j:BLOCK * (j + 1)] for j in range(3)]
    top = jnp.max(jnp.maximum(jnp.maximum(cols[0], cols[1]), cols[2]), axis=-1, keepdims=True)
    m = jnp.maximum(jnp.broadcast_to(top, (rows, BLOCK)), sink)
    e = [jnp.exp(c - m) for c in cols]
    es = jnp.exp(sink - m)
    inv = 1.0 / (jnp.broadcast_to(jnp.sum((e[0] + e[1]) + e[2], axis=-1, keepdims=True), (rows, BLOCK)) + es)
    return jnp.concatenate([c * inv for c in e], axis=1), es * inv


def _kv_rows(cur_ref, pkv_ref, nkv_ref):
    k = jnp.concatenate([pkv_ref[:, 0:D_KV], cur_ref[:, C_K:C_K + D_KV], nkv_ref[:, 0:D_KV]], axis=0)
    v = jnp.concatenate([pkv_ref[:, D_KV:2 * D_KV], cur_ref[:, C_V:C_V + D_KV], nkv_ref[:, D_KV:2 * D_KV]], axis=0)
    return k, v


def _overlap_add(parts):
    blocks = []
    for j in range(SUB + 2):
        terms = [parts[b][BLOCK * (j - b):BLOCK * (j - b + 1)] for b in range(SUB) if 0 <= j - b <= 2]
        total = terms[0]
        for t in terms[1:]:
            total = total + t
        blocks.append(total)
    return jnp.concatenate(blocks, axis=0)


def _mix_specs(nt):
    cur = pl.BlockSpec((TILE, D_IN), lambda i: (i, 0))
    kv_col = C_K // (2 * D_KV)
    pkv = pl.BlockSpec((BLOCK, 2 * D_KV), lambda i: (jnp.maximum(i * SUB - 1, 0), kv_col))
    nkv = pl.BlockSpec((BLOCK, 2 * D_KV), lambda i: (jnp.minimum((i + 1) * SUB, nt * SUB - 1), kv_col))
    table = (None, N_HEADS * BLOCK, 3 * BLOCK)
    first = pl.BlockSpec(table, lambda i: (jnp.where(i == 0, 0, 1), 0, 0))
    inner = pl.BlockSpec(table, lambda i: (1, 0, 0))
    last = pl.BlockSpec(table, lambda i: (jnp.where(i == nt - 1, 2, 1), 0, 0))
    return cur, pkv, nkv, [first] + [inner] * (SUB - 2) + [last]


def _proj_fwd(x, ng, scale, shift, wt, name, jobs=()):
    s = x.shape[0]
    ts = min(512, s)

    def body(x_ref, ng_ref, sc_ref, sh_ref, w_ref, o_ref):
        xv = x_ref[...]
        r = lax.rsqrt(jnp.mean(xv * xv, axis=-1, keepdims=True) + EPS)
        h = ((xv * r) * ng_ref[...]) * (1.0 + sc_ref[...]) + sh_ref[...]
        o_ref[...] = lax.dot_general(h.astype(BF16), w_ref[...], NT_DIMS, preferred_element_type=F32)

    vec = _full((1, D_MODEL))
    return _pallas(
        body, name=name, grid=(s // ts,),
        in_specs=[pl.BlockSpec((ts, D_MODEL), lambda i: (i, 0)), vec, vec, vec, _full((D_IN, D_MODEL))],
        out_specs=[pl.BlockSpec((ts, D_IN), lambda i: (i, 0))], out_shape=[_sds((s, D_IN))],
        operands=(x, ng, scale, shift, wt), vmem_mib=48, jobs=jobs)


def _diagonal():
    return lax.broadcasted_iota(jnp.int32, (BLOCK, BLOCK), 0) == lax.broadcasted_iota(jnp.int32, (BLOCK, BLOCK), 1)


def _column_as_row(wide, eye):
    return jnp.sum(jnp.where(eye, wide, 0.0), axis=0, keepdims=True)


def _mix_fwd(proj, bias, sink, qg2, kg2, ws, bsp, name, jobs=()):
    s = proj.shape[0]
    nt = s // TILE

    def body(sink_ref, cur_ref, pkv_ref, nkv_ref, *rest):
        bias_refs = rest[:SUB]
        qg_ref, kg_ref, ws_ref, bsp_ref, y_ref, p_ref, ps_ref, attn_ref, sv_ref = rest[SUB:]
        lo = _lo_mask((BLOCK, BLOCK))
        eye = _diagonal()
        lo_kv = _lo_mask((TILE + 2 * BLOCK, BLOCK))
        k_all, v_all = _kv_rows(cur_ref, pkv_ref, nkv_ref)
        kn_all = ((k_all * _half_rms_scale(k_all, lo_kv)) * kg_ref[...]).astype(BF16)
        vb_all = v_all.astype(BF16)
        for b in range(SUB):
            rows = slice(BLOCK * b, BLOCK * (b + 1))
            window = slice(BLOCK * b, BLOCK * (b + 3))
            qn = []
            for p in range(4):
                q = cur_ref[rows, C_Q + BLOCK * p:C_Q + BLOCK * (p + 1)]
                qn.append(((q * _half_rms_scale(q, lo)) * qg_ref[...]) * 0.125)
            q_stack = _stack_heads(qn, lo).astype(BF16)
            prob, psink = _attention_probs(q_stack, kn_all[window], bias_refs[b], sink_ref)
            pb = prob.astype(BF16)
            p_ref[STACK * b:STACK * (b + 1), :] = pb
            ps_ref[N_HEADS * b:N_HEADS * (b + 1), :] = jnp.concatenate(
                [_column_as_row(psink[BLOCK * h:BLOCK * (h + 1)], eye) for h in range(N_HEADS)], axis=0)
            o_stack = jnp.dot(pb, vb_all[window], preferred_element_type=F32)
            for p in range(4):
                g = cur_ref[rows, C_GA + BLOCK * p:C_GA + BLOCK * (p + 1)]
                attn = _unstack_pair(o_stack, p, lo)
                attn_ref[rows, BLOCK * p:BLOCK * (p + 1)] = attn.astype(BF16)
                y_ref[rows, BLOCK * p:BLOCK * (p + 1)] = (attn * (g * _sigmoid(g))).astype(BF16)

        for p in range(4):
            cols = slice(C_VG + BLOCK * p, C_VG + BLOCK * (p + 1))
            vn = []
            for b in range(SUB):
                vg = cur_ref[BLOCK * b:BLOCK * (b + 1), cols]
                vn.append((vg * _half_rms_scale(vg, lo)).astype(BF16))
            vn = jnp.concatenate(vn, axis=1)
            sv_a = jnp.dot(ws_ref[2 * p], vn, preferred_element_type=F32)
            sv_b = jnp.dot(ws_ref[2 * p + 1], vn, preferred_element_type=F32)
            for b in range(SUB):
                rows = slice(BLOCK * b, BLOCK * (b + 1))
                lanes = slice(BLOCK * b, BLOCK * (b + 1))
                sv = jnp.where(lo, sv_a[:, lanes], sv_b[:, lanes]) + bsp_ref[p]
                sv_ref[rows, BLOCK * p:BLOCK * (p + 1)] = sv.astype(BF16)
                u = cur_ref[rows, C_U + BLOCK * p:C_U + BLOCK * (p + 1)]
                g = cur_ref[rows, C_GG + BLOCK * p:C_GG + BLOCK * (p + 1)]
                y_ref[rows, D_ATTN + BLOCK * p:D_ATTN + BLOCK * (p + 1)] = ((u * sv) * (g * _sigmoid(g))).astype(BF16)

    cur, pkv, nkv, bias_specs = _mix_specs(nt)
    nb = nt * SUB
    half = pl.BlockSpec((TILE, D_ATTN), lambda i: (i, 0))
    return _pallas(
        body, name=name, grid=(nt,),
        in_specs=[pl.BlockSpec(memory_space=pltpu.SMEM), cur, pkv, nkv, *bias_specs, _full((1, BLOCK)), _full((1, BLOCK)),
                  _full((8, BLOCK, BLOCK)), _full((4, BLOCK, BLOCK))],
        out_specs=[pl.BlockSpec((TILE, D_MODEL), lambda i: (i, 0)), pl.BlockSpec((SUB * STACK, 3 * BLOCK), lambda i: (i, 0)),
                   pl.BlockSpec((SUB * N_HEADS, BLOCK), lambda i: (i, 0)), half, half],
        out_shape=[_sds((s, D_MODEL), BF16), _sds((nb * STACK, 3 * BLOCK), BF16), _sds((nb * N_HEADS, BLOCK)),
                   _sds((s, D_ATTN), BF16), _sds((s, D_ATTN), BF16)],
        operands=(sink, proj, proj, proj, *([bias] * SUB), qg2, kg2, ws, bsp), vmem_mib=56, jobs=jobs)


def _out_proj_fwd(y, x, gate, w_out, ng, scale, shift, wt, name):
    s = x.shape[0]
    ts = min(512, s)

    def body(y_ref, x_ref, g_ref, w_ref, ng_ref, sc_ref, sh_ref, wt_ref, xn_ref, p_ref):
        xv = x_ref[...] + g_ref[...] * jnp.dot(y_ref[...], w_ref[...], preferred_element_type=F32)
        xn_ref[...] = xv
        r = lax.rsqrt(jnp.mean(xv * xv, axis=-1, keepdims=True) + EPS)
        h = ((xv * r) * ng_ref[...]) * (1.0 + sc_ref[...]) + sh_ref[...]
        p_ref[...] = lax.dot_general(h.astype(BF16), wt_ref[...], NT_DIMS, preferred_element_type=F32)

    row = pl.BlockSpec((ts, D_MODEL), lambda i: (i, 0))
    vec = _full((1, D_MODEL))
    return _pallas(
        body, name=name, grid=(s // ts,),
        in_specs=[row, row, vec, _full((D_MODEL, D_MODEL)), vec, vec, vec, _full((D_IN, D_MODEL))],
        out_specs=[row, pl.BlockSpec((ts, D_IN), lambda i: (i, 0))], out_shape=[_sds((s, D_MODEL)), _sds((s, D_IN))],
        operands=(y, x, gate, w_out, ng, scale, shift, wt), vmem_mib=56)


def _out_loss_bwd(y, x, gate, w_out, target, name):
    s = x.shape[0]
    ts = min(512, s)
    steps = s // ts

    def body(y_ref, x_ref, g_ref, w_ref, t_ref, dx_ref, sq_ref, dy_ref, gw_ref, dg_ref):
        @pl.when(pl.program_id(0) == 0)
        def _():
            sq_ref[...] = jnp.zeros_like(sq_ref)
            gw_ref[...] = jnp.zeros_like(gw_ref)

        yv = y_ref[...]
        out = x_ref[...] + g_ref[...] * jnp.dot(yv, w_ref[...], preferred_element_type=F32)
        diff = out - t_ref[...]
        dx = diff * (1.0 / D_MODEL)
        dx_ref[...] = dx
        per_token = jnp.sum(diff * diff, axis=-1, keepdims=True) * (1.0 / D_MODEL)
        sq_ref[...] += jnp.sum(per_token, axis=0, keepdims=True)
        dy_ref[...] = lax.dot_general((dx * g_ref[...]).astype(BF16), w_ref[...], NT_DIMS, preferred_element_type=F32)
        gw_ref[...] += lax.dot_general(yv, dx.astype(BF16), TN_DIMS, preferred_element_type=F32)

        @pl.when(pl.program_id(0) == steps - 1)
        def _():
            m = gw_ref[...]
            dg_ref[...] = jnp.sum(w_ref[...].astype(F32) * m, axis=0, keepdims=True)
            gw_ref[...] = m * g_ref[...]

    row = pl.BlockSpec((ts, D_MODEL), lambda i: (i, 0))
    return _pallas(
        body, name=name, grid=(s // ts,), in_specs=[row, row, _full((1, D_MODEL)), _full((D_MODEL, D_MODEL)), row],
        out_specs=[row, _full((1, 1)), row, _full((D_MODEL, D_MODEL)), _full((1, D_MODEL))],
        out_shape=[_sds((s, D_MODEL)), _sds((1, 1)), _sds((s, D_MODEL)), _sds((D_MODEL, D_MODEL)), _sds((1, D_MODEL))],
        operands=(y, x, gate, w_out, target), vmem_mib=48)


def _out_bwd(dxo, y, gate, w_out, name, jobs=()):
    s = dxo.shape[0]
    ts = min(512, s)
    steps = s // ts

    def body(dx_ref, y_ref, g_ref, w_ref, dy_ref, gw_ref, dg_ref):
        @pl.when(pl.program_id(0) == 0)
        def _():
            gw_ref[...] = jnp.zeros_like(gw_ref)

        dx = dx_ref[...]
        dy_ref[...] = lax.dot_general((dx * g_ref[...]).astype(BF16), w_ref[...], NT_DIMS, preferred_element_type=F32)
        gw_ref[...] += lax.dot_general(y_ref[...], dx.astype(BF16), TN_DIMS, preferred_element_type=F32)

        @pl.when(pl.program_id(0) == steps - 1)
        def _():
            m = gw_ref[...]
            dg_ref[...] = jnp.sum(w_ref[...].astype(F32) * m, axis=0, keepdims=True)
            gw_ref[...] = m * g_ref[...]

    row = pl.BlockSpec((ts, D_MODEL), lambda i: (i, 0))
    return _pallas(
        body, name=name, grid=(s // ts,), in_specs=[row, row, _full((1, D_MODEL)), _full((D_MODEL, D_MODEL))],
        out_specs=[row, _full((D_MODEL, D_MODEL)), _full((1, D_MODEL))],
        out_shape=[_sds((s, D_MODEL)), _sds((D_MODEL, D_MODEL)), _sds((1, D_MODEL))],
        operands=(dxo, y, gate, w_out), vmem_mib=48, jobs=jobs)


def _mix_bwd(dy, proj, probs, psink, attn, sv, qg2, kg2, wst, name, jobs=()):
    s = proj.shape[0]
    nt = s // TILE

    def body(dy_ref, cur_ref, pkv_ref, nkv_ref, p_ref, ps_ref, attn_ref, sv_ref, qg_ref, kg_ref, wst_ref,
             dpb_ref, dkv_ref, p0_ref, p2_ref, dqg_ref, dkg_ref, dsink_ref, dws_ref, dbsp_ref):
        def put(rows, col, value):
            dpb_ref[rows, col:col + BLOCK] = value.astype(BF16)

        @pl.when(pl.program_id(0) == 0)
        def _():
            dqg_ref[...] = jnp.zeros_like(dqg_ref)
            dkg_ref[...] = jnp.zeros_like(dkg_ref)
            dsink_ref[...] = jnp.zeros_like(dsink_ref)
            dws_ref[...] = jnp.zeros_like(dws_ref)
            dbsp_ref[...] = jnp.zeros_like(dbsp_ref)

        lo = _lo_mask((BLOCK, BLOCK))
        lo_kv = _lo_mask((TILE + 2 * BLOCK, BLOCK))
        eye = _diagonal()
        lane_row = lax.broadcasted_iota(jnp.int32, (1, BLOCK), 1)
        qg = qg_ref[...]
        kg = kg_ref[...]

        k_all, v_all = _kv_rows(cur_ref, pkv_ref, nkv_ref)
        rk = _half_rms_scale(k_all, lo_kv)
        khat = k_all * rk
        kn_all = (khat * kg).astype(BF16)
        vb_all = v_all.astype(BF16)

        dkn_parts, dv_parts = [], []
        dsink = jnp.zeros((1, BLOCK), F32)
        dqg = jnp.zeros((1, BLOCK), F32)
        for b in range(SUB):
            rows = slice(BLOCK * b, BLOCK * (b + 1))
            window = slice(BLOCK * b, BLOCK * (b + 3))
            kn, vb = kn_all[window], vb_all[window]

            qhat, rq = [], []
            for p in range(4):
                q = cur_ref[rows, C_Q + BLOCK * p:C_Q + BLOCK * (p + 1)]
                r = _half_rms_scale(q, lo)
                rq.append(r)
                qhat.append(q * r)
            q_stack = _stack_heads([(qh * qg) * 0.125 for qh in qhat], lo).astype(BF16)
            pb = p_ref[STACK * b:STACK * (b + 1), :]
            prob = pb.astype(F32)

            dout = []
            for p in range(4):
                g = cur_ref[rows, C_GA + BLOCK * p:C_GA + BLOCK * (p + 1)]
                sg = _sigmoid(g)
                dya = dy_ref[rows, BLOCK * p:BLOCK * (p + 1)]
                attn = attn_ref[rows, BLOCK * p:BLOCK * (p + 1)]
                put(rows, C_GA + BLOCK * p, dya * attn * (sg * (1.0 + g * (1.0 - sg))))
                dout.append(dya * (g * sg))
            do_stack = _stack_heads(dout, lo).astype(BF16)
            dp = lax.dot_general(do_stack, vb, NT_DIMS, preferred_element_type=F32)
            delta = jnp.sum(prob * dp, axis=-1, keepdims=True)
            dsb = (prob * (dp - delta)).astype(BF16)

            for h in range(N_HEADS):
                delta_row = _column_as_row(jnp.broadcast_to(delta[BLOCK * h:BLOCK * (h + 1)], (BLOCK, BLOCK)), eye)
                tot = jnp.sum(ps_ref[N_HEADS * b + h:N_HEADS * b + h + 1, :] * delta_row, axis=-1, keepdims=True)
                dsink = dsink - jnp.where(lane_row == h, tot, 0.0)

            dq_stack = jnp.dot(dsb, kn, preferred_element_type=F32) * 0.125
            dkn_parts.append(lax.dot_general(dsb, q_stack, TN_DIMS, preferred_element_type=F32))
            dv_parts.append(lax.dot_general(pb, do_stack, TN_DIMS, preferred_element_type=F32))

            for p in range(4):
                dqn = _unstack_pair(dq_stack, p, lo)
                qh = qhat[p]
                dqg = dqg + jnp.sum(dqn * qh, axis=0, keepdims=True)
                dqh = dqn * qg
                mean = _half_sum(dqh * qh, lo) * (1.0 / HEAD_DIM)
                put(rows, C_Q + BLOCK * p, rq[p] * (dqh - qh * mean))

        dws_new, dbs_new = [], []
        for p in range(4):
            rs, vnfs, vns, dsvs, dbs = [], [], [], [], None
            for b in range(SUB):
                rows = slice(BLOCK * b, BLOCK * (b + 1))
                vg = cur_ref[rows, C_VG + BLOCK * p:C_VG + BLOCK * (p + 1)]
                r = _half_rms_scale(vg, lo)
                vnf = vg * r
                sv = sv_ref[rows, BLOCK * p:BLOCK * (p + 1)]
                u = cur_ref[rows, C_U + BLOCK * p:C_U + BLOCK * (p + 1)]
                g = cur_ref[rows, C_GG + BLOCK * p:C_GG + BLOCK * (p + 1)]
                sg = _sigmoid(g)
                dym = dy_ref[rows, D_ATTN + BLOCK * p:D_ATTN + BLOCK * (p + 1)]
                put(rows, C_GG + BLOCK * p, dym * (u * sv) * (sg * (1.0 + g * (1.0 - sg))))
                dgm = dym * (g * sg)
                put(rows, C_U + BLOCK * p, dgm * sv)
                dsv = dgm * u
                term = jnp.where(lo, jnp.sum(jnp.where(lo, dsv, 0.0), axis=-1, keepdims=True),
                                 jnp.sum(jnp.where(lo, 0.0, dsv), axis=-1, keepdims=True))
                dbs = term if dbs is None else dbs + term
                rs.append(r)
                vnfs.append(vnf)
                vns.append(vnf.astype(BF16))
                dsvs.append(dsv)
            dbs_new.append(dbs)
            vn = jnp.concatenate(vns, axis=1)
            dsv = jnp.concatenate(dsvs, axis=1)
            lo_t = (lax.broadcasted_iota(jnp.int32, dsv.shape, 1) & (BLOCK - 1)) < HEAD_DIM
            dws_new.append(lax.dot_general(jnp.where(lo_t, dsv, 0.0).astype(BF16), vn, NT_DIMS, preferred_element_type=F32))
            dws_new.append(lax.dot_general(jnp.where(lo_t, 0.0, dsv).astype(BF16), vn, NT_DIMS, preferred_element_type=F32))
            dsvb = dsv.astype(BF16)
            dvn_a = jnp.dot(wst_ref[2 * p], dsvb, preferred_element_type=F32)
            dvn_b = jnp.dot(wst_ref[2 * p + 1], dsvb, preferred_element_type=F32)
            for b in range(SUB):
                lanes = slice(BLOCK * b, BLOCK * (b + 1))
                dvn = jnp.where(lo, dvn_a[:, lanes], dvn_b[:, lanes])
                mean = _half_sum(dvn * vnfs[b], lo) * (1.0 / HEAD_DIM)
                put(slice(BLOCK * b, BLOCK * (b + 1)), C_VG + BLOCK * p, rs[b] * (dvn - vnfs[b] * mean))

        dsink_ref[...] += dsink
        dqg = jnp.broadcast_to(dqg, (8, BLOCK))
        dqg_ref[...] += dqg + pltpu.roll(dqg, HEAD_DIM, 1)

        dkn = _overlap_add(dkn_parts)
        dv = _overlap_add(dv_parts)
        dkg = jnp.broadcast_to(jnp.sum(dkn * khat, axis=0, keepdims=True), (8, BLOCK))
        dkg_ref[...] += dkg + pltpu.roll(dkg, HEAD_DIM, 1)
        dkh = dkn * kg
        dk = rk * (dkh - khat * (_half_sum(dkh * khat, lo_kv) * (1.0 / HEAD_DIM)))
        dpb_ref[:, C_K:C_GA] = jnp.zeros((TILE, 2 * D_KV), BF16)
        dkv_ref[:, 0:D_KV] = dk[BLOCK:BLOCK + TILE]
        dkv_ref[:, D_KV:2 * D_KV] = dv[BLOCK:BLOCK + TILE]
        p0_ref[:, 0:D_KV] = dk[0:BLOCK]
        p0_ref[:, D_KV:2 * D_KV] = dv[0:BLOCK]
        p2_ref[:, 0:D_KV] = dk[BLOCK + TILE:]
        p2_ref[:, D_KV:2 * D_KV] = dv[BLOCK + TILE:]
        for g, new in enumerate(dws_new):
            dws_ref[g] += new
        for p, new in enumerate(dbs_new):
            dbsp_ref[p] += new

    cur, pkv, nkv, _ = _mix_specs(nt)
    kv_blk = (BLOCK, 2 * D_KV)
    half = pl.BlockSpec((TILE, D_ATTN), lambda i: (i, 0))
    return _pallas(
        body, name=name, grid=(nt,),
        in_specs=[pl.BlockSpec((TILE, D_MODEL), lambda i: (i, 0)), cur, pkv, nkv,
                  pl.BlockSpec((SUB * STACK, 3 * BLOCK), lambda i: (i, 0)), pl.BlockSpec((SUB * N_HEADS, BLOCK), lambda i: (i, 0)),
                  half, half, _full((1, BLOCK)), _full((1, BLOCK)), _full((8, BLOCK, BLOCK))],
        out_specs=[cur, pl.BlockSpec((TILE, 2 * D_KV), lambda i: (i, 0)),
                   pl.BlockSpec(kv_blk, lambda i: ((i + nt - 1) % nt, 0)),
                   pl.BlockSpec(kv_blk, lambda i: ((i + 1) % nt, 0)),
                   _full((8, BLOCK)), _full((8, BLOCK)), _full((1, BLOCK)),
                   _full((8, BLOCK, BLOCK)), _full((4, BLOCK, BLOCK))],
        out_shape=[_sds((s, D_IN), BF16), _sds((s, 2 * D_KV)), _sds((nt * BLOCK, 2 * D_KV)), _sds((nt * BLOCK, 2 * D_KV)),
                   _sds((8, BLOCK)), _sds((8, BLOCK)), _sds((1, BLOCK)),
                   _sds((8, BLOCK, BLOCK)), _sds((4, BLOCK, BLOCK))],
        operands=(dy, proj, proj, proj, probs, psink, attn, sv, qg2, kg2, wst), vmem_mib=56, jobs=jobs)


def _w_in_grad(dpb, dkv, p0, p2, x, ng, scale, shift, name, jobs=()):
    s = x.shape[0]
    ts = min(2 * TILE, s)
    tiles = ts // TILE

    def body(dpb_ref, dkv_ref, p0_ref, p2_ref, x_ref, ng_ref, sc_ref, sh_ref, gw_ref, dkvb_ref):
        @pl.when(pl.program_id(0) == 0)
        def _():
            gw_ref[...] = jnp.zeros_like(gw_ref)

        xv = x_ref[...]
        r = lax.rsqrt(jnp.mean(xv * xv, axis=-1, keepdims=True) + EPS)
        h = (((xv * r) * ng_ref[...]) * (1.0 + sc_ref[...]) + sh_ref[...]).astype(BF16)
        for t in range(tiles):
            halo = slice(BLOCK * t, BLOCK * (t + 1))
            first = slice(TILE * t, TILE * t + BLOCK)
            last = slice(TILE * (t + 1) - BLOCK, TILE * (t + 1))
            dkvb_ref[first, :] = (dkv_ref[first, :] + p2_ref[halo, :]).astype(BF16)
            if SUB > 2:
                inner = slice(TILE * t + BLOCK, TILE * (t + 1) - BLOCK)
                dkvb_ref[inner, :] = dkv_ref[inner, :].astype(BF16)
            dkvb_ref[last, :] = (dkv_ref[last, :] + p0_ref[halo, :]).astype(BF16)
        gw_ref[...] += lax.dot_general(dpb_ref[...], h, TN_DIMS, preferred_element_type=F32)
        gw_ref[C_K:C_GA, :] += lax.dot_general(dkvb_ref[...], h, TN_DIMS, preferred_element_type=F32)

    kv = pl.BlockSpec((ts, 2 * D_KV), lambda i: (i, 0))
    halo = pl.BlockSpec((tiles * BLOCK, 2 * D_KV), lambda i: (i, 0))
    vec = _full((1, D_MODEL))
    return _pallas(
        body, name=name, grid=(s // ts,),
        in_specs=[pl.BlockSpec((ts, D_IN), lambda i: (i, 0)), kv, halo, halo,
                  pl.BlockSpec((ts, D_MODEL), lambda i: (i, 0)), vec, vec, vec],
        out_specs=[_full((D_IN, D_MODEL)), kv], out_shape=[_sds((D_IN, D_MODEL)), _sds((s, 2 * D_KV), BF16)],
        operands=(dpb, dkv, p0, p2, x, ng, scale, shift), vmem_mib=56, jobs=jobs)


def _proj_bwd(dpb, dkvb, x, dxo, ng, scale, wt, name, jobs=()):
    s = x.shape[0]
    ts = min(512, s)

    def body(dpb_ref, dkvb_ref, x_ref, dxo_ref, ng_ref, sc_ref, w_ref, dxi_ref, dsh_ref, dsc_ref, dng_ref):
        @pl.when(pl.program_id(0) == 0)
        def _():
            dsh_ref[...] = jnp.zeros_like(dsh_ref)
            dsc_ref[...] = jnp.zeros_like(dsc_ref)
            dng_ref[...] = jnp.zeros_like(dng_ref)

        dh = (jnp.dot(dpb_ref[...], w_ref[...], preferred_element_type=F32)
              + jnp.dot(dkvb_ref[...], w_ref[C_K:C_GA, :], preferred_element_type=F32))

        xv = x_ref[...]
        r = lax.rsqrt(jnp.mean(xv * xv, axis=-1, keepdims=True) + EPS)
        xn = xv * r
        ngv = ng_ref[...]
        sc1 = 1.0 + sc_ref[...]
        dsh_ref[...] += jnp.sum(dh, axis=0, keepdims=True)
        dsc_ref[...] += jnp.sum(dh * (xn * ngv), axis=0, keepdims=True)
        dh1 = dh * sc1
        dng_ref[...] += jnp.sum(dh1 * xn, axis=0, keepdims=True)
        dxn = dh1 * ngv
        dxi_ref[...] = r * (dxn - xn * jnp.mean(dxn * xn, axis=-1, keepdims=True)) + dxo_ref[...]

    row = pl.BlockSpec((ts, D_MODEL), lambda i: (i, 0))
    vec = _full((1, D_MODEL))
    return _pallas(
        body, name=name, grid=(s // ts,),
        in_specs=[pl.BlockSpec((ts, D_IN), lambda i: (i, 0)), pl.BlockSpec((ts, 2 * D_KV), lambda i: (i, 0)),
                  row, row, vec, vec, _full((D_IN, D_MODEL))],
        out_specs=[row, vec, vec, vec],
        out_shape=[_sds((s, D_MODEL)), _sds((1, D_MODEL)), _sds((1, D_MODEL)), _sds((1, D_MODEL))],
        operands=(dpb, dkvb, x, dxo, ng, scale, wt), vmem_mib=48, jobs=jobs)


def _pair_sum(g, r, c_idx, name, send_dtype=None):
    _, rows, cols = g.shape
    half = rows // 2

    def body(c_ref, g_ref, r_ref, o_ref, *narrow):
        total = g_ref[...] + r_ref[...]
        o_ref[...] = total
        for n_ref in narrow:
            n_ref[...] = total.astype(n_ref.dtype)

    blk = (None, half, cols)
    out_blk = pl.BlockSpec(blk, lambda j, c: (j, 0, 0))
    shapes = [_sds((N_CHIPS, half, cols))] + ([_sds((N_CHIPS, half, cols), send_dtype)] if send_dtype else [])
    return pl.pallas_call(
        body, name=name,
        grid_spec=pltpu.PrefetchScalarGridSpec(
            num_scalar_prefetch=1, grid=(N_CHIPS,),
            in_specs=[pl.BlockSpec(blk, lambda j, c: (j, c[0], 0)), out_blk], out_specs=[out_blk] * len(shapes)),
        out_shape=shapes,
        compiler_params=pltpu.CompilerParams(dimension_semantics=("arbitrary",), vmem_limit_bytes=32 * MIB),
    )(*_in_hbm(c_idx, g, r))


def _chip_sum(p, r, place, name):
    _, rows, cols = p.shape
    tr = rows // 2

    def body(j_ref, p_ref, r_ref, o_ref):
        o_ref[...] = ((p_ref[...] + r_ref[0].astype(F32)) + r_ref[1].astype(F32)) + r_ref[2].astype(F32)

    return pl.pallas_call(
        body, name=name,
        grid_spec=pltpu.PrefetchScalarGridSpec(
            num_scalar_prefetch=1, grid=(2,),
            in_specs=[pl.BlockSpec((None, tr, cols), lambda t, j: (j[0], t, 0)),
                      pl.BlockSpec((3, tr, cols), lambda t, j: (0, t, 0))],
            out_specs=pl.BlockSpec((tr, cols), lambda t, j: (2 * j[1] + t, 0))),
        out_shape=_sds((2 * rows, cols)),
        compiler_params=pltpu.CompilerParams(dimension_semantics=("arbitrary",), vmem_limit_bytes=32 * MIB),
    )(*_in_hbm(place, p, r))


def _cast_permute_w_in(wt, place_chunks):
    def body(t_ref, w_ref, w0_ref, w1_ref):
        def cast_into(o_ref):
            for t in range(N_CHUNKS):
                src = pl.multiple_of(t_ref[1 + t] * CHUNK_ROWS, CHUNK_ROWS)
                o_ref[CHUNK_ROWS * t:CHUNK_ROWS * (t + 1), :] = w_ref[pl.ds(src, CHUNK_ROWS), :].astype(BF16)

        @pl.when(pl.program_id(0) == 0)
        def _():
            cast_into(w0_ref)

        @pl.when(pl.program_id(0) == 1)
        def _():
            cast_into(w1_ref)

    return pl.pallas_call(
        body, name="cast_permute_w_in",
        grid_spec=pltpu.PrefetchScalarGridSpec(
            num_scalar_prefetch=1, grid=(DEPTH,),
            in_specs=[pl.BlockSpec((None, W_IN_BLK, D_MODEL), lambda l, tbl: (l, 0, 0))],
            out_specs=[pl.BlockSpec((None, W_IN_BLK, D_MODEL), lambda l, tbl: (tbl[0], 0, 0)),
                       pl.BlockSpec((W_IN_BLK, D_MODEL), lambda l, tbl: (0, 0))]),
        out_shape=[_sds((N_CHIPS, W_IN_BLK, D_MODEL), BF16), _sds((W_IN_BLK, D_MODEL), BF16)],
        compiler_params=pltpu.CompilerParams(dimension_semantics=("arbitrary",), vmem_limit_bytes=32 * MIB),
    )(*_in_hbm(place_chunks, wt))


def _gather_inputs(c, w_out, w0):
    half = W_IN_BLK // 2

    def body(c_ref, wout_ref, mine_ref, call_ref, woutb_ref, w0_ref, send_sems, recv_sems):
        x, y, cc = _coords()
        j = 2 * x + y
        b = 2 * j + cc
        sib = (x, y, 1 - cc)
        woutb_ref[...] = wout_ref[...].astype(BF16)
        call_ref[b] = c_ref[...]
        chips = _other_chips(x, y)

        def sems(k):
            return send_sems.at[k], recv_sems.at[k]

        def half_rows(chip_index):
            return w0_ref.at[chip_index, pl.ds(cc * half, half), :]

        first = [_remote(mine_ref.at[j, pl.ds(cc * half, half), :], half_rows(j), sems(k), (*chip, cc))
                 for k, chip in enumerate(chips)]
        k = 3
        rest = []
        for fx in (0, 1):
            for fy in (0, 1):
                for fc in (0, 1):
                    if fx or fy or fc:
                        dev = (1 - x if fx else x, 1 - y if fy else y, 1 - cc if fc else cc)
                        rest.append(_remote(call_ref.at[b], call_ref.at[b], sems(k), dev))
                        k += 1
        for cp in first + rest:
            cp.start()
        passed = []
        for k, chip in enumerate(chips):
            jk = 2 * chip[0] + chip[1]
            first[k].wait_recv()
            passed.append(_remote(half_rows(jk), half_rows(jk), sems(10 + k), sib))
            passed[k].start()
        for cp in first:
            cp.wait_send()
        for cp in rest + passed:
            cp.wait()

    return pl.pallas_call(
        body, name="gather_inputs", in_specs=[VMEM, VMEM, ANY], out_specs=[VMEM, VMEM, ANY],
        out_shape=[_sds((N_DEV, 1, D_MODEL)), _sds((DEPTH, W_OUT_BLK, D_MODEL), BF16),
                   _sds((N_CHIPS, W_IN_BLK, D_MODEL), BF16)],
        scratch_shapes=[pltpu.SemaphoreType.DMA((13,)), pltpu.SemaphoreType.DMA((13,))],
        input_output_aliases={2: 2},
        compiler_params=pltpu.CompilerParams(vmem_limit_bytes=32 * MIB),
    )(c, w_out, w0)


def _ada_rows(c_all, w_ada, b_blk):
    def body(c_ref, w_ref, b_ref, o_ref, cond_ref):
        cv = c_ref[...]
        cond = (cv * _sigmoid(cv)).astype(BF16)
        cond_ref[...] = cond.astype(F32)
        for l in range(DEPTH):
            o_ref[:, l, :] = jnp.dot(cond, w_ref[l].astype(BF16), preferred_element_type=F32) + b_ref[l:l + 1, :]

    return pl.pallas_call(
        body, name="ada_rows", in_specs=[VMEM, VMEM, VMEM], out_specs=[VMEM, VMEM],
        out_shape=[_sds((N_DEV, DEPTH, W_ADA_BLK)), _sds((N_DEV, D_MODEL))],
        compiler_params=pltpu.CompilerParams(vmem_limit_bytes=32 * MIB),
    )(c_all, w_ada, b_blk)


def _exchange_ada(part):
    def body(part_ref, out_ref, send_sems, recv_sems):
        x, y, cc = _coords()
        j = 2 * x + y
        out_ref[j] = part_ref[2 * j + cc]
        copies = []
        for k, chip in enumerate(_other_chips(x, y)):
            b_dst = 4 * chip[0] + 2 * chip[1] + cc
            copies.append(_remote(part_ref.at[b_dst], out_ref.at[j], (send_sems.at[k], recv_sems.at[k]), (*chip, cc)))
        for cp in copies:
            cp.start()
        for cp in copies:
            cp.wait()

    return pl.pallas_call(
        body, name="exchange_ada", in_specs=[VMEM], out_specs=VMEM,
        out_shape=_sds((N_CHIPS, DEPTH, W_ADA_BLK)),
        scratch_shapes=[pltpu.SemaphoreType.DMA((3,)), pltpu.SemaphoreType.DMA((3,))],
    )(part)


def _all_gather_rows(blk):
    m_per, n = blk.shape

    def body(x_ref, out_ref, send_sems, recv_sems, local_sem):
        x, y, c = _coords()
        me, sibling = (x, y, c), (x, y, 1 - c)
        chips = _other_chips(x, y)

        def rows(px, py, pc):
            return out_ref.at[pl.ds((4 * px + 2 * py + pc) * m_per, m_per), :]

        def copy(k, block, to, src=None):
            return _remote(rows(*block) if src is None else src, rows(*block), (send_sems.at[k], recv_sems.at[k]), to)

        mine = pltpu.make_async_copy(x_ref, rows(*me), local_sem)
        mine.start()
        first = [copy(0, me, sibling, src=x_ref)]
        first += [copy(1 + j, me, (*chip, c), src=x_ref) for j, chip in enumerate(chips)]
        for cp in first:
            cp.start()
        passed = [copy(4 + j, (*chip, c), sibling) for j, chip in enumerate(chips)]
        for j, chip in enumerate(chips):
            copy(1 + j, (*chip, c), me).wait_recv()
            passed[j].start()
        copy(0, sibling, me).wait_recv()
        for j, chip in enumerate(chips):
            copy(4 + j, (*chip, 1 - c), me).wait_recv()
        for cp in first + passed:
            cp.wait_send()
        mine.wait()

    return pl.pallas_call(
        body, name="all_gather_small", in_specs=[VMEM], out_specs=VMEM,
        out_shape=_sds((N_DEV * m_per, n), blk.dtype),
        scratch_shapes=[pltpu.SemaphoreType.DMA((7,)), pltpu.SemaphoreType.DMA((7,)), pltpu.SemaphoreType.DMA],
        compiler_params=pltpu.CompilerParams(vmem_limit_bytes=32 * MIB),
    )(blk)


def _adamw_math(w, g, m, v):
    m = ADAM_B1 * m + (1.0 - ADAM_B1) * g
    v = ADAM_B2 * v + (1.0 - ADAM_B2) * (g * g)
    m_hat = m / (1.0 - ADAM_B1 ** ADAM_STEP)
    v_hat = v / (1.0 - ADAM_B2 ** ADAM_STEP)
    delta = -ADAM_LR * (m_hat / (jnp.sqrt(v_hat) + ADAM_EPS) + ADAM_WD * w)
    return delta, m, v


def _adamw_w_in(w, g0, g1, m, v, pos_chunks):
    def body(t_ref, w_ref, g0_ref, g1_ref, m_ref, v_ref, g_ref, d_ref, nm_ref, nv_ref):
        for l, src in enumerate((g0_ref, g1_ref)):
            g = src[...]
            g_ref[l] = g
            d_ref[l], nm_ref[l], nv_ref[l] = _adamw_math(w_ref[l], g, m_ref[l], v_ref[l])

    nat = pl.BlockSpec((DEPTH, CHUNK_ROWS, D_MODEL), lambda t, tbl: (0, t, 0))
    per = pl.BlockSpec((CHUNK_ROWS, D_MODEL), lambda t, tbl: (tbl[t], 0))
    return pl.pallas_call(
        body, name="adamw_w_in",
        grid_spec=pltpu.PrefetchScalarGridSpec(num_scalar_prefetch=1, grid=(N_CHUNKS,),
                                               in_specs=[nat, per, per, nat, nat], out_specs=[nat] * 4),
        out_shape=[_sds(w.shape)] * 4,
        compiler_params=pltpu.CompilerParams(dimension_semantics=("arbitrary",)),
    )(*_in_hbm(pos_chunks, w, g0, g1, m, v))


def _adamw_w_out(w, g0, g1, m, v):
    def body(w_ref, g0_ref, g1_ref, m_ref, v_ref, g_ref, d_ref, nm_ref, nv_ref):
        g = jnp.where(pl.program_id(0) == 0, g0_ref[...], g1_ref[...])
        g_ref[...] = g
        d_ref[...], nm_ref[...], nv_ref[...] = _adamw_math(w_ref[...], g, m_ref[...], v_ref[...])

    blk = pl.BlockSpec((None, W_OUT_BLK, D_MODEL), lambda l: (l, 0, 0))
    gblk = _full((W_OUT_BLK, D_MODEL))
    return pl.pallas_call(
        body, name="adamw_w_out", grid=(DEPTH,), in_specs=[blk, gblk, gblk, blk, blk], out_specs=[blk] * 4,
        out_shape=[_sds(w.shape)] * 4,
        compiler_params=pltpu.CompilerParams(dimension_semantics=("arbitrary",), vmem_limit_bytes=32 * MIB),
    )(*_in_hbm(w, g0, g1, m, v))


def _w_ada_grad_adamw(cond_t, dada, w, m, v):
    _, rows, cols = w.shape
    tr = 256

    def body(ct_ref, da_ref, w_ref, m_ref, v_ref, g_ref, d_ref, nm_ref, nv_ref):
        g = jnp.dot(ct_ref[...], da_ref[...].astype(BF16), preferred_element_type=F32)
        g_ref[...] = g
        d_ref[...], nm_ref[...], nv_ref[...] = _adamw_math(w_ref[...], g, m_ref[...], v_ref[...])

    blk = pl.BlockSpec((None, tr, cols), lambda l, t: (l, t, 0))
    return pl.pallas_call(
        body, name="w_ada_grad_adamw", grid=(DEPTH, rows // tr),
        in_specs=[pl.BlockSpec((tr, BLOCK), lambda l, t: (t, 0)), pl.BlockSpec((None, BLOCK, cols), lambda l, t: (l, 0, 0)),
                  blk, blk, blk],
        out_specs=[blk] * 4, out_shape=[_sds(w.shape)] * 4,
        compiler_params=pltpu.CompilerParams(dimension_semantics=("arbitrary", "arbitrary"), vmem_limit_bytes=32 * MIB),
    )(*_in_hbm(cond_t, dada, w, m, v))


def _small_sum_adamw(gathered_a, gathered_b, w, m, v):
    def body(a_ref, b_ref, w_ref, m_ref, v_ref, g_ref, d_ref, nm_ref, nv_ref):
        def total(ref):
            g = ref[0]
            for b in range(1, N_DEV):
                g = g + ref[b]
            return g

        g = jnp.concatenate([total(a_ref), total(b_ref)], axis=0)
        g_ref[...] = g
        d_ref[...], nm_ref[...], nv_ref[...] = _adamw_math(w_ref[...], g, m_ref[...], v_ref[...])

    return pl.pallas_call(
        body, name="small_sum_adamw", in_specs=[VMEM] * 5, out_specs=[VMEM] * 4, out_shape=[_sds(w.shape)] * 4,
        compiler_params=pltpu.CompilerParams(vmem_limit_bytes=48 * MIB),
    )(gathered_a, gathered_b, w, m, v)


_SMALL_A = (("w_s", DEPTH * 8 * BLOCK), ("b_s", DEPTH * 8), ("q_gain", 1), ("k_gain", 1), ("sink", 1))
_SMALL_B = (("b_ada", DEPTH * 24), ("norm_gain", DEPTH * 8), ("sq_err", 1))


def _pack_rows(parts, layout):
    rows = []
    for name, n in layout:
        flat = parts[name].reshape(-1)
        rows.append(jnp.pad(flat, (0, n * 128 - flat.shape[0])).reshape(n, 128))
    n_rows = sum(n for _, n in layout)
    if n_rows % 8:
        rows.append(jnp.zeros((-n_rows % 8, 128), F32))
    return jnp.concatenate(rows, axis=0)


def _pack_small(parts):
    return jnp.concatenate([_pack_rows(parts, _SMALL_A), _pack_rows(parts, _SMALL_B)], axis=0)


def _unpack_small(packed, shapes):
    out, r0 = {}, 0
    for layout in (_SMALL_A, _SMALL_B):
        for name, n in layout:
            size = 1
            for d in shapes[name]:
                size *= d
            out[name] = packed[r0:r0 + n].reshape(-1)[:size].reshape(shapes[name])
            r0 += n
        r0 += -r0 % 8
    return out


def _permute_heads(a, axis):
    shp = a.shape
    a = a.reshape(shp[:axis] + (2, 4, HEAD_DIM) + shp[axis + 1:])
    a = jnp.swapaxes(a, axis, axis + 1)
    return a.reshape(shp)


def _unpermute_heads(a, axis):
    shp = a.shape
    a = a.reshape(shp[:axis] + (4, 2, HEAD_DIM) + shp[axis + 1:])
    a = jnp.swapaxes(a, axis, axis + 1)
    return a.reshape(shp)


def _permute_w_out(w):
    return jnp.concatenate([_permute_heads(w[:D_ATTN], 0), w[D_ATTN:]], axis=0)


def _unpermute_w_out(w):
    return jnp.concatenate([_unpermute_heads(w[:D_ATTN], 0), w[D_ATTN:]], axis=0)


def kernel(x, c, w_ada, b_ada, norm_gain, w_in, q_gain, k_gain, sink, w_s, b_s, w_out, loss_target, m_w_ada, m_b_ada, m_norm_gain, m_w_in, m_q_gain, m_k_gain, m_sink, m_w_s, m_b_s, m_w_out, v_w_ada, v_b_ada, v_norm_gain, v_w_in, v_q_gain, v_k_gain, v_sink, v_w_s, v_b_s, v_w_out):
    ix, iy, ic = _coords()
    chip = 2 * ix + iy
    chip_idx = jnp.stack([chip, ic]).astype(jnp.int32)
    core_idx = jnp.reshape(ic, (1,)).astype(jnp.int32)
    src_chunks = lax.dynamic_index_in_dim(jnp.asarray(_CHUNK_SRC), chip, 0, keepdims=False)
    pos_chunks = lax.dynamic_index_in_dim(jnp.asarray(_CHUNK_POS), chip, 0, keepdims=False)
    x0, target = x[0], loss_target[0]

    wt, mt, vt = (jnp.swapaxes(a, 1, 2) for a in (w_in, m_w_in, v_w_in))
    w0_mine, wloc_in1 = _cast_permute_w_in(wt, jnp.concatenate([chip_idx[:1], src_chunks]))
    c_all, wloc_out, w0 = _gather_inputs(c, w_out, w0_mine)
    wts = [w0.reshape(D_IN, D_MODEL), None]

    b_blk = lax.dynamic_slice_in_dim(b_ada, chip * W_ADA_BLK, W_ADA_BLK, axis=1)
    ada_part, cond = _ada_rows(c_all.reshape(N_DEV, D_MODEL), w_ada, b_blk)
    ada = jnp.moveaxis(_exchange_ada(ada_part), 0, 1).reshape(DEPTH, 3 * D_MODEL)
    shift = [ada[l:l + 1, 0:D_MODEL] for l in range(DEPTH)]
    scale = [ada[l:l + 1, D_MODEL:2 * D_MODEL] for l in range(DEPTH)]
    gate = [ada[l:l + 1, 2 * D_MODEL:] for l in range(DEPTH)]
    ng = [norm_gain[l:l + 1] for l in range(DEPTH)]

    qg2 = jnp.concatenate([q_gain, q_gain], axis=-1)
    kg2 = jnp.concatenate([k_gain, k_gain], axis=-1)
    ws_b = w_s.astype(BF16)
    wst_b = jnp.swapaxes(w_s, -1, -2).astype(BF16)
    bsp = jnp.repeat(jnp.swapaxes(b_s.reshape(DEPTH, 4, 2, BLOCK), -1, -2), HEAD_DIM, axis=-1)
    bias = jnp.asarray(_bias_table())

    def mix_args(l):
        return bias, sink[l], qg2[l:l + 1], kg2[l:l + 1], ws_b[l]

    w_out_shape = (W_OUT_BLK, D_MODEL)
    proj0, wo0, wo1 = _proj_fwd(x0, ng[0], scale[0], shift[0], wts[0], "proj_fwd_0",
                                jobs=[_job_gather([(wloc_out, 0), (wloc_out, 1)], [w_out_shape, w_out_shape])])
    y0, *kept0, w1 = _mix_fwd(proj0, *mix_args(0), bsp[0], "mix_fwd_0",
                              jobs=[_job_gather([(wloc_in1, None)], [(W_IN_BLK, D_MODEL)])])
    wts[1] = w1.reshape(D_IN, D_MODEL)
    wos = [_permute_w_out(w.reshape(D_MODEL, D_MODEL)) for w in (wo0, wo1)]
    x1, proj1 = _out_proj_fwd(y0, x0, gate[0], wos[0], ng[1], scale[1], shift[1], wts[1], "out_proj_fwd_01")
    y1, *kept1 = _mix_fwd(proj1, *mix_args(1), bsp[1], "mix_fwd_1")

    def blocks_out(gw):
        return _unpermute_w_out(gw).reshape(N_CHIPS, W_OUT_BLK, D_MODEL)

    dx2, sq, dy1, gwo1, dgate1 = _out_loss_bwd(y1, x1, gate[1], wos[1], target, "out_loss_bwd_1")
    go1 = blocks_out(gwo1)
    dpb, dkv, p0, p2, dqg1, dkg1, dsink1, dws1, dbsp1, ro1 = _mix_bwd(
        dy1, proj1, *kept1, qg2[1:2], kg2[1:2], wst_b[1], "mix_bwd_1", jobs=[_job_swap(go1)])
    po1, = _pair_sum(go1, ro1, core_idx, "pair_sum_w_out_1")
    gwi1, dkvb, co1 = _w_in_grad(dpb, dkv, p0, p2, x1, ng[1], scale[1], shift[1], "w_in_grad_1", jobs=[_job_scatter(po1)])
    gi1 = gwi1.reshape(N_CHIPS, W_IN_BLK, D_MODEL)
    fo1 = _chip_sum(po1, co1, chip_idx, "chip_sum_w_out_1")
    dx1, dsh1, dsc1, dng1, grad_wo1, ri1 = _proj_bwd(dpb, dkvb, x1, dx2, ng[1], scale[1], wts[1], "proj_bwd_1",
                                                     jobs=[_job_join(fo1), _job_swap(gi1)])
    pi1, = _pair_sum(gi1, ri1, core_idx, "pair_sum_w_in_1")

    dy0, gwo0, dgate0 = _out_bwd(dx1, y0, gate[0], wos[0], "out_bwd_0")
    go0 = blocks_out(gwo0)
    dpb, dkv, p0, p2, dqg0, dkg0, dsink0, dws0, dbsp0, ci1, ro0 = _mix_bwd(
        dy0, proj0, *kept0, qg2[0:1], kg2[0:1], wst_b[0], "mix_bwd_0", jobs=[_job_scatter(pi1), _job_swap(go0)])
    fi1 = _chip_sum(pi1, ci1, chip_idx, "chip_sum_w_in_1")
    po0, = _pair_sum(go0, ro0, core_idx, "pair_sum_w_out_0")

    def bs_grad(dbsp):
        return jnp.swapaxes(dbsp[:, :, ::HEAD_DIM], -1, -2).reshape(8, BLOCK)

    small_g = dict(
        w_s=jnp.stack([dws0, dws1]), b_s=jnp.stack([bs_grad(dbsp0), bs_grad(dbsp1)]),
        q_gain=jnp.stack([dqg0[0, :HEAD_DIM], dqg1[0, :HEAD_DIM]]), k_gain=jnp.stack([dkg0[0, :HEAD_DIM], dkg1[0, :HEAD_DIM]]),
        sink=jnp.stack([dsink0[0, :N_HEADS], dsink1[0, :N_HEADS]]))
    gwi0, dkvb, grad_wi1, co0, gathered_a = _w_in_grad(
        dpb, dkv, p0, p2, x0, ng[0], scale[0], shift[0], "w_in_grad_0",
        jobs=[_job_join(fi1), _job_scatter(po0), _job_all_gather(_pack_rows(small_g, _SMALL_A))])
    gi0 = gwi0.reshape(N_CHIPS, W_IN_BLK, D_MODEL)
    fo0 = _chip_sum(po0, co0, chip_idx, "chip_sum_w_out_0")

    ri0, grad_wo0 = _comm([_job_swap(gi0), _job_join(fo0)], "swap_w_in_0")
    pi0, pi0_send = _pair_sum(gi0, ri0, core_idx, "pair_sum_w_in_0", send_dtype=BF16)
    dx0, dsh0, dsc0, dng0, ci0 = _proj_bwd(dpb, dkvb, x0, dx1, ng[0], scale[0], wts[0], "proj_bwd_0",
                                           jobs=[_job_scatter(pi0_send)])
    fi0 = _chip_sum(pi0, ci0, chip_idx, "chip_sum_w_in_0")
    grad_wi0, = _comm([_job_join(fi0)], "join_w_in_0")

    small_g.update(
        b_ada=jnp.stack([jnp.concatenate([dsh0, dsc0, dgate0], axis=-1)[0], jnp.concatenate([dsh1, dsc1, dgate1], axis=-1)[0]]),
        norm_gain=jnp.stack([dng0[0], dng1[0]]), sq_err=sq[0])
    none = jnp.zeros((1,), F32)
    small_w = dict(w_s=w_s, b_s=b_s, b_ada=b_ada, norm_gain=norm_gain, q_gain=q_gain, k_gain=k_gain, sink=sink, sq_err=none)
    small_m = dict(w_s=m_w_s, b_s=m_b_s, b_ada=m_b_ada, norm_gain=m_norm_gain, q_gain=m_q_gain, k_gain=m_k_gain, sink=m_sink,
                   sq_err=none)
    small_v = dict(w_s=v_w_s, b_s=v_b_s, b_ada=v_b_ada, norm_gain=v_norm_gain, q_gain=v_q_gain, k_gain=v_k_gain, sink=v_sink,
                   sq_err=none)
    gathered_b = _all_gather_rows(_pack_rows(small_g, _SMALL_B))
    packed = _small_sum_adamw(gathered_a.reshape(N_DEV, -1, 128), gathered_b.reshape(N_DEV, -1, 128),
                              _pack_small(small_w), _pack_small(small_m), _pack_small(small_v))
    shapes = {k: a.shape for k, a in small_w.items()}
    sg, sd, sm, sv = (_unpack_small(p, shapes) for p in packed)
    loss = 0.5 * sg["sq_err"][0]

    dada_all = gathered_b.reshape(N_DEV, -1, 128)[:, 0:DEPTH * 24].reshape(N_DEV, DEPTH, 3 * D_MODEL)
    dada_blk = jnp.moveaxis(lax.dynamic_slice_in_dim(dada_all, chip * W_ADA_BLK, W_ADA_BLK, axis=2), 0, 1)
    pad = BLOCK - N_DEV
    ada_out = _w_ada_grad_adamw(
        jnp.pad(cond.T, ((0, 0), (0, pad))).astype(BF16), jnp.pad(dada_blk, ((0, 0), (0, pad), (0, 0))),
        w_ada, m_w_ada, v_w_ada)

    in_out = [jnp.swapaxes(a, 1, 2) for a in _adamw_w_in(wt, grad_wi0, grad_wi1, mt, vt, pos_chunks)]
    out_out = _adamw_w_out(w_out, grad_wo0, grad_wo1, m_w_out, v_w_out)

    def ordered(k):
        small = (sg, sd, sm, sv)[k]
        return (ada_out[k], small["b_ada"], small["norm_gain"], in_out[k], small["q_gain"], small["k_gain"], small["sink"],
                small["w_s"], small["b_s"], out_out[k])

    return (loss, dx0[None], *ordered(0), *ordered(1), *ordered(2), *ordered(3))
```

```python
import numpy as np

import jax
import jax.numpy as jnp
from jax import lax
from jax.experimental import pallas as pl
from jax.experimental.pallas import tpu as pltpu

F32 = jnp.float32
BF16 = jnp.bfloat16

D_MODEL = 1024
DEPTH = 2
HEAD_DIM = 64
N_HEADS = 8
BLOCK = 128
SUB = 4
TILE = SUB * BLOCK
STACK = 8 * BLOCK
D_ATTN = 512
D_KV = 128
D_IN = 2816
N_CHIPS = 4
N_DEV = 8
W_IN_BLK = D_IN // N_CHIPS
W_OUT_BLK = D_MODEL // N_CHIPS
W_ADA_BLK = 3 * D_MODEL // N_CHIPS
CHUNK_ROWS = HEAD_DIM
N_CHUNKS = W_IN_BLK // CHUNK_ROWS
EPS = 1e-6
NEG_INF = -1e30

C_Q, C_K, C_V, C_GA, C_U, C_VG, C_GG = 0, 512, 640, 768, 1280, 1792, 2304

ADAM_LR = 0.001
ADAM_B1 = 0.9
ADAM_B2 = 0.999
ADAM_EPS = 1e-08
ADAM_WD = 0.01
ADAM_STEP = 10

MESH = pl.DeviceIdType.MESH
MIB = 1024 * 1024
ANY = pl.BlockSpec(memory_space=pl.ANY)
VMEM = pl.BlockSpec(memory_space=pltpu.VMEM)

NT_DIMS = (((1,), (1,)), ((), ()))
TN_DIMS = (((0,), (0,)), ((), ()))

_PAIR_ORDER = (0, 4, 1, 5, 2, 6, 3, 7)
_CHUNK_SRC = np.array([
    list(_PAIR_ORDER) + [8, 9, 10],
    [0] + [1 + h for h in _PAIR_ORDER] + [9, 10],
    list(range(N_CHUNKS)),
    list(range(N_CHUNKS)),
], np.int32)
_CHUNK_POS = np.argsort(_CHUNK_SRC, axis=1).astype(np.int32)


def _bias_table():
    i = np.arange(N_HEADS * BLOCK)[:, None]
    j = np.arange(3 * BLOCK)[None, :]
    dist = np.abs(j - BLOCK - (i % BLOCK))
    slope = 2.0 ** -(i // BLOCK + 1.0)
    inner = np.where(dist <= BLOCK, -(slope * dist), NEG_INF)
    first = np.where(j >= BLOCK, inner, NEG_INF)
    last = np.where(j < 2 * BLOCK, inner, NEG_INF)
    return np.stack([first, inner, last]).astype(np.float32)


def _full(shape):
    n = len(shape)
    return pl.BlockSpec(shape, lambda *_: (0,) * n)


def _sds(shape, dtype=F32):
    return jax.ShapeDtypeStruct(shape, dtype)


def _coords():
    return lax.axis_index("x"), lax.axis_index("y"), lax.axis_index("c")


def _other_chips(x, y):
    return [(1 - x, y), (x, 1 - y), (1 - x, 1 - y)]


def _in_hbm(*operands):
    return [pltpu.with_memory_space_constraint(a, pltpu.HBM) if a.size * a.dtype.itemsize >= MIB // 4 else a
            for a in operands]


def _remote(src, dst, sems, dev):
    return pltpu.make_async_remote_copy(src_ref=src, dst_ref=dst, send_sem=sems[0], recv_sem=sems[1],
                                        device_id=dev, device_id_type=MESH)


class _Job:
    def __init__(self, inputs, out_shapes, n_remote, n_local, make, then=None, in_place=False):
        self.inputs, self.out_shapes, self.n_remote, self.n_local, self.make = inputs, out_shapes, n_remote, n_local, make
        self.then = then
        self.in_place = in_place


def _job_aliases(jobs, in_base, out_base):
    aliases, a, b = {}, 0, 0
    for j in jobs:
        if j.in_place:
            aliases.update({in_base + a + k: out_base + b + k for k in range(len(j.inputs))})
        a, b = a + len(j.inputs), b + len(j.out_shapes)
    return aliases


def _job_copies(jobs, jin, jout, sems, second=False):
    send, recv, loc = sems
    res, a, b, r, l = [], 0, 0, 0, 0
    for j in jobs:
        build = j.then if second else j.make
        if build is not None:
            res += build(jin[a:a + len(j.inputs)], jout[b:b + len(j.out_shapes)],
                         lambda k, r=r: (send.at[r + k], recv.at[r + k]), lambda k, l=l: loc.at[l + k])
        a, b, r, l = a + len(j.inputs), b + len(j.out_shapes), r + j.n_remote, l + j.n_local
    return res


def _run(copies):
    for cp in copies:
        cp.start()
    for cp in copies:
        cp.wait()


def _job_gather(sources, shapes):
    n = len(sources)

    def make(ins, outs, rsem, lsem):
        x, y, c = _coords()
        j = 2 * x + y
        res = []
        for t, ((_, layer), src, dst) in enumerate(zip(sources, ins, outs)):
            src = src if layer is None else src.at[layer]
            res.append(pltpu.make_async_copy(src, dst.at[j], lsem(t)))
            for k, chip in enumerate(_other_chips(x, y)):
                res.append(_remote(src, dst.at[j], rsem(3 * t + k), (*chip, c)))
        return res

    return _Job([a for a, _ in sources], [_sds((N_CHIPS,) + s, BF16) for s in shapes], 3 * n, n, make)


def _job_swap(g):
    _, rows, cols = g.shape
    half = rows // 2

    def make(ins, outs, rsem, lsem):
        x, y, c = _coords()
        return [_remote(ins[0].at[:, pl.ds((1 - c) * half, half), :], outs[0], rsem(0), (x, y, 1 - c))]

    return _Job([g], [_sds((N_CHIPS, half, cols))], 1, 0, make)


def _job_scatter(p):
    def make(ins, outs, rsem, lsem):
        x, y, c = _coords()
        return [_remote(ins[0].at[2 * chip[0] + chip[1]], outs[0].at[k], rsem(k), (*chip, c))
                for k, chip in enumerate(_other_chips(x, y))]

    return _Job([p], [_sds((3,) + p.shape[1:], p.dtype)], 3, 0, make)


def _job_all_gather(blk):
    m_per = blk.shape[0]

    def rows(ref, px, py, pc):
        return ref.at[pl.ds((4 * px + 2 * py + pc) * m_per, m_per), :]

    def make(ins, outs, rsem, lsem):
        x, y, c = _coords()
        res = [pltpu.make_async_copy(ins[0], rows(outs[0], x, y, c), lsem(0)),
               _remote(ins[0], rows(outs[0], x, y, c), rsem(0), (x, y, 1 - c))]
        res += [_remote(ins[0], rows(outs[0], x, y, c), rsem(1 + k), (*chip, c)) for k, chip in enumerate(_other_chips(x, y))]
        return res

    def then(ins, outs, rsem, lsem):
        x, y, c = _coords()
        return [_remote(rows(outs[0], *chip, c), rows(outs[0], *chip, c), rsem(4 + k), (x, y, 1 - c))
                for k, chip in enumerate(_other_chips(x, y))]

    return _Job([blk], [_sds((N_DEV * m_per, blk.shape[1]), blk.dtype)], 7, 1, make, then)


def _job_join(f):
    half = f.shape[0] // 2

    def make(ins, outs, rsem, lsem):
        x, y, c = _coords()
        mine = pl.ds(c * half, half)
        return [_remote(ins[0].at[mine, :], outs[0].at[mine, :], rsem(0), (x, y, 1 - c))]

    return _Job([f], [_sds(f.shape, f.dtype)], 1, 0, make, in_place=True)


def _pallas(body, *, name, grid, in_specs, out_specs, out_shape, operands, vmem_mib, jobs=(), scratch=()):
    in_specs, out_specs, out_shape = list(in_specs), list(out_specs), list(out_shape)
    n_in, n_out = len(in_specs), len(out_specs)
    j_in = [a for j in jobs for a in j.inputs]
    j_out = [s for j in jobs for s in j.out_shapes]
    n_rem = max(1, sum(j.n_remote for j in jobs))
    n_loc = max(1, sum(j.n_local for j in jobs))
    sems = [pltpu.SemaphoreType.DMA((n_rem,)), pltpu.SemaphoreType.DMA((n_rem,)),
            pltpu.SemaphoreType.DMA((n_loc,))] if jobs else []
    scratch = list(scratch) + sems

    def wrapped(*refs):
        ins = refs[:n_in]
        jin = refs[n_in:n_in + len(j_in)]
        outs = refs[n_in + len(j_in):n_in + len(j_in) + n_out]
        jout = refs[n_in + len(j_in) + n_out:n_in + len(j_in) + n_out + len(j_out)]
        own = refs[n_in + len(j_in) + n_out + len(j_out):len(refs) - len(sems)]

        if jobs:
            first = last = None
            for d, n in enumerate(grid):
                f, e = pl.program_id(d) == 0, pl.program_id(d) == n - 1
                first, last = (f, e) if first is None else (first & f, last & e)

            @pl.when(first)
            def _():
                for cp in _job_copies(jobs, jin, jout, refs[-3:]):
                    cp.start()

        body(*ins, *outs, *own)

        if jobs:
            @pl.when(last)
            def _():
                for cp in _job_copies(jobs, jin, jout, refs[-3:]):
                    cp.wait()
                _run(_job_copies(jobs, jin, jout, refs[-3:], second=True))

    return pl.pallas_call(
        wrapped, name=name, grid=grid,
        in_specs=in_specs + [ANY] * len(j_in), out_specs=out_specs + [ANY] * len(j_out),
        out_shape=out_shape + j_out, scratch_shapes=scratch, input_output_aliases=_job_aliases(jobs, n_in, n_out),
        compiler_params=pltpu.CompilerParams(dimension_semantics=("arbitrary",) * len(grid),
                                             vmem_limit_bytes=vmem_mib * MIB),
    )(*_in_hbm(*operands, *j_in))


def _comm(jobs, name):
    j_in = [a for j in jobs for a in j.inputs]
    j_out = [s for j in jobs for s in j.out_shapes]
    n_rem = max(1, sum(j.n_remote for j in jobs))
    n_loc = max(1, sum(j.n_local for j in jobs))

    def body(*refs):
        jin, jout = refs[:len(j_in)], refs[len(j_in):len(j_in) + len(j_out)]
        _run(_job_copies(jobs, jin, jout, refs[-3:]))
        _run(_job_copies(jobs, jin, jout, refs[-3:], second=True))

    return pl.pallas_call(
        body, name=name, in_specs=[ANY] * len(j_in), out_specs=[ANY] * len(j_out), out_shape=j_out,
        scratch_shapes=[pltpu.SemaphoreType.DMA((n_rem,)), pltpu.SemaphoreType.DMA((n_rem,)),
                        pltpu.SemaphoreType.DMA((n_loc,))],
        input_output_aliases=_job_aliases(jobs, 0, 0),
    )(*_in_hbm(*j_in))


def _sigmoid(x):
    return 1.0 / (1.0 + jnp.exp(-x))


def _lo_mask(shape):
    return lax.broadcasted_iota(jnp.int32, shape, len(shape) - 1) < HEAD_DIM


def _half_sum(x, lo):
    a = jnp.sum(jnp.where(lo, x, 0.0), axis=-1, keepdims=True)
    b = jnp.sum(jnp.where(lo, 0.0, x), axis=-1, keepdims=True)
    return jnp.where(lo, a, b)


def _half_rms_scale(x, lo):
    return lax.rsqrt(_half_sum(x * x, lo) * (1.0 / HEAD_DIM) + EPS)


def _stack_heads(pairs, lo):
    return jnp.concatenate([jnp.where(lo, t, 0.0) for t in pairs] + [jnp.where(lo, 0.0, t) for t in pairs], axis=0)


def _unstack_pair(stack, p, lo):
    return jnp.where(lo, stack[BLOCK * p:BLOCK * (p + 1)], stack[BLOCK * (4 + p):BLOCK * (5 + p)])


def _attention_probs(q_stack, kn, bias_ref, sink_ref):
    rows = N_HEADS * BLOCK
    s = lax.dot_general(q_stack, kn, NT_DIMS, preferred_element_type=F32) + bias_ref[...]
    sink = jnp.concatenate([jnp.full((BLOCK, BLOCK), sink_ref[h], F32) for h in range(N_HEADS)], axis=0)
    cols = [s[:, BLOCK * j:BLOCK * (j + 1)] for j in range(3)]
    top = jnp.max(jnp.maximum(jnp.maximum(cols[0], cols[1]), cols[2]), axis=-1, keepdims=True)
    m = jnp.maximum(jnp.broadcast_to(top, (rows, BLOCK)), sink)
    e = [jnp.exp(c - m) for c in cols]
    es = jnp.exp(sink - m)
    inv = 1.0 / (jnp.broadcast_to(jnp.sum((e[0] + e[1]) + e[2], axis=-1, keepdims=True), (rows, BLOCK)) + es)
    return jnp.concatenate([c * inv for c in e], axis=1), es * inv


def _kv_rows(cur_ref, pkv_ref, nkv_ref):
    k = jnp.concatenate([pkv_ref[:, 0:D_KV], cur_ref[:, C_K:C_K + D_KV], nkv_ref[:, 0:D_KV]], axis=0)
    v = jnp.concatenate([pkv_ref[:, D_KV:2 * D_KV], cur_ref[:, C_V:C_V + D_KV], nkv_ref[:, D_KV:2 * D_KV]], axis=0)
    return k, v


def _overlap_add(parts):
    blocks = []
    for j in range(SUB + 2):
        terms = [parts[b][BLOCK * (j - b):BLOCK * (j - b + 1)] for b in range(SUB) if 0 <= j - b <= 2]
        total = terms[0]
        for t in terms[1:]:
            total = total + t
        blocks.append(total)
    return jnp.concatenate(blocks, axis=0)


def _mix_specs(nt):
    cur = pl.BlockSpec((TILE, D_IN), lambda i: (i, 0))
    kv_col = C_K // (2 * D_KV)
    pkv = pl.BlockSpec((BLOCK, 2 * D_KV), lambda i: (jnp.maximum(i * SUB - 1, 0), kv_col))
    nkv = pl.BlockSpec((BLOCK, 2 * D_KV), lambda i: (jnp.minimum((i + 1) * SUB, nt * SUB - 1), kv_col))
    table = (None, N_HEADS * BLOCK, 3 * BLOCK)
    first = pl.BlockSpec(table, lambda i: (jnp.where(i == 0, 0, 1), 0, 0))
    inner = pl.BlockSpec(table, lambda i: (1, 0, 0))
    last = pl.BlockSpec(table, lambda i: (jnp.where(i == nt - 1, 2, 1), 0, 0))
    return cur, pkv, nkv, [first] + [inner] * (SUB - 2) + [last]


def _proj_fwd(x, ng, scale, shift, wt, name, jobs=()):
    s = x.shape[0]
    ts = min(512, s)

    def body(x_ref, ng_ref, sc_ref, sh_ref, w_ref, o_ref):
        xv = x_ref[...]
        r = lax.rsqrt(jnp.mean(xv * xv, axis=-1, keepdims=True) + EPS)
        h = ((xv * r) * ng_ref[...]) * (1.0 + sc_ref[...]) + sh_ref[...]
        o_ref[...] = lax.dot_general(h.astype(BF16), w_ref[...], NT_DIMS, preferred_element_type=F32)

    vec = _full((1, D_MODEL))
    return _pallas(
        body, name=name, grid=(s // ts,),
        in_specs=[pl.BlockSpec((ts, D_MODEL), lambda i: (i, 0)), vec, vec, vec, _full((D_IN, D_MODEL))],
        out_specs=[pl.BlockSpec((ts, D_IN), lambda i: (i, 0))], out_shape=[_sds((s, D_IN))],
        operands=(x, ng, scale, shift, wt), vmem_mib=48, jobs=jobs)


def _diagonal():
    return lax.broadcasted_iota(jnp.int32, (BLOCK, BLOCK), 0) == lax.broadcasted_iota(jnp.int32, (BLOCK, BLOCK), 1)


def _column_as_row(wide, eye):
    return jnp.sum(jnp.where(eye, wide, 0.0), axis=0, keepdims=True)


def _mix_fwd(proj, bias, sink, qg2, kg2, ws, bsp, name, jobs=()):
    s = proj.shape[0]
    nt = s // TILE

    def body(sink_ref, cur_ref, pkv_ref, nkv_ref, *rest):
        bias_refs = rest[:SUB]
        qg_ref, kg_ref, ws_ref, bsp_ref, y_ref, p_ref, ps_ref, attn_ref, sv_ref = rest[SUB:]
        lo = _lo_mask((BLOCK, BLOCK))
        eye = _diagonal()
        lo_kv = _lo_mask((TILE + 2 * BLOCK, BLOCK))
        k_all, v_all = _kv_rows(cur_ref, pkv_ref, nkv_ref)
        kn_all = ((k_all * _half_rms_scale(k_all, lo_kv)) * kg_ref[...]).astype(BF16)
        vb_all = v_all.astype(BF16)
        for b in range(SUB):
            rows = slice(BLOCK * b, BLOCK * (b + 1))
            window = slice(BLOCK * b, BLOCK * (b + 3))
            qn = []
            for p in range(4):
                q = cur_ref[rows, C_Q + BLOCK * p:C_Q + BLOCK * (p + 1)]
                qn.append(((q * _half_rms_scale(q, lo)) * qg_ref[...]) * 0.125)
            q_stack = _stack_heads(qn, lo).astype(BF16)
            prob, psink = _attention_probs(q_stack, kn_all[window], bias_refs[b], sink_ref)
            pb = prob.astype(BF16)
            p_ref[STACK * b:STACK * (b + 1), :] = pb
            ps_ref[N_HEADS * b:N_HEADS * (b + 1), :] = jnp.concatenate(
                [_column_as_row(psink[BLOCK * h:BLOCK * (h + 1)], eye) for h in range(N_HEADS)], axis=0)
            o_stack = jnp.dot(pb, vb_all[window], preferred_element_type=F32)
            for p in range(4):
                g = cur_ref[rows, C_GA + BLOCK * p:C_GA + BLOCK * (p + 1)]
                attn = _unstack_pair(o_stack, p, lo)
                attn_ref[rows, BLOCK * p:BLOCK * (p + 1)] = attn.astype(BF16)
                y_ref[rows, BLOCK * p:BLOCK * (p + 1)] = (attn * (g * _sigmoid(g))).astype(BF16)

        for p in range(4):
            cols = slice(C_VG + BLOCK * p, C_VG + BLOCK * (p + 1))
            vn = []
            for b in range(SUB):
                vg = cur_ref[BLOCK * b:BLOCK * (b + 1), cols]
                vn.append((vg * _half_rms_scale(vg, lo)).astype(BF16))
            vn = jnp.concatenate(vn, axis=1)
            sv_a = jnp.dot(ws_ref[2 * p], vn, preferred_element_type=F32)
            sv_b = jnp.dot(ws_ref[2 * p + 1], vn, preferred_element_type=F32)
            for b in range(SUB):
                rows = slice(BLOCK * b, BLOCK * (b + 1))
                lanes = slice(BLOCK * b, BLOCK * (b + 1))
                sv = jnp.where(lo, sv_a[:, lanes], sv_b[:, lanes]) + bsp_ref[p]
                sv_ref[rows, BLOCK * p:BLOCK * (p + 1)] = sv.astype(BF16)
                u = cur_ref[rows, C_U + BLOCK * p:C_U + BLOCK * (p + 1)]
                g = cur_ref[rows, C_GG + BLOCK * p:C_GG + BLOCK * (p + 1)]
                y_ref[rows, D_ATTN + BLOCK * p:D_ATTN + BLOCK * (p + 1)] = ((u * sv) * (g * _sigmoid(g))).astype(BF16)

    cur, pkv, nkv, bias_specs = _mix_specs(nt)
    nb = nt * SUB
    half = pl.BlockSpec((TILE, D_ATTN), lambda i: (i, 0))
    return _pallas(
        body, name=name, grid=(nt,),
        in_specs=[pl.BlockSpec(memory_space=pltpu.SMEM), cur, pkv, nkv, *bias_specs, _full((1, BLOCK)), _full((1, BLOCK)),
                  _full((8, BLOCK, BLOCK)), _full((4, BLOCK, BLOCK))],
        out_specs=[pl.BlockSpec((TILE, D_MODEL), lambda i: (i, 0)), pl.BlockSpec((SUB * STACK, 3 * BLOCK), lambda i: (i, 0)),
                   pl.BlockSpec((SUB * N_HEADS, BLOCK), lambda i: (i, 0)), half, half],
        out_shape=[_sds((s, D_MODEL), BF16), _sds((nb * STACK, 3 * BLOCK), BF16), _sds((nb * N_HEADS, BLOCK)),
                   _sds((s, D_ATTN), BF16), _sds((s, D_ATTN), BF16)],
        operands=(sink, proj, proj, proj, *([bias] * SUB), qg2, kg2, ws, bsp), vmem_mib=56, jobs=jobs)


def _out_proj_fwd(y, x, gate, w_out, ng, scale, shift, wt, name):
    s = x.shape[0]
    ts = min(512, s)

    def body(y_ref, x_ref, g_ref, w_ref, ng_ref, sc_ref, sh_ref, wt_ref, xn_ref, p_ref):
        xv = x_ref[...] + g_ref[...] * jnp.dot(y_ref[...], w_ref[...], preferred_element_type=F32)
        xn_ref[...] = xv
        r = lax.rsqrt(jnp.mean(xv * xv, axis=-1, keepdims=True) + EPS)
        h = ((xv * r) * ng_ref[...]) * (1.0 + sc_ref[...]) + sh_ref[...]
        p_ref[...] = lax.dot_general(h.astype(BF16), wt_ref[...], NT_DIMS, preferred_element_type=F32)

    row = pl.BlockSpec((ts, D_MODEL), lambda i: (i, 0))
    vec = _full((1, D_MODEL))
    return _pallas(
        body, name=name, grid=(s // ts,),
        in_specs=[row, row, vec, _full((D_MODEL, D_MODEL)), vec, vec, vec, _full((D_IN, D_MODEL))],
        out_specs=[row, pl.BlockSpec((ts, D_IN), lambda i: (i, 0))], out_shape=[_sds((s, D_MODEL)), _sds((s, D_IN))],
        operands=(y, x, gate, w_out, ng, scale, shift, wt), vmem_mib=56)


def _out_loss_bwd(y, x, gate, w_out, target, name):
    s = x.shape[0]
    ts = min(512, s)
    steps = s // ts

    def body(y_ref, x_ref, g_ref, w_ref, t_ref, dx_ref, sq_ref, dy_ref, gw_ref, dg_ref):
        @pl.when(pl.program_id(0) == 0)
        def _():
            sq_ref[...] = jnp.zeros_like(sq_ref)
            gw_ref[...] = jnp.zeros_like(gw_ref)

        yv = y_ref[...]
        out = x_ref[...] + g_ref[...] * jnp.dot(yv, w_ref[...], preferred_element_type=F32)
        diff = out - t_ref[...]
        dx = diff * (1.0 / D_MODEL)
        dx_ref[...] = dx
        per_token = jnp.sum(diff * diff, axis=-1, keepdims=True) * (1.0 / D_MODEL)
        sq_ref[...] += jnp.sum(per_token, axis=0, keepdims=True)
        dy_ref[...] = lax.dot_general((dx * g_ref[...]).astype(BF16), w_ref[...], NT_DIMS, preferred_element_type=F32)
        gw_ref[...] += lax.dot_general(yv, dx.astype(BF16), TN_DIMS, preferred_element_type=F32)

        @pl.when(pl.program_id(0) == steps - 1)
        def _():
            m = gw_ref[...]
            dg_ref[...] = jnp.sum(w_ref[...].astype(F32) * m, axis=0, keepdims=True)
            gw_ref[...] = m * g_ref[...]

    row = pl.BlockSpec((ts, D_MODEL), lambda i: (i, 0))
    return _pallas(
        body, name=name, grid=(s // ts,), in_specs=[row, row, _full((1, D_MODEL)), _full((D_MODEL, D_MODEL)), row],
        out_specs=[row, _full((1, 1)), row, _full((D_MODEL, D_MODEL)), _full((1, D_MODEL))],
        out_shape=[_sds((s, D_MODEL)), _sds((1, 1)), _sds((s, D_MODEL)), _sds((D_MODEL, D_MODEL)), _sds((1, D_MODEL))],
        operands=(y, x, gate, w_out, target), vmem_mib=48)


def _out_bwd(dxo, y, gate, w_out, name, jobs=()):
    s = dxo.shape[0]
    ts = min(512, s)
    steps = s // ts

    def body(dx_ref, y_ref, g_ref, w_ref, dy_ref, gw_ref, dg_ref):
        @pl.when(pl.program_id(0) == 0)
        def _():
            gw_ref[...] = jnp.zeros_like(gw_ref)

        dx = dx_ref[...]
        dy_ref[...] = lax.dot_general((dx * g_ref[...]).astype(BF16), w_ref[...], NT_DIMS, preferred_element_type=F32)
        gw_ref[...] += lax.dot_general(y_ref[...], dx.astype(BF16), TN_DIMS, preferred_element_type=F32)

        @pl.when(pl.program_id(0) == steps - 1)
        def _():
            m = gw_ref[...]
            dg_ref[...] = jnp.sum(w_ref[...].astype(F32) * m, axis=0, keepdims=True)
            gw_ref[...] = m * g_ref[...]

    row = pl.BlockSpec((ts, D_MODEL), lambda i: (i, 0))
    return _pallas(
        body, name=name, grid=(s // ts,), in_specs=[row, row, _full((1, D_MODEL)), _full((D_MODEL, D_MODEL))],
        out_specs=[row, _full((D_MODEL, D_MODEL)), _full((1, D_MODEL))],
        out_shape=[_sds((s, D_MODEL)), _sds((D_MODEL, D_MODEL)), _sds((1, D_MODEL))],
        operands=(dxo, y, gate, w_out), vmem_mib=48, jobs=jobs)


def _mix_bwd(dy, proj, probs, psink, attn, sv, qg2, kg2, wst, name, jobs=()):
    s = proj.shape[0]
    nt = s // TILE

    def body(dy_ref, cur_ref, pkv_ref, nkv_ref, p_ref, ps_ref, attn_ref, sv_ref, qg_ref, kg_ref, wst_ref,
             dpb_ref, dkv_ref, p0_ref, p2_ref, dqg_ref, dkg_ref, dsink_ref, dws_ref, dbsp_ref):
        def put(rows, col, value):
            dpb_ref[rows, col:col + BLOCK] = value.astype(BF16)

        @pl.when(pl.program_id(0) == 0)
        def _():
            dqg_ref[...] = jnp.zeros_like(dqg_ref)
            dkg_ref[...] = jnp.zeros_like(dkg_ref)
            dsink_ref[...] = jnp.zeros_like(dsink_ref)
            dws_ref[...] = jnp.zeros_like(dws_ref)
            dbsp_ref[...] = jnp.zeros_like(dbsp_ref)

        lo = _lo_mask((BLOCK, BLOCK))
        lo_kv = _lo_mask((TILE + 2 * BLOCK, BLOCK))
        eye = _diagonal()
        lane_row = lax.broadcasted_iota(jnp.int32, (1, BLOCK), 1)
        qg = qg_ref[...]
        kg = kg_ref[...]

        k_all, v_all = _kv_rows(cur_ref, pkv_ref, nkv_ref)
        rk = _half_rms_scale(k_all, lo_kv)
        khat = k_all * rk
        kn_all = (khat * kg).astype(BF16)
        vb_all = v_all.astype(BF16)

        dkn_parts, dv_parts = [], []
        dsink = jnp.zeros((1, BLOCK), F32)
        dqg = jnp.zeros((1, BLOCK), F32)
        for b in range(SUB):
            rows = slice(BLOCK * b, BLOCK * (b + 1))
            window = slice(BLOCK * b, BLOCK * (b + 3))
            kn, vb = kn_all[window], vb_all[window]

            qhat, rq = [], []
            for p in range(4):
                q = cur_ref[rows, C_Q + BLOCK * p:C_Q + BLOCK * (p + 1)]
                r = _half_rms_scale(q, lo)
                rq.append(r)
                qhat.append(q * r)
            q_stack = _stack_heads([(qh * qg) * 0.125 for qh in qhat], lo).astype(BF16)
            pb = p_ref[STACK * b:STACK * (b + 1), :]
            prob = pb.astype(F32)

            dout = []
            for p in range(4):
                g = cur_ref[rows, C_GA + BLOCK * p:C_GA + BLOCK * (p + 1)]
                sg = _sigmoid(g)
                dya = dy_ref[rows, BLOCK * p:BLOCK * (p + 1)]
                attn = attn_ref[rows, BLOCK * p:BLOCK * (p + 1)]
                put(rows, C_GA + BLOCK * p, dya * attn * (sg * (1.0 + g * (1.0 - sg))))
                dout.append(dya * (g * sg))
            do_stack = _stack_heads(dout, lo).astype(BF16)
            dp = lax.dot_general(do_stack, vb, NT_DIMS, preferred_element_type=F32)
            delta = jnp.sum(prob * dp, axis=-1, keepdims=True)
            dsb = (prob * (dp - delta)).astype(BF16)

            for h in range(N_HEADS):
                delta_row = _column_as_row(jnp.broadcast_to(delta[BLOCK * h:BLOCK * (h + 1)], (BLOCK, BLOCK)), eye)
                tot = jnp.sum(ps_ref[N_HEADS * b + h:N_HEADS * b + h + 1, :] * delta_row, axis=-1, keepdims=True)
                dsink = dsink - jnp.where(lane_row == h, tot, 0.0)

            dq_stack = jnp.dot(dsb, kn, preferred_element_type=F32) * 0.125
            dkn_parts.append(lax.dot_general(dsb, q_stack, TN_DIMS, preferred_element_type=F32))
            dv_parts.append(lax.dot_general(pb, do_stack, TN_DIMS, preferred_element_type=F32))

            for p in range(4):
                dqn = _unstack_pair(dq_stack, p, lo)
                qh = qhat[p]
                dqg = dqg + jnp.sum(dqn * qh, axis=0, keepdims=True)
                dqh = dqn * qg
                mean = _half_sum(dqh * qh, lo) * (1.0 / HEAD_DIM)
                put(rows, C_Q + BLOCK * p, rq[p] * (dqh - qh * mean))

        dws_new, dbs_new = [], []
        for p in range(4):
            rs, vnfs, vns, dsvs, dbs = [], [], [], [], None
            for b in range(SUB):
                rows = slice(BLOCK * b, BLOCK * (b + 1))
                vg = cur_ref[rows, C_VG + BLOCK * p:C_VG + BLOCK * (p + 1)]
                r = _half_rms_scale(vg, lo)
                vnf = vg * r
                sv = sv_ref[rows, BLOCK * p:BLOCK * (p + 1)]
                u = cur_ref[rows, C_U + BLOCK * p:C_U + BLOCK * (p + 1)]
                g = cur_ref[rows, C_GG + BLOCK * p:C_GG + BLOCK * (p + 1)]
                sg = _sigmoid(g)
                dym = dy_ref[rows, D_ATTN + BLOCK * p:D_ATTN + BLOCK * (p + 1)]
                put(rows, C_GG + BLOCK * p, dym * (u * sv) * (sg * (1.0 + g * (1.0 - sg))))
                dgm = dym * (g * sg)
                put(rows, C_U + BLOCK * p, dgm * sv)
                dsv = dgm * u
                term = jnp.where(lo, jnp.sum(jnp.where(lo, dsv, 0.0), axis=-1, keepdims=True),
                                 jnp.sum(jnp.where(lo, 0.0, dsv), axis=-1, keepdims=True))
                dbs = term if dbs is None else dbs + term
                rs.append(r)
                vnfs.append(vnf)
                vns.append(vnf.astype(BF16))
                dsvs.append(dsv)
            dbs_new.append(dbs)
            vn = jnp.concatenate(vns, axis=1)
            dsv = jnp.concatenate(dsvs, axis=1)
            lo_t = (lax.broadcasted_iota(jnp.int32, dsv.shape, 1) & (BLOCK - 1)) < HEAD_DIM
            dws_new.append(lax.dot_general(jnp.where(lo_t, dsv, 0.0).astype(BF16), vn, NT_DIMS, preferred_element_type=F32))
            dws_new.append(lax.dot_general(jnp.where(lo_t, 0.0, dsv).astype(BF16), vn, NT_DIMS, preferred_element_type=F32))
            dsvb = dsv.astype(BF16)
            dvn_a = jnp.dot(wst_ref[2 * p], dsvb, preferred_element_type=F32)
            dvn_b = jnp.dot(wst_ref[2 * p + 1], dsvb, preferred_element_type=F32)
            for b in range(SUB):
                lanes = slice(BLOCK * b, BLOCK * (b + 1))
                dvn = jnp.where(lo, dvn_a[:, lanes], dvn_b[:, lanes])
                mean = _half_sum(dvn * vnfs[b], lo) * (1.0 / HEAD_DIM)
                put(slice(BLOCK * b, BLOCK * (b + 1)), C_VG + BLOCK * p, rs[b] * (dvn - vnfs[b] * mean))

        dsink_ref[...] += dsink
        dqg = jnp.broadcast_to(dqg, (8, BLOCK))
        dqg_ref[...] += dqg + pltpu.roll(dqg, HEAD_DIM, 1)

        dkn = _overlap_add(dkn_parts)
        dv = _overlap_add(dv_parts)
        dkg = jnp.broadcast_to(jnp.sum(dkn * khat, axis=0, keepdims=True), (8, BLOCK))
        dkg_ref[...] += dkg + pltpu.roll(dkg, HEAD_DIM, 1)
        dkh = dkn * kg
        dk = rk * (dkh - khat * (_half_sum(dkh * khat, lo_kv) * (1.0 / HEAD_DIM)))
        dpb_ref[:, C_K:C_GA] = jnp.zeros((TILE, 2 * D_KV), BF16)
        dkv_ref[:, 0:D_KV] = dk[BLOCK:BLOCK + TILE]
        dkv_ref[:, D_KV:2 * D_KV] = dv[BLOCK:BLOCK + TILE]
        p0_ref[:, 0:D_KV] = dk[0:BLOCK]
        p0_ref[:, D_KV:2 * D_KV] = dv[0:BLOCK]
        p2_ref[:, 0:D_KV] = dk[BLOCK + TILE:]
        p2_ref[:, D_KV:2 * D_KV] = dv[BLOCK + TILE:]
        for g, new in enumerate(dws_new):
            dws_ref[g] += new
        for p, new in enumerate(dbs_new):
            dbsp_ref[p] += new

    cur, pkv, nkv, _ = _mix_specs(nt)
    kv_blk = (BLOCK, 2 * D_KV)
    half = pl.BlockSpec((TILE, D_ATTN), lambda i: (i, 0))
    return _pallas(
        body, name=name, grid=(nt,),
        in_specs=[pl.BlockSpec((TILE, D_MODEL), lambda i: (i, 0)), cur, pkv, nkv,
                  pl.BlockSpec((SUB * STACK, 3 * BLOCK), lambda i: (i, 0)), pl.BlockSpec((SUB * N_HEADS, BLOCK), lambda i: (i, 0)),
                  half, half, _full((1, BLOCK)), _full((1, BLOCK)), _full((8, BLOCK, BLOCK))],
        out_specs=[cur, pl.BlockSpec((TILE, 2 * D_KV), lambda i: (i, 0)),
                   pl.BlockSpec(kv_blk, lambda i: ((i + nt - 1) % nt, 0)),
                   pl.BlockSpec(kv_blk, lambda i: ((i + 1) % nt, 0)),
                   _full((8, BLOCK)), _full((8, BLOCK)), _full((1, BLOCK)),
                   _full((8, BLOCK, BLOCK)), _full((4, BLOCK, BLOCK))],
        out_shape=[_sds((s, D_IN), BF16), _sds((s, 2 * D_KV)), _sds((nt * BLOCK, 2 * D_KV)), _sds((nt * BLOCK, 2 * D_KV)),
                   _sds((8, BLOCK)), _sds((8, BLOCK)), _sds((1, BLOCK)),
                   _sds((8, BLOCK, BLOCK)), _sds((4, BLOCK, BLOCK))],
        operands=(dy, proj, proj, proj, probs, psink, attn, sv, qg2, kg2, wst), vmem_mib=56, jobs=jobs)


def _w_in_grad(dpb, dkv, p0, p2, x, ng, scale, shift, name, jobs=()):
    s = x.shape[0]
    ts = min(2 * TILE, s)
    tiles = ts // TILE

    def body(dpb_ref, dkv_ref, p0_ref, p2_ref, x_ref, ng_ref, sc_ref, sh_ref, gw_ref, dkvb_ref):
        @pl.when(pl.program_id(0) == 0)
        def _():
            gw_ref[...] = jnp.zeros_like(gw_ref)

        xv = x_ref[...]
        r = lax.rsqrt(jnp.mean(xv * xv, axis=-1, keepdims=True) + EPS)
        h = (((xv * r) * ng_ref[...]) * (1.0 + sc_ref[...]) + sh_ref[...]).astype(BF16)
        for t in range(tiles):
            halo = slice(BLOCK * t, BLOCK * (t + 1))
            first = slice(TILE * t, TILE * t + BLOCK)
            last = slice(TILE * (t + 1) - BLOCK, TILE * (t + 1))
            dkvb_ref[first, :] = (dkv_ref[first, :] + p2_ref[halo, :]).astype(BF16)
            if SUB > 2:
                inner = slice(TILE * t + BLOCK, TILE * (t + 1) - BLOCK)
                dkvb_ref[inner, :] = dkv_ref[inner, :].astype(BF16)
            dkvb_ref[last, :] = (dkv_ref[last, :] + p0_ref[halo, :]).astype(BF16)
        gw_ref[...] += lax.dot_general(dpb_ref[...], h, TN_DIMS, preferred_element_type=F32)
        gw_ref[C_K:C_GA, :] += lax.dot_general(dkvb_ref[...], h, TN_DIMS, preferred_element_type=F32)

    kv = pl.BlockSpec((ts, 2 * D_KV), lambda i: (i, 0))
    halo = pl.BlockSpec((tiles * BLOCK, 2 * D_KV), lambda i: (i, 0))
    vec = _full((1, D_MODEL))
    return _pallas(
        body, name=name, grid=(s // ts,),
        in_specs=[pl.BlockSpec((ts, D_IN), lambda i: (i, 0)), kv, halo, halo,
                  pl.BlockSpec((ts, D_MODEL), lambda i: (i, 0)), vec, vec, vec],
        out_specs=[_full((D_IN, D_MODEL)), kv], out_shape=[_sds((D_IN, D_MODEL)), _sds((s, 2 * D_KV), BF16)],
        operands=(dpb, dkv, p0, p2, x, ng, scale, shift), vmem_mib=56, jobs=jobs)


def _proj_bwd(dpb, dkvb, x, dxo, ng, scale, wt, name, jobs=()):
    s = x.shape[0]
    ts = min(512, s)

    def body(dpb_ref, dkvb_ref, x_ref, dxo_ref, ng_ref, sc_ref, w_ref, dxi_ref, dsh_ref, dsc_ref, dng_ref):
        @pl.when(pl.program_id(0) == 0)
        def _():
            dsh_ref[...] = jnp.zeros_like(dsh_ref)
            dsc_ref[...] = jnp.zeros_like(dsc_ref)
            dng_ref[...] = jnp.zeros_like(dng_ref)

        dh = (jnp.dot(dpb_ref[...], w_ref[...], preferred_element_type=F32)
              + jnp.dot(dkvb_ref[...], w_ref[C_K:C_GA, :], preferred_element_type=F32))

        xv = x_ref[...]
        r = lax.rsqrt(jnp.mean(xv * xv, axis=-1, keepdims=True) + EPS)
        xn = xv * r
        ngv = ng_ref[...]
        sc1 = 1.0 + sc_ref[...]
        dsh_ref[...] += jnp.sum(dh, axis=0, keepdims=True)
        dsc_ref[...] += jnp.sum(dh * (xn * ngv), axis=0, keepdims=True)
        dh1 = dh * sc1
        dng_ref[...] += jnp.sum(dh1 * xn, axis=0, keepdims=True)
        dxn = dh1 * ngv
        dxi_ref[...] = r * (dxn - xn * jnp.mean(dxn * xn, axis=-1, keepdims=True)) + dxo_ref[...]

    row = pl.BlockSpec((ts, D_MODEL), lambda i: (i, 0))
    vec = _full((1, D_MODEL))
    return _pallas(
        body, name=name, grid=(s // ts,),
        in_specs=[pl.BlockSpec((ts, D_IN), lambda i: (i, 0)), pl.BlockSpec((ts, 2 * D_KV), lambda i: (i, 0)),
                  row, row, vec, vec, _full((D_IN, D_MODEL))],
        out_specs=[row, vec, vec, vec],
        out_shape=[_sds((s, D_MODEL)), _sds((1, D_MODEL)), _sds((1, D_MODEL)), _sds((1, D_MODEL))],
        operands=(dpb, dkvb, x, dxo, ng, scale, wt), vmem_mib=48, jobs=jobs)


def _pair_sum(g, r, c_idx, name, send_dtype=None):
    _, rows, cols = g.shape
    half = rows // 2

    def body(c_ref, g_ref, r_ref, o_ref, *narrow):
        total = g_ref[...] + r_ref[...]
        o_ref[...] = total
        for n_ref in narrow:
            n_ref[...] = total.astype(n_ref.dtype)

    blk = (None, half, cols)
    out_blk = pl.BlockSpec(blk, lambda j, c: (j, 0, 0))
    shapes = [_sds((N_CHIPS, half, cols))] + ([_sds((N_CHIPS, half, cols), send_dtype)] if send_dtype else [])
    return pl.pallas_call(
        body, name=name,
        grid_spec=pltpu.PrefetchScalarGridSpec(
            num_scalar_prefetch=1, grid=(N_CHIPS,),
            in_specs=[pl.BlockSpec(blk, lambda j, c: (j, c[0], 0)), out_blk], out_specs=[out_blk] * len(shapes)),
        out_shape=shapes,
        compiler_params=pltpu.CompilerParams(dimension_semantics=("arbitrary",), vmem_limit_bytes=32 * MIB),
    )(*_in_hbm(c_idx, g, r))


def _chip_sum(p, r, place, name):
    _, rows, cols = p.shape
    tr = rows // 2

    def body(j_ref, p_ref, r_ref, o_ref):
        o_ref[...] = ((p_ref[...] + r_ref[0].astype(F32)) + r_ref[1].astype(F32)) + r_ref[2].astype(F32)

    return pl.pallas_call(
        body, name=name,
        grid_spec=pltpu.PrefetchScalarGridSpec(
            num_scalar_prefetch=1, grid=(2,),
            in_specs=[pl.BlockSpec((None, tr, cols), lambda t, j: (j[0], t, 0)),
                      pl.BlockSpec((3, tr, cols), lambda t, j: (0, t, 0))],
            out_specs=pl.BlockSpec((tr, cols), lambda t, j: (2 * j[1] + t, 0))),
        out_shape=_sds((2 * rows, cols)),
        compiler_params=pltpu.CompilerParams(dimension_semantics=("arbitrary",), vmem_limit_bytes=32 * MIB),
    )(*_in_hbm(place, p, r))


def _cast_permute_w_in(wt, place_chunks):
    def body(t_ref, w_ref, w0_ref, w1_ref):
        def cast_into(o_ref):
            for t in range(N_CHUNKS):
                src = pl.multiple_of(t_ref[1 + t] * CHUNK_ROWS, CHUNK_ROWS)
                o_ref[CHUNK_ROWS * t:CHUNK_ROWS * (t + 1), :] = w_ref[pl.ds(src, CHUNK_ROWS), :].astype(BF16)

        @pl.when(pl.program_id(0) == 0)
        def _():
            cast_into(w0_ref)

        @pl.when(pl.program_id(0) == 1)
        def _():
            cast_into(w1_ref)

    return pl.pallas_call(
        body, name="cast_permute_w_in",
        grid_spec=pltpu.PrefetchScalarGridSpec(
            num_scalar_prefetch=1, grid=(DEPTH,),
            in_specs=[pl.BlockSpec((None, W_IN_BLK, D_MODEL), lambda l, tbl: (l, 0, 0))],
            out_specs=[pl.BlockSpec((None, W_IN_BLK, D_MODEL), lambda l, tbl: (tbl[0], 0, 0)),
                       pl.BlockSpec((W_IN_BLK, D_MODEL), lambda l, tbl: (0, 0))]),
        out_shape=[_sds((N_CHIPS, W_IN_BLK, D_MODEL), BF16), _sds((W_IN_BLK, D_MODEL), BF16)],
        compiler_params=pltpu.CompilerParams(dimension_semantics=("arbitrary",), vmem_limit_bytes=32 * MIB),
    )(*_in_hbm(place_chunks, wt))


def _gather_inputs(c, w_out, w0):
    half = W_IN_BLK // 2

    def body(c_ref, wout_ref, mine_ref, call_ref, woutb_ref, w0_ref, send_sems, recv_sems):
        x, y, cc = _coords()
        j = 2 * x + y
        b = 2 * j + cc
        sib = (x, y, 1 - cc)
        woutb_ref[...] = wout_ref[...].astype(BF16)
        call_ref[b] = c_ref[...]
        chips = _other_chips(x, y)

        def sems(k):
            return send_sems.at[k], recv_sems.at[k]

        def half_rows(chip_index):
            return w0_ref.at[chip_index, pl.ds(cc * half, half), :]

        first = [_remote(mine_ref.at[j, pl.ds(cc * half, half), :], half_rows(j), sems(k), (*chip, cc))
                 for k, chip in enumerate(chips)]
        k = 3
        rest = []
        for fx in (0, 1):
            for fy in (0, 1):
                for fc in (0, 1):
                    if fx or fy or fc:
                        dev = (1 - x if fx else x, 1 - y if fy else y, 1 - cc if fc else cc)
                        rest.append(_remote(call_ref.at[b], call_ref.at[b], sems(k), dev))
                        k += 1
        for cp in first + rest:
            cp.start()
        passed = []
        for k, chip in enumerate(chips):
            jk = 2 * chip[0] + chip[1]
            first[k].wait_recv()
            passed.append(_remote(half_rows(jk), half_rows(jk), sems(10 + k), sib))
            passed[k].start()
        for cp in first:
            cp.wait_send()
        for cp in rest + passed:
            cp.wait()

    return pl.pallas_call(
        body, name="gather_inputs", in_specs=[VMEM, VMEM, ANY], out_specs=[VMEM, VMEM, ANY],
        out_shape=[_sds((N_DEV, 1, D_MODEL)), _sds((DEPTH, W_OUT_BLK, D_MODEL), BF16),
                   _sds((N_CHIPS, W_IN_BLK, D_MODEL), BF16)],
        scratch_shapes=[pltpu.SemaphoreType.DMA((13,)), pltpu.SemaphoreType.DMA((13,))],
        input_output_aliases={2: 2},
        compiler_params=pltpu.CompilerParams(vmem_limit_bytes=32 * MIB),
    )(c, w_out, w0)


def _ada_rows(c_all, w_ada, b_blk):
    def body(c_ref, w_ref, b_ref, o_ref, cond_ref):
        cv = c_ref[...]
        cond = (cv * _sigmoid(cv)).astype(BF16)
        cond_ref[...] = cond.astype(F32)
        for l in range(DEPTH):
            o_ref[:, l, :] = jnp.dot(cond, w_ref[l].astype(BF16), preferred_element_type=F32) + b_ref[l:l + 1, :]

    return pl.pallas_call(
        body, name="ada_rows", in_specs=[VMEM, VMEM, VMEM], out_specs=[VMEM, VMEM],
        out_shape=[_sds((N_DEV, DEPTH, W_ADA_BLK)), _sds((N_DEV, D_MODEL))],
        compiler_params=pltpu.CompilerParams(vmem_limit_bytes=32 * MIB),
    )(c_all, w_ada, b_blk)


def _exchange_ada(part):
    def body(part_ref, out_ref, send_sems, recv_sems):
        x, y, cc = _coords()
        j = 2 * x + y
        out_ref[j] = part_ref[2 * j + cc]
        copies = []
        for k, chip in enumerate(_other_chips(x, y)):
            b_dst = 4 * chip[0] + 2 * chip[1] + cc
            copies.append(_remote(part_ref.at[b_dst], out_ref.at[j], (send_sems.at[k], recv_sems.at[k]), (*chip, cc)))
        for cp in copies:
            cp.start()
        for cp in copies:
            cp.wait()

    return pl.pallas_call(
        body, name="exchange_ada", in_specs=[VMEM], out_specs=VMEM,
        out_shape=_sds((N_CHIPS, DEPTH, W_ADA_BLK)),
        scratch_shapes=[pltpu.SemaphoreType.DMA((3,)), pltpu.SemaphoreType.DMA((3,))],
    )(part)


def _adamw_math(w, g, m, v):
    m = ADAM_B1 * m + (1.0 - ADAM_B1) * g
    v = ADAM_B2 * v + (1.0 - ADAM_B2) * (g * g)
    m_hat = m / (1.0 - ADAM_B1 ** ADAM_STEP)
    v_hat = v / (1.0 - ADAM_B2 ** ADAM_STEP)
    delta = -ADAM_LR * (m_hat / (jnp.sqrt(v_hat) + ADAM_EPS) + ADAM_WD * w)
    return delta, m, v


def _adamw_w_in(w, g0, g1, m, v, pos_chunks):
    def body(t_ref, w_ref, g0_ref, g1_ref, m_ref, v_ref, g_ref, d_ref, nm_ref, nv_ref):
        for l, src in enumerate((g0_ref, g1_ref)):
            g = src[...]
            g_ref[l] = g
            d_ref[l], nm_ref[l], nv_ref[l] = _adamw_math(w_ref[l], g, m_ref[l], v_ref[l])

    nat = pl.BlockSpec((DEPTH, CHUNK_ROWS, D_MODEL), lambda t, tbl: (0, t, 0))
    per = pl.BlockSpec((CHUNK_ROWS, D_MODEL), lambda t, tbl: (tbl[t], 0))
    return pl.pallas_call(
        body, name="adamw_w_in",
        grid_spec=pltpu.PrefetchScalarGridSpec(num_scalar_prefetch=1, grid=(N_CHUNKS,),
                                               in_specs=[nat, per, per, nat, nat], out_specs=[nat] * 4),
        out_shape=[_sds(w.shape)] * 4,
        compiler_params=pltpu.CompilerParams(dimension_semantics=("arbitrary",)),
    )(*_in_hbm(pos_chunks, w, g0, g1, m, v))


def _adamw_w_out(w, g0, g1, m, v):
    def body(w_ref, g0_ref, g1_ref, m_ref, v_ref, g_ref, d_ref, nm_ref, nv_ref):
        g = jnp.where(pl.program_id(0) == 0, g0_ref[...], g1_ref[...])
        g_ref[...] = g
        d_ref[...], nm_ref[...], nv_ref[...] = _adamw_math(w_ref[...], g, m_ref[...], v_ref[...])

    blk = pl.BlockSpec((None, W_OUT_BLK, D_MODEL), lambda l: (l, 0, 0))
    gblk = _full((W_OUT_BLK, D_MODEL))
    return pl.pallas_call(
        body, name="adamw_w_out", grid=(DEPTH,), in_specs=[blk, gblk, gblk, blk, blk], out_specs=[blk] * 4,
        out_shape=[_sds(w.shape)] * 4,
        compiler_params=pltpu.CompilerParams(dimension_semantics=("arbitrary",), vmem_limit_bytes=32 * MIB),
    )(w, g0, g1, m, v)


def _w_ada_grad_adamw(cond_t, dada, w, m, v):
    _, rows, cols = w.shape
    tr = 256

    def body(ct_ref, da_ref, w_ref, m_ref, v_ref, g_ref, d_ref, nm_ref, nv_ref):
        g = jnp.dot(ct_ref[...], da_ref[...].astype(BF16), preferred_element_type=F32)
        g_ref[...] = g
        d_ref[...], nm_ref[...], nv_ref[...] = _adamw_math(w_ref[...], g, m_ref[...], v_ref[...])

    blk = pl.BlockSpec((None, tr, cols), lambda l, t: (l, t, 0))
    return pl.pallas_call(
        body, name="w_ada_grad_adamw", grid=(DEPTH, rows // tr),
        in_specs=[pl.BlockSpec((tr, BLOCK), lambda l, t: (t, 0)), pl.BlockSpec((None, BLOCK, cols), lambda l, t: (l, 0, 0)),
                  blk, blk, blk],
        out_specs=[blk] * 4, out_shape=[_sds(w.shape)] * 4,
        compiler_params=pltpu.CompilerParams(dimension_semantics=("arbitrary", "arbitrary"), vmem_limit_bytes=32 * MIB),
    )(cond_t, dada, w, m, v)


def _small_sum_adamw(gathered_a, gathered_b, w, m, v):
    def body(a_ref, b_ref, w_ref, m_ref, v_ref, g_ref, d_ref, nm_ref, nv_ref):
        def total(ref):
            g = ref[0]
            for b in range(1, N_DEV):
                g = g + ref[b]
            return g

        g = jnp.concatenate([total(a_ref), total(b_ref)], axis=0)
        g_ref[...] = g
        d_ref[...], nm_ref[...], nv_ref[...] = _adamw_math(w_ref[...], g, m_ref[...], v_ref[...])

    return pl.pallas_call(
        body, name="small_sum_adamw", in_specs=[VMEM] * 5, out_specs=[VMEM] * 4, out_shape=[_sds(w.shape)] * 4,
        compiler_params=pltpu.CompilerParams(vmem_limit_bytes=48 * MIB),
    )(gathered_a, gathered_b, w, m, v)


_SMALL_A = (("w_s", DEPTH * 8 * BLOCK), ("b_s", DEPTH * 8), ("q_gain", 1), ("k_gain", 1), ("sink", 1))
_SMALL_B = (("b_ada", DEPTH * 24), ("norm_gain", DEPTH * 8), ("sq_err", 1))


def _pack_rows(parts, layout):
    rows = []
    for name, n in layout:
        flat = parts[name].reshape(-1)
        rows.append(jnp.pad(flat, (0, n * 128 - flat.shape[0])).reshape(n, 128))
    n_rows = sum(n for _, n in layout)
    if n_rows % 8:
        rows.append(jnp.zeros((-n_rows % 8, 128), F32))
    return jnp.concatenate(rows, axis=0)


def _pack_small(parts):
    return jnp.concatenate([_pack_rows(parts, _SMALL_A), _pack_rows(parts, _SMALL_B)], axis=0)


def _unpack_small(packed, shapes):
    out, r0 = {}, 0
    for layout in (_SMALL_A, _SMALL_B):
        for name, n in layout:
            size = 1
            for d in shapes[name]:
                size *= d
            out[name] = packed[r0:r0 + n].reshape(-1)[:size].reshape(shapes[name])
            r0 += n
        r0 += -r0 % 8
    return out


def _permute_heads(a, axis):
    shp = a.shape
    a = a.reshape(shp[:axis] + (2, 4, HEAD_DIM) + shp[axis + 1:])
    a = jnp.swapaxes(a, axis, axis + 1)
    return a.reshape(shp)


def _unpermute_heads(a, axis):
    shp = a.shape
    a = a.reshape(shp[:axis] + (4, 2, HEAD_DIM) + shp[axis + 1:])
    a = jnp.swapaxes(a, axis, axis + 1)
    return a.reshape(shp)


def _permute_w_out(w):
    return jnp.concatenate([_permute_heads(w[:D_ATTN], 0), w[D_ATTN:]], axis=0)


def _unpermute_w_out(w):
    return jnp.concatenate([_unpermute_heads(w[:D_ATTN], 0), w[D_ATTN:]], axis=0)


def kernel(x, c, w_ada, b_ada, norm_gain, w_in, q_gain, k_gain, sink, w_s, b_s, w_out, loss_target, m_w_ada, m_b_ada, m_norm_gain, m_w_in, m_q_gain, m_k_gain, m_sink, m_w_s, m_b_s, m_w_out, v_w_ada, v_b_ada, v_norm_gain, v_w_in, v_q_gain, v_k_gain, v_sink, v_w_s, v_b_s, v_w_out):
    ix, iy, ic = _coords()
    chip = 2 * ix + iy
    chip_idx = jnp.stack([chip, ic]).astype(jnp.int32)
    core_idx = jnp.reshape(ic, (1,)).astype(jnp.int32)
    src_chunks = lax.dynamic_index_in_dim(jnp.asarray(_CHUNK_SRC), chip, 0, keepdims=False)
    pos_chunks = lax.dynamic_index_in_dim(jnp.asarray(_CHUNK_POS), chip, 0, keepdims=False)
    x0, target = x[0], loss_target[0]

    wt, mt, vt = (jnp.swapaxes(a, 1, 2) for a in (w_in, m_w_in, v_w_in))
    w0_mine, wloc_in1 = _cast_permute_w_in(wt, jnp.concatenate([chip_idx[:1], src_chunks]))
    c_all, wloc_out, w0 = _gather_inputs(c, w_out, w0_mine)
    wts = [w0.reshape(D_IN, D_MODEL), None]

    b_blk = lax.dynamic_slice_in_dim(b_ada, chip * W_ADA_BLK, W_ADA_BLK, axis=1)
    ada_part, cond = _ada_rows(c_all.reshape(N_DEV, D_MODEL), w_ada, b_blk)
    ada = jnp.moveaxis(_exchange_ada(ada_part), 0, 1).reshape(DEPTH, 3 * D_MODEL)
    shift = [ada[l:l + 1, 0:D_MODEL] for l in range(DEPTH)]
    scale = [ada[l:l + 1, D_MODEL:2 * D_MODEL] for l in range(DEPTH)]
    gate = [ada[l:l + 1, 2 * D_MODEL:] for l in range(DEPTH)]
    ng = [norm_gain[l:l + 1] for l in range(DEPTH)]

    qg2 = jnp.concatenate([q_gain, q_gain], axis=-1)
    kg2 = jnp.concatenate([k_gain, k_gain], axis=-1)
    ws_b = w_s.astype(BF16)
    wst_b = jnp.swapaxes(w_s, -1, -2).astype(BF16)
    bsp = jnp.repeat(jnp.swapaxes(b_s.reshape(DEPTH, 4, 2, BLOCK), -1, -2), HEAD_DIM, axis=-1)
    bias = jnp.asarray(_bias_table())

    def mix_args(l):
        return bias, sink[l], qg2[l:l + 1], kg2[l:l + 1], ws_b[l]

    w_out_shape = (W_OUT_BLK, D_MODEL)
    proj0, wo0, wo1 = _proj_fwd(x0, ng[0], scale[0], shift[0], wts[0], "proj_fwd_0",
                                jobs=[_job_gather([(wloc_out, 0), (wloc_out, 1)], [w_out_shape, w_out_shape])])
    y0, *kept0, w1 = _mix_fwd(proj0, *mix_args(0), bsp[0], "mix_fwd_0",
                              jobs=[_job_gather([(wloc_in1, None)], [(W_IN_BLK, D_MODEL)])])
    wts[1] = w1.reshape(D_IN, D_MODEL)
    wos = [_permute_w_out(w.reshape(D_MODEL, D_MODEL)) for w in (wo0, wo1)]
    x1, proj1 = _out_proj_fwd(y0, x0, gate[0], wos[0], ng[1], scale[1], shift[1], wts[1], "out_proj_fwd_01")
    y1, *kept1 = _mix_fwd(proj1, *mix_args(1), bsp[1], "mix_fwd_1")

    def blocks_out(gw):
        return _unpermute_w_out(gw).reshape(N_CHIPS, W_OUT_BLK, D_MODEL)

    dx2, sq, dy1, gwo1, dgate1 = _out_loss_bwd(y1, x1, gate[1], wos[1], target, "out_loss_bwd_1")
    go1 = blocks_out(gwo1)
    dpb, dkv, p0, p2, dqg1, dkg1, dsink1, dws1, dbsp1, ro1 = _mix_bwd(
        dy1, proj1, *kept1, qg2[1:2], kg2[1:2], wst_b[1], "mix_bwd_1", jobs=[_job_swap(go1)])
    po1, = _pair_sum(go1, ro1, core_idx, "pair_sum_w_out_1")
    gwi1, dkvb, co1 = _w_in_grad(dpb, dkv, p0, p2, x1, ng[1], scale[1], shift[1], "w_in_grad_1", jobs=[_job_scatter(po1)])
    gi1 = gwi1.reshape(N_CHIPS, W_IN_BLK, D_MODEL)
    fo1 = _chip_sum(po1, co1, chip_idx, "chip_sum_w_out_1")
    dx1, dsh1, dsc1, dng1, grad_wo1, ri1 = _proj_bwd(dpb, dkvb, x1, dx2, ng[1], scale[1], wts[1], "proj_bwd_1",
                                                     jobs=[_job_join(fo1), _job_swap(gi1)])
    pi1, = _pair_sum(gi1, ri1, core_idx, "pair_sum_w_in_1")

    dy0, gwo0, dgate0 = _out_bwd(dx1, y0, gate[0], wos[0], "out_bwd_0")
    go0 = blocks_out(gwo0)
    dpb, dkv, p0, p2, dqg0, dkg0, dsink0, dws0, dbsp0, ci1, ro0 = _mix_bwd(
        dy0, proj0, *kept0, qg2[0:1], kg2[0:1], wst_b[0], "mix_bwd_0", jobs=[_job_scatter(pi1), _job_swap(go0)])
    fi1 = _chip_sum(pi1, ci1, chip_idx, "chip_sum_w_in_1")
    po0, = _pair_sum(go0, ro0, core_idx, "pair_sum_w_out_0")

    def bs_grad(dbsp):
        return jnp.swapaxes(dbsp[:, :, ::HEAD_DIM], -1, -2).reshape(8, BLOCK)

    small_g = dict(
        w_s=jnp.stack([dws0, dws1]), b_s=jnp.stack([bs_grad(dbsp0), bs_grad(dbsp1)]),
        q_gain=jnp.stack([dqg0[0, :HEAD_DIM], dqg1[0, :HEAD_DIM]]), k_gain=jnp.stack([dkg0[0, :HEAD_DIM], dkg1[0, :HEAD_DIM]]),
        sink=jnp.stack([dsink0[0, :N_HEADS], dsink1[0, :N_HEADS]]))
    gwi0, dkvb, grad_wi1, co0, gathered_a = _w_in_grad(
        dpb, dkv, p0, p2, x0, ng[0], scale[0], shift[0], "w_in_grad_0",
        jobs=[_job_join(fi1), _job_scatter(po0), _job_all_gather(_pack_rows(small_g, _SMALL_A))])
    gi0 = gwi0.reshape(N_CHIPS, W_IN_BLK, D_MODEL)
    fo0 = _chip_sum(po0, co0, chip_idx, "chip_sum_w_out_0")

    ri0, grad_wo0 = _comm([_job_swap(gi0), _job_join(fo0)], "swap_w_in_0")
    pi0, pi0_send = _pair_sum(gi0, ri0, core_idx, "pair_sum_w_in_0", send_dtype=BF16)
    dx0, dsh0, dsc0, dng0, ci0 = _proj_bwd(dpb, dkvb, x0, dx1, ng[0], scale[0], wts[0], "proj_bwd_0",
                                           jobs=[_job_scatter(pi0_send)])
    fi0 = _chip_sum(pi0, ci0, chip_idx, "chip_sum_w_in_0")

    small_g.update(
        b_ada=jnp.stack([jnp.concatenate([dsh0, dsc0, dgate0], axis=-1)[0], jnp.concatenate([dsh1, dsc1, dgate1], axis=-1)[0]]),
        norm_gain=jnp.stack([dng0[0], dng1[0]]), sq_err=sq[0])
    none = jnp.zeros((1,), F32)
    small_w = dict(w_s=w_s, b_s=b_s, b_ada=b_ada, norm_gain=norm_gain, q_gain=q_gain, k_gain=k_gain, sink=sink, sq_err=none)
    small_m = dict(w_s=m_w_s, b_s=m_b_s, b_ada=m_b_ada, norm_gain=m_norm_gain, q_gain=m_q_gain, k_gain=m_k_gain, sink=m_sink,
                   sq_err=none)
    small_v = dict(w_s=v_w_s, b_s=v_b_s, b_ada=v_b_ada, norm_gain=v_norm_gain, q_gain=v_q_gain, k_gain=v_k_gain, sink=v_sink,
                   sq_err=none)
    grad_wi0, gathered_b = _comm([_job_join(fi0), _job_all_gather(_pack_rows(small_g, _SMALL_B))], "join_w_in_0")
    packed = _small_sum_adamw(gathered_a.reshape(N_DEV, -1, 128), gathered_b.reshape(N_DEV, -1, 128),
                              _pack_small(small_w), _pack_small(small_m), _pack_small(small_v))
    shapes = {k: a.shape for k, a in small_w.items()}
    sg, sd, sm, sv = (_unpack_small(p, shapes) for p in packed)
    loss = 0.5 * sg["sq_err"][0]

    dada_all = gathered_b.reshape(N_DEV, -1, 128)[:, 0:DEPTH * 24].reshape(N_DEV, DEPTH, 3 * D_MODEL)
    dada_blk = jnp.moveaxis(lax.dynamic_slice_in_dim(dada_all, chip * W_ADA_BLK, W_ADA_BLK, axis=2), 0, 1)
    pad = BLOCK - N_DEV
    ada_out = _w_ada_grad_adamw(
        jnp.pad(cond.T, ((0, 0), (0, pad))).astype(BF16), jnp.pad(dada_blk, ((0, 0), (0, pad), (0, 0))),
        w_ada, m_w_ada, v_w_ada)

    in_out = [jnp.swapaxes(a, 1, 2) for a in _adamw_w_in(wt, grad_wi0, grad_wi1, mt, vt, pos_chunks)]
    out_out = _adamw_w_out(w_out, grad_wo0, grad_wo1, m_w_out, v_w_out)

    def ordered(k):
        small = (sg, sd, sm, sv)[k]
        return (ada_out[k], small["b_ada"], small["norm_gain"], in_out[k], small["q_gain"], small["k_gain"], small["sink"],
                small["w_s"], small["b_s"], out_out[k])

    return (loss, dx0[None], *ordered(0), *ordered(1), *ordered(2), *ordered(3))
```

```python
import numpy as np

import jax
import jax.numpy as jnp
from jax import lax
from jax.experimental import pallas as pl
from jax.experimental.pallas import tpu as pltpu

F32 = jnp.float32
BF16 = jnp.bfloat16

D_MODEL = 1024
DEPTH = 2
HEAD_DIM = 64
N_HEADS = 8
BLOCK = 128
SUB = 4
TILE = SUB * BLOCK
STACK = 8 * BLOCK
D_ATTN = 512
D_KV = 128
D_IN = 2816
N_CHIPS = 4
N_DEV = 8
W_IN_BLK = D_IN // N_CHIPS
W_OUT_BLK = D_MODEL // N_CHIPS
W_ADA_BLK = 3 * D_MODEL // N_CHIPS
CHUNK_ROWS = HEAD_DIM
N_CHUNKS = W_IN_BLK // CHUNK_ROWS
EPS = 1e-6
NEG_INF = -1e30

C_Q, C_K, C_V, C_GA, C_U, C_VG, C_GG = 0, 512, 640, 768, 1280, 1792, 2304

ADAM_LR = 0.001
ADAM_B1 = 0.9
ADAM_B2 = 0.999
ADAM_EPS = 1e-08
ADAM_WD = 0.01
ADAM_STEP = 10

MESH = pl.DeviceIdType.MESH
MIB = 1024 * 1024
ANY = pl.BlockSpec(memory_space=pl.ANY)
VMEM = pl.BlockSpec(memory_space=pltpu.VMEM)

NT_DIMS = (((1,), (1,)), ((), ()))
TN_DIMS = (((0,), (0,)), ((), ()))

_PAIR_ORDER = (0, 4, 1, 5, 2, 6, 3, 7)
_CHUNK_SRC = np.array([
    list(_PAIR_ORDER) + [8, 9, 10],
    [0] + [1 + h for h in _PAIR_ORDER] + [9, 10],
    list(range(N_CHUNKS)),
    list(range(N_CHUNKS)),
], np.int32)
_CHUNK_POS = np.argsort(_CHUNK_SRC, axis=1).astype(np.int32)


def _bias_table():
    i = np.arange(N_HEADS * BLOCK)[:, None]
    j = np.arange(3 * BLOCK)[None, :]
    dist = np.abs(j - BLOCK - (i % BLOCK))
    slope = 2.0 ** -(i // BLOCK + 1.0)
    inner = np.where(dist <= BLOCK, -(slope * dist), NEG_INF)
    first = np.where(j >= BLOCK, inner, NEG_INF)
    last = np.where(j < 2 * BLOCK, inner, NEG_INF)
    return np.stack([first, inner, last]).astype(np.float32)


def _full(shape):
    n = len(shape)
    return pl.BlockSpec(shape, lambda *_: (0,) * n)


def _sds(shape, dtype=F32):
    return jax.ShapeDtypeStruct(shape, dtype)


def _coords():
    return lax.axis_index("x"), lax.axis_index("y"), lax.axis_index("c")


def _other_chips(x, y):
    return [(1 - x, y), (x, 1 - y), (1 - x, 1 - y)]


def _in_hbm(*operands):
    return [pltpu.with_memory_space_constraint(a, pltpu.HBM) if a.size * a.dtype.itemsize >= MIB // 4 else a
            for a in operands]


def _remote(src, dst, sems, dev):
    return pltpu.make_async_remote_copy(src_ref=src, dst_ref=dst, send_sem=sems[0], recv_sem=sems[1],
                                        device_id=dev, device_id_type=MESH)


class _Job:
    def __init__(self, inputs, out_shapes, n_remote, n_local, make, then=None, in_place=False):
        self.inputs, self.out_shapes, self.n_remote, self.n_local, self.make = inputs, out_shapes, n_remote, n_local, make
        self.then = then
        self.in_place = in_place


def _job_aliases(jobs, in_base, out_base):
    aliases, a, b = {}, 0, 0
    for j in jobs:
        if j.in_place:
            aliases.update({in_base + a + k: out_base + b + k for k in range(len(j.inputs))})
        a, b = a + len(j.inputs), b + len(j.out_shapes)
    return aliases


def _job_copies(jobs, jin, jout, sems, second=False):
    send, recv, loc = sems
    res, a, b, r, l = [], 0, 0, 0, 0
    for j in jobs:
        build = j.then if second else j.make
        if build is not None:
            res += build(jin[a:a + len(j.inputs)], jout[b:b + len(j.out_shapes)],
                         lambda k, r=r: (send.at[r + k], recv.at[r + k]), lambda k, l=l: loc.at[l + k])
        a, b, r, l = a + len(j.inputs), b + len(j.out_shapes), r + j.n_remote, l + j.n_local
    return res


def _run(copies):
    for cp in copies:
        cp.start()
    for cp in copies:
        cp.wait()


def _job_gather(sources, shapes):
    n = len(sources)

    def make(ins, outs, rsem, lsem):
        x, y, c = _coords()
        j = 2 * x + y
        res = []
        for t, ((_, layer), src, dst) in enumerate(zip(sources, ins, outs)):
            src = src if layer is None else src.at[layer]
            res.append(pltpu.make_async_copy(src, dst.at[j], lsem(t)))
            for k, chip in enumerate(_other_chips(x, y)):
                res.append(_remote(src, dst.at[j], rsem(3 * t + k), (*chip, c)))
        return res

    return _Job([a for a, _ in sources], [_sds((N_CHIPS,) + s, BF16) for s in shapes], 3 * n, n, make)


def _job_swap(g):
    _, rows, cols = g.shape
    half = rows // 2

    def make(ins, outs, rsem, lsem):
        x, y, c = _coords()
        return [_remote(ins[0].at[:, pl.ds((1 - c) * half, half), :], outs[0], rsem(0), (x, y, 1 - c))]

    return _Job([g], [_sds((N_CHIPS, half, cols))], 1, 0, make)


def _job_scatter(p):
    def make(ins, outs, rsem, lsem):
        x, y, c = _coords()
        return [_remote(ins[0].at[2 * chip[0] + chip[1]], outs[0].at[k], rsem(k), (*chip, c))
                for k, chip in enumerate(_other_chips(x, y))]

    return _Job([p], [_sds((3,) + p.shape[1:], p.dtype)], 3, 0, make)


def _job_all_gather(blk):
    m_per = blk.shape[0]

    def rows(ref, px, py, pc):
        return ref.at[pl.ds((4 * px + 2 * py + pc) * m_per, m_per), :]

    def make(ins, outs, rsem, lsem):
        x, y, c = _coords()
        res = [pltpu.make_async_copy(ins[0], rows(outs[0], x, y, c), lsem(0)),
               _remote(ins[0], rows(outs[0], x, y, c), rsem(0), (x, y, 1 - c))]
        res += [_remote(ins[0], rows(outs[0], x, y, c), rsem(1 + k), (*chip, c)) for k, chip in enumerate(_other_chips(x, y))]
        return res

    def then(ins, outs, rsem, lsem):
        x, y, c = _coords()
        return [_remote(rows(outs[0], *chip, c), rows(outs[0], *chip, c), rsem(4 + k), (x, y, 1 - c))
                for k, chip in enumerate(_other_chips(x, y))]

    return _Job([blk], [_sds((N_DEV * m_per, blk.shape[1]), blk.dtype)], 7, 1, make, then)


def _job_join(f):
    half = f.shape[0] // 2

    def make(ins, outs, rsem, lsem):
        x, y, c = _coords()
        mine = pl.ds(c * half, half)
        return [_remote(ins[0].at[mine, :], outs[0].at[mine, :], rsem(0), (x, y, 1 - c))]

    return _Job([f], [_sds(f.shape, f.dtype)], 1, 0, make, in_place=True)


def _pallas(body, *, name, grid, in_specs, out_specs, out_shape, operands, vmem_mib, jobs=(), scratch=()):
    in_specs, out_specs, out_shape = list(in_specs), list(out_specs), list(out_shape)
    n_in, n_out = len(in_specs), len(out_specs)
    j_in = [a for j in jobs for a in j.inputs]
    j_out = [s for j in jobs for s in j.out_shapes]
    n_rem = max(1, sum(j.n_remote for j in jobs))
    n_loc = max(1, sum(j.n_local for j in jobs))
    sems = [pltpu.SemaphoreType.DMA((n_rem,)), pltpu.SemaphoreType.DMA((n_rem,)),
            pltpu.SemaphoreType.DMA((n_loc,))] if jobs else []
    scratch = list(scratch) + sems

    def wrapped(*refs):
        ins = refs[:n_in]
        jin = refs[n_in:n_in + len(j_in)]
        outs = refs[n_in + len(j_in):n_in + len(j_in) + n_out]
        jout = refs[n_in + len(j_in) + n_out:n_in + len(j_in) + n_out + len(j_out)]
        own = refs[n_in + len(j_in) + n_out + len(j_out):len(refs) - len(sems)]

        if jobs:
            first = last = None
            for d, n in enumerate(grid):
                f, e = pl.program_id(d) == 0, pl.program_id(d) == n - 1
                first, last = (f, e) if first is None else (first & f, last & e)

            @pl.when(first)
            def _():
                for cp in _job_copies(jobs, jin, jout, refs[-3:]):
                    cp.start()

        body(*ins, *outs, *own)

        if jobs:
            @pl.when(last)
            def _():
                for cp in _job_copies(jobs, jin, jout, refs[-3:]):
                    cp.wait()
                _run(_job_copies(jobs, jin, jout, refs[-3:], second=True))

    return pl.pallas_call(
        wrapped, name=name, grid=grid,
        in_specs=in_specs + [ANY] * len(j_in), out_specs=out_specs + [ANY] * len(j_out),
        out_shape=out_shape + j_out, scratch_shapes=scratch, input_output_aliases=_job_aliases(jobs, n_in, n_out),
        compiler_params=pltpu.CompilerParams(dimension_semantics=("arbitrary",) * len(grid),
                                             vmem_limit_bytes=vmem_mib * MIB),
    )(*_in_hbm(*operands, *j_in))


def _comm(jobs, name):
    j_in = [a for j in jobs for a in j.inputs]
    j_out = [s for j in jobs for s in j.out_shapes]
    n_rem = max(1, sum(j.n_remote for j in jobs))
    n_loc = max(1, sum(j.n_local for j in jobs))

    def body(*refs):
        jin, jout = refs[:len(j_in)], refs[len(j_in):len(j_in) + len(j_out)]
        _run(_job_copies(jobs, jin, jout, refs[-3:]))
        _run(_job_copies(jobs, jin, jout, refs[-3:], second=True))

    return pl.pallas_call(
        body, name=name, in_specs=[ANY] * len(j_in), out_specs=[ANY] * len(j_out), out_shape=j_out,
        scratch_shapes=[pltpu.SemaphoreType.DMA((n_rem,)), pltpu.SemaphoreType.DMA((n_rem,)),
                        pltpu.SemaphoreType.DMA((n_loc,))],
        input_output_aliases=_job_aliases(jobs, 0, 0),
    )(*_in_hbm(*j_in))


def _sigmoid(x):
    return 1.0 / (1.0 + jnp.exp(-x))


def _lo_mask(shape):
    return lax.broadcasted_iota(jnp.int32, shape, len(shape) - 1) < HEAD_DIM


def _half_sum(x, lo):
    a = jnp.sum(jnp.where(lo, x, 0.0), axis=-1, keepdims=True)
    b = jnp.sum(jnp.where(lo, 0.0, x), axis=-1, keepdims=True)
    return jnp.where(lo, a, b)


def _half_rms_scale(x, lo):
    return lax.rsqrt(_half_sum(x * x, lo) * (1.0 / HEAD_DIM) + EPS)


def _stack_heads(pairs, lo):
    return jnp.concatenate([jnp.where(lo, t, 0.0) for t in pairs] + [jnp.where(lo, 0.0, t) for t in pairs], axis=0)


def _unstack_pair(stack, p, lo):
    return jnp.where(lo, stack[BLOCK * p:BLOCK * (p + 1)], stack[BLOCK * (4 + p):BLOCK * (5 + p)])


def _attention_probs(q_stack, kn, bias_ref, sink_ref):
    rows = N_HEADS * BLOCK
    s = lax.dot_general(q_stack, kn, NT_DIMS, preferred_element_type=F32) + bias_ref[...]
    sink = jnp.concatenate([jnp.full((BLOCK, BLOCK), sink_ref[h], F32) for h in range(N_HEADS)], axis=0)
    cols = [s[:, BLOCK * j:BLOCK * (j + 1)] for j in range(3)]
    top = jnp.max(jnp.maximum(jnp.maximum(cols[0], cols[1]), cols[2]), axis=-1, keepdims=True)
    m = jnp.maximum(jnp.broadcast_to(top, (rows, BLOCK)), sink)
    e = [jnp.exp(c - m) for c in cols]
    es = jnp.exp(sink - m)
    inv = 1.0 / (jnp.broadcast_to(jnp.sum((e[0] + e[1]) + e[2], axis=-1, keepdims=True), (rows, BLOCK)) + es)
    return jnp.concatenate([c * inv for c in e], axis=1), es * inv


def _kv_rows(cur_ref, pkv_ref, nkv_ref):
    k = jnp.concatenate([pkv_ref[:, 0:D_KV], cur_ref[:, C_K:C_K + D_KV], nkv_ref[:, 0:D_KV]], axis=0)
    v = jnp.concatenate([pkv_ref[:, D_KV:2 * D_KV], cur_ref[:, C_V:C_V + D_KV], nkv_ref[:, D_KV:2 * D_KV]], axis=0)
    return k, v


def _overlap_add(parts):
    blocks = []
    for j in range(SUB + 2):
        terms = [parts[b][BLOCK * (j - b):BLOCK * (j - b + 1)] for b in range(SUB) if 0 <= j - b <= 2]
        total = terms[0]
        for t in terms[1:]:
            total = total + t
        blocks.append(total)
    return jnp.concatenate(blocks, axis=0)


def _mix_specs(nt):
    cur = pl.BlockSpec((TILE, D_IN), lambda i: (i, 0))
    kv_col = C_K // (2 * D_KV)
    pkv = pl.BlockSpec((BLOCK, 2 * D_KV), lambda i: (jnp.maximum(i * SUB - 1, 0), kv_col))
    nkv = pl.BlockSpec((BLOCK, 2 * D_KV), lambda i: (jnp.minimum((i + 1) * SUB, nt * SUB - 1), kv_col))
    table = (None, N_HEADS * BLOCK, 3 * BLOCK)
    first = pl.BlockSpec(table, lambda i: (jnp.where(i == 0, 0, 1), 0, 0))
    inner = pl.BlockSpec(table, lambda i: (1, 0, 0))
    last = pl.BlockSpec(table, lambda i: (jnp.where(i == nt - 1, 2, 1), 0, 0))
    return cur, pkv, nkv, [first] + [inner] * (SUB - 2) + [last]


def _proj_fwd(x, ng, scale, shift, wt, name, jobs=()):
    s = x.shape[0]
    ts = min(512, s)

    def body(x_ref, ng_ref, sc_ref, sh_ref, w_ref, o_ref):
        xv = x_ref[...]
        r = lax.rsqrt(jnp.mean(xv * xv, axis=-1, keepdims=True) + EPS)
        h = ((xv * r) * ng_ref[...]) * (1.0 + sc_ref[...]) + sh_ref[...]
        o_ref[...] = lax.dot_general(h.astype(BF16), w_ref[...], NT_DIMS, preferred_element_type=F32)

    vec = _full((1, D_MODEL))
    return _pallas(
        body, name=name, grid=(s // ts,),
        in_specs=[pl.BlockSpec((ts, D_MODEL), lambda i: (i, 0)), vec, vec, vec, _full((D_IN, D_MODEL))],
        out_specs=[pl.BlockSpec((ts, D_IN), lambda i: (i, 0))], out_shape=[_sds((s, D_IN))],
        operands=(x, ng, scale, shift, wt), vmem_mib=48, jobs=jobs)


def _diagonal():
    return lax.broadcasted_iota(jnp.int32, (BLOCK, BLOCK), 0) == lax.broadcasted_iota(jnp.int32, (BLOCK, BLOCK), 1)


def _column_as_row(wide, eye):
    return jnp.sum(jnp.where(eye, wide, 0.0), axis=0, keepdims=True)


def _mix_fwd(proj, bias, sink, qg2, kg2, ws, bsp, name, jobs=()):
    s = proj.shape[0]
    nt = s // TILE

    def body(sink_ref, cur_ref, pkv_ref, nkv_ref, *rest):
        bias_refs = rest[:SUB]
        qg_ref, kg_ref, ws_ref, bsp_ref, y_ref, p_ref, ps_ref, attn_ref, sv_ref = rest[SUB:]
        lo = _lo_mask((BLOCK, BLOCK))
        eye = _diagonal()
        lo_kv = _lo_mask((TILE + 2 * BLOCK, BLOCK))
        k_all, v_all = _kv_rows(cur_ref, pkv_ref, nkv_ref)
        kn_all = ((k_all * _half_rms_scale(k_all, lo_kv)) * kg_ref[...]).astype(BF16)
        vb_all = v_all.astype(BF16)
        for b in range(SUB):
            rows = slice(BLOCK * b, BLOCK * (b + 1))
            window = slice(BLOCK * b, BLOCK * (b + 3))
            qn = []
            for p in range(4):
                q = cur_ref[rows, C_Q + BLOCK * p:C_Q + BLOCK * (p + 1)]
                qn.append(((q * _half_rms_scale(q, lo)) * qg_ref[...]) * 0.125)
            q_stack = _stack_heads(qn, lo).astype(BF16)
            prob, psink = _attention_probs(q_stack, kn_all[window], bias_refs[b], sink_ref)
            pb = prob.astype(BF16)
            p_ref[STACK * b:STACK * (b + 1), :] = pb
            ps_ref[N_HEADS * b:N_HEADS * (b + 1), :] = jnp.concatenate(
                [_column_as_row(psink[BLOCK * h:BLOCK * (h + 1)], eye) for h in range(N_HEADS)], axis=0)
            o_stack = jnp.dot(pb, vb_all[window], preferred_element_type=F32)
            for p in range(4):
                g = cur_ref[rows, C_GA + BLOCK * p:C_GA + BLOCK * (p + 1)]
                attn = _unstack_pair(o_stack, p, lo)
                attn_ref[rows, BLOCK * p:BLOCK * (p + 1)] = attn.astype(BF16)
                y_ref[rows, BLOCK * p:BLOCK * (p + 1)] = (attn * (g * _sigmoid(g))).astype(BF16)

        for p in range(4):
            cols = slice(C_VG + BLOCK * p, C_VG + BLOCK * (p + 1))
            vn = []
            for b in range(SUB):
                vg = cur_ref[BLOCK * b:BLOCK * (b + 1), cols]
                vn.append((vg * _half_rms_scale(vg, lo)).astype(BF16))
            vn = jnp.concatenate(vn, axis=1)
            sv_a = jnp.dot(ws_ref[2 * p], vn, preferred_element_type=F32)
            sv_b = jnp.dot(ws_ref[2 * p + 1], vn, preferred_element_type=F32)
            for b in range(SUB):
                rows = slice(BLOCK * b, BLOCK * (b + 1))
                lanes = slice(BLOCK * b, BLOCK * (b + 1))
                sv = jnp.where(lo, sv_a[:, lanes], sv_b[:, lanes]) + bsp_ref[p]
                sv_ref[rows, BLOCK * p:BLOCK * (p + 1)] = sv.astype(BF16)
                u = cur_ref[rows, C_U + BLOCK * p:C_U + BLOCK * (p + 1)]
                g = cur_ref[rows, C_GG + BLOCK * p:C_GG + BLOCK * (p + 1)]
                y_ref[rows, D_ATTN + BLOCK * p:D_ATTN + BLOCK * (p + 1)] = ((u * sv) * (g * _sigmoid(g))).astype(BF16)

    cur, pkv, nkv, bias_specs = _mix_specs(nt)
    nb = nt * SUB
    half = pl.BlockSpec((TILE, D_ATTN), lambda i: (i, 0))
    return _pallas(
        body, name=name, grid=(nt,),
        in_specs=[pl.BlockSpec(memory_space=pltpu.SMEM), cur, pkv, nkv, *bias_specs, _full((1, BLOCK)), _full((1, BLOCK)),
                  _full((8, BLOCK, BLOCK)), _full((4, BLOCK, BLOCK))],
        out_specs=[pl.BlockSpec((TILE, D_MODEL), lambda i: (i, 0)), pl.BlockSpec((SUB * STACK, 3 * BLOCK), lambda i: (i, 0)),
                   pl.BlockSpec((SUB * N_HEADS, BLOCK), lambda i: (i, 0)), half, half],
        out_shape=[_sds((s, D_MODEL), BF16), _sds((nb * STACK, 3 * BLOCK), BF16), _sds((nb * N_HEADS, BLOCK)),
                   _sds((s, D_ATTN), BF16), _sds((s, D_ATTN), BF16)],
        operands=(sink, proj, proj, proj, *([bias] * SUB), qg2, kg2, ws, bsp), vmem_mib=56, jobs=jobs)


def _out_proj_fwd(y, x, gate, w_out, ng, scale, shift, wt, name):
    s = x.shape[0]
    ts = min(512, s)

    def body(y_ref, x_ref, g_ref, w_ref, ng_ref, sc_ref, sh_ref, wt_ref, xn_ref, p_ref):
        xv = x_ref[...] + g_ref[...] * jnp.dot(y_ref[...], w_ref[...], preferred_element_type=F32)
        xn_ref[...] = xv
        r = lax.rsqrt(jnp.mean(xv * xv, axis=-1, keepdims=True) + EPS)
        h = ((xv * r) * ng_ref[...]) * (1.0 + sc_ref[...]) + sh_ref[...]
        p_ref[...] = lax.dot_general(h.astype(BF16), wt_ref[...], NT_DIMS, preferred_element_type=F32)

    row = pl.BlockSpec((ts, D_MODEL), lambda i: (i, 0))
    vec = _full((1, D_MODEL))
    return _pallas(
        body, name=name, grid=(s // ts,),
        in_specs=[row, row, vec, _full((D_MODEL, D_MODEL)), vec, vec, vec, _full((D_IN, D_MODEL))],
        out_specs=[row, pl.BlockSpec((ts, D_IN), lambda i: (i, 0))], out_shape=[_sds((s, D_MODEL)), _sds((s, D_IN))],
        operands=(y, x, gate, w_out, ng, scale, shift, wt), vmem_mib=56)


def _out_loss_bwd(y, x, gate, w_out, target, name):
    s = x.shape[0]
    ts = min(512, s)
    steps = s // ts

    def body(y_ref, x_ref, g_ref, w_ref, t_ref, dx_ref, sq_ref, dy_ref, gw_ref, dg_ref):
        @pl.when(pl.program_id(0) == 0)
        def _():
            sq_ref[...] = jnp.zeros_like(sq_ref)
            gw_ref[...] = jnp.zeros_like(gw_ref)

        yv = y_ref[...]
        out = x_ref[...] + g_ref[...] * jnp.dot(yv, w_ref[...], preferred_element_type=F32)
        diff = out - t_ref[...]
        dx = diff * (1.0 / D_MODEL)
        dx_ref[...] = dx
        per_token = jnp.sum(diff * diff, axis=-1, keepdims=True) * (1.0 / D_MODEL)
        sq_ref[...] += jnp.sum(per_token, axis=0, keepdims=True)
        dy_ref[...] = lax.dot_general((dx * g_ref[...]).astype(BF16), w_ref[...], NT_DIMS, preferred_element_type=F32)
        gw_ref[...] += lax.dot_general(yv, dx.astype(BF16), TN_DIMS, preferred_element_type=F32)

        @pl.when(pl.program_id(0) == steps - 1)
        def _():
            m = gw_ref[...]
            dg_ref[...] = jnp.sum(w_ref[...].astype(F32) * m, axis=0, keepdims=True)
            gw_ref[...] = m * g_ref[...]

    row = pl.BlockSpec((ts, D_MODEL), lambda i: (i, 0))
    return _pallas(
        body, name=name, grid=(s // ts,), in_specs=[row, row, _full((1, D_MODEL)), _full((D_MODEL, D_MODEL)), row],
        out_specs=[row, _full((1, 1)), row, _full((D_MODEL, D_MODEL)), _full((1, D_MODEL))],
        out_shape=[_sds((s, D_MODEL)), _sds((1, 1)), _sds((s, D_MODEL)), _sds((D_MODEL, D_MODEL)), _sds((1, D_MODEL))],
        operands=(y, x, gate, w_out, target), vmem_mib=48)


def _out_bwd(dxo, y, gate, w_out, name, jobs=()):
    s = dxo.shape[0]
    ts = min(512, s)
    steps = s // ts

    def body(dx_ref, y_ref, g_ref, w_ref, dy_ref, gw_ref, dg_ref):
        @pl.when(pl.program_id(0) == 0)
        def _():
            gw_ref[...] = jnp.zeros_like(gw_ref)

        dx = dx_ref[...]
        dy_ref[...] = lax.dot_general((dx * g_ref[...]).astype(BF16), w_ref[...], NT_DIMS, preferred_element_type=F32)
        gw_ref[...] += lax.dot_general(y_ref[...], dx.astype(BF16), TN_DIMS, preferred_element_type=F32)

        @pl.when(pl.program_id(0) == steps - 1)
        def _():
            m = gw_ref[...]
            dg_ref[...] = jnp.sum(w_ref[...].astype(F32) * m, axis=0, keepdims=True)
            gw_ref[...] = m * g_ref[...]

    row = pl.BlockSpec((ts, D_MODEL), lambda i: (i, 0))
    return _pallas(
        body, name=name, grid=(s // ts,), in_specs=[row, row, _full((1, D_MODEL)), _full((D_MODEL, D_MODEL))],
        out_specs=[row, _full((D_MODEL, D_MODEL)), _full((1, D_MODEL))],
        out_shape=[_sds((s, D_MODEL)), _sds((D_MODEL, D_MODEL)), _sds((1, D_MODEL))],
        operands=(dxo, y, gate, w_out), vmem_mib=48, jobs=jobs)


def _mix_bwd(dy, proj, probs, psink, attn, sv, qg2, kg2, wst, name, jobs=()):
    s = proj.shape[0]
    nt = s // TILE

    def body(dy_ref, cur_ref, pkv_ref, nkv_ref, p_ref, ps_ref, attn_ref, sv_ref, qg_ref, kg_ref, wst_ref,
             dpb_ref, dkv_ref, p0_ref, p2_ref, dqg_ref, dkg_ref, dsink_ref, dws_ref, dbsp_ref):
        def put(rows, col, value):
            dpb_ref[rows, col:col + BLOCK] = value.astype(BF16)

        @pl.when(pl.program_id(0) == 0)
        def _():
            dqg_ref[...] = jnp.zeros_like(dqg_ref)
            dkg_ref[...] = jnp.zeros_like(dkg_ref)
            dsink_ref[...] = jnp.zeros_like(dsink_ref)
            dws_ref[...] = jnp.zeros_like(dws_ref)
            dbsp_ref[...] = jnp.zeros_like(dbsp_ref)

        lo = _lo_mask((BLOCK, BLOCK))
        lo_kv = _lo_mask((TILE + 2 * BLOCK, BLOCK))
        eye = _diagonal()
        lane_row = lax.broadcasted_iota(jnp.int32, (1, BLOCK), 1)
        qg = qg_ref[...]
        kg = kg_ref[...]

        k_all, v_all = _kv_rows(cur_ref, pkv_ref, nkv_ref)
        rk = _half_rms_scale(k_all, lo_kv)
        khat = k_all * rk
        kn_all = (khat * kg).astype(BF16)
        vb_all = v_all.astype(BF16)

        dkn_parts, dv_parts = [], []
        dsink = jnp.zeros((1, BLOCK), F32)
        dqg = jnp.zeros((1, BLOCK), F32)
        for b in range(SUB):
            rows = slice(BLOCK * b, BLOCK * (b + 1))
            window = slice(BLOCK * b, BLOCK * (b + 3))
            kn, vb = kn_all[window], vb_all[window]

            qhat, rq = [], []
            for p in range(4):
                q = cur_ref[rows, C_Q + BLOCK * p:C_Q + BLOCK * (p + 1)]
                r = _half_rms_scale(q, lo)
                rq.append(r)
                qhat.append(q * r)
            q_stack = _stack_heads([(qh * qg) * 0.125 for qh in qhat], lo).astype(BF16)
            pb = p_ref[STACK * b:STACK * (b + 1), :]
            prob = pb.astype(F32)

            dout = []
            for p in range(4):
                g = cur_ref[rows, C_GA + BLOCK * p:C_GA + BLOCK * (p + 1)]
                sg = _sigmoid(g)
                dya = dy_ref[rows, BLOCK * p:BLOCK * (p + 1)]
                attn = attn_ref[rows, BLOCK * p:BLOCK * (p + 1)]
                put(rows, C_GA + BLOCK * p, dya * attn * (sg * (1.0 + g * (1.0 - sg))))
                dout.append(dya * (g * sg))
            do_stack = _stack_heads(dout, lo).astype(BF16)
            dp = lax.dot_general(do_stack, vb, NT_DIMS, preferred_element_type=F32)
            delta = jnp.sum(prob * dp, axis=-1, keepdims=True)
            dsb = (prob * (dp - delta)).astype(BF16)

            for h in range(N_HEADS):
                delta_row = _column_as_row(jnp.broadcast_to(delta[BLOCK * h:BLOCK * (h + 1)], (BLOCK, BLOCK)), eye)
                tot = jnp.sum(ps_ref[N_HEADS * b + h:N_HEADS * b + h + 1, :] * delta_row, axis=-1, keepdims=True)
                dsink = dsink - jnp.where(lane_row == h, tot, 0.0)

            dq_stack = jnp.dot(dsb, kn, preferred_element_type=F32) * 0.125
            dkn_parts.append(lax.dot_general(dsb, q_stack, TN_DIMS, preferred_element_type=F32))
            dv_parts.append(lax.dot_general(pb, do_stack, TN_DIMS, preferred_element_type=F32))

            for p in range(4):
                dqn = _unstack_pair(dq_stack, p, lo)
                qh = qhat[p]
                dqg = dqg + jnp.sum(dqn * qh, axis=0, keepdims=True)
                dqh = dqn * qg
                mean = _half_sum(dqh * qh, lo) * (1.0 / HEAD_DIM)
                put(rows, C_Q + BLOCK * p, rq[p] * (dqh - qh * mean))

        dws_new, dbs_new = [], []
        for p in range(4):
            rs, vnfs, vns, dsvs, dbs = [], [], [], [], None
            for b in range(SUB):
                rows = slice(BLOCK * b, BLOCK * (b + 1))
                vg = cur_ref[rows, C_VG + BLOCK * p:C_VG + BLOCK * (p + 1)]
                r = _half_rms_scale(vg, lo)
                vnf = vg * r
                sv = sv_ref[rows, BLOCK * p:BLOCK * (p + 1)]
                u = cur_ref[rows, C_U + BLOCK * p:C_U + BLOCK * (p + 1)]
                g = cur_ref[rows, C_GG + BLOCK * p:C_GG + BLOCK * (p + 1)]
                sg = _sigmoid(g)
                dym = dy_ref[rows, D_ATTN + BLOCK * p:D_ATTN + BLOCK * (p + 1)]
                put(rows, C_GG + BLOCK * p, dym * (u * sv) * (sg * (1.0 + g * (1.0 - sg))))
                dgm = dym * (g * sg)
                put(rows, C_U + BLOCK * p, dgm * sv)
                dsv = dgm * u
                term = jnp.where(lo, jnp.sum(jnp.where(lo, dsv, 0.0), axis=-1, keepdims=True),
                                 jnp.sum(jnp.where(lo, 0.0, dsv), axis=-1, keepdims=True))
                dbs = term if dbs is None else dbs + term
                rs.append(r)
                vnfs.append(vnf)
                vns.append(vnf.astype(BF16))
                dsvs.append(dsv)
            dbs_new.append(dbs)
            vn = jnp.concatenate(vns, axis=1)
            dsv = jnp.concatenate(dsvs, axis=1)
            lo_t = (lax.broadcasted_iota(jnp.int32, dsv.shape, 1) & (BLOCK - 1)) < HEAD_DIM
            dws_new.append(lax.dot_general(jnp.where(lo_t, dsv, 0.0).astype(BF16), vn, NT_DIMS, preferred_element_type=F32))
            dws_new.append(lax.dot_general(jnp.where(lo_t, 0.0, dsv).astype(BF16), vn, NT_DIMS, preferred_element_type=F32))
            dsvb = dsv.astype(BF16)
            dvn_a = jnp.dot(wst_ref[2 * p], dsvb, preferred_element_type=F32)
            dvn_b = jnp.dot(wst_ref[2 * p + 1], dsvb, preferred_element_type=F32)
            for b in range(SUB):
                lanes = slice(BLOCK * b, BLOCK * (b + 1))
                dvn = jnp.where(lo, dvn_a[:, lanes], dvn_b[:, lanes])
                mean = _half_sum(dvn * vnfs[b], lo) * (1.0 / HEAD_DIM)
                put(slice(BLOCK * b, BLOCK * (b + 1)), C_VG + BLOCK * p, rs[b] * (dvn - vnfs[b] * mean))

        dsink_ref[...] += dsink
        dqg = jnp.broadcast_to(dqg, (8, BLOCK))
        dqg_ref[...] += dqg + pltpu.roll(dqg, HEAD_DIM, 1)

        dkn = _overlap_add(dkn_parts)
        dv = _overlap_add(dv_parts)
        dkg = jnp.broadcast_to(jnp.sum(dkn * khat, axis=0, keepdims=True), (8, BLOCK))
        dkg_ref[...] += dkg + pltpu.roll(dkg, HEAD_DIM, 1)
        dkh = dkn * kg
        dk = rk * (dkh - khat * (_half_sum(dkh * khat, lo_kv) * (1.0 / HEAD_DIM)))
        dpb_ref[:, C_K:C_GA] = jnp.zeros((TILE, 2 * D_KV), BF16)
        dkv_ref[:, 0:D_KV] = dk[BLOCK:BLOCK + TILE]
        dkv_ref[:, D_KV:2 * D_KV] = dv[BLOCK:BLOCK + TILE]
        p0_ref[:, 0:D_KV] = dk[0:BLOCK]
        p0_ref[:, D_KV:2 * D_KV] = dv[0:BLOCK]
        p2_ref[:, 0:D_KV] = dk[BLOCK + TILE:]
        p2_ref[:, D_KV:2 * D_KV] = dv[BLOCK + TILE:]
        for g, new in enumerate(dws_new):
            dws_ref[g] += new
        for p, new in enumerate(dbs_new):
            dbsp_ref[p] += new

    cur, pkv, nkv, _ = _mix_specs(nt)
    kv_blk = (BLOCK, 2 * D_KV)
    half = pl.BlockSpec((TILE, D_ATTN), lambda i: (i, 0))
    return _pallas(
        body, name=name, grid=(nt,),
        in_specs=[pl.BlockSpec((TILE, D_MODEL), lambda i: (i, 0)), cur, pkv, nkv,
                  pl.BlockSpec((SUB * STACK, 3 * BLOCK), lambda i: (i, 0)), pl.BlockSpec((SUB * N_HEADS, BLOCK), lambda i: (i, 0)),
                  half, half, _full((1, BLOCK)), _full((1, BLOCK)), _full((8, BLOCK, BLOCK))],
        out_specs=[cur, pl.BlockSpec((TILE, 2 * D_KV), lambda i: (i, 0)),
                   pl.BlockSpec(kv_blk, lambda i: ((i + nt - 1) % nt, 0)),
                   pl.BlockSpec(kv_blk, lambda i: ((i + 1) % nt, 0)),
                   _full((8, BLOCK)), _full((8, BLOCK)), _full((1, BLOCK)),
                   _full((8, BLOCK, BLOCK)), _full((4, BLOCK, BLOCK))],
        out_shape=[_sds((s, D_IN), BF16), _sds((s, 2 * D_KV)), _sds((nt * BLOCK, 2 * D_KV)), _sds((nt * BLOCK, 2 * D_KV)),
                   _sds((8, BLOCK)), _sds((8, BLOCK)), _sds((1, BLOCK)),
                   _sds((8, BLOCK, BLOCK)), _sds((4, BLOCK, BLOCK))],
        operands=(dy, proj, proj, proj, probs, psink, attn, sv, qg2, kg2, wst), vmem_mib=56, jobs=jobs)


def _w_in_grad(dpb, dkv, p0, p2, x, ng, scale, shift, name, jobs=()):
    s = x.shape[0]
    ts = min(2 * TILE, s)
    tiles = ts // TILE

    def body(dpb_ref, dkv_ref, p0_ref, p2_ref, x_ref, ng_ref, sc_ref, sh_ref, gw_ref, dkvb_ref):
        @pl.when(pl.program_id(0) == 0)
        def _():
            gw_ref[...] = jnp.zeros_like(gw_ref)

        xv = x_ref[...]
        r = lax.rsqrt(jnp.mean(xv * xv, axis=-1, keepdims=True) + EPS)
        h = (((xv * r) * ng_ref[...]) * (1.0 + sc_ref[...]) + sh_ref[...]).astype(BF16)
        for t in range(tiles):
            halo = slice(BLOCK * t, BLOCK * (t + 1))
            first = slice(TILE * t, TILE * t + BLOCK)
            last = slice(TILE * (t + 1) - BLOCK, TILE * (t + 1))
            dkvb_ref[first, :] = (dkv_ref[first, :] + p2_ref[halo, :]).astype(BF16)
            if SUB > 2:
                inner = slice(TILE * t + BLOCK, TILE * (t + 1) - BLOCK)
                dkvb_ref[inner, :] = dkv_ref[inner, :].astype(BF16)
            dkvb_ref[last, :] = (dkv_ref[last, :] + p0_ref[halo, :]).astype(BF16)
        gw_ref[...] += lax.dot_general(dpb_ref[...], h, TN_DIMS, preferred_element_type=F32)
        gw_ref[C_K:C_GA, :] += lax.dot_general(dkvb_ref[...], h, TN_DIMS, preferred_element_type=F32)

    kv = pl.BlockSpec((ts, 2 * D_KV), lambda i: (i, 0))
    halo = pl.BlockSpec((tiles * BLOCK, 2 * D_KV), lambda i: (i, 0))
    vec = _full((1, D_MODEL))
    return _pallas(
        body, name=name, grid=(s // ts,),
        in_specs=[pl.BlockSpec((ts, D_IN), lambda i: (i, 0)), kv, halo, halo,
                  pl.BlockSpec((ts, D_MODEL), lambda i: (i, 0)), vec, vec, vec],
        out_specs=[_full((D_IN, D_MODEL)), kv], out_shape=[_sds((D_IN, D_MODEL)), _sds((s, 2 * D_KV), BF16)],
        operands=(dpb, dkv, p0, p2, x, ng, scale, shift), vmem_mib=56, jobs=jobs)


def _proj_bwd(dpb, dkvb, x, dxo, ng, scale, wt, name, jobs=()):
    s = x.shape[0]
    ts = min(512, s)

    def body(dpb_ref, dkvb_ref, x_ref, dxo_ref, ng_ref, sc_ref, w_ref, dxi_ref, dsh_ref, dsc_ref, dng_ref):
        @pl.when(pl.program_id(0) == 0)
        def _():
            dsh_ref[...] = jnp.zeros_like(dsh_ref)
            dsc_ref[...] = jnp.zeros_like(dsc_ref)
            dng_ref[...] = jnp.zeros_like(dng_ref)

        dh = (jnp.dot(dpb_ref[...], w_ref[...], preferred_element_type=F32)
              + jnp.dot(dkvb_ref[...], w_ref[C_K:C_GA, :], preferred_element_type=F32))

        xv = x_ref[...]
        r = lax.rsqrt(jnp.mean(xv * xv, axis=-1, keepdims=True) + EPS)
        xn = xv * r
        ngv = ng_ref[...]
        sc1 = 1.0 + sc_ref[...]
        dsh_ref[...] += jnp.sum(dh, axis=0, keepdims=True)
        dsc_ref[...] += jnp.sum(dh * (xn * ngv), axis=0, keepdims=True)
        dh1 = dh * sc1
        dng_ref[...] += jnp.sum(dh1 * xn, axis=0, keepdims=True)
        dxn = dh1 * ngv
        dxi_ref[...] = r * (dxn - xn * jnp.mean(dxn * xn, axis=-1, keepdims=True)) + dxo_ref[...]

    row = pl.BlockSpec((ts, D_MODEL), lambda i: (i, 0))
    vec = _full((1, D_MODEL))
    return _pallas(
        body, name=name, grid=(s // ts,),
        in_specs=[pl.BlockSpec((ts, D_IN), lambda i: (i, 0)), pl.BlockSpec((ts, 2 * D_KV), lambda i: (i, 0)),
                  row, row, vec, vec, _full((D_IN, D_MODEL))],
        out_specs=[row, vec, vec, vec],
        out_shape=[_sds((s, D_MODEL)), _sds((1, D_MODEL)), _sds((1, D_MODEL)), _sds((1, D_MODEL))],
        operands=(dpb, dkvb, x, dxo, ng, scale, wt), vmem_mib=48, jobs=jobs)


def _pair_sum(g, r, c_idx, name, send_dtype=None):
    _, rows, cols = g.shape
    half = rows // 2

    def body(c_ref, g_ref, r_ref, o_ref, *narrow):
        total = g_ref[...] + r_ref[...]
        o_ref[...] = total
        for n_ref in narrow:
            n_ref[...] = total.astype(n_ref.dtype)

    blk = (None, half, cols)
    out_blk = pl.BlockSpec(blk, lambda j, c: (j, 0, 0))
    shapes = [_sds((N_CHIPS, half, cols))] + ([_sds((N_CHIPS, half, cols), send_dtype)] if send_dtype else [])
    return pl.pallas_call(
        body, name=name,
        grid_spec=pltpu.PrefetchScalarGridSpec(
            num_scalar_prefetch=1, grid=(N_CHIPS,),
            in_specs=[pl.BlockSpec(blk, lambda j, c: (j, c[0], 0)), out_blk], out_specs=[out_blk] * len(shapes)),
        out_shape=shapes,
        compiler_params=pltpu.CompilerParams(dimension_semantics=("arbitrary",), vmem_limit_bytes=32 * MIB),
    )(*_in_hbm(c_idx, g, r))


def _chip_sum(p, r, place, name):
    _, rows, cols = p.shape
    tr = rows // 2

    def body(j_ref, p_ref, r_ref, o_ref):
        o_ref[...] = ((p_ref[...] + r_ref[0].astype(F32)) + r_ref[1].astype(F32)) + r_ref[2].astype(F32)

    return pl.pallas_call(
        body, name=name,
        grid_spec=pltpu.PrefetchScalarGridSpec(
            num_scalar_prefetch=1, grid=(2,),
            in_specs=[pl.BlockSpec((None, tr, cols), lambda t, j: (j[0], t, 0)),
                      pl.BlockSpec((3, tr, cols), lambda t, j: (0, t, 0))],
            out_specs=pl.BlockSpec((tr, cols), lambda t, j: (2 * j[1] + t, 0))),
        out_shape=_sds((2 * rows, cols)),
        compiler_params=pltpu.CompilerParams(dimension_semantics=("arbitrary",), vmem_limit_bytes=32 * MIB),
    )(*_in_hbm(place, p, r))


def _cast_permute_w_in(wt, place_chunks):
    def body(t_ref, w_ref, w0_ref, w1_ref):
        def cast_into(o_ref):
            for t in range(N_CHUNKS):
                src = pl.multiple_of(t_ref[1 + t] * CHUNK_ROWS, CHUNK_ROWS)
                o_ref[CHUNK_ROWS * t:CHUNK_ROWS * (t + 1), :] = w_ref[pl.ds(src, CHUNK_ROWS), :].astype(BF16)

        @pl.when(pl.program_id(0) == 0)
        def _():
            cast_into(w0_ref)

        @pl.when(pl.program_id(0) == 1)
        def _():
            cast_into(w1_ref)

    return pl.pallas_call(
        body, name="cast_permute_w_in",
        grid_spec=pltpu.PrefetchScalarGridSpec(
            num_scalar_prefetch=1, grid=(DEPTH,),
            in_specs=[pl.BlockSpec((None, W_IN_BLK, D_MODEL), lambda l, tbl: (l, 0, 0))],
            out_specs=[pl.BlockSpec((None, W_IN_BLK, D_MODEL), lambda l, tbl: (tbl[0], 0, 0)),
                       pl.BlockSpec((W_IN_BLK, D_MODEL), lambda l, tbl: (0, 0))]),
        out_shape=[_sds((N_CHIPS, W_IN_BLK, D_MODEL), BF16), _sds((W_IN_BLK, D_MODEL), BF16)],
        compiler_params=pltpu.CompilerParams(dimension_semantics=("arbitrary",), vmem_limit_bytes=32 * MIB),
    )(*_in_hbm(place_chunks, wt))


def _gather_inputs(c, w_out, w0):
    half = W_IN_BLK // 2

    def body(c_ref, wout_ref, mine_ref, call_ref, woutb_ref, w0_ref, send_sems, recv_sems):
        x, y, cc = _coords()
        j = 2 * x + y
        b = 2 * j + cc
        sib = (x, y, 1 - cc)
        woutb_ref[...] = wout_ref[...].astype(BF16)
        call_ref[b] = c_ref[...]
        chips = _other_chips(x, y)

        def sems(k):
            return send_sems.at[k], recv_sems.at[k]

        def half_rows(chip_index):
            return w0_ref.at[chip_index, pl.ds(cc * half, half), :]

        first = [_remote(mine_ref.at[j, pl.ds(cc * half, half), :], half_rows(j), sems(k), (*chip, cc))
                 for k, chip in enumerate(chips)]
        k = 3
        rest = []
        for fx in (0, 1):
            for fy in (0, 1):
                for fc in (0, 1):
                    if fx or fy or fc:
                        dev = (1 - x if fx else x, 1 - y if fy else y, 1 - cc if fc else cc)
                        rest.append(_remote(call_ref.at[b], call_ref.at[b], sems(k), dev))
                        k += 1
        for cp in first + rest:
            cp.start()
        passed = []
        for k, chip in enumerate(chips):
            jk = 2 * chip[0] + chip[1]
            first[k].wait_recv()
            passed.append(_remote(half_rows(jk), half_rows(jk), sems(10 + k), sib))
            passed[k].start()
        for cp in first:
            cp.wait_send()
        for cp in rest + passed:
            cp.wait()

    return pl.pallas_call(
        body, name="gather_inputs", in_specs=[VMEM, VMEM, ANY], out_specs=[VMEM, VMEM, ANY],
        out_shape=[_sds((N_DEV, 1, D_MODEL)), _sds((DEPTH, W_OUT_BLK, D_MODEL), BF16),
                   _sds((N_CHIPS, W_IN_BLK, D_MODEL), BF16)],
        scratch_shapes=[pltpu.SemaphoreType.DMA((13,)), pltpu.SemaphoreType.DMA((13,))],
        input_output_aliases={2: 2},
        compiler_params=pltpu.CompilerParams(vmem_limit_bytes=32 * MIB),
    )(c, w_out, w0)


def _ada_rows(c_all, w_ada, b_blk):
    def body(c_ref, w_ref, b_ref, o_ref, cond_ref):
        cv = c_ref[...]
        cond = (cv * _sigmoid(cv)).astype(BF16)
        cond_ref[...] = cond.astype(F32)
        for l in range(DEPTH):
            o_ref[:, l, :] = jnp.dot(cond, w_ref[l].astype(BF16), preferred_element_type=F32) + b_ref[l:l + 1, :]

    return pl.pallas_call(
        body, name="ada_rows", in_specs=[VMEM, VMEM, VMEM], out_specs=[VMEM, VMEM],
        out_shape=[_sds((N_DEV, DEPTH, W_ADA_BLK)), _sds((N_DEV, D_MODEL))],
        compiler_params=pltpu.CompilerParams(vmem_limit_bytes=32 * MIB),
    )(c_all, w_ada, b_blk)


def _exchange_ada(part):
    def body(part_ref, out_ref, send_sems, recv_sems):
        x, y, cc = _coords()
        j = 2 * x + y
        out_ref[j] = part_ref[2 * j + cc]
        copies = []
        for k, chip in enumerate(_other_chips(x, y)):
            b_dst = 4 * chip[0] + 2 * chip[1] + cc
            copies.append(_remote(part_ref.at[b_dst], out_ref.at[j], (send_sems.at[k], recv_sems.at[k]), (*chip, cc)))
        for cp in copies:
            cp.start()
        for cp in copies:
            cp.wait()

    return pl.pallas_call(
        body, name="exchange_ada", in_specs=[VMEM], out_specs=VMEM,
        out_shape=_sds((N_CHIPS, DEPTH, W_ADA_BLK)),
        scratch_shapes=[pltpu.SemaphoreType.DMA((3,)), pltpu.SemaphoreType.DMA((3,))],
    )(part)


def _adamw_math(w, g, m, v):
    m = ADAM_B1 * m + (1.0 - ADAM_B1) * g
    v = ADAM_B2 * v + (1.0 - ADAM_B2) * (g * g)
    m_hat = m / (1.0 - ADAM_B1 ** ADAM_STEP)
    v_hat = v / (1.0 - ADAM_B2 ** ADAM_STEP)
    delta = -ADAM_LR * (m_hat / (jnp.sqrt(v_hat) + ADAM_EPS) + ADAM_WD * w)
    return delta, m, v


def _adamw_w_in(w, g0, g1, m, v, pos_chunks):
    def body(t_ref, w_ref, g0_ref, g1_ref, m_ref, v_ref, g_ref, d_ref, nm_ref, nv_ref):
        for l, src in enumerate((g0_ref, g1_ref)):
            g = src[...]
            g_ref[l] = g
            d_ref[l], nm_ref[l], nv_ref[l] = _adamw_math(w_ref[l], g, m_ref[l], v_ref[l])

    nat = pl.BlockSpec((DEPTH, CHUNK_ROWS, D_MODEL), lambda t, tbl: (0, t, 0))
    per = pl.BlockSpec((CHUNK_ROWS, D_MODEL), lambda t, tbl: (tbl[t], 0))
    return pl.pallas_call(
        body, name="adamw_w_in",
        grid_spec=pltpu.PrefetchScalarGridSpec(num_scalar_prefetch=1, grid=(N_CHUNKS,),
                                               in_specs=[nat, per, per, nat, nat], out_specs=[nat] * 4),
        out_shape=[_sds(w.shape)] * 4,
        compiler_params=pltpu.CompilerParams(dimension_semantics=("arbitrary",)),
    )(*_in_hbm(pos_chunks, w, g0, g1, m, v))


def _adamw_w_out(w, g0, g1, m, v):
    def body(w_ref, g0_ref, g1_ref, m_ref, v_ref, g_ref, d_ref, nm_ref, nv_ref):
        g = jnp.where(pl.program_id(0) == 0, g0_ref[...], g1_ref[...])
        g_ref[...] = g
        d_ref[...], nm_ref[...], nv_ref[...] = _adamw_math(w_ref[...], g, m_ref[...], v_ref[...])

    blk = pl.BlockSpec((None, W_OUT_BLK, D_MODEL), lambda l: (l, 0, 0))
    gblk = _full((W_OUT_BLK, D_MODEL))
    return pl.pallas_call(
        body, name="adamw_w_out", grid=(DEPTH,), in_specs=[blk, gblk, gblk, blk, blk], out_specs=[blk] * 4,
        out_shape=[_sds(w.shape)] * 4,
        compiler_params=pltpu.CompilerParams(dimension_semantics=("arbitrary",), vmem_limit_bytes=32 * MIB),
    )(w, g0, g1, m, v)


def _w_ada_grad_adamw(cond_t, dada, w, m, v):
    _, rows, cols = w.shape
    tr = 256

    def body(ct_ref, da_ref, w_ref, m_ref, v_ref, g_ref, d_ref, nm_ref, nv_ref):
        g = jnp.dot(ct_ref[...], da_ref[...].astype(BF16), preferred_element_type=F32)
        g_ref[...] = g
        d_ref[...], nm_ref[...], nv_ref[...] = _adamw_math(w_ref[...], g, m_ref[...], v_ref[...])

    blk = pl.BlockSpec((None, tr, cols), lambda l, t: (l, t, 0))
    return pl.pallas_call(
        body, name="w_ada_grad_adamw", grid=(DEPTH, rows // tr),
        in_specs=[pl.BlockSpec((tr, BLOCK), lambda l, t: (t, 0)), pl.BlockSpec((None, BLOCK, cols), lambda l, t: (l, 0, 0)),
                  blk, blk, blk],
        out_specs=[blk] * 4, out_shape=[_sds(w.shape)] * 4,
        compiler_params=pltpu.CompilerParams(dimension_semantics=("arbitrary", "arbitrary"), vmem_limit_bytes=32 * MIB),
    )(cond_t, dada, w, m, v)


def _small_sum_adamw(gathered, w, m, v):
    n = len(gathered)

    def body(*refs):
        w_ref, m_ref, v_ref, g_ref, d_ref, nm_ref, nv_ref = refs[n:]

        def total(ref):
            g = ref[0]
            for b in range(1, N_DEV):
                g = g + ref[b]
            return g

        g = jnp.concatenate([total(ref) for ref in refs[:n]], axis=0)
        g_ref[...] = g
        d_ref[...], nm_ref[...], nv_ref[...] = _adamw_math(w_ref[...], g, m_ref[...], v_ref[...])

    return pl.pallas_call(
        body, name="small_sum_adamw", in_specs=[VMEM] * (n + 3), out_specs=[VMEM] * 4, out_shape=[_sds(w.shape)] * 4,
        compiler_params=pltpu.CompilerParams(vmem_limit_bytes=48 * MIB),
    )(*gathered, w, m, v)


_SMALL_A = (("w_s", 8 * BLOCK), ("b_s", 8), ("q_gain", 1), ("k_gain", 1), ("sink", 1))
_SMALL_B = (("b_ada", DEPTH * 24), ("norm_gain", DEPTH * 8), ("sq_err", 1))


def _pack_rows(parts, layout, layer=None):
    rows = []
    for name, n in layout:
        flat = (parts[name] if layer is None else parts[name][layer]).reshape(-1)
        rows.append(jnp.pad(flat, (0, n * 128 - flat.shape[0])).reshape(n, 128))
    n_rows = sum(n for _, n in layout)
    if n_rows % 8:
        rows.append(jnp.zeros((-n_rows % 8, 128), F32))
    return jnp.concatenate(rows, axis=0)


def _pack_small(parts):
    return jnp.concatenate([_pack_rows(parts, _SMALL_A, l) for l in range(DEPTH)] + [_pack_rows(parts, _SMALL_B)], axis=0)


def _unpack_small(packed, shapes):
    def take(r0, layout, shape_of):
        got = {}
        for name, n in layout:
            shape = shape_of(name)
            size = 1
            for d in shape:
                size *= d
            got[name] = packed[r0:r0 + n].reshape(-1)[:size].reshape(shape)
            r0 += n
        return got, r0 + -r0 % 8

    layers, r0 = [], 0
    for _ in range(DEPTH):
        got, r0 = take(r0, _SMALL_A, lambda name: shapes[name][1:])
        layers.append(got)
    out, _ = take(r0, _SMALL_B, lambda name: shapes[name])
    out.update({name: jnp.stack([layer[name] for layer in layers]) for name, _ in _SMALL_A})
    return out


def _permute_heads(a, axis):
    shp = a.shape
    a = a.reshape(shp[:axis] + (2, 4, HEAD_DIM) + shp[axis + 1:])
    a = jnp.swapaxes(a, axis, axis + 1)
    return a.reshape(shp)


def _unpermute_heads(a, axis):
    shp = a.shape
    a = a.reshape(shp[:axis] + (4, 2, HEAD_DIM) + shp[axis + 1:])
    a = jnp.swapaxes(a, axis, axis + 1)
    return a.reshape(shp)


def _permute_w_out(w):
    return jnp.concatenate([_permute_heads(w[:D_ATTN], 0), w[D_ATTN:]], axis=0)


def _unpermute_w_out(w):
    return jnp.concatenate([_unpermute_heads(w[:D_ATTN], 0), w[D_ATTN:]], axis=0)


def kernel(x, c, w_ada, b_ada, norm_gain, w_in, q_gain, k_gain, sink, w_s, b_s, w_out, loss_target, m_w_ada, m_b_ada, m_norm_gain, m_w_in, m_q_gain, m_k_gain, m_sink, m_w_s, m_b_s, m_w_out, v_w_ada, v_b_ada, v_norm_gain, v_w_in, v_q_gain, v_k_gain, v_sink, v_w_s, v_b_s, v_w_out):
    ix, iy, ic = _coords()
    chip = 2 * ix + iy
    chip_idx = jnp.stack([chip, ic]).astype(jnp.int32)
    core_idx = jnp.reshape(ic, (1,)).astype(jnp.int32)
    src_chunks = lax.dynamic_index_in_dim(jnp.asarray(_CHUNK_SRC), chip, 0, keepdims=False)
    pos_chunks = lax.dynamic_index_in_dim(jnp.asarray(_CHUNK_POS), chip, 0, keepdims=False)
    x0, target = x[0], loss_target[0]

    wt, mt, vt = (jnp.swapaxes(a, 1, 2) for a in (w_in, m_w_in, v_w_in))
    w0_mine, wloc_in1 = _cast_permute_w_in(wt, jnp.concatenate([chip_idx[:1], src_chunks]))
    c_all, wloc_out, w0 = _gather_inputs(c, w_out, w0_mine)
    wts = [w0.reshape(D_IN, D_MODEL), None]

    b_blk = lax.dynamic_slice_in_dim(b_ada, chip * W_ADA_BLK, W_ADA_BLK, axis=1)
    ada_part, cond = _ada_rows(c_all.reshape(N_DEV, D_MODEL), w_ada, b_blk)
    ada = jnp.moveaxis(_exchange_ada(ada_part), 0, 1).reshape(DEPTH, 3 * D_MODEL)
    shift = [ada[l:l + 1, 0:D_MODEL] for l in range(DEPTH)]
    scale = [ada[l:l + 1, D_MODEL:2 * D_MODEL] for l in range(DEPTH)]
    gate = [ada[l:l + 1, 2 * D_MODEL:] for l in range(DEPTH)]
    ng = [norm_gain[l:l + 1] for l in range(DEPTH)]

    qg2 = jnp.concatenate([q_gain, q_gain], axis=-1)
    kg2 = jnp.concatenate([k_gain, k_gain], axis=-1)
    ws_b = w_s.astype(BF16)
    wst_b = jnp.swapaxes(w_s, -1, -2).astype(BF16)
    bsp = jnp.repeat(jnp.swapaxes(b_s.reshape(DEPTH, 4, 2, BLOCK), -1, -2), HEAD_DIM, axis=-1)
    bias = jnp.asarray(_bias_table())

    def mix_args(l):
        return bias, sink[l], qg2[l:l + 1], kg2[l:l + 1], ws_b[l]

    w_out_shape = (W_OUT_BLK, D_MODEL)
    proj0, wo0, wo1 = _proj_fwd(x0, ng[0], scale[0], shift[0], wts[0], "proj_fwd_0",
                                jobs=[_job_gather([(wloc_out, 0), (wloc_out, 1)], [w_out_shape, w_out_shape])])
    y0, *kept0, w1 = _mix_fwd(proj0, *mix_args(0), bsp[0], "mix_fwd_0",
                              jobs=[_job_gather([(wloc_in1, None)], [(W_IN_BLK, D_MODEL)])])
    wts[1] = w1.reshape(D_IN, D_MODEL)
    wos = [_permute_w_out(w.reshape(D_MODEL, D_MODEL)) for w in (wo0, wo1)]
    x1, proj1 = _out_proj_fwd(y0, x0, gate[0], wos[0], ng[1], scale[1], shift[1], wts[1], "out_proj_fwd_01")
    y1, *kept1 = _mix_fwd(proj1, *mix_args(1), bsp[1], "mix_fwd_1")

    def blocks_out(gw):
        return _unpermute_w_out(gw).reshape(N_CHIPS, W_OUT_BLK, D_MODEL)

    def small_pack(dws, dbsp, dqg, dkg, dsink):
        b_s = jnp.swapaxes(dbsp[:, :, ::HEAD_DIM], -1, -2).reshape(8, BLOCK)
        return _pack_rows(dict(w_s=dws, b_s=b_s, q_gain=dqg[0, :HEAD_DIM], k_gain=dkg[0, :HEAD_DIM], sink=dsink[0, :N_HEADS]),
                          _SMALL_A)

    dx2, sq, dy1, gwo1, dgate1 = _out_loss_bwd(y1, x1, gate[1], wos[1], target, "out_loss_bwd_1")
    go1 = blocks_out(gwo1)
    dpb, dkv, p0, p2, dqg1, dkg1, dsink1, dws1, dbsp1, ro1 = _mix_bwd(
        dy1, proj1, *kept1, qg2[1:2], kg2[1:2], wst_b[1], "mix_bwd_1", jobs=[_job_swap(go1)])
    po1, = _pair_sum(go1, ro1, core_idx, "pair_sum_w_out_1")
    gwi1, dkvb, co1, gathered_a1 = _w_in_grad(
        dpb, dkv, p0, p2, x1, ng[1], scale[1], shift[1], "w_in_grad_1",
        jobs=[_job_scatter(po1), _job_all_gather(small_pack(dws1, dbsp1, dqg1, dkg1, dsink1))])
    gi1 = gwi1.reshape(N_CHIPS, W_IN_BLK, D_MODEL)
    fo1 = _chip_sum(po1, co1, chip_idx, "chip_sum_w_out_1")
    dx1, dsh1, dsc1, dng1, grad_wo1, ri1 = _proj_bwd(dpb, dkvb, x1, dx2, ng[1], scale[1], wts[1], "proj_bwd_1",
                                                     jobs=[_job_join(fo1), _job_swap(gi1)])
    pi1, = _pair_sum(gi1, ri1, core_idx, "pair_sum_w_in_1")

    dy0, gwo0, dgate0 = _out_bwd(dx1, y0, gate[0], wos[0], "out_bwd_0")
    go0 = blocks_out(gwo0)
    dpb, dkv, p0, p2, dqg0, dkg0, dsink0, dws0, dbsp0, ci1, ro0 = _mix_bwd(
        dy0, proj0, *kept0, qg2[0:1], kg2[0:1], wst_b[0], "mix_bwd_0", jobs=[_job_scatter(pi1), _job_swap(go0)])
    fi1 = _chip_sum(pi1, ci1, chip_idx, "chip_sum_w_in_1")
    po0, = _pair_sum(go0, ro0, core_idx, "pair_sum_w_out_0")

    gwi0, dkvb, grad_wi1, co0, gathered_a0 = _w_in_grad(
        dpb, dkv, p0, p2, x0, ng[0], scale[0], shift[0], "w_in_grad_0",
        jobs=[_job_join(fi1), _job_scatter(po0), _job_all_gather(small_pack(dws0, dbsp0, dqg0, dkg0, dsink0))])
    gi0 = gwi0.reshape(N_CHIPS, W_IN_BLK, D_MODEL)
    fo0 = _chip_sum(po0, co0, chip_idx, "chip_sum_w_out_0")

    ri0, grad_wo0 = _comm([_job_swap(gi0), _job_join(fo0)], "swap_w_in_0")
    pi0, pi0_send = _pair_sum(gi0, ri0, core_idx, "pair_sum_w_in_0", send_dtype=BF16)
    dx0, dsh0, dsc0, dng0, ci0 = _proj_bwd(dpb, dkvb, x0, dx1, ng[0], scale[0], wts[0], "proj_bwd_0",
                                           jobs=[_job_scatter(pi0_send)])
    fi0 = _chip_sum(pi0, ci0, chip_idx, "chip_sum_w_in_0")

    small_g = dict(
        b_ada=jnp.stack([jnp.concatenate([dsh0, dsc0, dgate0], axis=-1)[0], jnp.concatenate([dsh1, dsc1, dgate1], axis=-1)[0]]),
        norm_gain=jnp.stack([dng0[0], dng1[0]]), sq_err=sq[0])
    none = jnp.zeros((1,), F32)
    small_w = dict(w_s=w_s, b_s=b_s, b_ada=b_ada, norm_gain=norm_gain, q_gain=q_gain, k_gain=k_gain, sink=sink, sq_err=none)
    small_m = dict(w_s=m_w_s, b_s=m_b_s, b_ada=m_b_ada, norm_gain=m_norm_gain, q_gain=m_q_gain, k_gain=m_k_gain, sink=m_sink,
                   sq_err=none)
    small_v = dict(w_s=v_w_s, b_s=v_b_s, b_ada=v_b_ada, norm_gain=v_norm_gain, q_gain=v_q_gain, k_gain=v_k_gain, sink=v_sink,
                   sq_err=none)
    grad_wi0, gathered_b = _comm([_job_join(fi0), _job_all_gather(_pack_rows(small_g, _SMALL_B))], "join_w_in_0")
    packed = _small_sum_adamw([a.reshape(N_DEV, -1, 128) for a in (gathered_a0, gathered_a1, gathered_b)],
                              _pack_small(small_w), _pack_small(small_m), _pack_small(small_v))
    shapes = {k: a.shape for k, a in small_w.items()}
    sg, sd, sm, sv = (_unpack_small(p, shapes) for p in packed)
    loss = 0.5 * sg["sq_err"][0]

    dada_all = gathered_b.reshape(N_DEV, -1, 128)[:, 0:DEPTH * 24].reshape(N_DEV, DEPTH, 3 * D_MODEL)
    dada_blk = jnp.moveaxis(lax.dynamic_slice_in_dim(dada_all, chip * W_ADA_BLK, W_ADA_BLK, axis=2), 0, 1)
    pad = BLOCK - N_DEV
    ada_out = _w_ada_grad_adamw(
        jnp.pad(cond.T, ((0, 0), (0, pad))).astype(BF16), jnp.pad(dada_blk, ((0, 0), (0, pad), (0, 0))),
        w_ada, m_w_ada, v_w_ada)

    in_out = [jnp.swapaxes(a, 1, 2) for a in _adamw_w_in(wt, grad_wi0, grad_wi1, mt, vt, pos_chunks)]
    out_out = _adamw_w_out(w_out, grad_wo0, grad_wo1, m_w_out, v_w_out)

    def ordered(k):
        small = (sg, sd, sm, sv)[k]
        return (ada_out[k], small["b_ada"], small["norm_gain"], in_out[k], small["q_gain"], small["k_gain"], small["sink"],
                small["w_s"], small["b_s"], out_out[k])

    return (loss, dx0[None], *ordered(0), *ordered(1), *ordered(2), *ordered(3))
```

```python
import numpy as np

import jax
import jax.numpy as jnp
from jax import lax
from jax.experimental import pallas as pl
from jax.experimental.pallas import tpu as pltpu

F32 = jnp.float32
BF16 = jnp.bfloat16

D_MODEL = 1024
DEPTH = 2
HEAD_DIM = 64
N_HEADS = 8
BLOCK = 128
SUB = 4
TILE = SUB * BLOCK
STACK = 8 * BLOCK
D_ATTN = 512
D_KV = 128
D_IN = 2816
N_CHIPS = 4
N_DEV = 8
W_IN_BLK = D_IN // N_CHIPS
W_OUT_BLK = D_MODEL // N_CHIPS
W_ADA_BLK = 3 * D_MODEL // N_CHIPS
CHUNK_ROWS = HEAD_DIM
N_CHUNKS = W_IN_BLK // CHUNK_ROWS
EPS = 1e-6
NEG_INF = -1e30

C_Q, C_K, C_V, C_GA, C_U, C_VG, C_GG = 0, 512, 640, 768, 1280, 1792, 2304

ADAM_LR = 0.001
ADAM_B1 = 0.9
ADAM_B2 = 0.999
ADAM_EPS = 1e-08
ADAM_WD = 0.01
ADAM_STEP = 10

MESH = pl.DeviceIdType.MESH
MIB = 1024 * 1024
ANY = pl.BlockSpec(memory_space=pl.ANY)
VMEM = pl.BlockSpec(memory_space=pltpu.VMEM)

NT_DIMS = (((1,), (1,)), ((), ()))
TN_DIMS = (((0,), (0,)), ((), ()))

_PAIR_ORDER = (0, 4, 1, 5, 2, 6, 3, 7)
_CHUNK_SRC = np.array([
    list(_PAIR_ORDER) + [8, 9, 10],
    [0] + [1 + h for h in _PAIR_ORDER] + [9, 10],
    list(range(N_CHUNKS)),
    list(range(N_CHUNKS)),
], np.int32)
_CHUNK_POS = np.argsort(_CHUNK_SRC, axis=1).astype(np.int32)


def _bias_table():
    i = np.arange(N_HEADS * BLOCK)[:, None]
    j = np.arange(3 * BLOCK)[None, :]
    dist = np.abs(j - BLOCK - (i % BLOCK))
    slope = 2.0 ** -(i // BLOCK + 1.0)
    inner = np.where(dist <= BLOCK, -(slope * dist), NEG_INF)
    first = np.where(j >= BLOCK, inner, NEG_INF)
    last = np.where(j < 2 * BLOCK, inner, NEG_INF)
    return np.stack([first, inner, last]).astype(np.float32)


def _full(shape):
    n = len(shape)
    return pl.BlockSpec(shape, lambda *_: (0,) * n)


def _sds(shape, dtype=F32):
    return jax.ShapeDtypeStruct(shape, dtype)


def _coords():
    return lax.axis_index("x"), lax.axis_index("y"), lax.axis_index("c")


def _other_chips(x, y):
    return [(1 - x, y), (x, 1 - y), (1 - x, 1 - y)]


def _in_hbm(*operands):
    return [pltpu.with_memory_space_constraint(a, pltpu.HBM) if a.size * a.dtype.itemsize >= MIB // 4 else a
            for a in operands]


def _remote(src, dst, sems, dev):
    return pltpu.make_async_remote_copy(src_ref=src, dst_ref=dst, send_sem=sems[0], recv_sem=sems[1],
                                        device_id=dev, device_id_type=MESH)


class _Job:
    def __init__(self, inputs, out_shapes, n_remote, n_local, make, then=None, in_place=False):
        self.inputs, self.out_shapes, self.n_remote, self.n_local, self.make = inputs, out_shapes, n_remote, n_local, make
        self.then = then
        self.in_place = in_place


def _job_aliases(jobs, in_base, out_base):
    aliases, a, b = {}, 0, 0
    for j in jobs:
        if j.in_place:
            aliases.update({in_base + a + k: out_base + b + k for k in range(len(j.inputs))})
        a, b = a + len(j.inputs), b + len(j.out_shapes)
    return aliases


def _job_copies(jobs, jin, jout, sems, second=False):
    send, recv, loc = sems
    res, a, b, r, l = [], 0, 0, 0, 0
    for j in jobs:
        build = j.then if second else j.make
        if build is not None:
            res += build(jin[a:a + len(j.inputs)], jout[b:b + len(j.out_shapes)],
                         lambda k, r=r: (send.at[r + k], recv.at[r + k]), lambda k, l=l: loc.at[l + k])
        a, b, r, l = a + len(j.inputs), b + len(j.out_shapes), r + j.n_remote, l + j.n_local
    return res


def _run(copies):
    for cp in copies:
        cp.start()
    for cp in copies:
        cp.wait()


def _job_gather(sources, shapes):
    n = len(sources)

    def make(ins, outs, rsem, lsem):
        x, y, c = _coords()
        j = 2 * x + y
        res = []
        for t, ((_, layer), src, dst) in enumerate(zip(sources, ins, outs)):
            src = src if layer is None else src.at[layer]
            res.append(pltpu.make_async_copy(src, dst.at[j], lsem(t)))
            for k, chip in enumerate(_other_chips(x, y)):
                res.append(_remote(src, dst.at[j], rsem(3 * t + k), (*chip, c)))
        return res

    return _Job([a for a, _ in sources], [_sds((N_CHIPS,) + s, BF16) for s in shapes], 3 * n, n, make)


def _job_gather_once(buf):
    half = buf.shape[1] // 2

    def make(ins, outs, rsem, lsem):
        x, y, c = _coords()
        j = 2 * x + y
        mine = pl.ds(c * half, half)
        return [_remote(ins[0].at[j, mine, :], outs[0].at[j, mine, :], rsem(k), (*chip, c))
                for k, chip in enumerate(_other_chips(x, y))]

    def then(ins, outs, rsem, lsem):
        x, y, c = _coords()
        mine = pl.ds(c * half, half)
        return [_remote(outs[0].at[2 * chip[0] + chip[1], mine, :], outs[0].at[2 * chip[0] + chip[1], mine, :],
                        rsem(3 + k), (x, y, 1 - c)) for k, chip in enumerate(_other_chips(x, y))]

    return _Job([buf], [_sds(buf.shape, buf.dtype)], 6, 0, make, then, in_place=True)


def _job_swap(g):
    _, rows, cols = g.shape
    half = rows // 2

    def make(ins, outs, rsem, lsem):
        x, y, c = _coords()
        return [_remote(ins[0].at[:, pl.ds((1 - c) * half, half), :], outs[0], rsem(0), (x, y, 1 - c))]

    return _Job([g], [_sds((N_CHIPS, half, cols))], 1, 0, make)


def _job_scatter(p):
    def make(ins, outs, rsem, lsem):
        x, y, c = _coords()
        return [_remote(ins[0].at[2 * chip[0] + chip[1]], outs[0].at[k], rsem(k), (*chip, c))
                for k, chip in enumerate(_other_chips(x, y))]

    return _Job([p], [_sds((3,) + p.shape[1:], p.dtype)], 3, 0, make)


def _job_all_gather(blk):
    m_per = blk.shape[0]

    def rows(ref, px, py, pc):
        return ref.at[pl.ds((4 * px + 2 * py + pc) * m_per, m_per), :]

    def make(ins, outs, rsem, lsem):
        x, y, c = _coords()
        res = [pltpu.make_async_copy(ins[0], rows(outs[0], x, y, c), lsem(0)),
               _remote(ins[0], rows(outs[0], x, y, c), rsem(0), (x, y, 1 - c))]
        res += [_remote(ins[0], rows(outs[0], x, y, c), rsem(1 + k), (*chip, c)) for k, chip in enumerate(_other_chips(x, y))]
        return res

    def then(ins, outs, rsem, lsem):
        x, y, c = _coords()
        return [_remote(rows(outs[0], *chip, c), rows(outs[0], *chip, c), rsem(4 + k), (x, y, 1 - c))
                for k, chip in enumerate(_other_chips(x, y))]

    return _Job([blk], [_sds((N_DEV * m_per, blk.shape[1]), blk.dtype)], 7, 1, make, then)


def _job_join(f):
    half = f.shape[0] // 2

    def make(ins, outs, rsem, lsem):
        x, y, c = _coords()
        mine = pl.ds(c * half, half)
        return [_remote(ins[0].at[mine, :], outs[0].at[mine, :], rsem(0), (x, y, 1 - c))]

    return _Job([f], [_sds(f.shape, f.dtype)], 1, 0, make, in_place=True)


def _pallas(body, *, name, grid, in_specs, out_specs, out_shape, operands, vmem_mib, jobs=(), scratch=()):
    in_specs, out_specs, out_shape = list(in_specs), list(out_specs), list(out_shape)
    n_in, n_out = len(in_specs), len(out_specs)
    j_in = [a for j in jobs for a in j.inputs]
    j_out = [s for j in jobs for s in j.out_shapes]
    n_rem = max(1, sum(j.n_remote for j in jobs))
    n_loc = max(1, sum(j.n_local for j in jobs))
    sems = [pltpu.SemaphoreType.DMA((n_rem,)), pltpu.SemaphoreType.DMA((n_rem,)),
            pltpu.SemaphoreType.DMA((n_loc,))] if jobs else []
    scratch = list(scratch) + sems

    def wrapped(*refs):
        ins = refs[:n_in]
        jin = refs[n_in:n_in + len(j_in)]
        outs = refs[n_in + len(j_in):n_in + len(j_in) + n_out]
        jout = refs[n_in + len(j_in) + n_out:n_in + len(j_in) + n_out + len(j_out)]
        own = refs[n_in + len(j_in) + n_out + len(j_out):len(refs) - len(sems)]

        if jobs:
            first = last = None
            for d, n in enumerate(grid):
                f, e = pl.program_id(d) == 0, pl.program_id(d) == n - 1
                first, last = (f, e) if first is None else (first & f, last & e)

            @pl.when(first)
            def _():
                for cp in _job_copies(jobs, jin, jout, refs[-3:]):
                    cp.start()

        body(*ins, *outs, *own)

        if jobs:
            @pl.when(last)
            def _():
                for cp in _job_copies(jobs, jin, jout, refs[-3:]):
                    cp.wait()
                _run(_job_copies(jobs, jin, jout, refs[-3:], second=True))

    return pl.pallas_call(
        wrapped, name=name, grid=grid,
        in_specs=in_specs + [ANY] * len(j_in), out_specs=out_specs + [ANY] * len(j_out),
        out_shape=out_shape + j_out, scratch_shapes=scratch, input_output_aliases=_job_aliases(jobs, n_in, n_out),
        compiler_params=pltpu.CompilerParams(dimension_semantics=("arbitrary",) * len(grid),
                                             vmem_limit_bytes=vmem_mib * MIB),
    )(*_in_hbm(*operands, *j_in))


def _comm(jobs, name):
    j_in = [a for j in jobs for a in j.inputs]
    j_out = [s for j in jobs for s in j.out_shapes]
    n_rem = max(1, sum(j.n_remote for j in jobs))
    n_loc = max(1, sum(j.n_local for j in jobs))

    def body(*refs):
        jin, jout = refs[:len(j_in)], refs[len(j_in):len(j_in) + len(j_out)]
        _run(_job_copies(jobs, jin, jout, refs[-3:]))
        _run(_job_copies(jobs, jin, jout, refs[-3:], second=True))

    return pl.pallas_call(
        body, name=name, in_specs=[ANY] * len(j_in), out_specs=[ANY] * len(j_out), out_shape=j_out,
        scratch_shapes=[pltpu.SemaphoreType.DMA((n_rem,)), pltpu.SemaphoreType.DMA((n_rem,)),
                        pltpu.SemaphoreType.DMA((n_loc,))],
        input_output_aliases=_job_aliases(jobs, 0, 0),
    )(*_in_hbm(*j_in))


def _sigmoid(x):
    return 1.0 / (1.0 + jnp.exp(-x))


def _lo_mask(shape):
    return lax.broadcasted_iota(jnp.int32, shape, len(shape) - 1) < HEAD_DIM


def _half_sum(x, lo):
    a = jnp.sum(jnp.where(lo, x, 0.0), axis=-1, keepdims=True)
    b = jnp.sum(jnp.where(lo, 0.0, x), axis=-1, keepdims=True)
    return jnp.where(lo, a, b)


def _half_rms_scale(x, lo):
    return lax.rsqrt(_half_sum(x * x, lo) * (1.0 / HEAD_DIM) + EPS)


def _stack_heads(pairs, lo):
    return jnp.concatenate([jnp.where(lo, t, 0.0) for t in pairs] + [jnp.where(lo, 0.0, t) for t in pairs], axis=0)


def _unstack_pair(stack, p, lo):
    return jnp.where(lo, stack[BLOCK * p:BLOCK * (p + 1)], stack[BLOCK * (4 + p):BLOCK * (5 + p)])


def _attention_probs(q_stack, kn, bias_ref, sink_ref):
    rows = N_HEADS * BLOCK
    s = lax.dot_general(q_stack, kn, NT_DIMS, preferred_element_type=F32) + bias_ref[...]
    sink = jnp.concatenate([jnp.full((BLOCK, BLOCK), sink_ref[h], F32) for h in range(N_HEADS)], axis=0)
    cols = [s[:, BLOCK * j:BLOCK * (j + 1)] for j in range(3)]
    top = jnp.max(jnp.maximum(jnp.maximum(cols[0], cols[1]), cols[2]), axis=-1, keepdims=True)
    m = jnp.maximum(jnp.broadcast_to(top, (rows, BLOCK)), sink)
    e = [jnp.exp(c - m) for c in cols]
    es = jnp.exp(sink - m)
    inv = 1.0 / (jnp.broadcast_to(jnp.sum((e[0] + e[1]) + e[2], axis=-1, keepdims=True), (rows, BLOCK)) + es)
    return jnp.concatenate([c * inv for c in e], axis=1), es * inv


def _kv_rows(cur_ref, pkv_ref, nkv_ref):
    k = jnp.concatenate([pkv_ref[:, 0:D_KV], cur_ref[:, C_K:C_K + D_KV], nkv_ref[:, 0:D_KV]], axis=0)
    v = jnp.concatenate([pkv_ref[:, D_KV:2 * D_KV], cur_ref[:, C_V:C_V + D_KV], nkv_ref[:, D_KV:2 * D_KV]], axis=0)
    return k, v


def _overlap_add(parts):
    blocks = []
    for j in range(SUB + 2):
        terms = [parts[b][BLOCK * (j - b):BLOCK * (j - b + 1)] for b in range(SUB) if 0 <= j - b <= 2]
        total = terms[0]
        for t in terms[1:]:
            total = total + t
        blocks.append(total)
    return jnp.concatenate(blocks, axis=0)


def _mix_specs(nt):
    cur = pl.BlockSpec((TILE, D_IN), lambda i: (i, 0))
    kv_col = C_K // (2 * D_KV)
    pkv = pl.BlockSpec((BLOCK, 2 * D_KV), lambda i: (jnp.maximum(i * SUB - 1, 0), kv_col))
    nkv = pl.BlockSpec((BLOCK, 2 * D_KV), lambda i: (jnp.minimum((i + 1) * SUB, nt * SUB - 1), kv_col))
    table = (None, N_HEADS * BLOCK, 3 * BLOCK)
    first = pl.BlockSpec(table, lambda i: (jnp.where(i == 0, 0, 1), 0, 0))
    inner = pl.BlockSpec(table, lambda i: (1, 0, 0))
    last = pl.BlockSpec(table, lambda i: (jnp.where(i == nt - 1, 2, 1), 0, 0))
    return cur, pkv, nkv, [first] + [inner] * (SUB - 2) + [last]


def _proj_fwd(x, ng, scale, shift, wt, name, jobs=()):
    s = x.shape[0]
    ts = min(512, s)

    def body(x_ref, ng_ref, sc_ref, sh_ref, w_ref, o_ref):
        xv = x_ref[...]
        r = lax.rsqrt(jnp.mean(xv * xv, axis=-1, keepdims=True) + EPS)
        h = ((xv * r) * ng_ref[...]) * (1.0 + sc_ref[...]) + sh_ref[...]
        o_ref[...] = lax.dot_general(h.astype(BF16), w_ref[...], NT_DIMS, preferred_element_type=F32)

    vec = _full((1, D_MODEL))
    return _pallas(
        body, name=name, grid=(s // ts,),
        in_specs=[pl.BlockSpec((ts, D_MODEL), lambda i: (i, 0)), vec, vec, vec, _full((D_IN, D_MODEL))],
        out_specs=[pl.BlockSpec((ts, D_IN), lambda i: (i, 0))], out_shape=[_sds((s, D_IN))],
        operands=(x, ng, scale, shift, wt), vmem_mib=48, jobs=jobs)


def _diagonal():
    return lax.broadcasted_iota(jnp.int32, (BLOCK, BLOCK), 0) == lax.broadcasted_iota(jnp.int32, (BLOCK, BLOCK), 1)


def _column_as_row(wide, eye):
    return jnp.sum(jnp.where(eye, wide, 0.0), axis=0, keepdims=True)


def _mix_fwd(proj, bias, sink, qg2, kg2, ws, bsp, name, jobs=()):
    s = proj.shape[0]
    nt = s // TILE

    def body(sink_ref, cur_ref, pkv_ref, nkv_ref, *rest):
        bias_refs = rest[:SUB]
        qg_ref, kg_ref, ws_ref, bsp_ref, y_ref, p_ref, ps_ref, attn_ref, sv_ref = rest[SUB:]
        lo = _lo_mask((BLOCK, BLOCK))
        eye = _diagonal()
        lo_kv = _lo_mask((TILE + 2 * BLOCK, BLOCK))
        k_all, v_all = _kv_rows(cur_ref, pkv_ref, nkv_ref)
        kn_all = ((k_all * _half_rms_scale(k_all, lo_kv)) * kg_ref[...]).astype(BF16)
        vb_all = v_all.astype(BF16)
        for b in range(SUB):
            rows = slice(BLOCK * b, BLOCK * (b + 1))
            window = slice(BLOCK * b, BLOCK * (b + 3))
            qn = []
            for p in range(4):
                q = cur_ref[rows, C_Q + BLOCK * p:C_Q + BLOCK * (p + 1)]
                qn.append(((q * _half_rms_scale(q, lo)) * qg_ref[...]) * 0.125)
            q_stack = _stack_heads(qn, lo).astype(BF16)
            prob, psink = _attention_probs(q_stack, kn_all[window], bias_refs[b], sink_ref)
            pb = prob.astype(BF16)
            p_ref[STACK * b:STACK * (b + 1), :] = pb
            ps_ref[N_HEADS * b:N_HEADS * (b + 1), :] = jnp.concatenate(
                [_column_as_row(psink[BLOCK * h:BLOCK * (h + 1)], eye) for h in range(N_HEADS)], axis=0)
            o_stack = jnp.dot(pb, vb_all[window], preferred_element_type=F32)
            for p in range(4):
                g = cur_ref[rows, C_GA + BLOCK * p:C_GA + BLOCK * (p + 1)]
                attn = _unstack_pair(o_stack, p, lo)
                attn_ref[rows, BLOCK * p:BLOCK * (p + 1)] = attn.astype(BF16)
                y_ref[rows, BLOCK * p:BLOCK * (p + 1)] = (attn * (g * _sigmoid(g))).astype(BF16)

        for p in range(4):
            cols = slice(C_VG + BLOCK * p, C_VG + BLOCK * (p + 1))
            vn = []
            for b in range(SUB):
                vg = cur_ref[BLOCK * b:BLOCK * (b + 1), cols]
                vn.append((vg * _half_rms_scale(vg, lo)).astype(BF16))
            vn = jnp.concatenate(vn, axis=1)
            sv_a = jnp.dot(ws_ref[2 * p], vn, preferred_element_type=F32)
            sv_b = jnp.dot(ws_ref[2 * p + 1], vn, preferred_element_type=F32)
            for b in range(SUB):
                rows = slice(BLOCK * b, BLOCK * (b + 1))
                lanes = slice(BLOCK * b, BLOCK * (b + 1))
                sv = jnp.where(lo, sv_a[:, lanes], sv_b[:, lanes]) + bsp_ref[p]
                sv_ref[rows, BLOCK * p:BLOCK * (p + 1)] = sv.astype(BF16)
                u = cur_ref[rows, C_U + BLOCK * p:C_U + BLOCK * (p + 1)]
                g = cur_ref[rows, C_GG + BLOCK * p:C_GG + BLOCK * (p + 1)]
                y_ref[rows, D_ATTN + BLOCK * p:D_ATTN + BLOCK * (p + 1)] = ((u * sv) * (g * _sigmoid(g))).astype(BF16)

    cur, pkv, nkv, bias_specs = _mix_specs(nt)
    nb = nt * SUB
    half = pl.BlockSpec((TILE, D_ATTN), lambda i: (i, 0))
    return _pallas(
        body, name=name, grid=(nt,),
        in_specs=[pl.BlockSpec(memory_space=pltpu.SMEM), cur, pkv, nkv, *bias_specs, _full((1, BLOCK)), _full((1, BLOCK)),
                  _full((8, BLOCK, BLOCK)), _full((4, BLOCK, BLOCK))],
        out_specs=[pl.BlockSpec((TILE, D_MODEL), lambda i: (i, 0)), pl.BlockSpec((SUB * STACK, 3 * BLOCK), lambda i: (i, 0)),
                   pl.BlockSpec((SUB * N_HEADS, BLOCK), lambda i: (i, 0)), half, half],
        out_shape=[_sds((s, D_MODEL), BF16), _sds((nb * STACK, 3 * BLOCK), BF16), _sds((nb * N_HEADS, BLOCK)),
                   _sds((s, D_ATTN), BF16), _sds((s, D_ATTN), BF16)],
        operands=(sink, proj, proj, proj, *([bias] * SUB), qg2, kg2, ws, bsp), vmem_mib=56, jobs=jobs)


def _out_proj_fwd(y, x, gate, w_out, ng, scale, shift, wt, name):
    s = x.shape[0]
    ts = min(512, s)

    def body(y_ref, x_ref, g_ref, w_ref, ng_ref, sc_ref, sh_ref, wt_ref, xn_ref, p_ref):
        xv = x_ref[...] + g_ref[...] * jnp.dot(y_ref[...], w_ref[...], preferred_element_type=F32)
        xn_ref[...] = xv
        r = lax.rsqrt(jnp.mean(xv * xv, axis=-1, keepdims=True) + EPS)
        h = ((xv * r) * ng_ref[...]) * (1.0 + sc_ref[...]) + sh_ref[...]
        p_ref[...] = lax.dot_general(h.astype(BF16), wt_ref[...], NT_DIMS, preferred_element_type=F32)

    row = pl.BlockSpec((ts, D_MODEL), lambda i: (i, 0))
    vec = _full((1, D_MODEL))
    return _pallas(
        body, name=name, grid=(s // ts,),
        in_specs=[row, row, vec, _full((D_MODEL, D_MODEL)), vec, vec, vec, _full((D_IN, D_MODEL))],
        out_specs=[row, pl.BlockSpec((ts, D_IN), lambda i: (i, 0))], out_shape=[_sds((s, D_MODEL)), _sds((s, D_IN))],
        operands=(y, x, gate, w_out, ng, scale, shift, wt), vmem_mib=56)


def _out_loss_bwd(y, x, gate, w_out, target, name):
    s = x.shape[0]
    ts = min(512, s)
    steps = s // ts

    def body(y_ref, x_ref, g_ref, w_ref, t_ref, dx_ref, sq_ref, dy_ref, gw_ref, dg_ref):
        @pl.when(pl.program_id(0) == 0)
        def _():
            sq_ref[...] = jnp.zeros_like(sq_ref)
            gw_ref[...] = jnp.zeros_like(gw_ref)

        yv = y_ref[...]
        out = x_ref[...] + g_ref[...] * jnp.dot(yv, w_ref[...], preferred_element_type=F32)
        diff = out - t_ref[...]
        dx = diff * (1.0 / D_MODEL)
        dx_ref[...] = dx
        per_token = jnp.sum(diff * diff, axis=-1, keepdims=True) * (1.0 / D_MODEL)
        sq_ref[...] += jnp.sum(per_token, axis=0, keepdims=True)
        dy_ref[...] = lax.dot_general((dx * g_ref[...]).astype(BF16), w_ref[...], NT_DIMS, preferred_element_type=F32)
        gw_ref[...] += lax.dot_general(yv, dx.astype(BF16), TN_DIMS, preferred_element_type=F32)

        @pl.when(pl.program_id(0) == steps - 1)
        def _():
            m = gw_ref[...]
            dg_ref[...] = jnp.sum(w_ref[...].astype(F32) * m, axis=0, keepdims=True)
            gw_ref[...] = m * g_ref[...]

    row = pl.BlockSpec((ts, D_MODEL), lambda i: (i, 0))
    return _pallas(
        body, name=name, grid=(s // ts,), in_specs=[row, row, _full((1, D_MODEL)), _full((D_MODEL, D_MODEL)), row],
        out_specs=[row, _full((1, 1)), row, _full((D_MODEL, D_MODEL)), _full((1, D_MODEL))],
        out_shape=[_sds((s, D_MODEL)), _sds((1, 1)), _sds((s, D_MODEL)), _sds((D_MODEL, D_MODEL)), _sds((1, D_MODEL))],
        operands=(y, x, gate, w_out, target), vmem_mib=48)


def _out_bwd(dxo, y, gate, w_out, name, jobs=()):
    s = dxo.shape[0]
    ts = min(512, s)
    steps = s // ts

    def body(dx_ref, y_ref, g_ref, w_ref, dy_ref, gw_ref, dg_ref):
        @pl.when(pl.program_id(0) == 0)
        def _():
            gw_ref[...] = jnp.zeros_like(gw_ref)

        dx = dx_ref[...]
        dy_ref[...] = lax.dot_general((dx * g_ref[...]).astype(BF16), w_ref[...], NT_DIMS, preferred_element_type=F32)
        gw_ref[...] += lax.dot_general(y_ref[...], dx.astype(BF16), TN_DIMS, preferred_element_type=F32)

        @pl.when(pl.program_id(0) == steps - 1)
        def _():
            m = gw_ref[...]
            dg_ref[...] = jnp.sum(w_ref[...].astype(F32) * m, axis=0, keepdims=True)
            gw_ref[...] = m * g_ref[...]

    row = pl.BlockSpec((ts, D_MODEL), lambda i: (i, 0))
    return _pallas(
        body, name=name, grid=(s // ts,), in_specs=[row, row, _full((1, D_MODEL)), _full((D_MODEL, D_MODEL))],
        out_specs=[row, _full((D_MODEL, D_MODEL)), _full((1, D_MODEL))],
        out_shape=[_sds((s, D_MODEL)), _sds((D_MODEL, D_MODEL)), _sds((1, D_MODEL))],
        operands=(dxo, y, gate, w_out), vmem_mib=48, jobs=jobs)


def _mix_bwd(dy, proj, probs, psink, attn, sv, qg2, kg2, wst, name, jobs=()):
    s = proj.shape[0]
    nt = s // TILE

    def body(dy_ref, cur_ref, pkv_ref, nkv_ref, p_ref, ps_ref, attn_ref, sv_ref, qg_ref, kg_ref, wst_ref,
             dpb_ref, dkv_ref, p0_ref, p2_ref, dqg_ref, dkg_ref, dsink_ref, dws_ref, dbsp_ref):
        def put(rows, col, value):
            dpb_ref[rows, col:col + BLOCK] = value.astype(BF16)

        @pl.when(pl.program_id(0) == 0)
        def _():
            dqg_ref[...] = jnp.zeros_like(dqg_ref)
            dkg_ref[...] = jnp.zeros_like(dkg_ref)
            dsink_ref[...] = jnp.zeros_like(dsink_ref)
            dws_ref[...] = jnp.zeros_like(dws_ref)
            dbsp_ref[...] = jnp.zeros_like(dbsp_ref)

        lo = _lo_mask((BLOCK, BLOCK))
        lo_kv = _lo_mask((TILE + 2 * BLOCK, BLOCK))
        eye = _diagonal()
        lane_row = lax.broadcasted_iota(jnp.int32, (1, BLOCK), 1)
        qg = qg_ref[...]
        kg = kg_ref[...]

        k_all, v_all = _kv_rows(cur_ref, pkv_ref, nkv_ref)
        rk = _half_rms_scale(k_all, lo_kv)
        khat = k_all * rk
        kn_all = (khat * kg).astype(BF16)
        vb_all = v_all.astype(BF16)

        dkn_parts, dv_parts = [], []
        dsink = jnp.zeros((1, BLOCK), F32)
        dqg = jnp.zeros((1, BLOCK), F32)
        for b in range(SUB):
            rows = slice(BLOCK * b, BLOCK * (b + 1))
            window = slice(BLOCK * b, BLOCK * (b + 3))
            kn, vb = kn_all[window], vb_all[window]

            qhat, rq = [], []
            for p in range(4):
                q = cur_ref[rows, C_Q + BLOCK * p:C_Q + BLOCK * (p + 1)]
                r = _half_rms_scale(q, lo)
                rq.append(r)
                qhat.append(q * r)
            q_stack = _stack_heads([(qh * qg) * 0.125 for qh in qhat], lo).astype(BF16)
            pb = p_ref[STACK * b:STACK * (b + 1), :]
            prob = pb.astype(F32)

            dout = []
            for p in range(4):
                g = cur_ref[rows, C_GA + BLOCK * p:C_GA + BLOCK * (p + 1)]
                sg = _sigmoid(g)
                dya = dy_ref[rows, BLOCK * p:BLOCK * (p + 1)]
                attn = attn_ref[rows, BLOCK * p:BLOCK * (p + 1)]
                put(rows, C_GA + BLOCK * p, dya * attn * (sg * (1.0 + g * (1.0 - sg))))
                dout.append(dya * (g * sg))
            do_stack = _stack_heads(dout, lo).astype(BF16)
            dp = lax.dot_general(do_stack, vb, NT_DIMS, preferred_element_type=F32)
            delta = jnp.sum(prob * dp, axis=-1, keepdims=True)
            dsb = (prob * (dp - delta)).astype(BF16)

            for h in range(N_HEADS):
                delta_row = _column_as_row(jnp.broadcast_to(delta[BLOCK * h:BLOCK * (h + 1)], (BLOCK, BLOCK)), eye)
                tot = jnp.sum(ps_ref[N_HEADS * b + h:N_HEADS * b + h + 1, :] * delta_row, axis=-1, keepdims=True)
                dsink = dsink - jnp.where(lane_row == h, tot, 0.0)

            dq_stack = jnp.dot(dsb, kn, preferred_element_type=F32) * 0.125
            dkn_parts.append(lax.dot_general(dsb, q_stack, TN_DIMS, preferred_element_type=F32))
            dv_parts.append(lax.dot_general(pb, do_stack, TN_DIMS, preferred_element_type=F32))

            for p in range(4):
                dqn = _unstack_pair(dq_stack, p, lo)
                qh = qhat[p]
                dqg = dqg + jnp.sum(dqn * qh, axis=0, keepdims=True)
                dqh = dqn * qg
                mean = _half_sum(dqh * qh, lo) * (1.0 / HEAD_DIM)
                put(rows, C_Q + BLOCK * p, rq[p] * (dqh - qh * mean))

        dws_new, dbs_new = [], []
        for p in range(4):
            rs, vnfs, vns, dsvs, dbs = [], [], [], [], None
            for b in range(SUB):
                rows = slice(BLOCK * b, BLOCK * (b + 1))
                vg = cur_ref[rows, C_VG + BLOCK * p:C_VG + BLOCK * (p + 1)]
                r = _half_rms_scale(vg, lo)
                vnf = vg * r
                sv = sv_ref[rows, BLOCK * p:BLOCK * (p + 1)]
                u = cur_ref[rows, C_U + BLOCK * p:C_U + BLOCK * (p + 1)]
                g = cur_ref[rows, C_GG + BLOCK * p:C_GG + BLOCK * (p + 1)]
                sg = _sigmoid(g)
                dym = dy_ref[rows, D_ATTN + BLOCK * p:D_ATTN + BLOCK * (p + 1)]
                put(rows, C_GG + BLOCK * p, dym * (u * sv) * (sg * (1.0 + g * (1.0 - sg))))
                dgm = dym * (g * sg)
                put(rows, C_U + BLOCK * p, dgm * sv)
                dsv = dgm * u
                term = jnp.where(lo, jnp.sum(jnp.where(lo, dsv, 0.0), axis=-1, keepdims=True),
                                 jnp.sum(jnp.where(lo, 0.0, dsv), axis=-1, keepdims=True))
                dbs = term if dbs is None else dbs + term
                rs.append(r)
                vnfs.append(vnf)
                vns.append(vnf.astype(BF16))
                dsvs.append(dsv)
            dbs_new.append(dbs)
            vn = jnp.concatenate(vns, axis=1)
            dsv = jnp.concatenate(dsvs, axis=1)
            lo_t = (lax.broadcasted_iota(jnp.int32, dsv.shape, 1) & (BLOCK - 1)) < HEAD_DIM
            dws_new.append(lax.dot_general(jnp.where(lo_t, dsv, 0.0).astype(BF16), vn, NT_DIMS, preferred_element_type=F32))
            dws_new.append(lax.dot_general(jnp.where(lo_t, 0.0, dsv).astype(BF16), vn, NT_DIMS, preferred_element_type=F32))
            dsvb = dsv.astype(BF16)
            dvn_a = jnp.dot(wst_ref[2 * p], dsvb, preferred_element_type=F32)
            dvn_b = jnp.dot(wst_ref[2 * p + 1], dsvb, preferred_element_type=F32)
            for b in range(SUB):
                lanes = slice(BLOCK * b, BLOCK * (b + 1))
                dvn = jnp.where(lo, dvn_a[:, lanes], dvn_b[:, lanes])
                mean = _half_sum(dvn * vnfs[b], lo) * (1.0 / HEAD_DIM)
                put(slice(BLOCK * b, BLOCK * (b + 1)), C_VG + BLOCK * p, rs[b] * (dvn - vnfs[b] * mean))

        dsink_ref[...] += dsink
        dqg = jnp.broadcast_to(dqg, (8, BLOCK))
        dqg_ref[...] += dqg + pltpu.roll(dqg, HEAD_DIM, 1)

        dkn = _overlap_add(dkn_parts)
        dv = _overlap_add(dv_parts)
        dkg = jnp.broadcast_to(jnp.sum(dkn * khat, axis=0, keepdims=True), (8, BLOCK))
        dkg_ref[...] += dkg + pltpu.roll(dkg, HEAD_DIM, 1)
        dkh = dkn * kg
        dk = rk * (dkh - khat * (_half_sum(dkh * khat, lo_kv) * (1.0 / HEAD_DIM)))
        dpb_ref[:, C_K:C_GA] = jnp.zeros((TILE, 2 * D_KV), BF16)
        dkv_ref[:, 0:D_KV] = dk[BLOCK:BLOCK + TILE]
        dkv_ref[:, D_KV:2 * D_KV] = dv[BLOCK:BLOCK + TILE]
        p0_ref[:, 0:D_KV] = dk[0:BLOCK]
        p0_ref[:, D_KV:2 * D_KV] = dv[0:BLOCK]
        p2_ref[:, 0:D_KV] = dk[BLOCK + TILE:]
        p2_ref[:, D_KV:2 * D_KV] = dv[BLOCK + TILE:]
        for g, new in enumerate(dws_new):
            dws_ref[g] += new
        for p, new in enumerate(dbs_new):
            dbsp_ref[p] += new

    cur, pkv, nkv, _ = _mix_specs(nt)
    kv_blk = (BLOCK, 2 * D_KV)
    half = pl.BlockSpec((TILE, D_ATTN), lambda i: (i, 0))
    return _pallas(
        body, name=name, grid=(nt,),
        in_specs=[pl.BlockSpec((TILE, D_MODEL), lambda i: (i, 0)), cur, pkv, nkv,
                  pl.BlockSpec((SUB * STACK, 3 * BLOCK), lambda i: (i, 0)), pl.BlockSpec((SUB * N_HEADS, BLOCK), lambda i: (i, 0)),
                  half, half, _full((1, BLOCK)), _full((1, BLOCK)), _full((8, BLOCK, BLOCK))],
        out_specs=[cur, pl.BlockSpec((TILE, 2 * D_KV), lambda i: (i, 0)),
                   pl.BlockSpec(kv_blk, lambda i: ((i + nt - 1) % nt, 0)),
                   pl.BlockSpec(kv_blk, lambda i: ((i + 1) % nt, 0)),
                   _full((8, BLOCK)), _full((8, BLOCK)), _full((1, BLOCK)),
                   _full((8, BLOCK, BLOCK)), _full((4, BLOCK, BLOCK))],
        out_shape=[_sds((s, D_IN), BF16), _sds((s, 2 * D_KV)), _sds((nt * BLOCK, 2 * D_KV)), _sds((nt * BLOCK, 2 * D_KV)),
                   _sds((8, BLOCK)), _sds((8, BLOCK)), _sds((1, BLOCK)),
                   _sds((8, BLOCK, BLOCK)), _sds((4, BLOCK, BLOCK))],
        operands=(dy, proj, proj, proj, probs, psink, attn, sv, qg2, kg2, wst), vmem_mib=56, jobs=jobs)


def _w_in_grad(dpb, dkv, p0, p2, x, ng, scale, shift, name, jobs=()):
    s = x.shape[0]
    ts = min(2 * TILE, s)
    tiles = ts // TILE

    def body(dpb_ref, dkv_ref, p0_ref, p2_ref, x_ref, ng_ref, sc_ref, sh_ref, gw_ref, dkvb_ref):
        @pl.when(pl.program_id(0) == 0)
        def _():
            gw_ref[...] = jnp.zeros_like(gw_ref)

        xv = x_ref[...]
        r = lax.rsqrt(jnp.mean(xv * xv, axis=-1, keepdims=True) + EPS)
        h = (((xv * r) * ng_ref[...]) * (1.0 + sc_ref[...]) + sh_ref[...]).astype(BF16)
        for t in range(tiles):
            halo = slice(BLOCK * t, BLOCK * (t + 1))
            first = slice(TILE * t, TILE * t + BLOCK)
            last = slice(TILE * (t + 1) - BLOCK, TILE * (t + 1))
            dkvb_ref[first, :] = (dkv_ref[first, :] + p2_ref[halo, :]).astype(BF16)
            if SUB > 2:
                inner = slice(TILE * t + BLOCK, TILE * (t + 1) - BLOCK)
                dkvb_ref[inner, :] = dkv_ref[inner, :].astype(BF16)
            dkvb_ref[last, :] = (dkv_ref[last, :] + p0_ref[halo, :]).astype(BF16)
        gw_ref[...] += lax.dot_general(dpb_ref[...], h, TN_DIMS, preferred_element_type=F32)
        gw_ref[C_K:C_GA, :] += lax.dot_general(dkvb_ref[...], h, TN_DIMS, preferred_element_type=F32)

    kv = pl.BlockSpec((ts, 2 * D_KV), lambda i: (i, 0))
    halo = pl.BlockSpec((tiles * BLOCK, 2 * D_KV), lambda i: (i, 0))
    vec = _full((1, D_MODEL))
    return _pallas(
        body, name=name, grid=(s // ts,),
        in_specs=[pl.BlockSpec((ts, D_IN), lambda i: (i, 0)), kv, halo, halo,
                  pl.BlockSpec((ts, D_MODEL), lambda i: (i, 0)), vec, vec, vec],
        out_specs=[_full((D_IN, D_MODEL)), kv], out_shape=[_sds((D_IN, D_MODEL)), _sds((s, 2 * D_KV), BF16)],
        operands=(dpb, dkv, p0, p2, x, ng, scale, shift), vmem_mib=56, jobs=jobs)


def _proj_bwd(dpb, dkvb, x, dxo, ng, scale, wt, name, jobs=()):
    s = x.shape[0]
    ts = min(512, s)

    def body(dpb_ref, dkvb_ref, x_ref, dxo_ref, ng_ref, sc_ref, w_ref, dxi_ref, dsh_ref, dsc_ref, dng_ref):
        @pl.when(pl.program_id(0) == 0)
        def _():
            dsh_ref[...] = jnp.zeros_like(dsh_ref)
            dsc_ref[...] = jnp.zeros_like(dsc_ref)
            dng_ref[...] = jnp.zeros_like(dng_ref)

        dh = (jnp.dot(dpb_ref[...], w_ref[...], preferred_element_type=F32)
              + jnp.dot(dkvb_ref[...], w_ref[C_K:C_GA, :], preferred_element_type=F32))

        xv = x_ref[...]
        r = lax.rsqrt(jnp.mean(xv * xv, axis=-1, keepdims=True) + EPS)
        xn = xv * r
        ngv = ng_ref[...]
        sc1 = 1.0 + sc_ref[...]
        dsh_ref[...] += jnp.sum(dh, axis=0, keepdims=True)
        dsc_ref[...] += jnp.sum(dh * (xn * ngv), axis=0, keepdims=True)
        dh1 = dh * sc1
        dng_ref[...] += jnp.sum(dh1 * xn, axis=0, keepdims=True)
        dxn = dh1 * ngv
        dxi_ref[...] = r * (dxn - xn * jnp.mean(dxn * xn, axis=-1, keepdims=True)) + dxo_ref[...]

    row = pl.BlockSpec((ts, D_MODEL), lambda i: (i, 0))
    vec = _full((1, D_MODEL))
    return _pallas(
        body, name=name, grid=(s // ts,),
        in_specs=[pl.BlockSpec((ts, D_IN), lambda i: (i, 0)), pl.BlockSpec((ts, 2 * D_KV), lambda i: (i, 0)),
                  row, row, vec, vec, _full((D_IN, D_MODEL))],
        out_specs=[row, vec, vec, vec],
        out_shape=[_sds((s, D_MODEL)), _sds((1, D_MODEL)), _sds((1, D_MODEL)), _sds((1, D_MODEL))],
        operands=(dpb, dkvb, x, dxo, ng, scale, wt), vmem_mib=48, jobs=jobs)


def _pair_sum(g, r, c_idx, name, send_dtype=None):
    _, rows, cols = g.shape
    half = rows // 2

    def body(c_ref, g_ref, r_ref, o_ref, *narrow):
        total = g_ref[...] + r_ref[...]
        o_ref[...] = total
        for n_ref in narrow:
            n_ref[...] = total.astype(n_ref.dtype)

    blk = (None, half, cols)
    out_blk = pl.BlockSpec(blk, lambda j, c: (j, 0, 0))
    shapes = [_sds((N_CHIPS, half, cols))] + ([_sds((N_CHIPS, half, cols), send_dtype)] if send_dtype else [])
    return pl.pallas_call(
        body, name=name,
        grid_spec=pltpu.PrefetchScalarGridSpec(
            num_scalar_prefetch=1, grid=(N_CHIPS,),
            in_specs=[pl.BlockSpec(blk, lambda j, c: (j, c[0], 0)), out_blk], out_specs=[out_blk] * len(shapes)),
        out_shape=shapes,
        compiler_params=pltpu.CompilerParams(dimension_semantics=("arbitrary",), vmem_limit_bytes=32 * MIB),
    )(*_in_hbm(c_idx, g, r))


def _chip_sum(p, r, place, name):
    _, rows, cols = p.shape
    tr = rows // 2

    def body(j_ref, p_ref, r_ref, o_ref):
        o_ref[...] = ((p_ref[...] + r_ref[0].astype(F32)) + r_ref[1].astype(F32)) + r_ref[2].astype(F32)

    return pl.pallas_call(
        body, name=name,
        grid_spec=pltpu.PrefetchScalarGridSpec(
            num_scalar_prefetch=1, grid=(2,),
            in_specs=[pl.BlockSpec((None, tr, cols), lambda t, j: (j[0], t, 0)),
                      pl.BlockSpec((3, tr, cols), lambda t, j: (0, t, 0))],
            out_specs=pl.BlockSpec((tr, cols), lambda t, j: (2 * j[1] + t, 0))),
        out_shape=_sds((2 * rows, cols)),
        compiler_params=pltpu.CompilerParams(dimension_semantics=("arbitrary",), vmem_limit_bytes=32 * MIB),
    )(*_in_hbm(place, p, r))


def _cast_permute_w_in(wt, place_chunks):
    def body(t_ref, w_ref, w0_ref, w1_ref):
        def cast_into(o_ref):
            for t in range(N_CHUNKS):
                src = pl.multiple_of(t_ref[1 + t] * CHUNK_ROWS, CHUNK_ROWS)
                o_ref[CHUNK_ROWS * t:CHUNK_ROWS * (t + 1), :] = w_ref[pl.ds(src, CHUNK_ROWS), :].astype(BF16)

        @pl.when(pl.program_id(0) == 0)
        def _():
            cast_into(w0_ref)

        @pl.when(pl.program_id(0) == 1)
        def _():
            cast_into(w1_ref)

    return pl.pallas_call(
        body, name="cast_permute_w_in",
        grid_spec=pltpu.PrefetchScalarGridSpec(
            num_scalar_prefetch=1, grid=(DEPTH,),
            in_specs=[pl.BlockSpec((None, W_IN_BLK, D_MODEL), lambda l, tbl: (l, 0, 0))],
            out_specs=[pl.BlockSpec((None, W_IN_BLK, D_MODEL), lambda l, tbl: (tbl[0], 0, 0))] * DEPTH),
        out_shape=[_sds((N_CHIPS, W_IN_BLK, D_MODEL), BF16)] * DEPTH,
        compiler_params=pltpu.CompilerParams(dimension_semantics=("arbitrary",), vmem_limit_bytes=32 * MIB),
    )(*_in_hbm(place_chunks, wt))


def _gather_inputs(c, w_out, w0):
    half = W_IN_BLK // 2

    def body(c_ref, wout_ref, mine_ref, call_ref, woutb_ref, w0_ref, send_sems, recv_sems):
        x, y, cc = _coords()
        j = 2 * x + y
        b = 2 * j + cc
        sib = (x, y, 1 - cc)
        woutb_ref[...] = wout_ref[...].astype(BF16)
        call_ref[b] = c_ref[...]
        chips = _other_chips(x, y)

        def sems(k):
            return send_sems.at[k], recv_sems.at[k]

        def half_rows(chip_index):
            return w0_ref.at[chip_index, pl.ds(cc * half, half), :]

        first = [_remote(mine_ref.at[j, pl.ds(cc * half, half), :], half_rows(j), sems(k), (*chip, cc))
                 for k, chip in enumerate(chips)]
        k = 3
        rest = []
        for fx in (0, 1):
            for fy in (0, 1):
                for fc in (0, 1):
                    if fx or fy or fc:
                        dev = (1 - x if fx else x, 1 - y if fy else y, 1 - cc if fc else cc)
                        rest.append(_remote(call_ref.at[b], call_ref.at[b], sems(k), dev))
                        k += 1
        for cp in first + rest:
            cp.start()
        passed = []
        for k, chip in enumerate(chips):
            jk = 2 * chip[0] + chip[1]
            first[k].wait_recv()
            passed.append(_remote(half_rows(jk), half_rows(jk), sems(10 + k), sib))
            passed[k].start()
        for cp in first:
            cp.wait_send()
        for cp in rest + passed:
            cp.wait()

    return pl.pallas_call(
        body, name="gather_inputs", in_specs=[VMEM, VMEM, ANY], out_specs=[VMEM, VMEM, ANY],
        out_shape=[_sds((N_DEV, 1, D_MODEL)), _sds((DEPTH, W_OUT_BLK, D_MODEL), BF16),
                   _sds((N_CHIPS, W_IN_BLK, D_MODEL), BF16)],
        scratch_shapes=[pltpu.SemaphoreType.DMA((13,)), pltpu.SemaphoreType.DMA((13,))],
        input_output_aliases={2: 2},
        compiler_params=pltpu.CompilerParams(vmem_limit_bytes=32 * MIB),
    )(c, w_out, w0)


def _ada_rows(c_all, w_ada, b_blk):
    def body(c_ref, w_ref, b_ref, o_ref, cond_ref):
        cv = c_ref[...]
        cond = (cv * _sigmoid(cv)).astype(BF16)
        cond_ref[...] = cond.astype(F32)
        for l in range(DEPTH):
            o_ref[:, l, :] = jnp.dot(cond, w_ref[l].astype(BF16), preferred_element_type=F32) + b_ref[l:l + 1, :]

    return pl.pallas_call(
        body, name="ada_rows", in_specs=[VMEM, VMEM, VMEM], out_specs=[VMEM, VMEM],
        out_shape=[_sds((N_DEV, DEPTH, W_ADA_BLK)), _sds((N_DEV, D_MODEL))],
        compiler_params=pltpu.CompilerParams(vmem_limit_bytes=32 * MIB),
    )(c_all, w_ada, b_blk)


def _exchange_ada(part):
    def body(part_ref, out_ref, send_sems, recv_sems):
        x, y, cc = _coords()
        j = 2 * x + y
        out_ref[j] = part_ref[2 * j + cc]
        copies = []
        for k, chip in enumerate(_other_chips(x, y)):
            b_dst = 4 * chip[0] + 2 * chip[1] + cc
            copies.append(_remote(part_ref.at[b_dst], out_ref.at[j], (send_sems.at[k], recv_sems.at[k]), (*chip, cc)))
        for cp in copies:
            cp.start()
        for cp in copies:
            cp.wait()

    return pl.pallas_call(
        body, name="exchange_ada", in_specs=[VMEM], out_specs=VMEM,
        out_shape=_sds((N_CHIPS, DEPTH, W_ADA_BLK)),
        scratch_shapes=[pltpu.SemaphoreType.DMA((3,)), pltpu.SemaphoreType.DMA((3,))],
    )(part)


def _adamw_math(w, g, m, v):
    m = ADAM_B1 * m + (1.0 - ADAM_B1) * g
    v = ADAM_B2 * v + (1.0 - ADAM_B2) * (g * g)
    m_hat = m / (1.0 - ADAM_B1 ** ADAM_STEP)
    v_hat = v / (1.0 - ADAM_B2 ** ADAM_STEP)
    delta = -ADAM_LR * (m_hat / (jnp.sqrt(v_hat) + ADAM_EPS) + ADAM_WD * w)
    return delta, m, v


def _adamw_w_in(w, g0, g1, m, v, pos_chunks):
    def body(t_ref, w_ref, g0_ref, g1_ref, m_ref, v_ref, g_ref, d_ref, nm_ref, nv_ref):
        for l, src in enumerate((g0_ref, g1_ref)):
            g = src[...]
            g_ref[l] = g
            d_ref[l], nm_ref[l], nv_ref[l] = _adamw_math(w_ref[l], g, m_ref[l], v_ref[l])

    nat = pl.BlockSpec((DEPTH, CHUNK_ROWS, D_MODEL), lambda t, tbl: (0, t, 0))
    per = pl.BlockSpec((CHUNK_ROWS, D_MODEL), lambda t, tbl: (tbl[t], 0))
    return pl.pallas_call(
        body, name="adamw_w_in",
        grid_spec=pltpu.PrefetchScalarGridSpec(num_scalar_prefetch=1, grid=(N_CHUNKS,),
                                               in_specs=[nat, per, per, nat, nat], out_specs=[nat] * 4),
        out_shape=[_sds(w.shape)] * 4,
        compiler_params=pltpu.CompilerParams(dimension_semantics=("arbitrary",)),
    )(*_in_hbm(pos_chunks, w, g0, g1, m, v))


def _adamw_w_out(w, g0, g1, m, v):
    def body(w_ref, g0_ref, g1_ref, m_ref, v_ref, g_ref, d_ref, nm_ref, nv_ref):
        g = jnp.where(pl.program_id(0) == 0, g0_ref[...], g1_ref[...])
        g_ref[...] = g
        d_ref[...], nm_ref[...], nv_ref[...] = _adamw_math(w_ref[...], g, m_ref[...], v_ref[...])

    blk = pl.BlockSpec((None, W_OUT_BLK, D_MODEL), lambda l: (l, 0, 0))
    gblk = _full((W_OUT_BLK, D_MODEL))
    return pl.pallas_call(
        body, name="adamw_w_out", grid=(DEPTH,), in_specs=[blk, gblk, gblk, blk, blk], out_specs=[blk] * 4,
        out_shape=[_sds(w.shape)] * 4,
        compiler_params=pltpu.CompilerParams(dimension_semantics=("arbitrary",), vmem_limit_bytes=32 * MIB),
    )(w, g0, g1, m, v)


def _w_ada_grad_adamw(cond_t, dada, w, m, v):
    _, rows, cols = w.shape
    tr = 256

    def body(ct_ref, da_ref, w_ref, m_ref, v_ref, g_ref, d_ref, nm_ref, nv_ref):
        g = jnp.dot(ct_ref[...], da_ref[...].astype(BF16), preferred_element_type=F32)
        g_ref[...] = g
        d_ref[...], nm_ref[...], nv_ref[...] = _adamw_math(w_ref[...], g, m_ref[...], v_ref[...])

    blk = pl.BlockSpec((None, tr, cols), lambda l, t: (l, t, 0))
    return pl.pallas_call(
        body, name="w_ada_grad_adamw", grid=(DEPTH, rows // tr),
        in_specs=[pl.BlockSpec((tr, BLOCK), lambda l, t: (t, 0)), pl.BlockSpec((None, BLOCK, cols), lambda l, t: (l, 0, 0)),
                  blk, blk, blk],
        out_specs=[blk] * 4, out_shape=[_sds(w.shape)] * 4,
        compiler_params=pltpu.CompilerParams(dimension_semantics=("arbitrary", "arbitrary"), vmem_limit_bytes=32 * MIB),
    )(cond_t, dada, w, m, v)


def _small_sum_adamw(gathered, w, m, v):
    n = len(gathered)

    def body(*refs):
        w_ref, m_ref, v_ref, g_ref, d_ref, nm_ref, nv_ref = refs[n:]

        def total(ref):
            g = ref[0]
            for b in range(1, N_DEV):
                g = g + ref[b]
            return g

        g = jnp.concatenate([total(ref) for ref in refs[:n]], axis=0)
        g_ref[...] = g
        d_ref[...], nm_ref[...], nv_ref[...] = _adamw_math(w_ref[...], g, m_ref[...], v_ref[...])

    return pl.pallas_call(
        body, name="small_sum_adamw", in_specs=[VMEM] * (n + 3), out_specs=[VMEM] * 4, out_shape=[_sds(w.shape)] * 4,
        compiler_params=pltpu.CompilerParams(vmem_limit_bytes=48 * MIB),
    )(*gathered, w, m, v)


_SMALL_A = (("w_s", 8 * BLOCK), ("b_s", 8), ("q_gain", 1), ("k_gain", 1), ("sink", 1))
_SMALL_B = (("b_ada", DEPTH * 24), ("norm_gain", DEPTH * 8), ("sq_err", 1))


def _pack_rows(parts, layout, layer=None):
    rows = []
    for name, n in layout:
        flat = (parts[name] if layer is None else parts[name][layer]).reshape(-1)
        rows.append(jnp.pad(flat, (0, n * 128 - flat.shape[0])).reshape(n, 128))
    n_rows = sum(n for _, n in layout)
    if n_rows % 8:
        rows.append(jnp.zeros((-n_rows % 8, 128), F32))
    return jnp.concatenate(rows, axis=0)


def _pack_small(parts):
    return jnp.concatenate([_pack_rows(parts, _SMALL_A, l) for l in range(DEPTH)] + [_pack_rows(parts, _SMALL_B)], axis=0)


def _unpack_small(packed, shapes):
    def take(r0, layout, shape_of):
        got = {}
        for name, n in layout:
            shape = shape_of(name)
            size = 1
            for d in shape:
                size *= d
            got[name] = packed[r0:r0 + n].reshape(-1)[:size].reshape(shape)
            r0 += n
        return got, r0 + -r0 % 8

    layers, r0 = [], 0
    for _ in range(DEPTH):
        got, r0 = take(r0, _SMALL_A, lambda name: shapes[name][1:])
        layers.append(got)
    out, _ = take(r0, _SMALL_B, lambda name: shapes[name])
    out.update({name: jnp.stack([layer[name] for layer in layers]) for name, _ in _SMALL_A})
    return out


def _permute_heads(a, axis):
    shp = a.shape
    a = a.reshape(shp[:axis] + (2, 4, HEAD_DIM) + shp[axis + 1:])
    a = jnp.swapaxes(a, axis, axis + 1)
    return a.reshape(shp)


def _unpermute_heads(a, axis):
    shp = a.shape
    a = a.reshape(shp[:axis] + (4, 2, HEAD_DIM) + shp[axis + 1:])
    a = jnp.swapaxes(a, axis, axis + 1)
    return a.reshape(shp)


def _permute_w_out(w):
    return jnp.concatenate([_permute_heads(w[:D_ATTN], 0), w[D_ATTN:]], axis=0)


def _unpermute_w_out(w):
    return jnp.concatenate([_unpermute_heads(w[:D_ATTN], 0), w[D_ATTN:]], axis=0)


def kernel(x, c, w_ada, b_ada, norm_gain, w_in, q_gain, k_gain, sink, w_s, b_s, w_out, loss_target, m_w_ada, m_b_ada, m_norm_gain, m_w_in, m_q_gain, m_k_gain, m_sink, m_w_s, m_b_s, m_w_out, v_w_ada, v_b_ada, v_norm_gain, v_w_in, v_q_gain, v_k_gain, v_sink, v_w_s, v_b_s, v_w_out):
    ix, iy, ic = _coords()
    chip = 2 * ix + iy
    chip_idx = jnp.stack([chip, ic]).astype(jnp.int32)
    core_idx = jnp.reshape(ic, (1,)).astype(jnp.int32)
    src_chunks = lax.dynamic_index_in_dim(jnp.asarray(_CHUNK_SRC), chip, 0, keepdims=False)
    pos_chunks = lax.dynamic_index_in_dim(jnp.asarray(_CHUNK_POS), chip, 0, keepdims=False)
    x0, target = x[0], loss_target[0]

    wt, mt, vt = (jnp.swapaxes(a, 1, 2) for a in (w_in, m_w_in, v_w_in))
    w0_mine, w1_mine = _cast_permute_w_in(wt, jnp.concatenate([chip_idx[:1], src_chunks]))
    c_all, wloc_out, w0 = _gather_inputs(c, w_out, w0_mine)
    wts = [w0.reshape(D_IN, D_MODEL), None]

    b_blk = lax.dynamic_slice_in_dim(b_ada, chip * W_ADA_BLK, W_ADA_BLK, axis=1)
    ada_part, cond = _ada_rows(c_all.reshape(N_DEV, D_MODEL), w_ada, b_blk)
    ada = jnp.moveaxis(_exchange_ada(ada_part), 0, 1).reshape(DEPTH, 3 * D_MODEL)
    shift = [ada[l:l + 1, 0:D_MODEL] for l in range(DEPTH)]
    scale = [ada[l:l + 1, D_MODEL:2 * D_MODEL] for l in range(DEPTH)]
    gate = [ada[l:l + 1, 2 * D_MODEL:] for l in range(DEPTH)]
    ng = [norm_gain[l:l + 1] for l in range(DEPTH)]

    qg2 = jnp.concatenate([q_gain, q_gain], axis=-1)
    kg2 = jnp.concatenate([k_gain, k_gain], axis=-1)
    ws_b = w_s.astype(BF16)
    wst_b = jnp.swapaxes(w_s, -1, -2).astype(BF16)
    bsp = jnp.repeat(jnp.swapaxes(b_s.reshape(DEPTH, 4, 2, BLOCK), -1, -2), HEAD_DIM, axis=-1)
    bias = jnp.asarray(_bias_table())

    def mix_args(l):
        return bias, sink[l], qg2[l:l + 1], kg2[l:l + 1], ws_b[l]

    w_out_shape = (W_OUT_BLK, D_MODEL)
    proj0, wo0, wo1 = _proj_fwd(x0, ng[0], scale[0], shift[0], wts[0], "proj_fwd_0",
                                jobs=[_job_gather([(wloc_out, 0), (wloc_out, 1)], [w_out_shape, w_out_shape])])
    y0, *kept0, w1 = _mix_fwd(proj0, *mix_args(0), bsp[0], "mix_fwd_0",
                              jobs=[_job_gather_once(w1_mine)])
    wts[1] = w1.reshape(D_IN, D_MODEL)
    wos = [_permute_w_out(w.reshape(D_MODEL, D_MODEL)) for w in (wo0, wo1)]
    x1, proj1 = _out_proj_fwd(y0, x0, gate[0], wos[0], ng[1], scale[1], shift[1], wts[1], "out_proj_fwd_01")
    y1, *kept1 = _mix_fwd(proj1, *mix_args(1), bsp[1], "mix_fwd_1")

    def blocks_out(gw):
        return _unpermute_w_out(gw).reshape(N_CHIPS, W_OUT_BLK, D_MODEL)

    def small_pack(dws, dbsp, dqg, dkg, dsink):
        b_s = jnp.swapaxes(dbsp[:, :, ::HEAD_DIM], -1, -2).reshape(8, BLOCK)
        return _pack_rows(dict(w_s=dws, b_s=b_s, q_gain=dqg[0, :HEAD_DIM], k_gain=dkg[0, :HEAD_DIM], sink=dsink[0, :N_HEADS]),
                          _SMALL_A)

    dx2, sq, dy1, gwo1, dgate1 = _out_loss_bwd(y1, x1, gate[1], wos[1], target, "out_loss_bwd_1")
    go1 = blocks_out(gwo1)
    dpb, dkv, p0, p2, dqg1, dkg1, dsink1, dws1, dbsp1, ro1 = _mix_bwd(
        dy1, proj1, *kept1, qg2[1:2], kg2[1:2], wst_b[1], "mix_bwd_1", jobs=[_job_swap(go1)])
    po1, = _pair_sum(go1, ro1, core_idx, "pair_sum_w_out_1")
    gwi1, dkvb, co1, gathered_a1 = _w_in_grad(
        dpb, dkv, p0, p2, x1, ng[1], scale[1], shift[1], "w_in_grad_1",
        jobs=[_job_scatter(po1), _job_all_gather(small_pack(dws1, dbsp1, dqg1, dkg1, dsink1))])
    gi1 = gwi1.reshape(N_CHIPS, W_IN_BLK, D_MODEL)
    fo1 = _chip_sum(po1, co1, chip_idx, "chip_sum_w_out_1")
    dx1, dsh1, dsc1, dng1, grad_wo1, ri1 = _proj_bwd(dpb, dkvb, x1, dx2, ng[1], scale[1], wts[1], "proj_bwd_1",
                                                     jobs=[_job_join(fo1), _job_swap(gi1)])
    pi1, = _pair_sum(gi1, ri1, core_idx, "pair_sum_w_in_1")

    dy0, gwo0, dgate0 = _out_bwd(dx1, y0, gate[0], wos[0], "out_bwd_0")
    go0 = blocks_out(gwo0)
    dpb, dkv, p0, p2, dqg0, dkg0, dsink0, dws0, dbsp0, ci1, ro0 = _mix_bwd(
        dy0, proj0, *kept0, qg2[0:1], kg2[0:1], wst_b[0], "mix_bwd_0", jobs=[_job_scatter(pi1), _job_swap(go0)])
    fi1 = _chip_sum(pi1, ci1, chip_idx, "chip_sum_w_in_1")
    po0, = _pair_sum(go0, ro0, core_idx, "pair_sum_w_out_0")

    gwi0, dkvb, grad_wi1, co0, gathered_a0 = _w_in_grad(
        dpb, dkv, p0, p2, x0, ng[0], scale[0], shift[0], "w_in_grad_0",
        jobs=[_job_join(fi1), _job_scatter(po0), _job_all_gather(small_pack(dws0, dbsp0, dqg0, dkg0, dsink0))])
    gi0 = gwi0.reshape(N_CHIPS, W_IN_BLK, D_MODEL)
    fo0 = _chip_sum(po0, co0, chip_idx, "chip_sum_w_out_0")

    ri0, grad_wo0 = _comm([_job_swap(gi0), _job_join(fo0)], "swap_w_in_0")
    pi0, pi0_send = _pair_sum(gi0, ri0, core_idx, "pair_sum_w_in_0", send_dtype=BF16)
    dx0, dsh0, dsc0, dng0, ci0 = _proj_bwd(dpb, dkvb, x0, dx1, ng[0], scale[0], wts[0], "proj_bwd_0",
                                           jobs=[_job_scatter(pi0_send)])
    fi0 = _chip_sum(pi0, ci0, chip_idx, "chip_sum_w_in_0")

    small_g = dict(
        b_ada=jnp.stack([jnp.concatenate([dsh0, dsc0, dgate0], axis=-1)[0], jnp.concatenate([dsh1, dsc1, dgate1], axis=-1)[0]]),
        norm_gain=jnp.stack([dng0[0], dng1[0]]), sq_err=sq[0])
    none = jnp.zeros((1,), F32)
    small_w = dict(w_s=w_s, b_s=b_s, b_ada=b_ada, norm_gain=norm_gain, q_gain=q_gain, k_gain=k_gain, sink=sink, sq_err=none)
    small_m = dict(w_s=m_w_s, b_s=m_b_s, b_ada=m_b_ada, norm_gain=m_norm_gain, q_gain=m_q_gain, k_gain=m_k_gain, sink=m_sink,
                   sq_err=none)
    small_v = dict(w_s=v_w_s, b_s=v_b_s, b_ada=v_b_ada, norm_gain=v_norm_gain, q_gain=v_q_gain, k_gain=v_k_gain, sink=v_sink,
                   sq_err=none)
    grad_wi0, gathered_b = _comm([_job_join(fi0), _job_all_gather(_pack_rows(small_g, _SMALL_B))], "join_w_in_0")
    packed = _small_sum_adamw([a.reshape(N_DEV, -1, 128) for a in (gathered_a0, gathered_a1, gathered_b)],
                              _pack_small(small_w), _pack_small(small_m), _pack_small(small_v))
    shapes = {k: a.shape for k, a in small_w.items()}
    sg, sd, sm, sv = (_unpack_small(p, shapes) for p in packed)
    loss = 0.5 * sg["sq_err"][0]

    dada_all = gathered_b.reshape(N_DEV, -1, 128)[:, 0:DEPTH * 24].reshape(N_DEV, DEPTH, 3 * D_MODEL)
    dada_blk = jnp.moveaxis(lax.dynamic_slice_in_dim(dada_all, chip * W_ADA_BLK, W_ADA_BLK, axis=2), 0, 1)
    pad = BLOCK - N_DEV
    ada_out = _w_ada_grad_adamw(
        jnp.pad(cond.T, ((0, 0), (0, pad))).astype(BF16), jnp.pad(dada_blk, ((0, 0), (0, pad), (0, 0))),
        w_ada, m_w_ada, v_w_ada)

    in_out = [jnp.swapaxes(a, 1, 2) for a in _adamw_w_in(wt, grad_wi0, grad_wi1, mt, vt, pos_chunks)]
    out_out = _adamw_w_out(w_out, grad_wo0, grad_wo1, m_w_out, v_w_out)

    def ordered(k):
        small = (sg, sd, sm, sv)[k]
        return (ada_out[k], small["b_ada"], small["norm_gain"], in_out[k], small["q_gain"], small["k_gain"], small["sink"],
                small["w_s"], small["b_s"], out_out[k])

    return (loss, dx0[None], *ordered(0), *ordered(1), *ordered(2), *ordered(3))
```

```python
import numpy as np

import jax
import jax.numpy as jnp
from jax import lax
from jax.experimental import pallas as pl
from jax.experimental.pallas import tpu as pltpu

F32 = jnp.float32
BF16 = jnp.bfloat16

D_MODEL = 1024
DEPTH = 2
HEAD_DIM = 64
N_HEADS = 8
BLOCK = 128
SUB = 4
TILE = SUB * BLOCK
STACK = 8 * BLOCK
D_ATTN = 512
D_KV = 128
D_IN = 2816
N_CHIPS = 4
N_DEV = 8
W_IN_BLK = D_IN // N_CHIPS
W_OUT_BLK = D_MODEL // N_CHIPS
W_ADA_BLK = 3 * D_MODEL // N_CHIPS
CHUNK_ROWS = HEAD_DIM
N_CHUNKS = W_IN_BLK // CHUNK_ROWS
EPS = 1e-6
NEG_INF = -1e30

C_Q, C_K, C_V, C_GA, C_U, C_VG, C_GG = 0, 512, 640, 768, 1280, 1792, 2304

ADAM_LR = 0.001
ADAM_B1 = 0.9
ADAM_B2 = 0.999
ADAM_EPS = 1e-08
ADAM_WD = 0.01
ADAM_STEP = 10

MESH = pl.DeviceIdType.MESH
MIB = 1024 * 1024
ANY = pl.BlockSpec(memory_space=pl.ANY)
VMEM = pl.BlockSpec(memory_space=pltpu.VMEM)

NT_DIMS = (((1,), (1,)), ((), ()))
TN_DIMS = (((0,), (0,)), ((), ()))

_PAIR_ORDER = (0, 4, 1, 5, 2, 6, 3, 7)
_CHUNK_SRC = np.array([
    list(_PAIR_ORDER) + [8, 9, 10],
    [0] + [1 + h for h in _PAIR_ORDER] + [9, 10],
    list(range(N_CHUNKS)),
    list(range(N_CHUNKS)),
], np.int32)
_CHUNK_POS = np.argsort(_CHUNK_SRC, axis=1).astype(np.int32)


def _bias_table():
    i = np.arange(N_HEADS * BLOCK)[:, None]
    j = np.arange(3 * BLOCK)[None, :]
    dist = np.abs(j - BLOCK - (i % BLOCK))
    slope = 2.0 ** -(i // BLOCK + 1.0)
    inner = np.where(dist <= BLOCK, -(slope * dist), NEG_INF)
    first = np.where(j >= BLOCK, inner, NEG_INF)
    last = np.where(j < 2 * BLOCK, inner, NEG_INF)
    return np.stack([first, inner, last]).astype(np.float32)


def _full(shape):
    n = len(shape)
    return pl.BlockSpec(shape, lambda *_: (0,) * n)


def _sds(shape, dtype=F32):
    return jax.ShapeDtypeStruct(shape, dtype)


def _coords():
    return lax.axis_index("x"), lax.axis_index("y"), lax.axis_index("c")


def _other_chips(x, y):
    return [(1 - x, y), (x, 1 - y), (1 - x, 1 - y)]


def _in_hbm(*operands):
    return [pltpu.with_memory_space_constraint(a, pltpu.HBM) if a.size * a.dtype.itemsize >= MIB // 4 else a
            for a in operands]


def _remote(src, dst, sems, dev):
    return pltpu.make_async_remote_copy(src_ref=src, dst_ref=dst, send_sem=sems[0], recv_sem=sems[1],
                                        device_id=dev, device_id_type=MESH)


class _Job:
    def __init__(self, inputs, out_shapes, n_remote, n_local, make, then=None, in_place=False):
        self.inputs, self.out_shapes, self.n_remote, self.n_local, self.make = inputs, out_shapes, n_remote, n_local, make
        self.then = then
        self.in_place = in_place


def _job_aliases(jobs, in_base, out_base):
    aliases, a, b = {}, 0, 0
    for j in jobs:
        if j.in_place:
            pairs = j.in_place if isinstance(j.in_place, dict) else {k: k for k in range(len(j.inputs))}
            aliases.update({in_base + a + i: out_base + b + o for i, o in pairs.items()})
        a, b = a + len(j.inputs), b + len(j.out_shapes)
    return aliases


def _job_copies(jobs, jin, jout, sems, second=False):
    send, recv, loc = sems
    res, a, b, r, l = [], 0, 0, 0, 0
    for j in jobs:
        build = j.then if second else j.make
        if build is not None:
            res += build(jin[a:a + len(j.inputs)], jout[b:b + len(j.out_shapes)],
                         lambda k, r=r: (send.at[r + k], recv.at[r + k]), lambda k, l=l: loc.at[l + k])
        a, b, r, l = a + len(j.inputs), b + len(j.out_shapes), r + j.n_remote, l + j.n_local
    return res


def _run(copies):
    for cp in copies:
        cp.start()
    for cp in copies:
        cp.wait()


def _job_gather_rows(src, rows, shape=None, into=None):
    start, size = rows

    def make(ins, outs, rsem, lsem):
        x, y, c = _coords()
        j = 2 * x + y
        part = ins[0].at[pl.ds(start, size), :]
        dst = outs[-1].at[j, pl.ds(start, size), :]
        return [pltpu.make_async_copy(part, dst, lsem(0))] + [
            _remote(part, dst, rsem(k), (*chip, c)) for k, chip in enumerate(_other_chips(x, y))]

    if into is None:
        return _Job([src], [_sds((N_CHIPS,) + shape, BF16)], 3, 1, make)
    return _Job([src, into], [_sds(into.shape, into.dtype)], 3, 1, make, in_place={1: 0})


def _job_gather(sources, shapes):
    n = len(sources)

    def make(ins, outs, rsem, lsem):
        x, y, c = _coords()
        j = 2 * x + y
        res = []
        for t, ((_, layer), src, dst) in enumerate(zip(sources, ins, outs)):
            src = src if layer is None else src.at[layer]
            res.append(pltpu.make_async_copy(src, dst.at[j], lsem(t)))
            for k, chip in enumerate(_other_chips(x, y)):
                res.append(_remote(src, dst.at[j], rsem(3 * t + k), (*chip, c)))
        return res

    return _Job([a for a, _ in sources], [_sds((N_CHIPS,) + s, BF16) for s in shapes], 3 * n, n, make)


def _job_swap(g):
    _, rows, cols = g.shape
    half = rows // 2

    def make(ins, outs, rsem, lsem):
        x, y, c = _coords()
        return [_remote(ins[0].at[:, pl.ds((1 - c) * half, half), :], outs[0], rsem(0), (x, y, 1 - c))]

    return _Job([g], [_sds((N_CHIPS, half, cols))], 1, 0, make)


def _job_scatter(p):
    def make(ins, outs, rsem, lsem):
        x, y, c = _coords()
        return [_remote(ins[0].at[2 * chip[0] + chip[1]], outs[0].at[k], rsem(k), (*chip, c))
                for k, chip in enumerate(_other_chips(x, y))]

    return _Job([p], [_sds((3,) + p.shape[1:], p.dtype)], 3, 0, make)


def _job_all_gather(blk):
    m_per = blk.shape[0]

    def rows(ref, px, py, pc):
        return ref.at[pl.ds((4 * px + 2 * py + pc) * m_per, m_per), :]

    def make(ins, outs, rsem, lsem):
        x, y, c = _coords()
        res = [pltpu.make_async_copy(ins[0], rows(outs[0], x, y, c), lsem(0)),
               _remote(ins[0], rows(outs[0], x, y, c), rsem(0), (x, y, 1 - c))]
        res += [_remote(ins[0], rows(outs[0], x, y, c), rsem(1 + k), (*chip, c)) for k, chip in enumerate(_other_chips(x, y))]
        return res

    def then(ins, outs, rsem, lsem):
        x, y, c = _coords()
        return [_remote(rows(outs[0], *chip, c), rows(outs[0], *chip, c), rsem(4 + k), (x, y, 1 - c))
                for k, chip in enumerate(_other_chips(x, y))]

    return _Job([blk], [_sds((N_DEV * m_per, blk.shape[1]), blk.dtype)], 7, 1, make, then)


def _job_join(f):
    half = f.shape[0] // 2

    def make(ins, outs, rsem, lsem):
        x, y, c = _coords()
        mine = pl.ds(c * half, half)
        return [_remote(ins[0].at[mine, :], outs[0].at[mine, :], rsem(0), (x, y, 1 - c))]

    return _Job([f], [_sds(f.shape, f.dtype)], 1, 0, make, in_place=True)


def _pallas(body, *, name, grid, in_specs, out_specs, out_shape, operands, vmem_mib, jobs=(), scratch=()):
    in_specs, out_specs, out_shape = list(in_specs), list(out_specs), list(out_shape)
    n_in, n_out = len(in_specs), len(out_specs)
    j_in = [a for j in jobs for a in j.inputs]
    j_out = [s for j in jobs for s in j.out_shapes]
    n_rem = max(1, sum(j.n_remote for j in jobs))
    n_loc = max(1, sum(j.n_local for j in jobs))
    sems = [pltpu.SemaphoreType.DMA((n_rem,)), pltpu.SemaphoreType.DMA((n_rem,)),
            pltpu.SemaphoreType.DMA((n_loc,))] if jobs else []
    scratch = list(scratch) + sems

    def wrapped(*refs):
        ins = refs[:n_in]
        jin = refs[n_in:n_in + len(j_in)]
        outs = refs[n_in + len(j_in):n_in + len(j_in) + n_out]
        jout = refs[n_in + len(j_in) + n_out:n_in + len(j_in) + n_out + len(j_out)]
        own = refs[n_in + len(j_in) + n_out + len(j_out):len(refs) - len(sems)]

        if jobs:
            first = last = None
            for d, n in enumerate(grid):
                f, e = pl.program_id(d) == 0, pl.program_id(d) == n - 1
                first, last = (f, e) if first is None else (first & f, last & e)

            @pl.when(first)
            def _():
                for cp in _job_copies(jobs, jin, jout, refs[-3:]):
                    cp.start()

        body(*ins, *outs, *own)

        if jobs:
            @pl.when(last)
            def _():
                for cp in _job_copies(jobs, jin, jout, refs[-3:]):
                    cp.wait()
                _run(_job_copies(jobs, jin, jout, refs[-3:], second=True))

    return pl.pallas_call(
        wrapped, name=name, grid=grid,
        in_specs=in_specs + [ANY] * len(j_in), out_specs=out_specs + [ANY] * len(j_out),
        out_shape=out_shape + j_out, scratch_shapes=scratch, input_output_aliases=_job_aliases(jobs, n_in, n_out),
        compiler_params=pltpu.CompilerParams(dimension_semantics=("arbitrary",) * len(grid),
                                             vmem_limit_bytes=vmem_mib * MIB),
    )(*_in_hbm(*operands, *j_in))


def _comm(jobs, name):
    j_in = [a for j in jobs for a in j.inputs]
    j_out = [s for j in jobs for s in j.out_shapes]
    n_rem = max(1, sum(j.n_remote for j in jobs))
    n_loc = max(1, sum(j.n_local for j in jobs))

    def body(*refs):
        jin, jout = refs[:len(j_in)], refs[len(j_in):len(j_in) + len(j_out)]
        _run(_job_copies(jobs, jin, jout, refs[-3:]))
        _run(_job_copies(jobs, jin, jout, refs[-3:], second=True))

    return pl.pallas_call(
        body, name=name, in_specs=[ANY] * len(j_in), out_specs=[ANY] * len(j_out), out_shape=j_out,
        scratch_shapes=[pltpu.SemaphoreType.DMA((n_rem,)), pltpu.SemaphoreType.DMA((n_rem,)),
                        pltpu.SemaphoreType.DMA((n_loc,))],
        input_output_aliases=_job_aliases(jobs, 0, 0),
    )(*_in_hbm(*j_in))


def _sigmoid(x):
    return 1.0 / (1.0 + jnp.exp(-x))


def _lo_mask(shape):
    return lax.broadcasted_iota(jnp.int32, shape, len(shape) - 1) < HEAD_DIM


def _half_sum(x, lo):
    a = jnp.sum(jnp.where(lo, x, 0.0), axis=-1, keepdims=True)
    b = jnp.sum(jnp.where(lo, 0.0, x), axis=-1, keepdims=True)
    return jnp.where(lo, a, b)


def _half_rms_scale(x, lo):
    return lax.rsqrt(_half_sum(x * x, lo) * (1.0 / HEAD_DIM) + EPS)


def _stack_heads(pairs, lo):
    return jnp.concatenate([jnp.where(lo, t, 0.0) for t in pairs] + [jnp.where(lo, 0.0, t) for t in pairs], axis=0)


def _unstack_pair(stack, p, lo):
    return jnp.where(lo, stack[BLOCK * p:BLOCK * (p + 1)], stack[BLOCK * (4 + p):BLOCK * (5 + p)])


def _attention_probs(q_stack, kn, bias_ref, sink_ref):
    rows = N_HEADS * BLOCK
    s = lax.dot_general(q_stack, kn, NT_DIMS, preferred_element_type=F32) + bias_ref[...]
    sink = jnp.concatenate([jnp.full((BLOCK, BLOCK), sink_ref[h], F32) for h in range(N_HEADS)], axis=0)
    cols = [s[:, BLOCK * j:BLOCK * (j + 1)] for j in range(3)]
    top = jnp.max(jnp.maximum(jnp.maximum(cols[0], cols[1]), cols[2]), axis=-1, keepdims=True)
    m = jnp.maximum(jnp.broadcast_to(top, (rows, BLOCK)), sink)
    e = [jnp.exp(c - m) for c in cols]
    es = jnp.exp(sink - m)
    inv = 1.0 / (jnp.broadcast_to(jnp.sum((e[0] + e[1]) + e[2], axis=-1, keepdims=True), (rows, BLOCK)) + es)
    return jnp.concatenate([c * inv for c in e], axis=1), es * inv


def _kv_rows(cur_ref, pkv_ref, nkv_ref):
    k = jnp.concatenate([pkv_ref[:, 0:D_KV], cur_ref[:, C_K:C_K + D_KV], nkv_ref[:, 0:D_KV]], axis=0)
    v = jnp.concatenate([pkv_ref[:, D_KV:2 * D_KV], cur_ref[:, C_V:C_V + D_KV], nkv_ref[:, D_KV:2 * D_KV]], axis=0)
    return k, v


def _overlap_add(parts):
    blocks = []
    for j in range(SUB + 2):
        terms = [parts[b][BLOCK * (j - b):BLOCK * (j - b + 1)] for b in range(SUB) if 0 <= j - b <= 2]
        total = terms[0]
        for t in terms[1:]:
            total = total + t
        blocks.append(total)
    return jnp.concatenate(blocks, axis=0)


def _mix_specs(nt):
    cur = pl.BlockSpec((TILE, D_IN), lambda i: (i, 0))
    kv_col = C_K // (2 * D_KV)
    pkv = pl.BlockSpec((BLOCK, 2 * D_KV), lambda i: (jnp.maximum(i * SUB - 1, 0), kv_col))
    nkv = pl.BlockSpec((BLOCK, 2 * D_KV), lambda i: (jnp.minimum((i + 1) * SUB, nt * SUB - 1), kv_col))
    table = (None, N_HEADS * BLOCK, 3 * BLOCK)
    first = pl.BlockSpec(table, lambda i: (jnp.where(i == 0, 0, 1), 0, 0))
    inner = pl.BlockSpec(table, lambda i: (1, 0, 0))
    last = pl.BlockSpec(table, lambda i: (jnp.where(i == nt - 1, 2, 1), 0, 0))
    return cur, pkv, nkv, [first] + [inner] * (SUB - 2) + [last]


def _proj_fwd(x, ng, scale, shift, wt, name, jobs=()):
    s = x.shape[0]
    ts = min(512, s)

    def body(x_ref, ng_ref, sc_ref, sh_ref, w_ref, o_ref):
        xv = x_ref[...]
        r = lax.rsqrt(jnp.mean(xv * xv, axis=-1, keepdims=True) + EPS)
        h = ((xv * r) * ng_ref[...]) * (1.0 + sc_ref[...]) + sh_ref[...]
        o_ref[...] = lax.dot_general(h.astype(BF16), w_ref[...], NT_DIMS, preferred_element_type=F32)

    vec = _full((1, D_MODEL))
    return _pallas(
        body, name=name, grid=(s // ts,),
        in_specs=[pl.BlockSpec((ts, D_MODEL), lambda i: (i, 0)), vec, vec, vec, _full((D_IN, D_MODEL))],
        out_specs=[pl.BlockSpec((ts, D_IN), lambda i: (i, 0))], out_shape=[_sds((s, D_IN))],
        operands=(x, ng, scale, shift, wt), vmem_mib=48, jobs=jobs)


def _diagonal():
    return lax.broadcasted_iota(jnp.int32, (BLOCK, BLOCK), 0) == lax.broadcasted_iota(jnp.int32, (BLOCK, BLOCK), 1)


def _column_as_row(wide, eye):
    return jnp.sum(jnp.where(eye, wide, 0.0), axis=0, keepdims=True)


def _mix_fwd(proj, bias, sink, qg2, kg2, ws, bsp, name, jobs=()):
    s = proj.shape[0]
    nt = s // TILE

    def body(sink_ref, cur_ref, pkv_ref, nkv_ref, *rest):
        bias_refs = rest[:SUB]
        qg_ref, kg_ref, ws_ref, bsp_ref, y_ref, p_ref, ps_ref, attn_ref, sv_ref = rest[SUB:]
        lo = _lo_mask((BLOCK, BLOCK))
        eye = _diagonal()
        lo_kv = _lo_mask((TILE + 2 * BLOCK, BLOCK))
        k_all, v_all = _kv_rows(cur_ref, pkv_ref, nkv_ref)
        kn_all = ((k_all * _half_rms_scale(k_all, lo_kv)) * kg_ref[...]).astype(BF16)
        vb_all = v_all.astype(BF16)
        for b in range(SUB):
            rows = slice(BLOCK * b, BLOCK * (b + 1))
            window = slice(BLOCK * b, BLOCK * (b + 3))
            qn = []
            for p in range(4):
                q = cur_ref[rows, C_Q + BLOCK * p:C_Q + BLOCK * (p + 1)]
                qn.append(((q * _half_rms_scale(q, lo)) * qg_ref[...]) * 0.125)
            q_stack = _stack_heads(qn, lo).astype(BF16)
            prob, psink = _attention_probs(q_stack, kn_all[window], bias_refs[b], sink_ref)
            pb = prob.astype(BF16)
            p_ref[STACK * b:STACK * (b + 1), :] = pb
            ps_ref[N_HEADS * b:N_HEADS * (b + 1), :] = jnp.concatenate(
                [_column_as_row(psink[BLOCK * h:BLOCK * (h + 1)], eye) for h in range(N_HEADS)], axis=0)
            o_stack = jnp.dot(pb, vb_all[window], preferred_element_type=F32)
            for p in range(4):
                g = cur_ref[rows, C_GA + BLOCK * p:C_GA + BLOCK * (p + 1)]
                attn = _unstack_pair(o_stack, p, lo)
                attn_ref[rows, BLOCK * p:BLOCK * (p + 1)] = attn.astype(BF16)
                y_ref[rows, BLOCK * p:BLOCK * (p + 1)] = (attn * (g * _sigmoid(g))).astype(BF16)

        for p in range(4):
            cols = slice(C_VG + BLOCK * p, C_VG + BLOCK * (p + 1))
            vn = []
            for b in range(SUB):
                vg = cur_ref[BLOCK * b:BLOCK * (b + 1), cols]
                vn.append((vg * _half_rms_scale(vg, lo)).astype(BF16))
            vn = jnp.concatenate(vn, axis=1)
            sv_a = jnp.dot(ws_ref[2 * p], vn, preferred_element_type=F32)
            sv_b = jnp.dot(ws_ref[2 * p + 1], vn, preferred_element_type=F32)
            for b in range(SUB):
                rows = slice(BLOCK * b, BLOCK * (b + 1))
                lanes = slice(BLOCK * b, BLOCK * (b + 1))
                sv = jnp.where(lo, sv_a[:, lanes], sv_b[:, lanes]) + bsp_ref[p]
                sv_ref[rows, BLOCK * p:BLOCK * (p + 1)] = sv.astype(BF16)
                u = cur_ref[rows, C_U + BLOCK * p:C_U + BLOCK * (p + 1)]
                g = cur_ref[rows, C_GG + BLOCK * p:C_GG + BLOCK * (p + 1)]
                y_ref[rows, D_ATTN + BLOCK * p:D_ATTN + BLOCK * (p + 1)] = ((u * sv) * (g * _sigmoid(g))).astype(BF16)

    cur, pkv, nkv, bias_specs = _mix_specs(nt)
    nb = nt * SUB
    half = pl.BlockSpec((TILE, D_ATTN), lambda i: (i, 0))
    return _pallas(
        body, name=name, grid=(nt,),
        in_specs=[pl.BlockSpec(memory_space=pltpu.SMEM), cur, pkv, nkv, *bias_specs, _full((1, BLOCK)), _full((1, BLOCK)),
                  _full((8, BLOCK, BLOCK)), _full((4, BLOCK, BLOCK))],
        out_specs=[pl.BlockSpec((TILE, D_MODEL), lambda i: (i, 0)), pl.BlockSpec((SUB * STACK, 3 * BLOCK), lambda i: (i, 0)),
                   pl.BlockSpec((SUB * N_HEADS, BLOCK), lambda i: (i, 0)), half, half],
        out_shape=[_sds((s, D_MODEL), BF16), _sds((nb * STACK, 3 * BLOCK), BF16), _sds((nb * N_HEADS, BLOCK)),
                   _sds((s, D_ATTN), BF16), _sds((s, D_ATTN), BF16)],
        operands=(sink, proj, proj, proj, *([bias] * SUB), qg2, kg2, ws, bsp), vmem_mib=56, jobs=jobs)


def _out_proj_fwd(y, x, gate, w_out, ng, scale, shift, wt, name):
    s = x.shape[0]
    ts = min(512, s)

    def body(y_ref, x_ref, g_ref, w_ref, ng_ref, sc_ref, sh_ref, wt_ref, xn_ref, p_ref):
        xv = x_ref[...] + g_ref[...] * jnp.dot(y_ref[...], w_ref[...], preferred_element_type=F32)
        xn_ref[...] = xv
        r = lax.rsqrt(jnp.mean(xv * xv, axis=-1, keepdims=True) + EPS)
        h = ((xv * r) * ng_ref[...]) * (1.0 + sc_ref[...]) + sh_ref[...]
        p_ref[...] = lax.dot_general(h.astype(BF16), wt_ref[...], NT_DIMS, preferred_element_type=F32)

    row = pl.BlockSpec((ts, D_MODEL), lambda i: (i, 0))
    vec = _full((1, D_MODEL))
    return _pallas(
        body, name=name, grid=(s // ts,),
        in_specs=[row, row, vec, _full((D_MODEL, D_MODEL)), vec, vec, vec, _full((D_IN, D_MODEL))],
        out_specs=[row, pl.BlockSpec((ts, D_IN), lambda i: (i, 0))], out_shape=[_sds((s, D_MODEL)), _sds((s, D_IN))],
        operands=(y, x, gate, w_out, ng, scale, shift, wt), vmem_mib=56)


def _out_loss_bwd(y, x, gate, w_out, target, name):
    s = x.shape[0]
    ts = min(512, s)
    steps = s // ts

    def body(y_ref, x_ref, g_ref, w_ref, t_ref, dx_ref, sq_ref, dy_ref, gw_ref, dg_ref):
        @pl.when(pl.program_id(0) == 0)
        def _():
            sq_ref[...] = jnp.zeros_like(sq_ref)
            gw_ref[...] = jnp.zeros_like(gw_ref)

        yv = y_ref[...]
        out = x_ref[...] + g_ref[...] * jnp.dot(yv, w_ref[...], preferred_element_type=F32)
        diff = out - t_ref[...]
        dx = diff * (1.0 / D_MODEL)
        dx_ref[...] = dx
        per_token = jnp.sum(diff * diff, axis=-1, keepdims=True) * (1.0 / D_MODEL)
        sq_ref[...] += jnp.sum(per_token, axis=0, keepdims=True)
        dy_ref[...] = lax.dot_general((dx * g_ref[...]).astype(BF16), w_ref[...], NT_DIMS, preferred_element_type=F32)
        gw_ref[...] += lax.dot_general(yv, dx.astype(BF16), TN_DIMS, preferred_element_type=F32)

        @pl.when(pl.program_id(0) == steps - 1)
        def _():
            m = gw_ref[...]
            dg_ref[...] = jnp.sum(w_ref[...].astype(F32) * m, axis=0, keepdims=True)
            gw_ref[...] = m * g_ref[...]

    row = pl.BlockSpec((ts, D_MODEL), lambda i: (i, 0))
    return _pallas(
        body, name=name, grid=(s // ts,), in_specs=[row, row, _full((1, D_MODEL)), _full((D_MODEL, D_MODEL)), row],
        out_specs=[row, _full((1, 1)), row, _full((D_MODEL, D_MODEL)), _full((1, D_MODEL))],
        out_shape=[_sds((s, D_MODEL)), _sds((1, 1)), _sds((s, D_MODEL)), _sds((D_MODEL, D_MODEL)), _sds((1, D_MODEL))],
        operands=(y, x, gate, w_out, target), vmem_mib=48)


def _out_bwd(dxo, y, gate, w_out, name, jobs=()):
    s = dxo.shape[0]
    ts = min(512, s)
    steps = s // ts

    def body(dx_ref, y_ref, g_ref, w_ref, dy_ref, gw_ref, dg_ref):
        @pl.when(pl.program_id(0) == 0)
        def _():
            gw_ref[...] = jnp.zeros_like(gw_ref)

        dx = dx_ref[...]
        dy_ref[...] = lax.dot_general((dx * g_ref[...]).astype(BF16), w_ref[...], NT_DIMS, preferred_element_type=F32)
        gw_ref[...] += lax.dot_general(y_ref[...], dx.astype(BF16), TN_DIMS, preferred_element_type=F32)

        @pl.when(pl.program_id(0) == steps - 1)
        def _():
            m = gw_ref[...]
            dg_ref[...] = jnp.sum(w_ref[...].astype(F32) * m, axis=0, keepdims=True)
            gw_ref[...] = m * g_ref[...]

    row = pl.BlockSpec((ts, D_MODEL), lambda i: (i, 0))
    return _pallas(
        body, name=name, grid=(s // ts,), in_specs=[row, row, _full((1, D_MODEL)), _full((D_MODEL, D_MODEL))],
        out_specs=[row, _full((D_MODEL, D_MODEL)), _full((1, D_MODEL))],
        out_shape=[_sds((s, D_MODEL)), _sds((D_MODEL, D_MODEL)), _sds((1, D_MODEL))],
        operands=(dxo, y, gate, w_out), vmem_mib=48, jobs=jobs)


def _mix_bwd(dy, proj, probs, psink, attn, sv, qg2, kg2, wst, name, jobs=()):
    s = proj.shape[0]
    nt = s // TILE

    def body(dy_ref, cur_ref, pkv_ref, nkv_ref, p_ref, ps_ref, attn_ref, sv_ref, qg_ref, kg_ref, wst_ref,
             dpb_ref, dkv_ref, p0_ref, p2_ref, dqg_ref, dkg_ref, dsink_ref, dws_ref, dbsp_ref):
        def put(rows, col, value):
            dpb_ref[rows, col:col + BLOCK] = value.astype(BF16)

        @pl.when(pl.program_id(0) == 0)
        def _():
            dqg_ref[...] = jnp.zeros_like(dqg_ref)
            dkg_ref[...] = jnp.zeros_like(dkg_ref)
            dsink_ref[...] = jnp.zeros_like(dsink_ref)
            dws_ref[...] = jnp.zeros_like(dws_ref)
            dbsp_ref[...] = jnp.zeros_like(dbsp_ref)

        lo = _lo_mask((BLOCK, BLOCK))
        lo_kv = _lo_mask((TILE + 2 * BLOCK, BLOCK))
        eye = _diagonal()
        lane_row = lax.broadcasted_iota(jnp.int32, (1, BLOCK), 1)
        qg = qg_ref[...]
        kg = kg_ref[...]

        k_all, v_all = _kv_rows(cur_ref, pkv_ref, nkv_ref)
        rk = _half_rms_scale(k_all, lo_kv)
        khat = k_all * rk
        kn_all = (khat * kg).astype(BF16)
        vb_all = v_all.astype(BF16)

        dkn_parts, dv_parts = [], []
        dsink = jnp.zeros((1, BLOCK), F32)
        dqg = jnp.zeros((1, BLOCK), F32)
        for b in range(SUB):
            rows = slice(BLOCK * b, BLOCK * (b + 1))
            window = slice(BLOCK * b, BLOCK * (b + 3))
            kn, vb = kn_all[window], vb_all[window]

            qhat, rq = [], []
            for p in range(4):
                q = cur_ref[rows, C_Q + BLOCK * p:C_Q + BLOCK * (p + 1)]
                r = _half_rms_scale(q, lo)
                rq.append(r)
                qhat.append(q * r)
            q_stack = _stack_heads([(qh * qg) * 0.125 for qh in qhat], lo).astype(BF16)
            pb = p_ref[STACK * b:STACK * (b + 1), :]
            prob = pb.astype(F32)

            dout = []
            for p in range(4):
                g = cur_ref[rows, C_GA + BLOCK * p:C_GA + BLOCK * (p + 1)]
                sg = _sigmoid(g)
                dya = dy_ref[rows, BLOCK * p:BLOCK * (p + 1)]
                attn = attn_ref[rows, BLOCK * p:BLOCK * (p + 1)]
                put(rows, C_GA + BLOCK * p, dya * attn * (sg * (1.0 + g * (1.0 - sg))))
                dout.append(dya * (g * sg))
            do_stack = _stack_heads(dout, lo).astype(BF16)
            dp = lax.dot_general(do_stack, vb, NT_DIMS, preferred_element_type=F32)
            delta = jnp.sum(prob * dp, axis=-1, keepdims=True)
            dsb = (prob * (dp - delta)).astype(BF16)

            for h in range(N_HEADS):
                delta_row = _column_as_row(jnp.broadcast_to(delta[BLOCK * h:BLOCK * (h + 1)], (BLOCK, BLOCK)), eye)
                tot = jnp.sum(ps_ref[N_HEADS * b + h:N_HEADS * b + h + 1, :] * delta_row, axis=-1, keepdims=True)
                dsink = dsink - jnp.where(lane_row == h, tot, 0.0)

            dq_stack = jnp.dot(dsb, kn, preferred_element_type=F32) * 0.125
            dkn_parts.append(lax.dot_general(dsb, q_stack, TN_DIMS, preferred_element_type=F32))
            dv_parts.append(lax.dot_general(pb, do_stack, TN_DIMS, preferred_element_type=F32))

            for p in range(4):
                dqn = _unstack_pair(dq_stack, p, lo)
                qh = qhat[p]
                dqg = dqg + jnp.sum(dqn * qh, axis=0, keepdims=True)
                dqh = dqn * qg
                mean = _half_sum(dqh * qh, lo) * (1.0 / HEAD_DIM)
                put(rows, C_Q + BLOCK * p, rq[p] * (dqh - qh * mean))

        dws_new, dbs_new = [], []
        for p in range(4):
            rs, vnfs, vns, dsvs, dbs = [], [], [], [], None
            for b in range(SUB):
                rows = slice(BLOCK * b, BLOCK * (b + 1))
                vg = cur_ref[rows, C_VG + BLOCK * p:C_VG + BLOCK * (p + 1)]
                r = _half_rms_scale(vg, lo)
                vnf = vg * r
                sv = sv_ref[rows, BLOCK * p:BLOCK * (p + 1)]
                u = cur_ref[rows, C_U + BLOCK * p:C_U + BLOCK * (p + 1)]
                g = cur_ref[rows, C_GG + BLOCK * p:C_GG + BLOCK * (p + 1)]
                sg = _sigmoid(g)
                dym = dy_ref[rows, D_ATTN + BLOCK * p:D_ATTN + BLOCK * (p + 1)]
                put(rows, C_GG + BLOCK * p, dym * (u * sv) * (sg * (1.0 + g * (1.0 - sg))))
                dgm = dym * (g * sg)
                put(rows, C_U + BLOCK * p, dgm * sv)
                dsv = dgm * u
                term = jnp.where(lo, jnp.sum(jnp.where(lo, dsv, 0.0), axis=-1, keepdims=True),
                                 jnp.sum(jnp.where(lo, 0.0, dsv), axis=-1, keepdims=True))
                dbs = term if dbs is None else dbs + term
                rs.append(r)
                vnfs.append(vnf)
                vns.append(vnf.astype(BF16))
                dsvs.append(dsv)
            dbs_new.append(dbs)
            vn = jnp.concatenate(vns, axis=1)
            dsv = jnp.concatenate(dsvs, axis=1)
            lo_t = (lax.broadcasted_iota(jnp.int32, dsv.shape, 1) & (BLOCK - 1)) < HEAD_DIM
            dws_new.append(lax.dot_general(jnp.where(lo_t, dsv, 0.0).astype(BF16), vn, NT_DIMS, preferred_element_type=F32))
            dws_new.append(lax.dot_general(jnp.where(lo_t, 0.0, dsv).astype(BF16), vn, NT_DIMS, preferred_element_type=F32))
            dsvb = dsv.astype(BF16)
            dvn_a = jnp.dot(wst_ref[2 * p], dsvb, preferred_element_type=F32)
            dvn_b = jnp.dot(wst_ref[2 * p + 1], dsvb, preferred_element_type=F32)
            for b in range(SUB):
                lanes = slice(BLOCK * b, BLOCK * (b + 1))
                dvn = jnp.where(lo, dvn_a[:, lanes], dvn_b[:, lanes])
                mean = _half_sum(dvn * vnfs[b], lo) * (1.0 / HEAD_DIM)
                put(slice(BLOCK * b, BLOCK * (b + 1)), C_VG + BLOCK * p, rs[b] * (dvn - vnfs[b] * mean))

        dsink_ref[...] += dsink
        dqg = jnp.broadcast_to(dqg, (8, BLOCK))
        dqg_ref[...] += dqg + pltpu.roll(dqg, HEAD_DIM, 1)

        dkn = _overlap_add(dkn_parts)
        dv = _overlap_add(dv_parts)
        dkg = jnp.broadcast_to(jnp.sum(dkn * khat, axis=0, keepdims=True), (8, BLOCK))
        dkg_ref[...] += dkg + pltpu.roll(dkg, HEAD_DIM, 1)
        dkh = dkn * kg
        dk = rk * (dkh - khat * (_half_sum(dkh * khat, lo_kv) * (1.0 / HEAD_DIM)))
        dpb_ref[:, C_K:C_GA] = jnp.zeros((TILE, 2 * D_KV), BF16)
        dkv_ref[:, 0:D_KV] = dk[BLOCK:BLOCK + TILE]
        dkv_ref[:, D_KV:2 * D_KV] = dv[BLOCK:BLOCK + TILE]
        p0_ref[:, 0:D_KV] = dk[0:BLOCK]
        p0_ref[:, D_KV:2 * D_KV] = dv[0:BLOCK]
        p2_ref[:, 0:D_KV] = dk[BLOCK + TILE:]
        p2_ref[:, D_KV:2 * D_KV] = dv[BLOCK + TILE:]
        for g, new in enumerate(dws_new):
            dws_ref[g] += new
        for p, new in enumerate(dbs_new):
            dbsp_ref[p] += new

    cur, pkv, nkv, _ = _mix_specs(nt)
    kv_blk = (BLOCK, 2 * D_KV)
    half = pl.BlockSpec((TILE, D_ATTN), lambda i: (i, 0))
    return _pallas(
        body, name=name, grid=(nt,),
        in_specs=[pl.BlockSpec((TILE, D_MODEL), lambda i: (i, 0)), cur, pkv, nkv,
                  pl.BlockSpec((SUB * STACK, 3 * BLOCK), lambda i: (i, 0)), pl.BlockSpec((SUB * N_HEADS, BLOCK), lambda i: (i, 0)),
                  half, half, _full((1, BLOCK)), _full((1, BLOCK)), _full((8, BLOCK, BLOCK))],
        out_specs=[cur, pl.BlockSpec((TILE, 2 * D_KV), lambda i: (i, 0)),
                   pl.BlockSpec(kv_blk, lambda i: ((i + nt - 1) % nt, 0)),
                   pl.BlockSpec(kv_blk, lambda i: ((i + 1) % nt, 0)),
                   _full((8, BLOCK)), _full((8, BLOCK)), _full((1, BLOCK)),
                   _full((8, BLOCK, BLOCK)), _full((4, BLOCK, BLOCK))],
        out_shape=[_sds((s, D_IN), BF16), _sds((s, 2 * D_KV)), _sds((nt * BLOCK, 2 * D_KV)), _sds((nt * BLOCK, 2 * D_KV)),
                   _sds((8, BLOCK)), _sds((8, BLOCK)), _sds((1, BLOCK)),
                   _sds((8, BLOCK, BLOCK)), _sds((4, BLOCK, BLOCK))],
        operands=(dy, proj, proj, proj, probs, psink, attn, sv, qg2, kg2, wst), vmem_mib=56, jobs=jobs)


def _w_in_grad(dpb, dkv, p0, p2, x, ng, scale, shift, name, jobs=()):
    s = x.shape[0]
    ts = min(2 * TILE, s)
    tiles = ts // TILE

    def body(dpb_ref, dkv_ref, p0_ref, p2_ref, x_ref, ng_ref, sc_ref, sh_ref, gw_ref, dkvb_ref):
        @pl.when(pl.program_id(0) == 0)
        def _():
            gw_ref[...] = jnp.zeros_like(gw_ref)

        xv = x_ref[...]
        r = lax.rsqrt(jnp.mean(xv * xv, axis=-1, keepdims=True) + EPS)
        h = (((xv * r) * ng_ref[...]) * (1.0 + sc_ref[...]) + sh_ref[...]).astype(BF16)
        for t in range(tiles):
            halo = slice(BLOCK * t, BLOCK * (t + 1))
            first = slice(TILE * t, TILE * t + BLOCK)
            last = slice(TILE * (t + 1) - BLOCK, TILE * (t + 1))
            dkvb_ref[first, :] = (dkv_ref[first, :] + p2_ref[halo, :]).astype(BF16)
            if SUB > 2:
                inner = slice(TILE * t + BLOCK, TILE * (t + 1) - BLOCK)
                dkvb_ref[inner, :] = dkv_ref[inner, :].astype(BF16)
            dkvb_ref[last, :] = (dkv_ref[last, :] + p0_ref[halo, :]).astype(BF16)
        gw_ref[...] += lax.dot_general(dpb_ref[...], h, TN_DIMS, preferred_element_type=F32)
        gw_ref[C_K:C_GA, :] += lax.dot_general(dkvb_ref[...], h, TN_DIMS, preferred_element_type=F32)

    kv = pl.BlockSpec((ts, 2 * D_KV), lambda i: (i, 0))
    halo = pl.BlockSpec((tiles * BLOCK, 2 * D_KV), lambda i: (i, 0))
    vec = _full((1, D_MODEL))
    return _pallas(
        body, name=name, grid=(s // ts,),
        in_specs=[pl.BlockSpec((ts, D_IN), lambda i: (i, 0)), kv, halo, halo,
                  pl.BlockSpec((ts, D_MODEL), lambda i: (i, 0)), vec, vec, vec],
        out_specs=[_full((D_IN, D_MODEL)), kv], out_shape=[_sds((D_IN, D_MODEL)), _sds((s, 2 * D_KV), BF16)],
        operands=(dpb, dkv, p0, p2, x, ng, scale, shift), vmem_mib=56, jobs=jobs)


def _proj_bwd(dpb, dkvb, x, dxo, ng, scale, wt, name, jobs=()):
    s = x.shape[0]
    ts = min(512, s)

    def body(dpb_ref, dkvb_ref, x_ref, dxo_ref, ng_ref, sc_ref, w_ref, dxi_ref, dsh_ref, dsc_ref, dng_ref):
        @pl.when(pl.program_id(0) == 0)
        def _():
            dsh_ref[...] = jnp.zeros_like(dsh_ref)
            dsc_ref[...] = jnp.zeros_like(dsc_ref)
            dng_ref[...] = jnp.zeros_like(dng_ref)

        dh = (jnp.dot(dpb_ref[...], w_ref[...], preferred_element_type=F32)
              + jnp.dot(dkvb_ref[...], w_ref[C_K:C_GA, :], preferred_element_type=F32))

        xv = x_ref[...]
        r = lax.rsqrt(jnp.mean(xv * xv, axis=-1, keepdims=True) + EPS)
        xn = xv * r
        ngv = ng_ref[...]
        sc1 = 1.0 + sc_ref[...]
        dsh_ref[...] += jnp.sum(dh, axis=0, keepdims=True)
        dsc_ref[...] += jnp.sum(dh * (xn * ngv), axis=0, keepdims=True)
        dh1 = dh * sc1
        dng_ref[...] += jnp.sum(dh1 * xn, axis=0, keepdims=True)
        dxn = dh1 * ngv
        dxi_ref[...] = r * (dxn - xn * jnp.mean(dxn * xn, axis=-1, keepdims=True)) + dxo_ref[...]

    row = pl.BlockSpec((ts, D_MODEL), lambda i: (i, 0))
    vec = _full((1, D_MODEL))
    return _pallas(
        body, name=name, grid=(s // ts,),
        in_specs=[pl.BlockSpec((ts, D_IN), lambda i: (i, 0)), pl.BlockSpec((ts, 2 * D_KV), lambda i: (i, 0)),
                  row, row, vec, vec, _full((D_IN, D_MODEL))],
        out_specs=[row, vec, vec, vec],
        out_shape=[_sds((s, D_MODEL)), _sds((1, D_MODEL)), _sds((1, D_MODEL)), _sds((1, D_MODEL))],
        operands=(dpb, dkvb, x, dxo, ng, scale, wt), vmem_mib=48, jobs=jobs)


def _pair_sum(g, r, c_idx, name, send_dtype=None):
    _, rows, cols = g.shape
    half = rows // 2

    def body(c_ref, g_ref, r_ref, o_ref, *narrow):
        total = g_ref[...] + r_ref[...]
        o_ref[...] = total
        for n_ref in narrow:
            n_ref[...] = total.astype(n_ref.dtype)

    blk = (None, half, cols)
    out_blk = pl.BlockSpec(blk, lambda j, c: (j, 0, 0))
    shapes = [_sds((N_CHIPS, half, cols))] + ([_sds((N_CHIPS, half, cols), send_dtype)] if send_dtype else [])
    return pl.pallas_call(
        body, name=name,
        grid_spec=pltpu.PrefetchScalarGridSpec(
            num_scalar_prefetch=1, grid=(N_CHIPS,),
            in_specs=[pl.BlockSpec(blk, lambda j, c: (j, c[0], 0)), out_blk], out_specs=[out_blk] * len(shapes)),
        out_shape=shapes,
        compiler_params=pltpu.CompilerParams(dimension_semantics=("arbitrary",), vmem_limit_bytes=32 * MIB),
    )(*_in_hbm(c_idx, g, r))


def _chip_sum(p, r, place, name):
    _, rows, cols = p.shape
    tr = rows // 2

    def body(j_ref, p_ref, r_ref, o_ref):
        o_ref[...] = ((p_ref[...] + r_ref[0].astype(F32)) + r_ref[1].astype(F32)) + r_ref[2].astype(F32)

    return pl.pallas_call(
        body, name=name,
        grid_spec=pltpu.PrefetchScalarGridSpec(
            num_scalar_prefetch=1, grid=(2,),
            in_specs=[pl.BlockSpec((None, tr, cols), lambda t, j: (j[0], t, 0)),
                      pl.BlockSpec((3, tr, cols), lambda t, j: (0, t, 0))],
            out_specs=pl.BlockSpec((tr, cols), lambda t, j: (2 * j[1] + t, 0))),
        out_shape=_sds((2 * rows, cols)),
        compiler_params=pltpu.CompilerParams(dimension_semantics=("arbitrary",), vmem_limit_bytes=32 * MIB),
    )(*_in_hbm(place, p, r))


def _cast_permute_w_in(wt, place_chunks):
    def body(t_ref, w_ref, w0_ref, w1_ref):
        def cast_into(o_ref):
            for t in range(N_CHUNKS):
                src = pl.multiple_of(t_ref[1 + t] * CHUNK_ROWS, CHUNK_ROWS)
                o_ref[CHUNK_ROWS * t:CHUNK_ROWS * (t + 1), :] = w_ref[pl.ds(src, CHUNK_ROWS), :].astype(BF16)

        @pl.when(pl.program_id(0) == 0)
        def _():
            cast_into(w0_ref)

        @pl.when(pl.program_id(0) == 1)
        def _():
            cast_into(w1_ref)

    return pl.pallas_call(
        body, name="cast_permute_w_in",
        grid_spec=pltpu.PrefetchScalarGridSpec(
            num_scalar_prefetch=1, grid=(DEPTH,),
            in_specs=[pl.BlockSpec((None, W_IN_BLK, D_MODEL), lambda l, tbl: (l, 0, 0))],
            out_specs=[pl.BlockSpec((None, W_IN_BLK, D_MODEL), lambda l, tbl: (tbl[0], 0, 0)),
                       pl.BlockSpec((W_IN_BLK, D_MODEL), lambda l, tbl: (0, 0))]),
        out_shape=[_sds((N_CHIPS, W_IN_BLK, D_MODEL), BF16), _sds((W_IN_BLK, D_MODEL), BF16)],
        compiler_params=pltpu.CompilerParams(dimension_semantics=("arbitrary",), vmem_limit_bytes=32 * MIB),
    )(*_in_hbm(place_chunks, wt))


def _gather_inputs(c, w_out, w0):
    half = W_IN_BLK // 2

    def body(c_ref, wout_ref, mine_ref, call_ref, woutb_ref, w0_ref, send_sems, recv_sems):
        x, y, cc = _coords()
        j = 2 * x + y
        b = 2 * j + cc
        sib = (x, y, 1 - cc)
        woutb_ref[...] = wout_ref[...].astype(BF16)
        call_ref[b] = c_ref[...]
        chips = _other_chips(x, y)

        def sems(k):
            return send_sems.at[k], recv_sems.at[k]

        def half_rows(chip_index):
            return w0_ref.at[chip_index, pl.ds(cc * half, half), :]

        first = [_remote(mine_ref.at[j, pl.ds(cc * half, half), :], half_rows(j), sems(k), (*chip, cc))
                 for k, chip in enumerate(chips)]
        k = 3
        rest = []
        for fx in (0, 1):
            for fy in (0, 1):
                for fc in (0, 1):
                    if fx or fy or fc:
                        dev = (1 - x if fx else x, 1 - y if fy else y, 1 - cc if fc else cc)
                        rest.append(_remote(call_ref.at[b], call_ref.at[b], sems(k), dev))
                        k += 1
        for cp in first + rest:
            cp.start()
        passed = []
        for k, chip in enumerate(chips):
            jk = 2 * chip[0] + chip[1]
            first[k].wait_recv()
            passed.append(_remote(half_rows(jk), half_rows(jk), sems(10 + k), sib))
            passed[k].start()
        for cp in first:
            cp.wait_send()
        for cp in rest + passed:
            cp.wait()

    return pl.pallas_call(
        body, name="gather_inputs", in_specs=[VMEM, VMEM, ANY], out_specs=[VMEM, VMEM, ANY],
        out_shape=[_sds((N_DEV, 1, D_MODEL)), _sds((DEPTH, W_OUT_BLK, D_MODEL), BF16),
                   _sds((N_CHIPS, W_IN_BLK, D_MODEL), BF16)],
        scratch_shapes=[pltpu.SemaphoreType.DMA((13,)), pltpu.SemaphoreType.DMA((13,))],
        input_output_aliases={2: 2},
        compiler_params=pltpu.CompilerParams(vmem_limit_bytes=32 * MIB),
    )(c, w_out, w0)


def _ada_rows(c_all, w_ada, b_blk):
    def body(c_ref, w_ref, b_ref, o_ref, cond_ref):
        cv = c_ref[...]
        cond = (cv * _sigmoid(cv)).astype(BF16)
        cond_ref[...] = cond.astype(F32)
        for l in range(DEPTH):
            o_ref[:, l, :] = jnp.dot(cond, w_ref[l].astype(BF16), preferred_element_type=F32) + b_ref[l:l + 1, :]

    return pl.pallas_call(
        body, name="ada_rows", in_specs=[VMEM, VMEM, VMEM], out_specs=[VMEM, VMEM],
        out_shape=[_sds((N_DEV, DEPTH, W_ADA_BLK)), _sds((N_DEV, D_MODEL))],
        compiler_params=pltpu.CompilerParams(vmem_limit_bytes=32 * MIB),
    )(c_all, w_ada, b_blk)


def _exchange_ada(part):
    def body(part_ref, out_ref, send_sems, recv_sems):
        x, y, cc = _coords()
        j = 2 * x + y
        out_ref[j] = part_ref[2 * j + cc]
        copies = []
        for k, chip in enumerate(_other_chips(x, y)):
            b_dst = 4 * chip[0] + 2 * chip[1] + cc
            copies.append(_remote(part_ref.at[b_dst], out_ref.at[j], (send_sems.at[k], recv_sems.at[k]), (*chip, cc)))
        for cp in copies:
            cp.start()
        for cp in copies:
            cp.wait()

    return pl.pallas_call(
        body, name="exchange_ada", in_specs=[VMEM], out_specs=VMEM,
        out_shape=_sds((N_CHIPS, DEPTH, W_ADA_BLK)),
        scratch_shapes=[pltpu.SemaphoreType.DMA((3,)), pltpu.SemaphoreType.DMA((3,))],
    )(part)


def _adamw_math(w, g, m, v):
    m = ADAM_B1 * m + (1.0 - ADAM_B1) * g
    v = ADAM_B2 * v + (1.0 - ADAM_B2) * (g * g)
    m_hat = m / (1.0 - ADAM_B1 ** ADAM_STEP)
    v_hat = v / (1.0 - ADAM_B2 ** ADAM_STEP)
    delta = -ADAM_LR * (m_hat / (jnp.sqrt(v_hat) + ADAM_EPS) + ADAM_WD * w)
    return delta, m, v


def _adamw_w_in(w, g0, g1, m, v, pos_chunks):
    def body(t_ref, w_ref, g0_ref, g1_ref, m_ref, v_ref, g_ref, d_ref, nm_ref, nv_ref):
        for l, src in enumerate((g0_ref, g1_ref)):
            g = src[...]
            g_ref[l] = g
            d_ref[l], nm_ref[l], nv_ref[l] = _adamw_math(w_ref[l], g, m_ref[l], v_ref[l])

    nat = pl.BlockSpec((DEPTH, CHUNK_ROWS, D_MODEL), lambda t, tbl: (0, t, 0))
    per = pl.BlockSpec((CHUNK_ROWS, D_MODEL), lambda t, tbl: (tbl[t], 0))
    return pl.pallas_call(
        body, name="adamw_w_in",
        grid_spec=pltpu.PrefetchScalarGridSpec(num_scalar_prefetch=1, grid=(N_CHUNKS,),
                                               in_specs=[nat, per, per, nat, nat], out_specs=[nat] * 4),
        out_shape=[_sds(w.shape)] * 4,
        compiler_params=pltpu.CompilerParams(dimension_semantics=("arbitrary",)),
    )(*_in_hbm(pos_chunks, w, g0, g1, m, v))


def _adamw_w_out(w, g0, g1, m, v):
    def body(w_ref, g0_ref, g1_ref, m_ref, v_ref, g_ref, d_ref, nm_ref, nv_ref):
        g = jnp.where(pl.program_id(0) == 0, g0_ref[...], g1_ref[...])
        g_ref[...] = g
        d_ref[...], nm_ref[...], nv_ref[...] = _adamw_math(w_ref[...], g, m_ref[...], v_ref[...])

    blk = pl.BlockSpec((None, W_OUT_BLK, D_MODEL), lambda l: (l, 0, 0))
    gblk = _full((W_OUT_BLK, D_MODEL))
    return pl.pallas_call(
        body, name="adamw_w_out", grid=(DEPTH,), in_specs=[blk, gblk, gblk, blk, blk], out_specs=[blk] * 4,
        out_shape=[_sds(w.shape)] * 4,
        compiler_params=pltpu.CompilerParams(dimension_semantics=("arbitrary",), vmem_limit_bytes=32 * MIB),
    )(w, g0, g1, m, v)


def _w_ada_grad_adamw(cond_t, dada, w, m, v):
    _, rows, cols = w.shape
    tr = 256

    def body(ct_ref, da_ref, w_ref, m_ref, v_ref, g_ref, d_ref, nm_ref, nv_ref):
        g = jnp.dot(ct_ref[...], da_ref[...].astype(BF16), preferred_element_type=F32)
        g_ref[...] = g
        d_ref[...], nm_ref[...], nv_ref[...] = _adamw_math(w_ref[...], g, m_ref[...], v_ref[...])

    blk = pl.BlockSpec((None, tr, cols), lambda l, t: (l, t, 0))
    return pl.pallas_call(
        body, name="w_ada_grad_adamw", grid=(DEPTH, rows // tr),
        in_specs=[pl.BlockSpec((tr, BLOCK), lambda l, t: (t, 0)), pl.BlockSpec((None, BLOCK, cols), lambda l, t: (l, 0, 0)),
                  blk, blk, blk],
        out_specs=[blk] * 4, out_shape=[_sds(w.shape)] * 4,
        compiler_params=pltpu.CompilerParams(dimension_semantics=("arbitrary", "arbitrary"), vmem_limit_bytes=32 * MIB),
    )(cond_t, dada, w, m, v)


def _small_sum_adamw(gathered, w, m, v):
    n = len(gathered)

    def body(*refs):
        w_ref, m_ref, v_ref, g_ref, d_ref, nm_ref, nv_ref = refs[n:]

        def total(ref):
            g = ref[0]
            for b in range(1, N_DEV):
                g = g + ref[b]
            return g

        g = jnp.concatenate([total(ref) for ref in refs[:n]], axis=0)
        g_ref[...] = g
        d_ref[...], nm_ref[...], nv_ref[...] = _adamw_math(w_ref[...], g, m_ref[...], v_ref[...])

    return pl.pallas_call(
        body, name="small_sum_adamw", in_specs=[VMEM] * (n + 3), out_specs=[VMEM] * 4, out_shape=[_sds(w.shape)] * 4,
        compiler_params=pltpu.CompilerParams(vmem_limit_bytes=48 * MIB),
    )(*gathered, w, m, v)


_SMALL_A = (("w_s", 8 * BLOCK), ("b_s", 8), ("q_gain", 1), ("k_gain", 1), ("sink", 1))
_SMALL_B = (("b_ada", DEPTH * 24), ("norm_gain", DEPTH * 8), ("sq_err", 1))


def _pack_rows(parts, layout, layer=None):
    rows = []
    for name, n in layout:
        flat = (parts[name] if layer is None else parts[name][layer]).reshape(-1)
        rows.append(jnp.pad(flat, (0, n * 128 - flat.shape[0])).reshape(n, 128))
    n_rows = sum(n for _, n in layout)
    if n_rows % 8:
        rows.append(jnp.zeros((-n_rows % 8, 128), F32))
    return jnp.concatenate(rows, axis=0)


def _pack_small(parts):
    return jnp.concatenate([_pack_rows(parts, _SMALL_A, l) for l in range(DEPTH)] + [_pack_rows(parts, _SMALL_B)], axis=0)


def _unpack_small(packed, shapes):
    def take(r0, layout, shape_of):
        got = {}
        for name, n in layout:
            shape = shape_of(name)
            size = 1
            for d in shape:
                size *= d
            got[name] = packed[r0:r0 + n].reshape(-1)[:size].reshape(shape)
            r0 += n
        return got, r0 + -r0 % 8

    layers, r0 = [], 0
    for _ in range(DEPTH):
        got, r0 = take(r0, _SMALL_A, lambda name: shapes[name][1:])
        layers.append(got)
    out, _ = take(r0, _SMALL_B, lambda name: shapes[name])
    out.update({name: jnp.stack([layer[name] for layer in layers]) for name, _ in _SMALL_A})
    return out


def _permute_heads(a, axis):
    shp = a.shape
    a = a.reshape(shp[:axis] + (2, 4, HEAD_DIM) + shp[axis + 1:])
    a = jnp.swapaxes(a, axis, axis + 1)
    return a.reshape(shp)


def _unpermute_heads(a, axis):
    shp = a.shape
    a = a.reshape(shp[:axis] + (4, 2, HEAD_DIM) + shp[axis + 1:])
    a = jnp.swapaxes(a, axis, axis + 1)
    return a.reshape(shp)


def _permute_w_out(w):
    return jnp.concatenate([_permute_heads(w[:D_ATTN], 0), w[D_ATTN:]], axis=0)


def _unpermute_w_out(w):
    return jnp.concatenate([_unpermute_heads(w[:D_ATTN], 0), w[D_ATTN:]], axis=0)


def kernel(x, c, w_ada, b_ada, norm_gain, w_in, q_gain, k_gain, sink, w_s, b_s, w_out, loss_target, m_w_ada, m_b_ada, m_norm_gain, m_w_in, m_q_gain, m_k_gain, m_sink, m_w_s, m_b_s, m_w_out, v_w_ada, v_b_ada, v_norm_gain, v_w_in, v_q_gain, v_k_gain, v_sink, v_w_s, v_b_s, v_w_out):
    ix, iy, ic = _coords()
    chip = 2 * ix + iy
    chip_idx = jnp.stack([chip, ic]).astype(jnp.int32)
    core_idx = jnp.reshape(ic, (1,)).astype(jnp.int32)
    src_chunks = lax.dynamic_index_in_dim(jnp.asarray(_CHUNK_SRC), chip, 0, keepdims=False)
    pos_chunks = lax.dynamic_index_in_dim(jnp.asarray(_CHUNK_POS), chip, 0, keepdims=False)
    x0, target = x[0], loss_target[0]

    wt, mt, vt = (jnp.swapaxes(a, 1, 2) for a in (w_in, m_w_in, v_w_in))
    w0_mine, wloc_in1 = _cast_permute_w_in(wt, jnp.concatenate([chip_idx[:1], src_chunks]))
    c_all, wloc_out, w0 = _gather_inputs(c, w_out, w0_mine)
    wts = [w0.reshape(D_IN, D_MODEL), None]

    b_blk = lax.dynamic_slice_in_dim(b_ada, chip * W_ADA_BLK, W_ADA_BLK, axis=1)
    ada_part, cond = _ada_rows(c_all.reshape(N_DEV, D_MODEL), w_ada, b_blk)
    ada = jnp.moveaxis(_exchange_ada(ada_part), 0, 1).reshape(DEPTH, 3 * D_MODEL)
    shift = [ada[l:l + 1, 0:D_MODEL] for l in range(DEPTH)]
    scale = [ada[l:l + 1, D_MODEL:2 * D_MODEL] for l in range(DEPTH)]
    gate = [ada[l:l + 1, 2 * D_MODEL:] for l in range(DEPTH)]
    ng = [norm_gain[l:l + 1] for l in range(DEPTH)]

    qg2 = jnp.concatenate([q_gain, q_gain], axis=-1)
    kg2 = jnp.concatenate([k_gain, k_gain], axis=-1)
    ws_b = w_s.astype(BF16)
    wst_b = jnp.swapaxes(w_s, -1, -2).astype(BF16)
    bsp = jnp.repeat(jnp.swapaxes(b_s.reshape(DEPTH, 4, 2, BLOCK), -1, -2), HEAD_DIM, axis=-1)
    bias = jnp.asarray(_bias_table())

    def mix_args(l):
        return bias, sink[l], qg2[l:l + 1], kg2[l:l + 1], ws_b[l]

    w_out_shape = (W_OUT_BLK, D_MODEL)
    half = W_IN_BLK // 2
    proj0, wo0, w1 = _proj_fwd(x0, ng[0], scale[0], shift[0], wts[0], "proj_fwd_0",
                               jobs=[_job_gather([(wloc_out, 0)], [w_out_shape]),
                                     _job_gather_rows(wloc_in1, (0, half), shape=(W_IN_BLK, D_MODEL))])
    y0, *kept0, wo1, w1 = _mix_fwd(proj0, *mix_args(0), bsp[0], "mix_fwd_0",
                                   jobs=[_job_gather([(wloc_out, 1)], [w_out_shape]),
                                         _job_gather_rows(wloc_in1, (half, half), into=w1)])
    wts[1] = w1.reshape(D_IN, D_MODEL)
    wos = [_permute_w_out(w.reshape(D_MODEL, D_MODEL)) for w in (wo0, wo1)]
    x1, proj1 = _out_proj_fwd(y0, x0, gate[0], wos[0], ng[1], scale[1], shift[1], wts[1], "out_proj_fwd_01")
    y1, *kept1 = _mix_fwd(proj1, *mix_args(1), bsp[1], "mix_fwd_1")

    def blocks_out(gw):
        return _unpermute_w_out(gw).reshape(N_CHIPS, W_OUT_BLK, D_MODEL)

    def small_pack(dws, dbsp, dqg, dkg, dsink):
        b_s = jnp.swapaxes(dbsp[:, :, ::HEAD_DIM], -1, -2).reshape(8, BLOCK)
        return _pack_rows(dict(w_s=dws, b_s=b_s, q_gain=dqg[0, :HEAD_DIM], k_gain=dkg[0, :HEAD_DIM], sink=dsink[0, :N_HEADS]),
                          _SMALL_A)

    dx2, sq, dy1, gwo1, dgate1 = _out_loss_bwd(y1, x1, gate[1], wos[1], target, "out_loss_bwd_1")
    go1 = blocks_out(gwo1)
    dpb, dkv, p0, p2, dqg1, dkg1, dsink1, dws1, dbsp1, ro1 = _mix_bwd(
        dy1, proj1, *kept1, qg2[1:2], kg2[1:2], wst_b[1], "mix_bwd_1", jobs=[_job_swap(go1)])
    po1, = _pair_sum(go1, ro1, core_idx, "pair_sum_w_out_1")
    gwi1, dkvb, co1, gathered_a1 = _w_in_grad(
        dpb, dkv, p0, p2, x1, ng[1], scale[1], shift[1], "w_in_grad_1",
        jobs=[_job_scatter(po1), _job_all_gather(small_pack(dws1, dbsp1, dqg1, dkg1, dsink1))])
    gi1 = gwi1.reshape(N_CHIPS, W_IN_BLK, D_MODEL)
    fo1 = _chip_sum(po1, co1, chip_idx, "chip_sum_w_out_1")
    dx1, dsh1, dsc1, dng1, grad_wo1, ri1 = _proj_bwd(dpb, dkvb, x1, dx2, ng[1], scale[1], wts[1], "proj_bwd_1",
                                                     jobs=[_job_join(fo1), _job_swap(gi1)])
    pi1, = _pair_sum(gi1, ri1, core_idx, "pair_sum_w_in_1")

    dy0, gwo0, dgate0 = _out_bwd(dx1, y0, gate[0], wos[0], "out_bwd_0")
    go0 = blocks_out(gwo0)
    dpb, dkv, p0, p2, dqg0, dkg0, dsink0, dws0, dbsp0, ci1, ro0 = _mix_bwd(
        dy0, proj0, *kept0, qg2[0:1], kg2[0:1], wst_b[0], "mix_bwd_0", jobs=[_job_scatter(pi1), _job_swap(go0)])
    fi1 = _chip_sum(pi1, ci1, chip_idx, "chip_sum_w_in_1")
    po0, = _pair_sum(go0, ro0, core_idx, "pair_sum_w_out_0")

    gwi0, dkvb, grad_wi1, co0, gathered_a0 = _w_in_grad(
        dpb, dkv, p0, p2, x0, ng[0], scale[0], shift[0], "w_in_grad_0",
        jobs=[_job_join(fi1), _job_scatter(po0), _job_all_gather(small_pack(dws0, dbsp0, dqg0, dkg0, dsink0))])
    gi0 = gwi0.reshape(N_CHIPS, W_IN_BLK, D_MODEL)
    fo0 = _chip_sum(po0, co0, chip_idx, "chip_sum_w_out_0")

    ri0, grad_wo0 = _comm([_job_swap(gi0), _job_join(fo0)], "swap_w_in_0")
    pi0, pi0_send = _pair_sum(gi0, ri0, core_idx, "pair_sum_w_in_0", send_dtype=BF16)
    dx0, dsh0, dsc0, dng0, ci0 = _proj_bwd(dpb, dkvb, x0, dx1, ng[0], scale[0], wts[0], "proj_bwd_0",
                                           jobs=[_job_scatter(pi0_send)])
    fi0 = _chip_sum(pi0, ci0, chip_idx, "chip_sum_w_in_0")

    small_g = dict(
        b_ada=jnp.stack([jnp.concatenate([dsh0, dsc0, dgate0], axis=-1)[0], jnp.concatenate([dsh1, dsc1, dgate1], axis=-1)[0]]),
        norm_gain=jnp.stack([dng0[0], dng1[0]]), sq_err=sq[0])
    none = jnp.zeros((1,), F32)
    small_w = dict(w_s=w_s, b_s=b_s, b_ada=b_ada, norm_gain=norm_gain, q_gain=q_gain, k_gain=k_gain, sink=sink, sq_err=none)
    small_m = dict(w_s=m_w_s, b_s=m_b_s, b_ada=m_b_ada, norm_gain=m_norm_gain, q_gain=m_q_gain, k_gain=m_k_gain, sink=m_sink,
                   sq_err=none)
    small_v = dict(w_s=v_w_s, b_s=v_b_s, b_ada=v_b_ada, norm_gain=v_norm_gain, q_gain=v_q_gain, k_gain=v_k_gain, sink=v_sink,
                   sq_err=none)
    grad_wi0, gathered_b = _comm([_job_join(fi0), _job_all_gather(_pack_rows(small_g, _SMALL_B))], "join_w_in_0")
    packed = _small_sum_adamw([a.reshape(N_DEV, -1, 128) for a in (gathered_a0, gathered_a1, gathered_b)],
                              _pack_small(small_w), _pack_small(small_m), _pack_small(small_v))
    shapes = {k: a.shape for k, a in small_w.items()}
    sg, sd, sm, sv = (_unpack_small(p, shapes) for p in packed)
    loss = 0.5 * sg["sq_err"][0]

    dada_all = gathered_b.reshape(N_DEV, -1, 128)[:, 0:DEPTH * 24].reshape(N_DEV, DEPTH, 3 * D_MODEL)
    dada_blk = jnp.moveaxis(lax.dynamic_slice_in_dim(dada_all, chip * W_ADA_BLK, W_ADA_BLK, axis=2), 0, 1)
    pad = BLOCK - N_DEV
    ada_out = _w_ada_grad_adamw(
        jnp.pad(cond.T, ((0, 0), (0, pad))).astype(BF16), jnp.pad(dada_blk, ((0, 0), (0, pad), (0, 0))),
        w_ada, m_w_ada, v_w_ada)

    in_out = [jnp.swapaxes(a, 1, 2) for a in _adamw_w_in(wt, grad_wi0, grad_wi1, mt, vt, pos_chunks)]
    out_out = _adamw_w_out(w_out, grad_wo0, grad_wo1, m_w_out, v_w_out)

    def ordered(k):
        small = (sg, sd, sm, sv)[k]
        return (ada_out[k], small["b_ada"], small["norm_gain"], in_out[k], small["q_gain"], small["k_gain"], small["sink"],
                small["w_s"], small["b_s"], out_out[k])

    return (loss, dx0[None], *ordered(0), *ordered(1), *ordered(2), *ordered(3))
```

```python
import numpy as np

import jax
import jax.numpy as jnp
from jax import lax
from jax.experimental import pallas as pl
from jax.experimental.pallas import tpu as pltpu

F32 = jnp.float32
BF16 = jnp.bfloat16

D_MODEL = 1024
DEPTH = 2
HEAD_DIM = 64
N_HEADS = 8
BLOCK = 128
SUB = 4
TILE = SUB * BLOCK
STACK = 8 * BLOCK
D_ATTN = 512
D_KV = 128
D_IN = 2816
N_CHIPS = 4
N_DEV = 8
W_IN_BLK = D_IN // N_CHIPS
W_OUT_BLK = D_MODEL // N_CHIPS
W_ADA_BLK = 3 * D_MODEL // N_CHIPS
CHUNK_ROWS = HEAD_DIM
N_CHUNKS = W_IN_BLK // CHUNK_ROWS
EPS = 1e-6
NEG_INF = -1e30

C_Q, C_K, C_V, C_GA, C_U, C_VG, C_GG = 0, 512, 640, 768, 1280, 1792, 2304

ADAM_LR = 0.001
ADAM_B1 = 0.9
ADAM_B2 = 0.999
ADAM_EPS = 1e-08
ADAM_WD = 0.01
ADAM_STEP = 10

MESH = pl.DeviceIdType.MESH
MIB = 1024 * 1024
ANY = pl.BlockSpec(memory_space=pl.ANY)
VMEM = pl.BlockSpec(memory_space=pltpu.VMEM)

NT_DIMS = (((1,), (1,)), ((), ()))
TN_DIMS = (((0,), (0,)), ((), ()))

_PAIR_ORDER = (0, 4, 1, 5, 2, 6, 3, 7)
_CHUNK_SRC = np.array([
    list(_PAIR_ORDER) + [8, 9, 10],
    [0] + [1 + h for h in _PAIR_ORDER] + [9, 10],
    list(range(N_CHUNKS)),
    list(range(N_CHUNKS)),
], np.int32)
_CHUNK_POS = np.argsort(_CHUNK_SRC, axis=1).astype(np.int32)


def _bias_table():
    i = np.arange(N_HEADS * BLOCK)[:, None]
    j = np.arange(3 * BLOCK)[None, :]
    dist = np.abs(j - BLOCK - (i % BLOCK))
    slope = 2.0 ** -(i // BLOCK + 1.0)
    inner = np.where(dist <= BLOCK, -(slope * dist), NEG_INF)
    first = np.where(j >= BLOCK, inner, NEG_INF)
    last = np.where(j < 2 * BLOCK, inner, NEG_INF)
    return np.stack([first, inner, last]).astype(np.float32)


def _full(shape):
    n = len(shape)
    return pl.BlockSpec(shape, lambda *_: (0,) * n)


def _sds(shape, dtype=F32):
    return jax.ShapeDtypeStruct(shape, dtype)


def _coords():
    return lax.axis_index("x"), lax.axis_index("y"), lax.axis_index("c")


def _other_chips(x, y):
    return [(1 - x, y), (x, 1 - y), (1 - x, 1 - y)]


def _in_hbm(*operands):
    return [pltpu.with_memory_space_constraint(a, pltpu.HBM) if a.size * a.dtype.itemsize >= MIB // 4 else a
            for a in operands]


def _remote(src, dst, sems, dev):
    return pltpu.make_async_remote_copy(src_ref=src, dst_ref=dst, send_sem=sems[0], recv_sem=sems[1],
                                        device_id=dev, device_id_type=MESH)


class _Job:
    def __init__(self, inputs, out_shapes, n_remote, n_local, make, then=None, in_place=False):
        self.inputs, self.out_shapes, self.n_remote, self.n_local, self.make = inputs, out_shapes, n_remote, n_local, make
        self.then = then
        self.in_place = in_place


def _job_aliases(jobs, in_base, out_base):
    aliases, a, b = {}, 0, 0
    for j in jobs:
        if j.in_place:
            aliases.update({in_base + a + k: out_base + b + k for k in range(len(j.inputs))})
        a, b = a + len(j.inputs), b + len(j.out_shapes)
    return aliases


def _job_copies(jobs, jin, jout, sems, second=False):
    send, recv, loc = sems
    res, a, b, r, l = [], 0, 0, 0, 0
    for j in jobs:
        build = j.then if second else j.make
        if build is not None:
            res += build(jin[a:a + len(j.inputs)], jout[b:b + len(j.out_shapes)],
                         lambda k, r=r: (send.at[r + k], recv.at[r + k]), lambda k, l=l: loc.at[l + k])
        a, b, r, l = a + len(j.inputs), b + len(j.out_shapes), r + j.n_remote, l + j.n_local
    return res


def _run(copies):
    for cp in copies:
        cp.start()
    for cp in copies:
        cp.wait()


def _job_gather(sources, shapes):
    n = len(sources)

    def make(ins, outs, rsem, lsem):
        x, y, c = _coords()
        j = 2 * x + y
        res = []
        for t, ((_, layer), src, dst) in enumerate(zip(sources, ins, outs)):
            src = src if layer is None else src.at[layer]
            res.append(pltpu.make_async_copy(src, dst.at[j], lsem(t)))
            for k, chip in enumerate(_other_chips(x, y)):
                res.append(_remote(src, dst.at[j], rsem(3 * t + k), (*chip, c)))
        return res

    return _Job([a for a, _ in sources], [_sds((N_CHIPS,) + s, BF16) for s in shapes], 3 * n, n, make)


def _job_swap(g):
    _, rows, cols = g.shape
    half = rows // 2

    def make(ins, outs, rsem, lsem):
        x, y, c = _coords()
        return [_remote(ins[0].at[:, pl.ds((1 - c) * half, half), :], outs[0], rsem(0), (x, y, 1 - c))]

    return _Job([g], [_sds((N_CHIPS, half, cols))], 1, 0, make)


def _job_scatter(p):
    def make(ins, outs, rsem, lsem):
        x, y, c = _coords()
        return [_remote(ins[0].at[2 * chip[0] + chip[1]], outs[0].at[k], rsem(k), (*chip, c))
                for k, chip in enumerate(_other_chips(x, y))]

    return _Job([p], [_sds((3,) + p.shape[1:], p.dtype)], 3, 0, make)


def _job_all_gather(blk):
    m_per = blk.shape[0]

    def rows(ref, px, py, pc):
        return ref.at[pl.ds((4 * px + 2 * py + pc) * m_per, m_per), :]

    def make(ins, outs, rsem, lsem):
        x, y, c = _coords()
        res = [pltpu.make_async_copy(ins[0], rows(outs[0], x, y, c), lsem(0)),
               _remote(ins[0], rows(outs[0], x, y, c), rsem(0), (x, y, 1 - c))]
        res += [_remote(ins[0], rows(outs[0], x, y, c), rsem(1 + k), (*chip, c)) for k, chip in enumerate(_other_chips(x, y))]
        return res

    def then(ins, outs, rsem, lsem):
        x, y, c = _coords()
        return [_remote(rows(outs[0], *chip, c), rows(outs[0], *chip, c), rsem(4 + k), (x, y, 1 - c))
                for k, chip in enumerate(_other_chips(x, y))]

    return _Job([blk], [_sds((N_DEV * m_per, blk.shape[1]), blk.dtype)], 7, 1, make, then)


def _job_join(f):
    half = f.shape[0] // 2

    def make(ins, outs, rsem, lsem):
        x, y, c = _coords()
        mine = pl.ds(c * half, half)
        return [_remote(ins[0].at[mine, :], outs[0].at[mine, :], rsem(0), (x, y, 1 - c))]

    return _Job([f], [_sds(f.shape, f.dtype)], 1, 0, make, in_place=True)


def _pallas(body, *, name, grid, in_specs, out_specs, out_shape, operands, vmem_mib, jobs=(), scratch=()):
    in_specs, out_specs, out_shape = list(in_specs), list(out_specs), list(out_shape)
    n_in, n_out = len(in_specs), len(out_specs)
    j_in = [a for j in jobs for a in j.inputs]
    j_out = [s for j in jobs for s in j.out_shapes]
    n_rem = max(1, sum(j.n_remote for j in jobs))
    n_loc = max(1, sum(j.n_local for j in jobs))
    sems = [pltpu.SemaphoreType.DMA((n_rem,)), pltpu.SemaphoreType.DMA((n_rem,)),
            pltpu.SemaphoreType.DMA((n_loc,))] if jobs else []
    scratch = list(scratch) + sems

    def wrapped(*refs):
        ins = refs[:n_in]
        jin = refs[n_in:n_in + len(j_in)]
        outs = refs[n_in + len(j_in):n_in + len(j_in) + n_out]
        jout = refs[n_in + len(j_in) + n_out:n_in + len(j_in) + n_out + len(j_out)]
        own = refs[n_in + len(j_in) + n_out + len(j_out):len(refs) - len(sems)]

        if jobs:
            first = last = None
            for d, n in enumerate(grid):
                f, e = pl.program_id(d) == 0, pl.program_id(d) == n - 1
                first, last = (f, e) if first is None else (first & f, last & e)

            @pl.when(first)
            def _():
                for cp in _job_copies(jobs, jin, jout, refs[-3:]):
                    cp.start()

        body(*ins, *outs, *own)

        if jobs:
            @pl.when(last)
            def _():
                for cp in _job_copies(jobs, jin, jout, refs[-3:]):
                    cp.wait()
                _run(_job_copies(jobs, jin, jout, refs[-3:], second=True))

    return pl.pallas_call(
        wrapped, name=name, grid=grid,
        in_specs=in_specs + [ANY] * len(j_in), out_specs=out_specs + [ANY] * len(j_out),
        out_shape=out_shape + j_out, scratch_shapes=scratch, input_output_aliases=_job_aliases(jobs, n_in, n_out),
        compiler_params=pltpu.CompilerParams(dimension_semantics=("arbitrary",) * len(grid),
                                             vmem_limit_bytes=vmem_mib * MIB),
    )(*_in_hbm(*operands, *j_in))


def _comm(jobs, name):
    j_in = [a for j in jobs for a in j.inputs]
    j_out = [s for j in jobs for s in j.out_shapes]
    n_rem = max(1, sum(j.n_remote for j in jobs))
    n_loc = max(1, sum(j.n_local for j in jobs))

    def body(*refs):
        jin, jout = refs[:len(j_in)], refs[len(j_in):len(j_in) + len(j_out)]
        _run(_job_copies(jobs, jin, jout, refs[-3:]))
        _run(_job_copies(jobs, jin, jout, refs[-3:], second=True))

    return pl.pallas_call(
        body, name=name, in_specs=[ANY] * len(j_in), out_specs=[ANY] * len(j_out), out_shape=j_out,
        scratch_shapes=[pltpu.SemaphoreType.DMA((n_rem,)), pltpu.SemaphoreType.DMA((n_rem,)),
                        pltpu.SemaphoreType.DMA((n_loc,))],
        input_output_aliases=_job_aliases(jobs, 0, 0),
    )(*_in_hbm(*j_in))


def _sigmoid(x):
    return 1.0 / (1.0 + jnp.exp(-x))


def _lo_mask(shape):
    return lax.broadcasted_iota(jnp.int32, shape, len(shape) - 1) < HEAD_DIM


def _half_sum(x, lo):
    a = jnp.sum(jnp.where(lo, x, 0.0), axis=-1, keepdims=True)
    b = jnp.sum(jnp.where(lo, 0.0, x), axis=-1, keepdims=True)
    return jnp.where(lo, a, b)


def _half_rms_scale(x, lo):
    return lax.rsqrt(_half_sum(x * x, lo) * (1.0 / HEAD_DIM) + EPS)


def _stack_heads(pairs, lo):
    return jnp.concatenate([jnp.where(lo, t, 0.0) for t in pairs] + [jnp.where(lo, 0.0, t) for t in pairs], axis=0)


def _unstack_pair(stack, p, lo):
    return jnp.where(lo, stack[BLOCK * p:BLOCK * (p + 1)], stack[BLOCK * (4 + p):BLOCK * (5 + p)])


def _attention_probs(q_stack, kn, bias_ref, sink_ref):
    rows = N_HEADS * BLOCK
    s = lax.dot_general(q_stack, kn, NT_DIMS, preferred_element_type=F32) + bias_ref[...]
    sink = jnp.concatenate([jnp.full((BLOCK, BLOCK), sink_ref[h], F32) for h in range(N_HEADS)], axis=0)
    cols = [s[:, BLOCK * j:BLOCK * (j + 1)] for j in range(3)]
    top = jnp.max(jnp.maximum(jnp.maximum(cols[0], cols[1]), cols[2]), axis=-1, keepdims=True)
    m = jnp.maximum(jnp.broadcast_to(top, (rows, BLOCK)), sink)
    e = [jnp.exp(c - m) for c in cols]
    es = jnp.exp(sink - m)
    inv = 1.0 / (jnp.broadcast_to(jnp.sum((e[0] + e[1]) + e[2], axis=-1, keepdims=True), (rows, BLOCK)) + es)
    return jnp.concatenate([c * inv for c in e], axis=1), es * inv


def _kv_rows(cur_ref, pkv_ref, nkv_ref):
    k = jnp.concatenate([pkv_ref[:, 0:D_KV], cur_ref[:, C_K:C_K + D_KV], nkv_ref[:, 0:D_KV]], axis=0)
    v = jnp.concatenate([pkv_ref[:, D_KV:2 * D_KV], cur_ref[:, C_V:C_V + D_KV], nkv_ref[:, D_KV:2 * D_KV]], axis=0)
    return k, v


def _overlap_add(parts):
    blocks = []
    for j in range(SUB + 2):
        terms = [parts[b][BLOCK * (j - b):BLOCK * (j - b + 1)] for b in range(SUB) if 0 <= j - b <= 2]
        total = terms[0]
        for t in terms[1:]:
            total = total + t
        blocks.append(total)
    return jnp.concatenate(blocks, axis=0)


def _mix_specs(nt):
    cur = pl.BlockSpec((TILE, D_IN), lambda i: (i, 0))
    kv_col = C_K // (2 * D_KV)
    pkv = pl.BlockSpec((BLOCK, 2 * D_KV), lambda i: (jnp.maximum(i * SUB - 1, 0), kv_col))
    nkv = pl.BlockSpec((BLOCK, 2 * D_KV), lambda i: (jnp.minimum((i + 1) * SUB, nt * SUB - 1), kv_col))
    table = (None, N_HEADS * BLOCK, 3 * BLOCK)
    first = pl.BlockSpec(table, lambda i: (jnp.where(i == 0, 0, 1), 0, 0))
    inner = pl.BlockSpec(table, lambda i: (1, 0, 0))
    last = pl.BlockSpec(table, lambda i: (jnp.where(i == nt - 1, 2, 1), 0, 0))
    return cur, pkv, nkv, [first] + [inner] * (SUB - 2) + [last]


def _proj_fwd(x, ng, scale, shift, wt, name, jobs=()):
    s = x.shape[0]
    ts = min(512, s)

    def body(x_ref, ng_ref, sc_ref, sh_ref, w_ref, o_ref):
        xv = x_ref[...]
        r = lax.rsqrt(jnp.mean(xv * xv, axis=-1, keepdims=True) + EPS)
        h = ((xv * r) * ng_ref[...]) * (1.0 + sc_ref[...]) + sh_ref[...]
        o_ref[...] = lax.dot_general(h.astype(BF16), w_ref[...], NT_DIMS, preferred_element_type=F32)

    vec = _full((1, D_MODEL))
    return _pallas(
        body, name=name, grid=(s // ts,),
        in_specs=[pl.BlockSpec((ts, D_MODEL), lambda i: (i, 0)), vec, vec, vec, _full((D_IN, D_MODEL))],
        out_specs=[pl.BlockSpec((ts, D_IN), lambda i: (i, 0))], out_shape=[_sds((s, D_IN))],
        operands=(x, ng, scale, shift, wt), vmem_mib=48, jobs=jobs)


def _diagonal():
    return lax.broadcasted_iota(jnp.int32, (BLOCK, BLOCK), 0) == lax.broadcasted_iota(jnp.int32, (BLOCK, BLOCK), 1)


def _column_as_row(wide, eye):
    return jnp.sum(jnp.where(eye, wide, 0.0), axis=0, keepdims=True)


def _mix_fwd(proj, bias, sink, qg2, kg2, ws, bsp, name, jobs=()):
    s = proj.shape[0]
    nt = s // TILE

    def body(sink_ref, cur_ref, pkv_ref, nkv_ref, *rest):
        bias_refs = rest[:SUB]
        qg_ref, kg_ref, ws_ref, bsp_ref, y_ref, p_ref, ps_ref, attn_ref, sv_ref = rest[SUB:]
        lo = _lo_mask((BLOCK, BLOCK))
        eye = _diagonal()
        lo_kv = _lo_mask((TILE + 2 * BLOCK, BLOCK))
        k_all, v_all = _kv_rows(cur_ref, pkv_ref, nkv_ref)
        kn_all = ((k_all * _half_rms_scale(k_all, lo_kv)) * kg_ref[...]).astype(BF16)
        vb_all = v_all.astype(BF16)
        for b in range(SUB):
            rows = slice(BLOCK * b, BLOCK * (b + 1))
            window = slice(BLOCK * b, BLOCK * (b + 3))
            qn = []
            for p in range(4):
                q = cur_ref[rows, C_Q + BLOCK * p:C_Q + BLOCK * (p + 1)]
                qn.append(((q * _half_rms_scale(q, lo)) * qg_ref[...]) * 0.125)
            q_stack = _stack_heads(qn, lo).astype(BF16)
            prob, psink = _attention_probs(q_stack, kn_all[window], bias_refs[b], sink_ref)
            pb = prob.astype(BF16)
            p_ref[STACK * b:STACK * (b + 1), :] = pb
            ps_ref[N_HEADS * b:N_HEADS * (b + 1), :] = jnp.concatenate(
                [_column_as_row(psink[BLOCK * h:BLOCK * (h + 1)], eye) for h in range(N_HEADS)], axis=0)
            o_stack = jnp.dot(pb, vb_all[window], preferred_element_type=F32)
            for p in range(4):
                g = cur_ref[rows, C_GA + BLOCK * p:C_GA + BLOCK * (p + 1)]
                attn = _unstack_pair(o_stack, p, lo)
                attn_ref[rows, BLOCK * p:BLOCK * (p + 1)] = attn.astype(BF16)
                y_ref[rows, BLOCK * p:BLOCK * (p + 1)] = (attn * (g * _sigmoid(g))).astype(BF16)

        for p in range(4):
            cols = slice(C_VG + BLOCK * p, C_VG + BLOCK * (p + 1))
            vn = []
            for b in range(SUB):
                vg = cur_ref[BLOCK * b:BLOCK * (b + 1), cols]
                vn.append((vg * _half_rms_scale(vg, lo)).astype(BF16))
            vn = jnp.concatenate(vn, axis=1)
            sv_a = jnp.dot(ws_ref[2 * p], vn, preferred_element_type=F32)
            sv_b = jnp.dot(ws_ref[2 * p + 1], vn, preferred_element_type=F32)
            for b in range(SUB):
                rows = slice(BLOCK * b, BLOCK * (b + 1))
                lanes = slice(BLOCK * b, BLOCK * (b + 1))
                sv = jnp.where(lo, sv_a[:, lanes], sv_b[:, lanes]) + bsp_ref[p]
                sv_ref[rows, BLOCK * p:BLOCK * (p + 1)] = sv.astype(BF16)
                u = cur_ref[rows, C_U + BLOCK * p:C_U + BLOCK * (p + 1)]
                g = cur_ref[rows, C_GG + BLOCK * p:C_GG + BLOCK * (p + 1)]
                y_ref[rows, D_ATTN + BLOCK * p:D_ATTN + BLOCK * (p + 1)] = ((u * sv) * (g * _sigmoid(g))).astype(BF16)

    cur, pkv, nkv, bias_specs = _mix_specs(nt)
    nb = nt * SUB
    half = pl.BlockSpec((TILE, D_ATTN), lambda i: (i, 0))
    return _pallas(
        body, name=name, grid=(nt,),
        in_specs=[pl.BlockSpec(memory_space=pltpu.SMEM), cur, pkv, nkv, *bias_specs, _full((1, BLOCK)), _full((1, BLOCK)),
                  _full((8, BLOCK, BLOCK)), _full((4, BLOCK, BLOCK))],
        out_specs=[pl.BlockSpec((TILE, D_MODEL), lambda i: (i, 0)), pl.BlockSpec((SUB * STACK, 3 * BLOCK), lambda i: (i, 0)),
                   pl.BlockSpec((SUB * N_HEADS, BLOCK), lambda i: (i, 0)), half, half],
        out_shape=[_sds((s, D_MODEL), BF16), _sds((nb * STACK, 3 * BLOCK), BF16), _sds((nb * N_HEADS, BLOCK)),
                   _sds((s, D_ATTN), BF16), _sds((s, D_ATTN), BF16)],
        operands=(sink, proj, proj, proj, *([bias] * SUB), qg2, kg2, ws, bsp), vmem_mib=56, jobs=jobs)


def _out_proj_fwd(y, x, gate, w_out, ng, scale, shift, wt, name):
    s = x.shape[0]
    ts = min(512, s)

    def body(y_ref, x_ref, g_ref, w_ref, ng_ref, sc_ref, sh_ref, wt_ref, xn_ref, p_ref):
        xv = x_ref[...] + g_ref[...] * jnp.dot(y_ref[...], w_ref[...], preferred_element_type=F32)
        xn_ref[...] = xv
        r = lax.rsqrt(jnp.mean(xv * xv, axis=-1, keepdims=True) + EPS)
        h = ((xv * r) * ng_ref[...]) * (1.0 + sc_ref[...]) + sh_ref[...]
        p_ref[...] = lax.dot_general(h.astype(BF16), wt_ref[...], NT_DIMS, preferred_element_type=F32)

    row = pl.BlockSpec((ts, D_MODEL), lambda i: (i, 0))
    vec = _full((1, D_MODEL))
    return _pallas(
        body, name=name, grid=(s // ts,),
        in_specs=[row, row, vec, _full((D_MODEL, D_MODEL)), vec, vec, vec, _full((D_IN, D_MODEL))],
        out_specs=[row, pl.BlockSpec((ts, D_IN), lambda i: (i, 0))], out_shape=[_sds((s, D_MODEL)), _sds((s, D_IN))],
        operands=(y, x, gate, w_out, ng, scale, shift, wt), vmem_mib=56)


def _out_loss_bwd(y, x, gate, w_out, target, name):
    s = x.shape[0]
    ts = min(512, s)
    steps = s // ts

    def body(y_ref, x_ref, g_ref, w_ref, t_ref, dx_ref, sq_ref, dy_ref, gw_ref, dg_ref):
        @pl.when(pl.program_id(0) == 0)
        def _():
            sq_ref[...] = jnp.zeros_like(sq_ref)
            gw_ref[...] = jnp.zeros_like(gw_ref)

        yv = y_ref[...]
        out = x_ref[...] + g_ref[...] * jnp.dot(yv, w_ref[...], preferred_element_type=F32)
        diff = out - t_ref[...]
        dx = diff * (1.0 / D_MODEL)
        dx_ref[...] = dx
        per_token = jnp.sum(diff * diff, axis=-1, keepdims=True) * (1.0 / D_MODEL)
        sq_ref[...] += jnp.sum(per_token, axis=0, keepdims=True)
        dy_ref[...] = lax.dot_general((dx * g_ref[...]).astype(BF16), w_ref[...], NT_DIMS, preferred_element_type=F32)
        gw_ref[...] += lax.dot_general(yv, dx.astype(BF16), TN_DIMS, preferred_element_type=F32)

        @pl.when(pl.program_id(0) == steps - 1)
        def _():
            m = gw_ref[...]
            dg_ref[...] = jnp.sum(w_ref[...].astype(F32) * m, axis=0, keepdims=True)
            gw_ref[...] = m * g_ref[...]

    row = pl.BlockSpec((ts, D_MODEL), lambda i: (i, 0))
    return _pallas(
        body, name=name, grid=(s // ts,), in_specs=[row, row, _full((1, D_MODEL)), _full((D_MODEL, D_MODEL)), row],
        out_specs=[row, _full((1, 1)), row, _full((D_MODEL, D_MODEL)), _full((1, D_MODEL))],
        out_shape=[_sds((s, D_MODEL)), _sds((1, 1)), _sds((s, D_MODEL)), _sds((D_MODEL, D_MODEL)), _sds((1, D_MODEL))],
        operands=(y, x, gate, w_out, target), vmem_mib=48)


def _out_bwd(dxo, y, gate, w_out, name, jobs=()):
    s = dxo.shape[0]
    ts = min(512, s)
    steps = s // ts

    def body(dx_ref, y_ref, g_ref, w_ref, dy_ref, gw_ref, dg_ref):
        @pl.when(pl.program_id(0) == 0)
        def _():
            gw_ref[...] = jnp.zeros_like(gw_ref)

        dx = dx_ref[...]
        dy_ref[...] = lax.dot_general((dx * g_ref[...]).astype(BF16), w_ref[...], NT_DIMS, preferred_element_type=F32)
        gw_ref[...] += lax.dot_general(y_ref[...], dx.astype(BF16), TN_DIMS, preferred_element_type=F32)

        @pl.when(pl.program_id(0) == steps - 1)
        def _():
            m = gw_ref[...]
            dg_ref[...] = jnp.sum(w_ref[...].astype(F32) * m, axis=0, keepdims=True)
            gw_ref[...] = m * g_ref[...]

    row = pl.BlockSpec((ts, D_MODEL), lambda i: (i, 0))
    return _pallas(
        body, name=name, grid=(s // ts,), in_specs=[row, row, _full((1, D_MODEL)), _full((D_MODEL, D_MODEL))],
        out_specs=[row, _full((D_MODEL, D_MODEL)), _full((1, D_MODEL))],
        out_shape=[_sds((s, D_MODEL)), _sds((D_MODEL, D_MODEL)), _sds((1, D_MODEL))],
        operands=(dxo, y, gate, w_out), vmem_mib=48, jobs=jobs)


def _mix_bwd(dy, proj, probs, psink, attn, sv, qg2, kg2, wst, name, jobs=()):
    s = proj.shape[0]
    nt = s // TILE

    def body(dy_ref, cur_ref, pkv_ref, nkv_ref, p_ref, ps_ref, attn_ref, sv_ref, qg_ref, kg_ref, wst_ref,
             dpb_ref, dkv_ref, p0_ref, p2_ref, dqg_ref, dkg_ref, dsink_ref, dws_ref, dbsp_ref):
        def put(rows, col, value):
            dpb_ref[rows, col:col + BLOCK] = value.astype(BF16)

        @pl.when(pl.program_id(0) == 0)
        def _():
            dqg_ref[...] = jnp.zeros_like(dqg_ref)
            dkg_ref[...] = jnp.zeros_like(dkg_ref)
            dsink_ref[...] = jnp.zeros_like(dsink_ref)
            dws_ref[...] = jnp.zeros_like(dws_ref)
            dbsp_ref[...] = jnp.zeros_like(dbsp_ref)

        lo = _lo_mask((BLOCK, BLOCK))
        lo_kv = _lo_mask((TILE + 2 * BLOCK, BLOCK))
        eye = _diagonal()
        lane_row = lax.broadcasted_iota(jnp.int32, (1, BLOCK), 1)
        qg = qg_ref[...]
        kg = kg_ref[...]

        k_all, v_all = _kv_rows(cur_ref, pkv_ref, nkv_ref)
        rk = _half_rms_scale(k_all, lo_kv)
        khat = k_all * rk
        kn_all = (khat * kg).astype(BF16)
        vb_all = v_all.astype(BF16)

        dkn_parts, dv_parts = [], []
        dsink = jnp.zeros((1, BLOCK), F32)
        dqg = jnp.zeros((1, BLOCK), F32)
        for b in range(SUB):
            rows = slice(BLOCK * b, BLOCK * (b + 1))
            window = slice(BLOCK * b, BLOCK * (b + 3))
            kn, vb = kn_all[window], vb_all[window]

            qhat, rq = [], []
            for p in range(4):
                q = cur_ref[rows, C_Q + BLOCK * p:C_Q + BLOCK * (p + 1)]
                r = _half_rms_scale(q, lo)
                rq.append(r)
                qhat.append(q * r)
            q_stack = _stack_heads([(qh * qg) * 0.125 for qh in qhat], lo).astype(BF16)
            pb = p_ref[STACK * b:STACK * (b + 1), :]
            prob = pb.astype(F32)

            dout = []
            for p in range(4):
                g = cur_ref[rows, C_GA + BLOCK * p:C_GA + BLOCK * (p + 1)]
                sg = _sigmoid(g)
                dya = dy_ref[rows, BLOCK * p:BLOCK * (p + 1)]
                attn = attn_ref[rows, BLOCK * p:BLOCK * (p + 1)]
                put(rows, C_GA + BLOCK * p, dya * attn * (sg * (1.0 + g * (1.0 - sg))))
                dout.append(dya * (g * sg))
            do_stack = _stack_heads(dout, lo).astype(BF16)
            dp = lax.dot_general(do_stack, vb, NT_DIMS, preferred_element_type=F32)
            delta = jnp.sum(prob * dp, axis=-1, keepdims=True)
            dsb = (prob * (dp - delta)).astype(BF16)

            for h in range(N_HEADS):
                delta_row = _column_as_row(jnp.broadcast_to(delta[BLOCK * h:BLOCK * (h + 1)], (BLOCK, BLOCK)), eye)
                tot = jnp.sum(ps_ref[N_HEADS * b + h:N_HEADS * b + h + 1, :] * delta_row, axis=-1, keepdims=True)
                dsink = dsink - jnp.where(lane_row == h, tot, 0.0)

            dq_stack = jnp.dot(dsb, kn, preferred_element_type=F32) * 0.125
            dkn_parts.append(lax.dot_general(dsb, q_stack, TN_DIMS, preferred_element_type=F32))
            dv_parts.append(lax.dot_general(pb, do_stack, TN_DIMS, preferred_element_type=F32))

            for p in range(4):
                dqn = _unstack_pair(dq_stack, p, lo)
                qh = qhat[p]
                dqg = dqg + jnp.sum(dqn * qh, axis=0, keepdims=True)
                dqh = dqn * qg
                mean = _half_sum(dqh * qh, lo) * (1.0 / HEAD_DIM)
                put(rows, C_Q + BLOCK * p, rq[p] * (dqh - qh * mean))

        dws_new, dbs_new = [], []
        for p in range(4):
            rs, vnfs, vns, dsvs, dbs = [], [], [], [], None
            for b in range(SUB):
                rows = slice(BLOCK * b, BLOCK * (b + 1))
                vg = cur_ref[rows, C_VG + BLOCK * p:C_VG + BLOCK * (p + 1)]
                r = _half_rms_scale(vg, lo)
                vnf = vg * r
                sv = sv_ref[rows, BLOCK * p:BLOCK * (p + 1)]
                u = cur_ref[rows, C_U + BLOCK * p:C_U + BLOCK * (p + 1)]
                g = cur_ref[rows, C_GG + BLOCK * p:C_GG + BLOCK * (p + 1)]
                sg = _sigmoid(g)
                dym = dy_ref[rows, D_ATTN + BLOCK * p:D_ATTN + BLOCK * (p + 1)]
                put(rows, C_GG + BLOCK * p, dym * (u * sv) * (sg * (1.0 + g * (1.0 - sg))))
                dgm = dym * (g * sg)
                put(rows, C_U + BLOCK * p, dgm * sv)
                dsv = dgm * u
                term = jnp.where(lo, jnp.sum(jnp.where(lo, dsv, 0.0), axis=-1, keepdims=True),
                                 jnp.sum(jnp.where(lo, 0.0, dsv), axis=-1, keepdims=True))
                dbs = term if dbs is None else dbs + term
                rs.append(r)
                vnfs.append(vnf)
                vns.append(vnf.astype(BF16))
                dsvs.append(dsv)
            dbs_new.append(dbs)
            vn = jnp.concatenate(vns, axis=1)
            dsv = jnp.concatenate(dsvs, axis=1)
            lo_t = (lax.broadcasted_iota(jnp.int32, dsv.shape, 1) & (BLOCK - 1)) < HEAD_DIM
            dws_new.append(lax.dot_general(jnp.where(lo_t, dsv, 0.0).astype(BF16), vn, NT_DIMS, preferred_element_type=F32))
            dws_new.append(lax.dot_general(jnp.where(lo_t, 0.0, dsv).astype(BF16), vn, NT_DIMS, preferred_element_type=F32))
            dsvb = dsv.astype(BF16)
            dvn_a = jnp.dot(wst_ref[2 * p], dsvb, preferred_element_type=F32)
            dvn_b = jnp.dot(wst_ref[2 * p + 1], dsvb, preferred_element_type=F32)
            for b in range(SUB):
                lanes = slice(BLOCK * b, BLOCK * (b + 1))
                dvn = jnp.where(lo, dvn_a[:, lanes], dvn_b[:, lanes])
                mean = _half_sum(dvn * vnfs[b], lo) * (1.0 / HEAD_DIM)
                put(slice(BLOCK * b, BLOCK * (b + 1)), C_VG + BLOCK * p, rs[b] * (dvn - vnfs[b] * mean))

        dsink_ref[...] += dsink
        dqg = jnp.broadcast_to(dqg, (8, BLOCK))
        dqg_ref[...] += dqg + pltpu.roll(dqg, HEAD_DIM, 1)

        dkn = _overlap_add(dkn_parts)
        dv = _overlap_add(dv_parts)
        dkg = jnp.broadcast_to(jnp.sum(dkn * khat, axis=0, keepdims=True), (8, BLOCK))
        dkg_ref[...] += dkg + pltpu.roll(dkg, HEAD_DIM, 1)
        dkh = dkn * kg
        dk = rk * (dkh - khat * (_half_sum(dkh * khat, lo_kv) * (1.0 / HEAD_DIM)))
        dpb_ref[:, C_K:C_GA] = jnp.zeros((TILE, 2 * D_KV), BF16)
        dkv_ref[:, 0:D_KV] = dk[BLOCK:BLOCK + TILE]
        dkv_ref[:, D_KV:2 * D_KV] = dv[BLOCK:BLOCK + TILE]
        p0_ref[:, 0:D_KV] = dk[0:BLOCK]
        p0_ref[:, D_KV:2 * D_KV] = dv[0:BLOCK]
        p2_ref[:, 0:D_KV] = dk[BLOCK + TILE:]
        p2_ref[:, D_KV:2 * D_KV] = dv[BLOCK + TILE:]
        for g, new in enumerate(dws_new):
            dws_ref[g] += new
        for p, new in enumerate(dbs_new):
            dbsp_ref[p] += new

    cur, pkv, nkv, _ = _mix_specs(nt)
    kv_blk = (BLOCK, 2 * D_KV)
    half = pl.BlockSpec((TILE, D_ATTN), lambda i: (i, 0))
    return _pallas(
        body, name=name, grid=(nt,),
        in_specs=[pl.BlockSpec((TILE, D_MODEL), lambda i: (i, 0)), cur, pkv, nkv,
                  pl.BlockSpec((SUB * STACK, 3 * BLOCK), lambda i: (i, 0)), pl.BlockSpec((SUB * N_HEADS, BLOCK), lambda i: (i, 0)),
                  half, half, _full((1, BLOCK)), _full((1, BLOCK)), _full((8, BLOCK, BLOCK))],
        out_specs=[cur, pl.BlockSpec((TILE, 2 * D_KV), lambda i: (i, 0)),
                   pl.BlockSpec(kv_blk, lambda i: ((i + nt - 1) % nt, 0)),
                   pl.BlockSpec(kv_blk, lambda i: ((i + 1) % nt, 0)),
                   _full((8, BLOCK)), _full((8, BLOCK)), _full((1, BLOCK)),
                   _full((8, BLOCK, BLOCK)), _full((4, BLOCK, BLOCK))],
        out_shape=[_sds((s, D_IN), BF16), _sds((s, 2 * D_KV)), _sds((nt * BLOCK, 2 * D_KV)), _sds((nt * BLOCK, 2 * D_KV)),
                   _sds((8, BLOCK)), _sds((8, BLOCK)), _sds((1, BLOCK)),
                   _sds((8, BLOCK, BLOCK)), _sds((4, BLOCK, BLOCK))],
        operands=(dy, proj, proj, proj, probs, psink, attn, sv, qg2, kg2, wst), vmem_mib=56, jobs=jobs)


def _w_in_grad(dpb, dkv, p0, p2, x, ng, scale, shift, name, jobs=()):
    s = x.shape[0]
    ts = min(2 * TILE, s)
    tiles = ts // TILE

    def body(dpb_ref, dkv_ref, p0_ref, p2_ref, x_ref, ng_ref, sc_ref, sh_ref, gw_ref, dkvb_ref):
        @pl.when(pl.program_id(0) == 0)
        def _():
            gw_ref[...] = jnp.zeros_like(gw_ref)

        xv = x_ref[...]
        r = lax.rsqrt(jnp.mean(xv * xv, axis=-1, keepdims=True) + EPS)
        h = (((xv * r) * ng_ref[...]) * (1.0 + sc_ref[...]) + sh_ref[...]).astype(BF16)
        for t in range(tiles):
            halo = slice(BLOCK * t, BLOCK * (t + 1))
            first = slice(TILE * t, TILE * t + BLOCK)
            last = slice(TILE * (t + 1) - BLOCK, TILE * (t + 1))
            dkvb_ref[first, :] = (dkv_ref[first, :] + p2_ref[halo, :]).astype(BF16)
            if SUB > 2:
                inner = slice(TILE * t + BLOCK, TILE * (t + 1) - BLOCK)
                dkvb_ref[inner, :] = dkv_ref[inner, :].astype(BF16)
            dkvb_ref[last, :] = (dkv_ref[last, :] + p0_ref[halo, :]).astype(BF16)
        gw_ref[...] += lax.dot_general(dpb_ref[...], h, TN_DIMS, preferred_element_type=F32)
        gw_ref[C_K:C_GA, :] += lax.dot_general(dkvb_ref[...], h, TN_DIMS, preferred_element_type=F32)

    kv = pl.BlockSpec((ts, 2 * D_KV), lambda i: (i, 0))
    halo = pl.BlockSpec((tiles * BLOCK, 2 * D_KV), lambda i: (i, 0))
    vec = _full((1, D_MODEL))
    return _pallas(
        body, name=name, grid=(s // ts,),
        in_specs=[pl.BlockSpec((ts, D_IN), lambda i: (i, 0)), kv, halo, halo,
                  pl.BlockSpec((ts, D_MODEL), lambda i: (i, 0)), vec, vec, vec],
        out_specs=[_full((D_IN, D_MODEL)), kv], out_shape=[_sds((D_IN, D_MODEL)), _sds((s, 2 * D_KV), BF16)],
        operands=(dpb, dkv, p0, p2, x, ng, scale, shift), vmem_mib=56, jobs=jobs)


def _proj_bwd(dpb, dkvb, x, dxo, ng, scale, wt, name, jobs=()):
    s = x.shape[0]
    ts = min(512, s)

    def body(dpb_ref, dkvb_ref, x_ref, dxo_ref, ng_ref, sc_ref, w_ref, dxi_ref, dsh_ref, dsc_ref, dng_ref):
        @pl.when(pl.program_id(0) == 0)
        def _():
            dsh_ref[...] = jnp.zeros_like(dsh_ref)
            dsc_ref[...] = jnp.zeros_like(dsc_ref)
            dng_ref[...] = jnp.zeros_like(dng_ref)

        dh = (jnp.dot(dpb_ref[...], w_ref[...], preferred_element_type=F32)
              + jnp.dot(dkvb_ref[...], w_ref[C_K:C_GA, :], preferred_element_type=F32))

        xv = x_ref[...]
        r = lax.rsqrt(jnp.mean(xv * xv, axis=-1, keepdims=True) + EPS)
        xn = xv * r
        ngv = ng_ref[...]
        sc1 = 1.0 + sc_ref[...]
        dsh_ref[...] += jnp.sum(dh, axis=0, keepdims=True)
        dsc_ref[...] += jnp.sum(dh * (xn * ngv), axis=0, keepdims=True)
        dh1 = dh * sc1
        dng_ref[...] += jnp.sum(dh1 * xn, axis=0, keepdims=True)
        dxn = dh1 * ngv
        dxi_ref[...] = r * (dxn - xn * jnp.mean(dxn * xn, axis=-1, keepdims=True)) + dxo_ref[...]

    row = pl.BlockSpec((ts, D_MODEL), lambda i: (i, 0))
    vec = _full((1, D_MODEL))
    return _pallas(
        body, name=name, grid=(s // ts,),
        in_specs=[pl.BlockSpec((ts, D_IN), lambda i: (i, 0)), pl.BlockSpec((ts, 2 * D_KV), lambda i: (i, 0)),
                  row, row, vec, vec, _full((D_IN, D_MODEL))],
        out_specs=[row, vec, vec, vec],
        out_shape=[_sds((s, D_MODEL)), _sds((1, D_MODEL)), _sds((1, D_MODEL)), _sds((1, D_MODEL))],
        operands=(dpb, dkvb, x, dxo, ng, scale, wt), vmem_mib=48, jobs=jobs)


def _pair_sum(g, r, c_idx, name, send_dtype=None):
    _, rows, cols = g.shape
    half = rows // 2

    def body(c_ref, g_ref, r_ref, o_ref, *narrow):
        total = g_ref[...] + r_ref[...]
        o_ref[...] = total
        for n_ref in narrow:
            n_ref[...] = total.astype(n_ref.dtype)

    blk = (None, half, cols)
    out_blk = pl.BlockSpec(blk, lambda j, c: (j, 0, 0))
    shapes = [_sds((N_CHIPS, half, cols))] + ([_sds((N_CHIPS, half, cols), send_dtype)] if send_dtype else [])
    return pl.pallas_call(
        body, name=name,
        grid_spec=pltpu.PrefetchScalarGridSpec(
            num_scalar_prefetch=1, grid=(N_CHIPS,),
            in_specs=[pl.BlockSpec(blk, lambda j, c: (j, c[0], 0)), out_blk], out_specs=[out_blk] * len(shapes)),
        out_shape=shapes,
        compiler_params=pltpu.CompilerParams(dimension_semantics=("arbitrary",), vmem_limit_bytes=32 * MIB),
    )(*_in_hbm(c_idx, g, r))


def _chip_sum(p, r, place, name):
    _, rows, cols = p.shape
    tr = rows // 2

    def body(j_ref, p_ref, r_ref, o_ref):
        o_ref[...] = ((p_ref[...] + r_ref[0].astype(F32)) + r_ref[1].astype(F32)) + r_ref[2].astype(F32)

    return pl.pallas_call(
        body, name=name,
        grid_spec=pltpu.PrefetchScalarGridSpec(
            num_scalar_prefetch=1, grid=(2,),
            in_specs=[pl.BlockSpec((None, tr, cols), lambda t, j: (j[0], t, 0)),
                      pl.BlockSpec((3, tr, cols), lambda t, j: (0, t, 0))],
            out_specs=pl.BlockSpec((tr, cols), lambda t, j: (2 * j[1] + t, 0))),
        out_shape=_sds((2 * rows, cols)),
        compiler_params=pltpu.CompilerParams(dimension_semantics=("arbitrary",), vmem_limit_bytes=32 * MIB),
    )(*_in_hbm(place, p, r))


def _cast_permute_w_in(wt, place_chunks):
    def body(t_ref, w_ref, w0_ref, w1_ref):
        def cast_into(o_ref):
            for t in range(N_CHUNKS):
                src = pl.multiple_of(t_ref[1 + t] * CHUNK_ROWS, CHUNK_ROWS)
                o_ref[CHUNK_ROWS * t:CHUNK_ROWS * (t + 1), :] = w_ref[pl.ds(src, CHUNK_ROWS), :].astype(BF16)

        @pl.when(pl.program_id(0) == 0)
        def _():
            cast_into(w0_ref)

        @pl.when(pl.program_id(0) == 1)
        def _():
            cast_into(w1_ref)

    return pl.pallas_call(
        body, name="cast_permute_w_in",
        grid_spec=pltpu.PrefetchScalarGridSpec(
            num_scalar_prefetch=1, grid=(DEPTH,),
            in_specs=[pl.BlockSpec((None, W_IN_BLK, D_MODEL), lambda l, tbl: (l, 0, 0))],
            out_specs=[pl.BlockSpec((None, W_IN_BLK, D_MODEL), lambda l, tbl: (tbl[0], 0, 0)),
                       pl.BlockSpec((W_IN_BLK, D_MODEL), lambda l, tbl: (0, 0))]),
        out_shape=[_sds((N_CHIPS, W_IN_BLK, D_MODEL), BF16), _sds((W_IN_BLK, D_MODEL), BF16)],
        compiler_params=pltpu.CompilerParams(dimension_semantics=("arbitrary",), vmem_limit_bytes=32 * MIB),
    )(*_in_hbm(place_chunks, wt))


def _gather_inputs(c, w_out, w0):
    half = W_IN_BLK // 2

    def body(c_ref, wout_ref, mine_ref, call_ref, woutb_ref, w0_ref, send_sems, recv_sems):
        x, y, cc = _coords()
        j = 2 * x + y
        b = 2 * j + cc
        sib = (x, y, 1 - cc)
        woutb_ref[...] = wout_ref[...].astype(BF16)
        call_ref[b] = c_ref[...]
        chips = _other_chips(x, y)

        def sems(k):
            return send_sems.at[k], recv_sems.at[k]

        def half_rows(chip_index):
            return w0_ref.at[chip_index, pl.ds(cc * half, half), :]

        first = [_remote(mine_ref.at[j, pl.ds(cc * half, half), :], half_rows(j), sems(k), (*chip, cc))
                 for k, chip in enumerate(chips)]
        k = 3
        rest = []
        for fx in (0, 1):
            for fy in (0, 1):
                for fc in (0, 1):
                    if fx or fy or fc:
                        dev = (1 - x if fx else x, 1 - y if fy else y, 1 - cc if fc else cc)
                        rest.append(_remote(call_ref.at[b], call_ref.at[b], sems(k), dev))
                        k += 1
        for cp in first + rest:
            cp.start()
        passed = []
        for k, chip in enumerate(chips):
            jk = 2 * chip[0] + chip[1]
            first[k].wait_recv()
            passed.append(_remote(half_rows(jk), half_rows(jk), sems(10 + k), sib))
            passed[k].start()
        for cp in first:
            cp.wait_send()
        for cp in rest + passed:
            cp.wait()

    return pl.pallas_call(
        body, name="gather_inputs", in_specs=[VMEM, VMEM, ANY], out_specs=[VMEM, VMEM, ANY],
        out_shape=[_sds((N_DEV, 1, D_MODEL)), _sds((DEPTH, W_OUT_BLK, D_MODEL), BF16),
                   _sds((N_CHIPS, W_IN_BLK, D_MODEL), BF16)],
        scratch_shapes=[pltpu.SemaphoreType.DMA((13,)), pltpu.SemaphoreType.DMA((13,))],
        input_output_aliases={2: 2},
        compiler_params=pltpu.CompilerParams(vmem_limit_bytes=32 * MIB),
    )(c, w_out, w0)


def _ada_rows(c_all, w_ada, b_blk):
    def body(c_ref, w_ref, b_ref, o_ref, cond_ref):
        cv = c_ref[...]
        cond = (cv * _sigmoid(cv)).astype(BF16)
        cond_ref[...] = cond.astype(F32)
        for l in range(DEPTH):
            o_ref[:, l, :] = jnp.dot(cond, w_ref[l].astype(BF16), preferred_element_type=F32) + b_ref[l:l + 1, :]

    return pl.pallas_call(
        body, name="ada_rows", in_specs=[VMEM, VMEM, VMEM], out_specs=[VMEM, VMEM],
        out_shape=[_sds((N_DEV, DEPTH, W_ADA_BLK)), _sds((N_DEV, D_MODEL))],
        compiler_params=pltpu.CompilerParams(vmem_limit_bytes=32 * MIB),
    )(c_all, w_ada, b_blk)


def _exchange_ada(part):
    def body(part_ref, out_ref, send_sems, recv_sems):
        x, y, cc = _coords()
        j = 2 * x + y
        out_ref[j] = part_ref[2 * j + cc]
        copies = []
        for k, chip in enumerate(_other_chips(x, y)):
            b_dst = 4 * chip[0] + 2 * chip[1] + cc
            copies.append(_remote(part_ref.at[b_dst], out_ref.at[j], (send_sems.at[k], recv_sems.at[k]), (*chip, cc)))
        for cp in copies:
            cp.start()
        for cp in copies:
            cp.wait()

    return pl.pallas_call(
        body, name="exchange_ada", in_specs=[VMEM], out_specs=VMEM,
        out_shape=_sds((N_CHIPS, DEPTH, W_ADA_BLK)),
        scratch_shapes=[pltpu.SemaphoreType.DMA((3,)), pltpu.SemaphoreType.DMA((3,))],
    )(part)


def _adamw_math(w, g, m, v):
    m = ADAM_B1 * m + (1.0 - ADAM_B1) * g
    v = ADAM_B2 * v + (1.0 - ADAM_B2) * (g * g)
    m_hat = m / (1.0 - ADAM_B1 ** ADAM_STEP)
    v_hat = v / (1.0 - ADAM_B2 ** ADAM_STEP)
    delta = -ADAM_LR * (m_hat / (jnp.sqrt(v_hat) + ADAM_EPS) + ADAM_WD * w)
    return delta, m, v


def _adamw_w_in(w, g0, g1, m, v, pos_chunks):
    def body(t_ref, w_ref, g0_ref, g1_ref, m_ref, v_ref, g_ref, d_ref, nm_ref, nv_ref):
        for l, src in enumerate((g0_ref, g1_ref)):
            g = src[...]
            g_ref[l] = g
            d_ref[l], nm_ref[l], nv_ref[l] = _adamw_math(w_ref[l], g, m_ref[l], v_ref[l])

    nat = pl.BlockSpec((DEPTH, CHUNK_ROWS, D_MODEL), lambda t, tbl: (0, t, 0))
    per = pl.BlockSpec((CHUNK_ROWS, D_MODEL), lambda t, tbl: (tbl[t], 0))
    return pl.pallas_call(
        body, name="adamw_w_in",
        grid_spec=pltpu.PrefetchScalarGridSpec(num_scalar_prefetch=1, grid=(N_CHUNKS,),
                                               in_specs=[nat, per, per, nat, nat], out_specs=[nat] * 4),
        out_shape=[_sds(w.shape)] * 4,
        compiler_params=pltpu.CompilerParams(dimension_semantics=("arbitrary",)),
    )(*_in_hbm(pos_chunks, w, g0, g1, m, v))


def _adamw_w_out(w, g0, g1, m, v):
    def body(w_ref, g0_ref, g1_ref, m_ref, v_ref, g_ref, d_ref, nm_ref, nv_ref):
        g = jnp.where(pl.program_id(0) == 0, g0_ref[...], g1_ref[...])
        g_ref[...] = g
        d_ref[...], nm_ref[...], nv_ref[...] = _adamw_math(w_ref[...], g, m_ref[...], v_ref[...])

    blk = pl.BlockSpec((None, W_OUT_BLK, D_MODEL), lambda l: (l, 0, 0))
    gblk = _full((W_OUT_BLK, D_MODEL))
    return pl.pallas_call(
        body, name="adamw_w_out", grid=(DEPTH,), in_specs=[blk, gblk, gblk, blk, blk], out_specs=[blk] * 4,
        out_shape=[_sds(w.shape)] * 4,
        compiler_params=pltpu.CompilerParams(dimension_semantics=("arbitrary",), vmem_limit_bytes=32 * MIB),
    )(w, g0, g1, m, v)


def _w_ada_grad_adamw(cond_t, dada, w, m, v):
    _, rows, cols = w.shape
    tr = 256

    def body(ct_ref, da_ref, w_ref, m_ref, v_ref, g_ref, d_ref, nm_ref, nv_ref):
        g = jnp.dot(ct_ref[...], da_ref[...].astype(BF16), preferred_element_type=F32)
        g_ref[...] = g
        d_ref[...], nm_ref[...], nv_ref[...] = _adamw_math(w_ref[...], g, m_ref[...], v_ref[...])

    blk = pl.BlockSpec((None, tr, cols), lambda l, t: (l, t, 0))
    return pl.pallas_call(
        body, name="w_ada_grad_adamw", grid=(DEPTH, rows // tr),
        in_specs=[pl.BlockSpec((tr, BLOCK), lambda l, t: (t, 0)), pl.BlockSpec((None, BLOCK, cols), lambda l, t: (l, 0, 0)),
                  blk, blk, blk],
        out_specs=[blk] * 4, out_shape=[_sds(w.shape)] * 4,
        compiler_params=pltpu.CompilerParams(dimension_semantics=("arbitrary", "arbitrary"), vmem_limit_bytes=32 * MIB),
    )(cond_t, dada, w, m, v)


def _small_sum_adamw(gathered_a, gathered_b, ws, rest):
    n_ws = 8 * BLOCK

    def body(*refs):
        a_refs, b_ref = refs[:DEPTH], refs[DEPTH]
        ws_ref, ms_ref, vs_ref, wr_ref, mr_ref, vr_ref = refs[DEPTH + 1:DEPTH + 7]
        gs_ref, ds_ref, nms_ref, nvs_ref, gr_ref, dr_ref, nmr_ref, nvr_ref = refs[DEPTH + 7:]

        def total(ref):
            g = ref[0]
            for b in range(1, N_DEV):
                g = g + ref[b]
            return g

        totals = [total(ref) for ref in a_refs]
        for l, t in enumerate(totals):
            g = t[0:n_ws]
            gs_ref[l] = g
            ds_ref[l], nms_ref[l], nvs_ref[l] = _adamw_math(ws_ref[l], g, ms_ref[l], vs_ref[l])
        g = jnp.concatenate([t[n_ws:] for t in totals] + [total(b_ref)], axis=0)
        gr_ref[...] = g
        dr_ref[...], nmr_ref[...], nvr_ref[...] = _adamw_math(wr_ref[...], g, mr_ref[...], vr_ref[...])

    return pl.pallas_call(
        body, name="small_sum_adamw", in_specs=[VMEM] * (DEPTH + 7), out_specs=[VMEM] * 8,
        out_shape=[_sds(ws[0].shape)] * 4 + [_sds(rest[0].shape)] * 4,
        compiler_params=pltpu.CompilerParams(vmem_limit_bytes=48 * MIB),
    )(*gathered_a, gathered_b, *ws, *rest)


_SMALL_A_REST = (("b_s", 8), ("q_gain", 1), ("k_gain", 1), ("sink", 1))
_SMALL_A = (("w_s", 8 * BLOCK),) + _SMALL_A_REST
_SMALL_B = (("b_ada", DEPTH * 24), ("norm_gain", DEPTH * 8), ("sq_err", 1))


def _pack_rows(parts, layout, layer=None):
    rows = []
    for name, n in layout:
        flat = (parts[name] if layer is None else parts[name][layer]).reshape(-1)
        rows.append(jnp.pad(flat, (0, n * 128 - flat.shape[0])).reshape(n, 128))
    n_rows = sum(n for _, n in layout)
    if n_rows % 8:
        rows.append(jnp.zeros((-n_rows % 8, 128), F32))
    return jnp.concatenate(rows, axis=0)


def _pack_rest(parts):
    return jnp.concatenate([_pack_rows(parts, _SMALL_A_REST, l) for l in range(DEPTH)] + [_pack_rows(parts, _SMALL_B)], axis=0)


def _unpack_rest(packed, shapes):
    def take(r0, layout, shape_of):
        got = {}
        for name, n in layout:
            shape = shape_of(name)
            size = 1
            for d in shape:
                size *= d
            got[name] = packed[r0:r0 + n].reshape(-1)[:size].reshape(shape)
            r0 += n
        return got, r0 + -r0 % 8

    layers, r0 = [], 0
    for _ in range(DEPTH):
        got, r0 = take(r0, _SMALL_A_REST, lambda name: shapes[name][1:])
        layers.append(got)
    out, _ = take(r0, _SMALL_B, lambda name: shapes[name])
    out.update({name: jnp.stack([layer[name] for layer in layers]) for name, _ in _SMALL_A_REST})
    return out


def _permute_heads(a, axis):
    shp = a.shape
    a = a.reshape(shp[:axis] + (2, 4, HEAD_DIM) + shp[axis + 1:])
    a = jnp.swapaxes(a, axis, axis + 1)
    return a.reshape(shp)


def _unpermute_heads(a, axis):
    shp = a.shape
    a = a.reshape(shp[:axis] + (4, 2, HEAD_DIM) + shp[axis + 1:])
    a = jnp.swapaxes(a, axis, axis + 1)
    return a.reshape(shp)


def _permute_w_out(w):
    return jnp.concatenate([_permute_heads(w[:D_ATTN], 0), w[D_ATTN:]], axis=0)


def _unpermute_w_out(w):
    return jnp.concatenate([_unpermute_heads(w[:D_ATTN], 0), w[D_ATTN:]], axis=0)


def kernel(x, c, w_ada, b_ada, norm_gain, w_in, q_gain, k_gain, sink, w_s, b_s, w_out, loss_target, m_w_ada, m_b_ada, m_norm_gain, m_w_in, m_q_gain, m_k_gain, m_sink, m_w_s, m_b_s, m_w_out, v_w_ada, v_b_ada, v_norm_gain, v_w_in, v_q_gain, v_k_gain, v_sink, v_w_s, v_b_s, v_w_out):
    ix, iy, ic = _coords()
    chip = 2 * ix + iy
    chip_idx = jnp.stack([chip, ic]).astype(jnp.int32)
    core_idx = jnp.reshape(ic, (1,)).astype(jnp.int32)
    src_chunks = lax.dynamic_index_in_dim(jnp.asarray(_CHUNK_SRC), chip, 0, keepdims=False)
    pos_chunks = lax.dynamic_index_in_dim(jnp.asarray(_CHUNK_POS), chip, 0, keepdims=False)
    x0, target = x[0], loss_target[0]

    wt, mt, vt = (jnp.swapaxes(a, 1, 2) for a in (w_in, m_w_in, v_w_in))
    w0_mine, wloc_in1 = _cast_permute_w_in(wt, jnp.concatenate([chip_idx[:1], src_chunks]))
    c_all, wloc_out, w0 = _gather_inputs(c, w_out, w0_mine)
    wts = [w0.reshape(D_IN, D_MODEL), None]

    b_blk = lax.dynamic_slice_in_dim(b_ada, chip * W_ADA_BLK, W_ADA_BLK, axis=1)
    ada_part, cond = _ada_rows(c_all.reshape(N_DEV, D_MODEL), w_ada, b_blk)
    ada = jnp.moveaxis(_exchange_ada(ada_part), 0, 1).reshape(DEPTH, 3 * D_MODEL)
    shift = [ada[l:l + 1, 0:D_MODEL] for l in range(DEPTH)]
    scale = [ada[l:l + 1, D_MODEL:2 * D_MODEL] for l in range(DEPTH)]
    gate = [ada[l:l + 1, 2 * D_MODEL:] for l in range(DEPTH)]
    ng = [norm_gain[l:l + 1] for l in range(DEPTH)]

    qg2 = jnp.concatenate([q_gain, q_gain], axis=-1)
    kg2 = jnp.concatenate([k_gain, k_gain], axis=-1)
    ws_b = w_s.astype(BF16)
    wst_b = jnp.swapaxes(w_s, -1, -2).astype(BF16)
    bsp = jnp.repeat(jnp.swapaxes(b_s.reshape(DEPTH, 4, 2, BLOCK), -1, -2), HEAD_DIM, axis=-1)
    bias = jnp.asarray(_bias_table())

    def mix_args(l):
        return bias, sink[l], qg2[l:l + 1], kg2[l:l + 1], ws_b[l]

    w_out_shape = (W_OUT_BLK, D_MODEL)
    proj0, wo0, wo1 = _proj_fwd(x0, ng[0], scale[0], shift[0], wts[0], "proj_fwd_0",
                                jobs=[_job_gather([(wloc_out, 0), (wloc_out, 1)], [w_out_shape, w_out_shape])])
    y0, *kept0, w1 = _mix_fwd(proj0, *mix_args(0), bsp[0], "mix_fwd_0",
                              jobs=[_job_gather([(wloc_in1, None)], [(W_IN_BLK, D_MODEL)])])
    wts[1] = w1.reshape(D_IN, D_MODEL)
    wos = [_permute_w_out(w.reshape(D_MODEL, D_MODEL)) for w in (wo0, wo1)]
    x1, proj1 = _out_proj_fwd(y0, x0, gate[0], wos[0], ng[1], scale[1], shift[1], wts[1], "out_proj_fwd_01")
    y1, *kept1 = _mix_fwd(proj1, *mix_args(1), bsp[1], "mix_fwd_1")

    def blocks_out(gw):
        return _unpermute_w_out(gw).reshape(N_CHIPS, W_OUT_BLK, D_MODEL)

    def small_pack(dws, dbsp, dqg, dkg, dsink):
        b_s = jnp.swapaxes(dbsp[:, :, ::HEAD_DIM], -1, -2).reshape(8, BLOCK)
        return _pack_rows(dict(w_s=dws, b_s=b_s, q_gain=dqg[0, :HEAD_DIM], k_gain=dkg[0, :HEAD_DIM], sink=dsink[0, :N_HEADS]),
                          _SMALL_A)

    dx2, sq, dy1, gwo1, dgate1 = _out_loss_bwd(y1, x1, gate[1], wos[1], target, "out_loss_bwd_1")
    go1 = blocks_out(gwo1)
    dpb, dkv, p0, p2, dqg1, dkg1, dsink1, dws1, dbsp1, ro1 = _mix_bwd(
        dy1, proj1, *kept1, qg2[1:2], kg2[1:2], wst_b[1], "mix_bwd_1", jobs=[_job_swap(go1)])
    po1, = _pair_sum(go1, ro1, core_idx, "pair_sum_w_out_1")
    gwi1, dkvb, co1, gathered_a1 = _w_in_grad(
        dpb, dkv, p0, p2, x1, ng[1], scale[1], shift[1], "w_in_grad_1",
        jobs=[_job_scatter(po1), _job_all_gather(small_pack(dws1, dbsp1, dqg1, dkg1, dsink1))])
    gi1 = gwi1.reshape(N_CHIPS, W_IN_BLK, D_MODEL)
    fo1 = _chip_sum(po1, co1, chip_idx, "chip_sum_w_out_1")
    dx1, dsh1, dsc1, dng1, grad_wo1, ri1 = _proj_bwd(dpb, dkvb, x1, dx2, ng[1], scale[1], wts[1], "proj_bwd_1",
                                                     jobs=[_job_join(fo1), _job_swap(gi1)])
    pi1, = _pair_sum(gi1, ri1, core_idx, "pair_sum_w_in_1")

    dy0, gwo0, dgate0 = _out_bwd(dx1, y0, gate[0], wos[0], "out_bwd_0")
    go0 = blocks_out(gwo0)
    dpb, dkv, p0, p2, dqg0, dkg0, dsink0, dws0, dbsp0, ci1, ro0 = _mix_bwd(
        dy0, proj0, *kept0, qg2[0:1], kg2[0:1], wst_b[0], "mix_bwd_0", jobs=[_job_scatter(pi1), _job_swap(go0)])
    fi1 = _chip_sum(pi1, ci1, chip_idx, "chip_sum_w_in_1")
    po0, = _pair_sum(go0, ro0, core_idx, "pair_sum_w_out_0")

    gwi0, dkvb, grad_wi1, co0, gathered_a0 = _w_in_grad(
        dpb, dkv, p0, p2, x0, ng[0], scale[0], shift[0], "w_in_grad_0",
        jobs=[_job_join(fi1), _job_scatter(po0), _job_all_gather(small_pack(dws0, dbsp0, dqg0, dkg0, dsink0))])
    gi0 = gwi0.reshape(N_CHIPS, W_IN_BLK, D_MODEL)
    fo0 = _chip_sum(po0, co0, chip_idx, "chip_sum_w_out_0")

    ri0, grad_wo0 = _comm([_job_swap(gi0), _job_join(fo0)], "swap_w_in_0")
    pi0, pi0_send = _pair_sum(gi0, ri0, core_idx, "pair_sum_w_in_0", send_dtype=BF16)
    dx0, dsh0, dsc0, dng0, ci0 = _proj_bwd(dpb, dkvb, x0, dx1, ng[0], scale[0], wts[0], "proj_bwd_0",
                                           jobs=[_job_scatter(pi0_send)])
    fi0 = _chip_sum(pi0, ci0, chip_idx, "chip_sum_w_in_0")

    small_g = dict(
        b_ada=jnp.stack([jnp.concatenate([dsh0, dsc0, dgate0], axis=-1)[0], jnp.concatenate([dsh1, dsc1, dgate1], axis=-1)[0]]),
        norm_gain=jnp.stack([dng0[0], dng1[0]]), sq_err=sq[0])
    none = jnp.zeros((1,), F32)
    small_w = dict(w_s=w_s, b_s=b_s, b_ada=b_ada, norm_gain=norm_gain, q_gain=q_gain, k_gain=k_gain, sink=sink, sq_err=none)
    small_m = dict(w_s=m_w_s, b_s=m_b_s, b_ada=m_b_ada, norm_gain=m_norm_gain, q_gain=m_q_gain, k_gain=m_k_gain, sink=m_sink,
                   sq_err=none)
    small_v = dict(w_s=v_w_s, b_s=v_b_s, b_ada=v_b_ada, norm_gain=v_norm_gain, q_gain=v_q_gain, k_gain=v_k_gain, sink=v_sink,
                   sq_err=none)
    grad_wi0, gathered_b = _comm([_job_join(fi0), _job_all_gather(_pack_rows(small_g, _SMALL_B))], "join_w_in_0")
    packed = _small_sum_adamw(
        [a.reshape(N_DEV, -1, 128) for a in (gathered_a0, gathered_a1)], gathered_b.reshape(N_DEV, -1, 128),
        [a.reshape(DEPTH, 8 * BLOCK, BLOCK) for a in (w_s, m_w_s, v_w_s)], [_pack_rest(p) for p in (small_w, small_m, small_v)])
    shapes = {k: a.shape for k, a in small_w.items()}
    sg, sd, sm, sv = (dict(_unpack_rest(rest, shapes), w_s=ws.reshape(w_s.shape)) for ws, rest in zip(packed[:4], packed[4:]))
    loss = 0.5 * sg["sq_err"][0]

    dada_all = gathered_b.reshape(N_DEV, -1, 128)[:, 0:DEPTH * 24].reshape(N_DEV, DEPTH, 3 * D_MODEL)
    dada_blk = jnp.moveaxis(lax.dynamic_slice_in_dim(dada_all, chip * W_ADA_BLK, W_ADA_BLK, axis=2), 0, 1)
    pad = BLOCK - N_DEV
    ada_out = _w_ada_grad_adamw(
        jnp.pad(cond.T, ((0, 0), (0, pad))).astype(BF16), jnp.pad(dada_blk, ((0, 0), (0, pad), (0, 0))),
        w_ada, m_w_ada, v_w_ada)

    in_out = [jnp.swapaxes(a, 1, 2) for a in _adamw_w_in(wt, grad_wi0, grad_wi1, mt, vt, pos_chunks)]
    out_out = _adamw_w_out(w_out, grad_wo0, grad_wo1, m_w_out, v_w_out)

    def ordered(k):
        small = (sg, sd, sm, sv)[k]
        return (ada_out[k], small["b_ada"], small["norm_gain"], in_out[k], small["q_gain"], small["k_gain"], small["sink"],
                small["w_s"], small["b_s"], out_out[k])

    return (loss, dx0[None], *ordered(0), *ordered(1), *ordered(2), *ordered(3))
```

```python
import numpy as np

import jax
import jax.numpy as jnp
from jax import lax
from jax.experimental import pallas as pl
from jax.experimental.pallas import tpu as pltpu

F32 = jnp.float32
BF16 = jnp.bfloat16

D_MODEL = 1024
DEPTH = 2
HEAD_DIM = 64
N_HEADS = 8
BLOCK = 128
SUB = 4
TILE = SUB * BLOCK
STACK = 8 * BLOCK
D_ATTN = 512
D_KV = 128
D_IN = 2816
N_CHIPS = 4
N_DEV = 8
W_IN_BLK = D_IN // N_CHIPS
W_OUT_BLK = D_MODEL // N_CHIPS
W_ADA_BLK = 3 * D_MODEL // N_CHIPS
CHUNK_ROWS = HEAD_DIM
N_CHUNKS = W_IN_BLK // CHUNK_ROWS
EPS = 1e-6
NEG_INF = -1e30

C_Q, C_K, C_V, C_GA, C_U, C_VG, C_GG = 0, 512, 640, 768, 1280, 1792, 2304

ADAM_LR = 0.001
ADAM_B1 = 0.9
ADAM_B2 = 0.999
ADAM_EPS = 1e-08
ADAM_WD = 0.01
ADAM_STEP = 10

MESH = pl.DeviceIdType.MESH
MIB = 1024 * 1024
ANY = pl.BlockSpec(memory_space=pl.ANY)
VMEM = pl.BlockSpec(memory_space=pltpu.VMEM)

NT_DIMS = (((1,), (1,)), ((), ()))
TN_DIMS = (((0,), (0,)), ((), ()))

_PAIR_ORDER = (0, 4, 1, 5, 2, 6, 3, 7)
_CHUNK_SRC = np.array([
    list(_PAIR_ORDER) + [8, 9, 10],
    [0] + [1 + h for h in _PAIR_ORDER] + [9, 10],
    list(range(N_CHUNKS)),
    list(range(N_CHUNKS)),
], np.int32)
_CHUNK_POS = np.argsort(_CHUNK_SRC, axis=1).astype(np.int32)


def _bias_table():
    i = np.arange(N_HEADS * BLOCK)[:, None]
    j = np.arange(3 * BLOCK)[None, :]
    dist = np.abs(j - BLOCK - (i % BLOCK))
    slope = 2.0 ** -(i // BLOCK + 1.0)
    inner = np.where(dist <= BLOCK, -(slope * dist), NEG_INF)
    first = np.where(j >= BLOCK, inner, NEG_INF)
    last = np.where(j < 2 * BLOCK, inner, NEG_INF)
    return np.stack([first, inner, last]).astype(np.float32)


def _full(shape):
    n = len(shape)
    return pl.BlockSpec(shape, lambda *_: (0,) * n)


def _sds(shape, dtype=F32):
    return jax.ShapeDtypeStruct(shape, dtype)


def _coords():
    return lax.axis_index("x"), lax.axis_index("y"), lax.axis_index("c")


def _other_chips(x, y):
    return [(1 - x, y), (x, 1 - y), (1 - x, 1 - y)]


def _in_hbm(*operands):
    return [pltpu.with_memory_space_constraint(a, pltpu.HBM) if a.size * a.dtype.itemsize >= MIB // 4 else a
            for a in operands]


def _remote(src, dst, sems, dev):
    return pltpu.make_async_remote_copy(src_ref=src, dst_ref=dst, send_sem=sems[0], recv_sem=sems[1],
                                        device_id=dev, device_id_type=MESH)


class _Job:
    def __init__(self, inputs, out_shapes, n_remote, n_local, make, then=None, in_place=False):
        self.inputs, self.out_shapes, self.n_remote, self.n_local, self.make = inputs, out_shapes, n_remote, n_local, make
        self.then = then
        self.in_place = in_place


def _job_aliases(jobs, in_base, out_base):
    aliases, a, b = {}, 0, 0
    for j in jobs:
        if j.in_place:
            aliases.update({in_base + a + k: out_base + b + k for k in range(len(j.inputs))})
        a, b = a + len(j.inputs), b + len(j.out_shapes)
    return aliases


def _job_copies(jobs, jin, jout, sems, second=False):
    send, recv, loc = sems
    res, a, b, r, l = [], 0, 0, 0, 0
    for j in jobs:
        build = j.then if second else j.make
        if build is not None:
            res += build(jin[a:a + len(j.inputs)], jout[b:b + len(j.out_shapes)],
                         lambda k, r=r: (send.at[r + k], recv.at[r + k]), lambda k, l=l: loc.at[l + k])
        a, b, r, l = a + len(j.inputs), b + len(j.out_shapes), r + j.n_remote, l + j.n_local
    return res


def _run(copies):
    for cp in copies:
        cp.start()
    for cp in copies:
        cp.wait()


def _job_gather(sources, shapes):
    n = len(sources)

    def make(ins, outs, rsem, lsem):
        x, y, c = _coords()
        j = 2 * x + y
        res = []
        for t, ((_, layer), src, dst) in enumerate(zip(sources, ins, outs)):
            src = src if layer is None else src.at[layer]
            res.append(pltpu.make_async_copy(src, dst.at[j], lsem(t)))
            for k, chip in enumerate(_other_chips(x, y)):
                res.append(_remote(src, dst.at[j], rsem(3 * t + k), (*chip, c)))
        return res

    return _Job([a for a, _ in sources], [_sds((N_CHIPS,) + s, BF16) for s in shapes], 3 * n, n, make)


def _job_swap(g):
    _, rows, cols = g.shape
    half = rows // 2

    def make(ins, outs, rsem, lsem):
        x, y, c = _coords()
        return [_remote(ins[0].at[:, pl.ds((1 - c) * half, half), :], outs[0], rsem(0), (x, y, 1 - c))]

    return _Job([g], [_sds((N_CHIPS, half, cols))], 1, 0, make)


def _job_scatter(p):
    def make(ins, outs, rsem, lsem):
        x, y, c = _coords()
        return [_remote(ins[0].at[2 * chip[0] + chip[1]], outs[0].at[k], rsem(k), (*chip, c))
                for k, chip in enumerate(_other_chips(x, y))]

    return _Job([p], [_sds((3,) + p.shape[1:], p.dtype)], 3, 0, make)


def _job_all_gather(blk):
    m_per = blk.shape[0]

    def rows(ref, px, py, pc):
        return ref.at[pl.ds((4 * px + 2 * py + pc) * m_per, m_per), :]

    def make(ins, outs, rsem, lsem):
        x, y, c = _coords()
        res = [pltpu.make_async_copy(ins[0], rows(outs[0], x, y, c), lsem(0)),
               _remote(ins[0], rows(outs[0], x, y, c), rsem(0), (x, y, 1 - c))]
        res += [_remote(ins[0], rows(outs[0], x, y, c), rsem(1 + k), (*chip, c)) for k, chip in enumerate(_other_chips(x, y))]
        return res

    def then(ins, outs, rsem, lsem):
        x, y, c = _coords()
        return [_remote(rows(outs[0], *chip, c), rows(outs[0], *chip, c), rsem(4 + k), (x, y, 1 - c))
                for k, chip in enumerate(_other_chips(x, y))]

    return _Job([blk], [_sds((N_DEV * m_per, blk.shape[1]), blk.dtype)], 7, 1, make, then)


def _job_join(f):
    half = f.shape[0] // 2

    def make(ins, outs, rsem, lsem):
        x, y, c = _coords()
        mine = pl.ds(c * half, half)
        return [_remote(ins[0].at[mine, :], outs[0].at[mine, :], rsem(0), (x, y, 1 - c))]

    return _Job([f], [_sds(f.shape, f.dtype)], 1, 0, make, in_place=True)


def _pallas(body, *, name, grid, in_specs, out_specs, out_shape, operands, vmem_mib, jobs=(), scratch=()):
    in_specs, out_specs, out_shape = list(in_specs), list(out_specs), list(out_shape)
    n_in, n_out = len(in_specs), len(out_specs)
    j_in = [a for j in jobs for a in j.inputs]
    j_out = [s for j in jobs for s in j.out_shapes]
    n_rem = max(1, sum(j.n_remote for j in jobs))
    n_loc = max(1, sum(j.n_local for j in jobs))
    sems = [pltpu.SemaphoreType.DMA((n_rem,)), pltpu.SemaphoreType.DMA((n_rem,)),
            pltpu.SemaphoreType.DMA((n_loc,))] if jobs else []
    scratch = list(scratch) + sems

    def wrapped(*refs):
        ins = refs[:n_in]
        jin = refs[n_in:n_in + len(j_in)]
        outs = refs[n_in + len(j_in):n_in + len(j_in) + n_out]
        jout = refs[n_in + len(j_in) + n_out:n_in + len(j_in) + n_out + len(j_out)]
        own = refs[n_in + len(j_in) + n_out + len(j_out):len(refs) - len(sems)]

        if jobs:
            first = last = None
            for d, n in enumerate(grid):
                f, e = pl.program_id(d) == 0, pl.program_id(d) == n - 1
                first, last = (f, e) if first is None else (first & f, last & e)

            @pl.when(first)
            def _():
                for cp in _job_copies(jobs, jin, jout, refs[-3:]):
                    cp.start()

        body(*ins, *outs, *own)

        if jobs:
            @pl.when(last)
            def _():
                for cp in _job_copies(jobs, jin, jout, refs[-3:]):
                    cp.wait()
                _run(_job_copies(jobs, jin, jout, refs[-3:], second=True))

    return pl.pallas_call(
        wrapped, name=name, grid=grid,
        in_specs=in_specs + [ANY] * len(j_in), out_specs=out_specs + [ANY] * len(j_out),
        out_shape=out_shape + j_out, scratch_shapes=scratch, input_output_aliases=_job_aliases(jobs, n_in, n_out),
        compiler_params=pltpu.CompilerParams(dimension_semantics=("arbitrary",) * len(grid),
                                             vmem_limit_bytes=vmem_mib * MIB),
    )(*_in_hbm(*operands, *j_in))


def _comm(jobs, name):
    j_in = [a for j in jobs for a in j.inputs]
    j_out = [s for j in jobs for s in j.out_shapes]
    n_rem = max(1, sum(j.n_remote for j in jobs))
    n_loc = max(1, sum(j.n_local for j in jobs))

    def body(*refs):
        jin, jout = refs[:len(j_in)], refs[len(j_in):len(j_in) + len(j_out)]
        _run(_job_copies(jobs, jin, jout, refs[-3:]))
        _run(_job_copies(jobs, jin, jout, refs[-3:], second=True))

    return pl.pallas_call(
        body, name=name, in_specs=[ANY] * len(j_in), out_specs=[ANY] * len(j_out), out_shape=j_out,
        scratch_shapes=[pltpu.SemaphoreType.DMA((n_rem,)), pltpu.SemaphoreType.DMA((n_rem,)),
                        pltpu.SemaphoreType.DMA((n_loc,))],
        input_output_aliases=_job_aliases(jobs, 0, 0),
    )(*_in_hbm(*j_in))


def _sigmoid(x):
    return 1.0 / (1.0 + jnp.exp(-x))


def _lo_mask(shape):
    return lax.broadcasted_iota(jnp.int32, shape, len(shape) - 1) < HEAD_DIM


def _half_sum(x, lo):
    a = jnp.sum(jnp.where(lo, x, 0.0), axis=-1, keepdims=True)
    b = jnp.sum(jnp.where(lo, 0.0, x), axis=-1, keepdims=True)
    return jnp.where(lo, a, b)


def _half_rms_scale(x, lo):
    return lax.rsqrt(_half_sum(x * x, lo) * (1.0 / HEAD_DIM) + EPS)


def _stack_heads(pairs, lo):
    return jnp.concatenate([jnp.where(lo, t, 0.0) for t in pairs] + [jnp.where(lo, 0.0, t) for t in pairs], axis=0)


def _unstack_pair(stack, p, lo):
    return jnp.where(lo, stack[BLOCK * p:BLOCK * (p + 1)], stack[BLOCK * (4 + p):BLOCK * (5 + p)])


def _attention_probs(q_stack, kn, bias_ref, sink_ref):
    rows = N_HEADS * BLOCK
    s = lax.dot_general(q_stack, kn, NT_DIMS, preferred_element_type=F32) + bias_ref[...]
    sink = jnp.concatenate([jnp.full((BLOCK, BLOCK), sink_ref[h], F32) for h in range(N_HEADS)], axis=0)
    cols = [s[:, BLOCK * j:BLOCK * (j + 1)] for j in range(3)]
    top = jnp.max(jnp.maximum(jnp.maximum(cols[0], cols[1]), cols[2]), axis=-1, keepdims=True)
    m = jnp.maximum(jnp.broadcast_to(top, (rows, BLOCK)), sink)
    e = [jnp.exp(c - m) for c in cols]
    es = jnp.exp(sink - m)
    inv = 1.0 / (jnp.broadcast_to(jnp.sum((e[0] + e[1]) + e[2], axis=-1, keepdims=True), (rows, BLOCK)) + es)
    return jnp.concatenate([c * inv for c in e], axis=1), es * inv


def _kv_rows(cur_ref, pkv_ref, nkv_ref):
    k = jnp.concatenate([pkv_ref[:, 0:D_KV], cur_ref[:, C_K:C_K + D_KV], nkv_ref[:, 0:D_KV]], axis=0)
    v = jnp.concatenate([pkv_ref[:, D_KV:2 * D_KV], cur_ref[:, C_V:C_V + D_KV], nkv_ref[:, D_KV:2 * D_KV]], axis=0)
    return k, v


def _overlap_add(parts):
    blocks = []
    for j in range(SUB + 2):
        terms = [parts[b][BLOCK * (j - b):BLOCK * (j - b + 1)] for b in range(SUB) if 0 <= j - b <= 2]
        total = terms[0]
        for t in terms[1:]:
            total = total + t
        blocks.append(total)
    return jnp.concatenate(blocks, axis=0)


def _mix_specs(nt):
    cur = pl.BlockSpec((TILE, D_IN), lambda i: (i, 0))
    kv_col = C_K // (2 * D_KV)
    pkv = pl.BlockSpec((BLOCK, 2 * D_KV), lambda i: (jnp.maximum(i * SUB - 1, 0), kv_col))
    nkv = pl.BlockSpec((BLOCK, 2 * D_KV), lambda i: (jnp.minimum((i + 1) * SUB, nt * SUB - 1), kv_col))
    table = (None, N_HEADS * BLOCK, 3 * BLOCK)
    first = pl.BlockSpec(table, lambda i: (jnp.where(i == 0, 0, 1), 0, 0))
    inner = pl.BlockSpec(table, lambda i: (1, 0, 0))
    last = pl.BlockSpec(table, lambda i: (jnp.where(i == nt - 1, 2, 1), 0, 0))
    return cur, pkv, nkv, [first] + [inner] * (SUB - 2) + [last]


def _proj_fwd(x, ng, scale, shift, wt, name, jobs=()):
    s = x.shape[0]
    ts = min(512, s)

    def body(x_ref, ng_ref, sc_ref, sh_ref, w_ref, o_ref):
        xv = x_ref[...]
        r = lax.rsqrt(jnp.mean(xv * xv, axis=-1, keepdims=True) + EPS)
        h = ((xv * r) * ng_ref[...]) * (1.0 + sc_ref[...]) + sh_ref[...]
        o_ref[...] = lax.dot_general(h.astype(BF16), w_ref[...], NT_DIMS, preferred_element_type=F32)

    vec = _full((1, D_MODEL))
    return _pallas(
        body, name=name, grid=(s // ts,),
        in_specs=[pl.BlockSpec((ts, D_MODEL), lambda i: (i, 0)), vec, vec, vec, _full((D_IN, D_MODEL))],
        out_specs=[pl.BlockSpec((ts, D_IN), lambda i: (i, 0))], out_shape=[_sds((s, D_IN))],
        operands=(x, ng, scale, shift, wt), vmem_mib=48, jobs=jobs)


def _diagonal():
    return lax.broadcasted_iota(jnp.int32, (BLOCK, BLOCK), 0) == lax.broadcasted_iota(jnp.int32, (BLOCK, BLOCK), 1)


def _column_as_row(wide, eye):
    return jnp.sum(jnp.where(eye, wide, 0.0), axis=0, keepdims=True)


def _mix_fwd(proj, bias, sink, qg2, kg2, ws, bsp, name, jobs=()):
    s = proj.shape[0]
    nt = s // TILE

    def body(sink_ref, cur_ref, pkv_ref, nkv_ref, *rest):
        bias_refs = rest[:SUB]
        qg_ref, kg_ref, ws_ref, bsp_ref, y_ref, p_ref, ps_ref, attn_ref, sv_ref = rest[SUB:]
        lo = _lo_mask((BLOCK, BLOCK))
        eye = _diagonal()
        lo_kv = _lo_mask((TILE + 2 * BLOCK, BLOCK))
        k_all, v_all = _kv_rows(cur_ref, pkv_ref, nkv_ref)
        kn_all = ((k_all * _half_rms_scale(k_all, lo_kv)) * kg_ref[...]).astype(BF16)
        vb_all = v_all.astype(BF16)
        for b in range(SUB):
            rows = slice(BLOCK * b, BLOCK * (b + 1))
            window = slice(BLOCK * b, BLOCK * (b + 3))
            qn = []
            for p in range(4):
                q = cur_ref[rows, C_Q + BLOCK * p:C_Q + BLOCK * (p + 1)]
                qn.append(((q * _half_rms_scale(q, lo)) * qg_ref[...]) * 0.125)
            q_stack = _stack_heads(qn, lo).astype(BF16)
            prob, psink = _attention_probs(q_stack, kn_all[window], bias_refs[b], sink_ref)
            pb = prob.astype(BF16)
            p_ref[STACK * b:STACK * (b + 1), :] = pb
            ps_ref[N_HEADS * b:N_HEADS * (b + 1), :] = jnp.concatenate(
                [_column_as_row(psink[BLOCK * h:BLOCK * (h + 1)], eye) for h in range(N_HEADS)], axis=0)
            o_stack = jnp.dot(pb, vb_all[window], preferred_element_type=F32)
            for p in range(4):
                g = cur_ref[rows, C_GA + BLOCK * p:C_GA + BLOCK * (p + 1)]
                attn = _unstack_pair(o_stack, p, lo)
                attn_ref[rows, BLOCK * p:BLOCK * (p + 1)] = attn.astype(BF16)
                y_ref[rows, BLOCK * p:BLOCK * (p + 1)] = (attn * (g * _sigmoid(g))).astype(BF16)

        for p in range(4):
            cols = slice(C_VG + BLOCK * p, C_VG + BLOCK * (p + 1))
            vn = []
            for b in range(SUB):
                vg = cur_ref[BLOCK * b:BLOCK * (b + 1), cols]
                vn.append((vg * _half_rms_scale(vg, lo)).astype(BF16))
            vn = jnp.concatenate(vn, axis=1)
            sv_a = jnp.dot(ws_ref[2 * p], vn, preferred_element_type=F32)
            sv_b = jnp.dot(ws_ref[2 * p + 1], vn, preferred_element_type=F32)
            for b in range(SUB):
                rows = slice(BLOCK * b, BLOCK * (b + 1))
                lanes = slice(BLOCK * b, BLOCK * (b + 1))
                sv = jnp.where(lo, sv_a[:, lanes], sv_b[:, lanes]) + bsp_ref[p]
                sv_ref[rows, BLOCK * p:BLOCK * (p + 1)] = sv.astype(BF16)
                u = cur_ref[rows, C_U + BLOCK * p:C_U + BLOCK * (p + 1)]
                g = cur_ref[rows, C_GG + BLOCK * p:C_GG + BLOCK * (p + 1)]
                y_ref[rows, D_ATTN + BLOCK * p:D_ATTN + BLOCK * (p + 1)] = ((u * sv) * (g * _sigmoid(g))).astype(BF16)

    cur, pkv, nkv, bias_specs = _mix_specs(nt)
    nb = nt * SUB
    half = pl.BlockSpec((TILE, D_ATTN), lambda i: (i, 0))
    return _pallas(
        body, name=name, grid=(nt,),
        in_specs=[pl.BlockSpec(memory_space=pltpu.SMEM), cur, pkv, nkv, *bias_specs, _full((1, BLOCK)), _full((1, BLOCK)),
                  _full((8, BLOCK, BLOCK)), _full((4, BLOCK, BLOCK))],
        out_specs=[pl.BlockSpec((TILE, D_MODEL), lambda i: (i, 0)), pl.BlockSpec((SUB * STACK, 3 * BLOCK), lambda i: (i, 0)),
                   pl.BlockSpec((SUB * N_HEADS, BLOCK), lambda i: (i, 0)), half, half],
        out_shape=[_sds((s, D_MODEL), BF16), _sds((nb * STACK, 3 * BLOCK), BF16), _sds((nb * N_HEADS, BLOCK)),
                   _sds((s, D_ATTN), BF16), _sds((s, D_ATTN), BF16)],
        operands=(sink, proj, proj, proj, *([bias] * SUB), qg2, kg2, ws, bsp), vmem_mib=56, jobs=jobs)


def _out_proj_fwd(y, x, gate, w_out, ng, scale, shift, wt, name):
    s = x.shape[0]
    ts = min(512, s)

    def body(y_ref, x_ref, g_ref, w_ref, ng_ref, sc_ref, sh_ref, wt_ref, xn_ref, p_ref):
        xv = x_ref[...] + g_ref[...] * jnp.dot(y_ref[...], w_ref[...], preferred_element_type=F32)
        xn_ref[...] = xv
        r = lax.rsqrt(jnp.mean(xv * xv, axis=-1, keepdims=True) + EPS)
        h = ((xv * r) * ng_ref[...]) * (1.0 + sc_ref[...]) + sh_ref[...]
        p_ref[...] = lax.dot_general(h.astype(BF16), wt_ref[...], NT_DIMS, preferred_element_type=F32)

    row = pl.BlockSpec((ts, D_MODEL), lambda i: (i, 0))
    vec = _full((1, D_MODEL))
    return _pallas(
        body, name=name, grid=(s // ts,),
        in_specs=[row, row, vec, _full((D_MODEL, D_MODEL)), vec, vec, vec, _full((D_IN, D_MODEL))],
        out_specs=[row, pl.BlockSpec((ts, D_IN), lambda i: (i, 0))], out_shape=[_sds((s, D_MODEL)), _sds((s, D_IN))],
        operands=(y, x, gate, w_out, ng, scale, shift, wt), vmem_mib=56)


def _out_loss_bwd(y, x, gate, w_out, target, name):
    s = x.shape[0]
    ts = min(512, s)
    steps = s // ts

    def body(y_ref, x_ref, g_ref, w_ref, t_ref, dx_ref, sq_ref, dy_ref, gw_ref, dg_ref):
        @pl.when(pl.program_id(0) == 0)
        def _():
            sq_ref[...] = jnp.zeros_like(sq_ref)
            gw_ref[...] = jnp.zeros_like(gw_ref)

        yv = y_ref[...]
        out = x_ref[...] + g_ref[...] * jnp.dot(yv, w_ref[...], preferred_element_type=F32)
        diff = out - t_ref[...]
        dx = diff * (1.0 / D_MODEL)
        dx_ref[...] = dx
        per_token = jnp.sum(diff * diff, axis=-1, keepdims=True) * (1.0 / D_MODEL)
        sq_ref[...] += jnp.sum(per_token, axis=0, keepdims=True)
        dy_ref[...] = lax.dot_general((dx * g_ref[...]).astype(BF16), w_ref[...], NT_DIMS, preferred_element_type=F32)
        gw_ref[...] += lax.dot_general(yv, dx.astype(BF16), TN_DIMS, preferred_element_type=F32)

        @pl.when(pl.program_id(0) == steps - 1)
        def _():
            m = gw_ref[...]
            dg_ref[...] = jnp.sum(w_ref[...].astype(F32) * m, axis=0, keepdims=True)
            gw_ref[...] = m * g_ref[...]

    row = pl.BlockSpec((ts, D_MODEL), lambda i: (i, 0))
    return _pallas(
        body, name=name, grid=(s // ts,), in_specs=[row, row, _full((1, D_MODEL)), _full((D_MODEL, D_MODEL)), row],
        out_specs=[row, _full((1, 1)), row, _full((D_MODEL, D_MODEL)), _full((1, D_MODEL))],
        out_shape=[_sds((s, D_MODEL)), _sds((1, 1)), _sds((s, D_MODEL)), _sds((D_MODEL, D_MODEL)), _sds((1, D_MODEL))],
        operands=(y, x, gate, w_out, target), vmem_mib=48)


def _out_bwd(dxo, y, gate, w_out, name, jobs=()):
    s = dxo.shape[0]
    ts = min(512, s)
    steps = s // ts

    def body(dx_ref, y_ref, g_ref, w_ref, dy_ref, gw_ref, dg_ref):
        @pl.when(pl.program_id(0) == 0)
        def _():
            gw_ref[...] = jnp.zeros_like(gw_ref)

        dx = dx_ref[...]
        dy_ref[...] = lax.dot_general((dx * g_ref[...]).astype(BF16), w_ref[...], NT_DIMS, preferred_element_type=F32)
        gw_ref[...] += lax.dot_general(y_ref[...], dx.astype(BF16), TN_DIMS, preferred_element_type=F32)

        @pl.when(pl.program_id(0) == steps - 1)
        def _():
            m = gw_ref[...]
            dg_ref[...] = jnp.sum(w_ref[...].astype(F32) * m, axis=0, keepdims=True)
            gw_ref[...] = m * g_ref[...]

    row = pl.BlockSpec((ts, D_MODEL), lambda i: (i, 0))
    return _pallas(
        body, name=name, grid=(s // ts,), in_specs=[row, row, _full((1, D_MODEL)), _full((D_MODEL, D_MODEL))],
        out_specs=[row, _full((D_MODEL, D_MODEL)), _full((1, D_MODEL))],
        out_shape=[_sds((s, D_MODEL)), _sds((D_MODEL, D_MODEL)), _sds((1, D_MODEL))],
        operands=(dxo, y, gate, w_out), vmem_mib=48, jobs=jobs)


def _mix_bwd(dy, proj, probs, psink, attn, sv, qg2, kg2, wst, name, jobs=()):
    s = proj.shape[0]
    nt = s // TILE

    def body(dy_ref, cur_ref, pkv_ref, nkv_ref, p_ref, ps_ref, attn_ref, sv_ref, qg_ref, kg_ref, wst_ref,
             dpb_ref, dkv_ref, p0_ref, p2_ref, dqg_ref, dkg_ref, dsink_ref, dws_ref, dbsp_ref):
        def put(rows, col, value):
            dpb_ref[rows, col:col + BLOCK] = value.astype(BF16)

        @pl.when(pl.program_id(0) == 0)
        def _():
            dqg_ref[...] = jnp.zeros_like(dqg_ref)
            dkg_ref[...] = jnp.zeros_like(dkg_ref)
            dsink_ref[...] = jnp.zeros_like(dsink_ref)
            dws_ref[...] = jnp.zeros_like(dws_ref)
            dbsp_ref[...] = jnp.zeros_like(dbsp_ref)

        lo = _lo_mask((BLOCK, BLOCK))
        lo_kv = _lo_mask((TILE + 2 * BLOCK, BLOCK))
        eye = _diagonal()
        lane_row = lax.broadcasted_iota(jnp.int32, (1, BLOCK), 1)
        qg = qg_ref[...]
        kg = kg_ref[...]

        k_all, v_all = _kv_rows(cur_ref, pkv_ref, nkv_ref)
        rk = _half_rms_scale(k_all, lo_kv)
        khat = k_all * rk
        kn_all = (khat * kg).astype(BF16)
        vb_all = v_all.astype(BF16)

        dkn_parts, dv_parts = [], []
        dsink = jnp.zeros((1, BLOCK), F32)
        dqg = jnp.zeros((1, BLOCK), F32)
        for b in range(SUB):
            rows = slice(BLOCK * b, BLOCK * (b + 1))
            window = slice(BLOCK * b, BLOCK * (b + 3))
            kn, vb = kn_all[window], vb_all[window]

            qhat, rq = [], []
            for p in range(4):
                q = cur_ref[rows, C_Q + BLOCK * p:C_Q + BLOCK * (p + 1)]
                r = _half_rms_scale(q, lo)
                rq.append(r)
                qhat.append(q * r)
            q_stack = _stack_heads([(qh * qg) * 0.125 for qh in qhat], lo).astype(BF16)
            pb = p_ref[STACK * b:STACK * (b + 1), :]
            prob = pb.astype(F32)

            dout = []
            for p in range(4):
                g = cur_ref[rows, C_GA + BLOCK * p:C_GA + BLOCK * (p + 1)]
                sg = _sigmoid(g)
                dya = dy_ref[rows, BLOCK * p:BLOCK * (p + 1)]
                attn = attn_ref[rows, BLOCK * p:BLOCK * (p + 1)]
                put(rows, C_GA + BLOCK * p, dya * attn * (sg * (1.0 + g * (1.0 - sg))))
                dout.append(dya * (g * sg))
            do_stack = _stack_heads(dout, lo).astype(BF16)
            dp = lax.dot_general(do_stack, vb, NT_DIMS, preferred_element_type=F32)
            delta = jnp.sum(prob * dp, axis=-1, keepdims=True)
            dsb = (prob * (dp - delta)).astype(BF16)

            for h in range(N_HEADS):
                delta_row = _column_as_row(jnp.broadcast_to(delta[BLOCK * h:BLOCK * (h + 1)], (BLOCK, BLOCK)), eye)
                tot = jnp.sum(ps_ref[N_HEADS * b + h:N_HEADS * b + h + 1, :] * delta_row, axis=-1, keepdims=True)
                dsink = dsink - jnp.where(lane_row == h, tot, 0.0)

            dq_stack = jnp.dot(dsb, kn, preferred_element_type=F32) * 0.125
            dkn_parts.append(lax.dot_general(dsb, q_stack, TN_DIMS, preferred_element_type=F32))
            dv_parts.append(lax.dot_general(pb, do_stack, TN_DIMS, preferred_element_type=F32))

            for p in range(4):
                dqn = _unstack_pair(dq_stack, p, lo)
                qh = qhat[p]
                dqg = dqg + jnp.sum(dqn * qh, axis=0, keepdims=True)
                dqh = dqn * qg
                mean = _half_sum(dqh * qh, lo) * (1.0 / HEAD_DIM)
                put(rows, C_Q + BLOCK * p, rq[p] * (dqh - qh * mean))

        dws_new, dbs_new = [], []
        for p in range(4):
            rs, vnfs, vns, dsvs, dbs = [], [], [], [], None
            for b in range(SUB):
                rows = slice(BLOCK * b, BLOCK * (b + 1))
                vg = cur_ref[rows, C_VG + BLOCK * p:C_VG + BLOCK * (p + 1)]
                r = _half_rms_scale(vg, lo)
                vnf = vg * r
                sv = sv_ref[rows, BLOCK * p:BLOCK * (p + 1)]
                u = cur_ref[rows, C_U + BLOCK * p:C_U + BLOCK * (p + 1)]
                g = cur_ref[rows, C_GG + BLOCK * p:C_GG + BLOCK * (p + 1)]
                sg = _sigmoid(g)
                dym = dy_ref[rows, D_ATTN + BLOCK * p:D_ATTN + BLOCK * (p + 1)]
                put(rows, C_GG + BLOCK * p, dym * (u * sv) * (sg * (1.0 + g * (1.0 - sg))))
                dgm = dym * (g * sg)
                put(rows, C_U + BLOCK * p, dgm * sv)
                dsv = dgm * u
                term = jnp.where(lo, jnp.sum(jnp.where(lo, dsv, 0.0), axis=-1, keepdims=True),
                                 jnp.sum(jnp.where(lo, 0.0, dsv), axis=-1, keepdims=True))
                dbs = term if dbs is None else dbs + term
                rs.append(r)
                vnfs.append(vnf)
                vns.append(vnf.astype(BF16))
                dsvs.append(dsv)
            dbs_new.append(dbs)
            vn = jnp.concatenate(vns, axis=1)
            dsv = jnp.concatenate(dsvs, axis=1)
            lo_t = (lax.broadcasted_iota(jnp.int32, dsv.shape, 1) & (BLOCK - 1)) < HEAD_DIM
            dws_new.append(lax.dot_general(jnp.where(lo_t, dsv, 0.0).astype(BF16), vn, NT_DIMS, preferred_element_type=F32))
            dws_new.append(lax.dot_general(jnp.where(lo_t, 0.0, dsv).astype(BF16), vn, NT_DIMS, preferred_element_type=F32))
            dsvb = dsv.astype(BF16)
            dvn_a = jnp.dot(wst_ref[2 * p], dsvb, preferred_element_type=F32)
            dvn_b = jnp.dot(wst_ref[2 * p + 1], dsvb, preferred_element_type=F32)
            for b in range(SUB):
                lanes = slice(BLOCK * b, BLOCK * (b + 1))
                dvn = jnp.where(lo, dvn_a[:, lanes], dvn_b[:, lanes])
                mean = _half_sum(dvn * vnfs[b], lo) * (1.0 / HEAD_DIM)
                put(slice(BLOCK * b, BLOCK * (b + 1)), C_VG + BLOCK * p, rs[b] * (dvn - vnfs[b] * mean))

        dsink_ref[...] += dsink
        dqg = jnp.broadcast_to(dqg, (8, BLOCK))
        dqg_ref[...] += dqg + pltpu.roll(dqg, HEAD_DIM, 1)

        dkn = _overlap_add(dkn_parts)
        dv = _overlap_add(dv_parts)
        dkg = jnp.broadcast_to(jnp.sum(dkn * khat, axis=0, keepdims=True), (8, BLOCK))
        dkg_ref[...] += dkg + pltpu.roll(dkg, HEAD_DIM, 1)
        dkh = dkn * kg
        dk = rk * (dkh - khat * (_half_sum(dkh * khat, lo_kv) * (1.0 / HEAD_DIM)))
        dpb_ref[:, C_K:C_GA] = jnp.zeros((TILE, 2 * D_KV), BF16)
        dkv_ref[:, 0:D_KV] = dk[BLOCK:BLOCK + TILE]
        dkv_ref[:, D_KV:2 * D_KV] = dv[BLOCK:BLOCK + TILE]
        p0_ref[:, 0:D_KV] = dk[0:BLOCK]
        p0_ref[:, D_KV:2 * D_KV] = dv[0:BLOCK]
        p2_ref[:, 0:D_KV] = dk[BLOCK + TILE:]
        p2_ref[:, D_KV:2 * D_KV] = dv[BLOCK + TILE:]
        for g, new in enumerate(dws_new):
            dws_ref[g] += new
        for p, new in enumerate(dbs_new):
            dbsp_ref[p] += new

    cur, pkv, nkv, _ = _mix_specs(nt)
    kv_blk = (BLOCK, 2 * D_KV)
    half = pl.BlockSpec((TILE, D_ATTN), lambda i: (i, 0))
    return _pallas(
        body, name=name, grid=(nt,),
        in_specs=[pl.BlockSpec((TILE, D_MODEL), lambda i: (i, 0)), cur, pkv, nkv,
                  pl.BlockSpec((SUB * STACK, 3 * BLOCK), lambda i: (i, 0)), pl.BlockSpec((SUB * N_HEADS, BLOCK), lambda i: (i, 0)),
                  half, half, _full((1, BLOCK)), _full((1, BLOCK)), _full((8, BLOCK, BLOCK))],
        out_specs=[cur, pl.BlockSpec((TILE, 2 * D_KV), lambda i: (i, 0)),
                   pl.BlockSpec(kv_blk, lambda i: ((i + nt - 1) % nt, 0)),
                   pl.BlockSpec(kv_blk, lambda i: ((i + 1) % nt, 0)),
                   _full((8, BLOCK)), _full((8, BLOCK)), _full((1, BLOCK)),
                   _full((8, BLOCK, BLOCK)), _full((4, BLOCK, BLOCK))],
        out_shape=[_sds((s, D_IN), BF16), _sds((s, 2 * D_KV)), _sds((nt * BLOCK, 2 * D_KV)), _sds((nt * BLOCK, 2 * D_KV)),
                   _sds((8, BLOCK)), _sds((8, BLOCK)), _sds((1, BLOCK)),
                   _sds((8, BLOCK, BLOCK)), _sds((4, BLOCK, BLOCK))],
        operands=(dy, proj, proj, proj, probs, psink, attn, sv, qg2, kg2, wst), vmem_mib=56, jobs=jobs)


def _w_in_grad(dpb, dkv, p0, p2, x, ng, scale, shift, name, jobs=()):
    s = x.shape[0]
    ts = min(2 * TILE, s)
    tiles = ts // TILE

    def body(dpb_ref, dkv_ref, p0_ref, p2_ref, x_ref, ng_ref, sc_ref, sh_ref, gw_ref, dkvb_ref):
        @pl.when(pl.program_id(0) == 0)
        def _():
            gw_ref[...] = jnp.zeros_like(gw_ref)

        xv = x_ref[...]
        r = lax.rsqrt(jnp.mean(xv * xv, axis=-1, keepdims=True) + EPS)
        h = (((xv * r) * ng_ref[...]) * (1.0 + sc_ref[...]) + sh_ref[...]).astype(BF16)
        for t in range(tiles):
            halo = slice(BLOCK * t, BLOCK * (t + 1))
            first = slice(TILE * t, TILE * t + BLOCK)
            last = slice(TILE * (t + 1) - BLOCK, TILE * (t + 1))
            dkvb_ref[first, :] = (dkv_ref[first, :] + p2_ref[halo, :]).astype(BF16)
            if SUB > 2:
                inner = slice(TILE * t + BLOCK, TILE * (t + 1) - BLOCK)
                dkvb_ref[inner, :] = dkv_ref[inner, :].astype(BF16)
            dkvb_ref[last, :] = (dkv_ref[last, :] + p0_ref[halo, :]).astype(BF16)
        gw_ref[...] += lax.dot_general(dpb_ref[...], h, TN_DIMS, preferred_element_type=F32)
        gw_ref[C_K:C_GA, :] += lax.dot_general(dkvb_ref[...], h, TN_DIMS, preferred_element_type=F32)

    kv = pl.BlockSpec((ts, 2 * D_KV), lambda i: (i, 0))
    halo = pl.BlockSpec((tiles * BLOCK, 2 * D_KV), lambda i: (i, 0))
    vec = _full((1, D_MODEL))
    return _pallas(
        body, name=name, grid=(s // ts,),
        in_specs=[pl.BlockSpec((ts, D_IN), lambda i: (i, 0)), kv, halo, halo,
                  pl.BlockSpec((ts, D_MODEL), lambda i: (i, 0)), vec, vec, vec],
        out_specs=[_full((D_IN, D_MODEL)), kv], out_shape=[_sds((D_IN, D_MODEL)), _sds((s, 2 * D_KV), BF16)],
        operands=(dpb, dkv, p0, p2, x, ng, scale, shift), vmem_mib=56, jobs=jobs)


def _proj_bwd(dpb, dkvb, x, dxo, ng, scale, wt, name, jobs=()):
    s = x.shape[0]
    ts = min(512, s)

    def body(dpb_ref, dkvb_ref, x_ref, dxo_ref, ng_ref, sc_ref, w_ref, dxi_ref, dsh_ref, dsc_ref, dng_ref):
        @pl.when(pl.program_id(0) == 0)
        def _():
            dsh_ref[...] = jnp.zeros_like(dsh_ref)
            dsc_ref[...] = jnp.zeros_like(dsc_ref)
            dng_ref[...] = jnp.zeros_like(dng_ref)

        dh = (jnp.dot(dpb_ref[...], w_ref[...], preferred_element_type=F32)
              + jnp.dot(dkvb_ref[...], w_ref[C_K:C_GA, :], preferred_element_type=F32))

        xv = x_ref[...]
        r = lax.rsqrt(jnp.mean(xv * xv, axis=-1, keepdims=True) + EPS)
        xn = xv * r
        ngv = ng_ref[...]
        sc1 = 1.0 + sc_ref[...]
        dsh_ref[...] += jnp.sum(dh, axis=0, keepdims=True)
        dsc_ref[...] += jnp.sum(dh * (xn * ngv), axis=0, keepdims=True)
        dh1 = dh * sc1
        dng_ref[...] += jnp.sum(dh1 * xn, axis=0, keepdims=True)
        dxn = dh1 * ngv
        dxi_ref[...] = r * (dxn - xn * jnp.mean(dxn * xn, axis=-1, keepdims=True)) + dxo_ref[...]

    row = pl.BlockSpec((ts, D_MODEL), lambda i: (i, 0))
    vec = _full((1, D_MODEL))
    return _pallas(
        body, name=name, grid=(s // ts,),
        in_specs=[pl.BlockSpec((ts, D_IN), lambda i: (i, 0)), pl.BlockSpec((ts, 2 * D_KV), lambda i: (i, 0)),
                  row, row, vec, vec, _full((D_IN, D_MODEL))],
        out_specs=[row, vec, vec, vec],
        out_shape=[_sds((s, D_MODEL)), _sds((1, D_MODEL)), _sds((1, D_MODEL)), _sds((1, D_MODEL))],
        operands=(dpb, dkvb, x, dxo, ng, scale, wt), vmem_mib=48, jobs=jobs)


def _pair_sum(g, r, c_idx, name, send_dtype=None):
    _, rows, cols = g.shape
    half = rows // 2

    def body(c_ref, g_ref, r_ref, o_ref, *narrow):
        total = g_ref[...] + r_ref[...]
        o_ref[...] = total
        for n_ref in narrow:
            n_ref[...] = total.astype(n_ref.dtype)

    blk = (None, half, cols)
    out_blk = pl.BlockSpec(blk, lambda j, c: (j, 0, 0))
    shapes = [pltpu.HBM((N_CHIPS, half, cols), F32)] + ([pltpu.HBM((N_CHIPS, half, cols), send_dtype)] if send_dtype else [])
    return pl.pallas_call(
        body, name=name,
        grid_spec=pltpu.PrefetchScalarGridSpec(
            num_scalar_prefetch=1, grid=(N_CHIPS,),
            in_specs=[pl.BlockSpec(blk, lambda j, c: (j, c[0], 0)), out_blk], out_specs=[out_blk] * len(shapes)),
        out_shape=shapes,
        compiler_params=pltpu.CompilerParams(dimension_semantics=("arbitrary",), vmem_limit_bytes=32 * MIB),
    )(*_in_hbm(c_idx, g, r))


def _chip_sum(p, r, place, name):
    _, rows, cols = p.shape
    tr = rows // 2

    def body(j_ref, p_ref, r_ref, o_ref):
        o_ref[...] = ((p_ref[...] + r_ref[0].astype(F32)) + r_ref[1].astype(F32)) + r_ref[2].astype(F32)

    return pl.pallas_call(
        body, name=name,
        grid_spec=pltpu.PrefetchScalarGridSpec(
            num_scalar_prefetch=1, grid=(2,),
            in_specs=[pl.BlockSpec((None, tr, cols), lambda t, j: (j[0], t, 0)),
                      pl.BlockSpec((3, tr, cols), lambda t, j: (0, t, 0))],
            out_specs=pl.BlockSpec((tr, cols), lambda t, j: (2 * j[1] + t, 0))),
        out_shape=pltpu.HBM((2 * rows, cols), F32),
        compiler_params=pltpu.CompilerParams(dimension_semantics=("arbitrary",), vmem_limit_bytes=32 * MIB),
    )(*_in_hbm(place, p, r))


def _cast_permute_w_in(wt, place_chunks):
    def body(t_ref, w_ref, w0_ref, w1_ref):
        def cast_into(o_ref):
            for t in range(N_CHUNKS):
                src = pl.multiple_of(t_ref[1 + t] * CHUNK_ROWS, CHUNK_ROWS)
                o_ref[CHUNK_ROWS * t:CHUNK_ROWS * (t + 1), :] = w_ref[pl.ds(src, CHUNK_ROWS), :].astype(BF16)

        @pl.when(pl.program_id(0) == 0)
        def _():
            cast_into(w0_ref)

        @pl.when(pl.program_id(0) == 1)
        def _():
            cast_into(w1_ref)

    return pl.pallas_call(
        body, name="cast_permute_w_in",
        grid_spec=pltpu.PrefetchScalarGridSpec(
            num_scalar_prefetch=1, grid=(DEPTH,),
            in_specs=[pl.BlockSpec((None, W_IN_BLK, D_MODEL), lambda l, tbl: (l, 0, 0))],
            out_specs=[pl.BlockSpec((None, W_IN_BLK, D_MODEL), lambda l, tbl: (tbl[0], 0, 0)),
                       pl.BlockSpec((W_IN_BLK, D_MODEL), lambda l, tbl: (0, 0))]),
        out_shape=[pltpu.HBM((N_CHIPS, W_IN_BLK, D_MODEL), BF16), pltpu.HBM((W_IN_BLK, D_MODEL), BF16)],
        compiler_params=pltpu.CompilerParams(dimension_semantics=("arbitrary",), vmem_limit_bytes=32 * MIB),
    )(*_in_hbm(place_chunks, wt))


def _gather_inputs(c, w_out, w0):
    half = W_IN_BLK // 2

    def body(c_ref, wout_ref, mine_ref, call_ref, woutb_ref, w0_ref, send_sems, recv_sems):
        x, y, cc = _coords()
        j = 2 * x + y
        b = 2 * j + cc
        sib = (x, y, 1 - cc)
        woutb_ref[...] = wout_ref[...].astype(BF16)
        call_ref[b] = c_ref[...]
        chips = _other_chips(x, y)

        def sems(k):
            return send_sems.at[k], recv_sems.at[k]

        def half_rows(chip_index):
            return w0_ref.at[chip_index, pl.ds(cc * half, half), :]

        first = [_remote(mine_ref.at[j, pl.ds(cc * half, half), :], half_rows(j), sems(k), (*chip, cc))
                 for k, chip in enumerate(chips)]
        k = 3
        rest = []
        for fx in (0, 1):
            for fy in (0, 1):
                for fc in (0, 1):
                    if fx or fy or fc:
                        dev = (1 - x if fx else x, 1 - y if fy else y, 1 - cc if fc else cc)
                        rest.append(_remote(call_ref.at[b], call_ref.at[b], sems(k), dev))
                        k += 1
        for cp in first + rest:
            cp.start()
        passed = []
        for k, chip in enumerate(chips):
            jk = 2 * chip[0] + chip[1]
            first[k].wait_recv()
            passed.append(_remote(half_rows(jk), half_rows(jk), sems(10 + k), sib))
            passed[k].start()
        for cp in first:
            cp.wait_send()
        for cp in rest + passed:
            cp.wait()

    return pl.pallas_call(
        body, name="gather_inputs", in_specs=[VMEM, VMEM, ANY], out_specs=[VMEM, VMEM, ANY],
        out_shape=[_sds((N_DEV, 1, D_MODEL)), _sds((DEPTH, W_OUT_BLK, D_MODEL), BF16),
                   _sds((N_CHIPS, W_IN_BLK, D_MODEL), BF16)],
        scratch_shapes=[pltpu.SemaphoreType.DMA((13,)), pltpu.SemaphoreType.DMA((13,))],
        input_output_aliases={2: 2},
        compiler_params=pltpu.CompilerParams(vmem_limit_bytes=32 * MIB),
    )(c, w_out, w0)


def _ada_rows(c_all, w_ada, b_blk):
    def body(c_ref, w_ref, b_ref, o_ref, cond_ref):
        cv = c_ref[...]
        cond = (cv * _sigmoid(cv)).astype(BF16)
        cond_ref[...] = cond.astype(F32)
        for l in range(DEPTH):
            o_ref[:, l, :] = jnp.dot(cond, w_ref[l].astype(BF16), preferred_element_type=F32) + b_ref[l:l + 1, :]

    return pl.pallas_call(
        body, name="ada_rows", in_specs=[VMEM, VMEM, VMEM], out_specs=[VMEM, VMEM],
        out_shape=[_sds((N_DEV, DEPTH, W_ADA_BLK)), _sds((N_DEV, D_MODEL))],
        compiler_params=pltpu.CompilerParams(vmem_limit_bytes=32 * MIB),
    )(c_all, w_ada, b_blk)


def _exchange_ada(part):
    def body(part_ref, out_ref, send_sems, recv_sems):
        x, y, cc = _coords()
        j = 2 * x + y
        out_ref[j] = part_ref[2 * j + cc]
        copies = []
        for k, chip in enumerate(_other_chips(x, y)):
            b_dst = 4 * chip[0] + 2 * chip[1] + cc
            copies.append(_remote(part_ref.at[b_dst], out_ref.at[j], (send_sems.at[k], recv_sems.at[k]), (*chip, cc)))
        for cp in copies:
            cp.start()
        for cp in copies:
            cp.wait()

    return pl.pallas_call(
        body, name="exchange_ada", in_specs=[VMEM], out_specs=VMEM,
        out_shape=_sds((N_CHIPS, DEPTH, W_ADA_BLK)),
        scratch_shapes=[pltpu.SemaphoreType.DMA((3,)), pltpu.SemaphoreType.DMA((3,))],
    )(part)


def _adamw_math(w, g, m, v):
    m = ADAM_B1 * m + (1.0 - ADAM_B1) * g
    v = ADAM_B2 * v + (1.0 - ADAM_B2) * (g * g)
    m_hat = m / (1.0 - ADAM_B1 ** ADAM_STEP)
    v_hat = v / (1.0 - ADAM_B2 ** ADAM_STEP)
    delta = -ADAM_LR * (m_hat / (jnp.sqrt(v_hat) + ADAM_EPS) + ADAM_WD * w)
    return delta, m, v


def _adamw_w_in(w, g0, g1, m, v, pos_chunks):
    def body(t_ref, w_ref, g0_ref, g1_ref, m_ref, v_ref, g_ref, d_ref, nm_ref, nv_ref):
        for l, src in enumerate((g0_ref, g1_ref)):
            g = src[...]
            g_ref[l] = g
            d_ref[l], nm_ref[l], nv_ref[l] = _adamw_math(w_ref[l], g, m_ref[l], v_ref[l])

    nat = pl.BlockSpec((DEPTH, CHUNK_ROWS, D_MODEL), lambda t, tbl: (0, t, 0))
    per = pl.BlockSpec((CHUNK_ROWS, D_MODEL), lambda t, tbl: (tbl[t], 0))
    return pl.pallas_call(
        body, name="adamw_w_in",
        grid_spec=pltpu.PrefetchScalarGridSpec(num_scalar_prefetch=1, grid=(N_CHUNKS,),
                                               in_specs=[nat, per, per, nat, nat], out_specs=[nat] * 4),
        out_shape=[_sds(w.shape)] * 4,
        compiler_params=pltpu.CompilerParams(dimension_semantics=("arbitrary",)),
    )(*_in_hbm(pos_chunks, w, g0, g1, m, v))


def _adamw_w_out(w, g0, g1, m, v):
    def body(w_ref, g0_ref, g1_ref, m_ref, v_ref, g_ref, d_ref, nm_ref, nv_ref):
        g = jnp.where(pl.program_id(0) == 0, g0_ref[...], g1_ref[...])
        g_ref[...] = g
        d_ref[...], nm_ref[...], nv_ref[...] = _adamw_math(w_ref[...], g, m_ref[...], v_ref[...])

    blk = pl.BlockSpec((None, W_OUT_BLK, D_MODEL), lambda l: (l, 0, 0))
    gblk = _full((W_OUT_BLK, D_MODEL))
    return pl.pallas_call(
        body, name="adamw_w_out", grid=(DEPTH,), in_specs=[blk, gblk, gblk, blk, blk], out_specs=[blk] * 4,
        out_shape=[_sds(w.shape)] * 4,
        compiler_params=pltpu.CompilerParams(dimension_semantics=("arbitrary",), vmem_limit_bytes=32 * MIB),
    )(w, g0, g1, m, v)


def _w_ada_grad_adamw(cond_t, dada, w, m, v):
    _, rows, cols = w.shape
    tr = 256

    def body(ct_ref, da_ref, w_ref, m_ref, v_ref, g_ref, d_ref, nm_ref, nv_ref):
        g = jnp.dot(ct_ref[...], da_ref[...].astype(BF16), preferred_element_type=F32)
        g_ref[...] = g
        d_ref[...], nm_ref[...], nv_ref[...] = _adamw_math(w_ref[...], g, m_ref[...], v_ref[...])

    blk = pl.BlockSpec((None, tr, cols), lambda l, t: (l, t, 0))
    return pl.pallas_call(
        body, name="w_ada_grad_adamw", grid=(DEPTH, rows // tr),
        in_specs=[pl.BlockSpec((tr, BLOCK), lambda l, t: (t, 0)), pl.BlockSpec((None, BLOCK, cols), lambda l, t: (l, 0, 0)),
                  blk, blk, blk],
        out_specs=[blk] * 4, out_shape=[_sds(w.shape)] * 4,
        compiler_params=pltpu.CompilerParams(dimension_semantics=("arbitrary", "arbitrary"), vmem_limit_bytes=32 * MIB),
    )(cond_t, dada, w, m, v)


def _small_sum_adamw(gathered_a, gathered_b, ws, rest):
    n_ws = 8 * BLOCK

    def body(*refs):
        a_refs, b_ref = refs[:DEPTH], refs[DEPTH]
        ws_ref, ms_ref, vs_ref, wr_ref, mr_ref, vr_ref = refs[DEPTH + 1:DEPTH + 7]
        gs_ref, ds_ref, nms_ref, nvs_ref, gr_ref, dr_ref, nmr_ref, nvr_ref = refs[DEPTH + 7:]

        def total(ref):
            g = ref[0]
            for b in range(1, N_DEV):
                g = g + ref[b]
            return g

        totals = [total(ref) for ref in a_refs]
        for l, t in enumerate(totals):
            g = t[0:n_ws]
            gs_ref[l] = g
            ds_ref[l], nms_ref[l], nvs_ref[l] = _adamw_math(ws_ref[l], g, ms_ref[l], vs_ref[l])
        g = jnp.concatenate([t[n_ws:] for t in totals] + [total(b_ref)], axis=0)
        gr_ref[...] = g
        dr_ref[...], nmr_ref[...], nvr_ref[...] = _adamw_math(wr_ref[...], g, mr_ref[...], vr_ref[...])

    return pl.pallas_call(
        body, name="small_sum_adamw", in_specs=[VMEM] * (DEPTH + 7), out_specs=[VMEM] * 8,
        out_shape=[_sds(ws[0].shape)] * 4 + [_sds(rest[0].shape)] * 4,
        compiler_params=pltpu.CompilerParams(vmem_limit_bytes=48 * MIB),
    )(*gathered_a, gathered_b, *ws, *rest)


_SMALL_A_REST = (("b_s", 8), ("q_gain", 1), ("k_gain", 1), ("sink", 1))
_SMALL_A = (("w_s", 8 * BLOCK),) + _SMALL_A_REST
_SMALL_B = (("b_ada", DEPTH * 24), ("norm_gain", DEPTH * 8), ("sq_err", 1))


def _pack_rows(parts, layout, layer=None):
    rows = []
    for name, n in layout:
        flat = (parts[name] if layer is None else parts[name][layer]).reshape(-1)
        rows.append(jnp.pad(flat, (0, n * 128 - flat.shape[0])).reshape(n, 128))
    n_rows = sum(n for _, n in layout)
    if n_rows % 8:
        rows.append(jnp.zeros((-n_rows % 8, 128), F32))
    return jnp.concatenate(rows, axis=0)


def _pack_rest(parts):
    return jnp.concatenate([_pack_rows(parts, _SMALL_A_REST, l) for l in range(DEPTH)] + [_pack_rows(parts, _SMALL_B)], axis=0)


def _unpack_rest(packed, shapes):
    def take(r0, layout, shape_of):
        got = {}
        for name, n in layout:
            shape = shape_of(name)
            size = 1
            for d in shape:
                size *= d
            got[name] = packed[r0:r0 + n].reshape(-1)[:size].reshape(shape)
            r0 += n
        return got, r0 + -r0 % 8

    layers, r0 = [], 0
    for _ in range(DEPTH):
        got, r0 = take(r0, _SMALL_A_REST, lambda name: shapes[name][1:])
        layers.append(got)
    out, _ = take(r0, _SMALL_B, lambda name: shapes[name])
    out.update({name: jnp.stack([layer[name] for layer in layers]) for name, _ in _SMALL_A_REST})
    return out


def _permute_heads(a, axis):
    shp = a.shape
    a = a.reshape(shp[:axis] + (2, 4, HEAD_DIM) + shp[axis + 1:])
    a = jnp.swapaxes(a, axis, axis + 1)
    return a.reshape(shp)


def _unpermute_heads(a, axis):
    shp = a.shape
    a = a.reshape(shp[:axis] + (4, 2, HEAD_DIM) + shp[axis + 1:])
    a = jnp.swapaxes(a, axis, axis + 1)
    return a.reshape(shp)


def _permute_w_out(w):
    return jnp.concatenate([_permute_heads(w[:D_ATTN], 0), w[D_ATTN:]], axis=0)


def _unpermute_w_out(w):
    return jnp.concatenate([_unpermute_heads(w[:D_ATTN], 0), w[D_ATTN:]], axis=0)


def kernel(x, c, w_ada, b_ada, norm_gain, w_in, q_gain, k_gain, sink, w_s, b_s, w_out, loss_target, m_w_ada, m_b_ada, m_norm_gain, m_w_in, m_q_gain, m_k_gain, m_sink, m_w_s, m_b_s, m_w_out, v_w_ada, v_b_ada, v_norm_gain, v_w_in, v_q_gain, v_k_gain, v_sink, v_w_s, v_b_s, v_w_out):
    ix, iy, ic = _coords()
    chip = 2 * ix + iy
    chip_idx = jnp.stack([chip, ic]).astype(jnp.int32)
    core_idx = jnp.reshape(ic, (1,)).astype(jnp.int32)
    src_chunks = lax.dynamic_index_in_dim(jnp.asarray(_CHUNK_SRC), chip, 0, keepdims=False)
    pos_chunks = lax.dynamic_index_in_dim(jnp.asarray(_CHUNK_POS), chip, 0, keepdims=False)
    x0, target = x[0], loss_target[0]

    wt, mt, vt = (jnp.swapaxes(a, 1, 2) for a in (w_in, m_w_in, v_w_in))
    w0_mine, wloc_in1 = _cast_permute_w_in(wt, jnp.concatenate([chip_idx[:1], src_chunks]))
    c_all, wloc_out, w0 = _gather_inputs(c, w_out, w0_mine)
    wts = [w0.reshape(D_IN, D_MODEL), None]

    b_blk = lax.dynamic_slice_in_dim(b_ada, chip * W_ADA_BLK, W_ADA_BLK, axis=1)
    ada_part, cond = _ada_rows(c_all.reshape(N_DEV, D_MODEL), w_ada, b_blk)
    ada = jnp.moveaxis(_exchange_ada(ada_part), 0, 1).reshape(DEPTH, 3 * D_MODEL)
    shift = [ada[l:l + 1, 0:D_MODEL] for l in range(DEPTH)]
    scale = [ada[l:l + 1, D_MODEL:2 * D_MODEL] for l in range(DEPTH)]
    gate = [ada[l:l + 1, 2 * D_MODEL:] for l in range(DEPTH)]
    ng = [norm_gain[l:l + 1] for l in range(DEPTH)]

    qg2 = jnp.concatenate([q_gain, q_gain], axis=-1)
    kg2 = jnp.concatenate([k_gain, k_gain], axis=-1)
    ws_b = w_s.astype(BF16)
    wst_b = jnp.swapaxes(w_s, -1, -2).astype(BF16)
    bsp = jnp.repeat(jnp.swapaxes(b_s.reshape(DEPTH, 4, 2, BLOCK), -1, -2), HEAD_DIM, axis=-1)
    bias = jnp.asarray(_bias_table())

    def mix_args(l):
        return bias, sink[l], qg2[l:l + 1], kg2[l:l + 1], ws_b[l]

    w_out_shape = (W_OUT_BLK, D_MODEL)
    proj0, wo0, wo1 = _proj_fwd(x0, ng[0], scale[0], shift[0], wts[0], "proj_fwd_0",
                                jobs=[_job_gather([(wloc_out, 0), (wloc_out, 1)], [w_out_shape, w_out_shape])])
    y0, *kept0, w1 = _mix_fwd(proj0, *mix_args(0), bsp[0], "mix_fwd_0",
                              jobs=[_job_gather([(wloc_in1, None)], [(W_IN_BLK, D_MODEL)])])
    wts[1] = w1.reshape(D_IN, D_MODEL)
    wos = [_permute_w_out(w.reshape(D_MODEL, D_MODEL)) for w in (wo0, wo1)]
    x1, proj1 = _out_proj_fwd(y0, x0, gate[0], wos[0], ng[1], scale[1], shift[1], wts[1], "out_proj_fwd_01")
    y1, *kept1 = _mix_fwd(proj1, *mix_args(1), bsp[1], "mix_fwd_1")

    def blocks_out(gw):
        return _unpermute_w_out(gw).reshape(N_CHIPS, W_OUT_BLK, D_MODEL)

    def small_pack(dws, dbsp, dqg, dkg, dsink):
        b_s = jnp.swapaxes(dbsp[:, :, ::HEAD_DIM], -1, -2).reshape(8, BLOCK)
        return _pack_rows(dict(w_s=dws, b_s=b_s, q_gain=dqg[0, :HEAD_DIM], k_gain=dkg[0, :HEAD_DIM], sink=dsink[0, :N_HEADS]),
                          _SMALL_A)

    dx2, sq, dy1, gwo1, dgate1 = _out_loss_bwd(y1, x1, gate[1], wos[1], target, "out_loss_bwd_1")
    go1 = blocks_out(gwo1)
    dpb, dkv, p0, p2, dqg1, dkg1, dsink1, dws1, dbsp1, ro1 = _mix_bwd(
        dy1, proj1, *kept1, qg2[1:2], kg2[1:2], wst_b[1], "mix_bwd_1", jobs=[_job_swap(go1)])
    po1, = _pair_sum(go1, ro1, core_idx, "pair_sum_w_out_1")
    gwi1, dkvb, co1, gathered_a1 = _w_in_grad(
        dpb, dkv, p0, p2, x1, ng[1], scale[1], shift[1], "w_in_grad_1",
        jobs=[_job_scatter(po1), _job_all_gather(small_pack(dws1, dbsp1, dqg1, dkg1, dsink1))])
    gi1 = gwi1.reshape(N_CHIPS, W_IN_BLK, D_MODEL)
    fo1 = _chip_sum(po1, co1, chip_idx, "chip_sum_w_out_1")
    dx1, dsh1, dsc1, dng1, grad_wo1, ri1 = _proj_bwd(dpb, dkvb, x1, dx2, ng[1], scale[1], wts[1], "proj_bwd_1",
                                                     jobs=[_job_join(fo1), _job_swap(gi1)])
    pi1, = _pair_sum(gi1, ri1, core_idx, "pair_sum_w_in_1")

    dy0, gwo0, dgate0 = _out_bwd(dx1, y0, gate[0], wos[0], "out_bwd_0")
    go0 = blocks_out(gwo0)
    dpb, dkv, p0, p2, dqg0, dkg0, dsink0, dws0, dbsp0, ci1, ro0 = _mix_bwd(
        dy0, proj0, *kept0, qg2[0:1], kg2[0:1], wst_b[0], "mix_bwd_0", jobs=[_job_scatter(pi1), _job_swap(go0)])
    fi1 = _chip_sum(pi1, ci1, chip_idx, "chip_sum_w_in_1")
    po0, = _pair_sum(go0, ro0, core_idx, "pair_sum_w_out_0")

    gwi0, dkvb, grad_wi1, co0, gathered_a0 = _w_in_grad(
        dpb, dkv, p0, p2, x0, ng[0], scale[0], shift[0], "w_in_grad_0",
        jobs=[_job_join(fi1), _job_scatter(po0), _job_all_gather(small_pack(dws0, dbsp0, dqg0, dkg0, dsink0))])
    gi0 = gwi0.reshape(N_CHIPS, W_IN_BLK, D_MODEL)
    fo0 = _chip_sum(po0, co0, chip_idx, "chip_sum_w_out_0")

    ri0, grad_wo0 = _comm([_job_swap(gi0), _job_join(fo0)], "swap_w_in_0")
    pi0, pi0_send = _pair_sum(gi0, ri0, core_idx, "pair_sum_w_in_0", send_dtype=BF16)
    dx0, dsh0, dsc0, dng0, ci0 = _proj_bwd(dpb, dkvb, x0, dx1, ng[0], scale[0], wts[0], "proj_bwd_0",
                                           jobs=[_job_scatter(pi0_send)])
    fi0 = _chip_sum(pi0, ci0, chip_idx, "chip_sum_w_in_0")

    small_g = dict(
        b_ada=jnp.stack([jnp.concatenate([dsh0, dsc0, dgate0], axis=-1)[0], jnp.concatenate([dsh1, dsc1, dgate1], axis=-1)[0]]),
        norm_gain=jnp.stack([dng0[0], dng1[0]]), sq_err=sq[0])
    none = jnp.zeros((1,), F32)
    small_w = dict(w_s=w_s, b_s=b_s, b_ada=b_ada, norm_gain=norm_gain, q_gain=q_gain, k_gain=k_gain, sink=sink, sq_err=none)
    small_m = dict(w_s=m_w_s, b_s=m_b_s, b_ada=m_b_ada, norm_gain=m_norm_gain, q_gain=m_q_gain, k_gain=m_k_gain, sink=m_sink,
                   sq_err=none)
    small_v = dict(w_s=v_w_s, b_s=v_b_s, b_ada=v_b_ada, norm_gain=v_norm_gain, q_gain=v_q_gain, k_gain=v_k_gain, sink=v_sink,
                   sq_err=none)
    grad_wi0, gathered_b = _comm([_job_join(fi0), _job_all_gather(_pack_rows(small_g, _SMALL_B))], "join_w_in_0")
    packed = _small_sum_adamw(
        [a.reshape(N_DEV, -1, 128) for a in (gathered_a0, gathered_a1)], gathered_b.reshape(N_DEV, -1, 128),
        [a.reshape(DEPTH, 8 * BLOCK, BLOCK) for a in (w_s, m_w_s, v_w_s)], [_pack_rest(p) for p in (small_w, small_m, small_v)])
    shapes = {k: a.shape for k, a in small_w.items()}
    sg, sd, sm, sv = (dict(_unpack_rest(rest, shapes), w_s=ws.reshape(w_s.shape)) for ws, rest in zip(packed[:4], packed[4:]))
    loss = 0.5 * sg["sq_err"][0]

    dada_all = gathered_b.reshape(N_DEV, -1, 128)[:, 0:DEPTH * 24].reshape(N_DEV, DEPTH, 3 * D_MODEL)
    dada_blk = jnp.moveaxis(lax.dynamic_slice_in_dim(dada_all, chip * W_ADA_BLK, W_ADA_BLK, axis=2), 0, 1)
    pad = BLOCK - N_DEV
    ada_out = _w_ada_grad_adamw(
        jnp.pad(cond.T, ((0, 0), (0, pad))).astype(BF16), jnp.pad(dada_blk, ((0, 0), (0, pad), (0, 0))),
        w_ada, m_w_ada, v_w_ada)

    in_out = [jnp.swapaxes(a, 1, 2) for a in _adamw_w_in(wt, grad_wi0, grad_wi1, mt, vt, pos_chunks)]
    out_out = _adamw_w_out(w_out, grad_wo0, grad_wo1, m_w_out, v_w_out)

    def ordered(k):
        small = (sg, sd, sm, sv)[k]
        return (ada_out[k], small["b_ada"], small["norm_gain"], in_out[k], small["q_gain"], small["k_gain"], small["sink"],
                small["w_s"], small["b_s"], out_out[k])

    return (loss, dx0[None], *ordered(0), *ordered(1), *ordered(2), *ordered(3))
```

```python
import numpy as np

import jax
import jax.numpy as jnp
from jax import lax
from jax.experimental import pallas as pl
from jax.experimental.pallas import tpu as pltpu

F32 = jnp.float32
BF16 = jnp.bfloat16

D_MODEL = 1024
DEPTH = 2
HEAD_DIM = 64
N_HEADS = 8
BLOCK = 128
SUB = 4
TILE = SUB * BLOCK
STACK = 8 * BLOCK
D_ATTN = 512
D_KV = 128
D_IN = 2816
N_CHIPS = 4
N_DEV = 8
W_IN_BLK = D_IN // N_CHIPS
W_OUT_BLK = D_MODEL // N_CHIPS
W_ADA_BLK = 3 * D_MODEL // N_CHIPS
CHUNK_ROWS = HEAD_DIM
N_CHUNKS = W_IN_BLK // CHUNK_ROWS
EPS = 1e-6
NEG_INF = -1e30

C_Q, C_K, C_V, C_GA, C_U, C_VG, C_GG = 0, 512, 640, 768, 1280, 1792, 2304

ADAM_LR = 0.001
ADAM_B1 = 0.9
ADAM_B2 = 0.999
ADAM_EPS = 1e-08
ADAM_WD = 0.01
ADAM_STEP = 10

MESH = pl.DeviceIdType.MESH
MIB = 1024 * 1024
ANY = pl.BlockSpec(memory_space=pl.ANY)
VMEM = pl.BlockSpec(memory_space=pltpu.VMEM)

NT_DIMS = (((1,), (1,)), ((), ()))
TN_DIMS = (((0,), (0,)), ((), ()))

_PAIR_ORDER = (0, 4, 1, 5, 2, 6, 3, 7)
_CHUNK_SRC = np.array([
    list(_PAIR_ORDER) + [8, 9, 10],
    [0] + [1 + h for h in _PAIR_ORDER] + [9, 10],
    list(range(N_CHUNKS)),
    list(range(N_CHUNKS)),
], np.int32)
_CHUNK_POS = np.argsort(_CHUNK_SRC, axis=1).astype(np.int32)


def _bias_table():
    i = np.arange(N_HEADS * BLOCK)[:, None]
    j = np.arange(3 * BLOCK)[None, :]
    dist = np.abs(j - BLOCK - (i % BLOCK))
    slope = 2.0 ** -(i // BLOCK + 1.0)
    inner = np.where(dist <= BLOCK, -(slope * dist), NEG_INF)
    first = np.where(j >= BLOCK, inner, NEG_INF)
    last = np.where(j < 2 * BLOCK, inner, NEG_INF)
    return np.stack([first, inner, last]).astype(np.float32)


def _full(shape):
    n = len(shape)
    return pl.BlockSpec(shape, lambda *_: (0,) * n)


def _sds(shape, dtype=F32):
    return jax.ShapeDtypeStruct(shape, dtype)


def _coords():
    return lax.axis_index("x"), lax.axis_index("y"), lax.axis_index("c")


def _other_chips(x, y):
    return [(1 - x, y), (x, 1 - y), (1 - x, 1 - y)]


def _in_hbm(*operands):
    return [pltpu.with_memory_space_constraint(a, pltpu.HBM) if a.size * a.dtype.itemsize >= MIB // 4 else a
            for a in operands]


def _remote(src, dst, sems, dev):
    return pltpu.make_async_remote_copy(src_ref=src, dst_ref=dst, send_sem=sems[0], recv_sem=sems[1],
                                        device_id=dev, device_id_type=MESH)


class _Job:
    def __init__(self, inputs, out_shapes, n_remote, n_local, make, then=None, in_place=False):
        self.inputs, self.out_shapes, self.n_remote, self.n_local, self.make = inputs, out_shapes, n_remote, n_local, make
        self.then = then
        self.in_place = in_place


def _job_aliases(jobs, in_base, out_base):
    aliases, a, b = {}, 0, 0
    for j in jobs:
        if j.in_place:
            aliases.update({in_base + a + k: out_base + b + k for k in range(len(j.inputs))})
        a, b = a + len(j.inputs), b + len(j.out_shapes)
    return aliases


def _job_copies(jobs, jin, jout, sems, second=False):
    send, recv, loc = sems
    res, a, b, r, l = [], 0, 0, 0, 0
    for j in jobs:
        build = j.then if second else j.make
        if build is not None:
            res += build(jin[a:a + len(j.inputs)], jout[b:b + len(j.out_shapes)],
                         lambda k, r=r: (send.at[r + k], recv.at[r + k]), lambda k, l=l: loc.at[l + k])
        a, b, r, l = a + len(j.inputs), b + len(j.out_shapes), r + j.n_remote, l + j.n_local
    return res


def _run(copies):
    for cp in copies:
        cp.start()
    for cp in copies:
        cp.wait()


def _job_gather(sources, shapes):
    n = len(sources)

    def make(ins, outs, rsem, lsem):
        x, y, c = _coords()
        j = 2 * x + y
        res = []
        for t, ((_, layer), src, dst) in enumerate(zip(sources, ins, outs)):
            src = src if layer is None else src.at[layer]
            res.append(pltpu.make_async_copy(src, dst.at[j], lsem(t)))
            for k, chip in enumerate(_other_chips(x, y)):
                res.append(_remote(src, dst.at[j], rsem(3 * t + k), (*chip, c)))
        return res

    return _Job([a for a, _ in sources], [_sds((N_CHIPS,) + s, BF16) for s in shapes], 3 * n, n, make)


def _job_swap(g):
    _, rows, cols = g.shape
    half = rows // 2

    def make(ins, outs, rsem, lsem):
        x, y, c = _coords()
        return [_remote(ins[0].at[:, pl.ds((1 - c) * half, half), :], outs[0], rsem(0), (x, y, 1 - c))]

    return _Job([g], [_sds((N_CHIPS, half, cols))], 1, 0, make)


def _job_scatter(p):
    def make(ins, outs, rsem, lsem):
        x, y, c = _coords()
        return [_remote(ins[0].at[2 * chip[0] + chip[1]], outs[0].at[k], rsem(k), (*chip, c))
                for k, chip in enumerate(_other_chips(x, y))]

    return _Job([p], [_sds((3,) + p.shape[1:], p.dtype)], 3, 0, make)


def _job_all_gather(blk):
    m_per = blk.shape[0]

    def rows(ref, px, py, pc):
        return ref.at[pl.ds((4 * px + 2 * py + pc) * m_per, m_per), :]

    def make(ins, outs, rsem, lsem):
        x, y, c = _coords()
        res = [pltpu.make_async_copy(ins[0], rows(outs[0], x, y, c), lsem(0)),
               _remote(ins[0], rows(outs[0], x, y, c), rsem(0), (x, y, 1 - c))]
        res += [_remote(ins[0], rows(outs[0], x, y, c), rsem(1 + k), (*chip, c)) for k, chip in enumerate(_other_chips(x, y))]
        return res

    def then(ins, outs, rsem, lsem):
        x, y, c = _coords()
        return [_remote(rows(outs[0], *chip, c), rows(outs[0], *chip, c), rsem(4 + k), (x, y, 1 - c))
                for k, chip in enumerate(_other_chips(x, y))]

    return _Job([blk], [_sds((N_DEV * m_per, blk.shape[1]), blk.dtype)], 7, 1, make, then)


def _job_join(f):
    half = f.shape[0] // 2

    def make(ins, outs, rsem, lsem):
        x, y, c = _coords()
        mine = pl.ds(c * half, half)
        return [_remote(ins[0].at[mine, :], outs[0].at[mine, :], rsem(0), (x, y, 1 - c))]

    return _Job([f], [_sds(f.shape, f.dtype)], 1, 0, make, in_place=True)


def _pallas(body, *, name, grid, in_specs, out_specs, out_shape, operands, vmem_mib, jobs=()):
    in_specs, out_specs, out_shape = list(in_specs), list(out_specs), list(out_shape)
    n_in, n_out = len(in_specs), len(out_specs)
    j_in = [a for j in jobs for a in j.inputs]
    j_out = [s for j in jobs for s in j.out_shapes]
    n_rem = max(1, sum(j.n_remote for j in jobs))
    n_loc = max(1, sum(j.n_local for j in jobs))
    scratch = [pltpu.SemaphoreType.DMA((n_rem,)), pltpu.SemaphoreType.DMA((n_rem,)),
               pltpu.SemaphoreType.DMA((n_loc,))] if jobs else []

    def wrapped(*refs):
        ins = refs[:n_in]
        jin = refs[n_in:n_in + len(j_in)]
        outs = refs[n_in + len(j_in):n_in + len(j_in) + n_out]
        jout = refs[n_in + len(j_in) + n_out:n_in + len(j_in) + n_out + len(j_out)]

        if jobs:
            first = last = None
            for d, n in enumerate(grid):
                f, e = pl.program_id(d) == 0, pl.program_id(d) == n - 1
                first, last = (f, e) if first is None else (first & f, last & e)

            @pl.when(first)
            def _():
                for cp in _job_copies(jobs, jin, jout, refs[-3:]):
                    cp.start()

        body(*ins, *outs)

        if jobs:
            @pl.when(last)
            def _():
                for cp in _job_copies(jobs, jin, jout, refs[-3:]):
                    cp.wait()
                _run(_job_copies(jobs, jin, jout, refs[-3:], second=True))

    return pl.pallas_call(
        wrapped, name=name, grid=grid,
        in_specs=in_specs + [ANY] * len(j_in), out_specs=out_specs + [ANY] * len(j_out),
        out_shape=out_shape + j_out, scratch_shapes=scratch, input_output_aliases=_job_aliases(jobs, n_in, n_out),
        compiler_params=pltpu.CompilerParams(dimension_semantics=("arbitrary",) * len(grid),
                                             vmem_limit_bytes=vmem_mib * MIB),
    )(*_in_hbm(*operands, *j_in))


def _comm(jobs, name):
    j_in = [a for j in jobs for a in j.inputs]
    j_out = [s for j in jobs for s in j.out_shapes]
    n_rem = max(1, sum(j.n_remote for j in jobs))
    n_loc = max(1, sum(j.n_local for j in jobs))

    def body(*refs):
        jin, jout = refs[:len(j_in)], refs[len(j_in):len(j_in) + len(j_out)]
        _run(_job_copies(jobs, jin, jout, refs[-3:]))
        _run(_job_copies(jobs, jin, jout, refs[-3:], second=True))

    return pl.pallas_call(
        body, name=name, in_specs=[ANY] * len(j_in), out_specs=[ANY] * len(j_out), out_shape=j_out,
        scratch_shapes=[pltpu.SemaphoreType.DMA((n_rem,)), pltpu.SemaphoreType.DMA((n_rem,)),
                        pltpu.SemaphoreType.DMA((n_loc,))],
        input_output_aliases=_job_aliases(jobs, 0, 0),
    )(*_in_hbm(*j_in))


def _sigmoid(x):
    return 1.0 / (1.0 + jnp.exp(-x))


def _lo_mask(shape):
    return lax.broadcasted_iota(jnp.int32, shape, len(shape) - 1) < HEAD_DIM


def _half_sum(x, lo):
    a = jnp.sum(jnp.where(lo, x, 0.0), axis=-1, keepdims=True)
    b = jnp.sum(jnp.where(lo, 0.0, x), axis=-1, keepdims=True)
    return jnp.where(lo, a, b)


def _half_rms_scale(x, lo):
    return lax.rsqrt(_half_sum(x * x, lo) * (1.0 / HEAD_DIM) + EPS)


def _stack_heads(pairs, lo):
    return jnp.concatenate([jnp.where(lo, t, 0.0) for t in pairs] + [jnp.where(lo, 0.0, t) for t in pairs], axis=0)


def _unstack_pair(stack, p, lo):
    return jnp.where(lo, stack[BLOCK * p:BLOCK * (p + 1)], stack[BLOCK * (4 + p):BLOCK * (5 + p)])


def _attention_probs(q_stack, kn, bias_ref, sink_ref):
    rows = N_HEADS * BLOCK
    s = lax.dot_general(q_stack, kn, NT_DIMS, preferred_element_type=F32) + bias_ref[...]
    sink = jnp.concatenate([jnp.full((BLOCK, BLOCK), sink_ref[h], F32) for h in range(N_HEADS)], axis=0)
    cols = [s[:, BLOCK * j:BLOCK * (j + 1)] for j in range(3)]
    top = jnp.max(jnp.maximum(jnp.maximum(cols[0], cols[1]), cols[2]), axis=-1, keepdims=True)
    m = jnp.maximum(jnp.broadcast_to(top, (rows, BLOCK)), sink)
    e = [jnp.exp(c - m) for c in cols]
    es = jnp.exp(sink - m)
    inv = 1.0 / (jnp.broadcast_to(jnp.sum((e[0] + e[1]) + e[2], axis=-1, keepdims=True), (rows, BLOCK)) + es)
    return jnp.concatenate([c * inv for c in e], axis=1), es * inv


def _kv_rows(cur_ref, pkv_ref, nkv_ref):
    k = jnp.concatenate([pkv_ref[:, 0:D_KV], cur_ref[:, C_K:C_K + D_KV], nkv_ref[:, 0:D_KV]], axis=0)
    v = jnp.concatenate([pkv_ref[:, D_KV:2 * D_KV], cur_ref[:, C_V:C_V + D_KV], nkv_ref[:, D_KV:2 * D_KV]], axis=0)
    return k, v


def _overlap_add(parts):
    blocks = []
    for j in range(SUB + 2):
        terms = [parts[b][BLOCK * (j - b):BLOCK * (j - b + 1)] for b in range(SUB) if 0 <= j - b <= 2]
        total = terms[0]
        for t in terms[1:]:
            total = total + t
        blocks.append(total)
    return jnp.concatenate(blocks, axis=0)


def _mix_specs(nt):
    cur = pl.BlockSpec((TILE, D_IN), lambda i: (i, 0))
    kv_col = C_K // (2 * D_KV)
    pkv = pl.BlockSpec((BLOCK, 2 * D_KV), lambda i: (jnp.maximum(i * SUB - 1, 0), kv_col))
    nkv = pl.BlockSpec((BLOCK, 2 * D_KV), lambda i: (jnp.minimum((i + 1) * SUB, nt * SUB - 1), kv_col))
    table = (None, N_HEADS * BLOCK, 3 * BLOCK)
    first = pl.BlockSpec(table, lambda i: (jnp.where(i == 0, 0, 1), 0, 0))
    inner = pl.BlockSpec(table, lambda i: (1, 0, 0))
    last = pl.BlockSpec(table, lambda i: (jnp.where(i == nt - 1, 2, 1), 0, 0))
    return cur, pkv, nkv, [first] + [inner] * (SUB - 2) + [last]


V_NG, V_SCALE, V_SHIFT, V_GATE = 0, 1, 2, 3
VECS = _full((8, D_MODEL))


def _row(v_ref, k):
    return v_ref[k:k + 1, :]


def _adaln(xv, v_ref):
    r = lax.rsqrt(jnp.mean(xv * xv, axis=-1, keepdims=True) + EPS)
    return ((xv * r) * _row(v_ref, V_NG)) * (1.0 + _row(v_ref, V_SCALE)) + _row(v_ref, V_SHIFT)


def _proj_fwd(x, vecs, wt, name, jobs=()):
    s = x.shape[0]
    ts = min(512, s)

    def body(x_ref, v_ref, w_ref, o_ref):
        h = _adaln(x_ref[...], v_ref)
        o_ref[...] = lax.dot_general(h.astype(BF16), w_ref[...], NT_DIMS, preferred_element_type=F32)

    return _pallas(
        body, name=name, grid=(s // ts,),
        in_specs=[pl.BlockSpec((ts, D_MODEL), lambda i: (i, 0)), VECS, _full((D_IN, D_MODEL))],
        out_specs=[pl.BlockSpec((ts, D_IN), lambda i: (i, 0))], out_shape=[_sds((s, D_IN))],
        operands=(x, vecs, wt), vmem_mib=48, jobs=jobs)


def _diagonal():
    return lax.broadcasted_iota(jnp.int32, (BLOCK, BLOCK), 0) == lax.broadcasted_iota(jnp.int32, (BLOCK, BLOCK), 1)


def _column_as_row(wide, eye):
    return jnp.sum(jnp.where(eye, wide, 0.0), axis=0, keepdims=True)


def _mix_fwd(proj, bias, sink, qg2, kg2, ws, bsp, name, jobs=()):
    s = proj.shape[0]
    nt = s // TILE

    def body(sink_ref, cur_ref, pkv_ref, nkv_ref, *rest):
        bias_refs = rest[:SUB]
        qg_ref, kg_ref, ws_ref, bsp_ref, y_ref, p_ref, ps_ref, attn_ref, sv_ref = rest[SUB:]
        lo = _lo_mask((BLOCK, BLOCK))
        eye = _diagonal()
        lo_kv = _lo_mask((TILE + 2 * BLOCK, BLOCK))
        k_all, v_all = _kv_rows(cur_ref, pkv_ref, nkv_ref)
        kn_all = ((k_all * _half_rms_scale(k_all, lo_kv)) * kg_ref[...]).astype(BF16)
        vb_all = v_all.astype(BF16)
        for b in range(SUB):
            rows = slice(BLOCK * b, BLOCK * (b + 1))
            window = slice(BLOCK * b, BLOCK * (b + 3))
            qn = []
            for p in range(4):
                q = cur_ref[rows, C_Q + BLOCK * p:C_Q + BLOCK * (p + 1)]
                qn.append(((q * _half_rms_scale(q, lo)) * qg_ref[...]) * 0.125)
            q_stack = _stack_heads(qn, lo).astype(BF16)
            prob, psink = _attention_probs(q_stack, kn_all[window], bias_refs[b], sink_ref)
            pb = prob.astype(BF16)
            p_ref[STACK * b:STACK * (b + 1), :] = pb
            ps_ref[N_HEADS * b:N_HEADS * (b + 1), :] = jnp.concatenate(
                [_column_as_row(psink[BLOCK * h:BLOCK * (h + 1)], eye) for h in range(N_HEADS)], axis=0)
            o_stack = jnp.dot(pb, vb_all[window], preferred_element_type=F32)
            for p in range(4):
                g = cur_ref[rows, C_GA + BLOCK * p:C_GA + BLOCK * (p + 1)]
                attn = _unstack_pair(o_stack, p, lo)
                attn_ref[rows, BLOCK * p:BLOCK * (p + 1)] = attn.astype(BF16)
                y_ref[rows, BLOCK * p:BLOCK * (p + 1)] = (attn * (g * _sigmoid(g))).astype(BF16)

        for p in range(4):
            cols = slice(C_VG + BLOCK * p, C_VG + BLOCK * (p + 1))
            vn = []
            for b in range(SUB):
                vg = cur_ref[BLOCK * b:BLOCK * (b + 1), cols]
                vn.append((vg * _half_rms_scale(vg, lo)).astype(BF16))
            vn = jnp.concatenate(vn, axis=1)
            sv_a = jnp.dot(ws_ref[2 * p], vn, preferred_element_type=F32)
            sv_b = jnp.dot(ws_ref[2 * p + 1], vn, preferred_element_type=F32)
            for b in range(SUB):
                rows = slice(BLOCK * b, BLOCK * (b + 1))
                lanes = slice(BLOCK * b, BLOCK * (b + 1))
                sv = jnp.where(lo, sv_a[:, lanes], sv_b[:, lanes]) + bsp_ref[p]
                sv_ref[rows, BLOCK * p:BLOCK * (p + 1)] = sv.astype(BF16)
                u = cur_ref[rows, C_U + BLOCK * p:C_U + BLOCK * (p + 1)]
                g = cur_ref[rows, C_GG + BLOCK * p:C_GG + BLOCK * (p + 1)]
                y_ref[rows, D_ATTN + BLOCK * p:D_ATTN + BLOCK * (p + 1)] = ((u * sv) * (g * _sigmoid(g))).astype(BF16)

    cur, pkv, nkv, bias_specs = _mix_specs(nt)
    nb = nt * SUB
    half = pl.BlockSpec((TILE, D_ATTN), lambda i: (i, 0))
    return _pallas(
        body, name=name, grid=(nt,),
        in_specs=[pl.BlockSpec(memory_space=pltpu.SMEM), cur, pkv, nkv, *bias_specs, _full((1, BLOCK)), _full((1, BLOCK)),
                  _full((8, BLOCK, BLOCK)), _full((4, BLOCK, BLOCK))],
        out_specs=[pl.BlockSpec((TILE, D_MODEL), lambda i: (i, 0)), pl.BlockSpec((SUB * STACK, 3 * BLOCK), lambda i: (i, 0)),
                   pl.BlockSpec((SUB * N_HEADS, BLOCK), lambda i: (i, 0)), half, half],
        out_shape=[_sds((s, D_MODEL), BF16), _sds((nb * STACK, 3 * BLOCK), BF16), _sds((nb * N_HEADS, BLOCK)),
                   _sds((s, D_ATTN), BF16), _sds((s, D_ATTN), BF16)],
        operands=(sink, proj, proj, proj, *([bias] * SUB), qg2, kg2, ws, bsp), vmem_mib=56, jobs=jobs)


def _out_proj_fwd(y, x, vecs, w_out, vecs_next, wt, name):
    s = x.shape[0]
    ts = min(512, s)

    def body(y_ref, x_ref, v_ref, w_ref, vn_ref, wt_ref, xn_ref, p_ref):
        xv = x_ref[...] + _row(v_ref, V_GATE) * jnp.dot(y_ref[...], w_ref[...], preferred_element_type=F32)
        xn_ref[...] = xv
        p_ref[...] = lax.dot_general(_adaln(xv, vn_ref).astype(BF16), wt_ref[...], NT_DIMS, preferred_element_type=F32)

    row = pl.BlockSpec((ts, D_MODEL), lambda i: (i, 0))
    return _pallas(
        body, name=name, grid=(s // ts,),
        in_specs=[row, row, VECS, _full((D_MODEL, D_MODEL)), VECS, _full((D_IN, D_MODEL))],
        out_specs=[row, pl.BlockSpec((ts, D_IN), lambda i: (i, 0))], out_shape=[_sds((s, D_MODEL)), _sds((s, D_IN))],
        operands=(y, x, vecs, w_out, vecs_next, wt), vmem_mib=56)


def _out_loss_bwd(y, x, vecs, w_out, target, name):
    s = x.shape[0]
    ts = min(512, s)
    steps = s // ts

    def body(y_ref, x_ref, v_ref, w_ref, t_ref, dx_ref, sq_ref, dy_ref, gw_ref, dg_ref):
        @pl.when(pl.program_id(0) == 0)
        def _():
            sq_ref[...] = jnp.zeros_like(sq_ref)
            gw_ref[...] = jnp.zeros_like(gw_ref)

        gate = _row(v_ref, V_GATE)
        yv = y_ref[...]
        out = x_ref[...] + gate * jnp.dot(yv, w_ref[...], preferred_element_type=F32)
        diff = out - t_ref[...]
        dx = diff * (1.0 / D_MODEL)
        dx_ref[...] = dx
        per_token = jnp.sum(diff * diff, axis=-1, keepdims=True) * (1.0 / D_MODEL)
        sq_ref[...] += jnp.sum(per_token, axis=0, keepdims=True)
        dy_ref[...] = lax.dot_general((dx * gate).astype(BF16), w_ref[...], NT_DIMS, preferred_element_type=F32)
        gw_ref[...] += lax.dot_general(yv, dx.astype(BF16), TN_DIMS, preferred_element_type=F32)

        @pl.when(pl.program_id(0) == steps - 1)
        def _():
            m = gw_ref[...]
            dg_ref[...] = jnp.sum(w_ref[...].astype(F32) * m, axis=0, keepdims=True)
            gw_ref[...] = m * _row(v_ref, V_GATE)

    row = pl.BlockSpec((ts, D_MODEL), lambda i: (i, 0))
    return _pallas(
        body, name=name, grid=(s // ts,), in_specs=[row, row, VECS, _full((D_MODEL, D_MODEL)), row],
        out_specs=[row, _full((1, 1)), row, _full((D_MODEL, D_MODEL)), _full((1, D_MODEL))],
        out_shape=[_sds((s, D_MODEL)), _sds((1, 1)), _sds((s, D_MODEL)), _sds((D_MODEL, D_MODEL)), _sds((1, D_MODEL))],
        operands=(y, x, vecs, w_out, target), vmem_mib=48)


def _out_bwd(dxo, y, vecs, w_out, name, jobs=()):
    s = dxo.shape[0]
    ts = min(512, s)
    steps = s // ts

    def body(dx_ref, y_ref, v_ref, w_ref, dy_ref, gw_ref, dg_ref):
        @pl.when(pl.program_id(0) == 0)
        def _():
            gw_ref[...] = jnp.zeros_like(gw_ref)

        dx = dx_ref[...]
        dy_ref[...] = lax.dot_general((dx * _row(v_ref, V_GATE)).astype(BF16), w_ref[...], NT_DIMS,
                                      preferred_element_type=F32)
        gw_ref[...] += lax.dot_general(y_ref[...], dx.astype(BF16), TN_DIMS, preferred_element_type=F32)

        @pl.when(pl.program_id(0) == steps - 1)
        def _():
            m = gw_ref[...]
            dg_ref[...] = jnp.sum(w_ref[...].astype(F32) * m, axis=0, keepdims=True)
            gw_ref[...] = m * _row(v_ref, V_GATE)

    row = pl.BlockSpec((ts, D_MODEL), lambda i: (i, 0))
    return _pallas(
        body, name=name, grid=(s // ts,), in_specs=[row, row, VECS, _full((D_MODEL, D_MODEL))],
        out_specs=[row, _full((D_MODEL, D_MODEL)), _full((1, D_MODEL))],
        out_shape=[_sds((s, D_MODEL)), _sds((D_MODEL, D_MODEL)), _sds((1, D_MODEL))],
        operands=(dxo, y, vecs, w_out), vmem_mib=48, jobs=jobs)


def _mix_bwd(dy, proj, probs, psink, attn, sv, qg2, kg2, wst, name, jobs=()):
    s = proj.shape[0]
    nt = s // TILE

    def body(dy_ref, cur_ref, pkv_ref, nkv_ref, p_ref, ps_ref, attn_ref, sv_ref, qg_ref, kg_ref, wst_ref,
             dpb_ref, dkv_ref, p0_ref, p2_ref, dqg_ref, dkg_ref, dsink_ref, dws_ref, dbsp_ref):
        def put(rows, col, value):
            dpb_ref[rows, col:col + BLOCK] = value.astype(BF16)

        @pl.when(pl.program_id(0) == 0)
        def _():
            dqg_ref[...] = jnp.zeros_like(dqg_ref)
            dkg_ref[...] = jnp.zeros_like(dkg_ref)
            dsink_ref[...] = jnp.zeros_like(dsink_ref)
            dws_ref[...] = jnp.zeros_like(dws_ref)
            dbsp_ref[...] = jnp.zeros_like(dbsp_ref)

        lo = _lo_mask((BLOCK, BLOCK))
        lo_kv = _lo_mask((TILE + 2 * BLOCK, BLOCK))
        eye = _diagonal()
        lane_row = lax.broadcasted_iota(jnp.int32, (1, BLOCK), 1)
        qg = qg_ref[...]
        kg = kg_ref[...]

        k_all, v_all = _kv_rows(cur_ref, pkv_ref, nkv_ref)
        rk = _half_rms_scale(k_all, lo_kv)
        khat = k_all * rk
        kn_all = (khat * kg).astype(BF16)
        vb_all = v_all.astype(BF16)

        dkn_parts, dv_parts = [], []
        dsink = jnp.zeros((1, BLOCK), F32)
        dqg = jnp.zeros((1, BLOCK), F32)
        for b in range(SUB):
            rows = slice(BLOCK * b, BLOCK * (b + 1))
            window = slice(BLOCK * b, BLOCK * (b + 3))
            kn, vb = kn_all[window], vb_all[window]

            qhat, rq = [], []
            for p in range(4):
                q = cur_ref[rows, C_Q + BLOCK * p:C_Q + BLOCK * (p + 1)]
                r = _half_rms_scale(q, lo)
                rq.append(r)
                qhat.append(q * r)
            q_stack = _stack_heads([(qh * qg) * 0.125 for qh in qhat], lo).astype(BF16)
            pb = p_ref[STACK * b:STACK * (b + 1), :]
            prob = pb.astype(F32)

            dout = []
            for p in range(4):
                g = cur_ref[rows, C_GA + BLOCK * p:C_GA + BLOCK * (p + 1)]
                sg = _sigmoid(g)
                dya = dy_ref[rows, BLOCK * p:BLOCK * (p + 1)]
                attn = attn_ref[rows, BLOCK * p:BLOCK * (p + 1)]
                put(rows, C_GA + BLOCK * p, dya * attn * (sg * (1.0 + g * (1.0 - sg))))
                dout.append(dya * (g * sg))
            do_stack = _stack_heads(dout, lo).astype(BF16)
            dp = lax.dot_general(do_stack, vb, NT_DIMS, preferred_element_type=F32)
            delta = jnp.sum(prob * dp, axis=-1, keepdims=True)
            dsb = (prob * (dp - delta)).astype(BF16)

            for h in range(N_HEADS):
                delta_row = _column_as_row(jnp.broadcast_to(delta[BLOCK * h:BLOCK * (h + 1)], (BLOCK, BLOCK)), eye)
                tot = jnp.sum(ps_ref[N_HEADS * b + h:N_HEADS * b + h + 1, :] * delta_row, axis=-1, keepdims=True)
                dsink = dsink - jnp.where(lane_row == h, tot, 0.0)

            dq_stack = jnp.dot(dsb, kn, preferred_element_type=F32) * 0.125
            dkn_parts.append(lax.dot_general(dsb, q_stack, TN_DIMS, preferred_element_type=F32))
            dv_parts.append(lax.dot_general(pb, do_stack, TN_DIMS, preferred_element_type=F32))

            for p in range(4):
                dqn = _unstack_pair(dq_stack, p, lo)
                qh = qhat[p]
                dqg = dqg + jnp.sum(dqn * qh, axis=0, keepdims=True)
                dqh = dqn * qg
                mean = _half_sum(dqh * qh, lo) * (1.0 / HEAD_DIM)
                put(rows, C_Q + BLOCK * p, rq[p] * (dqh - qh * mean))

        dws_new, dbs_new = [], []
        for p in range(4):
            rs, vnfs, vns, dsvs, dbs = [], [], [], [], None
            for b in range(SUB):
                rows = slice(BLOCK * b, BLOCK * (b + 1))
                vg = cur_ref[rows, C_VG + BLOCK * p:C_VG + BLOCK * (p + 1)]
                r = _half_rms_scale(vg, lo)
                vnf = vg * r
                sv = sv_ref[rows, BLOCK * p:BLOCK * (p + 1)]
                u = cur_ref[rows, C_U + BLOCK * p:C_U + BLOCK * (p + 1)]
                g = cur_ref[rows, C_GG + BLOCK * p:C_GG + BLOCK * (p + 1)]
                sg = _sigmoid(g)
                dym = dy_ref[rows, D_ATTN + BLOCK * p:D_ATTN + BLOCK * (p + 1)]
                put(rows, C_GG + BLOCK * p, dym * (u * sv) * (sg * (1.0 + g * (1.0 - sg))))
                dgm = dym * (g * sg)
                put(rows, C_U + BLOCK * p, dgm * sv)
                dsv = dgm * u
                term = jnp.where(lo, jnp.sum(jnp.where(lo, dsv, 0.0), axis=-1, keepdims=True),
                                 jnp.sum(jnp.where(lo, 0.0, dsv), axis=-1, keepdims=True))
                dbs = term if dbs is None else dbs + term
                rs.append(r)
                vnfs.append(vnf)
                vns.append(vnf.astype(BF16))
                dsvs.append(dsv)
            dbs_new.append(dbs)
            vn = jnp.concatenate(vns, axis=1)
            dsv = jnp.concatenate(dsvs, axis=1)
            lo_t = (lax.broadcasted_iota(jnp.int32, dsv.shape, 1) & (BLOCK - 1)) < HEAD_DIM
            dws_new.append(lax.dot_general(jnp.where(lo_t, dsv, 0.0).astype(BF16), vn, NT_DIMS, preferred_element_type=F32))
            dws_new.append(lax.dot_general(jnp.where(lo_t, 0.0, dsv).astype(BF16), vn, NT_DIMS, preferred_element_type=F32))
            dsvb = dsv.astype(BF16)
            dvn_a = jnp.dot(wst_ref[2 * p], dsvb, preferred_element_type=F32)
            dvn_b = jnp.dot(wst_ref[2 * p + 1], dsvb, preferred_element_type=F32)
            for b in range(SUB):
                lanes = slice(BLOCK * b, BLOCK * (b + 1))
                dvn = jnp.where(lo, dvn_a[:, lanes], dvn_b[:, lanes])
                mean = _half_sum(dvn * vnfs[b], lo) * (1.0 / HEAD_DIM)
                put(slice(BLOCK * b, BLOCK * (b + 1)), C_VG + BLOCK * p, rs[b] * (dvn - vnfs[b] * mean))

        dsink_ref[...] += dsink
        dqg = jnp.broadcast_to(dqg, (8, BLOCK))
        dqg_ref[...] += dqg + pltpu.roll(dqg, HEAD_DIM, 1)

        dkn = _overlap_add(dkn_parts)
        dv = _overlap_add(dv_parts)
        dkg = jnp.broadcast_to(jnp.sum(dkn * khat, axis=0, keepdims=True), (8, BLOCK))
        dkg_ref[...] += dkg + pltpu.roll(dkg, HEAD_DIM, 1)
        dkh = dkn * kg
        dk = rk * (dkh - khat * (_half_sum(dkh * khat, lo_kv) * (1.0 / HEAD_DIM)))
        dpb_ref[:, C_K:C_GA] = jnp.zeros((TILE, 2 * D_KV), BF16)
        dkv_ref[:, 0:D_KV] = dk[BLOCK:BLOCK + TILE]
        dkv_ref[:, D_KV:2 * D_KV] = dv[BLOCK:BLOCK + TILE]
        p0_ref[:, 0:D_KV] = dk[0:BLOCK]
        p0_ref[:, D_KV:2 * D_KV] = dv[0:BLOCK]
        p2_ref[:, 0:D_KV] = dk[BLOCK + TILE:]
        p2_ref[:, D_KV:2 * D_KV] = dv[BLOCK + TILE:]
        for g, new in enumerate(dws_new):
            dws_ref[g] += new
        for p, new in enumerate(dbs_new):
            dbsp_ref[p] += new

    cur, pkv, nkv, _ = _mix_specs(nt)
    kv_blk = (BLOCK, 2 * D_KV)
    half = pl.BlockSpec((TILE, D_ATTN), lambda i: (i, 0))
    return _pallas(
        body, name=name, grid=(nt,),
        in_specs=[pl.BlockSpec((TILE, D_MODEL), lambda i: (i, 0)), cur, pkv, nkv,
                  pl.BlockSpec((SUB * STACK, 3 * BLOCK), lambda i: (i, 0)), pl.BlockSpec((SUB * N_HEADS, BLOCK), lambda i: (i, 0)),
                  half, half, _full((1, BLOCK)), _full((1, BLOCK)), _full((8, BLOCK, BLOCK))],
        out_specs=[cur, pl.BlockSpec((TILE, 2 * D_KV), lambda i: (i, 0)),
                   pl.BlockSpec(kv_blk, lambda i: ((i + nt - 1) % nt, 0)),
                   pl.BlockSpec(kv_blk, lambda i: ((i + 1) % nt, 0)),
                   _full((8, BLOCK)), _full((8, BLOCK)), _full((1, BLOCK)),
                   _full((8, BLOCK, BLOCK)), _full((4, BLOCK, BLOCK))],
        out_shape=[_sds((s, D_IN), BF16), _sds((s, 2 * D_KV)), _sds((nt * BLOCK, 2 * D_KV)), _sds((nt * BLOCK, 2 * D_KV)),
                   _sds((8, BLOCK)), _sds((8, BLOCK)), _sds((1, BLOCK)),
                   _sds((8, BLOCK, BLOCK)), _sds((4, BLOCK, BLOCK))],
        operands=(dy, proj, proj, proj, probs, psink, attn, sv, qg2, kg2, wst), vmem_mib=56, jobs=jobs)


def _w_in_grad(dpb, dkv, p0, p2, x, vecs, name, jobs=()):
    s = x.shape[0]
    ts = min(2 * TILE, s)
    tiles = ts // TILE

    def body(dpb_ref, dkv_ref, p0_ref, p2_ref, x_ref, v_ref, gw_ref, dkvb_ref):
        @pl.when(pl.program_id(0) == 0)
        def _():
            gw_ref[...] = jnp.zeros_like(gw_ref)

        h = _adaln(x_ref[...], v_ref).astype(BF16)
        for t in range(tiles):
            halo = slice(BLOCK * t, BLOCK * (t + 1))
            first = slice(TILE * t, TILE * t + BLOCK)
            last = slice(TILE * (t + 1) - BLOCK, TILE * (t + 1))
            dkvb_ref[first, :] = (dkv_ref[first, :] + p2_ref[halo, :]).astype(BF16)
            if SUB > 2:
                inner = slice(TILE * t + BLOCK, TILE * (t + 1) - BLOCK)
                dkvb_ref[inner, :] = dkv_ref[inner, :].astype(BF16)
            dkvb_ref[last, :] = (dkv_ref[last, :] + p0_ref[halo, :]).astype(BF16)
        gw_ref[...] += lax.dot_general(dpb_ref[...], h, TN_DIMS, preferred_element_type=F32)
        gw_ref[C_K:C_GA, :] += lax.dot_general(dkvb_ref[...], h, TN_DIMS, preferred_element_type=F32)

    kv = pl.BlockSpec((ts, 2 * D_KV), lambda i: (i, 0))
    halo = pl.BlockSpec((tiles * BLOCK, 2 * D_KV), lambda i: (i, 0))
    return _pallas(
        body, name=name, grid=(s // ts,),
        in_specs=[pl.BlockSpec((ts, D_IN), lambda i: (i, 0)), kv, halo, halo,
                  pl.BlockSpec((ts, D_MODEL), lambda i: (i, 0)), VECS],
        out_specs=[_full((D_IN, D_MODEL)), kv], out_shape=[_sds((D_IN, D_MODEL)), _sds((s, 2 * D_KV), BF16)],
        operands=(dpb, dkv, p0, p2, x, vecs), vmem_mib=56, jobs=jobs)


def _proj_bwd(dpb, dkvb, x, dxo, vecs, wt, name, jobs=()):
    s = x.shape[0]
    ts = min(512, s)

    def body(dpb_ref, dkvb_ref, x_ref, dxo_ref, v_ref, w_ref, dxi_ref, dsh_ref, dsc_ref, dng_ref):
        @pl.when(pl.program_id(0) == 0)
        def _():
            dsh_ref[...] = jnp.zeros_like(dsh_ref)
            dsc_ref[...] = jnp.zeros_like(dsc_ref)
            dng_ref[...] = jnp.zeros_like(dng_ref)

        dh = (jnp.dot(dpb_ref[...], w_ref[...], preferred_element_type=F32)
              + jnp.dot(dkvb_ref[...], w_ref[C_K:C_GA, :], preferred_element_type=F32))

        xv = x_ref[...]
        r = lax.rsqrt(jnp.mean(xv * xv, axis=-1, keepdims=True) + EPS)
        xn = xv * r
        ngv = _row(v_ref, V_NG)
        sc1 = 1.0 + _row(v_ref, V_SCALE)
        dsh_ref[...] += jnp.sum(dh, axis=0, keepdims=True)
        dsc_ref[...] += jnp.sum(dh * (xn * ngv), axis=0, keepdims=True)
        dh1 = dh * sc1
        dng_ref[...] += jnp.sum(dh1 * xn, axis=0, keepdims=True)
        dxn = dh1 * ngv
        dxi_ref[...] = r * (dxn - xn * jnp.mean(dxn * xn, axis=-1, keepdims=True)) + dxo_ref[...]

    row = pl.BlockSpec((ts, D_MODEL), lambda i: (i, 0))
    vec = _full((1, D_MODEL))
    return _pallas(
        body, name=name, grid=(s // ts,),
        in_specs=[pl.BlockSpec((ts, D_IN), lambda i: (i, 0)), pl.BlockSpec((ts, 2 * D_KV), lambda i: (i, 0)),
                  row, row, VECS, _full((D_IN, D_MODEL))],
        out_specs=[row, vec, vec, vec],
        out_shape=[_sds((s, D_MODEL)), _sds((1, D_MODEL)), _sds((1, D_MODEL)), _sds((1, D_MODEL))],
        operands=(dpb, dkvb, x, dxo, vecs, wt), vmem_mib=48, jobs=jobs)


def _pair_sum(g, r, c_idx, name, send_dtype=None):
    _, rows, cols = g.shape
    half = rows // 2

    def body(c_ref, g_ref, r_ref, o_ref, *narrow):
        total = g_ref[...] + r_ref[...]
        o_ref[...] = total
        for n_ref in narrow:
            n_ref[...] = total.astype(n_ref.dtype)

    blk = (None, half, cols)
    out_blk = pl.BlockSpec(blk, lambda j, c: (j, 0, 0))
    shapes = [_sds((N_CHIPS, half, cols))] + ([_sds((N_CHIPS, half, cols), send_dtype)] if send_dtype else [])
    return pl.pallas_call(
        body, name=name,
        grid_spec=pltpu.PrefetchScalarGridSpec(
            num_scalar_prefetch=1, grid=(N_CHIPS,),
            in_specs=[pl.BlockSpec(blk, lambda j, c: (j, c[0], 0)), out_blk], out_specs=[out_blk] * len(shapes)),
        out_shape=shapes,
        compiler_params=pltpu.CompilerParams(dimension_semantics=("arbitrary",), vmem_limit_bytes=32 * MIB),
    )(*_in_hbm(c_idx, g, r))


def _chip_sum(p, r, place, name):
    _, rows, cols = p.shape
    tr = rows // 2

    def body(j_ref, p_ref, r_ref, o_ref):
        o_ref[...] = ((p_ref[...] + r_ref[0].astype(F32)) + r_ref[1].astype(F32)) + r_ref[2].astype(F32)

    return pl.pallas_call(
        body, name=name,
        grid_spec=pltpu.PrefetchScalarGridSpec(
            num_scalar_prefetch=1, grid=(2,),
            in_specs=[pl.BlockSpec((None, tr, cols), lambda t, j: (j[0], t, 0)),
                      pl.BlockSpec((3, tr, cols), lambda t, j: (0, t, 0))],
            out_specs=pl.BlockSpec((tr, cols), lambda t, j: (2 * j[1] + t, 0))),
        out_shape=_sds((2 * rows, cols)),
        compiler_params=pltpu.CompilerParams(dimension_semantics=("arbitrary",), vmem_limit_bytes=32 * MIB),
    )(*_in_hbm(place, p, r))


def _cast_permute_w_in(wt, place_chunks):
    def body(t_ref, w_ref, w0_ref, w1_ref):
        def cast_into(o_ref):
            for t in range(N_CHUNKS):
                src = pl.multiple_of(t_ref[1 + t] * CHUNK_ROWS, CHUNK_ROWS)
                o_ref[CHUNK_ROWS * t:CHUNK_ROWS * (t + 1), :] = w_ref[pl.ds(src, CHUNK_ROWS), :].astype(BF16)

        @pl.when(pl.program_id(0) == 0)
        def _():
            cast_into(w0_ref)

        @pl.when(pl.program_id(0) == 1)
        def _():
            cast_into(w1_ref)

    return pl.pallas_call(
        body, name="cast_permute_w_in",
        grid_spec=pltpu.PrefetchScalarGridSpec(
            num_scalar_prefetch=1, grid=(DEPTH,),
            in_specs=[pl.BlockSpec((None, W_IN_BLK, D_MODEL), lambda l, tbl: (l, 0, 0))],
            out_specs=[pl.BlockSpec((None, W_IN_BLK, D_MODEL), lambda l, tbl: (tbl[0], 0, 0)),
                       pl.BlockSpec((W_IN_BLK, D_MODEL), lambda l, tbl: (0, 0))]),
        out_shape=[_sds((N_CHIPS, W_IN_BLK, D_MODEL), BF16), _sds((W_IN_BLK, D_MODEL), BF16)],
        compiler_params=pltpu.CompilerParams(dimension_semantics=("arbitrary",), vmem_limit_bytes=32 * MIB),
    )(*_in_hbm(place_chunks, wt))


def _gather_inputs(c, w_out, w0):
    half = W_IN_BLK // 2

    def body(c_ref, wout_ref, mine_ref, call_ref, woutb_ref, w0_ref, send_sems, recv_sems):
        x, y, cc = _coords()
        j = 2 * x + y
        b = 2 * j + cc
        sib = (x, y, 1 - cc)
        woutb_ref[...] = wout_ref[...].astype(BF16)
        call_ref[b] = c_ref[...]
        chips = _other_chips(x, y)

        def sems(k):
            return send_sems.at[k], recv_sems.at[k]

        def half_rows(chip_index):
            return w0_ref.at[chip_index, pl.ds(cc * half, half), :]

        first = [_remote(mine_ref.at[j, pl.ds(cc * half, half), :], half_rows(j), sems(k), (*chip, cc))
                 for k, chip in enumerate(chips)]
        k = 3
        rest = []
        for fx in (0, 1):
            for fy in (0, 1):
                for fc in (0, 1):
                    if fx or fy or fc:
                        dev = (1 - x if fx else x, 1 - y if fy else y, 1 - cc if fc else cc)
                        rest.append(_remote(call_ref.at[b], call_ref.at[b], sems(k), dev))
                        k += 1
        for cp in first + rest:
            cp.start()
        passed = []
        for k, chip in enumerate(chips):
            jk = 2 * chip[0] + chip[1]
            first[k].wait_recv()
            passed.append(_remote(half_rows(jk), half_rows(jk), sems(10 + k), sib))
            passed[k].start()
        for cp in first:
            cp.wait_send()
        for cp in rest + passed:
            cp.wait()

    return pl.pallas_call(
        body, name="gather_inputs", in_specs=[VMEM, VMEM, ANY], out_specs=[VMEM, VMEM, ANY],
        out_shape=[_sds((N_DEV, 1, D_MODEL)), _sds((DEPTH, W_OUT_BLK, D_MODEL), BF16),
                   _sds((N_CHIPS, W_IN_BLK, D_MODEL), BF16)],
        scratch_shapes=[pltpu.SemaphoreType.DMA((13,)), pltpu.SemaphoreType.DMA((13,))],
        input_output_aliases={2: 2},
        compiler_params=pltpu.CompilerParams(vmem_limit_bytes=32 * MIB),
    )(c, w_out, w0)


def _ada_rows(c_all, w_ada, b_blk):
    def body(c_ref, w_ref, b_ref, o_ref, cond_ref):
        cv = c_ref[...]
        cond = (cv * _sigmoid(cv)).astype(BF16)
        cond_ref[...] = cond.astype(F32)
        for l in range(DEPTH):
            o_ref[:, l, :] = jnp.dot(cond, w_ref[l].astype(BF16), preferred_element_type=F32) + b_ref[l:l + 1, :]

    return pl.pallas_call(
        body, name="ada_rows", in_specs=[VMEM, VMEM, VMEM], out_specs=[VMEM, VMEM],
        out_shape=[_sds((N_DEV, DEPTH, W_ADA_BLK)), _sds((N_DEV, D_MODEL))],
        compiler_params=pltpu.CompilerParams(vmem_limit_bytes=32 * MIB),
    )(c_all, w_ada, b_blk)


def _exchange_ada(part):
    def body(part_ref, out_ref, send_sems, recv_sems):
        x, y, cc = _coords()
        j = 2 * x + y
        out_ref[j] = part_ref[2 * j + cc]
        copies = []
        for k, chip in enumerate(_other_chips(x, y)):
            b_dst = 4 * chip[0] + 2 * chip[1] + cc
            copies.append(_remote(part_ref.at[b_dst], out_ref.at[j], (send_sems.at[k], recv_sems.at[k]), (*chip, cc)))
        for cp in copies:
            cp.start()
        for cp in copies:
            cp.wait()

    return pl.pallas_call(
        body, name="exchange_ada", in_specs=[VMEM], out_specs=VMEM,
        out_shape=_sds((N_CHIPS, DEPTH, W_ADA_BLK)),
        scratch_shapes=[pltpu.SemaphoreType.DMA((3,)), pltpu.SemaphoreType.DMA((3,))],
    )(part)


def _adamw_math(w, g, m, v):
    m = ADAM_B1 * m + (1.0 - ADAM_B1) * g
    v = ADAM_B2 * v + (1.0 - ADAM_B2) * (g * g)
    m_hat = m / (1.0 - ADAM_B1 ** ADAM_STEP)
    v_hat = v / (1.0 - ADAM_B2 ** ADAM_STEP)
    delta = -ADAM_LR * (m_hat / (jnp.sqrt(v_hat) + ADAM_EPS) + ADAM_WD * w)
    return delta, m, v


def _adamw_w_in(w, g0, g1, m, v, pos_chunks):
    def body(t_ref, w_ref, g0_ref, g1_ref, m_ref, v_ref, g_ref, d_ref, nm_ref, nv_ref):
        for l, src in enumerate((g0_ref, g1_ref)):
            g = src[...]
            g_ref[l] = g
            d_ref[l], nm_ref[l], nv_ref[l] = _adamw_math(w_ref[l], g, m_ref[l], v_ref[l])

    nat = pl.BlockSpec((DEPTH, CHUNK_ROWS, D_MODEL), lambda t, tbl: (0, t, 0))
    per = pl.BlockSpec((CHUNK_ROWS, D_MODEL), lambda t, tbl: (tbl[t], 0))
    return pl.pallas_call(
        body, name="adamw_w_in",
        grid_spec=pltpu.PrefetchScalarGridSpec(num_scalar_prefetch=1, grid=(N_CHUNKS,),
                                               in_specs=[nat, per, per, nat, nat], out_specs=[nat] * 4),
        out_shape=[_sds(w.shape)] * 4,
        compiler_params=pltpu.CompilerParams(dimension_semantics=("arbitrary",)),
    )(*_in_hbm(pos_chunks, w, g0, g1, m, v))


def _adamw_w_out(w, g0, g1, m, v):
    def body(w_ref, g0_ref, g1_ref, m_ref, v_ref, g_ref, d_ref, nm_ref, nv_ref):
        g = jnp.where(pl.program_id(0) == 0, g0_ref[...], g1_ref[...])
        g_ref[...] = g
        d_ref[...], nm_ref[...], nv_ref[...] = _adamw_math(w_ref[...], g, m_ref[...], v_ref[...])

    blk = pl.BlockSpec((None, W_OUT_BLK, D_MODEL), lambda l: (l, 0, 0))
    gblk = _full((W_OUT_BLK, D_MODEL))
    return pl.pallas_call(
        body, name="adamw_w_out", grid=(DEPTH,), in_specs=[blk, gblk, gblk, blk, blk], out_specs=[blk] * 4,
        out_shape=[_sds(w.shape)] * 4,
        compiler_params=pltpu.CompilerParams(dimension_semantics=("arbitrary",), vmem_limit_bytes=32 * MIB),
    )(w, g0, g1, m, v)


def _w_ada_grad_adamw(cond_t, dada, w, m, v):
    _, rows, cols = w.shape
    tr = 256

    def body(ct_ref, da_ref, w_ref, m_ref, v_ref, g_ref, d_ref, nm_ref, nv_ref):
        g = jnp.dot(ct_ref[...], da_ref[...].astype(BF16), preferred_element_type=F32)
        g_ref[...] = g
        d_ref[...], nm_ref[...], nv_ref[...] = _adamw_math(w_ref[...], g, m_ref[...], v_ref[...])

    blk = pl.BlockSpec((None, tr, cols), lambda l, t: (l, t, 0))
    return pl.pallas_call(
        body, name="w_ada_grad_adamw", grid=(DEPTH, rows // tr),
        in_specs=[pl.BlockSpec((tr, BLOCK), lambda l, t: (t, 0)), pl.BlockSpec((None, BLOCK, cols), lambda l, t: (l, 0, 0)),
                  blk, blk, blk],
        out_specs=[blk] * 4, out_shape=[_sds(w.shape)] * 4,
        compiler_params=pltpu.CompilerParams(dimension_semantics=("arbitrary", "arbitrary"), vmem_limit_bytes=32 * MIB),
    )(cond_t, dada, w, m, v)


def _small_sum_adamw(gathered_a, gathered_b, ws, rest):
    n_ws = 8 * BLOCK

    def body(*refs):
        a_refs, b_ref = refs[:DEPTH], refs[DEPTH]
        ws_ref, ms_ref, vs_ref, wr_ref, mr_ref, vr_ref = refs[DEPTH + 1:DEPTH + 7]
        gs_ref, ds_ref, nms_ref, nvs_ref, gr_ref, dr_ref, nmr_ref, nvr_ref = refs[DEPTH + 7:]

        def total(ref):
            g = ref[0]
            for b in range(1, N_DEV):
                g = g + ref[b]
            return g

        totals = [total(ref) for ref in a_refs]
        for l, t in enumerate(totals):
            g = t[0:n_ws]
            gs_ref[l] = g
            ds_ref[l], nms_ref[l], nvs_ref[l] = _adamw_math(ws_ref[l], g, ms_ref[l], vs_ref[l])
        g = jnp.concatenate([t[n_ws:] for t in totals] + [total(b_ref)], axis=0)
        gr_ref[...] = g
        dr_ref[...], nmr_ref[...], nvr_ref[...] = _adamw_math(wr_ref[...], g, mr_ref[...], vr_ref[...])

    return pl.pallas_call(
        body, name="small_sum_adamw", in_specs=[VMEM] * (DEPTH + 7), out_specs=[VMEM] * 8,
        out_shape=[_sds(ws[0].shape)] * 4 + [_sds(rest[0].shape)] * 4,
        compiler_params=pltpu.CompilerParams(vmem_limit_bytes=48 * MIB),
    )(*gathered_a, gathered_b, *ws, *rest)


_SMALL_A_REST = (("b_s", 8), ("q_gain", 1), ("k_gain", 1), ("sink", 1))
_SMALL_A = (("w_s", 8 * BLOCK),) + _SMALL_A_REST
_SMALL_B = (("b_ada", DEPTH * 24), ("norm_gain", DEPTH * 8), ("sq_err", 1))


def _pack_rows(parts, layout, layer=None):
    rows = []
    for name, n in layout:
        flat = (parts[name] if layer is None else parts[name][layer]).reshape(-1)
        rows.append(jnp.pad(flat, (0, n * 128 - flat.shape[0])).reshape(n, 128))
    n_rows = sum(n for _, n in layout)
    if n_rows % 8:
        rows.append(jnp.zeros((-n_rows % 8, 128), F32))
    return jnp.concatenate(rows, axis=0)


def _pack_rest(parts):
    return jnp.concatenate([_pack_rows(parts, _SMALL_A_REST, l) for l in range(DEPTH)] + [_pack_rows(parts, _SMALL_B)], axis=0)


def _unpack_rest(packed, shapes):
    def take(r0, layout, shape_of):
        got = {}
        for name, n in layout:
            shape = shape_of(name)
            size = 1
            for d in shape:
                size *= d
            got[name] = packed[r0:r0 + n].reshape(-1)[:size].reshape(shape)
            r0 += n
        return got, r0 + -r0 % 8

    layers, r0 = [], 0
    for _ in range(DEPTH):
        got, r0 = take(r0, _SMALL_A_REST, lambda name: shapes[name][1:])
        layers.append(got)
    out, _ = take(r0, _SMALL_B, lambda name: shapes[name])
    out.update({name: jnp.stack([layer[name] for layer in layers]) for name, _ in _SMALL_A_REST})
    return out


def _permute_heads(a, axis):
    shp = a.shape
    a = a.reshape(shp[:axis] + (2, 4, HEAD_DIM) + shp[axis + 1:])
    a = jnp.swapaxes(a, axis, axis + 1)
    return a.reshape(shp)


def _unpermute_heads(a, axis):
    shp = a.shape
    a = a.reshape(shp[:axis] + (4, 2, HEAD_DIM) + shp[axis + 1:])
    a = jnp.swapaxes(a, axis, axis + 1)
    return a.reshape(shp)


def _permute_w_out(w):
    return jnp.concatenate([_permute_heads(w[:D_ATTN], 0), w[D_ATTN:]], axis=0)


def _unpermute_w_out(w):
    return jnp.concatenate([_unpermute_heads(w[:D_ATTN], 0), w[D_ATTN:]], axis=0)


def kernel(x, c, w_ada, b_ada, norm_gain, w_in, q_gain, k_gain, sink, w_s, b_s, w_out, loss_target, m_w_ada, m_b_ada, m_norm_gain, m_w_in, m_q_gain, m_k_gain, m_sink, m_w_s, m_b_s, m_w_out, v_w_ada, v_b_ada, v_norm_gain, v_w_in, v_q_gain, v_k_gain, v_sink, v_w_s, v_b_s, v_w_out):
    ix, iy, ic = _coords()
    chip = 2 * ix + iy
    chip_idx = jnp.stack([chip, ic]).astype(jnp.int32)
    core_idx = jnp.reshape(ic, (1,)).astype(jnp.int32)
    src_chunks = lax.dynamic_index_in_dim(jnp.asarray(_CHUNK_SRC), chip, 0, keepdims=False)
    pos_chunks = lax.dynamic_index_in_dim(jnp.asarray(_CHUNK_POS), chip, 0, keepdims=False)
    x0, target = x[0], loss_target[0]

    wt, mt, vt = (jnp.swapaxes(a, 1, 2) for a in (w_in, m_w_in, v_w_in))
    w0_mine, wloc_in1 = _cast_permute_w_in(wt, jnp.concatenate([chip_idx[:1], src_chunks]))
    c_all, wloc_out, w0 = _gather_inputs(c, w_out, w0_mine)
    wts = [w0.reshape(D_IN, D_MODEL), None]

    b_blk = lax.dynamic_slice_in_dim(b_ada, chip * W_ADA_BLK, W_ADA_BLK, axis=1)
    ada_part, cond = _ada_rows(c_all.reshape(N_DEV, D_MODEL), w_ada, b_blk)
    ada = jnp.moveaxis(_exchange_ada(ada_part), 0, 1).reshape(DEPTH, 3 * D_MODEL)
    vecs = [jnp.concatenate([norm_gain[l:l + 1], ada[l:l + 1, D_MODEL:2 * D_MODEL], ada[l:l + 1, 0:D_MODEL],
                             ada[l:l + 1, 2 * D_MODEL:], jnp.zeros((4, D_MODEL), F32)], axis=0) for l in range(DEPTH)]

    qg2 = jnp.concatenate([q_gain, q_gain], axis=-1)
    kg2 = jnp.concatenate([k_gain, k_gain], axis=-1)
    ws_b = w_s.astype(BF16)
    wst_b = jnp.swapaxes(w_s, -1, -2).astype(BF16)
    bsp = jnp.repeat(jnp.swapaxes(b_s.reshape(DEPTH, 4, 2, BLOCK), -1, -2), HEAD_DIM, axis=-1)
    bias = jnp.asarray(_bias_table())

    def mix_args(l):
        return bias, sink[l], qg2[l:l + 1], kg2[l:l + 1], ws_b[l]

    w_out_shape = (W_OUT_BLK, D_MODEL)
    proj0, wo0, wo1 = _proj_fwd(x0, vecs[0], wts[0], "proj_fwd_0",
                                jobs=[_job_gather([(wloc_out, 0), (wloc_out, 1)], [w_out_shape, w_out_shape])])
    y0, *kept0, w1 = _mix_fwd(proj0, *mix_args(0), bsp[0], "mix_fwd_0",
                              jobs=[_job_gather([(wloc_in1, None)], [(W_IN_BLK, D_MODEL)])])
    wts[1] = w1.reshape(D_IN, D_MODEL)
    wos = [_permute_w_out(w.reshape(D_MODEL, D_MODEL)) for w in (wo0, wo1)]
    x1, proj1 = _out_proj_fwd(y0, x0, vecs[0], wos[0], vecs[1], wts[1], "out_proj_fwd_01")
    y1, *kept1 = _mix_fwd(proj1, *mix_args(1), bsp[1], "mix_fwd_1")

    def blocks_out(gw):
        return _unpermute_w_out(gw).reshape(N_CHIPS, W_OUT_BLK, D_MODEL)

    def small_pack(dws, dbsp, dqg, dkg, dsink):
        b_s = jnp.swapaxes(dbsp[:, :, ::HEAD_DIM], -1, -2).reshape(8, BLOCK)
        return _pack_rows(dict(w_s=dws, b_s=b_s, q_gain=dqg[0, :HEAD_DIM], k_gain=dkg[0, :HEAD_DIM], sink=dsink[0, :N_HEADS]),
                          _SMALL_A)

    dx2, sq, dy1, gwo1, dgate1 = _out_loss_bwd(y1, x1, vecs[1], wos[1], target, "out_loss_bwd_1")
    go1 = blocks_out(gwo1)
    dpb, dkv, p0, p2, dqg1, dkg1, dsink1, dws1, dbsp1, ro1 = _mix_bwd(
        dy1, proj1, *kept1, qg2[1:2], kg2[1:2], wst_b[1], "mix_bwd_1", jobs=[_job_swap(go1)])
    po1, = _pair_sum(go1, ro1, core_idx, "pair_sum_w_out_1")
    gwi1, dkvb, co1, gathered_a1 = _w_in_grad(
        dpb, dkv, p0, p2, x1, vecs[1], "w_in_grad_1",
        jobs=[_job_scatter(po1), _job_all_gather(small_pack(dws1, dbsp1, dqg1, dkg1, dsink1))])
    gi1 = gwi1.reshape(N_CHIPS, W_IN_BLK, D_MODEL)
    fo1 = _chip_sum(po1, co1, chip_idx, "chip_sum_w_out_1")
    dx1, dsh1, dsc1, dng1, grad_wo1, ri1 = _proj_bwd(dpb, dkvb, x1, dx2, vecs[1], wts[1], "proj_bwd_1",
                                                     jobs=[_job_join(fo1), _job_swap(gi1)])
    pi1, = _pair_sum(gi1, ri1, core_idx, "pair_sum_w_in_1")

    dy0, gwo0, dgate0 = _out_bwd(dx1, y0, vecs[0], wos[0], "out_bwd_0")
    go0 = blocks_out(gwo0)
    dpb, dkv, p0, p2, dqg0, dkg0, dsink0, dws0, dbsp0, ci1, ro0 = _mix_bwd(
        dy0, proj0, *kept0, qg2[0:1], kg2[0:1], wst_b[0], "mix_bwd_0", jobs=[_job_scatter(pi1), _job_swap(go0)])
    fi1 = _chip_sum(pi1, ci1, chip_idx, "chip_sum_w_in_1")
    po0, = _pair_sum(go0, ro0, core_idx, "pair_sum_w_out_0")

    gwi0, dkvb, grad_wi1, co0, gathered_a0 = _w_in_grad(
        dpb, dkv, p0, p2, x0, vecs[0], "w_in_grad_0",
        jobs=[_job_join(fi1), _job_scatter(po0), _job_all_gather(small_pack(dws0, dbsp0, dqg0, dkg0, dsink0))])
    gi0 = gwi0.reshape(N_CHIPS, W_IN_BLK, D_MODEL)
    fo0 = _chip_sum(po0, co0, chip_idx, "chip_sum_w_out_0")

    ri0, grad_wo0 = _comm([_job_swap(gi0), _job_join(fo0)], "swap_w_in_0")
    pi0, pi0_send = _pair_sum(gi0, ri0, core_idx, "pair_sum_w_in_0", send_dtype=BF16)
    dx0, dsh0, dsc0, dng0, ci0 = _proj_bwd(dpb, dkvb, x0, dx1, vecs[0], wts[0], "proj_bwd_0",
                                           jobs=[_job_scatter(pi0_send)])
    fi0 = _chip_sum(pi0, ci0, chip_idx, "chip_sum_w_in_0")

    small_g = dict(
        b_ada=jnp.stack([jnp.concatenate([dsh0, dsc0, dgate0], axis=-1)[0], jnp.concatenate([dsh1, dsc1, dgate1], axis=-1)[0]]),
        norm_gain=jnp.stack([dng0[0], dng1[0]]), sq_err=sq[0])
    none = jnp.zeros((1,), F32)
    small_w = dict(w_s=w_s, b_s=b_s, b_ada=b_ada, norm_gain=norm_gain, q_gain=q_gain, k_gain=k_gain, sink=sink, sq_err=none)
    small_m = dict(w_s=m_w_s, b_s=m_b_s, b_ada=m_b_ada, norm_gain=m_norm_gain, q_gain=m_q_gain, k_gain=m_k_gain, sink=m_sink,
                   sq_err=none)
    small_v = dict(w_s=v_w_s, b_s=v_b_s, b_ada=v_b_ada, norm_gain=v_norm_gain, q_gain=v_q_gain, k_gain=v_k_gain, sink=v_sink,
                   sq_err=none)
    grad_wi0, gathered_b = _comm([_job_join(fi0), _job_all_gather(_pack_rows(small_g, _SMALL_B))], "join_w_in_0")
    packed = _small_sum_adamw(
        [a.reshape(N_DEV, -1, 128) for a in (gathered_a0, gathered_a1)], gathered_b.reshape(N_DEV, -1, 128),
        [a.reshape(DEPTH, 8 * BLOCK, BLOCK) for a in (w_s, m_w_s, v_w_s)], [_pack_rest(p) for p in (small_w, small_m, small_v)])
    shapes = {k: a.shape for k, a in small_w.items()}
    sg, sd, sm, sv = (dict(_unpack_rest(rest, shapes), w_s=ws.reshape(w_s.shape)) for ws, rest in zip(packed[:4], packed[4:]))
    loss = 0.5 * sg["sq_err"][0]

    dada_all = gathered_b.reshape(N_DEV, -1, 128)[:, 0:DEPTH * 24].reshape(N_DEV, DEPTH, 3 * D_MODEL)
    dada_blk = jnp.moveaxis(lax.dynamic_slice_in_dim(dada_all, chip * W_ADA_BLK, W_ADA_BLK, axis=2), 0, 1)
    pad = BLOCK - N_DEV
    ada_out = _w_ada_grad_adamw(
        jnp.pad(cond.T, ((0, 0), (0, pad))).astype(BF16), jnp.pad(dada_blk, ((0, 0), (0, pad), (0, 0))),
        w_ada, m_w_ada, v_w_ada)

    in_out = [jnp.swapaxes(a, 1, 2) for a in _adamw_w_in(wt, grad_wi0, grad_wi1, mt, vt, pos_chunks)]
    out_out = _adamw_w_out(w_out, grad_wo0, grad_wo1, m_w_out, v_w_out)

    def ordered(k):
        small = (sg, sd, sm, sv)[k]
        return (ada_out[k], small["b_ada"], small["norm_gain"], in_out[k], small["q_gain"], small["k_gain"], small["sink"],
                small["w_s"], small["b_s"], out_out[k])

    return (loss, dx0[None], *ordered(0), *ordered(1), *ordered(2), *ordered(3))
```

```python
import numpy as np

import jax
import jax.numpy as jnp
from jax import lax
from jax.experimental import pallas as pl
from jax.experimental.pallas import tpu as pltpu

F32 = jnp.float32
BF16 = jnp.bfloat16

D_MODEL = 1024
DEPTH = 2
HEAD_DIM = 64
N_HEADS = 8
BLOCK = 128
SUB = 4
TILE = SUB * BLOCK
STACK = 8 * BLOCK
D_ATTN = 512
D_KV = 128
D_IN = 2816
N_CHIPS = 4
N_DEV = 8
W_IN_BLK = D_IN // N_CHIPS
W_OUT_BLK = D_MODEL // N_CHIPS
W_ADA_BLK = 3 * D_MODEL // N_CHIPS
CHUNK_ROWS = HEAD_DIM
N_CHUNKS = W_IN_BLK // CHUNK_ROWS
EPS = 1e-6
NEG_INF = -1e30

C_Q, C_K, C_V, C_GA, C_U, C_VG, C_GG = 0, 512, 640, 768, 1280, 1792, 2304

ADAM_LR = 0.001
ADAM_B1 = 0.9
ADAM_B2 = 0.999
ADAM_EPS = 1e-08
ADAM_WD = 0.01
ADAM_STEP = 10

MESH = pl.DeviceIdType.MESH
MIB = 1024 * 1024
ANY = pl.BlockSpec(memory_space=pl.ANY)
VMEM = pl.BlockSpec(memory_space=pltpu.VMEM)

NT_DIMS = (((1,), (1,)), ((), ()))
TN_DIMS = (((0,), (0,)), ((), ()))

_PAIR_ORDER = (0, 4, 1, 5, 2, 6, 3, 7)
_CHUNK_SRC = np.array([
    list(_PAIR_ORDER) + [8, 9, 10],
    [0] + [1 + h for h in _PAIR_ORDER] + [9, 10],
    list(range(N_CHUNKS)),
    list(range(N_CHUNKS)),
], np.int32)
_CHUNK_POS = np.argsort(_CHUNK_SRC, axis=1).astype(np.int32)


def _bias_table():
    i = np.arange(N_HEADS * BLOCK)[:, None]
    j = np.arange(3 * BLOCK)[None, :]
    dist = np.abs(j - BLOCK - (i % BLOCK))
    slope = 2.0 ** -(i // BLOCK + 1.0)
    inner = np.where(dist <= BLOCK, -(slope * dist), NEG_INF)
    first = np.where(j >= BLOCK, inner, NEG_INF)
    last = np.where(j < 2 * BLOCK, inner, NEG_INF)
    return np.stack([first, inner, last]).astype(np.float32)


def _full(shape):
    n = len(shape)
    return pl.BlockSpec(shape, lambda *_: (0,) * n)


def _sds(shape, dtype=F32):
    return jax.ShapeDtypeStruct(shape, dtype)


def _coords():
    return lax.axis_index("x"), lax.axis_index("y"), lax.axis_index("c")


def _other_chips(x, y):
    return [(1 - x, y), (x, 1 - y), (1 - x, 1 - y)]


def _in_hbm(*operands):
    return [pltpu.with_memory_space_constraint(a, pltpu.HBM) if a.size * a.dtype.itemsize >= MIB // 4 else a
            for a in operands]


def _remote(src, dst, sems, dev):
    return pltpu.make_async_remote_copy(src_ref=src, dst_ref=dst, send_sem=sems[0], recv_sem=sems[1],
                                        device_id=dev, device_id_type=MESH)


class _Job:
    def __init__(self, inputs, out_shapes, n_remote, n_local, make, then=None, in_place=False):
        self.inputs, self.out_shapes, self.n_remote, self.n_local, self.make = inputs, out_shapes, n_remote, n_local, make
        self.then = then
        self.in_place = in_place


def _job_aliases(jobs, in_base, out_base):
    aliases, a, b = {}, 0, 0
    for j in jobs:
        if j.in_place:
            aliases.update({in_base + a + k: out_base + b + k for k in range(len(j.inputs))})
        a, b = a + len(j.inputs), b + len(j.out_shapes)
    return aliases


def _job_copies(jobs, jin, jout, sems, second=False):
    send, recv, loc = sems
    res, a, b, r, l = [], 0, 0, 0, 0
    for j in jobs:
        build = j.then if second else j.make
        if build is not None:
            res += build(jin[a:a + len(j.inputs)], jout[b:b + len(j.out_shapes)],
                         lambda k, r=r: (send.at[r + k], recv.at[r + k]), lambda k, l=l: loc.at[l + k])
        a, b, r, l = a + len(j.inputs), b + len(j.out_shapes), r + j.n_remote, l + j.n_local
    return res


def _run(copies):
    for cp in copies:
        cp.start()
    for cp in copies:
        cp.wait()


def _job_gather(sources, shapes):
    n = len(sources)

    def make(ins, outs, rsem, lsem):
        x, y, c = _coords()
        j = 2 * x + y
        res = []
        for t, ((_, layer), src, dst) in enumerate(zip(sources, ins, outs)):
            src = src if layer is None else src.at[layer]
            res.append(pltpu.make_async_copy(src, dst.at[j], lsem(t)))
            for k, chip in enumerate(_other_chips(x, y)):
                res.append(_remote(src, dst.at[j], rsem(3 * t + k), (*chip, c)))
        return res

    return _Job([a for a, _ in sources], [_sds((N_CHIPS,) + s, BF16) for s in shapes], 3 * n, n, make)


def _job_swap(g):
    _, rows, cols = g.shape
    half = rows // 2

    def make(ins, outs, rsem, lsem):
        x, y, c = _coords()
        return [_remote(ins[0].at[:, pl.ds((1 - c) * half, half), :], outs[0], rsem(0), (x, y, 1 - c))]

    return _Job([g], [_sds((N_CHIPS, half, cols))], 1, 0, make)


def _job_scatter(p):
    def make(ins, outs, rsem, lsem):
        x, y, c = _coords()
        return [_remote(ins[0].at[2 * chip[0] + chip[1]], outs[0].at[k], rsem(k), (*chip, c))
                for k, chip in enumerate(_other_chips(x, y))]

    return _Job([p], [_sds((3,) + p.shape[1:], p.dtype)], 3, 0, make)


def _job_all_gather(blk):
    m_per = blk.shape[0]

    def rows(ref, px, py, pc):
        return ref.at[pl.ds((4 * px + 2 * py + pc) * m_per, m_per), :]

    def make(ins, outs, rsem, lsem):
        x, y, c = _coords()
        res = [pltpu.make_async_copy(ins[0], rows(outs[0], x, y, c), lsem(0)),
               _remote(ins[0], rows(outs[0], x, y, c), rsem(0), (x, y, 1 - c))]
        res += [_remote(ins[0], rows(outs[0], x, y, c), rsem(1 + k), (*chip, c)) for k, chip in enumerate(_other_chips(x, y))]
        return res

    def then(ins, outs, rsem, lsem):
        x, y, c = _coords()
        return [_remote(rows(outs[0], *chip, c), rows(outs[0], *chip, c), rsem(4 + k), (x, y, 1 - c))
                for k, chip in enumerate(_other_chips(x, y))]

    return _Job([blk], [_sds((N_DEV * m_per, blk.shape[1]), blk.dtype)], 7, 1, make, then)


def _job_join(f):
    half = f.shape[0] // 2

    def make(ins, outs, rsem, lsem):
        x, y, c = _coords()
        mine = pl.ds(c * half, half)
        return [_remote(ins[0].at[mine, :], outs[0].at[mine, :], rsem(0), (x, y, 1 - c))]

    return _Job([f], [_sds(f.shape, f.dtype)], 1, 0, make, in_place=True)


def _pallas(body, *, name, grid, in_specs, out_specs, out_shape, operands, vmem_mib, jobs=()):
    in_specs, out_specs, out_shape = list(in_specs), list(out_specs), list(out_shape)
    n_in, n_out = len(in_specs), len(out_specs)
    j_in = [a for j in jobs for a in j.inputs]
    j_out = [s for j in jobs for s in j.out_shapes]
    n_rem = max(1, sum(j.n_remote for j in jobs))
    n_loc = max(1, sum(j.n_local for j in jobs))
    scratch = [pltpu.SemaphoreType.DMA((n_rem,)), pltpu.SemaphoreType.DMA((n_rem,)),
               pltpu.SemaphoreType.DMA((n_loc,))] if jobs else []

    def wrapped(*refs):
        ins = refs[:n_in]
        jin = refs[n_in:n_in + len(j_in)]
        outs = refs[n_in + len(j_in):n_in + len(j_in) + n_out]
        jout = refs[n_in + len(j_in) + n_out:n_in + len(j_in) + n_out + len(j_out)]

        if jobs:
            first = last = None
            for d, n in enumerate(grid):
                f, e = pl.program_id(d) == 0, pl.program_id(d) == n - 1
                first, last = (f, e) if first is None else (first & f, last & e)

            @pl.when(first)
            def _():
                for cp in _job_copies(jobs, jin, jout, refs[-3:]):
                    cp.start()

        body(*ins, *outs)

        if jobs:
            @pl.when(last)
            def _():
                for cp in _job_copies(jobs, jin, jout, refs[-3:]):
                    cp.wait()
                _run(_job_copies(jobs, jin, jout, refs[-3:], second=True))

    return pl.pallas_call(
        wrapped, name=name, grid=grid,
        in_specs=in_specs + [ANY] * len(j_in), out_specs=out_specs + [ANY] * len(j_out),
        out_shape=out_shape + j_out, scratch_shapes=scratch, input_output_aliases=_job_aliases(jobs, n_in, n_out),
        compiler_params=pltpu.CompilerParams(dimension_semantics=("arbitrary",) * len(grid),
                                             vmem_limit_bytes=vmem_mib * MIB),
    )(*_in_hbm(*operands, *j_in))


def _comm(jobs, name):
    j_in = [a for j in jobs for a in j.inputs]
    j_out = [s for j in jobs for s in j.out_shapes]
    n_rem = max(1, sum(j.n_remote for j in jobs))
    n_loc = max(1, sum(j.n_local for j in jobs))

    def body(*refs):
        jin, jout = refs[:len(j_in)], refs[len(j_in):len(j_in) + len(j_out)]
        _run(_job_copies(jobs, jin, jout, refs[-3:]))
        _run(_job_copies(jobs, jin, jout, refs[-3:], second=True))

    return pl.pallas_call(
        body, name=name, in_specs=[ANY] * len(j_in), out_specs=[ANY] * len(j_out), out_shape=j_out,
        scratch_shapes=[pltpu.SemaphoreType.DMA((n_rem,)), pltpu.SemaphoreType.DMA((n_rem,)),
                        pltpu.SemaphoreType.DMA((n_loc,))],
        input_output_aliases=_job_aliases(jobs, 0, 0),
    )(*_in_hbm(*j_in))


def _sigmoid(x):
    return 1.0 / (1.0 + jnp.exp(-x))


def _lo_mask(shape):
    return lax.broadcasted_iota(jnp.int32, shape, len(shape) - 1) < HEAD_DIM


def _half_sum(x, lo):
    a = jnp.sum(jnp.where(lo, x, 0.0), axis=-1, keepdims=True)
    b = jnp.sum(jnp.where(lo, 0.0, x), axis=-1, keepdims=True)
    return jnp.where(lo, a, b)


def _half_rms_scale(x, lo):
    return lax.rsqrt(_half_sum(x * x, lo) * (1.0 / HEAD_DIM) + EPS)


def _stack_heads(pairs, lo):
    return jnp.concatenate([jnp.where(lo, t, 0.0) for t in pairs] + [jnp.where(lo, 0.0, t) for t in pairs], axis=0)


def _unstack_pair(stack, p, lo):
    return jnp.where(lo, stack[BLOCK * p:BLOCK * (p + 1)], stack[BLOCK * (4 + p):BLOCK * (5 + p)])


def _attention_probs(q_stack, kn, bias_ref, sink_ref):
    rows = N_HEADS * BLOCK
    s = lax.dot_general(q_stack, kn, NT_DIMS, preferred_element_type=F32) + bias_ref[...]
    sink = jnp.concatenate([jnp.full((BLOCK, BLOCK), sink_ref[h], F32) for h in range(N_HEADS)], axis=0)
    cols = [s[:, BLOCK * j:BLOCK * (j + 1)] for j in range(3)]
    top = jnp.max(jnp.maximum(jnp.maximum(cols[0], cols[1]), cols[2]), axis=-1, keepdims=True)
    m = jnp.maximum(jnp.broadcast_to(top, (rows, BLOCK)), sink)
    e = [jnp.exp(c - m) for c in cols]
    es = jnp.exp(sink - m)
    inv = 1.0 / (jnp.broadcast_to(jnp.sum((e[0] + e[1]) + e[2], axis=-1, keepdims=True), (rows, BLOCK)) + es)
    return jnp.concatenate([c * inv for c in e], axis=1), es * inv


def _kv_rows(cur_ref, pkv_ref, nkv_ref):
    k = jnp.concatenate([pkv_ref[:, 0:D_KV], cur_ref[:, C_K:C_K + D_KV], nkv_ref[:, 0:D_KV]], axis=0)
    v = jnp.concatenate([pkv_ref[:, D_KV:2 * D_KV], cur_ref[:, C_V:C_V + D_KV], nkv_ref[:, D_KV:2 * D_KV]], axis=0)
    return k, v


def _overlap_add(parts):
    blocks = []
    for j in range(SUB + 2):
        terms = [parts[b][BLOCK * (j - b):BLOCK * (j - b + 1)] for b in range(SUB) if 0 <= j - b <= 2]
        total = terms[0]
        for t in terms[1:]:
            total = total + t
        blocks.append(total)
    return jnp.concatenate(blocks, axis=0)


def _mix_specs(nt):
    cur = pl.BlockSpec((TILE, D_IN), lambda i: (i, 0))
    kv_col = C_K // (2 * D_KV)
    pkv = pl.BlockSpec((BLOCK, 2 * D_KV), lambda i: (jnp.maximum(i * SUB - 1, 0), kv_col))
    nkv = pl.BlockSpec((BLOCK, 2 * D_KV), lambda i: (jnp.minimum((i + 1) * SUB, nt * SUB - 1), kv_col))
    table = (None, N_HEADS * BLOCK, 3 * BLOCK)
    first = pl.BlockSpec(table, lambda i: (jnp.where(i == 0, 0, 1), 0, 0))
    inner = pl.BlockSpec(table, lambda i: (1, 0, 0))
    last = pl.BlockSpec(table, lambda i: (jnp.where(i == nt - 1, 2, 1), 0, 0))
    return cur, pkv, nkv, [first] + [inner] * (SUB - 2) + [last]


V_NG, V_SCALE, V_SHIFT, V_GATE = 0, 1, 2, 3
VECS = _full((8, D_MODEL))


def _row(v_ref, k):
    return v_ref[k:k + 1, :]


def _adaln(xv, v_ref):
    r = lax.rsqrt(jnp.mean(xv * xv, axis=-1, keepdims=True) + EPS)
    return ((xv * r) * _row(v_ref, V_NG)) * (1.0 + _row(v_ref, V_SCALE)) + _row(v_ref, V_SHIFT)


def _proj_fwd(x, vecs, wt, name, jobs=()):
    s = x.shape[0]
    ts = min(512, s)

    def body(x_ref, v_ref, w_ref, o_ref):
        h = _adaln(x_ref[...], v_ref)
        o_ref[...] = lax.dot_general(h.astype(BF16), w_ref[...], NT_DIMS, preferred_element_type=F32)

    return _pallas(
        body, name=name, grid=(s // ts,),
        in_specs=[pl.BlockSpec((ts, D_MODEL), lambda i: (i, 0)), VECS, _full((D_IN, D_MODEL))],
        out_specs=[pl.BlockSpec((ts, D_IN), lambda i: (i, 0))], out_shape=[_sds((s, D_IN))],
        operands=(x, vecs, wt), vmem_mib=48, jobs=jobs)


def _diagonal():
    return lax.broadcasted_iota(jnp.int32, (BLOCK, BLOCK), 0) == lax.broadcasted_iota(jnp.int32, (BLOCK, BLOCK), 1)


def _column_as_row(wide, eye):
    return jnp.sum(jnp.where(eye, wide, 0.0), axis=0, keepdims=True)


def _mix_fwd(proj, bias, sink, qg2, kg2, ws, bsp, name, jobs=()):
    s = proj.shape[0]
    nt = s // TILE

    def body(sink_ref, cur_ref, pkv_ref, nkv_ref, *rest):
        bias_refs = rest[:SUB]
        qg_ref, kg_ref, ws_ref, bsp_ref, y_ref, p_ref, ps_ref, attn_ref, sv_ref = rest[SUB:]
        lo = _lo_mask((BLOCK, BLOCK))
        eye = _diagonal()
        lo_kv = _lo_mask((TILE + 2 * BLOCK, BLOCK))
        k_all, v_all = _kv_rows(cur_ref, pkv_ref, nkv_ref)
        kn_all = ((k_all * _half_rms_scale(k_all, lo_kv)) * kg_ref[...]).astype(BF16)
        vb_all = v_all.astype(BF16)
        for b in range(SUB):
            rows = slice(BLOCK * b, BLOCK * (b + 1))
            window = slice(BLOCK * b, BLOCK * (b + 3))
            qn = []
            for p in range(4):
                q = cur_ref[rows, C_Q + BLOCK * p:C_Q + BLOCK * (p + 1)]
                qn.append(((q * _half_rms_scale(q, lo)) * qg_ref[...]) * 0.125)
            q_stack = _stack_heads(qn, lo).astype(BF16)
            prob, psink = _attention_probs(q_stack, kn_all[window], bias_refs[b], sink_ref)
            pb = prob.astype(BF16)
            p_ref[STACK * b:STACK * (b + 1), :] = pb
            ps_ref[N_HEADS * b:N_HEADS * (b + 1), :] = jnp.concatenate(
                [_column_as_row(psink[BLOCK * h:BLOCK * (h + 1)], eye) for h in range(N_HEADS)], axis=0)
            o_stack = jnp.dot(pb, vb_all[window], preferred_element_type=F32)
            for p in range(4):
                g = cur_ref[rows, C_GA + BLOCK * p:C_GA + BLOCK * (p + 1)]
                attn = _unstack_pair(o_stack, p, lo)
                attn_ref[rows, BLOCK * p:BLOCK * (p + 1)] = attn.astype(BF16)
                y_ref[rows, BLOCK * p:BLOCK * (p + 1)] = (attn * (g * _sigmoid(g))).astype(BF16)

        for p in range(4):
            cols = slice(C_VG + BLOCK * p, C_VG + BLOCK * (p + 1))
            vn = []
            for b in range(SUB):
                vg = cur_ref[BLOCK * b:BLOCK * (b + 1), cols]
                vn.append((vg * _half_rms_scale(vg, lo)).astype(BF16))
            vn = jnp.concatenate(vn, axis=1)
            sv_a = jnp.dot(ws_ref[2 * p], vn, preferred_element_type=F32)
            sv_b = jnp.dot(ws_ref[2 * p + 1], vn, preferred_element_type=F32)
            for b in range(SUB):
                rows = slice(BLOCK * b, BLOCK * (b + 1))
                lanes = slice(BLOCK * b, BLOCK * (b + 1))
                sv = jnp.where(lo, sv_a[:, lanes], sv_b[:, lanes]) + bsp_ref[p]
                sv_ref[rows, BLOCK * p:BLOCK * (p + 1)] = sv.astype(BF16)
                u = cur_ref[rows, C_U + BLOCK * p:C_U + BLOCK * (p + 1)]
                g = cur_ref[rows, C_GG + BLOCK * p:C_GG + BLOCK * (p + 1)]
                y_ref[rows, D_ATTN + BLOCK * p:D_ATTN + BLOCK * (p + 1)] = ((u * sv) * (g * _sigmoid(g))).astype(BF16)

    cur, pkv, nkv, bias_specs = _mix_specs(nt)
    nb = nt * SUB
    half = pl.BlockSpec((TILE, D_ATTN), lambda i: (i, 0))
    return _pallas(
        body, name=name, grid=(nt,),
        in_specs=[pl.BlockSpec(memory_space=pltpu.SMEM), cur, pkv, nkv, *bias_specs, _full((1, BLOCK)), _full((1, BLOCK)),
                  _full((8, BLOCK, BLOCK)), _full((4, BLOCK, BLOCK))],
        out_specs=[pl.BlockSpec((TILE, D_MODEL), lambda i: (i, 0)), pl.BlockSpec((SUB * STACK, 3 * BLOCK), lambda i: (i, 0)),
                   pl.BlockSpec((SUB * N_HEADS, BLOCK), lambda i: (i, 0)), half, half],
        out_shape=[_sds((s, D_MODEL), BF16), _sds((nb * STACK, 3 * BLOCK), BF16), _sds((nb * N_HEADS, BLOCK)),
                   _sds((s, D_ATTN), BF16), _sds((s, D_ATTN), BF16)],
        operands=(sink, proj, proj, proj, *([bias] * SUB), qg2, kg2, ws, bsp), vmem_mib=56, jobs=jobs)


def _out_proj_fwd(y, x, vecs, w_out, vecs_next, wt, name):
    s = x.shape[0]
    ts = min(512, s)

    def body(y_ref, x_ref, v_ref, w_ref, vn_ref, wt_ref, xn_ref, p_ref):
        xv = x_ref[...] + _row(v_ref, V_GATE) * jnp.dot(y_ref[...], w_ref[...], preferred_element_type=F32)
        xn_ref[...] = xv
        p_ref[...] = lax.dot_general(_adaln(xv, vn_ref).astype(BF16), wt_ref[...], NT_DIMS, preferred_element_type=F32)

    row = pl.BlockSpec((ts, D_MODEL), lambda i: (i, 0))
    return _pallas(
        body, name=name, grid=(s // ts,),
        in_specs=[row, row, VECS, _full((D_MODEL, D_MODEL)), VECS, _full((D_IN, D_MODEL))],
        out_specs=[row, pl.BlockSpec((ts, D_IN), lambda i: (i, 0))], out_shape=[_sds((s, D_MODEL)), _sds((s, D_IN))],
        operands=(y, x, vecs, w_out, vecs_next, wt), vmem_mib=56)


def _out_loss_bwd(y, x, vecs, w_out, target, name):
    s = x.shape[0]
    ts = min(512, s)
    steps = s // ts

    def body(y_ref, x_ref, v_ref, w_ref, t_ref, dx_ref, sq_ref, dy_ref, gw_ref, dg_ref):
        @pl.when(pl.program_id(0) == 0)
        def _():
            sq_ref[...] = jnp.zeros_like(sq_ref)
            gw_ref[...] = jnp.zeros_like(gw_ref)

        gate = _row(v_ref, V_GATE)
        yv = y_ref[...]
        out = x_ref[...] + gate * jnp.dot(yv, w_ref[...], preferred_element_type=F32)
        diff = out - t_ref[...]
        dx = diff * (1.0 / D_MODEL)
        dx_ref[...] = dx
        per_token = jnp.sum(diff * diff, axis=-1, keepdims=True) * (1.0 / D_MODEL)
        sq_ref[...] += jnp.sum(per_token, axis=0, keepdims=True)
        dy_ref[...] = lax.dot_general((dx * gate).astype(BF16), w_ref[...], NT_DIMS, preferred_element_type=F32)
        gw_ref[...] += lax.dot_general(yv, dx.astype(BF16), TN_DIMS, preferred_element_type=F32)

        @pl.when(pl.program_id(0) == steps - 1)
        def _():
            m = gw_ref[...]
            dg_ref[...] = jnp.sum(w_ref[...].astype(F32) * m, axis=0, keepdims=True)
            gw_ref[...] = m * _row(v_ref, V_GATE)

    row = pl.BlockSpec((ts, D_MODEL), lambda i: (i, 0))
    return _pallas(
        body, name=name, grid=(s // ts,), in_specs=[row, row, VECS, _full((D_MODEL, D_MODEL)), row],
        out_specs=[row, _full((1, 1)), row, _full((D_MODEL, D_MODEL)), _full((1, D_MODEL))],
        out_shape=[_sds((s, D_MODEL)), _sds((1, 1)), _sds((s, D_MODEL)), _sds((D_MODEL, D_MODEL)), _sds((1, D_MODEL))],
        operands=(y, x, vecs, w_out, target), vmem_mib=48)


def _mix_bwd(dy, proj, probs, psink, attn, sv, qg2, kg2, wst, name, jobs=()):
    s = proj.shape[0]
    nt = s // TILE

    def body(dy_ref, cur_ref, pkv_ref, nkv_ref, p_ref, ps_ref, attn_ref, sv_ref, qg_ref, kg_ref, wst_ref,
             dpb_ref, dkv_ref, p0_ref, p2_ref, dqg_ref, dkg_ref, dsink_ref, dws_ref, dbsp_ref):
        def put(rows, col, value):
            dpb_ref[rows, col:col + BLOCK] = value.astype(BF16)

        @pl.when(pl.program_id(0) == 0)
        def _():
            dqg_ref[...] = jnp.zeros_like(dqg_ref)
            dkg_ref[...] = jnp.zeros_like(dkg_ref)
            dsink_ref[...] = jnp.zeros_like(dsink_ref)
            dws_ref[...] = jnp.zeros_like(dws_ref)
            dbsp_ref[...] = jnp.zeros_like(dbsp_ref)

        lo = _lo_mask((BLOCK, BLOCK))
        lo_kv = _lo_mask((TILE + 2 * BLOCK, BLOCK))
        eye = _diagonal()
        lane_row = lax.broadcasted_iota(jnp.int32, (1, BLOCK), 1)
        qg = qg_ref[...]
        kg = kg_ref[...]

        k_all, v_all = _kv_rows(cur_ref, pkv_ref, nkv_ref)
        rk = _half_rms_scale(k_all, lo_kv)
        khat = k_all * rk
        kn_all = (khat * kg).astype(BF16)
        vb_all = v_all.astype(BF16)

        dkn_parts, dv_parts = [], []
        dsink = jnp.zeros((1, BLOCK), F32)
        dqg = jnp.zeros((1, BLOCK), F32)
        for b in range(SUB):
            rows = slice(BLOCK * b, BLOCK * (b + 1))
            window = slice(BLOCK * b, BLOCK * (b + 3))
            kn, vb = kn_all[window], vb_all[window]

            qhat, rq = [], []
            for p in range(4):
                q = cur_ref[rows, C_Q + BLOCK * p:C_Q + BLOCK * (p + 1)]
                r = _half_rms_scale(q, lo)
                rq.append(r)
                qhat.append(q * r)
            q_stack = _stack_heads([(qh * qg) * 0.125 for qh in qhat], lo).astype(BF16)
            pb = p_ref[STACK * b:STACK * (b + 1), :]
            prob = pb.astype(F32)

            dout = []
            for p in range(4):
                g = cur_ref[rows, C_GA + BLOCK * p:C_GA + BLOCK * (p + 1)]
                sg = _sigmoid(g)
                dya = dy_ref[rows, BLOCK * p:BLOCK * (p + 1)]
                attn = attn_ref[rows, BLOCK * p:BLOCK * (p + 1)]
                put(rows, C_GA + BLOCK * p, dya * attn * (sg * (1.0 + g * (1.0 - sg))))
                dout.append(dya * (g * sg))
            do_stack = _stack_heads(dout, lo).astype(BF16)
            dp = lax.dot_general(do_stack, vb, NT_DIMS, preferred_element_type=F32)
            delta = jnp.sum(prob * dp, axis=-1, keepdims=True)
            dsb = (prob * (dp - delta)).astype(BF16)

            for h in range(N_HEADS):
                delta_row = _column_as_row(jnp.broadcast_to(delta[BLOCK * h:BLOCK * (h + 1)], (BLOCK, BLOCK)), eye)
                tot = jnp.sum(ps_ref[N_HEADS * b + h:N_HEADS * b + h + 1, :] * delta_row, axis=-1, keepdims=True)
                dsink = dsink - jnp.where(lane_row == h, tot, 0.0)

            dq_stack = jnp.dot(dsb, kn, preferred_element_type=F32) * 0.125
            dkn_parts.append(lax.dot_general(dsb, q_stack, TN_DIMS, preferred_element_type=F32))
            dv_parts.append(lax.dot_general(pb, do_stack, TN_DIMS, preferred_element_type=F32))

            for p in range(4):
                dqn = _unstack_pair(dq_stack, p, lo)
                qh = qhat[p]
                dqg = dqg + jnp.sum(dqn * qh, axis=0, keepdims=True)
                dqh = dqn * qg
                mean = _half_sum(dqh * qh, lo) * (1.0 / HEAD_DIM)
                put(rows, C_Q + BLOCK * p, rq[p] * (dqh - qh * mean))

        dws_new, dbs_new = [], []
        for p in range(4):
            rs, vnfs, vns, dsvs, dbs = [], [], [], [], None
            for b in range(SUB):
                rows = slice(BLOCK * b, BLOCK * (b + 1))
                vg = cur_ref[rows, C_VG + BLOCK * p:C_VG + BLOCK * (p + 1)]
                r = _half_rms_scale(vg, lo)
                vnf = vg * r
                sv = sv_ref[rows, BLOCK * p:BLOCK * (p + 1)]
                u = cur_ref[rows, C_U + BLOCK * p:C_U + BLOCK * (p + 1)]
                g = cur_ref[rows, C_GG + BLOCK * p:C_GG + BLOCK * (p + 1)]
                sg = _sigmoid(g)
                dym = dy_ref[rows, D_ATTN + BLOCK * p:D_ATTN + BLOCK * (p + 1)]
                put(rows, C_GG + BLOCK * p, dym * (u * sv) * (sg * (1.0 + g * (1.0 - sg))))
                dgm = dym * (g * sg)
                put(rows, C_U + BLOCK * p, dgm * sv)
                dsv = dgm * u
                term = jnp.where(lo, jnp.sum(jnp.where(lo, dsv, 0.0), axis=-1, keepdims=True),
                                 jnp.sum(jnp.where(lo, 0.0, dsv), axis=-1, keepdims=True))
                dbs = term if dbs is None else dbs + term
                rs.append(r)
                vnfs.append(vnf)
                vns.append(vnf.astype(BF16))
                dsvs.append(dsv)
            dbs_new.append(dbs)
            vn = jnp.concatenate(vns, axis=1)
            dsv = jnp.concatenate(dsvs, axis=1)
            lo_t = (lax.broadcasted_iota(jnp.int32, dsv.shape, 1) & (BLOCK - 1)) < HEAD_DIM
            dws_new.append(lax.dot_general(jnp.where(lo_t, dsv, 0.0).astype(BF16), vn, NT_DIMS, preferred_element_type=F32))
            dws_new.append(lax.dot_general(jnp.where(lo_t, 0.0, dsv).astype(BF16), vn, NT_DIMS, preferred_element_type=F32))
            dsvb = dsv.astype(BF16)
            dvn_a = jnp.dot(wst_ref[2 * p], dsvb, preferred_element_type=F32)
            dvn_b = jnp.dot(wst_ref[2 * p + 1], dsvb, preferred_element_type=F32)
            for b in range(SUB):
                lanes = slice(BLOCK * b, BLOCK * (b + 1))
                dvn = jnp.where(lo, dvn_a[:, lanes], dvn_b[:, lanes])
                mean = _half_sum(dvn * vnfs[b], lo) * (1.0 / HEAD_DIM)
                put(slice(BLOCK * b, BLOCK * (b + 1)), C_VG + BLOCK * p, rs[b] * (dvn - vnfs[b] * mean))

        dsink_ref[...] += dsink
        dqg = jnp.broadcast_to(dqg, (8, BLOCK))
        dqg_ref[...] += dqg + pltpu.roll(dqg, HEAD_DIM, 1)

        dkn = _overlap_add(dkn_parts)
        dv = _overlap_add(dv_parts)
        dkg = jnp.broadcast_to(jnp.sum(dkn * khat, axis=0, keepdims=True), (8, BLOCK))
        dkg_ref[...] += dkg + pltpu.roll(dkg, HEAD_DIM, 1)
        dkh = dkn * kg
        dk = rk * (dkh - khat * (_half_sum(dkh * khat, lo_kv) * (1.0 / HEAD_DIM)))
        dpb_ref[:, C_K:C_GA] = jnp.zeros((TILE, 2 * D_KV), BF16)
        dkv_ref[:, 0:D_KV] = dk[BLOCK:BLOCK + TILE]
        dkv_ref[:, D_KV:2 * D_KV] = dv[BLOCK:BLOCK + TILE]
        p0_ref[:, 0:D_KV] = dk[0:BLOCK]
        p0_ref[:, D_KV:2 * D_KV] = dv[0:BLOCK]
        p2_ref[:, 0:D_KV] = dk[BLOCK + TILE:]
        p2_ref[:, D_KV:2 * D_KV] = dv[BLOCK + TILE:]
        for g, new in enumerate(dws_new):
            dws_ref[g] += new
        for p, new in enumerate(dbs_new):
            dbsp_ref[p] += new

    cur, pkv, nkv, _ = _mix_specs(nt)
    kv_blk = (BLOCK, 2 * D_KV)
    half = pl.BlockSpec((TILE, D_ATTN), lambda i: (i, 0))
    return _pallas(
        body, name=name, grid=(nt,),
        in_specs=[pl.BlockSpec((TILE, D_MODEL), lambda i: (i, 0)), cur, pkv, nkv,
                  pl.BlockSpec((SUB * STACK, 3 * BLOCK), lambda i: (i, 0)), pl.BlockSpec((SUB * N_HEADS, BLOCK), lambda i: (i, 0)),
                  half, half, _full((1, BLOCK)), _full((1, BLOCK)), _full((8, BLOCK, BLOCK))],
        out_specs=[cur, pl.BlockSpec((TILE, 2 * D_KV), lambda i: (i, 0)),
                   pl.BlockSpec(kv_blk, lambda i: ((i + nt - 1) % nt, 0)),
                   pl.BlockSpec(kv_blk, lambda i: ((i + 1) % nt, 0)),
                   _full((8, BLOCK)), _full((8, BLOCK)), _full((1, BLOCK)),
                   _full((8, BLOCK, BLOCK)), _full((4, BLOCK, BLOCK))],
        out_shape=[_sds((s, D_IN), BF16), _sds((s, 2 * D_KV)), _sds((nt * BLOCK, 2 * D_KV)), _sds((nt * BLOCK, 2 * D_KV)),
                   _sds((8, BLOCK)), _sds((8, BLOCK)), _sds((1, BLOCK)),
                   _sds((8, BLOCK, BLOCK)), _sds((4, BLOCK, BLOCK))],
        operands=(dy, proj, proj, proj, probs, psink, attn, sv, qg2, kg2, wst), vmem_mib=56, jobs=jobs)


def _w_in_grad(dpb, dkv, p0, p2, x, vecs, name, jobs=()):
    s = x.shape[0]
    ts = min(2 * TILE, s)
    tiles = ts // TILE

    def body(dpb_ref, dkv_ref, p0_ref, p2_ref, x_ref, v_ref, gw_ref, dkvb_ref):
        @pl.when(pl.program_id(0) == 0)
        def _():
            gw_ref[...] = jnp.zeros_like(gw_ref)

        h = _adaln(x_ref[...], v_ref).astype(BF16)
        for t in range(tiles):
            halo = slice(BLOCK * t, BLOCK * (t + 1))
            first = slice(TILE * t, TILE * t + BLOCK)
            last = slice(TILE * (t + 1) - BLOCK, TILE * (t + 1))
            dkvb_ref[first, :] = (dkv_ref[first, :] + p2_ref[halo, :]).astype(BF16)
            if SUB > 2:
                inner = slice(TILE * t + BLOCK, TILE * (t + 1) - BLOCK)
                dkvb_ref[inner, :] = dkv_ref[inner, :].astype(BF16)
            dkvb_ref[last, :] = (dkv_ref[last, :] + p0_ref[halo, :]).astype(BF16)
        gw_ref[...] += lax.dot_general(dpb_ref[...], h, TN_DIMS, preferred_element_type=F32)
        gw_ref[C_K:C_GA, :] += lax.dot_general(dkvb_ref[...], h, TN_DIMS, preferred_element_type=F32)

    kv = pl.BlockSpec((ts, 2 * D_KV), lambda i: (i, 0))
    halo = pl.BlockSpec((tiles * BLOCK, 2 * D_KV), lambda i: (i, 0))
    return _pallas(
        body, name=name, grid=(s // ts,),
        in_specs=[pl.BlockSpec((ts, D_IN), lambda i: (i, 0)), kv, halo, halo,
                  pl.BlockSpec((ts, D_MODEL), lambda i: (i, 0)), VECS],
        out_specs=[_full((D_IN, D_MODEL)), kv], out_shape=[_sds((D_IN, D_MODEL)), _sds((s, 2 * D_KV), BF16)],
        operands=(dpb, dkv, p0, p2, x, vecs), vmem_mib=56, jobs=jobs)


def _proj_bwd(dpb, dkvb, x, dxo, vecs, wt, name, below=None, jobs=()):
    s = x.shape[0]
    ts = min(512, s)
    steps = s // ts

    def body(dpb_ref, dkvb_ref, x_ref, dxo_ref, v_ref, w_ref, *rest):
        if below is None:
            dxi_ref, dsh_ref, dsc_ref, dng_ref = rest
        else:
            y_ref, vb_ref, wo_ref, dxi_ref, dsh_ref, dsc_ref, dng_ref, dy_ref, gw_ref, dg_ref = rest

        @pl.when(pl.program_id(0) == 0)
        def _():
            dsh_ref[...] = jnp.zeros_like(dsh_ref)
            dsc_ref[...] = jnp.zeros_like(dsc_ref)
            dng_ref[...] = jnp.zeros_like(dng_ref)
            if below is not None:
                gw_ref[...] = jnp.zeros_like(gw_ref)

        dh = (jnp.dot(dpb_ref[...], w_ref[...], preferred_element_type=F32)
              + jnp.dot(dkvb_ref[...], w_ref[C_K:C_GA, :], preferred_element_type=F32))

        xv = x_ref[...]
        r = lax.rsqrt(jnp.mean(xv * xv, axis=-1, keepdims=True) + EPS)
        xn = xv * r
        ngv = _row(v_ref, V_NG)
        sc1 = 1.0 + _row(v_ref, V_SCALE)
        dsh_ref[...] += jnp.sum(dh, axis=0, keepdims=True)
        dsc_ref[...] += jnp.sum(dh * (xn * ngv), axis=0, keepdims=True)
        dh1 = dh * sc1
        dng_ref[...] += jnp.sum(dh1 * xn, axis=0, keepdims=True)
        dxn = dh1 * ngv
        dx = r * (dxn - xn * jnp.mean(dxn * xn, axis=-1, keepdims=True)) + dxo_ref[...]
        dxi_ref[...] = dx

        if below is not None:
            dy_ref[...] = lax.dot_general((dx * _row(vb_ref, V_GATE)).astype(BF16), wo_ref[...], NT_DIMS,
                                          preferred_element_type=F32)
            gw_ref[...] += lax.dot_general(y_ref[...], dx.astype(BF16), TN_DIMS, preferred_element_type=F32)

            @pl.when(pl.program_id(0) == steps - 1)
            def _():
                m = gw_ref[...]
                dg_ref[...] = jnp.sum(wo_ref[...].astype(F32) * m, axis=0, keepdims=True)
                gw_ref[...] = m * _row(vb_ref, V_GATE)

    row = pl.BlockSpec((ts, D_MODEL), lambda i: (i, 0))
    vec = _full((1, D_MODEL))
    square = _full((D_MODEL, D_MODEL))
    in_specs = [pl.BlockSpec((ts, D_IN), lambda i: (i, 0)), pl.BlockSpec((ts, 2 * D_KV), lambda i: (i, 0)),
                row, row, VECS, _full((D_IN, D_MODEL))]
    out_specs = [row, vec, vec, vec]
    out_shape = [_sds((s, D_MODEL)), _sds((1, D_MODEL)), _sds((1, D_MODEL)), _sds((1, D_MODEL))]
    operands = (dpb, dkvb, x, dxo, vecs, wt)
    if below is not None:
        in_specs += [row, VECS, square]
        out_specs += [row, square, vec]
        out_shape += [_sds((s, D_MODEL)), _sds((D_MODEL, D_MODEL)), _sds((1, D_MODEL))]
        operands += tuple(below)
    return _pallas(
        body, name=name, grid=(steps,), in_specs=in_specs, out_specs=out_specs, out_shape=out_shape,
        operands=operands, vmem_mib=48 if below is None else 60, jobs=jobs)


def _pair_sum(g, r, c_idx, name, send_dtype=None):
    _, rows, cols = g.shape
    half = rows // 2

    def body(c_ref, g_ref, r_ref, o_ref, *narrow):
        total = g_ref[...] + r_ref[...]
        o_ref[...] = total
        for n_ref in narrow:
            n_ref[...] = total.astype(n_ref.dtype)

    blk = (None, half, cols)
    out_blk = pl.BlockSpec(blk, lambda j, c: (j, 0, 0))
    shapes = [_sds((N_CHIPS, half, cols))] + ([_sds((N_CHIPS, half, cols), send_dtype)] if send_dtype else [])
    return pl.pallas_call(
        body, name=name,
        grid_spec=pltpu.PrefetchScalarGridSpec(
            num_scalar_prefetch=1, grid=(N_CHIPS,),
            in_specs=[pl.BlockSpec(blk, lambda j, c: (j, c[0], 0)), out_blk], out_specs=[out_blk] * len(shapes)),
        out_shape=shapes,
        compiler_params=pltpu.CompilerParams(dimension_semantics=("arbitrary",), vmem_limit_bytes=32 * MIB),
    )(*_in_hbm(c_idx, g, r))


def _chip_sum(p, r, place, name):
    _, rows, cols = p.shape
    tr = rows // 2

    def body(j_ref, p_ref, r_ref, o_ref):
        o_ref[...] = ((p_ref[...] + r_ref[0].astype(F32)) + r_ref[1].astype(F32)) + r_ref[2].astype(F32)

    return pl.pallas_call(
        body, name=name,
        grid_spec=pltpu.PrefetchScalarGridSpec(
            num_scalar_prefetch=1, grid=(2,),
            in_specs=[pl.BlockSpec((None, tr, cols), lambda t, j: (j[0], t, 0)),
                      pl.BlockSpec((3, tr, cols), lambda t, j: (0, t, 0))],
            out_specs=pl.BlockSpec((tr, cols), lambda t, j: (2 * j[1] + t, 0))),
        out_shape=_sds((2 * rows, cols)),
        compiler_params=pltpu.CompilerParams(dimension_semantics=("arbitrary",), vmem_limit_bytes=32 * MIB),
    )(*_in_hbm(place, p, r))


def _cast_permute_w_in(wt, place_chunks):
    def body(t_ref, w_ref, w0_ref, w1_ref):
        def cast_into(o_ref):
            for t in range(N_CHUNKS):
                src = pl.multiple_of(t_ref[1 + t] * CHUNK_ROWS, CHUNK_ROWS)
                o_ref[CHUNK_ROWS * t:CHUNK_ROWS * (t + 1), :] = w_ref[pl.ds(src, CHUNK_ROWS), :].astype(BF16)

        @pl.when(pl.program_id(0) == 0)
        def _():
            cast_into(w0_ref)

        @pl.when(pl.program_id(0) == 1)
        def _():
            cast_into(w1_ref)

    return pl.pallas_call(
        body, name="cast_permute_w_in",
        grid_spec=pltpu.PrefetchScalarGridSpec(
            num_scalar_prefetch=1, grid=(DEPTH,),
            in_specs=[pl.BlockSpec((None, W_IN_BLK, D_MODEL), lambda l, tbl: (l, 0, 0))],
            out_specs=[pl.BlockSpec((None, W_IN_BLK, D_MODEL), lambda l, tbl: (tbl[0], 0, 0)),
                       pl.BlockSpec((W_IN_BLK, D_MODEL), lambda l, tbl: (0, 0))]),
        out_shape=[_sds((N_CHIPS, W_IN_BLK, D_MODEL), BF16), _sds((W_IN_BLK, D_MODEL), BF16)],
        compiler_params=pltpu.CompilerParams(dimension_semantics=("arbitrary",), vmem_limit_bytes=32 * MIB),
    )(*_in_hbm(place_chunks, wt))


def _gather_inputs(c, w_out, w0):
    half = W_IN_BLK // 2

    def body(c_ref, wout_ref, mine_ref, call_ref, woutb_ref, w0_ref, send_sems, recv_sems):
        x, y, cc = _coords()
        j = 2 * x + y
        b = 2 * j + cc
        sib = (x, y, 1 - cc)
        woutb_ref[...] = wout_ref[...].astype(BF16)
        call_ref[b] = c_ref[...]
        chips = _other_chips(x, y)

        def sems(k):
            return send_sems.at[k], recv_sems.at[k]

        def half_rows(chip_index):
            return w0_ref.at[chip_index, pl.ds(cc * half, half), :]

        first = [_remote(mine_ref.at[j, pl.ds(cc * half, half), :], half_rows(j), sems(k), (*chip, cc))
                 for k, chip in enumerate(chips)]
        k = 3
        rest = []
        for fx in (0, 1):
            for fy in (0, 1):
                for fc in (0, 1):
                    if fx or fy or fc:
                        dev = (1 - x if fx else x, 1 - y if fy else y, 1 - cc if fc else cc)
                        rest.append(_remote(call_ref.at[b], call_ref.at[b], sems(k), dev))
                        k += 1
        for cp in first + rest:
            cp.start()
        passed = []
        for k, chip in enumerate(chips):
            jk = 2 * chip[0] + chip[1]
            first[k].wait_recv()
            passed.append(_remote(half_rows(jk), half_rows(jk), sems(10 + k), sib))
            passed[k].start()
        for cp in first:
            cp.wait_send()
        for cp in rest + passed:
            cp.wait()

    return pl.pallas_call(
        body, name="gather_inputs", in_specs=[VMEM, VMEM, ANY], out_specs=[VMEM, VMEM, ANY],
        out_shape=[_sds((N_DEV, 1, D_MODEL)), _sds((DEPTH, W_OUT_BLK, D_MODEL), BF16),
                   _sds((N_CHIPS, W_IN_BLK, D_MODEL), BF16)],
        scratch_shapes=[pltpu.SemaphoreType.DMA((13,)), pltpu.SemaphoreType.DMA((13,))],
        input_output_aliases={2: 2},
        compiler_params=pltpu.CompilerParams(vmem_limit_bytes=32 * MIB),
    )(c, w_out, w0)


def _ada_rows(c_all, w_ada, b_blk):
    def body(c_ref, w_ref, b_ref, o_ref, cond_ref):
        cv = c_ref[...]
        cond = (cv * _sigmoid(cv)).astype(BF16)
        cond_ref[...] = cond.astype(F32)
        for l in range(DEPTH):
            o_ref[:, l, :] = jnp.dot(cond, w_ref[l].astype(BF16), preferred_element_type=F32) + b_ref[l:l + 1, :]

    return pl.pallas_call(
        body, name="ada_rows", in_specs=[VMEM, VMEM, VMEM], out_specs=[VMEM, VMEM],
        out_shape=[_sds((N_DEV, DEPTH, W_ADA_BLK)), _sds((N_DEV, D_MODEL))],
        compiler_params=pltpu.CompilerParams(vmem_limit_bytes=32 * MIB),
    )(c_all, w_ada, b_blk)


def _exchange_ada(part):
    def body(part_ref, out_ref, send_sems, recv_sems):
        x, y, cc = _coords()
        j = 2 * x + y
        out_ref[j] = part_ref[2 * j + cc]
        copies = []
        for k, chip in enumerate(_other_chips(x, y)):
            b_dst = 4 * chip[0] + 2 * chip[1] + cc
            copies.append(_remote(part_ref.at[b_dst], out_ref.at[j], (send_sems.at[k], recv_sems.at[k]), (*chip, cc)))
        for cp in copies:
            cp.start()
        for cp in copies:
            cp.wait()

    return pl.pallas_call(
        body, name="exchange_ada", in_specs=[VMEM], out_specs=VMEM,
        out_shape=_sds((N_CHIPS, DEPTH, W_ADA_BLK)),
        scratch_shapes=[pltpu.SemaphoreType.DMA((3,)), pltpu.SemaphoreType.DMA((3,))],
    )(part)


def _adamw_math(w, g, m, v):
    m = ADAM_B1 * m + (1.0 - ADAM_B1) * g
    v = ADAM_B2 * v + (1.0 - ADAM_B2) * (g * g)
    m_hat = m / (1.0 - ADAM_B1 ** ADAM_STEP)
    v_hat = v / (1.0 - ADAM_B2 ** ADAM_STEP)
    delta = -ADAM_LR * (m_hat / (jnp.sqrt(v_hat) + ADAM_EPS) + ADAM_WD * w)
    return delta, m, v


def _adamw_w_in(w, g0, g1, m, v, pos_chunks):
    def body(t_ref, w_ref, g0_ref, g1_ref, m_ref, v_ref, g_ref, d_ref, nm_ref, nv_ref):
        for l, src in enumerate((g0_ref, g1_ref)):
            g = src[...]
            g_ref[l] = g
            d_ref[l], nm_ref[l], nv_ref[l] = _adamw_math(w_ref[l], g, m_ref[l], v_ref[l])

    nat = pl.BlockSpec((DEPTH, CHUNK_ROWS, D_MODEL), lambda t, tbl: (0, t, 0))
    per = pl.BlockSpec((CHUNK_ROWS, D_MODEL), lambda t, tbl: (tbl[t], 0))
    return pl.pallas_call(
        body, name="adamw_w_in",
        grid_spec=pltpu.PrefetchScalarGridSpec(num_scalar_prefetch=1, grid=(N_CHUNKS,),
                                               in_specs=[nat, per, per, nat, nat], out_specs=[nat] * 4),
        out_shape=[_sds(w.shape)] * 4,
        compiler_params=pltpu.CompilerParams(dimension_semantics=("arbitrary",)),
    )(*_in_hbm(pos_chunks, w, g0, g1, m, v))


def _adamw_w_out(w, g0, g1, m, v):
    def body(w_ref, g0_ref, g1_ref, m_ref, v_ref, g_ref, d_ref, nm_ref, nv_ref):
        g = jnp.where(pl.program_id(0) == 0, g0_ref[...], g1_ref[...])
        g_ref[...] = g
        d_ref[...], nm_ref[...], nv_ref[...] = _adamw_math(w_ref[...], g, m_ref[...], v_ref[...])

    blk = pl.BlockSpec((None, W_OUT_BLK, D_MODEL), lambda l: (l, 0, 0))
    gblk = _full((W_OUT_BLK, D_MODEL))
    return pl.pallas_call(
        body, name="adamw_w_out", grid=(DEPTH,), in_specs=[blk, gblk, gblk, blk, blk], out_specs=[blk] * 4,
        out_shape=[_sds(w.shape)] * 4,
        compiler_params=pltpu.CompilerParams(dimension_semantics=("arbitrary",), vmem_limit_bytes=32 * MIB),
    )(w, g0, g1, m, v)


def _w_ada_grad_adamw(cond_t, dada, w, m, v):
    _, rows, cols = w.shape
    tr = 256

    def body(ct_ref, da_ref, w_ref, m_ref, v_ref, g_ref, d_ref, nm_ref, nv_ref):
        g = jnp.dot(ct_ref[...], da_ref[...].astype(BF16), preferred_element_type=F32)
        g_ref[...] = g
        d_ref[...], nm_ref[...], nv_ref[...] = _adamw_math(w_ref[...], g, m_ref[...], v_ref[...])

    blk = pl.BlockSpec((None, tr, cols), lambda l, t: (l, t, 0))
    return pl.pallas_call(
        body, name="w_ada_grad_adamw", grid=(DEPTH, rows // tr),
        in_specs=[pl.BlockSpec((tr, BLOCK), lambda l, t: (t, 0)), pl.BlockSpec((None, BLOCK, cols), lambda l, t: (l, 0, 0)),
                  blk, blk, blk],
        out_specs=[blk] * 4, out_shape=[_sds(w.shape)] * 4,
        compiler_params=pltpu.CompilerParams(dimension_semantics=("arbitrary", "arbitrary"), vmem_limit_bytes=32 * MIB),
    )(cond_t, dada, w, m, v)


def _small_sum_adamw(gathered_a, gathered_b, ws, rest):
    n_ws = 8 * BLOCK

    def body(*refs):
        a_refs, b_ref = refs[:DEPTH], refs[DEPTH]
        ws_ref, ms_ref, vs_ref, wr_ref, mr_ref, vr_ref = refs[DEPTH + 1:DEPTH + 7]
        gs_ref, ds_ref, nms_ref, nvs_ref, gr_ref, dr_ref, nmr_ref, nvr_ref = refs[DEPTH + 7:]

        def total(ref):
            g = ref[0]
            for b in range(1, N_DEV):
                g = g + ref[b]
            return g

        totals = [total(ref) for ref in a_refs]
        for l, t in enumerate(totals):
            g = t[0:n_ws]
            gs_ref[l] = g
            ds_ref[l], nms_ref[l], nvs_ref[l] = _adamw_math(ws_ref[l], g, ms_ref[l], vs_ref[l])
        g = jnp.concatenate([t[n_ws:] for t in totals] + [total(b_ref)], axis=0)
        gr_ref[...] = g
        dr_ref[...], nmr_ref[...], nvr_ref[...] = _adamw_math(wr_ref[...], g, mr_ref[...], vr_ref[...])

    return pl.pallas_call(
        body, name="small_sum_adamw", in_specs=[VMEM] * (DEPTH + 7), out_specs=[VMEM] * 8,
        out_shape=[_sds(ws[0].shape)] * 4 + [_sds(rest[0].shape)] * 4,
        compiler_params=pltpu.CompilerParams(vmem_limit_bytes=48 * MIB),
    )(*gathered_a, gathered_b, *ws, *rest)


_SMALL_A_REST = (("b_s", 8), ("q_gain", 1), ("k_gain", 1), ("sink", 1))
_SMALL_A = (("w_s", 8 * BLOCK),) + _SMALL_A_REST
_SMALL_B = (("b_ada", DEPTH * 24), ("norm_gain", DEPTH * 8), ("sq_err", 1))


def _pack_rows(parts, layout, layer=None):
    rows = []
    for name, n in layout:
        flat = (parts[name] if layer is None else parts[name][layer]).reshape(-1)
        rows.append(jnp.pad(flat, (0, n * 128 - flat.shape[0])).reshape(n, 128))
    n_rows = sum(n for _, n in layout)
    if n_rows % 8:
        rows.append(jnp.zeros((-n_rows % 8, 128), F32))
    return jnp.concatenate(rows, axis=0)


def _pack_rest(parts):
    return jnp.concatenate([_pack_rows(parts, _SMALL_A_REST, l) for l in range(DEPTH)] + [_pack_rows(parts, _SMALL_B)], axis=0)


def _unpack_rest(packed, shapes):
    def take(r0, layout, shape_of):
        got = {}
        for name, n in layout:
            shape = shape_of(name)
            size = 1
            for d in shape:
                size *= d
            got[name] = packed[r0:r0 + n].reshape(-1)[:size].reshape(shape)
            r0 += n
        return got, r0 + -r0 % 8

    layers, r0 = [], 0
    for _ in range(DEPTH):
        got, r0 = take(r0, _SMALL_A_REST, lambda name: shapes[name][1:])
        layers.append(got)
    out, _ = take(r0, _SMALL_B, lambda name: shapes[name])
    out.update({name: jnp.stack([layer[name] for layer in layers]) for name, _ in _SMALL_A_REST})
    return out


def _permute_heads(a, axis):
    shp = a.shape
    a = a.reshape(shp[:axis] + (2, 4, HEAD_DIM) + shp[axis + 1:])
    a = jnp.swapaxes(a, axis, axis + 1)
    return a.reshape(shp)


def _unpermute_heads(a, axis):
    shp = a.shape
    a = a.reshape(shp[:axis] + (4, 2, HEAD_DIM) + shp[axis + 1:])
    a = jnp.swapaxes(a, axis, axis + 1)
    return a.reshape(shp)


def _permute_w_out(w):
    return jnp.concatenate([_permute_heads(w[:D_ATTN], 0), w[D_ATTN:]], axis=0)


def _unpermute_w_out(w):
    return jnp.concatenate([_unpermute_heads(w[:D_ATTN], 0), w[D_ATTN:]], axis=0)


def kernel(x, c, w_ada, b_ada, norm_gain, w_in, q_gain, k_gain, sink, w_s, b_s, w_out, loss_target, m_w_ada, m_b_ada, m_norm_gain, m_w_in, m_q_gain, m_k_gain, m_sink, m_w_s, m_b_s, m_w_out, v_w_ada, v_b_ada, v_norm_gain, v_w_in, v_q_gain, v_k_gain, v_sink, v_w_s, v_b_s, v_w_out):
    ix, iy, ic = _coords()
    chip = 2 * ix + iy
    chip_idx = jnp.stack([chip, ic]).astype(jnp.int32)
    core_idx = jnp.reshape(ic, (1,)).astype(jnp.int32)
    src_chunks = lax.dynamic_index_in_dim(jnp.asarray(_CHUNK_SRC), chip, 0, keepdims=False)
    pos_chunks = lax.dynamic_index_in_dim(jnp.asarray(_CHUNK_POS), chip, 0, keepdims=False)
    x0, target = x[0], loss_target[0]

    wt, mt, vt = (jnp.swapaxes(a, 1, 2) for a in (w_in, m_w_in, v_w_in))
    w0_mine, wloc_in1 = _cast_permute_w_in(wt, jnp.concatenate([chip_idx[:1], src_chunks]))
    c_all, wloc_out, w0 = _gather_inputs(c, w_out, w0_mine)
    wts = [w0.reshape(D_IN, D_MODEL), None]

    b_blk = lax.dynamic_slice_in_dim(b_ada, chip * W_ADA_BLK, W_ADA_BLK, axis=1)
    ada_part, cond = _ada_rows(c_all.reshape(N_DEV, D_MODEL), w_ada, b_blk)
    ada = jnp.moveaxis(_exchange_ada(ada_part), 0, 1).reshape(DEPTH, 3 * D_MODEL)
    vecs = [jnp.concatenate([norm_gain[l:l + 1], ada[l:l + 1, D_MODEL:2 * D_MODEL], ada[l:l + 1, 0:D_MODEL],
                             ada[l:l + 1, 2 * D_MODEL:], jnp.zeros((4, D_MODEL), F32)], axis=0) for l in range(DEPTH)]

    qg2 = jnp.concatenate([q_gain, q_gain], axis=-1)
    kg2 = jnp.concatenate([k_gain, k_gain], axis=-1)
    ws_b = w_s.astype(BF16)
    wst_b = jnp.swapaxes(w_s, -1, -2).astype(BF16)
    bsp = jnp.repeat(jnp.swapaxes(b_s.reshape(DEPTH, 4, 2, BLOCK), -1, -2), HEAD_DIM, axis=-1)
    bias = jnp.asarray(_bias_table())

    def mix_args(l):
        return bias, sink[l], qg2[l:l + 1], kg2[l:l + 1], ws_b[l]

    w_out_shape = (W_OUT_BLK, D_MODEL)
    proj0, wo0, wo1 = _proj_fwd(x0, vecs[0], wts[0], "proj_fwd_0",
                                jobs=[_job_gather([(wloc_out, 0), (wloc_out, 1)], [w_out_shape, w_out_shape])])
    y0, *kept0, w1 = _mix_fwd(proj0, *mix_args(0), bsp[0], "mix_fwd_0",
                              jobs=[_job_gather([(wloc_in1, None)], [(W_IN_BLK, D_MODEL)])])
    wts[1] = w1.reshape(D_IN, D_MODEL)
    wos = [_permute_w_out(w.reshape(D_MODEL, D_MODEL)) for w in (wo0, wo1)]
    x1, proj1 = _out_proj_fwd(y0, x0, vecs[0], wos[0], vecs[1], wts[1], "out_proj_fwd_01")
    y1, *kept1 = _mix_fwd(proj1, *mix_args(1), bsp[1], "mix_fwd_1")

    def blocks_out(gw):
        return _unpermute_w_out(gw).reshape(N_CHIPS, W_OUT_BLK, D_MODEL)

    def small_pack(dws, dbsp, dqg, dkg, dsink):
        b_s = jnp.swapaxes(dbsp[:, :, ::HEAD_DIM], -1, -2).reshape(8, BLOCK)
        return _pack_rows(dict(w_s=dws, b_s=b_s, q_gain=dqg[0, :HEAD_DIM], k_gain=dkg[0, :HEAD_DIM], sink=dsink[0, :N_HEADS]),
                          _SMALL_A)

    dx2, sq, dy1, gwo1, dgate1 = _out_loss_bwd(y1, x1, vecs[1], wos[1], target, "out_loss_bwd_1")
    go1 = blocks_out(gwo1)
    dpb, dkv, p0, p2, dqg1, dkg1, dsink1, dws1, dbsp1, ro1 = _mix_bwd(
        dy1, proj1, *kept1, qg2[1:2], kg2[1:2], wst_b[1], "mix_bwd_1", jobs=[_job_swap(go1)])
    po1, = _pair_sum(go1, ro1, core_idx, "pair_sum_w_out_1")
    gwi1, dkvb, co1, gathered_a1 = _w_in_grad(
        dpb, dkv, p0, p2, x1, vecs[1], "w_in_grad_1",
        jobs=[_job_scatter(po1), _job_all_gather(small_pack(dws1, dbsp1, dqg1, dkg1, dsink1))])
    gi1 = gwi1.reshape(N_CHIPS, W_IN_BLK, D_MODEL)
    fo1 = _chip_sum(po1, co1, chip_idx, "chip_sum_w_out_1")
    dx1, dsh1, dsc1, dng1, dy0, gwo0, dgate0, grad_wo1, ri1 = _proj_bwd(
        dpb, dkvb, x1, dx2, vecs[1], wts[1], "proj_out_bwd_10", below=(y0, vecs[0], wos[0]),
        jobs=[_job_join(fo1), _job_swap(gi1)])
    pi1, = _pair_sum(gi1, ri1, core_idx, "pair_sum_w_in_1")
    go0 = blocks_out(gwo0)
    dpb, dkv, p0, p2, dqg0, dkg0, dsink0, dws0, dbsp0, ci1, ro0 = _mix_bwd(
        dy0, proj0, *kept0, qg2[0:1], kg2[0:1], wst_b[0], "mix_bwd_0", jobs=[_job_scatter(pi1), _job_swap(go0)])
    fi1 = _chip_sum(pi1, ci1, chip_idx, "chip_sum_w_in_1")
    po0, = _pair_sum(go0, ro0, core_idx, "pair_sum_w_out_0")

    gwi0, dkvb, grad_wi1, co0, gathered_a0 = _w_in_grad(
        dpb, dkv, p0, p2, x0, vecs[0], "w_in_grad_0",
        jobs=[_job_join(fi1), _job_scatter(po0), _job_all_gather(small_pack(dws0, dbsp0, dqg0, dkg0, dsink0))])
    gi0 = gwi0.reshape(N_CHIPS, W_IN_BLK, D_MODEL)
    fo0 = _chip_sum(po0, co0, chip_idx, "chip_sum_w_out_0")

    ri0, grad_wo0 = _comm([_job_swap(gi0), _job_join(fo0)], "swap_w_in_0")
    pi0, pi0_send = _pair_sum(gi0, ri0, core_idx, "pair_sum_w_in_0", send_dtype=BF16)
    dx0, dsh0, dsc0, dng0, ci0 = _proj_bwd(dpb, dkvb, x0, dx1, vecs[0], wts[0], "proj_bwd_0",
                                           jobs=[_job_scatter(pi0_send)])
    fi0 = _chip_sum(pi0, ci0, chip_idx, "chip_sum_w_in_0")

    small_g = dict(
        b_ada=jnp.stack([jnp.concatenate([dsh0, dsc0, dgate0], axis=-1)[0], jnp.concatenate([dsh1, dsc1, dgate1], axis=-1)[0]]),
        norm_gain=jnp.stack([dng0[0], dng1[0]]), sq_err=sq[0])
    none = jnp.zeros((1,), F32)
    small_w = dict(w_s=w_s, b_s=b_s, b_ada=b_ada, norm_gain=norm_gain, q_gain=q_gain, k_gain=k_gain, sink=sink, sq_err=none)
    small_m = dict(w_s=m_w_s, b_s=m_b_s, b_ada=m_b_ada, norm_gain=m_norm_gain, q_gain=m_q_gain, k_gain=m_k_gain, sink=m_sink,
                   sq_err=none)
    small_v = dict(w_s=v_w_s, b_s=v_b_s, b_ada=v_b_ada, norm_gain=v_norm_gain, q_gain=v_q_gain, k_gain=v_k_gain, sink=v_sink,
                   sq_err=none)
    grad_wi0, gathered_b = _comm([_job_join(fi0), _job_all_gather(_pack_rows(small_g, _SMALL_B))], "join_w_in_0")
    packed = _small_sum_adamw(
        [a.reshape(N_DEV, -1, 128) for a in (gathered_a0, gathered_a1)], gathered_b.reshape(N_DEV, -1, 128),
        [a.reshape(DEPTH, 8 * BLOCK, BLOCK) for a in (w_s, m_w_s, v_w_s)], [_pack_rest(p) for p in (small_w, small_m, small_v)])
    shapes = {k: a.shape for k, a in small_w.items()}
    sg, sd, sm, sv = (dict(_unpack_rest(rest, shapes), w_s=ws.reshape(w_s.shape)) for ws, rest in zip(packed[:4], packed[4:]))
    loss = 0.5 * sg["sq_err"][0]

    dada_all = gathered_b.reshape(N_DEV, -1, 128)[:, 0:DEPTH * 24].reshape(N_DEV, DEPTH, 3 * D_MODEL)
    dada_blk = jnp.moveaxis(lax.dynamic_slice_in_dim(dada_all, chip * W_ADA_BLK, W_ADA_BLK, axis=2), 0, 1)
    pad = BLOCK - N_DEV
    ada_out = _w_ada_grad_adamw(
        jnp.pad(cond.T, ((0, 0), (0, pad))).astype(BF16), jnp.pad(dada_blk, ((0, 0), (0, pad), (0, 0))),
        w_ada, m_w_ada, v_w_ada)

    in_out = [jnp.swapaxes(a, 1, 2) for a in _adamw_w_in(wt, grad_wi0, grad_wi1, mt, vt, pos_chunks)]
    out_out = _adamw_w_out(w_out, grad_wo0, grad_wo1, m_w_out, v_w_out)

    def ordered(k):
        small = (sg, sd, sm, sv)[k]
        return (ada_out[k], small["b_ada"], small["norm_gain"], in_out[k], small["q_gain"], small["k_gain"], small["sink"],
                small["w_s"], small["b_s"], out_out[k])

    return (loss, dx0[None], *ordered(0), *ordered(1), *ordered(2), *ordered(3))
```

```python
import numpy as np

import jax
import jax.numpy as jnp
from jax import lax
from jax.experimental import pallas as pl
from jax.experimental.pallas import tpu as pltpu

F32 = jnp.float32
BF16 = jnp.bfloat16

D_MODEL = 1024
DEPTH = 2
HEAD_DIM = 64
N_HEADS = 8
BLOCK = 128
SUB = 4
TILE = SUB * BLOCK
STACK = 8 * BLOCK
D_ATTN = 512
D_KV = 128
D_IN = 2816
N_CHIPS = 4
N_DEV = 8
W_IN_BLK = D_IN // N_CHIPS
W_OUT_BLK = D_MODEL // N_CHIPS
W_ADA_BLK = 3 * D_MODEL // N_CHIPS
CHUNK_ROWS = HEAD_DIM
N_CHUNKS = W_IN_BLK // CHUNK_ROWS
EPS = 1e-6
NEG_INF = -1e30

C_Q, C_K, C_V, C_GA, C_U, C_VG, C_GG = 0, 512, 640, 768, 1280, 1792, 2304

ADAM_LR = 0.001
ADAM_B1 = 0.9
ADAM_B2 = 0.999
ADAM_EPS = 1e-08
ADAM_WD = 0.01
ADAM_STEP = 10

MESH = pl.DeviceIdType.MESH
MIB = 1024 * 1024
ANY = pl.BlockSpec(memory_space=pl.ANY)
VMEM = pl.BlockSpec(memory_space=pltpu.VMEM)

NT_DIMS = (((1,), (1,)), ((), ()))
TN_DIMS = (((0,), (0,)), ((), ()))

_PAIR_ORDER = (0, 4, 1, 5, 2, 6, 3, 7)
_CHUNK_SRC = np.array([
    list(_PAIR_ORDER) + [8, 9, 10],
    [0] + [1 + h for h in _PAIR_ORDER] + [9, 10],
    list(range(N_CHUNKS)),
    list(range(N_CHUNKS)),
], np.int32)
_CHUNK_POS = np.argsort(_CHUNK_SRC, axis=1).astype(np.int32)


def _bias_table():
    i = np.arange(N_HEADS * BLOCK)[:, None]
    j = np.arange(3 * BLOCK)[None, :]
    dist = np.abs(j - BLOCK - (i % BLOCK))
    slope = 2.0 ** -(i // BLOCK + 1.0)
    inner = np.where(dist <= BLOCK, -(slope * dist), NEG_INF)
    first = np.where(j >= BLOCK, inner, NEG_INF)
    last = np.where(j < 2 * BLOCK, inner, NEG_INF)
    return np.stack([first, inner, last]).astype(np.float32)


def _full(shape):
    n = len(shape)
    return pl.BlockSpec(shape, lambda *_: (0,) * n)


def _sds(shape, dtype=F32):
    return jax.ShapeDtypeStruct(shape, dtype)


def _coords():
    return lax.axis_index("x"), lax.axis_index("y"), lax.axis_index("c")


def _other_chips(x, y):
    return [(1 - x, y), (x, 1 - y), (1 - x, 1 - y)]


def _in_hbm(*operands):
    return [pltpu.with_memory_space_constraint(a, pltpu.HBM) if a.size * a.dtype.itemsize >= MIB // 4 else a
            for a in operands]


def _remote(src, dst, sems, dev):
    return pltpu.make_async_remote_copy(src_ref=src, dst_ref=dst, send_sem=sems[0], recv_sem=sems[1],
                                        device_id=dev, device_id_type=MESH)


class _Job:
    def __init__(self, inputs, out_shapes, n_remote, n_local, make, then=None, in_place=False):
        self.inputs, self.out_shapes, self.n_remote, self.n_local, self.make = inputs, out_shapes, n_remote, n_local, make
        self.then = then
        self.in_place = in_place


def _job_aliases(jobs, in_base, out_base):
    aliases, a, b = {}, 0, 0
    for j in jobs:
        if j.in_place:
            aliases.update({in_base + a + k: out_base + b + k for k in range(len(j.inputs))})
        a, b = a + len(j.inputs), b + len(j.out_shapes)
    return aliases


def _job_copies(jobs, jin, jout, sems, second=False):
    send, recv, loc = sems
    res, a, b, r, l = [], 0, 0, 0, 0
    for j in jobs:
        build = j.then if second else j.make
        if build is not None:
            res += build(jin[a:a + len(j.inputs)], jout[b:b + len(j.out_shapes)],
                         lambda k, r=r: (send.at[r + k], recv.at[r + k]), lambda k, l=l: loc.at[l + k])
        a, b, r, l = a + len(j.inputs), b + len(j.out_shapes), r + j.n_remote, l + j.n_local
    return res


def _run(copies):
    for cp in copies:
        cp.start()
    for cp in copies:
        cp.wait()


def _job_gather(sources, shapes):
    n = len(sources)

    def make(ins, outs, rsem, lsem):
        x, y, c = _coords()
        j = 2 * x + y
        res = []
        for t, ((_, layer), src, dst) in enumerate(zip(sources, ins, outs)):
            src = src if layer is None else src.at[layer]
            res.append(pltpu.make_async_copy(src, dst.at[j], lsem(t)))
            for k, chip in enumerate(_other_chips(x, y)):
                res.append(_remote(src, dst.at[j], rsem(3 * t + k), (*chip, c)))
        return res

    return _Job([a for a, _ in sources], [_sds((N_CHIPS,) + s, BF16) for s in shapes], 3 * n, n, make)


def _job_swap(g):
    _, rows, cols = g.shape
    half = rows // 2

    def make(ins, outs, rsem, lsem):
        x, y, c = _coords()
        return [_remote(ins[0].at[:, pl.ds((1 - c) * half, half), :], outs[0], rsem(0), (x, y, 1 - c))]

    return _Job([g], [_sds((N_CHIPS, half, cols))], 1, 0, make)


def _job_scatter(p):
    def make(ins, outs, rsem, lsem):
        x, y, c = _coords()
        return [_remote(ins[0].at[2 * chip[0] + chip[1]], outs[0].at[k], rsem(k), (*chip, c))
                for k, chip in enumerate(_other_chips(x, y))]

    return _Job([p], [_sds((3,) + p.shape[1:], p.dtype)], 3, 0, make)


def _job_all_gather(blk):
    m_per = blk.shape[0]

    def rows(ref, px, py, pc):
        return ref.at[pl.ds((4 * px + 2 * py + pc) * m_per, m_per), :]

    def make(ins, outs, rsem, lsem):
        x, y, c = _coords()
        res = [pltpu.make_async_copy(ins[0], rows(outs[0], x, y, c), lsem(0)),
               _remote(ins[0], rows(outs[0], x, y, c), rsem(0), (x, y, 1 - c))]
        res += [_remote(ins[0], rows(outs[0], x, y, c), rsem(1 + k), (*chip, c)) for k, chip in enumerate(_other_chips(x, y))]
        return res

    def then(ins, outs, rsem, lsem):
        x, y, c = _coords()
        return [_remote(rows(outs[0], *chip, c), rows(outs[0], *chip, c), rsem(4 + k), (x, y, 1 - c))
                for k, chip in enumerate(_other_chips(x, y))]

    return _Job([blk], [_sds((N_DEV * m_per, blk.shape[1]), blk.dtype)], 7, 1, make, then)


def _job_join(f):
    half = f.shape[0] // 2

    def make(ins, outs, rsem, lsem):
        x, y, c = _coords()
        mine = pl.ds(c * half, half)
        return [_remote(ins[0].at[mine, :], outs[0].at[mine, :], rsem(0), (x, y, 1 - c))]

    return _Job([f], [_sds(f.shape, f.dtype)], 1, 0, make, in_place=True)


def _pallas(body, *, name, grid, in_specs, out_specs, out_shape, operands, vmem_mib, jobs=()):
    in_specs, out_specs, out_shape = list(in_specs), list(out_specs), list(out_shape)
    n_in, n_out = len(in_specs), len(out_specs)
    j_in = [a for j in jobs for a in j.inputs]
    j_out = [s for j in jobs for s in j.out_shapes]
    n_rem = max(1, sum(j.n_remote for j in jobs))
    n_loc = max(1, sum(j.n_local for j in jobs))
    scratch = [pltpu.SemaphoreType.DMA((n_rem,)), pltpu.SemaphoreType.DMA((n_rem,)),
               pltpu.SemaphoreType.DMA((n_loc,))] if jobs else []

    def wrapped(*refs):
        ins = refs[:n_in]
        jin = refs[n_in:n_in + len(j_in)]
        outs = refs[n_in + len(j_in):n_in + len(j_in) + n_out]
        jout = refs[n_in + len(j_in) + n_out:n_in + len(j_in) + n_out + len(j_out)]

        if jobs:
            first = last = None
            for d, n in enumerate(grid):
                f, e = pl.program_id(d) == 0, pl.program_id(d) == n - 1
                first, last = (f, e) if first is None else (first & f, last & e)

            @pl.when(first)
            def _():
                for cp in _job_copies(jobs, jin, jout, refs[-3:]):
                    cp.start()

        body(*ins, *outs)

        if jobs:
            @pl.when(last)
            def _():
                for cp in _job_copies(jobs, jin, jout, refs[-3:]):
                    cp.wait()
                _run(_job_copies(jobs, jin, jout, refs[-3:], second=True))

    return pl.pallas_call(
        wrapped, name=name, grid=grid,
        in_specs=in_specs + [ANY] * len(j_in), out_specs=out_specs + [ANY] * len(j_out),
        out_shape=out_shape + j_out, scratch_shapes=scratch, input_output_aliases=_job_aliases(jobs, n_in, n_out),
        compiler_params=pltpu.CompilerParams(dimension_semantics=("arbitrary",) * len(grid),
                                             vmem_limit_bytes=vmem_mib * MIB),
    )(*_in_hbm(*operands, *j_in))


def _comm(jobs, name):
    j_in = [a for j in jobs for a in j.inputs]
    j_out = [s for j in jobs for s in j.out_shapes]
    n_rem = max(1, sum(j.n_remote for j in jobs))
    n_loc = max(1, sum(j.n_local for j in jobs))

    def body(*refs):
        jin, jout = refs[:len(j_in)], refs[len(j_in):len(j_in) + len(j_out)]
        _run(_job_copies(jobs, jin, jout, refs[-3:]))
        _run(_job_copies(jobs, jin, jout, refs[-3:], second=True))

    return pl.pallas_call(
        body, name=name, in_specs=[ANY] * len(j_in), out_specs=[ANY] * len(j_out), out_shape=j_out,
        scratch_shapes=[pltpu.SemaphoreType.DMA((n_rem,)), pltpu.SemaphoreType.DMA((n_rem,)),
                        pltpu.SemaphoreType.DMA((n_loc,))],
        input_output_aliases=_job_aliases(jobs, 0, 0),
    )(*_in_hbm(*j_in))


def _sigmoid(x):
    return 1.0 / (1.0 + jnp.exp(-x))


def _lo_mask(shape):
    return lax.broadcasted_iota(jnp.int32, shape, len(shape) - 1) < HEAD_DIM


def _half_sum(x, lo):
    a = jnp.sum(jnp.where(lo, x, 0.0), axis=-1, keepdims=True)
    b = jnp.sum(jnp.where(lo, 0.0, x), axis=-1, keepdims=True)
    return jnp.where(lo, a, b)


def _half_rms_scale(x, lo):
    return lax.rsqrt(_half_sum(x * x, lo) * (1.0 / HEAD_DIM) + EPS)


def _stack_heads(pairs, lo):
    return jnp.concatenate([jnp.where(lo, t, 0.0) for t in pairs] + [jnp.where(lo, 0.0, t) for t in pairs], axis=0)


def _unstack_pair(stack, p, lo):
    return jnp.where(lo, stack[BLOCK * p:BLOCK * (p + 1)], stack[BLOCK * (4 + p):BLOCK * (5 + p)])


def _attention_probs(q_stack, kn, bias_ref, sink_ref):
    rows = N_HEADS * BLOCK
    s = lax.dot_general(q_stack, kn, NT_DIMS, preferred_element_type=F32) + bias_ref[...]
    sink = jnp.concatenate([jnp.full((BLOCK, BLOCK), sink_ref[h], F32) for h in range(N_HEADS)], axis=0)
    cols = [s[:, BLOCK * j:BLOCK * (j + 1)] for j in range(3)]
    top = jnp.max(jnp.maximum(jnp.maximum(cols[0], cols[1]), cols[2]), axis=-1, keepdims=True)
    m = jnp.maximum(jnp.broadcast_to(top, (rows, BLOCK)), sink)
    e = [jnp.exp(c - m) for c in cols]
    es = jnp.exp(sink - m)
    inv = 1.0 / (jnp.broadcast_to(jnp.sum((e[0] + e[1]) + e[2], axis=-1, keepdims=True), (rows, BLOCK)) + es)
    return jnp.concatenate([c * inv for c in e], axis=1), es * inv


def _kv_rows(cur_ref, pkv_ref, nkv_ref):
    k = jnp.concatenate([pkv_ref[:, 0:D_KV], cur_ref[:, C_K:C_K + D_KV], nkv_ref[:, 0:D_KV]], axis=0)
    v = jnp.concatenate([pkv_ref[:, D_KV:2 * D_KV], cur_ref[:, C_V:C_V + D_KV], nkv_ref[:, D_KV:2 * D_KV]], axis=0)
    return k, v


def _overlap_add(parts):
    blocks = []
    for j in range(SUB + 2):
        terms = [parts[b][BLOCK * (j - b):BLOCK * (j - b + 1)] for b in range(SUB) if 0 <= j - b <= 2]
        total = terms[0]
        for t in terms[1:]:
            total = total + t
        blocks.append(total)
    return jnp.concatenate(blocks, axis=0)


def _mix_specs(nt):
    cur = pl.BlockSpec((TILE, D_IN), lambda i: (i, 0))
    kv_col = C_K // (2 * D_KV)
    pkv = pl.BlockSpec((BLOCK, 2 * D_KV), lambda i: (jnp.maximum(i * SUB - 1, 0), kv_col))
    nkv = pl.BlockSpec((BLOCK, 2 * D_KV), lambda i: (jnp.minimum((i + 1) * SUB, nt * SUB - 1), kv_col))
    table = (None, N_HEADS * BLOCK, 3 * BLOCK)
    first = pl.BlockSpec(table, lambda i: (jnp.where(i == 0, 0, 1), 0, 0))
    inner = pl.BlockSpec(table, lambda i: (1, 0, 0))
    last = pl.BlockSpec(table, lambda i: (jnp.where(i == nt - 1, 2, 1), 0, 0))
    return cur, pkv, nkv, [first] + [inner] * (SUB - 2) + [last]


V_NG, V_SCALE, V_SHIFT, V_GATE = 0, 1, 2, 3
VECS = _full((8, D_MODEL))


def _row(v_ref, k):
    return v_ref[k:k + 1, :]


def _adaln(xv, v_ref):
    r = lax.rsqrt(jnp.mean(xv * xv, axis=-1, keepdims=True) + EPS)
    return ((xv * r) * _row(v_ref, V_NG)) * (1.0 + _row(v_ref, V_SCALE)) + _row(v_ref, V_SHIFT)


def _proj_fwd(x, vecs, wt, name, jobs=()):
    s = x.shape[0]
    ts = min(512, s)

    def body(x_ref, v_ref, w_ref, o_ref):
        h = _adaln(x_ref[...], v_ref)
        o_ref[...] = lax.dot_general(h.astype(BF16), w_ref[...], NT_DIMS, preferred_element_type=F32)

    return _pallas(
        body, name=name, grid=(s // ts,),
        in_specs=[pl.BlockSpec((ts, D_MODEL), lambda i: (i, 0)), VECS, _full((D_IN, D_MODEL))],
        out_specs=[pl.BlockSpec((ts, D_IN), lambda i: (i, 0))], out_shape=[_sds((s, D_IN))],
        operands=(x, vecs, wt), vmem_mib=48, jobs=jobs)


def _diagonal():
    return lax.broadcasted_iota(jnp.int32, (BLOCK, BLOCK), 0) == lax.broadcasted_iota(jnp.int32, (BLOCK, BLOCK), 1)


def _column_as_row(wide, eye):
    return jnp.sum(jnp.where(eye, wide, 0.0), axis=0, keepdims=True)


def _mix_fwd(proj, bias, sink, qg2, kg2, ws, bsp, name, jobs=()):
    s = proj.shape[0]
    nt = s // TILE

    def body(sink_ref, cur_ref, pkv_ref, nkv_ref, *rest):
        bias_refs = rest[:SUB]
        qg_ref, kg_ref, ws_ref, bsp_ref, y_ref, p_ref, ps_ref, attn_ref, sv_ref = rest[SUB:]
        lo = _lo_mask((BLOCK, BLOCK))
        eye = _diagonal()
        lo_kv = _lo_mask((TILE + 2 * BLOCK, BLOCK))
        k_all, v_all = _kv_rows(cur_ref, pkv_ref, nkv_ref)
        kn_all = ((k_all * _half_rms_scale(k_all, lo_kv)) * kg_ref[...]).astype(BF16)
        vb_all = v_all.astype(BF16)
        for b in range(SUB):
            rows = slice(BLOCK * b, BLOCK * (b + 1))
            window = slice(BLOCK * b, BLOCK * (b + 3))
            qn = []
            for p in range(4):
                q = cur_ref[rows, C_Q + BLOCK * p:C_Q + BLOCK * (p + 1)]
                qn.append(((q * _half_rms_scale(q, lo)) * qg_ref[...]) * 0.125)
            q_stack = _stack_heads(qn, lo).astype(BF16)
            prob, psink = _attention_probs(q_stack, kn_all[window], bias_refs[b], sink_ref)
            pb = prob.astype(BF16)
            p_ref[STACK * b:STACK * (b + 1), :] = pb
            ps_ref[N_HEADS * b:N_HEADS * (b + 1), :] = jnp.concatenate(
                [_column_as_row(psink[BLOCK * h:BLOCK * (h + 1)], eye) for h in range(N_HEADS)], axis=0)
            o_stack = jnp.dot(pb, vb_all[window], preferred_element_type=F32)
            for p in range(4):
                g = cur_ref[rows, C_GA + BLOCK * p:C_GA + BLOCK * (p + 1)]
                attn = _unstack_pair(o_stack, p, lo)
                attn_ref[rows, BLOCK * p:BLOCK * (p + 1)] = attn.astype(BF16)
                y_ref[rows, BLOCK * p:BLOCK * (p + 1)] = (attn * (g * _sigmoid(g))).astype(BF16)

        for p in range(4):
            cols = slice(C_VG + BLOCK * p, C_VG + BLOCK * (p + 1))
            vn = []
            for b in range(SUB):
                vg = cur_ref[BLOCK * b:BLOCK * (b + 1), cols]
                vn.append((vg * _half_rms_scale(vg, lo)).astype(BF16))
            vn = jnp.concatenate(vn, axis=1)
            sv_a = jnp.dot(ws_ref[2 * p], vn, preferred_element_type=F32)
            sv_b = jnp.dot(ws_ref[2 * p + 1], vn, preferred_element_type=F32)
            for b in range(SUB):
                rows = slice(BLOCK * b, BLOCK * (b + 1))
                lanes = slice(BLOCK * b, BLOCK * (b + 1))
                sv = jnp.where(lo, sv_a[:, lanes], sv_b[:, lanes]) + bsp_ref[p]
                sv_ref[rows, BLOCK * p:BLOCK * (p + 1)] = sv.astype(BF16)
                u = cur_ref[rows, C_U + BLOCK * p:C_U + BLOCK * (p + 1)]
                g = cur_ref[rows, C_GG + BLOCK * p:C_GG + BLOCK * (p + 1)]
                y_ref[rows, D_ATTN + BLOCK * p:D_ATTN + BLOCK * (p + 1)] = ((u * sv) * (g * _sigmoid(g))).astype(BF16)

    cur, pkv, nkv, bias_specs = _mix_specs(nt)
    nb = nt * SUB
    half = pl.BlockSpec((TILE, D_ATTN), lambda i: (i, 0))
    return _pallas(
        body, name=name, grid=(nt,),
        in_specs=[pl.BlockSpec(memory_space=pltpu.SMEM), cur, pkv, nkv, *bias_specs, _full((1, BLOCK)), _full((1, BLOCK)),
                  _full((8, BLOCK, BLOCK)), _full((4, BLOCK, BLOCK))],
        out_specs=[pl.BlockSpec((TILE, D_MODEL), lambda i: (i, 0)), pl.BlockSpec((SUB * STACK, 3 * BLOCK), lambda i: (i, 0)),
                   pl.BlockSpec((SUB * N_HEADS, BLOCK), lambda i: (i, 0)), half, half],
        out_shape=[_sds((s, D_MODEL), BF16), _sds((nb * STACK, 3 * BLOCK), BF16), _sds((nb * N_HEADS, BLOCK)),
                   _sds((s, D_ATTN), BF16), _sds((s, D_ATTN), BF16)],
        operands=(sink, proj, proj, proj, *([bias] * SUB), qg2, kg2, ws, bsp), vmem_mib=56, jobs=jobs)


def _out_proj_fwd(y, x, vecs, w_out, vecs_next, wt, name):
    s = x.shape[0]
    ts = min(512, s)

    def body(y_ref, x_ref, v_ref, w_ref, vn_ref, wt_ref, xn_ref, p_ref):
        xv = x_ref[...] + _row(v_ref, V_GATE) * jnp.dot(y_ref[...], w_ref[...], preferred_element_type=F32)
        xn_ref[...] = xv
        p_ref[...] = lax.dot_general(_adaln(xv, vn_ref).astype(BF16), wt_ref[...], NT_DIMS, preferred_element_type=F32)

    row = pl.BlockSpec((ts, D_MODEL), lambda i: (i, 0))
    return _pallas(
        body, name=name, grid=(s // ts,),
        in_specs=[row, row, VECS, _full((D_MODEL, D_MODEL)), VECS, _full((D_IN, D_MODEL))],
        out_specs=[row, pl.BlockSpec((ts, D_IN), lambda i: (i, 0))], out_shape=[_sds((s, D_MODEL)), _sds((s, D_IN))],
        operands=(y, x, vecs, w_out, vecs_next, wt), vmem_mib=56)


def _out_loss_bwd(y, x, vecs, w_out, target, name):
    s = x.shape[0]
    ts = min(512, s)
    steps = s // ts

    def body(y_ref, x_ref, v_ref, w_ref, t_ref, dx_ref, sq_ref, dy_ref, gw_ref, dg_ref):
        @pl.when(pl.program_id(0) == 0)
        def _():
            sq_ref[...] = jnp.zeros_like(sq_ref)
            gw_ref[...] = jnp.zeros_like(gw_ref)

        gate = _row(v_ref, V_GATE)
        yv = y_ref[...]
        out = x_ref[...] + gate * jnp.dot(yv, w_ref[...], preferred_element_type=F32)
        diff = out - t_ref[...]
        dx = diff * (1.0 / D_MODEL)
        dx_ref[...] = dx
        per_token = jnp.sum(diff * diff, axis=-1, keepdims=True) * (1.0 / D_MODEL)
        sq_ref[...] += jnp.sum(per_token, axis=0, keepdims=True)
        dy_ref[...] = lax.dot_general((dx * gate).astype(BF16), w_ref[...], NT_DIMS, preferred_element_type=F32)
        gw_ref[...] += lax.dot_general(yv, dx.astype(BF16), TN_DIMS, preferred_element_type=F32)

        @pl.when(pl.program_id(0) == steps - 1)
        def _():
            m = gw_ref[...]
            dg_ref[...] = jnp.sum(w_ref[...].astype(F32) * m, axis=0, keepdims=True)
            gw_ref[...] = m * _row(v_ref, V_GATE)

    row = pl.BlockSpec((ts, D_MODEL), lambda i: (i, 0))
    return _pallas(
        body, name=name, grid=(s // ts,), in_specs=[row, row, VECS, _full((D_MODEL, D_MODEL)), row],
        out_specs=[row, _full((1, 1)), row, _full((D_MODEL, D_MODEL)), _full((1, D_MODEL))],
        out_shape=[_sds((s, D_MODEL)), _sds((1, 1)), _sds((s, D_MODEL)), _sds((D_MODEL, D_MODEL)), _sds((1, D_MODEL))],
        operands=(y, x, vecs, w_out, target), vmem_mib=48)


def _mix_bwd(dy, proj, probs, psink, attn, sv, qg2, kg2, wst, name, jobs=()):
    s = proj.shape[0]
    nt = s // TILE

    def body(dy_ref, cur_ref, pkv_ref, nkv_ref, p_ref, ps_ref, attn_ref, sv_ref, qg_ref, kg_ref, wst_ref,
             dpb_ref, dkv_ref, p0_ref, p2_ref, dqg_ref, dkg_ref, dsink_ref, dws_ref, dbsp_ref):
        def put(rows, col, value):
            dpb_ref[rows, col:col + BLOCK] = value.astype(BF16)

        @pl.when(pl.program_id(0) == 0)
        def _():
            dqg_ref[...] = jnp.zeros_like(dqg_ref)
            dkg_ref[...] = jnp.zeros_like(dkg_ref)
            dsink_ref[...] = jnp.zeros_like(dsink_ref)
            dws_ref[...] = jnp.zeros_like(dws_ref)
            dbsp_ref[...] = jnp.zeros_like(dbsp_ref)

        lo = _lo_mask((BLOCK, BLOCK))
        lo_kv = _lo_mask((TILE + 2 * BLOCK, BLOCK))
        eye = _diagonal()
        lane_row = lax.broadcasted_iota(jnp.int32, (1, BLOCK), 1)
        qg = qg_ref[...]
        kg = kg_ref[...]

        k_all, v_all = _kv_rows(cur_ref, pkv_ref, nkv_ref)
        rk = _half_rms_scale(k_all, lo_kv)
        khat = k_all * rk
        kn_all = (khat * kg).astype(BF16)
        vb_all = v_all.astype(BF16)

        dkn_parts, dv_parts = [], []
        dsink = jnp.zeros((1, BLOCK), F32)
        dqg = jnp.zeros((1, BLOCK), F32)
        for b in range(SUB):
            rows = slice(BLOCK * b, BLOCK * (b + 1))
            window = slice(BLOCK * b, BLOCK * (b + 3))
            kn, vb = kn_all[window], vb_all[window]

            qhat, rq = [], []
            for p in range(4):
                q = cur_ref[rows, C_Q + BLOCK * p:C_Q + BLOCK * (p + 1)]
                r = _half_rms_scale(q, lo)
                rq.append(r)
                qhat.append(q * r)
            q_stack = _stack_heads([(qh * qg) * 0.125 for qh in qhat], lo).astype(BF16)
            pb = p_ref[STACK * b:STACK * (b + 1), :]
            prob = pb.astype(F32)

            dout = []
            for p in range(4):
                g = cur_ref[rows, C_GA + BLOCK * p:C_GA + BLOCK * (p + 1)]
                sg = _sigmoid(g)
                dya = dy_ref[rows, BLOCK * p:BLOCK * (p + 1)]
                attn = attn_ref[rows, BLOCK * p:BLOCK * (p + 1)]
                put(rows, C_GA + BLOCK * p, dya * attn * (sg * (1.0 + g * (1.0 - sg))))
                dout.append(dya * (g * sg))
            do_stack = _stack_heads(dout, lo).astype(BF16)
            dp = lax.dot_general(do_stack, vb, NT_DIMS, preferred_element_type=F32)
            delta = jnp.sum(prob * dp, axis=-1, keepdims=True)
            dsb = (prob * (dp - delta)).astype(BF16)

            for h in range(N_HEADS):
                delta_row = _column_as_row(jnp.broadcast_to(delta[BLOCK * h:BLOCK * (h + 1)], (BLOCK, BLOCK)), eye)
                tot = jnp.sum(ps_ref[N_HEADS * b + h:N_HEADS * b + h + 1, :] * delta_row, axis=-1, keepdims=True)
                dsink = dsink - jnp.where(lane_row == h, tot, 0.0)

            dq_stack = jnp.dot(dsb, kn, preferred_element_type=F32) * 0.125
            dkn_parts.append(lax.dot_general(dsb, q_stack, TN_DIMS, preferred_element_type=F32))
            dv_parts.append(lax.dot_general(pb, do_stack, TN_DIMS, preferred_element_type=F32))

            for p in range(4):
                dqn = _unstack_pair(dq_stack, p, lo)
                qh = qhat[p]
                dqg = dqg + jnp.sum(dqn * qh, axis=0, keepdims=True)
                dqh = dqn * qg
                mean = _half_sum(dqh * qh, lo) * (1.0 / HEAD_DIM)
                put(rows, C_Q + BLOCK * p, rq[p] * (dqh - qh * mean))

        dws_new, dbs_new = [], []
        for p in range(4):
            rs, vnfs, vns, dsvs, dbs = [], [], [], [], None
            for b in range(SUB):
                rows = slice(BLOCK * b, BLOCK * (b + 1))
                vg = cur_ref[rows, C_VG + BLOCK * p:C_VG + BLOCK * (p + 1)]
                r = _half_rms_scale(vg, lo)
                vnf = vg * r
                sv = sv_ref[rows, BLOCK * p:BLOCK * (p + 1)]
                u = cur_ref[rows, C_U + BLOCK * p:C_U + BLOCK * (p + 1)]
                g = cur_ref[rows, C_GG + BLOCK * p:C_GG + BLOCK * (p + 1)]
                sg = _sigmoid(g)
                dym = dy_ref[rows, D_ATTN + BLOCK * p:D_ATTN + BLOCK * (p + 1)]
                put(rows, C_GG + BLOCK * p, dym * (u * sv) * (sg * (1.0 + g * (1.0 - sg))))
                dgm = dym * (g * sg)
                put(rows, C_U + BLOCK * p, dgm * sv)
                dsv = dgm * u
                term = jnp.where(lo, jnp.sum(jnp.where(lo, dsv, 0.0), axis=-1, keepdims=True),
                                 jnp.sum(jnp.where(lo, 0.0, dsv), axis=-1, keepdims=True))
                dbs = term if dbs is None else dbs + term
                rs.append(r)
                vnfs.append(vnf)
                vns.append(vnf.astype(BF16))
                dsvs.append(dsv)
            dbs_new.append(dbs)
            vn = jnp.concatenate(vns, axis=1)
            dsv = jnp.concatenate(dsvs, axis=1)
            lo_t = (lax.broadcasted_iota(jnp.int32, dsv.shape, 1) & (BLOCK - 1)) < HEAD_DIM
            dws_new.append(lax.dot_general(jnp.where(lo_t, dsv, 0.0).astype(BF16), vn, NT_DIMS, preferred_element_type=F32))
            dws_new.append(lax.dot_general(jnp.where(lo_t, 0.0, dsv).astype(BF16), vn, NT_DIMS, preferred_element_type=F32))
            dsvb = dsv.astype(BF16)
            dvn_a = jnp.dot(wst_ref[2 * p], dsvb, preferred_element_type=F32)
            dvn_b = jnp.dot(wst_ref[2 * p + 1], dsvb, preferred_element_type=F32)
            for b in range(SUB):
                lanes = slice(BLOCK * b, BLOCK * (b + 1))
                dvn = jnp.where(lo, dvn_a[:, lanes], dvn_b[:, lanes])
                mean = _half_sum(dvn * vnfs[b], lo) * (1.0 / HEAD_DIM)
                put(slice(BLOCK * b, BLOCK * (b + 1)), C_VG + BLOCK * p, rs[b] * (dvn - vnfs[b] * mean))

        dsink_ref[...] += dsink
        dqg = jnp.broadcast_to(dqg, (8, BLOCK))
        dqg_ref[...] += dqg + pltpu.roll(dqg, HEAD_DIM, 1)

        dkn = _overlap_add(dkn_parts)
        dv = _overlap_add(dv_parts)
        dkg = jnp.broadcast_to(jnp.sum(dkn * khat, axis=0, keepdims=True), (8, BLOCK))
        dkg_ref[...] += dkg + pltpu.roll(dkg, HEAD_DIM, 1)
        dkh = dkn * kg
        dk = rk * (dkh - khat * (_half_sum(dkh * khat, lo_kv) * (1.0 / HEAD_DIM)))
        dpb_ref[:, C_K:C_GA] = jnp.zeros((TILE, 2 * D_KV), BF16)
        dkv_ref[:, 0:D_KV] = dk[BLOCK:BLOCK + TILE]
        dkv_ref[:, D_KV:2 * D_KV] = dv[BLOCK:BLOCK + TILE]
        p0_ref[:, 0:D_KV] = dk[0:BLOCK]
        p0_ref[:, D_KV:2 * D_KV] = dv[0:BLOCK]
        p2_ref[:, 0:D_KV] = dk[BLOCK + TILE:]
        p2_ref[:, D_KV:2 * D_KV] = dv[BLOCK + TILE:]
        for g, new in enumerate(dws_new):
            dws_ref[g] += new
        for p, new in enumerate(dbs_new):
            dbsp_ref[p] += new

    cur, pkv, nkv, _ = _mix_specs(nt)
    kv_blk = (BLOCK, 2 * D_KV)
    half = pl.BlockSpec((TILE, D_ATTN), lambda i: (i, 0))
    return _pallas(
        body, name=name, grid=(nt,),
        in_specs=[pl.BlockSpec((TILE, D_MODEL), lambda i: (i, 0)), cur, pkv, nkv,
                  pl.BlockSpec((SUB * STACK, 3 * BLOCK), lambda i: (i, 0)), pl.BlockSpec((SUB * N_HEADS, BLOCK), lambda i: (i, 0)),
                  half, half, _full((1, BLOCK)), _full((1, BLOCK)), _full((8, BLOCK, BLOCK))],
        out_specs=[cur, pl.BlockSpec((TILE, 2 * D_KV), lambda i: (i, 0)),
                   pl.BlockSpec(kv_blk, lambda i: ((i + nt - 1) % nt, 0)),
                   pl.BlockSpec(kv_blk, lambda i: ((i + 1) % nt, 0)),
                   _full((8, BLOCK)), _full((8, BLOCK)), _full((1, BLOCK)),
                   _full((8, BLOCK, BLOCK)), _full((4, BLOCK, BLOCK))],
        out_shape=[_sds((s, D_IN), BF16), _sds((s, 2 * D_KV)), _sds((nt * BLOCK, 2 * D_KV)), _sds((nt * BLOCK, 2 * D_KV)),
                   _sds((8, BLOCK)), _sds((8, BLOCK)), _sds((1, BLOCK)),
                   _sds((8, BLOCK, BLOCK)), _sds((4, BLOCK, BLOCK))],
        operands=(dy, proj, proj, proj, probs, psink, attn, sv, qg2, kg2, wst), vmem_mib=56, jobs=jobs)


def _w_in_grad(dpb, dkv, p0, p2, x, vecs, name, jobs=()):
    s = x.shape[0]
    ts = min(2 * TILE, s)
    tiles = ts // TILE

    def body(dpb_ref, dkv_ref, p0_ref, p2_ref, x_ref, v_ref, gw_ref, dkvb_ref):
        @pl.when(pl.program_id(0) == 0)
        def _():
            gw_ref[...] = jnp.zeros_like(gw_ref)

        h = _adaln(x_ref[...], v_ref).astype(BF16)
        for t in range(tiles):
            halo = slice(BLOCK * t, BLOCK * (t + 1))
            first = slice(TILE * t, TILE * t + BLOCK)
            last = slice(TILE * (t + 1) - BLOCK, TILE * (t + 1))
            dkvb_ref[first, :] = (dkv_ref[first, :] + p2_ref[halo, :]).astype(BF16)
            if SUB > 2:
                inner = slice(TILE * t + BLOCK, TILE * (t + 1) - BLOCK)
                dkvb_ref[inner, :] = dkv_ref[inner, :].astype(BF16)
            dkvb_ref[last, :] = (dkv_ref[last, :] + p0_ref[halo, :]).astype(BF16)
        gw_ref[...] += lax.dot_general(dpb_ref[...], h, TN_DIMS, preferred_element_type=F32)
        gw_ref[C_K:C_GA, :] += lax.dot_general(dkvb_ref[...], h, TN_DIMS, preferred_element_type=F32)

    kv = pl.BlockSpec((ts, 2 * D_KV), lambda i: (i, 0))
    halo = pl.BlockSpec((tiles * BLOCK, 2 * D_KV), lambda i: (i, 0))
    return _pallas(
        body, name=name, grid=(s // ts,),
        in_specs=[pl.BlockSpec((ts, D_IN), lambda i: (i, 0)), kv, halo, halo,
                  pl.BlockSpec((ts, D_MODEL), lambda i: (i, 0)), VECS],
        out_specs=[_full((D_IN, D_MODEL)), kv], out_shape=[_sds((D_IN, D_MODEL)), _sds((s, 2 * D_KV), BF16)],
        operands=(dpb, dkv, p0, p2, x, vecs), vmem_mib=56, jobs=jobs)


def _proj_bwd(dpb, dkvb, x, dxo, vecs, wt, name, below=None, jobs=()):
    s = x.shape[0]
    ts = min(512, s)
    steps = s // ts

    def body(dpb_ref, dkvb_ref, x_ref, v_ref, w_ref, *rest):
        if dxo is not None:
            dxo_ref, *rest = rest
        if below is None:
            dxi_ref, dsh_ref, dsc_ref, dng_ref = rest
        else:
            y_ref, vb_ref, wo_ref, dxi_ref, dsh_ref, dsc_ref, dng_ref, dy_ref, gw_ref, dg_ref = rest

        @pl.when(pl.program_id(0) == 0)
        def _():
            dsh_ref[...] = jnp.zeros_like(dsh_ref)
            dsc_ref[...] = jnp.zeros_like(dsc_ref)
            dng_ref[...] = jnp.zeros_like(dng_ref)
            if below is not None:
                gw_ref[...] = jnp.zeros_like(gw_ref)

        dh = (jnp.dot(dpb_ref[...], w_ref[...], preferred_element_type=F32)
              + jnp.dot(dkvb_ref[...], w_ref[C_K:C_GA, :], preferred_element_type=F32))

        xv = x_ref[...]
        r = lax.rsqrt(jnp.mean(xv * xv, axis=-1, keepdims=True) + EPS)
        xn = xv * r
        ngv = _row(v_ref, V_NG)
        sc1 = 1.0 + _row(v_ref, V_SCALE)
        dsh_ref[...] += jnp.sum(dh, axis=0, keepdims=True)
        dsc_ref[...] += jnp.sum(dh * (xn * ngv), axis=0, keepdims=True)
        dh1 = dh * sc1
        dng_ref[...] += jnp.sum(dh1 * xn, axis=0, keepdims=True)
        dxn = dh1 * ngv
        dx = r * (dxn - xn * jnp.mean(dxn * xn, axis=-1, keepdims=True))
        if dxo is not None:
            dx = dx + dxo_ref[...]
        dxi_ref[...] = dx

        if below is not None:
            dy_ref[...] = lax.dot_general((dx * _row(vb_ref, V_GATE)).astype(BF16), wo_ref[...], NT_DIMS,
                                          preferred_element_type=F32)
            gw_ref[...] += lax.dot_general(y_ref[...], dx.astype(BF16), TN_DIMS, preferred_element_type=F32)

            @pl.when(pl.program_id(0) == steps - 1)
            def _():
                m = gw_ref[...]
                dg_ref[...] = jnp.sum(wo_ref[...].astype(F32) * m, axis=0, keepdims=True)
                gw_ref[...] = m * _row(vb_ref, V_GATE)

    row = pl.BlockSpec((ts, D_MODEL), lambda i: (i, 0))
    vec = _full((1, D_MODEL))
    square = _full((D_MODEL, D_MODEL))
    in_specs = [pl.BlockSpec((ts, D_IN), lambda i: (i, 0)), pl.BlockSpec((ts, 2 * D_KV), lambda i: (i, 0)),
                row, VECS, _full((D_IN, D_MODEL))]
    out_specs = [row, vec, vec, vec]
    out_shape = [_sds((s, D_MODEL)), _sds((1, D_MODEL)), _sds((1, D_MODEL)), _sds((1, D_MODEL))]
    operands = (dpb, dkvb, x, vecs, wt)
    if dxo is not None:
        in_specs += [row]
        operands += (dxo,)
    if below is not None:
        in_specs += [row, VECS, square]
        out_specs += [row, square, vec]
        out_shape += [_sds((s, D_MODEL)), _sds((D_MODEL, D_MODEL)), _sds((1, D_MODEL))]
        operands += tuple(below)
    return _pallas(
        body, name=name, grid=(steps,), in_specs=in_specs, out_specs=out_specs, out_shape=out_shape,
        operands=operands, vmem_mib=48 if below is None else 60, jobs=jobs)


def _add_rows(a, b, name):
    s = a.shape[0]
    ts = min(1024, s)

    def body(a_ref, b_ref, o_ref):
        o_ref[...] = a_ref[...] + b_ref[...]

    row = pl.BlockSpec((ts, D_MODEL), lambda i: (i, 0))
    return _pallas(body, name=name, grid=(s // ts,), in_specs=[row, row], out_specs=[row],
                   out_shape=[_sds((s, D_MODEL))], operands=(a, b), vmem_mib=32)[0]


def _pair_sum(g, r, c_idx, name, send_dtype=None):
    _, rows, cols = g.shape
    half = rows // 2

    def body(c_ref, g_ref, r_ref, o_ref, *narrow):
        total = g_ref[...] + r_ref[...]
        o_ref[...] = total
        for n_ref in narrow:
            n_ref[...] = total.astype(n_ref.dtype)

    blk = (None, half, cols)
    out_blk = pl.BlockSpec(blk, lambda j, c: (j, 0, 0))
    shapes = [_sds((N_CHIPS, half, cols))] + ([_sds((N_CHIPS, half, cols), send_dtype)] if send_dtype else [])
    return pl.pallas_call(
        body, name=name,
        grid_spec=pltpu.PrefetchScalarGridSpec(
            num_scalar_prefetch=1, grid=(N_CHIPS,),
            in_specs=[pl.BlockSpec(blk, lambda j, c: (j, c[0], 0)), out_blk], out_specs=[out_blk] * len(shapes)),
        out_shape=shapes,
        compiler_params=pltpu.CompilerParams(dimension_semantics=("arbitrary",), vmem_limit_bytes=32 * MIB),
    )(*_in_hbm(c_idx, g, r))


def _chip_sum(p, r, place, name):
    _, rows, cols = p.shape
    tr = rows // 2

    def body(j_ref, p_ref, r_ref, o_ref):
        o_ref[...] = ((p_ref[...] + r_ref[0].astype(F32)) + r_ref[1].astype(F32)) + r_ref[2].astype(F32)

    return pl.pallas_call(
        body, name=name,
        grid_spec=pltpu.PrefetchScalarGridSpec(
            num_scalar_prefetch=1, grid=(2,),
            in_specs=[pl.BlockSpec((None, tr, cols), lambda t, j: (j[0], t, 0)),
                      pl.BlockSpec((3, tr, cols), lambda t, j: (0, t, 0))],
            out_specs=pl.BlockSpec((tr, cols), lambda t, j: (2 * j[1] + t, 0))),
        out_shape=_sds((2 * rows, cols)),
        compiler_params=pltpu.CompilerParams(dimension_semantics=("arbitrary",), vmem_limit_bytes=32 * MIB),
    )(*_in_hbm(place, p, r))


def _cast_permute_w_in(wt, place_chunks):
    def body(t_ref, w_ref, w0_ref, w1_ref):
        def cast_into(o_ref):
            for t in range(N_CHUNKS):
                src = pl.multiple_of(t_ref[1 + t] * CHUNK_ROWS, CHUNK_ROWS)
                o_ref[CHUNK_ROWS * t:CHUNK_ROWS * (t + 1), :] = w_ref[pl.ds(src, CHUNK_ROWS), :].astype(BF16)

        @pl.when(pl.program_id(0) == 0)
        def _():
            cast_into(w0_ref)

        @pl.when(pl.program_id(0) == 1)
        def _():
            cast_into(w1_ref)

    return pl.pallas_call(
        body, name="cast_permute_w_in",
        grid_spec=pltpu.PrefetchScalarGridSpec(
            num_scalar_prefetch=1, grid=(DEPTH,),
            in_specs=[pl.BlockSpec((None, W_IN_BLK, D_MODEL), lambda l, tbl: (l, 0, 0))],
            out_specs=[pl.BlockSpec((None, W_IN_BLK, D_MODEL), lambda l, tbl: (tbl[0], 0, 0)),
                       pl.BlockSpec((W_IN_BLK, D_MODEL), lambda l, tbl: (0, 0))]),
        out_shape=[_sds((N_CHIPS, W_IN_BLK, D_MODEL), BF16), _sds((W_IN_BLK, D_MODEL), BF16)],
        compiler_params=pltpu.CompilerParams(dimension_semantics=("arbitrary",), vmem_limit_bytes=32 * MIB),
    )(*_in_hbm(place_chunks, wt))


def _gather_inputs(c, w_out, w0):
    half = W_IN_BLK // 2

    def body(c_ref, wout_ref, mine_ref, call_ref, woutb_ref, w0_ref, send_sems, recv_sems):
        x, y, cc = _coords()
        j = 2 * x + y
        b = 2 * j + cc
        sib = (x, y, 1 - cc)
        woutb_ref[...] = wout_ref[...].astype(BF16)
        call_ref[b] = c_ref[...]
        chips = _other_chips(x, y)

        def sems(k):
            return send_sems.at[k], recv_sems.at[k]

        def half_rows(chip_index):
            return w0_ref.at[chip_index, pl.ds(cc * half, half), :]

        first = [_remote(mine_ref.at[j, pl.ds(cc * half, half), :], half_rows(j), sems(k), (*chip, cc))
                 for k, chip in enumerate(chips)]
        k = 3
        rest = []
        for fx in (0, 1):
            for fy in (0, 1):
                for fc in (0, 1):
                    if fx or fy or fc:
                        dev = (1 - x if fx else x, 1 - y if fy else y, 1 - cc if fc else cc)
                        rest.append(_remote(call_ref.at[b], call_ref.at[b], sems(k), dev))
                        k += 1
        for cp in first + rest:
            cp.start()
        passed = []
        for k, chip in enumerate(chips):
            jk = 2 * chip[0] + chip[1]
            first[k].wait_recv()
            passed.append(_remote(half_rows(jk), half_rows(jk), sems(10 + k), sib))
            passed[k].start()
        for cp in first:
            cp.wait_send()
        for cp in rest + passed:
            cp.wait()

    return pl.pallas_call(
        body, name="gather_inputs", in_specs=[VMEM, VMEM, ANY], out_specs=[VMEM, VMEM, ANY],
        out_shape=[_sds((N_DEV, 1, D_MODEL)), _sds((DEPTH, W_OUT_BLK, D_MODEL), BF16),
                   _sds((N_CHIPS, W_IN_BLK, D_MODEL), BF16)],
        scratch_shapes=[pltpu.SemaphoreType.DMA((13,)), pltpu.SemaphoreType.DMA((13,))],
        input_output_aliases={2: 2},
        compiler_params=pltpu.CompilerParams(vmem_limit_bytes=32 * MIB),
    )(c, w_out, w0)


def _ada_rows(c_all, w_ada, b_blk):
    def body(c_ref, w_ref, b_ref, o_ref, cond_ref):
        cv = c_ref[...]
        cond = (cv * _sigmoid(cv)).astype(BF16)
        cond_ref[...] = cond.astype(F32)
        for l in range(DEPTH):
            o_ref[:, l, :] = jnp.dot(cond, w_ref[l].astype(BF16), preferred_element_type=F32) + b_ref[l:l + 1, :]

    return pl.pallas_call(
        body, name="ada_rows", in_specs=[VMEM, VMEM, VMEM], out_specs=[VMEM, VMEM],
        out_shape=[_sds((N_DEV, DEPTH, W_ADA_BLK)), _sds((N_DEV, D_MODEL))],
        compiler_params=pltpu.CompilerParams(vmem_limit_bytes=32 * MIB),
    )(c_all, w_ada, b_blk)


def _exchange_ada(part):
    def body(part_ref, out_ref, send_sems, recv_sems):
        x, y, cc = _coords()
        j = 2 * x + y
        out_ref[j] = part_ref[2 * j + cc]
        copies = []
        for k, chip in enumerate(_other_chips(x, y)):
            b_dst = 4 * chip[0] + 2 * chip[1] + cc
            copies.append(_remote(part_ref.at[b_dst], out_ref.at[j], (send_sems.at[k], recv_sems.at[k]), (*chip, cc)))
        for cp in copies:
            cp.start()
        for cp in copies:
            cp.wait()

    return pl.pallas_call(
        body, name="exchange_ada", in_specs=[VMEM], out_specs=VMEM,
        out_shape=_sds((N_CHIPS, DEPTH, W_ADA_BLK)),
        scratch_shapes=[pltpu.SemaphoreType.DMA((3,)), pltpu.SemaphoreType.DMA((3,))],
    )(part)


def _adamw_math(w, g, m, v):
    m = ADAM_B1 * m + (1.0 - ADAM_B1) * g
    v = ADAM_B2 * v + (1.0 - ADAM_B2) * (g * g)
    m_hat = m / (1.0 - ADAM_B1 ** ADAM_STEP)
    v_hat = v / (1.0 - ADAM_B2 ** ADAM_STEP)
    delta = -ADAM_LR * (m_hat / (jnp.sqrt(v_hat) + ADAM_EPS) + ADAM_WD * w)
    return delta, m, v


def _adamw_w_in(w, g0, g1, m, v, pos_chunks):
    def body(t_ref, w_ref, g0_ref, g1_ref, m_ref, v_ref, g_ref, d_ref, nm_ref, nv_ref):
        for l, src in enumerate((g0_ref, g1_ref)):
            g = src[...]
            g_ref[l] = g
            d_ref[l], nm_ref[l], nv_ref[l] = _adamw_math(w_ref[l], g, m_ref[l], v_ref[l])

    nat = pl.BlockSpec((DEPTH, CHUNK_ROWS, D_MODEL), lambda t, tbl: (0, t, 0))
    per = pl.BlockSpec((CHUNK_ROWS, D_MODEL), lambda t, tbl: (tbl[t], 0))
    return pl.pallas_call(
        body, name="adamw_w_in",
        grid_spec=pltpu.PrefetchScalarGridSpec(num_scalar_prefetch=1, grid=(N_CHUNKS,),
                                               in_specs=[nat, per, per, nat, nat], out_specs=[nat] * 4),
        out_shape=[_sds(w.shape)] * 4,
        compiler_params=pltpu.CompilerParams(dimension_semantics=("arbitrary",)),
    )(*_in_hbm(pos_chunks, w, g0, g1, m, v))


def _adamw_w_out(w, g0, g1, m, v):
    def body(w_ref, g0_ref, g1_ref, m_ref, v_ref, g_ref, d_ref, nm_ref, nv_ref):
        g = jnp.where(pl.program_id(0) == 0, g0_ref[...], g1_ref[...])
        g_ref[...] = g
        d_ref[...], nm_ref[...], nv_ref[...] = _adamw_math(w_ref[...], g, m_ref[...], v_ref[...])

    blk = pl.BlockSpec((None, W_OUT_BLK, D_MODEL), lambda l: (l, 0, 0))
    gblk = _full((W_OUT_BLK, D_MODEL))
    return pl.pallas_call(
        body, name="adamw_w_out", grid=(DEPTH,), in_specs=[blk, gblk, gblk, blk, blk], out_specs=[blk] * 4,
        out_shape=[_sds(w.shape)] * 4,
        compiler_params=pltpu.CompilerParams(dimension_semantics=("arbitrary",), vmem_limit_bytes=32 * MIB),
    )(w, g0, g1, m, v)


def _w_ada_grad_adamw(cond_t, dada, w, m, v):
    _, rows, cols = w.shape
    tr = 256

    def body(ct_ref, da_ref, w_ref, m_ref, v_ref, g_ref, d_ref, nm_ref, nv_ref):
        g = jnp.dot(ct_ref[...], da_ref[...].astype(BF16), preferred_element_type=F32)
        g_ref[...] = g
        d_ref[...], nm_ref[...], nv_ref[...] = _adamw_math(w_ref[...], g, m_ref[...], v_ref[...])

    blk = pl.BlockSpec((None, tr, cols), lambda l, t: (l, t, 0))
    return pl.pallas_call(
        body, name="w_ada_grad_adamw", grid=(DEPTH, rows // tr),
        in_specs=[pl.BlockSpec((tr, BLOCK), lambda l, t: (t, 0)), pl.BlockSpec((None, BLOCK, cols), lambda l, t: (l, 0, 0)),
                  blk, blk, blk],
        out_specs=[blk] * 4, out_shape=[_sds(w.shape)] * 4,
        compiler_params=pltpu.CompilerParams(dimension_semantics=("arbitrary", "arbitrary"), vmem_limit_bytes=32 * MIB),
    )(cond_t, dada, w, m, v)


def _small_sum_adamw(gathered_a, gathered_b, ws, rest):
    n_ws = 8 * BLOCK

    def body(*refs):
        a_refs, b_ref = refs[:DEPTH], refs[DEPTH]
        ws_ref, ms_ref, vs_ref, wr_ref, mr_ref, vr_ref = refs[DEPTH + 1:DEPTH + 7]
        gs_ref, ds_ref, nms_ref, nvs_ref, gr_ref, dr_ref, nmr_ref, nvr_ref = refs[DEPTH + 7:]

        def total(ref):
            g = ref[0]
            for b in range(1, N_DEV):
                g = g + ref[b]
            return g

        totals = [total(ref) for ref in a_refs]
        for l, t in enumerate(totals):
            g = t[0:n_ws]
            gs_ref[l] = g
            ds_ref[l], nms_ref[l], nvs_ref[l] = _adamw_math(ws_ref[l], g, ms_ref[l], vs_ref[l])
        g = jnp.concatenate([t[n_ws:] for t in totals] + [total(b_ref)], axis=0)
        gr_ref[...] = g
        dr_ref[...], nmr_ref[...], nvr_ref[...] = _adamw_math(wr_ref[...], g, mr_ref[...], vr_ref[...])

    return pl.pallas_call(
        body, name="small_sum_adamw", in_specs=[VMEM] * (DEPTH + 7), out_specs=[VMEM] * 8,
        out_shape=[_sds(ws[0].shape)] * 4 + [_sds(rest[0].shape)] * 4,
        compiler_params=pltpu.CompilerParams(vmem_limit_bytes=48 * MIB),
    )(*gathered_a, gathered_b, *ws, *rest)


_SMALL_A_REST = (("b_s", 8), ("q_gain", 1), ("k_gain", 1), ("sink", 1))
_SMALL_A = (("w_s", 8 * BLOCK),) + _SMALL_A_REST
_SMALL_B = (("b_ada", DEPTH * 24), ("norm_gain", DEPTH * 8), ("sq_err", 1))


def _pack_rows(parts, layout, layer=None):
    rows = []
    for name, n in layout:
        flat = (parts[name] if layer is None else parts[name][layer]).reshape(-1)
        rows.append(jnp.pad(flat, (0, n * 128 - flat.shape[0])).reshape(n, 128))
    n_rows = sum(n for _, n in layout)
    if n_rows % 8:
        rows.append(jnp.zeros((-n_rows % 8, 128), F32))
    return jnp.concatenate(rows, axis=0)


def _pack_rest(parts):
    return jnp.concatenate([_pack_rows(parts, _SMALL_A_REST, l) for l in range(DEPTH)] + [_pack_rows(parts, _SMALL_B)], axis=0)


def _unpack_rest(packed, shapes):
    def take(r0, layout, shape_of):
        got = {}
        for name, n in layout:
            shape = shape_of(name)
            size = 1
            for d in shape:
                size *= d
            got[name] = packed[r0:r0 + n].reshape(-1)[:size].reshape(shape)
            r0 += n
        return got, r0 + -r0 % 8

    layers, r0 = [], 0
    for _ in range(DEPTH):
        got, r0 = take(r0, _SMALL_A_REST, lambda name: shapes[name][1:])
        layers.append(got)
    out, _ = take(r0, _SMALL_B, lambda name: shapes[name])
    out.update({name: jnp.stack([layer[name] for layer in layers]) for name, _ in _SMALL_A_REST})
    return out


def _permute_heads(a, axis):
    shp = a.shape
    a = a.reshape(shp[:axis] + (2, 4, HEAD_DIM) + shp[axis + 1:])
    a = jnp.swapaxes(a, axis, axis + 1)
    return a.reshape(shp)


def _unpermute_heads(a, axis):
    shp = a.shape
    a = a.reshape(shp[:axis] + (4, 2, HEAD_DIM) + shp[axis + 1:])
    a = jnp.swapaxes(a, axis, axis + 1)
    return a.reshape(shp)


def _permute_w_out(w):
    return jnp.concatenate([_permute_heads(w[:D_ATTN], 0), w[D_ATTN:]], axis=0)


def _unpermute_w_out(w):
    return jnp.concatenate([_unpermute_heads(w[:D_ATTN], 0), w[D_ATTN:]], axis=0)


def kernel(x, c, w_ada, b_ada, norm_gain, w_in, q_gain, k_gain, sink, w_s, b_s, w_out, loss_target, m_w_ada, m_b_ada, m_norm_gain, m_w_in, m_q_gain, m_k_gain, m_sink, m_w_s, m_b_s, m_w_out, v_w_ada, v_b_ada, v_norm_gain, v_w_in, v_q_gain, v_k_gain, v_sink, v_w_s, v_b_s, v_w_out):
    ix, iy, ic = _coords()
    chip = 2 * ix + iy
    chip_idx = jnp.stack([chip, ic]).astype(jnp.int32)
    core_idx = jnp.reshape(ic, (1,)).astype(jnp.int32)
    src_chunks = lax.dynamic_index_in_dim(jnp.asarray(_CHUNK_SRC), chip, 0, keepdims=False)
    pos_chunks = lax.dynamic_index_in_dim(jnp.asarray(_CHUNK_POS), chip, 0, keepdims=False)
    x0, target = x[0], loss_target[0]

    wt, mt, vt = (jnp.swapaxes(a, 1, 2) for a in (w_in, m_w_in, v_w_in))
    w0_mine, wloc_in1 = _cast_permute_w_in(wt, jnp.concatenate([chip_idx[:1], src_chunks]))
    c_all, wloc_out, w0 = _gather_inputs(c, w_out, w0_mine)
    wts = [w0.reshape(D_IN, D_MODEL), None]

    b_blk = lax.dynamic_slice_in_dim(b_ada, chip * W_ADA_BLK, W_ADA_BLK, axis=1)
    ada_part, cond = _ada_rows(c_all.reshape(N_DEV, D_MODEL), w_ada, b_blk)
    ada = jnp.moveaxis(_exchange_ada(ada_part), 0, 1).reshape(DEPTH, 3 * D_MODEL)
    vecs = [jnp.concatenate([norm_gain[l:l + 1], ada[l:l + 1, D_MODEL:2 * D_MODEL], ada[l:l + 1, 0:D_MODEL],
                             ada[l:l + 1, 2 * D_MODEL:], jnp.zeros((4, D_MODEL), F32)], axis=0) for l in range(DEPTH)]

    qg2 = jnp.concatenate([q_gain, q_gain], axis=-1)
    kg2 = jnp.concatenate([k_gain, k_gain], axis=-1)
    ws_b = w_s.astype(BF16)
    wst_b = jnp.swapaxes(w_s, -1, -2).astype(BF16)
    bsp = jnp.repeat(jnp.swapaxes(b_s.reshape(DEPTH, 4, 2, BLOCK), -1, -2), HEAD_DIM, axis=-1)
    bias = jnp.asarray(_bias_table())

    def mix_args(l):
        return bias, sink[l], qg2[l:l + 1], kg2[l:l + 1], ws_b[l]

    w_out_shape = (W_OUT_BLK, D_MODEL)
    proj0, wo0, wo1 = _proj_fwd(x0, vecs[0], wts[0], "proj_fwd_0",
                                jobs=[_job_gather([(wloc_out, 0), (wloc_out, 1)], [w_out_shape, w_out_shape])])
    y0, *kept0, w1 = _mix_fwd(proj0, *mix_args(0), bsp[0], "mix_fwd_0",
                              jobs=[_job_gather([(wloc_in1, None)], [(W_IN_BLK, D_MODEL)])])
    wts[1] = w1.reshape(D_IN, D_MODEL)
    wos = [_permute_w_out(w.reshape(D_MODEL, D_MODEL)) for w in (wo0, wo1)]
    x1, proj1 = _out_proj_fwd(y0, x0, vecs[0], wos[0], vecs[1], wts[1], "out_proj_fwd_01")
    y1, *kept1 = _mix_fwd(proj1, *mix_args(1), bsp[1], "mix_fwd_1")

    def blocks_out(gw):
        return _unpermute_w_out(gw).reshape(N_CHIPS, W_OUT_BLK, D_MODEL)

    def small_pack(dws, dbsp, dqg, dkg, dsink):
        b_s = jnp.swapaxes(dbsp[:, :, ::HEAD_DIM], -1, -2).reshape(8, BLOCK)
        return _pack_rows(dict(w_s=dws, b_s=b_s, q_gain=dqg[0, :HEAD_DIM], k_gain=dkg[0, :HEAD_DIM], sink=dsink[0, :N_HEADS]),
                          _SMALL_A)

    dx2, sq, dy1, gwo1, dgate1 = _out_loss_bwd(y1, x1, vecs[1], wos[1], target, "out_loss_bwd_1")
    go1 = blocks_out(gwo1)
    dpb, dkv, p0, p2, dqg1, dkg1, dsink1, dws1, dbsp1, ro1 = _mix_bwd(
        dy1, proj1, *kept1, qg2[1:2], kg2[1:2], wst_b[1], "mix_bwd_1", jobs=[_job_swap(go1)])
    po1, = _pair_sum(go1, ro1, core_idx, "pair_sum_w_out_1")
    gwi1, dkvb, co1, gathered_a1 = _w_in_grad(
        dpb, dkv, p0, p2, x1, vecs[1], "w_in_grad_1",
        jobs=[_job_scatter(po1), _job_all_gather(small_pack(dws1, dbsp1, dqg1, dkg1, dsink1))])
    gi1 = gwi1.reshape(N_CHIPS, W_IN_BLK, D_MODEL)
    fo1 = _chip_sum(po1, co1, chip_idx, "chip_sum_w_out_1")
    dx1, dsh1, dsc1, dng1, dy0, gwo0, dgate0, grad_wo1, ri1 = _proj_bwd(
        dpb, dkvb, x1, dx2, vecs[1], wts[1], "proj_out_bwd_10", below=(y0, vecs[0], wos[0]),
        jobs=[_job_join(fo1), _job_swap(gi1)])
    pi1, = _pair_sum(gi1, ri1, core_idx, "pair_sum_w_in_1")
    go0 = blocks_out(gwo0)
    dpb, dkv, p0, p2, dqg0, dkg0, dsink0, dws0, dbsp0, ci1, ro0 = _mix_bwd(
        dy0, proj0, *kept0, qg2[0:1], kg2[0:1], wst_b[0], "mix_bwd_0", jobs=[_job_scatter(pi1), _job_swap(go0)])
    fi1 = _chip_sum(pi1, ci1, chip_idx, "chip_sum_w_in_1")
    po0, = _pair_sum(go0, ro0, core_idx, "pair_sum_w_out_0")

    gwi0, dkvb, grad_wi1, co0, gathered_a0 = _w_in_grad(
        dpb, dkv, p0, p2, x0, vecs[0], "w_in_grad_0",
        jobs=[_job_join(fi1), _job_scatter(po0), _job_all_gather(small_pack(dws0, dbsp0, dqg0, dkg0, dsink0))])
    gi0 = gwi0.reshape(N_CHIPS, W_IN_BLK, D_MODEL)
    fo0 = _chip_sum(po0, co0, chip_idx, "chip_sum_w_out_0")

    ri0, grad_wo0 = _comm([_job_swap(gi0), _job_join(fo0)], "swap_w_in_0")
    pi0, pi0_send = _pair_sum(gi0, ri0, core_idx, "pair_sum_w_in_0", send_dtype=BF16)
    dx0_own, dsh0, dsc0, dng0, ci0 = _proj_bwd(dpb, dkvb, x0, None, vecs[0], wts[0], "proj_bwd_0",
                                               jobs=[_job_scatter(pi0_send)])
    dx0 = _add_rows(dx0_own, dx1, "grad_x_add")
    fi0 = _chip_sum(pi0, ci0, chip_idx, "chip_sum_w_in_0")

    small_g = dict(
        b_ada=jnp.stack([jnp.concatenate([dsh0, dsc0, dgate0], axis=-1)[0], jnp.concatenate([dsh1, dsc1, dgate1], axis=-1)[0]]),
        norm_gain=jnp.stack([dng0[0], dng1[0]]), sq_err=sq[0])
    none = jnp.zeros((1,), F32)
    small_w = dict(w_s=w_s, b_s=b_s, b_ada=b_ada, norm_gain=norm_gain, q_gain=q_gain, k_gain=k_gain, sink=sink, sq_err=none)
    small_m = dict(w_s=m_w_s, b_s=m_b_s, b_ada=m_b_ada, norm_gain=m_norm_gain, q_gain=m_q_gain, k_gain=m_k_gain, sink=m_sink,
                   sq_err=none)
    small_v = dict(w_s=v_w_s, b_s=v_b_s, b_ada=v_b_ada, norm_gain=v_norm_gain, q_gain=v_q_gain, k_gain=v_k_gain, sink=v_sink,
                   sq_err=none)
    grad_wi0, gathered_b = _comm([_job_join(fi0), _job_all_gather(_pack_rows(small_g, _SMALL_B))], "join_w_in_0")
    packed = _small_sum_adamw(
        [a.reshape(N_DEV, -1, 128) for a in (gathered_a0, gathered_a1)], gathered_b.reshape(N_DEV, -1, 128),
        [a.reshape(DEPTH, 8 * BLOCK, BLOCK) for a in (w_s, m_w_s, v_w_s)], [_pack_rest(p) for p in (small_w, small_m, small_v)])
    shapes = {k: a.shape for k, a in small_w.items()}
    sg, sd, sm, sv = (dict(_unpack_rest(rest, shapes), w_s=ws.reshape(w_s.shape)) for ws, rest in zip(packed[:4], packed[4:]))
    loss = 0.5 * sg["sq_err"][0]

    dada_all = gathered_b.reshape(N_DEV, -1, 128)[:, 0:DEPTH * 24].reshape(N_DEV, DEPTH, 3 * D_MODEL)
    dada_blk = jnp.moveaxis(lax.dynamic_slice_in_dim(dada_all, chip * W_ADA_BLK, W_ADA_BLK, axis=2), 0, 1)
    pad = BLOCK - N_DEV
    ada_out = _w_ada_grad_adamw(
        jnp.pad(cond.T, ((0, 0), (0, pad))).astype(BF16), jnp.pad(dada_blk, ((0, 0), (0, pad), (0, 0))),
        w_ada, m_w_ada, v_w_ada)

    in_out = [jnp.swapaxes(a, 1, 2) for a in _adamw_w_in(wt, grad_wi0, grad_wi1, mt, vt, pos_chunks)]
    out_out = _adamw_w_out(w_out, grad_wo0, grad_wo1, m_w_out, v_w_out)

    def ordered(k):
        small = (sg, sd, sm, sv)[k]
        return (ada_out[k], small["b_ada"], small["norm_gain"], in_out[k], small["q_gain"], small["k_gain"], small["sink"],
                small["w_s"], small["b_s"], out_out[k])

    return (loss, dx0[None], *ordered(0), *ordered(1), *ordered(2), *ordered(3))
```

```python
import numpy as np

import jax
import jax.numpy as jnp
from jax import lax
from jax.experimental import pallas as pl
from jax.experimental.pallas import tpu as pltpu

F32 = jnp.float32
BF16 = jnp.bfloat16

D_MODEL = 1024
DEPTH = 2
HEAD_DIM = 64
N_HEADS = 8
BLOCK = 128
SUB = 4
TILE = SUB * BLOCK
STACK = 8 * BLOCK
D_ATTN = 512
D_KV = 128
D_IN = 2816
N_CHIPS = 4
N_DEV = 8
W_IN_BLK = D_IN // N_CHIPS
W_OUT_BLK = D_MODEL // N_CHIPS
W_ADA_BLK = 3 * D_MODEL // N_CHIPS
CHUNK_ROWS = HEAD_DIM
N_CHUNKS = W_IN_BLK // CHUNK_ROWS
EPS = 1e-6
NEG_INF = -1e30

C_Q, C_K, C_V, C_GA, C_U, C_VG, C_GG = 0, 512, 640, 768, 1280, 1792, 2304

ADAM_LR = 0.001
ADAM_B1 = 0.9
ADAM_B2 = 0.999
ADAM_EPS = 1e-08
ADAM_WD = 0.01
ADAM_STEP = 10

MESH = pl.DeviceIdType.MESH
MIB = 1024 * 1024
ANY = pl.BlockSpec(memory_space=pl.ANY)
VMEM = pl.BlockSpec(memory_space=pltpu.VMEM)

NT_DIMS = (((1,), (1,)), ((), ()))
TN_DIMS = (((0,), (0,)), ((), ()))

_PAIR_ORDER = (0, 4, 1, 5, 2, 6, 3, 7)
_CHUNK_SRC = np.array([
    list(_PAIR_ORDER) + [8, 9, 10],
    [0] + [1 + h for h in _PAIR_ORDER] + [9, 10],
    list(range(N_CHUNKS)),
    list(range(N_CHUNKS)),
], np.int32)
_CHUNK_POS = np.argsort(_CHUNK_SRC, axis=1).astype(np.int32)


def _bias_table():
    i = np.arange(N_HEADS * BLOCK)[:, None]
    j = np.arange(3 * BLOCK)[None, :]
    dist = np.abs(j - BLOCK - (i % BLOCK))
    slope = 2.0 ** -(i // BLOCK + 1.0)
    inner = np.where(dist <= BLOCK, -(slope * dist), NEG_INF)
    first = np.where(j >= BLOCK, inner, NEG_INF)
    last = np.where(j < 2 * BLOCK, inner, NEG_INF)
    return np.stack([first, inner, last]).astype(np.float32)


def _full(shape):
    n = len(shape)
    return pl.BlockSpec(shape, lambda *_: (0,) * n)


def _sds(shape, dtype=F32):
    return jax.ShapeDtypeStruct(shape, dtype)


def _coords():
    return lax.axis_index("x"), lax.axis_index("y"), lax.axis_index("c")


def _other_chips(x, y):
    return [(1 - x, y), (x, 1 - y), (1 - x, 1 - y)]


def _in_hbm(*operands):
    return [pltpu.with_memory_space_constraint(a, pltpu.HBM) if a.size * a.dtype.itemsize >= MIB // 4 else a
            for a in operands]


def _remote(src, dst, sems, dev):
    return pltpu.make_async_remote_copy(src_ref=src, dst_ref=dst, send_sem=sems[0], recv_sem=sems[1],
                                        device_id=dev, device_id_type=MESH)


class _Job:
    def __init__(self, inputs, out_shapes, n_remote, n_local, make, then=None, in_place=False):
        self.inputs, self.out_shapes, self.n_remote, self.n_local, self.make = inputs, out_shapes, n_remote, n_local, make
        self.then = then
        self.in_place = in_place


def _job_aliases(jobs, in_base, out_base):
    aliases, a, b = {}, 0, 0
    for j in jobs:
        if j.in_place:
            aliases.update({in_base + a + k: out_base + b + k for k in range(len(j.inputs))})
        a, b = a + len(j.inputs), b + len(j.out_shapes)
    return aliases


def _job_copies(jobs, jin, jout, sems, second=False):
    send, recv, loc = sems
    res, a, b, r, l = [], 0, 0, 0, 0
    for j in jobs:
        build = j.then if second else j.make
        if build is not None:
            res += build(jin[a:a + len(j.inputs)], jout[b:b + len(j.out_shapes)],
                         lambda k, r=r: (send.at[r + k], recv.at[r + k]), lambda k, l=l: loc.at[l + k])
        a, b, r, l = a + len(j.inputs), b + len(j.out_shapes), r + j.n_remote, l + j.n_local
    return res


def _run(copies):
    for cp in copies:
        cp.start()
    for cp in copies:
        cp.wait()


def _job_gather(sources, shapes):
    n = len(sources)

    def make(ins, outs, rsem, lsem):
        x, y, c = _coords()
        j = 2 * x + y
        res = []
        for t, ((_, layer), src, dst) in enumerate(zip(sources, ins, outs)):
            src = src if layer is None else src.at[layer]
            res.append(pltpu.make_async_copy(src, dst.at[j], lsem(t)))
            for k, chip in enumerate(_other_chips(x, y)):
                res.append(_remote(src, dst.at[j], rsem(3 * t + k), (*chip, c)))
        return res

    return _Job([a for a, _ in sources], [_sds((N_CHIPS,) + s, BF16) for s in shapes], 3 * n, n, make)


def _job_swap(g):
    _, rows, cols = g.shape
    half = rows // 2

    def make(ins, outs, rsem, lsem):
        x, y, c = _coords()
        return [_remote(ins[0].at[:, pl.ds((1 - c) * half, half), :], outs[0], rsem(0), (x, y, 1 - c))]

    return _Job([g], [_sds((N_CHIPS, half, cols))], 1, 0, make)


def _job_scatter(p):
    def make(ins, outs, rsem, lsem):
        x, y, c = _coords()
        return [_remote(ins[0].at[2 * chip[0] + chip[1]], outs[0].at[k], rsem(k), (*chip, c))
                for k, chip in enumerate(_other_chips(x, y))]

    return _Job([p], [_sds((3,) + p.shape[1:], p.dtype)], 3, 0, make)


def _job_all_gather(blk):
    m_per = blk.shape[0]

    def rows(ref, px, py, pc):
        return ref.at[pl.ds((4 * px + 2 * py + pc) * m_per, m_per), :]

    def make(ins, outs, rsem, lsem):
        x, y, c = _coords()
        res = [pltpu.make_async_copy(ins[0], rows(outs[0], x, y, c), lsem(0)),
               _remote(ins[0], rows(outs[0], x, y, c), rsem(0), (x, y, 1 - c))]
        res += [_remote(ins[0], rows(outs[0], x, y, c), rsem(1 + k), (*chip, c)) for k, chip in enumerate(_other_chips(x, y))]
        return res

    def then(ins, outs, rsem, lsem):
        x, y, c = _coords()
        return [_remote(rows(outs[0], *chip, c), rows(outs[0], *chip, c), rsem(4 + k), (x, y, 1 - c))
                for k, chip in enumerate(_other_chips(x, y))]

    return _Job([blk], [_sds((N_DEV * m_per, blk.shape[1]), blk.dtype)], 7, 1, make, then)


def _job_join(f):
    half = f.shape[0] // 2

    def make(ins, outs, rsem, lsem):
        x, y, c = _coords()
        mine = pl.ds(c * half, half)
        return [_remote(ins[0].at[mine, :], outs[0].at[mine, :], rsem(0), (x, y, 1 - c))]

    return _Job([f], [_sds(f.shape, f.dtype)], 1, 0, make, in_place=True)


def _pallas(body, *, name, grid, in_specs, out_specs, out_shape, operands, vmem_mib, jobs=()):
    in_specs, out_specs, out_shape = list(in_specs), list(out_specs), list(out_shape)
    n_in, n_out = len(in_specs), len(out_specs)
    j_in = [a for j in jobs for a in j.inputs]
    j_out = [s for j in jobs for s in j.out_shapes]
    n_rem = max(1, sum(j.n_remote for j in jobs))
    n_loc = max(1, sum(j.n_local for j in jobs))
    scratch = [pltpu.SemaphoreType.DMA((n_rem,)), pltpu.SemaphoreType.DMA((n_rem,)),
               pltpu.SemaphoreType.DMA((n_loc,))] if jobs else []

    def wrapped(*refs):
        ins = refs[:n_in]
        jin = refs[n_in:n_in + len(j_in)]
        outs = refs[n_in + len(j_in):n_in + len(j_in) + n_out]
        jout = refs[n_in + len(j_in) + n_out:n_in + len(j_in) + n_out + len(j_out)]

        if jobs:
            first = last = None
            for d, n in enumerate(grid):
                f, e = pl.program_id(d) == 0, pl.program_id(d) == n - 1
                first, last = (f, e) if first is None else (first & f, last & e)

            @pl.when(first)
            def _():
                for cp in _job_copies(jobs, jin, jout, refs[-3:]):
                    cp.start()

        body(*ins, *outs)

        if jobs:
            @pl.when(last)
            def _():
                for cp in _job_copies(jobs, jin, jout, refs[-3:]):
                    cp.wait()
                _run(_job_copies(jobs, jin, jout, refs[-3:], second=True))

    return pl.pallas_call(
        wrapped, name=name, grid=grid,
        in_specs=in_specs + [ANY] * len(j_in), out_specs=out_specs + [ANY] * len(j_out),
        out_shape=out_shape + j_out, scratch_shapes=scratch, input_output_aliases=_job_aliases(jobs, n_in, n_out),
        compiler_params=pltpu.CompilerParams(dimension_semantics=("arbitrary",) * len(grid),
                                             vmem_limit_bytes=vmem_mib * MIB),
    )(*_in_hbm(*operands, *j_in))


def _comm(jobs, name):
    j_in = [a for j in jobs for a in j.inputs]
    j_out = [s for j in jobs for s in j.out_shapes]
    n_rem = max(1, sum(j.n_remote for j in jobs))
    n_loc = max(1, sum(j.n_local for j in jobs))

    def body(*refs):
        jin, jout = refs[:len(j_in)], refs[len(j_in):len(j_in) + len(j_out)]
        _run(_job_copies(jobs, jin, jout, refs[-3:]))
        _run(_job_copies(jobs, jin, jout, refs[-3:], second=True))

    return pl.pallas_call(
        body, name=name, in_specs=[ANY] * len(j_in), out_specs=[ANY] * len(j_out), out_shape=j_out,
        scratch_shapes=[pltpu.SemaphoreType.DMA((n_rem,)), pltpu.SemaphoreType.DMA((n_rem,)),
                        pltpu.SemaphoreType.DMA((n_loc,))],
        input_output_aliases=_job_aliases(jobs, 0, 0),
    )(*_in_hbm(*j_in))


def _sigmoid(x):
    return 1.0 / (1.0 + jnp.exp(-x))


def _lo_mask(shape):
    return lax.broadcasted_iota(jnp.int32, shape, len(shape) - 1) < HEAD_DIM


def _half_sum(x, lo):
    a = jnp.sum(jnp.where(lo, x, 0.0), axis=-1, keepdims=True)
    b = jnp.sum(jnp.where(lo, 0.0, x), axis=-1, keepdims=True)
    return jnp.where(lo, a, b)


def _half_rms_scale(x, lo):
    return lax.rsqrt(_half_sum(x * x, lo) * (1.0 / HEAD_DIM) + EPS)


def _stack_heads(pairs, lo):
    return jnp.concatenate([jnp.where(lo, t, 0.0) for t in pairs] + [jnp.where(lo, 0.0, t) for t in pairs], axis=0)


def _unstack_pair(stack, p, lo):
    return jnp.where(lo, stack[BLOCK * p:BLOCK * (p + 1)], stack[BLOCK * (4 + p):BLOCK * (5 + p)])


def _attention_probs(q_stack, kn, bias_ref, sink_ref):
    rows = N_HEADS * BLOCK
    s = lax.dot_general(q_stack, kn, NT_DIMS, preferred_element_type=F32) + bias_ref[...]
    sink = jnp.concatenate([jnp.full((BLOCK, BLOCK), sink_ref[h], F32) for h in range(N_HEADS)], axis=0)
    cols = [s[:, BLOCK * j:BLOCK * (j + 1)] for j in range(3)]
    top = jnp.max(jnp.maximum(jnp.maximum(cols[0], cols[1]), cols[2]), axis=-1, keepdims=True)
    m = jnp.maximum(jnp.broadcast_to(top, (rows, BLOCK)), sink)
    e = [jnp.exp(c - m) for c in cols]
    es = jnp.exp(sink - m)
    inv = 1.0 / (jnp.broadcast_to(jnp.sum((e[0] + e[1]) + e[2], axis=-1, keepdims=True), (rows, BLOCK)) + es)
    return jnp.concatenate([c * inv for c in e], axis=1), es * inv


def _kv_rows(cur_ref, pkv_ref, nkv_ref):
    k = jnp.concatenate([pkv_ref[:, 0:D_KV], cur_ref[:, C_K:C_K + D_KV], nkv_ref[:, 0:D_KV]], axis=0)
    v = jnp.concatenate([pkv_ref[:, D_KV:2 * D_KV], cur_ref[:, C_V:C_V + D_KV], nkv_ref[:, D_KV:2 * D_KV]], axis=0)
    return k, v


def _overlap_add(parts):
    blocks = []
    for j in range(SUB + 2):
        terms = [parts[b][BLOCK * (j - b):BLOCK * (j - b + 1)] for b in range(SUB) if 0 <= j - b <= 2]
        total = terms[0]
        for t in terms[1:]:
            total = total + t
        blocks.append(total)
    return jnp.concatenate(blocks, axis=0)


def _mix_specs(nt):
    cur = pl.BlockSpec((TILE, D_IN), lambda i: (i, 0))
    kv_col = C_K // (2 * D_KV)
    pkv = pl.BlockSpec((BLOCK, 2 * D_KV), lambda i: (jnp.maximum(i * SUB - 1, 0), kv_col))
    nkv = pl.BlockSpec((BLOCK, 2 * D_KV), lambda i: (jnp.minimum((i + 1) * SUB, nt * SUB - 1), kv_col))
    table = (None, N_HEADS * BLOCK, 3 * BLOCK)
    first = pl.BlockSpec(table, lambda i: (jnp.where(i == 0, 0, 1), 0, 0))
    inner = pl.BlockSpec(table, lambda i: (1, 0, 0))
    last = pl.BlockSpec(table, lambda i: (jnp.where(i == nt - 1, 2, 1), 0, 0))
    return cur, pkv, nkv, [first] + [inner] * (SUB - 2) + [last]


V_NG, V_SCALE, V_SHIFT, V_GATE = 0, 1, 2, 3
VECS = _full((8, D_MODEL))


def _row(v_ref, k):
    return v_ref[k:k + 1, :]


def _adaln(xv, v_ref):
    r = lax.rsqrt(jnp.mean(xv * xv, axis=-1, keepdims=True) + EPS)
    return ((xv * r) * _row(v_ref, V_NG)) * (1.0 + _row(v_ref, V_SCALE)) + _row(v_ref, V_SHIFT)


def _proj_fwd(x, vecs, wt, name, jobs=()):
    s = x.shape[0]
    ts = min(512, s)

    def body(x_ref, v_ref, w_ref, o_ref):
        h = _adaln(x_ref[...], v_ref)
        o_ref[...] = lax.dot_general(h.astype(BF16), w_ref[...], NT_DIMS, preferred_element_type=F32)

    return _pallas(
        body, name=name, grid=(s // ts,),
        in_specs=[pl.BlockSpec((ts, D_MODEL), lambda i: (i, 0)), VECS, _full((D_IN, D_MODEL))],
        out_specs=[pl.BlockSpec((ts, D_IN), lambda i: (i, 0))], out_shape=[_sds((s, D_IN))],
        operands=(x, vecs, wt), vmem_mib=48, jobs=jobs)


def _diagonal():
    return lax.broadcasted_iota(jnp.int32, (BLOCK, BLOCK), 0) == lax.broadcasted_iota(jnp.int32, (BLOCK, BLOCK), 1)


def _column_as_row(wide, eye):
    return jnp.sum(jnp.where(eye, wide, 0.0), axis=0, keepdims=True)


def _mix_fwd(proj, bias, sink, qg2, kg2, ws, bsp, name, jobs=()):
    s = proj.shape[0]
    nt = s // TILE

    def body(sink_ref, cur_ref, pkv_ref, nkv_ref, *rest):
        bias_refs = rest[:SUB]
        qg_ref, kg_ref, ws_ref, bsp_ref, y_ref, p_ref, ps_ref, attn_ref, sv_ref = rest[SUB:]
        lo = _lo_mask((BLOCK, BLOCK))
        eye = _diagonal()
        lo_kv = _lo_mask((TILE + 2 * BLOCK, BLOCK))
        k_all, v_all = _kv_rows(cur_ref, pkv_ref, nkv_ref)
        kn_all = ((k_all * _half_rms_scale(k_all, lo_kv)) * kg_ref[...]).astype(BF16)
        vb_all = v_all.astype(BF16)
        for b in range(SUB):
            rows = slice(BLOCK * b, BLOCK * (b + 1))
            window = slice(BLOCK * b, BLOCK * (b + 3))
            qn = []
            for p in range(4):
                q = cur_ref[rows, C_Q + BLOCK * p:C_Q + BLOCK * (p + 1)]
                qn.append(((q * _half_rms_scale(q, lo)) * qg_ref[...]) * 0.125)
            q_stack = _stack_heads(qn, lo).astype(BF16)
            prob, psink = _attention_probs(q_stack, kn_all[window], bias_refs[b], sink_ref)
            pb = prob.astype(BF16)
            p_ref[STACK * b:STACK * (b + 1), :] = pb
            ps_ref[N_HEADS * b:N_HEADS * (b + 1), :] = jnp.concatenate(
                [_column_as_row(psink[BLOCK * h:BLOCK * (h + 1)], eye) for h in range(N_HEADS)], axis=0)
            o_stack = jnp.dot(pb, vb_all[window], preferred_element_type=F32)
            for p in range(4):
                g = cur_ref[rows, C_GA + BLOCK * p:C_GA + BLOCK * (p + 1)]
                attn = _unstack_pair(o_stack, p, lo)
                attn_ref[rows, BLOCK * p:BLOCK * (p + 1)] = attn.astype(BF16)
                y_ref[rows, BLOCK * p:BLOCK * (p + 1)] = (attn * (g * _sigmoid(g))).astype(BF16)

        for p in range(4):
            cols = slice(C_VG + BLOCK * p, C_VG + BLOCK * (p + 1))
            vn = []
            for b in range(SUB):
                vg = cur_ref[BLOCK * b:BLOCK * (b + 1), cols]
                vn.append((vg * _half_rms_scale(vg, lo)).astype(BF16))
            vn = jnp.concatenate(vn, axis=1)
            sv_a = jnp.dot(ws_ref[2 * p], vn, preferred_element_type=F32)
            sv_b = jnp.dot(ws_ref[2 * p + 1], vn, preferred_element_type=F32)
            for b in range(SUB):
                rows = slice(BLOCK * b, BLOCK * (b + 1))
                lanes = slice(BLOCK * b, BLOCK * (b + 1))
                sv = jnp.where(lo, sv_a[:, lanes], sv_b[:, lanes]) + bsp_ref[p]
                sv_ref[rows, BLOCK * p:BLOCK * (p + 1)] = sv.astype(BF16)
                u = cur_ref[rows, C_U + BLOCK * p:C_U + BLOCK * (p + 1)]
                g = cur_ref[rows, C_GG + BLOCK * p:C_GG + BLOCK * (p + 1)]
                y_ref[rows, D_ATTN + BLOCK * p:D_ATTN + BLOCK * (p + 1)] = ((u * sv) * (g * _sigmoid(g))).astype(BF16)

    cur, pkv, nkv, bias_specs = _mix_specs(nt)
    nb = nt * SUB
    half = pl.BlockSpec((TILE, D_ATTN), lambda i: (i, 0))
    return _pallas(
        body, name=name, grid=(nt,),
        in_specs=[pl.BlockSpec(memory_space=pltpu.SMEM), cur, pkv, nkv, *bias_specs, _full((1, BLOCK)), _full((1, BLOCK)),
                  _full((8, BLOCK, BLOCK)), _full((4, BLOCK, BLOCK))],
        out_specs=[pl.BlockSpec((TILE, D_MODEL), lambda i: (i, 0)), pl.BlockSpec((SUB * STACK, 3 * BLOCK), lambda i: (i, 0)),
                   pl.BlockSpec((SUB * N_HEADS, BLOCK), lambda i: (i, 0)), half, half],
        out_shape=[_sds((s, D_MODEL), BF16), _sds((nb * STACK, 3 * BLOCK), BF16), _sds((nb * N_HEADS, BLOCK)),
                   _sds((s, D_ATTN), BF16), _sds((s, D_ATTN), BF16)],
        operands=(sink, proj, proj, proj, *([bias] * SUB), qg2, kg2, ws, bsp), vmem_mib=56, jobs=jobs)


def _out_proj_fwd(y, x, vecs, w_out, vecs_next, wt, name):
    s = x.shape[0]
    ts = min(512, s)

    def body(y_ref, x_ref, v_ref, w_ref, vn_ref, wt_ref, xn_ref, p_ref):
        xv = x_ref[...] + _row(v_ref, V_GATE) * jnp.dot(y_ref[...], w_ref[...], preferred_element_type=F32)
        xn_ref[...] = xv
        p_ref[...] = lax.dot_general(_adaln(xv, vn_ref).astype(BF16), wt_ref[...], NT_DIMS, preferred_element_type=F32)

    row = pl.BlockSpec((ts, D_MODEL), lambda i: (i, 0))
    return _pallas(
        body, name=name, grid=(s // ts,),
        in_specs=[row, row, VECS, _full((D_MODEL, D_MODEL)), VECS, _full((D_IN, D_MODEL))],
        out_specs=[row, pl.BlockSpec((ts, D_IN), lambda i: (i, 0))], out_shape=[_sds((s, D_MODEL)), _sds((s, D_IN))],
        operands=(y, x, vecs, w_out, vecs_next, wt), vmem_mib=56)


def _out_loss_bwd(y, x, vecs, w_out, target, name):
    s = x.shape[0]
    ts = min(512, s)
    steps = s // ts

    def body(y_ref, x_ref, v_ref, w_ref, t_ref, dx_ref, sq_ref, dy_ref, gw_ref, dg_ref):
        @pl.when(pl.program_id(0) == 0)
        def _():
            sq_ref[...] = jnp.zeros_like(sq_ref)
            gw_ref[...] = jnp.zeros_like(gw_ref)

        gate = _row(v_ref, V_GATE)
        yv = y_ref[...]
        out = x_ref[...] + gate * jnp.dot(yv, w_ref[...], preferred_element_type=F32)
        diff = out - t_ref[...]
        dx = diff * (1.0 / D_MODEL)
        dx_ref[...] = dx
        per_token = jnp.sum(diff * diff, axis=-1, keepdims=True) * (1.0 / D_MODEL)
        sq_ref[...] += jnp.sum(per_token, axis=0, keepdims=True)
        dy_ref[...] = lax.dot_general((dx * gate).astype(BF16), w_ref[...], NT_DIMS, preferred_element_type=F32)
        gw_ref[...] += lax.dot_general(yv, dx.astype(BF16), TN_DIMS, preferred_element_type=F32)

        @pl.when(pl.program_id(0) == steps - 1)
        def _():
            m = gw_ref[...]
            dg_ref[...] = jnp.sum(w_ref[...].astype(F32) * m, axis=0, keepdims=True)
            gw_ref[...] = m * _row(v_ref, V_GATE)

    row = pl.BlockSpec((ts, D_MODEL), lambda i: (i, 0))
    return _pallas(
        body, name=name, grid=(s // ts,), in_specs=[row, row, VECS, _full((D_MODEL, D_MODEL)), row],
        out_specs=[row, _full((1, 1)), row, _full((D_MODEL, D_MODEL)), _full((1, D_MODEL))],
        out_shape=[_sds((s, D_MODEL)), _sds((1, 1)), _sds((s, D_MODEL)), _sds((D_MODEL, D_MODEL)), _sds((1, D_MODEL))],
        operands=(y, x, vecs, w_out, target), vmem_mib=48)


def _mix_bwd(dy, proj, probs, psink, attn, sv, qg2, kg2, wst, name, jobs=()):
    s = proj.shape[0]
    nt = s // TILE

    def body(dy_ref, cur_ref, pkv_ref, nkv_ref, p_ref, ps_ref, attn_ref, sv_ref, qg_ref, kg_ref, wst_ref,
             dpb_ref, dkv_ref, p0_ref, p2_ref, dqg_ref, dkg_ref, dsink_ref, dws_ref, dbsp_ref):
        def put(rows, col, value):
            dpb_ref[rows, col:col + BLOCK] = value.astype(BF16)

        @pl.when(pl.program_id(0) == 0)
        def _():
            dqg_ref[...] = jnp.zeros_like(dqg_ref)
            dkg_ref[...] = jnp.zeros_like(dkg_ref)
            dsink_ref[...] = jnp.zeros_like(dsink_ref)
            dws_ref[...] = jnp.zeros_like(dws_ref)
            dbsp_ref[...] = jnp.zeros_like(dbsp_ref)

        lo = _lo_mask((BLOCK, BLOCK))
        lo_kv = _lo_mask((TILE + 2 * BLOCK, BLOCK))
        eye = _diagonal()
        lane_row = lax.broadcasted_iota(jnp.int32, (1, BLOCK), 1)
        qg = qg_ref[...]
        kg = kg_ref[...]

        k_all, v_all = _kv_rows(cur_ref, pkv_ref, nkv_ref)
        rk = _half_rms_scale(k_all, lo_kv)
        khat = k_all * rk
        kn_all = (khat * kg).astype(BF16)
        vb_all = v_all.astype(BF16)

        dkn_parts, dv_parts = [], []
        dsink = jnp.zeros((1, BLOCK), F32)
        dqg = jnp.zeros((1, BLOCK), F32)
        for b in range(SUB):
            rows = slice(BLOCK * b, BLOCK * (b + 1))
            window = slice(BLOCK * b, BLOCK * (b + 3))
            kn, vb = kn_all[window], vb_all[window]

            qhat, rq = [], []
            for p in range(4):
                q = cur_ref[rows, C_Q + BLOCK * p:C_Q + BLOCK * (p + 1)]
                r = _half_rms_scale(q, lo)
                rq.append(r)
                qhat.append(q * r)
            q_stack = _stack_heads([(qh * qg) * 0.125 for qh in qhat], lo).astype(BF16)
            pb = p_ref[STACK * b:STACK * (b + 1), :]
            prob = pb.astype(F32)

            dout = []
            for p in range(4):
                g = cur_ref[rows, C_GA + BLOCK * p:C_GA + BLOCK * (p + 1)]
                sg = _sigmoid(g)
                dya = dy_ref[rows, BLOCK * p:BLOCK * (p + 1)]
                attn = attn_ref[rows, BLOCK * p:BLOCK * (p + 1)]
                put(rows, C_GA + BLOCK * p, dya * attn * (sg * (1.0 + g * (1.0 - sg))))
                dout.append(dya * (g * sg))
            do_stack = _stack_heads(dout, lo).astype(BF16)
            dp = lax.dot_general(do_stack, vb, NT_DIMS, preferred_element_type=F32)
            delta = jnp.sum(prob * dp, axis=-1, keepdims=True)
            dsb = (prob * (dp - delta)).astype(BF16)

            for h in range(N_HEADS):
                delta_row = _column_as_row(jnp.broadcast_to(delta[BLOCK * h:BLOCK * (h + 1)], (BLOCK, BLOCK)), eye)
                tot = jnp.sum(ps_ref[N_HEADS * b + h:N_HEADS * b + h + 1, :] * delta_row, axis=-1, keepdims=True)
                dsink = dsink - jnp.where(lane_row == h, tot, 0.0)

            dq_stack = jnp.dot(dsb, kn, preferred_element_type=F32) * 0.125
            dkn_parts.append(lax.dot_general(dsb, q_stack, TN_DIMS, preferred_element_type=F32))
            dv_parts.append(lax.dot_general(pb, do_stack, TN_DIMS, preferred_element_type=F32))

            for p in range(4):
                dqn = _unstack_pair(dq_stack, p, lo)
                qh = qhat[p]
                dqg = dqg + jnp.sum(dqn * qh, axis=0, keepdims=True)
                dqh = dqn * qg
                mean = _half_sum(dqh * qh, lo) * (1.0 / HEAD_DIM)
                put(rows, C_Q + BLOCK * p, rq[p] * (dqh - qh * mean))

        dws_new, dbs_new = [], []
        for p in range(4):
            rs, vnfs, vns, dsvs, dbs = [], [], [], [], None
            for b in range(SUB):
                rows = slice(BLOCK * b, BLOCK * (b + 1))
                vg = cur_ref[rows, C_VG + BLOCK * p:C_VG + BLOCK * (p + 1)]
                r = _half_rms_scale(vg, lo)
                vnf = vg * r
                sv = sv_ref[rows, BLOCK * p:BLOCK * (p + 1)]
                u = cur_ref[rows, C_U + BLOCK * p:C_U + BLOCK * (p + 1)]
                g = cur_ref[rows, C_GG + BLOCK * p:C_GG + BLOCK * (p + 1)]
                sg = _sigmoid(g)
                dym = dy_ref[rows, D_ATTN + BLOCK * p:D_ATTN + BLOCK * (p + 1)]
                put(rows, C_GG + BLOCK * p, dym * (u * sv) * (sg * (1.0 + g * (1.0 - sg))))
                dgm = dym * (g * sg)
                put(rows, C_U + BLOCK * p, dgm * sv)
                dsv = dgm * u
                term = jnp.where(lo, jnp.sum(jnp.where(lo, dsv, 0.0), axis=-1, keepdims=True),
                                 jnp.sum(jnp.where(lo, 0.0, dsv), axis=-1, keepdims=True))
                dbs = term if dbs is None else dbs + term
                rs.append(r)
                vnfs.append(vnf)
                vns.append(vnf.astype(BF16))
                dsvs.append(dsv)
            dbs_new.append(dbs)
            vn = jnp.concatenate(vns, axis=1)
            dsv = jnp.concatenate(dsvs, axis=1)
            lo_t = (lax.broadcasted_iota(jnp.int32, dsv.shape, 1) & (BLOCK - 1)) < HEAD_DIM
            dws_new.append(lax.dot_general(jnp.where(lo_t, dsv, 0.0).astype(BF16), vn, NT_DIMS, preferred_element_type=F32))
            dws_new.append(lax.dot_general(jnp.where(lo_t, 0.0, dsv).astype(BF16), vn, NT_DIMS, preferred_element_type=F32))
            dsvb = dsv.astype(BF16)
            dvn_a = jnp.dot(wst_ref[2 * p], dsvb, preferred_element_type=F32)
            dvn_b = jnp.dot(wst_ref[2 * p + 1], dsvb, preferred_element_type=F32)
            for b in range(SUB):
                lanes = slice(BLOCK * b, BLOCK * (b + 1))
                dvn = jnp.where(lo, dvn_a[:, lanes], dvn_b[:, lanes])
                mean = _half_sum(dvn * vnfs[b], lo) * (1.0 / HEAD_DIM)
                put(slice(BLOCK * b, BLOCK * (b + 1)), C_VG + BLOCK * p, rs[b] * (dvn - vnfs[b] * mean))

        dsink_ref[...] += dsink
        dqg = jnp.broadcast_to(dqg, (8, BLOCK))
        dqg_ref[...] += dqg + pltpu.roll(dqg, HEAD_DIM, 1)

        dkn = _overlap_add(dkn_parts)
        dv = _overlap_add(dv_parts)
        dkg = jnp.broadcast_to(jnp.sum(dkn * khat, axis=0, keepdims=True), (8, BLOCK))
        dkg_ref[...] += dkg + pltpu.roll(dkg, HEAD_DIM, 1)
        dkh = dkn * kg
        dk = rk * (dkh - khat * (_half_sum(dkh * khat, lo_kv) * (1.0 / HEAD_DIM)))
        dpb_ref[:, C_K:C_GA] = jnp.zeros((TILE, 2 * D_KV), BF16)
        dkv_ref[:, 0:D_KV] = dk[BLOCK:BLOCK + TILE]
        dkv_ref[:, D_KV:2 * D_KV] = dv[BLOCK:BLOCK + TILE]
        p0_ref[:, 0:D_KV] = dk[0:BLOCK]
        p0_ref[:, D_KV:2 * D_KV] = dv[0:BLOCK]
        p2_ref[:, 0:D_KV] = dk[BLOCK + TILE:]
        p2_ref[:, D_KV:2 * D_KV] = dv[BLOCK + TILE:]
        for g, new in enumerate(dws_new):
            dws_ref[g] += new
        for p, new in enumerate(dbs_new):
            dbsp_ref[p] += new

    cur, pkv, nkv, _ = _mix_specs(nt)
    kv_blk = (BLOCK, 2 * D_KV)
    half = pl.BlockSpec((TILE, D_ATTN), lambda i: (i, 0))
    return _pallas(
        body, name=name, grid=(nt,),
        in_specs=[pl.BlockSpec((TILE, D_MODEL), lambda i: (i, 0)), cur, pkv, nkv,
                  pl.BlockSpec((SUB * STACK, 3 * BLOCK), lambda i: (i, 0)), pl.BlockSpec((SUB * N_HEADS, BLOCK), lambda i: (i, 0)),
                  half, half, _full((1, BLOCK)), _full((1, BLOCK)), _full((8, BLOCK, BLOCK))],
        out_specs=[cur, pl.BlockSpec((TILE, 2 * D_KV), lambda i: (i, 0)),
                   pl.BlockSpec(kv_blk, lambda i: ((i + nt - 1) % nt, 0)),
                   pl.BlockSpec(kv_blk, lambda i: ((i + 1) % nt, 0)),
                   _full((8, BLOCK)), _full((8, BLOCK)), _full((1, BLOCK)),
                   _full((8, BLOCK, BLOCK)), _full((4, BLOCK, BLOCK))],
        out_shape=[_sds((s, D_IN), BF16), _sds((s, 2 * D_KV)), _sds((nt * BLOCK, 2 * D_KV)), _sds((nt * BLOCK, 2 * D_KV)),
                   _sds((8, BLOCK)), _sds((8, BLOCK)), _sds((1, BLOCK)),
                   _sds((8, BLOCK, BLOCK)), _sds((4, BLOCK, BLOCK))],
        operands=(dy, proj, proj, proj, probs, psink, attn, sv, qg2, kg2, wst), vmem_mib=56, jobs=jobs)


def _w_in_grad(dpb, dkv, p0, p2, x, vecs, name, jobs=()):
    s = x.shape[0]
    ts = min(2 * TILE, s)
    tiles = ts // TILE

    def body(dpb_ref, dkv_ref, p0_ref, p2_ref, x_ref, v_ref, gw_ref, dkvb_ref):
        @pl.when(pl.program_id(0) == 0)
        def _():
            gw_ref[...] = jnp.zeros_like(gw_ref)

        h = _adaln(x_ref[...], v_ref).astype(BF16)
        for t in range(tiles):
            halo = slice(BLOCK * t, BLOCK * (t + 1))
            first = slice(TILE * t, TILE * t + BLOCK)
            last = slice(TILE * (t + 1) - BLOCK, TILE * (t + 1))
            dkvb_ref[first, :] = (dkv_ref[first, :] + p2_ref[halo, :]).astype(BF16)
            if SUB > 2:
                inner = slice(TILE * t + BLOCK, TILE * (t + 1) - BLOCK)
                dkvb_ref[inner, :] = dkv_ref[inner, :].astype(BF16)
            dkvb_ref[last, :] = (dkv_ref[last, :] + p0_ref[halo, :]).astype(BF16)
        gw_ref[...] += lax.dot_general(dpb_ref[...], h, TN_DIMS, preferred_element_type=F32)
        gw_ref[C_K:C_GA, :] += lax.dot_general(dkvb_ref[...], h, TN_DIMS, preferred_element_type=F32)

    kv = pl.BlockSpec((ts, 2 * D_KV), lambda i: (i, 0))
    halo = pl.BlockSpec((tiles * BLOCK, 2 * D_KV), lambda i: (i, 0))
    return _pallas(
        body, name=name, grid=(s // ts,),
        in_specs=[pl.BlockSpec((ts, D_IN), lambda i: (i, 0)), kv, halo, halo,
                  pl.BlockSpec((ts, D_MODEL), lambda i: (i, 0)), VECS],
        out_specs=[_full((D_IN, D_MODEL)), kv], out_shape=[_sds((D_IN, D_MODEL)), _sds((s, 2 * D_KV), BF16)],
        operands=(dpb, dkv, p0, p2, x, vecs), vmem_mib=56, jobs=jobs)


def _proj_bwd(dpb, dkvb, x, dxo, vecs, wt, name, below=None, jobs=()):
    s = x.shape[0]
    ts = min(512, s)
    steps = s // ts

    def body(dpb_ref, dkvb_ref, x_ref, dxo_ref, v_ref, w_ref, *rest):
        if below is None:
            dxi_ref, dsh_ref, dsc_ref, dng_ref = rest
        else:
            y_ref, vb_ref, wo_ref, dxi_ref, dsh_ref, dsc_ref, dng_ref, dy_ref, gw_ref, dg_ref = rest

        @pl.when(pl.program_id(0) == 0)
        def _():
            dsh_ref[...] = jnp.zeros_like(dsh_ref)
            dsc_ref[...] = jnp.zeros_like(dsc_ref)
            dng_ref[...] = jnp.zeros_like(dng_ref)
            if below is not None:
                gw_ref[...] = jnp.zeros_like(gw_ref)

        dh = (jnp.dot(dpb_ref[...], w_ref[...], preferred_element_type=F32)
              + jnp.dot(dkvb_ref[...], w_ref[C_K:C_GA, :], preferred_element_type=F32))

        xv = x_ref[...]
        r = lax.rsqrt(jnp.mean(xv * xv, axis=-1, keepdims=True) + EPS)
        xn = xv * r
        ngv = _row(v_ref, V_NG)
        sc1 = 1.0 + _row(v_ref, V_SCALE)
        dsh_ref[...] += jnp.sum(dh, axis=0, keepdims=True)
        dsc_ref[...] += jnp.sum(dh * (xn * ngv), axis=0, keepdims=True)
        dh1 = dh * sc1
        dng_ref[...] += jnp.sum(dh1 * xn, axis=0, keepdims=True)
        dxn = dh1 * ngv
        dx = r * (dxn - xn * jnp.mean(dxn * xn, axis=-1, keepdims=True)) + dxo_ref[...]
        dxi_ref[...] = dx

        if below is not None:
            dy_ref[...] = lax.dot_general((dx * _row(vb_ref, V_GATE)).astype(BF16), wo_ref[...], NT_DIMS,
                                          preferred_element_type=F32)
            gw_ref[...] += lax.dot_general(y_ref[...], dx.astype(BF16), TN_DIMS, preferred_element_type=F32)

            @pl.when(pl.program_id(0) == steps - 1)
            def _():
                m = gw_ref[...]
                dg_ref[...] = jnp.sum(wo_ref[...].astype(F32) * m, axis=0, keepdims=True)
                gw_ref[...] = m * _row(vb_ref, V_GATE)

    row = pl.BlockSpec((ts, D_MODEL), lambda i: (i, 0))
    vec = _full((1, D_MODEL))
    square = _full((D_MODEL, D_MODEL))
    in_specs = [pl.BlockSpec((ts, D_IN), lambda i: (i, 0)), pl.BlockSpec((ts, 2 * D_KV), lambda i: (i, 0)),
                row, row, VECS, _full((D_IN, D_MODEL))]
    out_specs = [row, vec, vec, vec]
    out_shape = [_sds((s, D_MODEL)), _sds((1, D_MODEL)), _sds((1, D_MODEL)), _sds((1, D_MODEL))]
    operands = (dpb, dkvb, x, dxo, vecs, wt)
    if below is not None:
        in_specs += [row, VECS, square]
        out_specs += [row, square, vec]
        out_shape += [_sds((s, D_MODEL)), _sds((D_MODEL, D_MODEL)), _sds((1, D_MODEL))]
        operands += tuple(below)
    return _pallas(
        body, name=name, grid=(steps,), in_specs=in_specs, out_specs=out_specs, out_shape=out_shape,
        operands=operands, vmem_mib=48 if below is None else 60, jobs=jobs)


def _pair_sum(g, r, c_idx, name, send_dtype=None):
    _, rows, cols = g.shape
    half = rows // 2

    def body(c_ref, g_ref, r_ref, o_ref, *narrow):
        total = g_ref[...] + r_ref[...]
        o_ref[...] = total
        for n_ref in narrow:
            n_ref[...] = total.astype(n_ref.dtype)

    blk = (None, half, cols)
    out_blk = pl.BlockSpec(blk, lambda j, c: (j, 0, 0))
    shapes = [_sds((N_CHIPS, half, cols))] + ([_sds((N_CHIPS, half, cols), send_dtype)] if send_dtype else [])
    return pl.pallas_call(
        body, name=name,
        grid_spec=pltpu.PrefetchScalarGridSpec(
            num_scalar_prefetch=1, grid=(N_CHIPS,),
            in_specs=[pl.BlockSpec(blk, lambda j, c: (j, c[0], 0)), out_blk], out_specs=[out_blk] * len(shapes)),
        out_shape=shapes,
        compiler_params=pltpu.CompilerParams(dimension_semantics=("arbitrary",), vmem_limit_bytes=32 * MIB),
    )(*_in_hbm(c_idx, g, r))


def _chip_sum(p, r, place, name):
    _, rows, cols = p.shape
    tr = rows // 2

    def body(j_ref, p_ref, r_ref, o_ref):
        o_ref[...] = ((p_ref[...] + r_ref[0].astype(F32)) + r_ref[1].astype(F32)) + r_ref[2].astype(F32)

    return pl.pallas_call(
        body, name=name,
        grid_spec=pltpu.PrefetchScalarGridSpec(
            num_scalar_prefetch=1, grid=(2,),
            in_specs=[pl.BlockSpec((None, tr, cols), lambda t, j: (j[0], t, 0)),
                      pl.BlockSpec((3, tr, cols), lambda t, j: (0, t, 0))],
            out_specs=pl.BlockSpec((tr, cols), lambda t, j: (2 * j[1] + t, 0))),
        out_shape=_sds((2 * rows, cols)),
        compiler_params=pltpu.CompilerParams(dimension_semantics=("arbitrary",), vmem_limit_bytes=32 * MIB),
    )(*_in_hbm(place, p, r))


def _cast_permute_w_in(wt, place_chunks):
    def body(t_ref, w_ref, w0_ref, w1_ref):
        def cast_into(o_ref):
            for t in range(N_CHUNKS):
                src = pl.multiple_of(t_ref[1 + t] * CHUNK_ROWS, CHUNK_ROWS)
                o_ref[CHUNK_ROWS * t:CHUNK_ROWS * (t + 1), :] = w_ref[pl.ds(src, CHUNK_ROWS), :].astype(BF16)

        @pl.when(pl.program_id(0) == 0)
        def _():
            cast_into(w0_ref)

        @pl.when(pl.program_id(0) == 1)
        def _():
            cast_into(w1_ref)

    return pl.pallas_call(
        body, name="cast_permute_w_in",
        grid_spec=pltpu.PrefetchScalarGridSpec(
            num_scalar_prefetch=1, grid=(DEPTH,),
            in_specs=[pl.BlockSpec((None, W_IN_BLK, D_MODEL), lambda l, tbl: (l, 0, 0))],
            out_specs=[pl.BlockSpec((None, W_IN_BLK, D_MODEL), lambda l, tbl: (tbl[0], 0, 0)),
                       pl.BlockSpec((W_IN_BLK, D_MODEL), lambda l, tbl: (0, 0))]),
        out_shape=[_sds((N_CHIPS, W_IN_BLK, D_MODEL), BF16), _sds((W_IN_BLK, D_MODEL), BF16)],
        compiler_params=pltpu.CompilerParams(dimension_semantics=("arbitrary",), vmem_limit_bytes=32 * MIB),
    )(*_in_hbm(place_chunks, wt))


def _gather_inputs(c, w_out, w0):
    half = W_IN_BLK // 2

    def body(c_ref, wout_ref, mine_ref, call_ref, woutb_ref, w0_ref, send_sems, recv_sems):
        x, y, cc = _coords()
        j = 2 * x + y
        b = 2 * j + cc
        sib = (x, y, 1 - cc)
        call_ref[b] = c_ref[...]
        chips = _other_chips(x, y)

        def sems(k):
            return send_sems.at[k], recv_sems.at[k]

        def half_rows(chip_index):
            return w0_ref.at[chip_index, pl.ds(cc * half, half), :]

        first = [_remote(mine_ref.at[j, pl.ds(cc * half, half), :], half_rows(j), sems(k), (*chip, cc))
                 for k, chip in enumerate(chips)]
        k = 3
        rest = []
        for fx in (0, 1):
            for fy in (0, 1):
                for fc in (0, 1):
                    if fx or fy or fc:
                        dev = (1 - x if fx else x, 1 - y if fy else y, 1 - cc if fc else cc)
                        rest.append(_remote(call_ref.at[b], call_ref.at[b], sems(k), dev))
                        k += 1
        for cp in first + rest:
            cp.start()
        woutb_ref[...] = wout_ref[...].astype(BF16)
        passed = []
        for k, chip in enumerate(chips):
            jk = 2 * chip[0] + chip[1]
            first[k].wait_recv()
            passed.append(_remote(half_rows(jk), half_rows(jk), sems(10 + k), sib))
            passed[k].start()
        for cp in first:
            cp.wait_send()
        for cp in rest + passed:
            cp.wait()

    return pl.pallas_call(
        body, name="gather_inputs", in_specs=[VMEM, VMEM, ANY], out_specs=[VMEM, VMEM, ANY],
        out_shape=[_sds((N_DEV, 1, D_MODEL)), _sds((DEPTH, W_OUT_BLK, D_MODEL), BF16),
                   _sds((N_CHIPS, W_IN_BLK, D_MODEL), BF16)],
        scratch_shapes=[pltpu.SemaphoreType.DMA((13,)), pltpu.SemaphoreType.DMA((13,))],
        input_output_aliases={2: 2},
        compiler_params=pltpu.CompilerParams(vmem_limit_bytes=32 * MIB),
    )(c, w_out, w0)


def _ada_rows(c_all, w_ada, b_blk):
    def body(c_ref, w_ref, b_ref, o_ref, cond_ref):
        cv = c_ref[...]
        cond = (cv * _sigmoid(cv)).astype(BF16)
        cond_ref[...] = cond.astype(F32)
        for l in range(DEPTH):
            o_ref[:, l, :] = jnp.dot(cond, w_ref[l].astype(BF16), preferred_element_type=F32) + b_ref[l:l + 1, :]

    return pl.pallas_call(
        body, name="ada_rows", in_specs=[VMEM, VMEM, VMEM], out_specs=[VMEM, VMEM],
        out_shape=[_sds((N_DEV, DEPTH, W_ADA_BLK)), _sds((N_DEV, D_MODEL))],
        compiler_params=pltpu.CompilerParams(vmem_limit_bytes=32 * MIB),
    )(c_all, w_ada, b_blk)


def _exchange_ada(part):
    def body(part_ref, out_ref, send_sems, recv_sems):
        x, y, cc = _coords()
        j = 2 * x + y
        out_ref[j] = part_ref[2 * j + cc]
        copies = []
        for k, chip in enumerate(_other_chips(x, y)):
            b_dst = 4 * chip[0] + 2 * chip[1] + cc
            copies.append(_remote(part_ref.at[b_dst], out_ref.at[j], (send_sems.at[k], recv_sems.at[k]), (*chip, cc)))
        for cp in copies:
            cp.start()
        for cp in copies:
            cp.wait()

    return pl.pallas_call(
        body, name="exchange_ada", in_specs=[VMEM], out_specs=VMEM,
        out_shape=_sds((N_CHIPS, DEPTH, W_ADA_BLK)),
        scratch_shapes=[pltpu.SemaphoreType.DMA((3,)), pltpu.SemaphoreType.DMA((3,))],
    )(part)


def _adamw_math(w, g, m, v):
    m = ADAM_B1 * m + (1.0 - ADAM_B1) * g
    v = ADAM_B2 * v + (1.0 - ADAM_B2) * (g * g)
    m_hat = m / (1.0 - ADAM_B1 ** ADAM_STEP)
    v_hat = v / (1.0 - ADAM_B2 ** ADAM_STEP)
    delta = -ADAM_LR * (m_hat / (jnp.sqrt(v_hat) + ADAM_EPS) + ADAM_WD * w)
    return delta, m, v


def _adamw_w_in(w, g0, g1, m, v, pos_chunks):
    def body(t_ref, w_ref, g0_ref, g1_ref, m_ref, v_ref, g_ref, d_ref, nm_ref, nv_ref):
        for l, src in enumerate((g0_ref, g1_ref)):
            g = src[...]
            g_ref[l] = g
            d_ref[l], nm_ref[l], nv_ref[l] = _adamw_math(w_ref[l], g, m_ref[l], v_ref[l])

    nat = pl.BlockSpec((DEPTH, CHUNK_ROWS, D_MODEL), lambda t, tbl: (0, t, 0))
    per = pl.BlockSpec((CHUNK_ROWS, D_MODEL), lambda t, tbl: (tbl[t], 0))
    return pl.pallas_call(
        body, name="adamw_w_in",
        grid_spec=pltpu.PrefetchScalarGridSpec(num_scalar_prefetch=1, grid=(N_CHUNKS,),
                                               in_specs=[nat, per, per, nat, nat], out_specs=[nat] * 4),
        out_shape=[_sds(w.shape)] * 4,
        compiler_params=pltpu.CompilerParams(dimension_semantics=("arbitrary",)),
    )(*_in_hbm(pos_chunks, w, g0, g1, m, v))


def _adamw_w_out(w, g0, g1, m, v):
    def body(w_ref, g0_ref, g1_ref, m_ref, v_ref, g_ref, d_ref, nm_ref, nv_ref):
        g = jnp.where(pl.program_id(0) == 0, g0_ref[...], g1_ref[...])
        g_ref[...] = g
        d_ref[...], nm_ref[...], nv_ref[...] = _adamw_math(w_ref[...], g, m_ref[...], v_ref[...])

    blk = pl.BlockSpec((None, W_OUT_BLK, D_MODEL), lambda l: (l, 0, 0))
    gblk = _full((W_OUT_BLK, D_MODEL))
    return pl.pallas_call(
        body, name="adamw_w_out", grid=(DEPTH,), in_specs=[blk, gblk, gblk, blk, blk], out_specs=[blk] * 4,
        out_shape=[_sds(w.shape)] * 4,
        compiler_params=pltpu.CompilerParams(dimension_semantics=("arbitrary",), vmem_limit_bytes=32 * MIB),
    )(w, g0, g1, m, v)


def _w_ada_grad_adamw(cond_t, dada, w, m, v):
    _, rows, cols = w.shape
    tr = 256

    def body(ct_ref, da_ref, w_ref, m_ref, v_ref, g_ref, d_ref, nm_ref, nv_ref):
        g = jnp.dot(ct_ref[...], da_ref[...].astype(BF16), preferred_element_type=F32)
        g_ref[...] = g
        d_ref[...], nm_ref[...], nv_ref[...] = _adamw_math(w_ref[...], g, m_ref[...], v_ref[...])

    blk = pl.BlockSpec((None, tr, cols), lambda l, t: (l, t, 0))
    return pl.pallas_call(
        body, name="w_ada_grad_adamw", grid=(DEPTH, rows // tr),
        in_specs=[pl.BlockSpec((tr, BLOCK), lambda l, t: (t, 0)), pl.BlockSpec((None, BLOCK, cols), lambda l, t: (l, 0, 0)),
                  blk, blk, blk],
        out_specs=[blk] * 4, out_shape=[_sds(w.shape)] * 4,
        compiler_params=pltpu.CompilerParams(dimension_semantics=("arbitrary", "arbitrary"), vmem_limit_bytes=32 * MIB),
    )(cond_t, dada, w, m, v)


def _small_sum_adamw(gathered_a, gathered_b, ws, rest):
    n_ws = 8 * BLOCK

    def body(*refs):
        a_refs, b_ref = refs[:DEPTH], refs[DEPTH]
        ws_ref, ms_ref, vs_ref, wr_ref, mr_ref, vr_ref = refs[DEPTH + 1:DEPTH + 7]
        gs_ref, ds_ref, nms_ref, nvs_ref, gr_ref, dr_ref, nmr_ref, nvr_ref = refs[DEPTH + 7:]

        def total(ref):
            g = ref[0]
            for b in range(1, N_DEV):
                g = g + ref[b]
            return g

        totals = [total(ref) for ref in a_refs]
        for l, t in enumerate(totals):
            g = t[0:n_ws]
            gs_ref[l] = g
            ds_ref[l], nms_ref[l], nvs_ref[l] = _adamw_math(ws_ref[l], g, ms_ref[l], vs_ref[l])
        g = jnp.concatenate([t[n_ws:] for t in totals] + [total(b_ref)], axis=0)
        gr_ref[...] = g
        dr_ref[...], nmr_ref[...], nvr_ref[...] = _adamw_math(wr_ref[...], g, mr_ref[...], vr_ref[...])

    return pl.pallas_call(
        body, name="small_sum_adamw", in_specs=[VMEM] * (DEPTH + 7), out_specs=[VMEM] * 8,
        out_shape=[_sds(ws[0].shape)] * 4 + [_sds(rest[0].shape)] * 4,
        compiler_params=pltpu.CompilerParams(vmem_limit_bytes=48 * MIB),
    )(*gathered_a, gathered_b, *ws, *rest)


_SMALL_A_REST = (("b_s", 8), ("q_gain", 1), ("k_gain", 1), ("sink", 1))
_SMALL_A = (("w_s", 8 * BLOCK),) + _SMALL_A_REST
_SMALL_B = (("b_ada", DEPTH * 24), ("norm_gain", DEPTH * 8), ("sq_err", 1))


def _pack_rows(parts, layout, layer=None):
    rows = []
    for name, n in layout:
        flat = (parts[name] if layer is None else parts[name][layer]).reshape(-1)
        rows.append(jnp.pad(flat, (0, n * 128 - flat.shape[0])).reshape(n, 128))
    n_rows = sum(n for _, n in layout)
    if n_rows % 8:
        rows.append(jnp.zeros((-n_rows % 8, 128), F32))
    return jnp.concatenate(rows, axis=0)


def _pack_rest(parts):
    return jnp.concatenate([_pack_rows(parts, _SMALL_A_REST, l) for l in range(DEPTH)] + [_pack_rows(parts, _SMALL_B)], axis=0)


def _unpack_rest(packed, shapes):
    def take(r0, layout, shape_of):
        got = {}
        for name, n in layout:
            shape = shape_of(name)
            size = 1
            for d in shape:
                size *= d
            got[name] = packed[r0:r0 + n].reshape(-1)[:size].reshape(shape)
            r0 += n
        return got, r0 + -r0 % 8

    layers, r0 = [], 0
    for _ in range(DEPTH):
        got, r0 = take(r0, _SMALL_A_REST, lambda name: shapes[name][1:])
        layers.append(got)
    out, _ = take(r0, _SMALL_B, lambda name: shapes[name])
    out.update({name: jnp.stack([layer[name] for layer in layers]) for name, _ in _SMALL_A_REST})
    return out


def _permute_heads(a, axis):
    shp = a.shape
    a = a.reshape(shp[:axis] + (2, 4, HEAD_DIM) + shp[axis + 1:])
    a = jnp.swapaxes(a, axis, axis + 1)
    return a.reshape(shp)


def _unpermute_heads(a, axis):
    shp = a.shape
    a = a.reshape(shp[:axis] + (4, 2, HEAD_DIM) + shp[axis + 1:])
    a = jnp.swapaxes(a, axis, axis + 1)
    return a.reshape(shp)


def _permute_w_out(w):
    return jnp.concatenate([_permute_heads(w[:D_ATTN], 0), w[D_ATTN:]], axis=0)


def _unpermute_w_out(w):
    return jnp.concatenate([_unpermute_heads(w[:D_ATTN], 0), w[D_ATTN:]], axis=0)


def kernel(x, c, w_ada, b_ada, norm_gain, w_in, q_gain, k_gain, sink, w_s, b_s, w_out, loss_target, m_w_ada, m_b_ada, m_norm_gain, m_w_in, m_q_gain, m_k_gain, m_sink, m_w_s, m_b_s, m_w_out, v_w_ada, v_b_ada, v_norm_gain, v_w_in, v_q_gain, v_k_gain, v_sink, v_w_s, v_b_s, v_w_out):
    ix, iy, ic = _coords()
    chip = 2 * ix + iy
    chip_idx = jnp.stack([chip, ic]).astype(jnp.int32)
    core_idx = jnp.reshape(ic, (1,)).astype(jnp.int32)
    src_chunks = lax.dynamic_index_in_dim(jnp.asarray(_CHUNK_SRC), chip, 0, keepdims=False)
    pos_chunks = lax.dynamic_index_in_dim(jnp.asarray(_CHUNK_POS), chip, 0, keepdims=False)
    x0, target = x[0], loss_target[0]

    wt, mt, vt = (jnp.swapaxes(a, 1, 2) for a in (w_in, m_w_in, v_w_in))
    w0_mine, wloc_in1 = _cast_permute_w_in(wt, jnp.concatenate([chip_idx[:1], src_chunks]))
    c_all, wloc_out, w0 = _gather_inputs(c, w_out, w0_mine)
    wts = [w0.reshape(D_IN, D_MODEL), None]

    b_blk = lax.dynamic_slice_in_dim(b_ada, chip * W_ADA_BLK, W_ADA_BLK, axis=1)
    ada_part, cond = _ada_rows(c_all.reshape(N_DEV, D_MODEL), w_ada, b_blk)
    ada = jnp.moveaxis(_exchange_ada(ada_part), 0, 1).reshape(DEPTH, 3 * D_MODEL)
    vecs = [jnp.concatenate([norm_gain[l:l + 1], ada[l:l + 1, D_MODEL:2 * D_MODEL], ada[l:l + 1, 0:D_MODEL],
                             ada[l:l + 1, 2 * D_MODEL:], jnp.zeros((4, D_MODEL), F32)], axis=0) for l in range(DEPTH)]

    qg2 = jnp.concatenate([q_gain, q_gain], axis=-1)
    kg2 = jnp.concatenate([k_gain, k_gain], axis=-1)
    ws_b = w_s.astype(BF16)
    wst_b = jnp.swapaxes(w_s, -1, -2).astype(BF16)
    bsp = jnp.repeat(jnp.swapaxes(b_s.reshape(DEPTH, 4, 2, BLOCK), -1, -2), HEAD_DIM, axis=-1)
    bias = jnp.asarray(_bias_table())

    def mix_args(l):
        return bias, sink[l], qg2[l:l + 1], kg2[l:l + 1], ws_b[l]

    w_out_shape = (W_OUT_BLK, D_MODEL)
    proj0, wo0, wo1 = _proj_fwd(x0, vecs[0], wts[0], "proj_fwd_0",
                                jobs=[_job_gather([(wloc_out, 0), (wloc_out, 1)], [w_out_shape, w_out_shape])])
    y0, *kept0, w1 = _mix_fwd(proj0, *mix_args(0), bsp[0], "mix_fwd_0",
                              jobs=[_job_gather([(wloc_in1, None)], [(W_IN_BLK, D_MODEL)])])
    wts[1] = w1.reshape(D_IN, D_MODEL)
    wos = [_permute_w_out(w.reshape(D_MODEL, D_MODEL)) for w in (wo0, wo1)]
    x1, proj1 = _out_proj_fwd(y0, x0, vecs[0], wos[0], vecs[1], wts[1], "out_proj_fwd_01")
    y1, *kept1 = _mix_fwd(proj1, *mix_args(1), bsp[1], "mix_fwd_1")

    def blocks_out(gw):
        return _unpermute_w_out(gw).reshape(N_CHIPS, W_OUT_BLK, D_MODEL)

    def small_pack(dws, dbsp, dqg, dkg, dsink):
        b_s = jnp.swapaxes(dbsp[:, :, ::HEAD_DIM], -1, -2).reshape(8, BLOCK)
        return _pack_rows(dict(w_s=dws, b_s=b_s, q_gain=dqg[0, :HEAD_DIM], k_gain=dkg[0, :HEAD_DIM], sink=dsink[0, :N_HEADS]),
                          _SMALL_A)

    dx2, sq, dy1, gwo1, dgate1 = _out_loss_bwd(y1, x1, vecs[1], wos[1], target, "out_loss_bwd_1")
    go1 = blocks_out(gwo1)
    dpb, dkv, p0, p2, dqg1, dkg1, dsink1, dws1, dbsp1, ro1 = _mix_bwd(
        dy1, proj1, *kept1, qg2[1:2], kg2[1:2], wst_b[1], "mix_bwd_1", jobs=[_job_swap(go1)])
    po1, = _pair_sum(go1, ro1, core_idx, "pair_sum_w_out_1")
    gwi1, dkvb, co1, gathered_a1 = _w_in_grad(
        dpb, dkv, p0, p2, x1, vecs[1], "w_in_grad_1",
        jobs=[_job_scatter(po1), _job_all_gather(small_pack(dws1, dbsp1, dqg1, dkg1, dsink1))])
    gi1 = gwi1.reshape(N_CHIPS, W_IN_BLK, D_MODEL)
    fo1 = _chip_sum(po1, co1, chip_idx, "chip_sum_w_out_1")
    dx1, dsh1, dsc1, dng1, dy0, gwo0, dgate0, grad_wo1, ri1 = _proj_bwd(
        dpb, dkvb, x1, dx2, vecs[1], wts[1], "proj_out_bwd_10", below=(y0, vecs[0], wos[0]),
        jobs=[_job_join(fo1), _job_swap(gi1)])
    pi1, = _pair_sum(gi1, ri1, core_idx, "pair_sum_w_in_1")
    go0 = blocks_out(gwo0)
    dpb, dkv, p0, p2, dqg0, dkg0, dsink0, dws0, dbsp0, ci1, ro0 = _mix_bwd(
        dy0, proj0, *kept0, qg2[0:1], kg2[0:1], wst_b[0], "mix_bwd_0", jobs=[_job_scatter(pi1), _job_swap(go0)])
    fi1 = _chip_sum(pi1, ci1, chip_idx, "chip_sum_w_in_1")
    po0, = _pair_sum(go0, ro0, core_idx, "pair_sum_w_out_0")

    gwi0, dkvb, grad_wi1, co0, gathered_a0 = _w_in_grad(
        dpb, dkv, p0, p2, x0, vecs[0], "w_in_grad_0",
        jobs=[_job_join(fi1), _job_scatter(po0), _job_all_gather(small_pack(dws0, dbsp0, dqg0, dkg0, dsink0))])
    gi0 = gwi0.reshape(N_CHIPS, W_IN_BLK, D_MODEL)
    fo0 = _chip_sum(po0, co0, chip_idx, "chip_sum_w_out_0")

    ri0, grad_wo0 = _comm([_job_swap(gi0), _job_join(fo0)], "swap_w_in_0")
    pi0, pi0_send = _pair_sum(gi0, ri0, core_idx, "pair_sum_w_in_0", send_dtype=BF16)
    dx0, dsh0, dsc0, dng0, ci0 = _proj_bwd(dpb, dkvb, x0, dx1, vecs[0], wts[0], "proj_bwd_0",
                                           jobs=[_job_scatter(pi0_send)])
    fi0 = _chip_sum(pi0, ci0, chip_idx, "chip_sum_w_in_0")

    small_g = dict(
        b_ada=jnp.stack([jnp.concatenate([dsh0, dsc0, dgate0], axis=-1)[0], jnp.concatenate([dsh1, dsc1, dgate1], axis=-1)[0]]),
        norm_gain=jnp.stack([dng0[0], dng1[0]]), sq_err=sq[0])
    none = jnp.zeros((1,), F32)
    small_w = dict(w_s=w_s, b_s=b_s, b_ada=b_ada, norm_gain=norm_gain, q_gain=q_gain, k_gain=k_gain, sink=sink, sq_err=none)
    small_m = dict(w_s=m_w_s, b_s=m_b_s, b_ada=m_b_ada, norm_gain=m_norm_gain, q_gain=m_q_gain, k_gain=m_k_gain, sink=m_sink,
                   sq_err=none)
    small_v = dict(w_s=v_w_s, b_s=v_b_s, b_ada=v_b_ada, norm_gain=v_norm_gain, q_gain=v_q_gain, k_gain=v_k_gain, sink=v_sink,
                   sq_err=none)
    grad_wi0, gathered_b = _comm([_job_join(fi0), _job_all_gather(_pack_rows(small_g, _SMALL_B))], "join_w_in_0")
    packed = _small_sum_adamw(
        [a.reshape(N_DEV, -1, 128) for a in (gathered_a0, gathered_a1)], gathered_b.reshape(N_DEV, -1, 128),
        [a.reshape(DEPTH, 8 * BLOCK, BLOCK) for a in (w_s, m_w_s, v_w_s)], [_pack_rest(p) for p in (small_w, small_m, small_v)])
    shapes = {k: a.shape for k, a in small_w.items()}
    sg, sd, sm, sv = (dict(_unpack_rest(rest, shapes), w_s=ws.reshape(w_s.shape)) for ws, rest in zip(packed[:4], packed[4:]))
    loss = 0.5 * sg["sq_err"][0]

    dada_all = gathered_b.reshape(N_DEV, -1, 128)[:, 0:DEPTH * 24].reshape(N_DEV, DEPTH, 3 * D_MODEL)
    dada_blk = jnp.moveaxis(lax.dynamic_slice_in_dim(dada_all, chip * W_ADA_BLK, W_ADA_BLK, axis=2), 0, 1)
    pad = BLOCK - N_DEV
    ada_out = _w_ada_grad_adamw(
        jnp.pad(cond.T, ((0, 0), (0, pad))).astype(BF16), jnp.pad(dada_blk, ((0, 0), (0, pad), (0, 0))),
        w_ada, m_w_ada, v_w_ada)

    in_out = [jnp.swapaxes(a, 1, 2) for a in _adamw_w_in(wt, grad_wi0, grad_wi1, mt, vt, pos_chunks)]
    out_out = _adamw_w_out(w_out, grad_wo0, grad_wo1, m_w_out, v_w_out)

    def ordered(k):
        small = (sg, sd, sm, sv)[k]
        return (ada_out[k], small["b_ada"], small["norm_gain"], in_out[k], small["q_gain"], small["k_gain"], small["sink"],
                small["w_s"], small["b_s"], out_out[k])

    return (loss, dx0[None], *ordered(0), *ordered(1), *ordered(2), *ordered(3))
```
